```python
import math
import jax
import jax.numpy as jnp
from jax import lax
import numpy as np

D_MODEL = 2048
BATCH = 2
SEQ = 4096
DEPTH = 1

MEM_LEN = 256
SSD_D_INNER = D_MODEL
SSD_HEAD_DIM = 64
SSD_HEADS = SSD_D_INNER // SSD_HEAD_DIM
SSD_GROUPS = 4
SSD_STATE = 128
SSD_CONV = 4
SSD_CHUNK = 128
SSD_CONV_DIM = SSD_D_INNER + 2 * SSD_GROUPS * SSD_STATE
SWA_HEADS = 16
SWA_KV_HEADS = 4
SWA_HEAD_DIM = 64
SWA_WINDOW = 128
SWA_BLOCK = SWA_WINDOW
REL_BUCKETS = 32
REL_MAX_DIST = 128
XA_HEADS = 4
XA_HEAD_DIM = D_MODEL // 8
N_BRANCH = 3
N_EXPERTS = 64
TOP_K = 8
N_EXPERT_GROUPS = 8
TOPK_GROUPS = 4
EXPERT_DIM = D_MODEL // 4
SHARED_DIM = D_MODEL // 4
ROUTED_SCALE = 2.5
MOE_BLOCK = 128
ALPHA = (2.0 * DEPTH) ** 0.25
BETA = (8.0 * DEPTH) ** -0.25
LN_EPS = 1e-5
RMS_EPS = 1e-5

SWA_Q_DIM = SWA_HEADS * SWA_HEAD_DIM
SWA_KV_DIM = SWA_KV_HEADS * SWA_HEAD_DIM
XA_DIM = XA_HEADS * XA_HEAD_DIM
IN_SIZES = (SSD_D_INNER, SSD_CONV_DIM, SSD_HEADS, SWA_Q_DIM, SWA_KV_DIM, SWA_KV_DIM, XA_DIM, N_BRANCH * D_MODEL)
IN_TOTAL = sum(IN_SIZES)

kernel_name = "hybrid_ssd_swa_memxattn_moe_deepnorm"


def _split_points(sizes):
    pts, acc = [], 0
    for s in sizes[:-1]:
        acc += s
        pts.append(acc)
    return pts


def layer_norm(x, g, b):
    xf = x.astype(jnp.float32)
    mu = jnp.mean(xf, axis=-1, keepdims=True)
    var = jnp.mean(jnp.square(xf - mu), axis=-1, keepdims=True)
    return ((xf - mu) * lax.rsqrt(var + LN_EPS)).astype(x.dtype) * g + b


def causal_dwconv(u, w, b):
    c = u.shape[-1]
    out = lax.conv_general_dilated(u, w[:, None, :], window_strides=(1,), padding=[(SSD_CONV - 1, 0)],
                                   dimension_numbers=('NWC', 'WIO', 'NWC'), feature_group_count=c)
    return out + b


def ssd_chunked_scan(xs, dt, a, bm, cm):
    b, l, nh, p = xs.shape
    g, n = bm.shape[2], bm.shape[3]
    e = nh // g
    q = SSD_CHUNK
    nc = l // q
    dtype = xs.dtype
    xd = (xs * dt.astype(dtype)[..., None]).reshape(b, nc, q, g, e, p)
    la_cum = jnp.cumsum((dt * a).reshape(b, nc, q, g, e), axis=2)
    bc = bm.reshape(b, nc, q, g, n)
    cc = cm.reshape(b, nc, q, g, n)
    seg = la_cum[:, :, :, None] - la_cum[:, :, None, :]
    causal = jnp.tril(jnp.ones((q, q), dtype=bool))[:, :, None, None]
    decay = jnp.exp(jnp.where(causal, seg, -jnp.inf)).astype(dtype)
    cb = jnp.einsum('bcign,bcjgn->bcijg', cc, bc)
    y_diag = jnp.einsum('bcijg,bcijge,bcjgep->bcigep', cb, decay, xd)
    decay_end = jnp.exp(la_cum[:, :, -1:] - la_cum).astype(dtype)
    states = jnp.einsum('bcjgn,bcjge,bcjgep->bcgepn', bc, decay_end, xd).astype(jnp.float32)
    chunk_decay = jnp.exp(la_cum[:, :, -1])

    def step(h, inp):
        s_c, d_c = inp
        return h * d_c[..., None, None] + s_c, h

    h0 = jnp.zeros((b, g, e, p, n), jnp.float32)
    _, h_prev = lax.scan(step, h0, (jnp.moveaxis(states, 1, 0), jnp.moveaxis(chunk_decay, 1, 0)))
    h_prev = jnp.moveaxis(h_prev, 0, 1).astype(dtype)
    y_off = jnp.einsum('bcign,bcgepn,bcige->bcigep', cc, h_prev, jnp.exp(la_cum).astype(dtype))
    return (y_diag + y_off).reshape(b, l, nh, p)


def ssd_mixer(z, xbc, dt_raw, conv_w, conv_b, dt_bias, a_log, d_skip, norm_g):
    b, l, _ = z.shape
    xbc = jax.nn.silu(causal_dwconv(xbc, conv_w, conv_b))
    xs, bm, cm = jnp.split(xbc, [SSD_D_INNER, SSD_D_INNER + SSD_GROUPS * SSD_STATE], axis=-1)
    xs = xs.reshape(b, l, SSD_HEADS, SSD_HEAD_DIM)
    bm = bm.reshape(b, l, SSD_GROUPS, SSD_STATE)
    cm = cm.reshape(b, l, SSD_GROUPS, SSD_STATE)
    dt = jax.nn.softplus((dt_raw + dt_bias).astype(jnp.float32))
    a = -jnp.exp(a_log.astype(jnp.float32))
    y = ssd_chunked_scan(xs, dt, a, bm, cm) + xs * d_skip[:, None]
    y = y.reshape(b, l, SSD_D_INNER) * jax.nn.silu(z)
    yg = y.reshape(b, l, SSD_GROUPS, -1).astype(jnp.float32)
    yg = yg * lax.rsqrt(jnp.mean(jnp.square(yg), axis=-1, keepdims=True) + RMS_EPS)
    return yg.reshape(b, l, SSD_D_INNER).astype(z.dtype) * norm_g


def t5_causal_bucket(dist):
    max_exact = REL_BUCKETS // 2
    large = max_exact + (jnp.log(jnp.maximum(dist, 1).astype(jnp.float32) / max_exact)
                         / math.log(REL_MAX_DIST / max_exact) * (REL_BUCKETS - max_exact)).astype(jnp.int32)
    large = jnp.minimum(large, REL_BUCKETS - 1)
    return jnp.where(dist < max_exact, dist, large)


def swa_mixer(q, k, v, sinks, rel_bias):
    b, l, _ = q.shape
    nb = l // SWA_BLOCK
    grp = SWA_HEADS // SWA_KV_HEADS
    q = q.reshape(b, nb, SWA_BLOCK, SWA_KV_HEADS, grp, SWA_HEAD_DIM)
    pad = jnp.zeros((b, SWA_BLOCK, SWA_KV_HEADS, SWA_HEAD_DIM), k.dtype)
    kp = jnp.concatenate([pad, k.reshape(b, l, SWA_KV_HEADS, SWA_HEAD_DIM)], axis=1)
    vp = jnp.concatenate([pad, v.reshape(b, l, SWA_KV_HEADS, SWA_HEAD_DIM)], axis=1)
    kp = kp.reshape(b, nb + 1, SWA_BLOCK, SWA_KV_HEADS, SWA_HEAD_DIM)
    vp = vp.reshape(b, nb + 1, SWA_BLOCK, SWA_KV_HEADS, SWA_HEAD_DIM)
    kband = jnp.concatenate([kp[:, :-1], kp[:, 1:]], axis=2)
    vband = jnp.concatenate([vp[:, :-1], vp[:, 1:]], axis=2)
    scores = jnp.einsum('bnihgd,bnjhd->bnhgij', q, kband).astype(jnp.float32) * (SWA_HEAD_DIM ** -0.5)
    qi = jnp.arange(SWA_BLOCK)[:, None]
    kj = jnp.arange(2 * SWA_BLOCK)[None, :]
    dist = qi + SWA_BLOCK - kj
    in_window = (dist >= 0) & (dist < SWA_WINDOW)
    block_valid = (jnp.arange(nb)[:, None, None] > 0) | (kj[None] >= SWA_BLOCK)
    mask = in_window[None] & block_valid
    bias = rel_bias.astype(jnp.float32)[t5_causal_bucket(jnp.maximum(dist, 0))]
    bias = bias.transpose(2, 0, 1).reshape(SWA_KV_HEADS, grp, SWA_BLOCK, 2 * SWA_BLOCK)
    scores = jnp.where(mask[None, :, None, None], scores + bias, -jnp.inf)
    sink = sinks.astype(jnp.float32).reshape(SWA_KV_HEADS, grp)[:, :, None, None]
    m = jnp.maximum(jnp.max(scores, axis=-1, keepdims=True), sink)
    pr = jnp.exp(scores - m)
    pr = pr / (jnp.sum(pr, axis=-1, keepdims=True) + jnp.exp(sink - m))
    out = jnp.einsum('bnhgij,bnjhd->bnihgd', pr.astype(vband.dtype), vband)
    return out.reshape(b, l, SWA_Q_DIM)


def mem_cross_attention(q, mem_kv):
    b, l, _ = q.shape
    q = q.reshape(b, l, XA_HEADS, XA_HEAD_DIM)
    mk, mv = jnp.split(mem_kv, 2, axis=-1)
    mk = mk.reshape(b, -1, XA_HEADS, XA_HEAD_DIM)
    mv = mv.reshape(b, -1, XA_HEADS, XA_HEAD_DIM)
    s = jnp.einsum('blhd,bmhd->bhlm', q, mk).astype(jnp.float32) * (XA_HEAD_DIM ** -0.5)
    pr = jax.nn.softmax(s, axis=-1).astype(mv.dtype)
    return jnp.einsum('bhlm,bmhd->blhd', pr, mv).reshape(b, l, XA_DIM)


def mixer_sublayer(h, mem, w_in, conv_w, conv_b, dt_bias, a_log, d_skip, ssd_norm_g, swa_sinks, rel_bias,
                   w_mem_kv, w_ssd_o, w_swa_o, w_xa_o, w_out):
    b, l, _ = h.shape
    proj = h @ w_in
    z, xbc, dt_raw, q_s, k_s, v_s, q_x, gates = jnp.split(proj, _split_points(IN_SIZES), axis=-1)
    y_ssd = ssd_mixer(z, xbc, dt_raw, conv_w, conv_b, dt_bias, a_log, d_skip, ssd_norm_g) @ w_ssd_o
    y_swa = swa_mixer(q_s, k_s, v_s, swa_sinks, rel_bias) @ w_swa_o
    y_xa = mem_cross_attention(q_x, mem @ w_mem_kv) @ w_xa_o
    g = jax.nn.sigmoid(gates.astype(jnp.float32)).astype(h.dtype).reshape(b, l, N_BRANCH, D_MODEL)
    merged = g[:, :, 0] * y_ssd + g[:, :, 1] * y_swa + g[:, :, 2] * y_xa
    return merged @ w_out


def route(xf, router_w, router_bias):
    t = xf.shape[0]
    scores = jax.nn.sigmoid((xf @ router_w).astype(jnp.float32))
    sel = scores + router_bias.astype(jnp.float32)
    grp = sel.reshape(t, N_EXPERT_GROUPS, N_EXPERTS // N_EXPERT_GROUPS)
    grp_score = jnp.sum(lax.top_k(grp, 2)[0], axis=-1)
    _, top_g = lax.top_k(grp_score, TOPK_GROUPS)
    gmask = jnp.sum(jax.nn.one_hot(top_g, N_EXPERT_GROUPS, dtype=jnp.float32), axis=1) > 0
    gmask = jnp.repeat(gmask, N_EXPERTS // N_EXPERT_GROUPS, axis=-1)
    _, idx = lax.top_k(jnp.where(gmask, sel, -jnp.inf), TOP_K)
    w = jnp.take_along_axis(scores, idx, axis=-1)
    w = w / jnp.sum(w, axis=-1, keepdims=True) * ROUTED_SCALE
    return idx, w


def routed_experts(xf, idx, w, w1, w3, w2):
    t = xf.shape[0]
    tk = t * TOP_K
    flat_e = idx.reshape(-1)
    order = jnp.argsort(flat_e)
    sorted_e = flat_e[order]
    counts = jnp.bincount(flat_e, length=N_EXPERTS)
    padded = (counts + MOE_BLOCK - 1) // MOE_BLOCK * MOE_BLOCK
    start = jnp.cumsum(counts) - counts
    pend = jnp.cumsum(padded)
    pstart = pend - padded
    dest = pstart[sorted_e] + jnp.arange(tk) - start[sorted_e]
    n_blocks = -(-tk // MOE_BLOCK) + N_EXPERTS
    n_slots = n_blocks * MOE_BLOCK
    slot_tok = jnp.full((n_slots,), t, jnp.int32).at[dest].set((order // TOP_K).astype(jnp.int32))
    slot_w = jnp.zeros((n_slots,), w.dtype).at[dest].set(w.reshape(-1)[order])
    block_e = jnp.minimum(jnp.searchsorted(pend, jnp.arange(n_blocks) * MOE_BLOCK, side='right'), N_EXPERTS - 1)
    xpad = jnp.concatenate([xf, jnp.zeros((1, xf.shape[1]), xf.dtype)], axis=0)

    def block_ffn(args):
        tok, wt, e = args
        xb = xpad[tok]
        hdn = jax.nn.silu(xb @ w1[e]) * (xb @ w3[e])
        return (hdn @ w2[e]) * wt[:, None].astype(xb.dtype)

    yb = lax.map(block_ffn, (slot_tok.reshape(n_blocks, MOE_BLOCK), slot_w.reshape(n_blocks, MOE_BLOCK), block_e))
    out = jnp.zeros((t + 1, xf.shape[1]), xf.dtype).at[slot_tok].add(yb.reshape(n_slots, -1))
    return out[:t]


def moe_sublayer(h, router_w, router_bias, w1, w3, w2, ws1, ws3, ws2):
    b, l, d = h.shape
    xf = h.reshape(b * l, d)
    idx, w = route(xf, router_w, router_bias)
    y = routed_experts(xf, idx, w, w1, w3, w2) + (jax.nn.silu(xf @ ws1) * (xf @ ws3)) @ ws2
    return y.reshape(b, l, d)


def setup_inputs(seed: int = 0) -> dict:
    key = jax.random.key(seed)
    ks = jax.random.split(key, 32)
    f32 = jnp.float32
    nrm = lambda k, shape, s: jax.random.normal(k, shape, f32) * s
    dt0 = jnp.exp(jax.random.uniform(ks[5], (DEPTH, SSD_HEADS), f32, math.log(1e-3), math.log(1e-1)))
    return {
        "x": nrm(ks[0], (BATCH, SEQ, D_MODEL), 1.0),
        "mem": nrm(ks[1], (BATCH, MEM_LEN, D_MODEL), 1.0),
        "w_in": nrm(ks[2], (DEPTH, D_MODEL, IN_TOTAL), D_MODEL ** -0.5),
        "conv_w": nrm(ks[3], (DEPTH, SSD_CONV, SSD_CONV_DIM), SSD_CONV ** -0.5),
        "conv_b": nrm(ks[4], (DEPTH, SSD_CONV_DIM), 0.01),
        "dt_bias": dt0 + jnp.log(-jnp.expm1(-dt0)),
        "a_log": jnp.log(jax.random.uniform(ks[6], (DEPTH, SSD_HEADS), f32, 1.0, 16.0)),
        "d_skip": 1.0 + nrm(ks[7], (DEPTH, SSD_HEADS), 0.01),
        "ssd_norm_g": 1.0 + nrm(ks[8], (DEPTH, SSD_D_INNER), 0.01),
        "swa_sinks": nrm(ks[9], (DEPTH, SWA_HEADS), 1.0),
        "rel_bias": nrm(ks[10], (REL_BUCKETS, SWA_HEADS), 0.5),
        "w_mem_kv": nrm(ks[11], (DEPTH, D_MODEL, 2 * XA_DIM), D_MODEL ** -0.5),
        "w_ssd_o": nrm(ks[12], (DEPTH, SSD_D_INNER, D_MODEL), BETA * SSD_D_INNER ** -0.5),
        "w_swa_o": nrm(ks[13], (DEPTH, SWA_Q_DIM, D_MODEL), BETA * SWA_Q_DIM ** -0.5),
        "w_xa_o": nrm(ks[14], (DEPTH, XA_DIM, D_MODEL), BETA * XA_DIM ** -0.5),
        "w_out": nrm(ks[15], (DEPTH, D_MODEL, D_MODEL), BETA * D_MODEL ** -0.5),
        "ln1_g": 1.0 + nrm(ks[16], (DEPTH, D_MODEL), 0.01),
        "ln1_b": nrm(ks[17], (DEPTH, D_MODEL), 0.01),
        "router_w": nrm(ks[18], (DEPTH, D_MODEL, N_EXPERTS), D_MODEL ** -0.5),
        "router_bias": nrm(ks[19], (DEPTH, N_EXPERTS), 0.01),
        "w1": nrm(ks[20], (DEPTH, N_EXPERTS, D_MODEL, EXPERT_DIM), D_MODEL ** -0.5),
        "w3": nrm(ks[21], (DEPTH, N_EXPERTS, D_MODEL, EXPERT_DIM), D_MODEL ** -0.5),
        "w2": nrm(ks[22], (DEPTH, N_EXPERTS, EXPERT_DIM, D_MODEL), BETA * EXPERT_DIM ** -0.5),
        "ws1": nrm(ks[23], (DEPTH, D_MODEL, SHARED_DIM), D_MODEL ** -0.5),
        "ws3": nrm(ks[24], (DEPTH, D_MODEL, SHARED_DIM), D_MODEL ** -0.5),
        "ws2": nrm(ks[25], (DEPTH, SHARED_DIM, D_MODEL), BETA * SHARED_DIM ** -0.5),
        "ln2_g": 1.0 + nrm(ks[26], (DEPTH, D_MODEL), 0.01),
        "ln2_b": nrm(ks[27], (DEPTH, D_MODEL), 0.01),
    }


def reference(x, mem, w_in, conv_w, conv_b, dt_bias, a_log, d_skip, ssd_norm_g, swa_sinks, rel_bias,
              w_mem_kv, w_ssd_o, w_swa_o, w_xa_o, w_out, ln1_g, ln1_b, router_w, router_bias,
              w1, w3, w2, ws1, ws3, ws2, ln2_g, ln2_b):
    h = x
    for i in range(DEPTH):
        mix = mixer_sublayer(h, mem, w_in[i], conv_w[i], conv_b[i], dt_bias[i], a_log[i], d_skip[i],
                             ssd_norm_g[i], swa_sinks[i], rel_bias, w_mem_kv[i], w_ssd_o[i], w_swa_o[i],
                             w_xa_o[i], w_out[i])
        h = layer_norm(ALPHA * h + mix, ln1_g[i], ln1_b[i])
        ffn = moe_sublayer(h, router_w[i], router_bias[i], w1[i], w3[i], w2[i], ws1[i], ws3[i], ws2[i])
        h = layer_norm(ALPHA * h + ffn, ln2_g[i], ln2_b[i])
    return h
```

```python
import functools
import math

import jax
import jax.numpy as jnp
from jax import lax
from jax.experimental import pallas as pl
from jax.experimental.pallas import tpu as pltpu

F32 = jnp.float32
BF16 = jnp.bfloat16
I32 = jnp.int32

D_MODEL = 2048
MEM_LEN = 256
SSD_D_INNER = D_MODEL
SSD_HEAD_DIM = 64
SSD_HEADS = 32
SSD_GROUPS = 4
SSD_STATE = 128
SSD_CONV = 4
SSD_CHUNK = 128
SWA_HEADS = 16
SWA_KV_HEADS = 4
SWA_HEAD_DIM = 64
SWA_WINDOW = 128
SWA_BLOCK = 128
REL_BUCKETS = 32
REL_MAX_DIST = 128
XA_HEADS = 4
XA_HEAD_DIM = 256
N_BRANCH = 3
N_EXPERTS = 64
TOP_K = 8
N_EXPERT_GROUPS = 8
TOPK_GROUPS = 4
EXPERT_DIM = 512
SHARED_DIM = 512
ROUTED_SCALE = 2.5
LN_EPS = 1e-5
RMS_EPS = 1e-5

SWA_Q_DIM = SWA_HEADS * SWA_HEAD_DIM
SWA_KV_DIM = SWA_KV_HEADS * SWA_HEAD_DIM
XA_DIM = XA_HEADS * XA_HEAD_DIM
BC_DIM = 2 * SSD_GROUPS * SSD_STATE

OFF_GATES = 0
OFF_Z = OFF_GATES + N_BRANCH * D_MODEL
OFF_XS = OFF_Z + SSD_D_INNER
OFF_BC = OFF_XS + SSD_D_INNER
OFF_Q = OFF_BC + BC_DIM
OFF_K = OFF_Q + SWA_Q_DIM
OFF_V = OFF_K + 2 * SWA_KV_DIM
OFF_QX = OFF_V + 2 * SWA_KV_DIM
NP_TOTAL = OFF_QX + XA_DIM

LANES = 128
MOE_BLK = 256
NEG_INF = float("-inf")


def _cparams(sem, vmem_mb=48):
    return pltpu.CompilerParams(dimension_semantics=sem, vmem_limit_bytes=vmem_mb * 1024 * 1024)


def _dot(a, b):
    return jnp.dot(a, b, preferred_element_type=F32)


def _dot_nt(a, b):
    return lax.dot_general(a, b, (((1,), (1,)), ((), ())), preferred_element_type=F32)


def _sigmoid(x):
    return 1.0 / (1.0 + jnp.exp(-x))


def _silu(x):
    return x * _sigmoid(x)


TM_IN = 1024
TN_IN = 1024


def _inproj_kernel(x_ref, w_ref, wdt_ref, p_ref, dt_ref, xb_ref):
    @pl.when(pl.program_id(1) == 0)
    def _():
        xb = x_ref[...].astype(BF16)
        xb_ref[...] = xb
        dt_ref[...] = _dot(xb, wdt_ref[...])

    p_ref[...] = _dot(xb_ref[...], w_ref[...]).astype(BF16)


def _inproj(x2d, wp, wdt):
    t = x2d.shape[0]
    return pl.pallas_call(
        _inproj_kernel,
        grid=(t // TM_IN, NP_TOTAL // TN_IN),
        in_specs=[
            pl.BlockSpec((TM_IN, D_MODEL), lambda i, j: (i, 0)),
            pl.BlockSpec((D_MODEL, TN_IN), lambda i, j: (0, j)),
            pl.BlockSpec((D_MODEL, LANES), lambda i, j: (0, 0)),
        ],
        out_specs=[
            pl.BlockSpec((TM_IN, TN_IN), lambda i, j: (i, j)),
            pl.BlockSpec((TM_IN, LANES), lambda i, j: (i, 0)),
        ],
        out_shape=[jax.ShapeDtypeStruct((t, NP_TOTAL), BF16), jax.ShapeDtypeStruct((t, LANES), F32)],
        scratch_shapes=[pltpu.VMEM((TM_IN, D_MODEL), BF16)],
        compiler_params=_cparams(("parallel", "arbitrary")),
        name="inproj",
    )(x2d, wp, wdt)


def _mm_kernel(a_ref, b_ref, o_ref):
    o_ref[...] = _dot(a_ref[...].astype(BF16), b_ref[...]).astype(o_ref.dtype)


def _mm(a, b, out_dtype, tm, tn):
    m, k = a.shape
    n = b.shape[1]
    return pl.pallas_call(
        _mm_kernel,
        grid=(m // tm, n // tn),
        in_specs=[pl.BlockSpec((tm, k), lambda i, j: (i, 0)), pl.BlockSpec((k, tn), lambda i, j: (0, j))],
        out_specs=pl.BlockSpec((tm, tn), lambda i, j: (i, j)),
        out_shape=jax.ShapeDtypeStruct((m, n), out_dtype),
        compiler_params=_cparams(("parallel", "parallel")),
        name="mm",
    )(a, b)


CONV_PAD = 8
Q = SSD_CHUNK


def _ssd_kernel(xs_ref, bc_ref, z_ref, dtr_ref, cwx_ref, cbx_ref, cwb_ref, cbb_ref, dtb_ref, alog_ref,
                dsk_ref, ng_ref, o_ref, hx_ref, hb_ref, st_ref, cumt_ref, dtt_ref, wt_ref, y_ref):
    c = pl.program_id(1)

    @pl.when(c == 0)
    def _():
        hx_ref[0:CONV_PAD, :] = jnp.zeros((CONV_PAD, SSD_D_INNER), F32)
        hb_ref[0:CONV_PAD, :] = jnp.zeros((CONV_PAD, BC_DIM), F32)
        st_ref[...] = jnp.zeros_like(st_ref)

    def conv_silu(in_ref, h_ref, w_ref, b_ref):
        cur = in_ref[...].astype(F32)
        h_ref[CONV_PAD:CONV_PAD + Q, :] = cur
        acc = b_ref[...]
        for k in range(SSD_CONV):
            off = CONV_PAD - (SSD_CONV - 1) + k
            acc = acc + w_ref[k:k + 1, :] * h_ref[off:off + Q, :]
        h_ref[0:CONV_PAD, :] = cur[Q - CONV_PAD:Q, :]
        return _silu(acc)

    xs_c = conv_silu(xs_ref, hx_ref, cwx_ref, cbx_ref)
    bc_c = conv_silu(bc_ref, hb_ref, cwb_ref, cbb_ref)

    lane = lax.broadcasted_iota(I32, (Q, LANES), 1)
    row = lax.broadcasted_iota(I32, (Q, LANES), 0)
    tril = row >= lane
    lo_mask = lane < SSD_HEAD_DIM

    dtv = dtr_ref[...] + dtb_ref[...]
    dt = jnp.maximum(dtv, 0.0) + jnp.log1p(jnp.exp(-jnp.abs(dtv)))
    a = jnp.where(lane[0:1, :] < SSD_HEADS, -jnp.exp(alog_ref[...]), 0.0)
    la = dt * a
    tri = jnp.where(tril, 1.0, 0.0).astype(BF16)
    p1 = la.astype(BF16)
    r1 = la - p1.astype(F32)
    p2 = r1.astype(BF16)
    p3 = (r1 - p2.astype(F32)).astype(BF16)
    cum = _dot(tri, p1) + _dot(tri, p2) + _dot(tri, p3)
    cum_t = cum.T
    dt_t = dt.T
    last = cum_t[:, Q - 1:Q]
    cumt_ref[...] = cum_t
    dtt_ref[...] = dt_t
    wt_ref[...] = jnp.exp(last - cum_t) * dt_t
    chunk_decay = jnp.exp(last)

    def head_parts(h, cb, cm, bm_t):
        col = jnp.broadcast_to(cum[:, h:h + 1], (Q, Q))
        rw = cumt_ref[h:h + 1, :]
        dec = jnp.exp(jnp.where(tril, col - rw, NEG_INF))
        m = cb * (dec * dtt_ref[h:h + 1, :])
        ce = cm * jnp.exp(col)
        sl = bm_t * wt_ref[h:h + 1, :]
        return m.astype(BF16), ce.astype(BF16), sl.astype(BF16)

    for g in range(SSD_GROUPS):
        bm = bc_c[:, g * SSD_STATE:(g + 1) * SSD_STATE]
        cm = bc_c[:, BC_DIM // 2 + g * SSD_STATE:BC_DIM // 2 + (g + 1) * SSD_STATE]
        cb = _dot_nt(cm.astype(BF16), bm.astype(BF16))
        bm_t = bm.T
        for qd in range(SSD_HEADS // SSD_GROUPS // 2):
            hp = g * (SSD_HEADS // SSD_GROUPS // 2) + qd
            ha, hb = 2 * hp, 2 * hp + 1
            sl_ = slice(hp * LANES, (hp + 1) * LANES)
            xs_pair = xs_c[:, sl_]
            x_lo = jnp.where(lo_mask, xs_pair, 0.0).astype(BF16)
            x_hi = jnp.where(lo_mask, 0.0, xs_pair).astype(BF16)
            h_prev = st_ref[:, sl_]
            h_lo = jnp.where(lo_mask, h_prev, 0.0).astype(BF16)
            h_hi = jnp.where(lo_mask, 0.0, h_prev).astype(BF16)
            m_a, ce_a, sl_a = head_parts(ha, cb, cm, bm_t)
            m_b, ce_b, sl_b = head_parts(hb, cb, cm, bm_t)
            lhs = jnp.concatenate([m_a, ce_a, m_b, ce_b], axis=1)
            rhs = jnp.concatenate([x_lo, h_lo, x_hi, h_hi], axis=0)
            y_ref[:, sl_] = _dot(lhs, rhs)
            new = _dot(jnp.concatenate([sl_a, sl_b], axis=1), jnp.concatenate([x_lo, x_hi], axis=0))
            cd = jnp.where(lo_mask[0:1, :], chunk_decay[ha:ha + 1, :], chunk_decay[hb:hb + 1, :])
            st_ref[:, sl_] = h_prev * cd + new

    y = y_ref[...] + xs_c * dsk_ref[...]
    y = y * _silu(z_ref[...].astype(F32))
    gw = SSD_D_INNER // SSD_GROUPS
    for g in range(SSD_GROUPS):
        yg = y[:, g * gw:(g + 1) * gw]
        ms = jnp.mean(yg * yg, axis=-1, keepdims=True)
        o_ref[:, g * gw:(g + 1) * gw] = ((yg * lax.rsqrt(ms + RMS_EPS)) * ng_ref[:, g * gw:(g + 1) * gw]).astype(BF16)


def _ssd(p, dt_raw, cwx, cbx, cwb, cbb, dtb, alog, dsk, ng, batch, seq):
    nc = seq // Q
    t = batch * seq
    rowblk = lambda b, c: b * nc + c
    const = lambda b, c: (0, 0)
    return pl.pallas_call(
        _ssd_kernel,
        grid=(batch, nc),
        in_specs=[
            pl.BlockSpec((Q, SSD_D_INNER), lambda b, c: (rowblk(b, c), OFF_XS // SSD_D_INNER)),
            pl.BlockSpec((Q, BC_DIM), lambda b, c: (rowblk(b, c), OFF_BC // BC_DIM)),
            pl.BlockSpec((Q, SSD_D_INNER), lambda b, c: (rowblk(b, c), OFF_Z // SSD_D_INNER)),
            pl.BlockSpec((Q, LANES), lambda b, c: (rowblk(b, c), 0)),
            pl.BlockSpec((SSD_CONV, SSD_D_INNER), const),
            pl.BlockSpec((1, SSD_D_INNER), const),
            pl.BlockSpec((SSD_CONV, BC_DIM), const),
            pl.BlockSpec((1, BC_DIM), const),
            pl.BlockSpec((1, LANES), const),
            pl.BlockSpec((1, LANES), const),
            pl.BlockSpec((1, SSD_D_INNER), const),
            pl.BlockSpec((1, SSD_D_INNER), const),
        ],
        out_specs=pl.BlockSpec((Q, SSD_D_INNER), lambda b, c: (rowblk(b, c), 0)),
        out_shape=jax.ShapeDtypeStruct((t, SSD_D_INNER), BF16),
        scratch_shapes=[
            pltpu.VMEM((CONV_PAD + Q, SSD_D_INNER), F32),
            pltpu.VMEM((CONV_PAD + Q, BC_DIM), F32),
            pltpu.VMEM((SSD_STATE, SSD_D_INNER), F32),
            pltpu.VMEM((LANES, Q), F32),
            pltpu.VMEM((LANES, Q), F32),
            pltpu.VMEM((LANES, Q), F32),
            pltpu.VMEM((Q, SSD_D_INNER), F32),
        ],
        compiler_params=_cparams(("parallel", "arbitrary")),
        name="ssd",
    )(p, p, p, dt_raw, cwx, cbx, cwb, cbb, dtb, alog, dsk, ng)


N_PAIRS = SWA_HEADS // 2


def _swa_kernel(sink_ref, q_ref, kp_ref, kc_ref, vp_ref, vc_ref, bias_ref, o_ref):
    n = pl.program_id(1)
    blk = SWA_BLOCK
    kband = jnp.concatenate([kp_ref[...], kc_ref[...]], axis=0)
    vband = jnp.concatenate([vp_ref[...], vc_ref[...]], axis=0)
    lane = lax.broadcasted_iota(I32, (1, LANES), 1)
    lo_f = jnp.where(lane < SWA_HEAD_DIM, 1.0, 0.0)
    lo_m = lo_f.astype(BF16)
    hi_m = (1.0 - lo_f).astype(BF16)
    colj = lax.broadcasted_iota(I32, (2 * blk, 2 * blk), 1)
    rowi = lax.broadcasted_iota(I32, (2 * blk, 1), 0)
    valid = (colj >= blk) | (n > 0)
    scale = SWA_HEAD_DIM ** -0.5
    for p in range(N_PAIRS):
        kv = p // (SWA_HEADS // SWA_KV_HEADS // 2)
        qp = q_ref[:, p * LANES:(p + 1) * LANES]
        qs = jnp.concatenate([qp * lo_m, qp * hi_m], axis=0)
        kd = kband[:, kv * LANES:(kv + 1) * LANES]
        s = _dot_nt(qs, kd) * scale + bias_ref[p]
        s = jnp.where(valid, s, NEG_INF)
        sink = jnp.where(rowi < blk, sink_ref[2 * p], sink_ref[2 * p + 1])
        m = jnp.maximum(jnp.max(s, axis=-1, keepdims=True), sink)
        e = jnp.exp(s - m)
        den = jnp.sum(e, axis=-1, keepdims=True) + jnp.exp(sink - m)
        pr = (e * (1.0 / den)).astype(BF16)
        vd = vband[:, kv * LANES:(kv + 1) * LANES]
        lhs = jnp.concatenate([pr[:blk], pr[blk:]], axis=1)
        rhs = jnp.concatenate([vd * lo_m, vd * hi_m], axis=0)
        o_ref[:, p * LANES:(p + 1) * LANES] = _dot(lhs, rhs).astype(BF16)


def _swa(p, sinks, bias, batch, seq):
    nb = seq // SWA_BLOCK
    t = batch * seq
    kw = 2 * SWA_KV_DIM
    cur = lambda b, n, s: b * nb + n
    prev = lambda b, n, s: b * nb + jnp.maximum(n - 1, 0)
    return pl.pallas_call(
        _swa_kernel,
        grid_spec=pltpu.PrefetchScalarGridSpec(
            num_scalar_prefetch=1,
            grid=(batch, nb),
            in_specs=[
                pl.BlockSpec((SWA_BLOCK, SWA_Q_DIM), lambda b, n, s: (cur(b, n, s), OFF_Q // SWA_Q_DIM)),
                pl.BlockSpec((SWA_BLOCK, kw), lambda b, n, s: (prev(b, n, s), OFF_K // kw)),
                pl.BlockSpec((SWA_BLOCK, kw), lambda b, n, s: (cur(b, n, s), OFF_K // kw)),
                pl.BlockSpec((SWA_BLOCK, kw), lambda b, n, s: (prev(b, n, s), OFF_V // kw)),
                pl.BlockSpec((SWA_BLOCK, kw), lambda b, n, s: (cur(b, n, s), OFF_V // kw)),
                pl.BlockSpec((N_PAIRS, 2 * SWA_BLOCK, 2 * SWA_BLOCK), lambda b, n, s: (0, 0, 0)),
            ],
            out_specs=pl.BlockSpec((SWA_BLOCK, SWA_Q_DIM), lambda b, n, s: (cur(b, n, s), 0)),
        ),
        out_shape=jax.ShapeDtypeStruct((t, SWA_Q_DIM), BF16),
        compiler_params=_cparams(("parallel", "arbitrary")),
        name="swa",
    )(sinks, p, p, p, p, p, bias)


def _t5_causal_bucket(dist):
    max_exact = REL_BUCKETS // 2
    large = max_exact + (jnp.log(jnp.maximum(dist, 1).astype(F32) / max_exact)
                         / math.log(REL_MAX_DIST / max_exact) * (REL_BUCKETS - max_exact)).astype(I32)
    large = jnp.minimum(large, REL_BUCKETS - 1)
    return jnp.where(dist < max_exact, dist, large)


def _swa_bias_table(rel_bias):
    qi = jnp.arange(SWA_BLOCK)[:, None]
    kj = jnp.arange(2 * SWA_BLOCK)[None, :]
    dist = qi + SWA_BLOCK - kj
    in_window = (dist >= 0) & (dist < SWA_WINDOW)
    bias = rel_bias.astype(F32)[_t5_causal_bucket(jnp.maximum(dist, 0))]
    bias = jnp.where(in_window[:, :, None], bias, NEG_INF).transpose(2, 0, 1)
    return bias.reshape(N_PAIRS, 2 * SWA_BLOCK, 2 * SWA_BLOCK)


TL_XA = 512


def _xa_kernel(q_ref, mk_ref, mv_ref, o_ref):
    s = _dot_nt(q_ref[...], mk_ref[...]) * (XA_HEAD_DIM ** -0.5)
    m = jnp.max(s, axis=-1, keepdims=True)
    e = jnp.exp(s - m)
    pr = e * (1.0 / jnp.sum(e, axis=-1, keepdims=True))
    o_ref[...] = _dot(pr.astype(BF16), mv_ref[...]).astype(BF16)


def _xa(p, memkv, batch, seq):
    nl = seq // TL_XA
    t = batch * seq
    return pl.pallas_call(
        _xa_kernel,
        grid=(batch, nl, XA_HEADS),
        in_specs=[
            pl.BlockSpec((TL_XA, XA_HEAD_DIM), lambda b, l, h: (b * nl + l, OFF_QX // XA_HEAD_DIM + h)),
            pl.BlockSpec((MEM_LEN, XA_HEAD_DIM), lambda b, l, h: (b, h)),
            pl.BlockSpec((MEM_LEN, XA_HEAD_DIM), lambda b, l, h: (b, XA_HEADS + h)),
        ],
        out_specs=pl.BlockSpec((TL_XA, XA_HEAD_DIM), lambda b, l, h: (b * nl + l, h)),
        out_shape=jax.ShapeDtypeStruct((t, XA_DIM), BF16),
        compiler_params=_cparams(("parallel", "parallel", "parallel")),
        name="xattn",
    )(p, memkv, memkv)


TM_MG = 1024
TN_MG = 512


def _merge_kernel(ys_ref, yw_ref, yx_ref, ws_ref, ww_ref, wx_ref, g0_ref, g1_ref, g2_ref, o_ref):
    a = _dot(ys_ref[...], ws_ref[...])
    b = _dot(yw_ref[...], ww_ref[...])
    c = _dot(yx_ref[...], wx_ref[...])
    g0 = _sigmoid(g0_ref[...].astype(F32))
    g1 = _sigmoid(g1_ref[...].astype(F32))
    g2 = _sigmoid(g2_ref[...].astype(F32))
    o_ref[...] = (g0 * a + g1 * b + g2 * c).astype(BF16)


def _merge(y_ssd, y_swa, y_xa, w_ssd_o, w_swa_o, w_xa_o, p):
    t = y_ssd.shape[0]
    gpb = D_MODEL // TN_MG
    gate = lambda k: pl.BlockSpec((TM_MG, TN_MG), lambda i, j: (i, OFF_GATES // TN_MG + k * gpb + j))
    return pl.pallas_call(
        _merge_kernel,
        grid=(t // TM_MG, D_MODEL // TN_MG),
        in_specs=[
            pl.BlockSpec((TM_MG, SSD_D_INNER), lambda i, j: (i, 0)),
            pl.BlockSpec((TM_MG, SWA_Q_DIM), lambda i, j: (i, 0)),
            pl.BlockSpec((TM_MG, XA_DIM), lambda i, j: (i, 0)),
            pl.BlockSpec((SSD_D_INNER, TN_MG), lambda i, j: (0, j)),
            pl.BlockSpec((SWA_Q_DIM, TN_MG), lambda i, j: (0, j)),
            pl.BlockSpec((XA_DIM, TN_MG), lambda i, j: (0, j)),
            gate(0), gate(1), gate(2),
        ],
        out_specs=pl.BlockSpec((TM_MG, TN_MG), lambda i, j: (i, j)),
        out_shape=jax.ShapeDtypeStruct((t, D_MODEL), BF16),
        compiler_params=_cparams(("parallel", "parallel")),
        name="merge",
    )(y_ssd, y_swa, y_xa, w_ssd_o, w_swa_o, w_xa_o, p, p, p)


TM_LN = 512


def _layer_norm(v, g, b):
    mu = jnp.mean(v, axis=-1, keepdims=True)
    d = v - mu
    var = jnp.mean(d * d, axis=-1, keepdims=True)
    return (d * lax.rsqrt(var + LN_EPS)) * g + b


def _out_ln_kernel(alpha, m_ref, w_ref, x_ref, g_ref, b_ref, h_ref, hb_ref):
    mix = _dot(m_ref[...], w_ref[...])
    hn = _layer_norm(alpha * x_ref[...] + mix, g_ref[...], b_ref[...])
    h_ref[...] = hn
    hb_ref[...] = hn.astype(BF16)


def _out_ln(merged, w_out, x2d, g, b, alpha):
    t = x2d.shape[0]
    row = lambda i: (i, 0)
    const = lambda i: (0, 0)
    return pl.pallas_call(
        functools.partial(_out_ln_kernel, alpha),
        grid=(t // TM_LN,),
        in_specs=[
            pl.BlockSpec((TM_LN, D_MODEL), row),
            pl.BlockSpec((D_MODEL, D_MODEL), const),
            pl.BlockSpec((TM_LN, D_MODEL), row),
            pl.BlockSpec((1, D_MODEL), const),
            pl.BlockSpec((1, D_MODEL), const),
        ],
        out_specs=[pl.BlockSpec((TM_LN, D_MODEL), row), pl.BlockSpec((TM_LN, D_MODEL), row)],
        out_shape=[jax.ShapeDtypeStruct((t, D_MODEL), F32), jax.ShapeDtypeStruct((t, D_MODEL), BF16)],
        compiler_params=_cparams(("parallel",)),
        name="out_ln1",
    )(merged, w_out, x2d, g, b)


TM_RT = 512
EPG = N_EXPERTS // N_EXPERT_GROUPS


def _router_kernel(h_ref, whi_ref, wlo_ref, bias_ref, su_ref, idx_ref, w_ref, rank_ref, cnt_ref, run_ref):
    @pl.when(pl.program_id(0) == 0)
    def _():
        run_ref[...] = jnp.zeros_like(run_ref)

    h = h_ref[...]
    h_hi = h.astype(BF16)
    h_lo = (h - h_hi.astype(F32)).astype(BF16)
    whi = whi_ref[...]
    logits = _dot_nt(whi, h_hi) + _dot_nt(whi, h_lo) + _dot_nt(wlo_ref[...], h_hi)
    scores = _sigmoid(logits)
    sel = scores + bias_ref[...]

    iota_m = lax.broadcasted_iota(I32, (EPG, TM_RT), 0).astype(F32)
    iota_g = lax.broadcasted_iota(I32, (N_EXPERT_GROUPS, TM_RT), 0).astype(F32)
    gs = []
    for g in range(N_EXPERT_GROUPS):
        v = sel[g * EPG:(g + 1) * EPG, :]
        m1 = jnp.max(v, axis=0, keepdims=True)
        i1 = jnp.min(jnp.where(v == m1, iota_m, float(EPG)), axis=0, keepdims=True)
        m2 = jnp.max(jnp.where(iota_m == i1, NEG_INF, v), axis=0, keepdims=True)
        gs.append(m1 + m2)
    gsc = jnp.concatenate(gs, axis=0)
    chosen = jnp.zeros((N_EXPERT_GROUPS, TM_RT), F32)
    for _ in range(TOPK_GROUPS):
        m = jnp.max(gsc, axis=0, keepdims=True)
        ig = jnp.min(jnp.where(gsc == m, iota_g, float(N_EXPERT_GROUPS)), axis=0, keepdims=True)
        hit = iota_g == ig
        chosen = jnp.where(hit, 1.0, chosen)
        gsc = jnp.where(hit, NEG_INF, gsc)
    gmask = jnp.concatenate(
        [jnp.broadcast_to(chosen[g:g + 1, :], (EPG, TM_RT)) for g in range(N_EXPERT_GROUPS)], axis=0)
    masked = jnp.where(gmask > 0.0, sel, NEG_INF)

    iota_e = lax.broadcasted_iota(I32, (N_EXPERTS, TM_RT), 0).astype(F32)
    hits, idxs, ws = [], [], []
    for _ in range(TOP_K):
        m = jnp.max(masked, axis=0, keepdims=True)
        ie = jnp.min(jnp.where(masked == m, iota_e, float(N_EXPERTS)), axis=0, keepdims=True)
        hit = iota_e == ie
        hits.append(hit)
        idxs.append(ie)
        ws.append(jnp.sum(jnp.where(hit, scores, 0.0), axis=0, keepdims=True))
        masked = jnp.where(hit, NEG_INF, masked)
    wsum = ws[0]
    for k in range(1, TOP_K):
        wsum = wsum + ws[k]
    inv = 1.0 / wsum
    idx_ref[...] = jnp.concatenate(idxs, axis=0).astype(I32)
    w_ref[...] = jnp.concatenate([w * inv * ROUTED_SCALE for w in ws], axis=0)

    assign = jnp.zeros((N_EXPERTS, TM_RT), F32)
    for hit in hits:
        assign = jnp.where(hit, 1.0, assign)
    pos = _dot(assign.astype(BF16), su_ref[...]) + run_ref[...]
    rank_ref[...] = jnp.concatenate(
        [jnp.sum(jnp.where(hit, pos, 0.0), axis=0, keepdims=True) for hit in hits], axis=0).astype(I32)
    run = run_ref[...] + jnp.sum(assign, axis=1, keepdims=True)
    run_ref[...] = run
    cnt_ref[...] = run


def _router(h1, r_hi, r_lo, rbias, su):
    t = h1.shape[0]
    tile = lambda i: (0, i)
    const = lambda i: (0, 0)
    return pl.pallas_call(
        _router_kernel,
        grid=(t // TM_RT,),
        in_specs=[
            pl.BlockSpec((TM_RT, D_MODEL), lambda i: (i, 0)),
            pl.BlockSpec((N_EXPERTS, D_MODEL), const),
            pl.BlockSpec((N_EXPERTS, D_MODEL), const),
            pl.BlockSpec((N_EXPERTS, 1), const),
            pl.BlockSpec((TM_RT, TM_RT), const),
        ],
        out_specs=[
            pl.BlockSpec((TOP_K, TM_RT), tile),
            pl.BlockSpec((TOP_K, TM_RT), tile),
            pl.BlockSpec((TOP_K, TM_RT), tile),
            pl.BlockSpec((N_EXPERTS, TM_RT), const),
        ],
        out_shape=[
            jax.ShapeDtypeStruct((TOP_K, t), I32),
            jax.ShapeDtypeStruct((TOP_K, t), F32),
            jax.ShapeDtypeStruct((TOP_K, t), I32),
            jax.ShapeDtypeStruct((N_EXPERTS, TM_RT), F32),
        ],
        scratch_shapes=[pltpu.VMEM((N_EXPERTS, TM_RT), F32)],
        compiler_params=_cparams(("arbitrary",)),
        name="router",
    )(h1, r_hi, r_lo, rbias, su)


def _expert_kernel(be_ref, nu_ref, x_ref, w1_ref, w3_ref, w2_ref, sw_ref, o_ref, w1b, w3b, w2b):
    b = pl.program_id(0)
    e = be_ref[b]
    prev = be_ref[jnp.maximum(b - 1, 0)]

    @pl.when((b == 0) | (e != prev))
    def _():
        w1b[...] = w1_ref[0].astype(BF16)
        w3b[...] = w3_ref[0].astype(BF16)
        w2b[...] = w2_ref[0].astype(BF16)

    @pl.when(b < nu_ref[0])
    def _():
        x = x_ref[...]
        hdn = _silu(_dot(x, w1b[...])) * _dot(x, w3b[...])
        y = _dot(hdn.astype(BF16), w2b[...])
        o_ref[...] = (y * sw_ref[...]).astype(BF16)

    @pl.when(b >= nu_ref[0])
    def _():
        o_ref[...] = jnp.zeros_like(o_ref)


def _experts(block_e, n_used, x_sorted, w1, w3, w2, slot_w):
    n_slots = x_sorted.shape[0]
    n_blocks = n_slots // MOE_BLK
    return pl.pallas_call(
        _expert_kernel,
        grid_spec=pltpu.PrefetchScalarGridSpec(
            num_scalar_prefetch=2,
            grid=(n_blocks,),
            in_specs=[
                pl.BlockSpec((MOE_BLK, D_MODEL), lambda b, be, nu: (b, 0)),
                pl.BlockSpec((1, D_MODEL, EXPERT_DIM), lambda b, be, nu: (be[b], 0, 0)),
                pl.BlockSpec((1, D_MODEL, EXPERT_DIM), lambda b, be, nu: (be[b], 0, 0)),
                pl.BlockSpec((1, EXPERT_DIM, D_MODEL), lambda b, be, nu: (be[b], 0, 0)),
                pl.BlockSpec((MOE_BLK, 1), lambda b, be, nu: (b, 0)),
            ],
            out_specs=pl.BlockSpec((MOE_BLK, D_MODEL), lambda b, be, nu: (b, 0)),
            scratch_shapes=[
                pltpu.VMEM((D_MODEL, EXPERT_DIM), BF16),
                pltpu.VMEM((D_MODEL, EXPERT_DIM), BF16),
                pltpu.VMEM((EXPERT_DIM, D_MODEL), BF16),
            ],
        ),
        out_shape=jax.ShapeDtypeStruct((n_slots, D_MODEL), BF16),
        compiler_params=_cparams(("arbitrary",), vmem_mb=56),
        name="experts",
    )(block_e, n_used, x_sorted, w1, w3, w2, slot_w)


TM_FN = 256


def _final_kernel(alpha, yg_ref, hb_ref, h_ref, ws1_ref, ws3_ref, ws2_ref, g_ref, b_ref, o_ref):
    acc = yg_ref[0].astype(F32)
    for k in range(1, TOP_K):
        acc = acc + yg_ref[k].astype(F32)
    xb = hb_ref[...]
    hdn = _silu(_dot(xb, ws1_ref[...])) * _dot(xb, ws3_ref[...])
    ffn = acc + _dot(hdn.astype(BF16), ws2_ref[...])
    o_ref[...] = _layer_norm(alpha * h_ref[...] + ffn, g_ref[...], b_ref[...])


def _final(yg, h1b, h1, ws1, ws3, ws2, g, b, alpha):
    t = h1.shape[0]
    row = lambda i: (i, 0)
    const = lambda i: (0, 0)
    return pl.pallas_call(
        functools.partial(_final_kernel, alpha),
        grid=(t // TM_FN,),
        in_specs=[
            pl.BlockSpec((TOP_K, TM_FN, D_MODEL), lambda i: (0, i, 0)),
            pl.BlockSpec((TM_FN, D_MODEL), row),
            pl.BlockSpec((TM_FN, D_MODEL), row),
            pl.BlockSpec((D_MODEL, SHARED_DIM), const),
            pl.BlockSpec((D_MODEL, SHARED_DIM), const),
            pl.BlockSpec((SHARED_DIM, D_MODEL), const),
            pl.BlockSpec((1, D_MODEL), const),
            pl.BlockSpec((1, D_MODEL), const),
        ],
        out_specs=pl.BlockSpec((TM_FN, D_MODEL), row),
        out_shape=jax.ShapeDtypeStruct((t, D_MODEL), F32),
        compiler_params=_cparams(("parallel",)),
        name="combine_ln2",
    )(yg, h1b, h1, ws1, ws3, ws2, g, b)


def _dup_heads(w, heads, dim):
    d = w.shape[0]
    w = w.reshape(d, heads, 1, dim)
    return jnp.broadcast_to(w, (d, heads, 2, dim)).reshape(d, heads * 2 * dim)


def _prep_w_in(w_in):
    sizes = (SSD_D_INNER, SSD_D_INNER + BC_DIM, SSD_HEADS, SWA_Q_DIM, SWA_KV_DIM, SWA_KV_DIM, XA_DIM,
             N_BRANCH * D_MODEL)
    pts = []
    acc = 0
    for s in sizes[:-1]:
        acc += s
        pts.append(acc)
    z, xbc, dt, q_s, k_s, v_s, q_x, gates = jnp.split(w_in, pts, axis=1)
    wp = jnp.concatenate(
        [gates, z, xbc, q_s, _dup_heads(k_s, SWA_KV_HEADS, SWA_HEAD_DIM),
         _dup_heads(v_s, SWA_KV_HEADS, SWA_HEAD_DIM), q_x], axis=1).astype(BF16)
    wdt = jnp.pad(dt, ((0, 0), (0, LANES - SSD_HEADS))).astype(BF16)
    return wp, wdt


def _pad_lanes(v):
    return jnp.pad(v, (0, LANES - v.shape[0]))[None, :].astype(F32)


def _layer(h2d, mem2d, batch, seq, w_in, conv_w, conv_b, dt_bias, a_log, d_skip, ssd_norm_g, swa_sinks, rel_bias,
           w_mem_kv, w_ssd_o, w_swa_o, w_xa_o, w_out, ln1_g, ln1_b, router_w, router_bias, w1, w3, w2, ws1, ws3,
           ws2, ln2_g, ln2_b, alpha):
    t = batch * seq
    wp, wdt = _prep_w_in(w_in)
    p, dt_raw = _inproj(h2d, wp, wdt)

    y_ssd = _ssd(
        p, dt_raw,
        conv_w[:, :SSD_D_INNER], conv_b[None, :SSD_D_INNER], conv_w[:, SSD_D_INNER:], conv_b[None, SSD_D_INNER:],
        _pad_lanes(dt_bias), _pad_lanes(a_log),
        jnp.repeat(d_skip, SSD_HEAD_DIM)[None, :], ssd_norm_g[None, :], batch, seq)

    y_swa = _swa(p, swa_sinks.astype(F32), _swa_bias_table(rel_bias), batch, seq)

    memkv = _mm(mem2d, w_mem_kv.astype(BF16), BF16, batch * MEM_LEN, 1024)
    y_xa = _xa(p, memkv, batch, seq)

    merged = _merge(y_ssd, y_swa, y_xa, w_ssd_o.astype(BF16), w_swa_o.astype(BF16), w_xa_o.astype(BF16), p)
    h1, h1b = _out_ln(merged, w_out.astype(BF16), h2d, ln1_g[None, :], ln1_b[None, :], alpha)

    rwt = router_w.T
    r_hi = rwt.astype(BF16)
    r_lo = (rwt - r_hi.astype(F32)).astype(BF16)
    ti = jnp.arange(TM_RT)
    su = (ti[:, None] < ti[None, :]).astype(BF16)
    idx_t, w_t, rank_t, cnt = _router(h1, r_hi, r_lo, router_bias[:, None].astype(F32), su)

    counts = cnt[:, 0].astype(I32)
    padded = (counts + MOE_BLK - 1) // MOE_BLK * MOE_BLK
    pend = jnp.cumsum(padded)
    pstart = pend - padded
    n_blocks = t * TOP_K // MOE_BLK + N_EXPERTS
    n_slots = n_blocks * MOE_BLK
    dest = (pstart[idx_t] + rank_t).reshape(-1)
    tok = jnp.broadcast_to(jnp.arange(t, dtype=I32)[None, :], (TOP_K, t)).reshape(-1)
    slot_tok = jnp.zeros((n_slots,), I32).at[dest].set(tok)
    slot_w = jnp.zeros((n_slots,), F32).at[dest].set(w_t.reshape(-1))
    block_e = jnp.minimum(jnp.searchsorted(pend, jnp.arange(n_blocks, dtype=I32) * MOE_BLK, side='right'),
                          N_EXPERTS - 1).astype(I32)
    n_used = (pend[-1:] // MOE_BLK).astype(I32)

    x_sorted = jnp.take(h1b, slot_tok, axis=0)
    ys = _experts(block_e, n_used, x_sorted, w1, w3, w2, slot_w[:, None])
    yg = jnp.take(ys, dest, axis=0).reshape(TOP_K, t, D_MODEL)

    return _final(yg, h1b, h1, ws1.astype(BF16), ws3.astype(BF16), ws2.astype(BF16), ln2_g[None, :],
                  ln2_b[None, :], alpha)


def kernel(x, mem, w_in, conv_w, conv_b, dt_bias, a_log, d_skip, ssd_norm_g, swa_sinks, rel_bias, w_mem_kv, w_ssd_o, w_swa_o, w_xa_o, w_out, ln1_g, ln1_b, router_w, router_bias, w1, w3, w2, ws1, ws3, ws2, ln2_g, ln2_b):
    batch, seq, d = x.shape
    depth = w_in.shape[0]
    alpha = (2.0 * depth) ** 0.25
    h = x.reshape(batch * seq, d)
    mem2d = mem.reshape(batch * MEM_LEN, d)
    for i in range(depth):
        h = _layer(h, mem2d, batch, seq, w_in[i], conv_w[i], conv_b[i], dt_bias[i], a_log[i], d_skip[i],
                   ssd_norm_g[i], swa_sinks[i], rel_bias, w_mem_kv[i], w_ssd_o[i], w_swa_o[i], w_xa_o[i], w_out[i],
                   ln1_g[i], ln1_b[i], router_w[i], router_bias[i], w1[i], w3[i], w2[i], ws1[i], ws3[i], ws2[i],
                   ln2_g[i], ln2_b[i], alpha)
    return h.reshape(batch, seq, d)
```

```python
import functools
import math

import jax
import jax.numpy as jnp
from jax import lax
from jax.experimental import pallas as pl
from jax.experimental.pallas import tpu as pltpu

F32 = jnp.float32
BF16 = jnp.bfloat16
I32 = jnp.int32
U32 = jnp.uint32

D_MODEL = 2048
MEM_LEN = 256
SSD_D_INNER = D_MODEL
SSD_HEAD_DIM = 64
SSD_HEADS = 32
SSD_GROUPS = 4
SSD_STATE = 128
SSD_CONV = 4
SSD_CHUNK = 128
SWA_HEADS = 16
SWA_KV_HEADS = 4
SWA_HEAD_DIM = 64
SWA_WINDOW = 128
SWA_BLOCK = 128
REL_BUCKETS = 32
REL_MAX_DIST = 128
XA_HEADS = 4
XA_HEAD_DIM = 256
N_BRANCH = 3
N_EXPERTS = 64
TOP_K = 8
N_EXPERT_GROUPS = 8
TOPK_GROUPS = 4
EXPERT_DIM = 512
SHARED_DIM = 512
ROUTED_SCALE = 2.5
LN_EPS = 1e-5
RMS_EPS = 1e-5

SWA_Q_DIM = SWA_HEADS * SWA_HEAD_DIM
SWA_KV_DIM = SWA_KV_HEADS * SWA_HEAD_DIM
XA_DIM = XA_HEADS * XA_HEAD_DIM
BC_DIM = 2 * SSD_GROUPS * SSD_STATE

OFF_GATES = 0
OFF_Z = OFF_GATES + N_BRANCH * D_MODEL
OFF_XS = OFF_Z + SSD_D_INNER
OFF_BC = OFF_XS + SSD_D_INNER
OFF_Q = OFF_BC + BC_DIM
OFF_K = OFF_Q + SWA_Q_DIM
OFF_V = OFF_K + 2 * SWA_KV_DIM
OFF_QX = OFF_V + 2 * SWA_KV_DIM
NP_TOTAL = OFF_QX + XA_DIM

LANES = 128
MOE_BLK = 256
NEG_INF = float("-inf")


def _cparams(sem, vmem_mb=48):
    return pltpu.CompilerParams(dimension_semantics=sem, vmem_limit_bytes=vmem_mb * 1024 * 1024)


def _dot(a, b):
    return jnp.dot(a, b, preferred_element_type=F32)


def _dot_nt(a, b):
    return lax.dot_general(a, b, (((1,), (1,)), ((), ())), preferred_element_type=F32)


def _sigmoid(x):
    return 1.0 / (1.0 + jnp.exp(-x))


def _silu(x):
    return x * _sigmoid(x)


TM_IN = 1024
TN_IN = 1024


def _inproj_kernel(x_ref, w_ref, wdt_ref, p_ref, dt_ref, xb_ref):
    @pl.when(pl.program_id(1) == 0)
    def _():
        xb = x_ref[...].astype(BF16)
        xb_ref[...] = xb
        dt_ref[...] = _dot(xb, wdt_ref[...])

    p_ref[...] = _dot(xb_ref[...], w_ref[...]).astype(BF16)


def _inproj(x2d, wp, wdt):
    t = x2d.shape[0]
    return pl.pallas_call(
        _inproj_kernel,
        grid=(t // TM_IN, NP_TOTAL // TN_IN),
        in_specs=[
            pl.BlockSpec((TM_IN, D_MODEL), lambda i, j: (i, 0)),
            pl.BlockSpec((D_MODEL, TN_IN), lambda i, j: (0, j)),
            pl.BlockSpec((D_MODEL, LANES), lambda i, j: (0, 0)),
        ],
        out_specs=[
            pl.BlockSpec((TM_IN, TN_IN), lambda i, j: (i, j)),
            pl.BlockSpec((TM_IN, LANES), lambda i, j: (i, 0)),
        ],
        out_shape=[jax.ShapeDtypeStruct((t, NP_TOTAL), BF16), jax.ShapeDtypeStruct((t, LANES), F32)],
        scratch_shapes=[pltpu.VMEM((TM_IN, D_MODEL), BF16)],
        compiler_params=_cparams(("parallel", "arbitrary")),
        name="inproj",
    )(x2d, wp, wdt)


def _mm_kernel(a_ref, b_ref, o_ref):
    o_ref[...] = _dot(a_ref[...].astype(BF16), b_ref[...]).astype(o_ref.dtype)


def _mm(a, b, out_dtype, tm, tn):
    m, k = a.shape
    n = b.shape[1]
    return pl.pallas_call(
        _mm_kernel,
        grid=(m // tm, n // tn),
        in_specs=[pl.BlockSpec((tm, k), lambda i, j: (i, 0)), pl.BlockSpec((k, tn), lambda i, j: (0, j))],
        out_specs=pl.BlockSpec((tm, tn), lambda i, j: (i, j)),
        out_shape=jax.ShapeDtypeStruct((m, n), out_dtype),
        compiler_params=_cparams(("parallel", "parallel")),
        name="mm",
    )(a, b)


CONV_PAD = 8
Q = SSD_CHUNK


def _ssd_kernel(xs_ref, bc_ref, z_ref, dtr_ref, cwx_ref, cbx_ref, cwb_ref, cbb_ref, dtb_ref, alog_ref,
                dsk_ref, ng_ref, o_ref, hx_ref, hb_ref, st_ref, cumt_ref, dtt_ref, wt_ref, y_ref):
    c = pl.program_id(1)

    @pl.when(c == 0)
    def _():
        hx_ref[0:CONV_PAD, :] = jnp.zeros((CONV_PAD, SSD_D_INNER), F32)
        hb_ref[0:CONV_PAD, :] = jnp.zeros((CONV_PAD, BC_DIM), F32)
        st_ref[...] = jnp.zeros_like(st_ref)

    def conv_silu(in_ref, h_ref, w_ref, b_ref):
        cur = in_ref[...].astype(F32)
        h_ref[CONV_PAD:CONV_PAD + Q, :] = cur
        acc = b_ref[...]
        for k in range(SSD_CONV):
            off = CONV_PAD - (SSD_CONV - 1) + k
            acc = acc + w_ref[k:k + 1, :] * h_ref[off:off + Q, :]
        h_ref[0:CONV_PAD, :] = cur[Q - CONV_PAD:Q, :]
        return _silu(acc)

    xs_c = conv_silu(xs_ref, hx_ref, cwx_ref, cbx_ref)
    bc_c = conv_silu(bc_ref, hb_ref, cwb_ref, cbb_ref)

    lane = lax.broadcasted_iota(I32, (Q, LANES), 1)
    row = lax.broadcasted_iota(I32, (Q, LANES), 0)
    tril = row >= lane
    lo_mask = lane < SSD_HEAD_DIM

    dtv = dtr_ref[...] + dtb_ref[...]
    dt = jnp.maximum(dtv, 0.0) + jnp.log1p(jnp.exp(-jnp.abs(dtv)))
    a = jnp.where(lane[0:1, :] < SSD_HEADS, -jnp.exp(alog_ref[...]), 0.0)
    la = dt * a
    tri = jnp.where(tril, 1.0, 0.0).astype(BF16)
    p1 = la.astype(BF16)
    r1 = la - p1.astype(F32)
    p2 = r1.astype(BF16)
    p3 = (r1 - p2.astype(F32)).astype(BF16)
    cum = _dot(tri, p1) + _dot(tri, p2) + _dot(tri, p3)
    cum_t = cum.T
    dt_t = dt.T
    last = cum_t[:, Q - 1:Q]
    cumt_ref[...] = cum_t
    dtt_ref[...] = dt_t
    wt_ref[...] = jnp.exp(last - cum_t) * dt_t
    chunk_decay = jnp.exp(last)

    def head_parts(h, cb, cm, bm_t):
        col = jnp.broadcast_to(cum[:, h:h + 1], (Q, Q))
        rw = cumt_ref[h:h + 1, :]
        dec = jnp.exp(jnp.where(tril, col - rw, NEG_INF))
        m = cb * (dec * dtt_ref[h:h + 1, :])
        ce = cm * jnp.exp(col)
        sl = bm_t * wt_ref[h:h + 1, :]
        return m.astype(BF16), ce.astype(BF16), sl.astype(BF16)

    for g in range(SSD_GROUPS):
        bm = bc_c[:, g * SSD_STATE:(g + 1) * SSD_STATE]
        cm = bc_c[:, BC_DIM // 2 + g * SSD_STATE:BC_DIM // 2 + (g + 1) * SSD_STATE]
        cb = _dot_nt(cm.astype(BF16), bm.astype(BF16))
        bm_t = bm.T
        for qd in range(SSD_HEADS // SSD_GROUPS // 2):
            hp = g * (SSD_HEADS // SSD_GROUPS // 2) + qd
            ha, hb = 2 * hp, 2 * hp + 1
            sl_ = slice(hp * LANES, (hp + 1) * LANES)
            xs_pair = xs_c[:, sl_]
            x_lo = jnp.where(lo_mask, xs_pair, 0.0).astype(BF16)
            x_hi = jnp.where(lo_mask, 0.0, xs_pair).astype(BF16)
            h_prev = st_ref[:, sl_]
            h_lo = jnp.where(lo_mask, h_prev, 0.0).astype(BF16)
            h_hi = jnp.where(lo_mask, 0.0, h_prev).astype(BF16)
            m_a, ce_a, sl_a = head_parts(ha, cb, cm, bm_t)
            m_b, ce_b, sl_b = head_parts(hb, cb, cm, bm_t)
            lhs = jnp.concatenate([m_a, ce_a, m_b, ce_b], axis=1)
            rhs = jnp.concatenate([x_lo, h_lo, x_hi, h_hi], axis=0)
            y_ref[:, sl_] = _dot(lhs, rhs)
            new = _dot(jnp.concatenate([sl_a, sl_b], axis=1), jnp.concatenate([x_lo, x_hi], axis=0))
            cd = jnp.where(lo_mask[0:1, :], chunk_decay[ha:ha + 1, :], chunk_decay[hb:hb + 1, :])
            st_ref[:, sl_] = h_prev * cd + new

    y = y_ref[...] + xs_c * dsk_ref[...]
    y = y * _silu(z_ref[...].astype(F32))
    gw = SSD_D_INNER // SSD_GROUPS
    for g in range(SSD_GROUPS):
        yg = y[:, g * gw:(g + 1) * gw]
        ms = jnp.mean(yg * yg, axis=-1, keepdims=True)
        o_ref[:, g * gw:(g + 1) * gw] = ((yg * lax.rsqrt(ms + RMS_EPS)) * ng_ref[:, g * gw:(g + 1) * gw]).astype(BF16)


def _ssd(p, dt_raw, cwx, cbx, cwb, cbb, dtb, alog, dsk, ng, batch, seq):
    nc = seq // Q
    t = batch * seq
    rowblk = lambda b, c: b * nc + c
    const = lambda b, c: (0, 0)
    return pl.pallas_call(
        _ssd_kernel,
        grid=(batch, nc),
        in_specs=[
            pl.BlockSpec((Q, SSD_D_INNER), lambda b, c: (rowblk(b, c), OFF_XS // SSD_D_INNER)),
            pl.BlockSpec((Q, BC_DIM), lambda b, c: (rowblk(b, c), OFF_BC // BC_DIM)),
            pl.BlockSpec((Q, SSD_D_INNER), lambda b, c: (rowblk(b, c), OFF_Z // SSD_D_INNER)),
            pl.BlockSpec((Q, LANES), lambda b, c: (rowblk(b, c), 0)),
            pl.BlockSpec((SSD_CONV, SSD_D_INNER), const),
            pl.BlockSpec((1, SSD_D_INNER), const),
            pl.BlockSpec((SSD_CONV, BC_DIM), const),
            pl.BlockSpec((1, BC_DIM), const),
            pl.BlockSpec((1, LANES), const),
            pl.BlockSpec((1, LANES), const),
            pl.BlockSpec((1, SSD_D_INNER), const),
            pl.BlockSpec((1, SSD_D_INNER), const),
        ],
        out_specs=pl.BlockSpec((Q, SSD_D_INNER), lambda b, c: (rowblk(b, c), 0)),
        out_shape=jax.ShapeDtypeStruct((t, SSD_D_INNER), BF16),
        scratch_shapes=[
            pltpu.VMEM((CONV_PAD + Q, SSD_D_INNER), F32),
            pltpu.VMEM((CONV_PAD + Q, BC_DIM), F32),
            pltpu.VMEM((SSD_STATE, SSD_D_INNER), F32),
            pltpu.VMEM((LANES, Q), F32),
            pltpu.VMEM((LANES, Q), F32),
            pltpu.VMEM((LANES, Q), F32),
            pltpu.VMEM((Q, SSD_D_INNER), F32),
        ],
        compiler_params=_cparams(("parallel", "arbitrary")),
        name="ssd",
    )(p, p, p, dt_raw, cwx, cbx, cwb, cbb, dtb, alog, dsk, ng)


N_PAIRS = SWA_HEADS // 2


def _swa_kernel(sink_ref, q_ref, kp_ref, kc_ref, vp_ref, vc_ref, bias_ref, o_ref):
    n = pl.program_id(1)
    blk = SWA_BLOCK
    kband = jnp.concatenate([kp_ref[...], kc_ref[...]], axis=0)
    vband = jnp.concatenate([vp_ref[...], vc_ref[...]], axis=0)
    lane = lax.broadcasted_iota(I32, (1, LANES), 1)
    lo_f = jnp.where(lane < SWA_HEAD_DIM, 1.0, 0.0)
    lo_m = lo_f.astype(BF16)
    hi_m = (1.0 - lo_f).astype(BF16)
    colj = lax.broadcasted_iota(I32, (2 * blk, 2 * blk), 1)
    rowi = lax.broadcasted_iota(I32, (2 * blk, 1), 0)
    valid = (colj >= blk) | (n > 0)
    scale = SWA_HEAD_DIM ** -0.5
    for p in range(N_PAIRS):
        kv = p // (SWA_HEADS // SWA_KV_HEADS // 2)
        qp = q_ref[:, p * LANES:(p + 1) * LANES]
        qs = jnp.concatenate([qp * lo_m, qp * hi_m], axis=0)
        kd = kband[:, kv * LANES:(kv + 1) * LANES]
        s = _dot_nt(qs, kd) * scale + bias_ref[p]
        s = jnp.where(valid, s, NEG_INF)
        sink = jnp.where(rowi < blk, sink_ref[2 * p], sink_ref[2 * p + 1])
        m = jnp.maximum(jnp.max(s, axis=-1, keepdims=True), sink)
        e = jnp.exp(s - m)
        den = jnp.sum(e, axis=-1, keepdims=True) + jnp.exp(sink - m)
        pr = (e * (1.0 / den)).astype(BF16)
        vd = vband[:, kv * LANES:(kv + 1) * LANES]
        lhs = jnp.concatenate([pr[:blk], pr[blk:]], axis=1)
        rhs = jnp.concatenate([vd * lo_m, vd * hi_m], axis=0)
        o_ref[:, p * LANES:(p + 1) * LANES] = _dot(lhs, rhs).astype(BF16)


def _swa(p, sinks, bias, batch, seq):
    nb = seq // SWA_BLOCK
    t = batch * seq
    kw = 2 * SWA_KV_DIM
    cur = lambda b, n, s: b * nb + n
    prev = lambda b, n, s: b * nb + jnp.maximum(n - 1, 0)
    return pl.pallas_call(
        _swa_kernel,
        grid_spec=pltpu.PrefetchScalarGridSpec(
            num_scalar_prefetch=1,
            grid=(batch, nb),
            in_specs=[
                pl.BlockSpec((SWA_BLOCK, SWA_Q_DIM), lambda b, n, s: (cur(b, n, s), OFF_Q // SWA_Q_DIM)),
                pl.BlockSpec((SWA_BLOCK, kw), lambda b, n, s: (prev(b, n, s), OFF_K // kw)),
                pl.BlockSpec((SWA_BLOCK, kw), lambda b, n, s: (cur(b, n, s), OFF_K // kw)),
                pl.BlockSpec((SWA_BLOCK, kw), lambda b, n, s: (prev(b, n, s), OFF_V // kw)),
                pl.BlockSpec((SWA_BLOCK, kw), lambda b, n, s: (cur(b, n, s), OFF_V // kw)),
                pl.BlockSpec((N_PAIRS, 2 * SWA_BLOCK, 2 * SWA_BLOCK), lambda b, n, s: (0, 0, 0)),
            ],
            out_specs=pl.BlockSpec((SWA_BLOCK, SWA_Q_DIM), lambda b, n, s: (cur(b, n, s), 0)),
        ),
        out_shape=jax.ShapeDtypeStruct((t, SWA_Q_DIM), BF16),
        compiler_params=_cparams(("parallel", "arbitrary")),
        name="swa",
    )(sinks, p, p, p, p, p, bias)


def _t5_causal_bucket(dist):
    max_exact = REL_BUCKETS // 2
    large = max_exact + (jnp.log(jnp.maximum(dist, 1).astype(F32) / max_exact)
                         / math.log(REL_MAX_DIST / max_exact) * (REL_BUCKETS - max_exact)).astype(I32)
    large = jnp.minimum(large, REL_BUCKETS - 1)
    return jnp.where(dist < max_exact, dist, large)


def _swa_bias_table(rel_bias):
    qi = jnp.arange(SWA_BLOCK)[:, None]
    kj = jnp.arange(2 * SWA_BLOCK)[None, :]
    dist = qi + SWA_BLOCK - kj
    in_window = (dist >= 0) & (dist < SWA_WINDOW)
    bias = rel_bias.astype(F32)[_t5_causal_bucket(jnp.maximum(dist, 0))]
    bias = jnp.where(in_window[:, :, None], bias, NEG_INF).transpose(2, 0, 1)
    return bias.reshape(N_PAIRS, 2 * SWA_BLOCK, 2 * SWA_BLOCK)


TL_XA = 512


def _xa_kernel(q_ref, mk_ref, mv_ref, o_ref):
    s = _dot_nt(q_ref[...], mk_ref[...]) * (XA_HEAD_DIM ** -0.5)
    m = jnp.max(s, axis=-1, keepdims=True)
    e = jnp.exp(s - m)
    pr = e * (1.0 / jnp.sum(e, axis=-1, keepdims=True))
    o_ref[...] = _dot(pr.astype(BF16), mv_ref[...]).astype(BF16)


def _xa(p, memkv, batch, seq):
    nl = seq // TL_XA
    t = batch * seq
    return pl.pallas_call(
        _xa_kernel,
        grid=(batch, nl, XA_HEADS),
        in_specs=[
            pl.BlockSpec((TL_XA, XA_HEAD_DIM), lambda b, l, h: (b * nl + l, OFF_QX // XA_HEAD_DIM + h)),
            pl.BlockSpec((MEM_LEN, XA_HEAD_DIM), lambda b, l, h: (b, h)),
            pl.BlockSpec((MEM_LEN, XA_HEAD_DIM), lambda b, l, h: (b, XA_HEADS + h)),
        ],
        out_specs=pl.BlockSpec((TL_XA, XA_HEAD_DIM), lambda b, l, h: (b * nl + l, h)),
        out_shape=jax.ShapeDtypeStruct((t, XA_DIM), BF16),
        compiler_params=_cparams(("parallel", "parallel", "parallel")),
        name="xattn",
    )(p, memkv, memkv)


TM_MG = 1024
TN_MG = 512


def _merge_kernel(ys_ref, yw_ref, yx_ref, ws_ref, ww_ref, wx_ref, g0_ref, g1_ref, g2_ref, o_ref):
    a = _dot(ys_ref[...], ws_ref[...])
    b = _dot(yw_ref[...], ww_ref[...])
    c = _dot(yx_ref[...], wx_ref[...])
    g0 = _sigmoid(g0_ref[...].astype(F32))
    g1 = _sigmoid(g1_ref[...].astype(F32))
    g2 = _sigmoid(g2_ref[...].astype(F32))
    o_ref[...] = (g0 * a + g1 * b + g2 * c).astype(BF16)


def _merge(y_ssd, y_swa, y_xa, w_ssd_o, w_swa_o, w_xa_o, p):
    t = y_ssd.shape[0]
    gpb = D_MODEL // TN_MG
    gate = lambda k: pl.BlockSpec((TM_MG, TN_MG), lambda i, j: (i, OFF_GATES // TN_MG + k * gpb + j))
    return pl.pallas_call(
        _merge_kernel,
        grid=(t // TM_MG, D_MODEL // TN_MG),
        in_specs=[
            pl.BlockSpec((TM_MG, SSD_D_INNER), lambda i, j: (i, 0)),
            pl.BlockSpec((TM_MG, SWA_Q_DIM), lambda i, j: (i, 0)),
            pl.BlockSpec((TM_MG, XA_DIM), lambda i, j: (i, 0)),
            pl.BlockSpec((SSD_D_INNER, TN_MG), lambda i, j: (0, j)),
            pl.BlockSpec((SWA_Q_DIM, TN_MG), lambda i, j: (0, j)),
            pl.BlockSpec((XA_DIM, TN_MG), lambda i, j: (0, j)),
            gate(0), gate(1), gate(2),
        ],
        out_specs=pl.BlockSpec((TM_MG, TN_MG), lambda i, j: (i, j)),
        out_shape=jax.ShapeDtypeStruct((t, D_MODEL), BF16),
        compiler_params=_cparams(("parallel", "parallel")),
        name="merge",
    )(y_ssd, y_swa, y_xa, w_ssd_o, w_swa_o, w_xa_o, p, p, p)


TM_LN = 512


def _layer_norm(v, g, b):
    mu = jnp.mean(v, axis=-1, keepdims=True)
    d = v - mu
    var = jnp.mean(d * d, axis=-1, keepdims=True)
    return (d * lax.rsqrt(var + LN_EPS)) * g + b


def _pack_halves(v):
    n = v.shape[1] // 2
    lo = lax.bitcast_convert_type(v[:, :n].astype(BF16).astype(F32), U32) >> 16
    hi = lax.bitcast_convert_type(v[:, n:].astype(BF16).astype(F32), U32) & jnp.uint32(0xFFFF0000)
    return hi | lo


def _unpack_halves(w):
    lo = lax.bitcast_convert_type(w << 16, F32)
    hi = lax.bitcast_convert_type(w & jnp.uint32(0xFFFF0000), F32)
    return lo, hi


def _out_ln_kernel(alpha, m_ref, w_ref, x_ref, g_ref, b_ref, h_ref, hp_ref):
    mix = _dot(m_ref[...], w_ref[...])
    hn = _layer_norm(alpha * x_ref[...] + mix, g_ref[...], b_ref[...])
    h_ref[...] = hn
    hp_ref[...] = _pack_halves(hn)


def _out_ln(merged, w_out, x2d, g, b, alpha):
    t = x2d.shape[0]
    row = lambda i: (i, 0)
    const = lambda i: (0, 0)
    return pl.pallas_call(
        functools.partial(_out_ln_kernel, alpha),
        grid=(t // TM_LN,),
        in_specs=[
            pl.BlockSpec((TM_LN, D_MODEL), row),
            pl.BlockSpec((D_MODEL, D_MODEL), const),
            pl.BlockSpec((TM_LN, D_MODEL), row),
            pl.BlockSpec((1, D_MODEL), const),
            pl.BlockSpec((1, D_MODEL), const),
        ],
        out_specs=[pl.BlockSpec((TM_LN, D_MODEL), row), pl.BlockSpec((TM_LN, D_MODEL // 2), row)],
        out_shape=[jax.ShapeDtypeStruct((t, D_MODEL), F32), jax.ShapeDtypeStruct((t, D_MODEL // 2), U32)],
        compiler_params=_cparams(("parallel",)),
        name="out_ln1",
    )(merged, w_out, x2d, g, b)


TM_RT = 512
EPG = N_EXPERTS // N_EXPERT_GROUPS


def _router_kernel(h_ref, whi_ref, wlo_ref, bias_ref, su_ref, idx_ref, w_ref, rank_ref, cnt_ref, run_ref):
    @pl.when(pl.program_id(0) == 0)
    def _():
        run_ref[...] = jnp.zeros_like(run_ref)

    h = h_ref[...]
    h_hi = h.astype(BF16)
    h_lo = (h - h_hi.astype(F32)).astype(BF16)
    whi = whi_ref[...]
    logits = _dot_nt(whi, h_hi) + _dot_nt(whi, h_lo) + _dot_nt(wlo_ref[...], h_hi)
    scores = _sigmoid(logits)
    sel = scores + bias_ref[...]

    iota_m = lax.broadcasted_iota(I32, (EPG, TM_RT), 0).astype(F32)
    iota_g = lax.broadcasted_iota(I32, (N_EXPERT_GROUPS, TM_RT), 0).astype(F32)
    gs = []
    for g in range(N_EXPERT_GROUPS):
        v = sel[g * EPG:(g + 1) * EPG, :]
        m1 = jnp.max(v, axis=0, keepdims=True)
        i1 = jnp.min(jnp.where(v == m1, iota_m, float(EPG)), axis=0, keepdims=True)
        m2 = jnp.max(jnp.where(iota_m == i1, NEG_INF, v), axis=0, keepdims=True)
        gs.append(m1 + m2)
    gsc = jnp.concatenate(gs, axis=0)
    chosen = jnp.zeros((N_EXPERT_GROUPS, TM_RT), F32)
    for _ in range(TOPK_GROUPS):
        m = jnp.max(gsc, axis=0, keepdims=True)
        ig = jnp.min(jnp.where(gsc == m, iota_g, float(N_EXPERT_GROUPS)), axis=0, keepdims=True)
        hit = iota_g == ig
        chosen = jnp.where(hit, 1.0, chosen)
        gsc = jnp.where(hit, NEG_INF, gsc)
    gmask = jnp.concatenate(
        [jnp.broadcast_to(chosen[g:g + 1, :], (EPG, TM_RT)) for g in range(N_EXPERT_GROUPS)], axis=0)
    masked = jnp.where(gmask > 0.0, sel, NEG_INF)

    iota_e = lax.broadcasted_iota(I32, (N_EXPERTS, TM_RT), 0).astype(F32)
    hits, idxs, ws = [], [], []
    for _ in range(TOP_K):
        m = jnp.max(masked, axis=0, keepdims=True)
        ie = jnp.min(jnp.where(masked == m, iota_e, float(N_EXPERTS)), axis=0, keepdims=True)
        hit = iota_e == ie
        hits.append(hit)
        idxs.append(ie)
        ws.append(jnp.sum(jnp.where(hit, scores, 0.0), axis=0, keepdims=True))
        masked = jnp.where(hit, NEG_INF, masked)
    wsum = ws[0]
    for k in range(1, TOP_K):
        wsum = wsum + ws[k]
    inv = 1.0 / wsum
    idx_ref[...] = jnp.concatenate(idxs, axis=0).astype(I32)
    w_ref[...] = jnp.concatenate([w * inv * ROUTED_SCALE for w in ws], axis=0)

    assign = jnp.zeros((N_EXPERTS, TM_RT), F32)
    for hit in hits:
        assign = jnp.where(hit, 1.0, assign)
    pos = _dot(assign.astype(BF16), su_ref[...]) + run_ref[...]
    rank_ref[...] = jnp.concatenate(
        [jnp.sum(jnp.where(hit, pos, 0.0), axis=0, keepdims=True) for hit in hits], axis=0).astype(I32)
    run = run_ref[...] + jnp.sum(assign, axis=1, keepdims=True)
    run_ref[...] = run
    cnt_ref[...] = run


def _router(h1, r_hi, r_lo, rbias, su):
    t = h1.shape[0]
    tile = lambda i: (0, i)
    const = lambda i: (0, 0)
    return pl.pallas_call(
        _router_kernel,
        grid=(t // TM_RT,),
        in_specs=[
            pl.BlockSpec((TM_RT, D_MODEL), lambda i: (i, 0)),
            pl.BlockSpec((N_EXPERTS, D_MODEL), const),
            pl.BlockSpec((N_EXPERTS, D_MODEL), const),
            pl.BlockSpec((N_EXPERTS, 1), const),
            pl.BlockSpec((TM_RT, TM_RT), const),
        ],
        out_specs=[
            pl.BlockSpec((TOP_K, TM_RT), tile),
            pl.BlockSpec((TOP_K, TM_RT), tile),
            pl.BlockSpec((TOP_K, TM_RT), tile),
            pl.BlockSpec((N_EXPERTS, TM_RT), const),
        ],
        out_shape=[
            jax.ShapeDtypeStruct((TOP_K, t), I32),
            jax.ShapeDtypeStruct((TOP_K, t), F32),
            jax.ShapeDtypeStruct((TOP_K, t), I32),
            jax.ShapeDtypeStruct((N_EXPERTS, TM_RT), F32),
        ],
        scratch_shapes=[pltpu.VMEM((N_EXPERTS, TM_RT), F32)],
        compiler_params=_cparams(("arbitrary",)),
        name="router",
    )(h1, r_hi, r_lo, rbias, su)


TM_DS = 2048


def _dest_kernel(idx_ref, rank_ref, ps_ref, d_ref):
    iota_e = lax.broadcasted_iota(I32, (N_EXPERTS, TM_DS), 0)
    ps = ps_ref[...]
    rows = []
    for k in range(TOP_K):
        hit = iota_e == idx_ref[k:k + 1, :]
        rows.append(jnp.sum(jnp.where(hit, ps, 0.0), axis=0, keepdims=True))
    d_ref[...] = jnp.concatenate(rows, axis=0).astype(I32) + rank_ref[...]


def _dest(idx_t, rank_t, pstart):
    t = idx_t.shape[1]
    tile = lambda i: (0, i)
    return pl.pallas_call(
        _dest_kernel,
        grid=(t // TM_DS,),
        in_specs=[pl.BlockSpec((TOP_K, TM_DS), tile), pl.BlockSpec((TOP_K, TM_DS), tile),
                  pl.BlockSpec((N_EXPERTS, 1), lambda i: (0, 0))],
        out_specs=pl.BlockSpec((TOP_K, TM_DS), tile),
        out_shape=jax.ShapeDtypeStruct((TOP_K, t), I32),
        compiler_params=_cparams(("parallel",)),
        name="dest",
    )(idx_t, rank_t, pstart)


TM_DP = 512
ROW_WORDS = D_MODEL // 2


def _dispatch_kernel(dest_ref, hp_ref, xs_ref, sem):
    t_total = pl.num_programs(0) * TM_DP
    base = pl.program_id(0) * TM_DP

    def issue(t, carry):
        for k in range(TOP_K):
            slot = dest_ref[k * t_total + base + t]
            pltpu.make_async_copy(hp_ref.at[pl.ds(t, 1), :], xs_ref.at[pl.ds(slot, 1), :], sem).start()
        return carry

    lax.fori_loop(0, TM_DP, issue, 0)
    for k in range(TOP_K):
        pltpu.make_async_copy(hp_ref, xs_ref.at[pl.ds(0, TM_DP), :], sem).wait()


def _dispatch(dest_flat, hp, n_slots):
    t = hp.shape[0]
    return pl.pallas_call(
        _dispatch_kernel,
        grid_spec=pltpu.PrefetchScalarGridSpec(
            num_scalar_prefetch=1,
            grid=(t // TM_DP,),
            in_specs=[pl.BlockSpec((TM_DP, ROW_WORDS), lambda i, d: (i, 0))],
            out_specs=pl.BlockSpec(memory_space=pl.ANY),
            scratch_shapes=[pltpu.SemaphoreType.DMA],
        ),
        out_shape=jax.ShapeDtypeStruct((n_slots, ROW_WORDS), U32),
        compiler_params=_cparams(("arbitrary",)),
        name="dispatch",
    )(dest_flat, hp)


def _expert_kernel(be_ref, nv_ref, x_ref, w1_ref, w3_ref, w2_ref, o_ref, w1b, w3b, w2b):
    b = pl.program_id(0)
    e = be_ref[b]
    prev = be_ref[jnp.maximum(b - 1, 0)]
    nv = nv_ref[b]

    @pl.when((b == 0) | (e != prev))
    def _():
        w1b[...] = w1_ref[0].astype(BF16)
        w3b[...] = w3_ref[0].astype(BF16)
        w2b[...] = w2_ref[0].astype(BF16)

    @pl.when(nv > 0)
    def _():
        live = lax.broadcasted_iota(I32, (MOE_BLK, 1), 0) < nv
        lo, hi = _unpack_halves(jnp.where(live, x_ref[...], jnp.uint32(0)))
        x = jnp.concatenate([lo.astype(BF16), hi.astype(BF16)], axis=1)
        hdn = _silu(_dot(x, w1b[...])) * _dot(x, w3b[...])
        o_ref[...] = _pack_halves(_dot(hdn.astype(BF16), w2b[...]))

    @pl.when(nv <= 0)
    def _():
        o_ref[...] = jnp.zeros_like(o_ref)


def _experts(block_e, block_nv, x_sorted, w1, w3, w2):
    n_slots = x_sorted.shape[0]
    n_blocks = n_slots // MOE_BLK
    return pl.pallas_call(
        _expert_kernel,
        grid_spec=pltpu.PrefetchScalarGridSpec(
            num_scalar_prefetch=2,
            grid=(n_blocks,),
            in_specs=[
                pl.BlockSpec((MOE_BLK, ROW_WORDS), lambda b, be, nv: (b, 0)),
                pl.BlockSpec((1, D_MODEL, EXPERT_DIM), lambda b, be, nv: (be[b], 0, 0)),
                pl.BlockSpec((1, D_MODEL, EXPERT_DIM), lambda b, be, nv: (be[b], 0, 0)),
                pl.BlockSpec((1, EXPERT_DIM, D_MODEL), lambda b, be, nv: (be[b], 0, 0)),
            ],
            out_specs=pl.BlockSpec((MOE_BLK, ROW_WORDS), lambda b, be, nv: (b, 0)),
            scratch_shapes=[
                pltpu.VMEM((D_MODEL, EXPERT_DIM), BF16),
                pltpu.VMEM((D_MODEL, EXPERT_DIM), BF16),
                pltpu.VMEM((EXPERT_DIM, D_MODEL), BF16),
            ],
        ),
        out_shape=jax.ShapeDtypeStruct((n_slots, ROW_WORDS), U32),
        compiler_params=_cparams(("arbitrary",), vmem_mb=56),
        name="experts",
    )(block_e, block_nv, x_sorted, w1, w3, w2)


TM_FN = 256


def _final_kernel(alpha, dest_ref, ys_ref, w_ref, h_ref, ws1_ref, ws3_ref, ws2_ref, g_ref, b_ref, o_ref, buf, sem):
    t_total = pl.num_programs(0) * TM_FN
    base = pl.program_id(0) * TM_FN

    def issue(t, carry):
        for k in range(TOP_K):
            slot = dest_ref[k * t_total + base + t]
            pltpu.make_async_copy(ys_ref.at[pl.ds(slot, 1), :], buf.at[k, pl.ds(t, 1), :], sem).start()
        return carry

    lax.fori_loop(0, TM_FN, issue, 0)

    h = h_ref[...]
    xb = h.astype(BF16)
    hdn = _silu(_dot(xb, ws1_ref[...])) * _dot(xb, ws3_ref[...])
    shared = _dot(hdn.astype(BF16), ws2_ref[...])

    for k in range(TOP_K):
        pltpu.make_async_copy(ys_ref.at[pl.ds(0, TM_FN), :], buf.at[k], sem).wait()
    half = D_MODEL // 2
    acc_lo = shared[:, :half]
    acc_hi = shared[:, half:]
    for k in range(TOP_K):
        lo, hi = _unpack_halves(buf[k])
        wk = w_ref[:, k:k + 1]
        acc_lo = acc_lo + wk * lo
        acc_hi = acc_hi + wk * hi
    ffn = jnp.concatenate([acc_lo, acc_hi], axis=1)
    o_ref[...] = _layer_norm(alpha * h + ffn, g_ref[...], b_ref[...])


def _final(dest_flat, ys, w_tok, h1, ws1, ws3, ws2, g, b, alpha):
    t = h1.shape[0]
    row = lambda i, d: (i, 0)
    const = lambda i, d: (0, 0)
    return pl.pallas_call(
        functools.partial(_final_kernel, alpha),
        grid_spec=pltpu.PrefetchScalarGridSpec(
            num_scalar_prefetch=1,
            grid=(t // TM_FN,),
            in_specs=[
                pl.BlockSpec(memory_space=pl.ANY),
                pl.BlockSpec((TM_FN, TOP_K), row),
                pl.BlockSpec((TM_FN, D_MODEL), row),
                pl.BlockSpec((D_MODEL, SHARED_DIM), const),
                pl.BlockSpec((D_MODEL, SHARED_DIM), const),
                pl.BlockSpec((SHARED_DIM, D_MODEL), const),
                pl.BlockSpec((1, D_MODEL), const),
                pl.BlockSpec((1, D_MODEL), const),
            ],
            out_specs=pl.BlockSpec((TM_FN, D_MODEL), row),
            scratch_shapes=[pltpu.VMEM((TOP_K, TM_FN, ROW_WORDS), U32), pltpu.SemaphoreType.DMA],
        ),
        out_shape=jax.ShapeDtypeStruct((t, D_MODEL), F32),
        compiler_params=_cparams(("arbitrary",)),
        name="combine_ln2",
    )(dest_flat, ys, w_tok, h1, ws1, ws3, ws2, g, b)


def _dup_heads(w, heads, dim):
    d = w.shape[0]
    w = w.reshape(d, heads, 1, dim)
    return jnp.broadcast_to(w, (d, heads, 2, dim)).reshape(d, heads * 2 * dim)


def _prep_w_in(w_in):
    sizes = (SSD_D_INNER, SSD_D_INNER + BC_DIM, SSD_HEADS, SWA_Q_DIM, SWA_KV_DIM, SWA_KV_DIM, XA_DIM,
             N_BRANCH * D_MODEL)
    pts = []
    acc = 0
    for s in sizes[:-1]:
        acc += s
        pts.append(acc)
    z, xbc, dt, q_s, k_s, v_s, q_x, gates = jnp.split(w_in, pts, axis=1)
    wp = jnp.concatenate(
        [gates, z, xbc, q_s, _dup_heads(k_s, SWA_KV_HEADS, SWA_HEAD_DIM),
         _dup_heads(v_s, SWA_KV_HEADS, SWA_HEAD_DIM), q_x], axis=1).astype(BF16)
    wdt = jnp.pad(dt, ((0, 0), (0, LANES - SSD_HEADS))).astype(BF16)
    return wp, wdt


def _pad_lanes(v):
    return jnp.pad(v, (0, LANES - v.shape[0]))[None, :].astype(F32)


def _layer(h2d, mem2d, batch, seq, w_in, conv_w, conv_b, dt_bias, a_log, d_skip, ssd_norm_g, swa_sinks, rel_bias,
           w_mem_kv, w_ssd_o, w_swa_o, w_xa_o, w_out, ln1_g, ln1_b, router_w, router_bias, w1, w3, w2, ws1, ws3,
           ws2, ln2_g, ln2_b, alpha):
    t = batch * seq
    wp, wdt = _prep_w_in(w_in)
    p, dt_raw = _inproj(h2d, wp, wdt)

    y_ssd = _ssd(
        p, dt_raw,
        conv_w[:, :SSD_D_INNER], conv_b[None, :SSD_D_INNER], conv_w[:, SSD_D_INNER:], conv_b[None, SSD_D_INNER:],
        _pad_lanes(dt_bias), _pad_lanes(a_log),
        jnp.repeat(d_skip, SSD_HEAD_DIM)[None, :], ssd_norm_g[None, :], batch, seq)

    y_swa = _swa(p, swa_sinks.astype(F32), _swa_bias_table(rel_bias), batch, seq)

    memkv = _mm(mem2d, w_mem_kv.astype(BF16), BF16, batch * MEM_LEN, 1024)
    y_xa = _xa(p, memkv, batch, seq)

    merged = _merge(y_ssd, y_swa, y_xa, w_ssd_o.astype(BF16), w_swa_o.astype(BF16), w_xa_o.astype(BF16), p)
    h1, hp = _out_ln(merged, w_out.astype(BF16), h2d, ln1_g[None, :], ln1_b[None, :], alpha)

    rwt = router_w.T
    r_hi = rwt.astype(BF16)
    r_lo = (rwt - r_hi.astype(F32)).astype(BF16)
    ti = jnp.arange(TM_RT)
    su = (ti[:, None] < ti[None, :]).astype(BF16)
    idx_t, w_t, rank_t, cnt = _router(h1, r_hi, r_lo, router_bias[:, None].astype(F32), su)

    counts = cnt[:, 0].astype(I32)
    padded = (counts + MOE_BLK - 1) // MOE_BLK * MOE_BLK
    pend = jnp.cumsum(padded)
    pstart = pend - padded
    n_blocks = t * TOP_K // MOE_BLK + N_EXPERTS
    blk_lo = jnp.arange(n_blocks, dtype=I32) * MOE_BLK
    block_e = jnp.minimum(jnp.sum((pend[None, :] <= blk_lo[:, None]).astype(I32), axis=1), N_EXPERTS - 1)
    of_e = block_e[:, None] == jnp.arange(N_EXPERTS, dtype=I32)[None, :]
    valid_end = jnp.sum(jnp.where(of_e, (pstart + counts)[None, :], 0), axis=1)
    block_nv = jnp.clip(valid_end - blk_lo, 0, MOE_BLK).astype(I32)

    dest_flat = _dest(idx_t, rank_t, pstart.astype(F32)[:, None]).reshape(-1)
    x_sorted = _dispatch(dest_flat, hp, n_blocks * MOE_BLK)
    ys = _experts(block_e, block_nv, x_sorted, w1, w3, w2)
    return _final(dest_flat, ys, w_t.T, h1, ws1.astype(BF16), ws3.astype(BF16), ws2.astype(BF16), ln2_g[None, :],
                  ln2_b[None, :], alpha)


def kernel(x, mem, w_in, conv_w, conv_b, dt_bias, a_log, d_skip, ssd_norm_g, swa_sinks, rel_bias, w_mem_kv, w_ssd_o, w_swa_o, w_xa_o, w_out, ln1_g, ln1_b, router_w, router_bias, w1, w3, w2, ws1, ws3, ws2, ln2_g, ln2_b):
    batch, seq, d = x.shape
    depth = w_in.shape[0]
    alpha = (2.0 * depth) ** 0.25
    h = x.reshape(batch * seq, d)
    mem2d = mem.reshape(batch * MEM_LEN, d)
    for i in range(depth):
        h = _layer(h, mem2d, batch, seq, w_in[i], conv_w[i], conv_b[i], dt_bias[i], a_log[i], d_skip[i],
                   ssd_norm_g[i], swa_sinks[i], rel_bias, w_mem_kv[i], w_ssd_o[i], w_swa_o[i], w_xa_o[i], w_out[i],
                   ln1_g[i], ln1_b[i], router_w[i], router_bias[i], w1[i], w3[i], w2[i], ws1[i], ws3[i], ws2[i],
                   ln2_g[i], ln2_b[i], alpha)
    return h.reshape(batch, seq, d)
```

```python
import functools
import math

import jax
import jax.numpy as jnp
from jax import lax
from jax.experimental import pallas as pl
from jax.experimental.pallas import tpu as pltpu

F32 = jnp.float32
BF16 = jnp.bfloat16
I32 = jnp.int32
U32 = jnp.uint32

D_MODEL = 2048
MEM_LEN = 256
SSD_D_INNER = D_MODEL
SSD_HEAD_DIM = 64
SSD_HEADS = 32
SSD_GROUPS = 4
SSD_STATE = 128
SSD_CONV = 4
SSD_CHUNK = 128
SWA_HEADS = 16
SWA_KV_HEADS = 4
SWA_HEAD_DIM = 64
SWA_WINDOW = 128
SWA_BLOCK = 128
REL_BUCKETS = 32
REL_MAX_DIST = 128
XA_HEADS = 4
XA_HEAD_DIM = 256
N_BRANCH = 3
N_EXPERTS = 64
TOP_K = 8
N_EXPERT_GROUPS = 8
TOPK_GROUPS = 4
EXPERT_DIM = 512
SHARED_DIM = 512
ROUTED_SCALE = 2.5
LN_EPS = 1e-5
RMS_EPS = 1e-5

SWA_Q_DIM = SWA_HEADS * SWA_HEAD_DIM
SWA_KV_DIM = SWA_KV_HEADS * SWA_HEAD_DIM
XA_DIM = XA_HEADS * XA_HEAD_DIM
BC_DIM = 2 * SSD_GROUPS * SSD_STATE

OFF_GATES = 0
OFF_Z = OFF_GATES + N_BRANCH * D_MODEL
OFF_XS = OFF_Z + SSD_D_INNER
OFF_BC = OFF_XS + SSD_D_INNER
OFF_Q = OFF_BC + BC_DIM
OFF_K = OFF_Q + SWA_Q_DIM
OFF_V = OFF_K + 2 * SWA_KV_DIM
OFF_QX = OFF_V + 2 * SWA_KV_DIM
NP_TOTAL = OFF_QX + XA_DIM

LANES = 128
MOE_BLK = 256
NEG_INF = float("-inf")


def _cparams(sem, vmem_mb=48):
    return pltpu.CompilerParams(dimension_semantics=sem, vmem_limit_bytes=vmem_mb * 1024 * 1024)


def _dot(a, b):
    return jnp.dot(a, b, preferred_element_type=F32)


def _dot_nt(a, b):
    return lax.dot_general(a, b, (((1,), (1,)), ((), ())), preferred_element_type=F32)


def _sigmoid(x):
    return 1.0 / (1.0 + jnp.exp(-x))


def _silu(x):
    return x * _sigmoid(x)


TM_IN = 1024
TN_IN = 1024


def _inproj_kernel(x_ref, w_ref, wdt_ref, p_ref, dt_ref, xb_ref):
    @pl.when(pl.program_id(1) == 0)
    def _():
        xb = x_ref[...].astype(BF16)
        xb_ref[...] = xb
        dt_ref[...] = _dot(xb, wdt_ref[...])

    p_ref[...] = _dot(xb_ref[...], w_ref[...]).astype(BF16)


def _inproj(x2d, wp, wdt):
    t = x2d.shape[0]
    return pl.pallas_call(
        _inproj_kernel,
        grid=(t // TM_IN, NP_TOTAL // TN_IN),
        in_specs=[
            pl.BlockSpec((TM_IN, D_MODEL), lambda i, j: (i, 0)),
            pl.BlockSpec((D_MODEL, TN_IN), lambda i, j: (0, j)),
            pl.BlockSpec((D_MODEL, LANES), lambda i, j: (0, 0)),
        ],
        out_specs=[
            pl.BlockSpec((TM_IN, TN_IN), lambda i, j: (i, j)),
            pl.BlockSpec((TM_IN, LANES), lambda i, j: (i, 0)),
        ],
        out_shape=[jax.ShapeDtypeStruct((t, NP_TOTAL), BF16), jax.ShapeDtypeStruct((t, LANES), F32)],
        scratch_shapes=[pltpu.VMEM((TM_IN, D_MODEL), BF16)],
        compiler_params=_cparams(("parallel", "arbitrary")),
        name="inproj",
    )(x2d, wp, wdt)


def _mm_kernel(a_ref, b_ref, o_ref):
    o_ref[...] = _dot(a_ref[...].astype(BF16), b_ref[...]).astype(o_ref.dtype)


def _mm(a, b, out_dtype, tm, tn):
    m, k = a.shape
    n = b.shape[1]
    return pl.pallas_call(
        _mm_kernel,
        grid=(m // tm, n // tn),
        in_specs=[pl.BlockSpec((tm, k), lambda i, j: (i, 0)), pl.BlockSpec((k, tn), lambda i, j: (0, j))],
        out_specs=pl.BlockSpec((tm, tn), lambda i, j: (i, j)),
        out_shape=jax.ShapeDtypeStruct((m, n), out_dtype),
        compiler_params=_cparams(("parallel", "parallel")),
        name="mm",
    )(a, b)


CONV_PAD = 8
Q = SSD_CHUNK


def _ssd_kernel(xs_ref, bc_ref, z_ref, dtr_ref, cwx_ref, cbx_ref, cwb_ref, cbb_ref, dtb_ref, alog_ref,
                dsk_ref, ng_ref, o_ref, hx_ref, hb_ref, st_ref, cumt_ref, dtt_ref, wt_ref, y_ref):
    c = pl.program_id(1)

    @pl.when(c == 0)
    def _():
        hx_ref[0:CONV_PAD, :] = jnp.zeros((CONV_PAD, SSD_D_INNER), F32)
        hb_ref[0:CONV_PAD, :] = jnp.zeros((CONV_PAD, BC_DIM), F32)
        st_ref[...] = jnp.zeros_like(st_ref)

    def conv_silu(in_ref, h_ref, w_ref, b_ref):
        cur = in_ref[...].astype(F32)
        h_ref[CONV_PAD:CONV_PAD + Q, :] = cur
        acc = b_ref[...]
        for k in range(SSD_CONV):
            off = CONV_PAD - (SSD_CONV - 1) + k
            acc = acc + w_ref[k:k + 1, :] * h_ref[off:off + Q, :]
        h_ref[0:CONV_PAD, :] = cur[Q - CONV_PAD:Q, :]
        return _silu(acc)

    xs_c = conv_silu(xs_ref, hx_ref, cwx_ref, cbx_ref)
    bc_c = conv_silu(bc_ref, hb_ref, cwb_ref, cbb_ref)

    lane = lax.broadcasted_iota(I32, (Q, LANES), 1)
    row = lax.broadcasted_iota(I32, (Q, LANES), 0)
    tril = row >= lane
    lo_mask = lane < SSD_HEAD_DIM

    dtv = dtr_ref[...] + dtb_ref[...]
    dt = jnp.maximum(dtv, 0.0) + jnp.log1p(jnp.exp(-jnp.abs(dtv)))
    a = jnp.where(lane[0:1, :] < SSD_HEADS, -jnp.exp(alog_ref[...]), 0.0)
    la = dt * a
    tri = jnp.where(tril, 1.0, 0.0).astype(BF16)
    p1 = la.astype(BF16)
    r1 = la - p1.astype(F32)
    p2 = r1.astype(BF16)
    p3 = (r1 - p2.astype(F32)).astype(BF16)
    cum = _dot(tri, p1) + _dot(tri, p2) + _dot(tri, p3)
    cum_t = cum.T
    dt_t = dt.T
    last = cum_t[:, Q - 1:Q]
    cumt_ref[...] = cum_t
    dtt_ref[...] = dt_t
    wt_ref[...] = jnp.exp(last - cum_t) * dt_t
    chunk_decay = jnp.exp(last)

    def head_parts(h, cb, cm, bm_t):
        col = jnp.broadcast_to(cum[:, h:h + 1], (Q, Q))
        rw = cumt_ref[h:h + 1, :]
        dec = jnp.exp(jnp.where(tril, col - rw, NEG_INF))
        m = cb * (dec * dtt_ref[h:h + 1, :])
        ce = cm * jnp.exp(col)
        sl = bm_t * wt_ref[h:h + 1, :]
        return m.astype(BF16), ce.astype(BF16), sl.astype(BF16)

    for g in range(SSD_GROUPS):
        bm = bc_c[:, g * SSD_STATE:(g + 1) * SSD_STATE]
        cm = bc_c[:, BC_DIM // 2 + g * SSD_STATE:BC_DIM // 2 + (g + 1) * SSD_STATE]
        cb = _dot_nt(cm.astype(BF16), bm.astype(BF16))
        bm_t = bm.T
        for qd in range(SSD_HEADS // SSD_GROUPS // 2):
            hp = g * (SSD_HEADS // SSD_GROUPS // 2) + qd
            ha, hb = 2 * hp, 2 * hp + 1
            sl_ = slice(hp * LANES, (hp + 1) * LANES)
            xs_pair = xs_c[:, sl_]
            x_lo = jnp.where(lo_mask, xs_pair, 0.0).astype(BF16)
            x_hi = jnp.where(lo_mask, 0.0, xs_pair).astype(BF16)
            h_prev = st_ref[:, sl_]
            h_lo = jnp.where(lo_mask, h_prev, 0.0).astype(BF16)
            h_hi = jnp.where(lo_mask, 0.0, h_prev).astype(BF16)
            m_a, ce_a, sl_a = head_parts(ha, cb, cm, bm_t)
            m_b, ce_b, sl_b = head_parts(hb, cb, cm, bm_t)
            lhs = jnp.concatenate([m_a, ce_a, m_b, ce_b], axis=1)
            rhs = jnp.concatenate([x_lo, h_lo, x_hi, h_hi], axis=0)
            y_ref[:, sl_] = _dot(lhs, rhs)
            new = _dot(jnp.concatenate([sl_a, sl_b], axis=1), jnp.concatenate([x_lo, x_hi], axis=0))
            cd = jnp.where(lo_mask[0:1, :], chunk_decay[ha:ha + 1, :], chunk_decay[hb:hb + 1, :])
            st_ref[:, sl_] = h_prev * cd + new

    y = y_ref[...] + xs_c * dsk_ref[...]
    y = y * _silu(z_ref[...].astype(F32))
    gw = SSD_D_INNER // SSD_GROUPS
    for g in range(SSD_GROUPS):
        yg = y[:, g * gw:(g + 1) * gw]
        ms = jnp.mean(yg * yg, axis=-1, keepdims=True)
        o_ref[:, g * gw:(g + 1) * gw] = ((yg * lax.rsqrt(ms + RMS_EPS)) * ng_ref[:, g * gw:(g + 1) * gw]).astype(BF16)


def _ssd(p, dt_raw, cwx, cbx, cwb, cbb, dtb, alog, dsk, ng, batch, seq):
    nc = seq // Q
    t = batch * seq
    rowblk = lambda b, c: b * nc + c
    const = lambda b, c: (0, 0)
    return pl.pallas_call(
        _ssd_kernel,
        grid=(batch, nc),
        in_specs=[
            pl.BlockSpec((Q, SSD_D_INNER), lambda b, c: (rowblk(b, c), OFF_XS // SSD_D_INNER)),
            pl.BlockSpec((Q, BC_DIM), lambda b, c: (rowblk(b, c), OFF_BC // BC_DIM)),
            pl.BlockSpec((Q, SSD_D_INNER), lambda b, c: (rowblk(b, c), OFF_Z // SSD_D_INNER)),
            pl.BlockSpec((Q, LANES), lambda b, c: (rowblk(b, c), 0)),
            pl.BlockSpec((SSD_CONV, SSD_D_INNER), const),
            pl.BlockSpec((1, SSD_D_INNER), const),
            pl.BlockSpec((SSD_CONV, BC_DIM), const),
            pl.BlockSpec((1, BC_DIM), const),
            pl.BlockSpec((1, LANES), const),
            pl.BlockSpec((1, LANES), const),
            pl.BlockSpec((1, SSD_D_INNER), const),
            pl.BlockSpec((1, SSD_D_INNER), const),
        ],
        out_specs=pl.BlockSpec((Q, SSD_D_INNER), lambda b, c: (rowblk(b, c), 0)),
        out_shape=jax.ShapeDtypeStruct((t, SSD_D_INNER), BF16),
        scratch_shapes=[
            pltpu.VMEM((CONV_PAD + Q, SSD_D_INNER), F32),
            pltpu.VMEM((CONV_PAD + Q, BC_DIM), F32),
            pltpu.VMEM((SSD_STATE, SSD_D_INNER), F32),
            pltpu.VMEM((LANES, Q), F32),
            pltpu.VMEM((LANES, Q), F32),
            pltpu.VMEM((LANES, Q), F32),
            pltpu.VMEM((Q, SSD_D_INNER), F32),
        ],
        compiler_params=_cparams(("parallel", "arbitrary")),
        name="ssd",
    )(p, p, p, dt_raw, cwx, cbx, cwb, cbb, dtb, alog, dsk, ng)


N_PAIRS = SWA_HEADS // 2


def _swa_kernel(sink_ref, q_ref, kp_ref, kc_ref, vp_ref, vc_ref, bias_ref, o_ref):
    n = pl.program_id(1)
    blk = SWA_BLOCK
    kband = jnp.concatenate([kp_ref[...], kc_ref[...]], axis=0)
    vband = jnp.concatenate([vp_ref[...], vc_ref[...]], axis=0)
    lane = lax.broadcasted_iota(I32, (1, LANES), 1)
    lo_f = jnp.where(lane < SWA_HEAD_DIM, 1.0, 0.0)
    lo_m = lo_f.astype(BF16)
    hi_m = (1.0 - lo_f).astype(BF16)
    colj = lax.broadcasted_iota(I32, (2 * blk, 2 * blk), 1)
    rowi = lax.broadcasted_iota(I32, (2 * blk, 1), 0)
    valid = (colj >= blk) | (n > 0)
    scale = SWA_HEAD_DIM ** -0.5
    for p in range(N_PAIRS):
        kv = p // (SWA_HEADS // SWA_KV_HEADS // 2)
        qp = q_ref[:, p * LANES:(p + 1) * LANES]
        qs = jnp.concatenate([qp * lo_m, qp * hi_m], axis=0)
        kd = kband[:, kv * LANES:(kv + 1) * LANES]
        s = _dot_nt(qs, kd) * scale + bias_ref[p]
        s = jnp.where(valid, s, NEG_INF)
        sink = jnp.where(rowi < blk, sink_ref[2 * p], sink_ref[2 * p + 1])
        m = jnp.maximum(jnp.max(s, axis=-1, keepdims=True), sink)
        e = jnp.exp(s - m)
        den = jnp.sum(e, axis=-1, keepdims=True) + jnp.exp(sink - m)
        pr = (e * (1.0 / den)).astype(BF16)
        vd = vband[:, kv * LANES:(kv + 1) * LANES]
        lhs = jnp.concatenate([pr[:blk], pr[blk:]], axis=1)
        rhs = jnp.concatenate([vd * lo_m, vd * hi_m], axis=0)
        o_ref[:, p * LANES:(p + 1) * LANES] = _dot(lhs, rhs).astype(BF16)


def _swa(p, sinks, bias, batch, seq):
    nb = seq // SWA_BLOCK
    t = batch * seq
    kw = 2 * SWA_KV_DIM
    cur = lambda b, n, s: b * nb + n
    prev = lambda b, n, s: b * nb + jnp.maximum(n - 1, 0)
    return pl.pallas_call(
        _swa_kernel,
        grid_spec=pltpu.PrefetchScalarGridSpec(
            num_scalar_prefetch=1,
            grid=(batch, nb),
            in_specs=[
                pl.BlockSpec((SWA_BLOCK, SWA_Q_DIM), lambda b, n, s: (cur(b, n, s), OFF_Q // SWA_Q_DIM)),
                pl.BlockSpec((SWA_BLOCK, kw), lambda b, n, s: (prev(b, n, s), OFF_K // kw)),
                pl.BlockSpec((SWA_BLOCK, kw), lambda b, n, s: (cur(b, n, s), OFF_K // kw)),
                pl.BlockSpec((SWA_BLOCK, kw), lambda b, n, s: (prev(b, n, s), OFF_V // kw)),
                pl.BlockSpec((SWA_BLOCK, kw), lambda b, n, s: (cur(b, n, s), OFF_V // kw)),
                pl.BlockSpec((N_PAIRS, 2 * SWA_BLOCK, 2 * SWA_BLOCK), lambda b, n, s: (0, 0, 0)),
            ],
            out_specs=pl.BlockSpec((SWA_BLOCK, SWA_Q_DIM), lambda b, n, s: (cur(b, n, s), 0)),
        ),
        out_shape=jax.ShapeDtypeStruct((t, SWA_Q_DIM), BF16),
        compiler_params=_cparams(("parallel", "arbitrary")),
        name="swa",
    )(sinks, p, p, p, p, p, bias)


def _t5_causal_bucket(dist):
    max_exact = REL_BUCKETS // 2
    large = max_exact + (jnp.log(jnp.maximum(dist, 1).astype(F32) / max_exact)
                         / math.log(REL_MAX_DIST / max_exact) * (REL_BUCKETS - max_exact)).astype(I32)
    large = jnp.minimum(large, REL_BUCKETS - 1)
    return jnp.where(dist < max_exact, dist, large)


def _swa_bias_table(rel_bias):
    qi = jnp.arange(SWA_BLOCK)[:, None]
    kj = jnp.arange(2 * SWA_BLOCK)[None, :]
    dist = qi + SWA_BLOCK - kj
    in_window = (dist >= 0) & (dist < SWA_WINDOW)
    bucket = _t5_causal_bucket(jnp.maximum(dist, 0))
    onehot = (bucket[:, :, None] == jnp.arange(REL_BUCKETS)[None, None, :]).astype(F32)
    bias = jnp.einsum('ijb,bh->hij', onehot, rel_bias.astype(F32), precision=lax.Precision.HIGHEST)
    bias = jnp.where(in_window[None, :, :], bias, NEG_INF)
    return bias.reshape(N_PAIRS, 2 * SWA_BLOCK, 2 * SWA_BLOCK)


TL_XA = 512


def _xa_kernel(q_ref, mk_ref, mv_ref, o_ref):
    s = _dot_nt(q_ref[...], mk_ref[...]) * (XA_HEAD_DIM ** -0.5)
    m = jnp.max(s, axis=-1, keepdims=True)
    e = jnp.exp(s - m)
    pr = e * (1.0 / jnp.sum(e, axis=-1, keepdims=True))
    o_ref[...] = _dot(pr.astype(BF16), mv_ref[...]).astype(BF16)


def _xa(p, memkv, batch, seq):
    nl = seq // TL_XA
    t = batch * seq
    return pl.pallas_call(
        _xa_kernel,
        grid=(batch, nl, XA_HEADS),
        in_specs=[
            pl.BlockSpec((TL_XA, XA_HEAD_DIM), lambda b, l, h: (b * nl + l, OFF_QX // XA_HEAD_DIM + h)),
            pl.BlockSpec((MEM_LEN, XA_HEAD_DIM), lambda b, l, h: (b, h)),
            pl.BlockSpec((MEM_LEN, XA_HEAD_DIM), lambda b, l, h: (b, XA_HEADS + h)),
        ],
        out_specs=pl.BlockSpec((TL_XA, XA_HEAD_DIM), lambda b, l, h: (b * nl + l, h)),
        out_shape=jax.ShapeDtypeStruct((t, XA_DIM), BF16),
        compiler_params=_cparams(("parallel", "parallel", "parallel")),
        name="xattn",
    )(p, memkv, memkv)


TM_MG = 1024
TN_MG = 512


def _merge_kernel(ys_ref, yw_ref, yx_ref, ws_ref, ww_ref, wx_ref, g0_ref, g1_ref, g2_ref, o_ref):
    a = _dot(ys_ref[...], ws_ref[...])
    b = _dot(yw_ref[...], ww_ref[...])
    c = _dot(yx_ref[...], wx_ref[...])
    g0 = _sigmoid(g0_ref[...].astype(F32))
    g1 = _sigmoid(g1_ref[...].astype(F32))
    g2 = _sigmoid(g2_ref[...].astype(F32))
    o_ref[...] = (g0 * a + g1 * b + g2 * c).astype(BF16)


def _merge(y_ssd, y_swa, y_xa, w_ssd_o, w_swa_o, w_xa_o, p):
    t = y_ssd.shape[0]
    gpb = D_MODEL // TN_MG
    gate = lambda k: pl.BlockSpec((TM_MG, TN_MG), lambda i, j: (i, OFF_GATES // TN_MG + k * gpb + j))
    return pl.pallas_call(
        _merge_kernel,
        grid=(t // TM_MG, D_MODEL // TN_MG),
        in_specs=[
            pl.BlockSpec((TM_MG, SSD_D_INNER), lambda i, j: (i, 0)),
            pl.BlockSpec((TM_MG, SWA_Q_DIM), lambda i, j: (i, 0)),
            pl.BlockSpec((TM_MG, XA_DIM), lambda i, j: (i, 0)),
            pl.BlockSpec((SSD_D_INNER, TN_MG), lambda i, j: (0, j)),
            pl.BlockSpec((SWA_Q_DIM, TN_MG), lambda i, j: (0, j)),
            pl.BlockSpec((XA_DIM, TN_MG), lambda i, j: (0, j)),
            gate(0), gate(1), gate(2),
        ],
        out_specs=pl.BlockSpec((TM_MG, TN_MG), lambda i, j: (i, j)),
        out_shape=jax.ShapeDtypeStruct((t, D_MODEL), BF16),
        compiler_params=_cparams(("parallel", "parallel")),
        name="merge",
    )(y_ssd, y_swa, y_xa, w_ssd_o, w_swa_o, w_xa_o, p, p, p)


TM_LN = 512


def _layer_norm(v, g, b):
    mu = jnp.mean(v, axis=-1, keepdims=True)
    d = v - mu
    var = jnp.mean(d * d, axis=-1, keepdims=True)
    return (d * lax.rsqrt(var + LN_EPS)) * g + b


def _pack_halves(v):
    n = v.shape[1] // 2
    lo = lax.bitcast_convert_type(v[:, :n].astype(BF16).astype(F32), U32) >> 16
    hi = lax.bitcast_convert_type(v[:, n:].astype(BF16).astype(F32), U32) & jnp.uint32(0xFFFF0000)
    return hi | lo


def _unpack_halves(w):
    lo = lax.bitcast_convert_type(w << 16, F32)
    hi = lax.bitcast_convert_type(w & jnp.uint32(0xFFFF0000), F32)
    return lo, hi


def _out_ln_kernel(alpha, m_ref, w_ref, x_ref, g_ref, b_ref, h_ref, hp_ref):
    mix = _dot(m_ref[...], w_ref[...])
    hn = _layer_norm(alpha * x_ref[...] + mix, g_ref[...], b_ref[...])
    h_ref[...] = hn
    hp_ref[...] = _pack_halves(hn)


def _out_ln(merged, w_out, x2d, g, b, alpha):
    t = x2d.shape[0]
    row = lambda i: (i, 0)
    const = lambda i: (0, 0)
    return pl.pallas_call(
        functools.partial(_out_ln_kernel, alpha),
        grid=(t // TM_LN,),
        in_specs=[
            pl.BlockSpec((TM_LN, D_MODEL), row),
            pl.BlockSpec((D_MODEL, D_MODEL), const),
            pl.BlockSpec((TM_LN, D_MODEL), row),
            pl.BlockSpec((1, D_MODEL), const),
            pl.BlockSpec((1, D_MODEL), const),
        ],
        out_specs=[pl.BlockSpec((TM_LN, D_MODEL), row), pl.BlockSpec((TM_LN, D_MODEL // 2), row)],
        out_shape=[jax.ShapeDtypeStruct((t, D_MODEL), F32), jax.ShapeDtypeStruct((t, D_MODEL // 2), U32)],
        compiler_params=_cparams(("parallel",)),
        name="out_ln1",
    )(merged, w_out, x2d, g, b)


TM_RT = 512
EPG = N_EXPERTS // N_EXPERT_GROUPS


def _router_kernel(h_ref, whi_ref, wlo_ref, bias_ref, su_ref, idx_ref, w_ref, rank_ref, cnt_ref, run_ref):
    @pl.when(pl.program_id(0) == 0)
    def _():
        run_ref[...] = jnp.zeros_like(run_ref)

    h = h_ref[...]
    h_hi = h.astype(BF16)
    h_lo = (h - h_hi.astype(F32)).astype(BF16)
    whi = whi_ref[...]
    logits = _dot_nt(whi, h_hi) + _dot_nt(whi, h_lo) + _dot_nt(wlo_ref[...], h_hi)
    scores = _sigmoid(logits)
    sel = scores + bias_ref[...]

    iota_m = lax.broadcasted_iota(I32, (EPG, TM_RT), 0).astype(F32)
    iota_g = lax.broadcasted_iota(I32, (N_EXPERT_GROUPS, TM_RT), 0).astype(F32)
    gs = []
    for g in range(N_EXPERT_GROUPS):
        v = sel[g * EPG:(g + 1) * EPG, :]
        m1 = jnp.max(v, axis=0, keepdims=True)
        i1 = jnp.min(jnp.where(v == m1, iota_m, float(EPG)), axis=0, keepdims=True)
        m2 = jnp.max(jnp.where(iota_m == i1, NEG_INF, v), axis=0, keepdims=True)
        gs.append(m1 + m2)
    gsc = jnp.concatenate(gs, axis=0)
    chosen = jnp.zeros((N_EXPERT_GROUPS, TM_RT), F32)
    for _ in range(TOPK_GROUPS):
        m = jnp.max(gsc, axis=0, keepdims=True)
        ig = jnp.min(jnp.where(gsc == m, iota_g, float(N_EXPERT_GROUPS)), axis=0, keepdims=True)
        hit = iota_g == ig
        chosen = jnp.where(hit, 1.0, chosen)
        gsc = jnp.where(hit, NEG_INF, gsc)
    gmask = jnp.concatenate(
        [jnp.broadcast_to(chosen[g:g + 1, :], (EPG, TM_RT)) for g in range(N_EXPERT_GROUPS)], axis=0)
    masked = jnp.where(gmask > 0.0, sel, NEG_INF)

    iota_e = lax.broadcasted_iota(I32, (N_EXPERTS, TM_RT), 0).astype(F32)
    hits, idxs, ws = [], [], []
    for _ in range(TOP_K):
        m = jnp.max(masked, axis=0, keepdims=True)
        ie = jnp.min(jnp.where(masked == m, iota_e, float(N_EXPERTS)), axis=0, keepdims=True)
        hit = iota_e == ie
        hits.append(hit)
        idxs.append(ie)
        ws.append(jnp.sum(jnp.where(hit, scores, 0.0), axis=0, keepdims=True))
        masked = jnp.where(hit, NEG_INF, masked)
    wsum = ws[0]
    for k in range(1, TOP_K):
        wsum = wsum + ws[k]
    inv = 1.0 / wsum
    idx_ref[...] = jnp.concatenate(idxs, axis=0).astype(I32)
    w_ref[...] = jnp.concatenate([w * inv * ROUTED_SCALE for w in ws], axis=0)

    assign = jnp.zeros((N_EXPERTS, TM_RT), F32)
    for hit in hits:
        assign = jnp.where(hit, 1.0, assign)
    pos = _dot(assign.astype(BF16), su_ref[...]) + run_ref[...]
    rank_ref[...] = jnp.concatenate(
        [jnp.sum(jnp.where(hit, pos, 0.0), axis=0, keepdims=True) for hit in hits], axis=0).astype(I32)
    run = run_ref[...] + jnp.sum(assign, axis=1, keepdims=True)
    run_ref[...] = run
    cnt_ref[...] = run


def _router(h1, r_hi, r_lo, rbias, su):
    t = h1.shape[0]
    tile = lambda i: (0, i)
    const = lambda i: (0, 0)
    return pl.pallas_call(
        _router_kernel,
        grid=(t // TM_RT,),
        in_specs=[
            pl.BlockSpec((TM_RT, D_MODEL), lambda i: (i, 0)),
            pl.BlockSpec((N_EXPERTS, D_MODEL), const),
            pl.BlockSpec((N_EXPERTS, D_MODEL), const),
            pl.BlockSpec((N_EXPERTS, 1), const),
            pl.BlockSpec((TM_RT, TM_RT), const),
        ],
        out_specs=[
            pl.BlockSpec((TOP_K, TM_RT), tile),
            pl.BlockSpec((TOP_K, TM_RT), tile),
            pl.BlockSpec((TOP_K, TM_RT), tile),
            pl.BlockSpec((N_EXPERTS, TM_RT), const),
        ],
        out_shape=[
            jax.ShapeDtypeStruct((TOP_K, t), I32),
            jax.ShapeDtypeStruct((TOP_K, t), F32),
            jax.ShapeDtypeStruct((TOP_K, t), I32),
            jax.ShapeDtypeStruct((N_EXPERTS, TM_RT), F32),
        ],
        scratch_shapes=[pltpu.VMEM((N_EXPERTS, TM_RT), F32)],
        compiler_params=_cparams(("arbitrary",)),
        name="router",
    )(h1, r_hi, r_lo, rbias, su)


TM_DS = 2048


def _dest_kernel(idx_ref, rank_ref, ps_ref, d_ref):
    iota_e = lax.broadcasted_iota(I32, (N_EXPERTS, TM_DS), 0)
    ps = ps_ref[...]
    rows = []
    for k in range(TOP_K):
        hit = iota_e == idx_ref[k:k + 1, :]
        rows.append(jnp.sum(jnp.where(hit, ps, 0.0), axis=0, keepdims=True))
    d_ref[...] = jnp.concatenate(rows, axis=0).astype(I32) + rank_ref[...]


def _dest(idx_t, rank_t, pstart):
    t = idx_t.shape[1]
    tile = lambda i: (0, i)
    return pl.pallas_call(
        _dest_kernel,
        grid=(t // TM_DS,),
        in_specs=[pl.BlockSpec((TOP_K, TM_DS), tile), pl.BlockSpec((TOP_K, TM_DS), tile),
                  pl.BlockSpec((N_EXPERTS, 1), lambda i: (0, 0))],
        out_specs=pl.BlockSpec((TOP_K, TM_DS), tile),
        out_shape=jax.ShapeDtypeStruct((TOP_K, t), I32),
        compiler_params=_cparams(("parallel",)),
        name="dest",
    )(idx_t, rank_t, pstart)


TM_DP = 512
ROW_WORDS = D_MODEL // 2


def _dispatch_kernel(dest_ref, hp_ref, xs_ref, sem):
    t_total = pl.num_programs(0) * TM_DP
    base = pl.program_id(0) * TM_DP

    def issue(t, carry):
        for k in range(TOP_K):
            slot = dest_ref[k * t_total + base + t]
            pltpu.make_async_copy(hp_ref.at[pl.ds(t, 1), :], xs_ref.at[pl.ds(slot, 1), :], sem).start()
        return carry

    lax.fori_loop(0, TM_DP, issue, 0)
    for k in range(TOP_K):
        pltpu.make_async_copy(hp_ref, xs_ref.at[pl.ds(0, TM_DP), :], sem).wait()


def _dispatch(dest_flat, hp, n_slots):
    t = hp.shape[0]
    return pl.pallas_call(
        _dispatch_kernel,
        grid_spec=pltpu.PrefetchScalarGridSpec(
            num_scalar_prefetch=1,
            grid=(t // TM_DP,),
            in_specs=[pl.BlockSpec((TM_DP, ROW_WORDS), lambda i, d: (i, 0))],
            out_specs=pl.BlockSpec(memory_space=pl.ANY),
            scratch_shapes=[pltpu.SemaphoreType.DMA],
        ),
        out_shape=jax.ShapeDtypeStruct((n_slots, ROW_WORDS), U32),
        compiler_params=_cparams(("arbitrary",)),
        name="dispatch",
    )(dest_flat, hp)


def _expert_kernel(be_ref, nv_ref, first_ref, nxt_ref, x_ref, w1_ref, w3_ref, w2_ref, o_ref,
                   wf1, wf3, wf2, w1b, w3b, w2b, slot_ref, sems):
    b = pl.program_id(0)
    nv = nv_ref[b]

    def weight_copies(e, s):
        return (pltpu.make_async_copy(w1_ref.at[e], wf1.at[s], sems.at[s, 0]),
                pltpu.make_async_copy(w3_ref.at[e], wf3.at[s], sems.at[s, 1]),
                pltpu.make_async_copy(w2_ref.at[e], wf2.at[s], sems.at[s, 2]))

    @pl.when(b == 0)
    def _():
        slot_ref[0] = 0
        for cp in weight_copies(be_ref[0], 0):
            cp.start()

    @pl.when(first_ref[b] == 1)
    def _():
        s = slot_ref[0]
        for cp in weight_copies(be_ref[b], s):
            cp.wait()

        @pl.when(nxt_ref[b] >= 0)
        def _():
            for cp in weight_copies(nxt_ref[b], 1 - s):
                cp.start(priority=1)

        w1b[...] = wf1[s].astype(BF16)
        w3b[...] = wf3[s].astype(BF16)
        w2b[...] = wf2[s].astype(BF16)
        slot_ref[0] = 1 - s

    @pl.when(nv > 0)
    def _():
        live = lax.broadcasted_iota(I32, (MOE_BLK, 1), 0) < nv
        lo, hi = _unpack_halves(jnp.where(live, x_ref[...], jnp.uint32(0)))
        x = jnp.concatenate([lo.astype(BF16), hi.astype(BF16)], axis=1)
        hdn = _silu(_dot(x, w1b[...])) * _dot(x, w3b[...])
        o_ref[...] = _pack_halves(_dot(hdn.astype(BF16), w2b[...]))

    @pl.when(nv <= 0)
    def _():
        o_ref[...] = jnp.zeros_like(o_ref)


def _experts(block_e, block_nv, block_first, block_next, x_sorted, w1, w3, w2):
    n_slots = x_sorted.shape[0]
    n_blocks = n_slots // MOE_BLK
    row = lambda b, *_: (b, 0)
    return pl.pallas_call(
        _expert_kernel,
        grid_spec=pltpu.PrefetchScalarGridSpec(
            num_scalar_prefetch=4,
            grid=(n_blocks,),
            in_specs=[
                pl.BlockSpec((MOE_BLK, ROW_WORDS), row),
                pl.BlockSpec(memory_space=pl.ANY),
                pl.BlockSpec(memory_space=pl.ANY),
                pl.BlockSpec(memory_space=pl.ANY),
            ],
            out_specs=pl.BlockSpec((MOE_BLK, ROW_WORDS), row),
            scratch_shapes=[
                pltpu.VMEM((2, D_MODEL, EXPERT_DIM), F32),
                pltpu.VMEM((2, D_MODEL, EXPERT_DIM), F32),
                pltpu.VMEM((2, EXPERT_DIM, D_MODEL), F32),
                pltpu.VMEM((D_MODEL, EXPERT_DIM), BF16),
                pltpu.VMEM((D_MODEL, EXPERT_DIM), BF16),
                pltpu.VMEM((EXPERT_DIM, D_MODEL), BF16),
                pltpu.SMEM((1,), I32),
                pltpu.SemaphoreType.DMA((2, 3)),
            ],
        ),
        out_shape=jax.ShapeDtypeStruct((n_slots, ROW_WORDS), U32),
        compiler_params=_cparams(("arbitrary",), vmem_mb=56),
        name="experts",
    )(block_e, block_nv, block_first, block_next, x_sorted, w1, w3, w2)


TM_FN = 256


def _final_kernel(alpha, dest_ref, ys_ref, w_ref, h_ref, ws1_ref, ws3_ref, ws2_ref, g_ref, b_ref, o_ref, buf, sem):
    t_total = pl.num_programs(0) * TM_FN
    base = pl.program_id(0) * TM_FN

    def issue(t, carry):
        for k in range(TOP_K):
            slot = dest_ref[k * t_total + base + t]
            pltpu.make_async_copy(ys_ref.at[pl.ds(slot, 1), :], buf.at[k, pl.ds(t, 1), :], sem).start()
        return carry

    lax.fori_loop(0, TM_FN, issue, 0)

    h = h_ref[...]
    xb = h.astype(BF16)
    hdn = _silu(_dot(xb, ws1_ref[...])) * _dot(xb, ws3_ref[...])
    shared = _dot(hdn.astype(BF16), ws2_ref[...])

    for k in range(TOP_K):
        pltpu.make_async_copy(ys_ref.at[pl.ds(0, TM_FN), :], buf.at[k], sem).wait()
    half = D_MODEL // 2
    acc_lo = shared[:, :half]
    acc_hi = shared[:, half:]
    for k in range(TOP_K):
        lo, hi = _unpack_halves(buf[k])
        wk = w_ref[:, k:k + 1]
        acc_lo = acc_lo + wk * lo
        acc_hi = acc_hi + wk * hi
    ffn = jnp.concatenate([acc_lo, acc_hi], axis=1)
    o_ref[...] = _layer_norm(alpha * h + ffn, g_ref[...], b_ref[...])


def _final(dest_flat, ys, w_tok, h1, ws1, ws3, ws2, g, b, alpha):
    t = h1.shape[0]
    row = lambda i, d: (i, 0)
    const = lambda i, d: (0, 0)
    return pl.pallas_call(
        functools.partial(_final_kernel, alpha),
        grid_spec=pltpu.PrefetchScalarGridSpec(
            num_scalar_prefetch=1,
            grid=(t // TM_FN,),
            in_specs=[
                pl.BlockSpec(memory_space=pl.ANY),
                pl.BlockSpec((TM_FN, TOP_K), row),
                pl.BlockSpec((TM_FN, D_MODEL), row),
                pl.BlockSpec((D_MODEL, SHARED_DIM), const),
                pl.BlockSpec((D_MODEL, SHARED_DIM), const),
                pl.BlockSpec((SHARED_DIM, D_MODEL), const),
                pl.BlockSpec((1, D_MODEL), const),
                pl.BlockSpec((1, D_MODEL), const),
            ],
            out_specs=pl.BlockSpec((TM_FN, D_MODEL), row),
            scratch_shapes=[pltpu.VMEM((TOP_K, TM_FN, ROW_WORDS), U32), pltpu.SemaphoreType.DMA],
        ),
        out_shape=jax.ShapeDtypeStruct((t, D_MODEL), F32),
        compiler_params=_cparams(("arbitrary",)),
        name="combine_ln2",
    )(dest_flat, ys, w_tok, h1, ws1, ws3, ws2, g, b)


def _dup_heads(w, heads, dim):
    d = w.shape[0]
    w = w.reshape(d, heads, 1, dim)
    return jnp.broadcast_to(w, (d, heads, 2, dim)).reshape(d, heads * 2 * dim)


def _prep_w_in(w_in):
    sizes = (SSD_D_INNER, SSD_D_INNER + BC_DIM, SSD_HEADS, SWA_Q_DIM, SWA_KV_DIM, SWA_KV_DIM, XA_DIM,
             N_BRANCH * D_MODEL)
    pts = []
    acc = 0
    for s in sizes[:-1]:
        acc += s
        pts.append(acc)
    z, xbc, dt, q_s, k_s, v_s, q_x, gates = jnp.split(w_in, pts, axis=1)
    wp = jnp.concatenate(
        [gates, z, xbc, q_s, _dup_heads(k_s, SWA_KV_HEADS, SWA_HEAD_DIM),
         _dup_heads(v_s, SWA_KV_HEADS, SWA_HEAD_DIM), q_x], axis=1).astype(BF16)
    wdt = jnp.pad(dt, ((0, 0), (0, LANES - SSD_HEADS))).astype(BF16)
    return wp, wdt


def _pad_lanes(v):
    return jnp.pad(v, (0, LANES - v.shape[0]))[None, :].astype(F32)


def _layer(h2d, mem2d, batch, seq, w_in, conv_w, conv_b, dt_bias, a_log, d_skip, ssd_norm_g, swa_sinks, rel_bias,
           w_mem_kv, w_ssd_o, w_swa_o, w_xa_o, w_out, ln1_g, ln1_b, router_w, router_bias, w1, w3, w2, ws1, ws3,
           ws2, ln2_g, ln2_b, alpha):
    t = batch * seq
    wp, wdt = _prep_w_in(w_in)
    p, dt_raw = _inproj(h2d, wp, wdt)

    y_ssd = _ssd(
        p, dt_raw,
        conv_w[:, :SSD_D_INNER], conv_b[None, :SSD_D_INNER], conv_w[:, SSD_D_INNER:], conv_b[None, SSD_D_INNER:],
        _pad_lanes(dt_bias), _pad_lanes(a_log),
        jnp.repeat(d_skip, SSD_HEAD_DIM)[None, :], ssd_norm_g[None, :], batch, seq)

    y_swa = _swa(p, swa_sinks.astype(F32), _swa_bias_table(rel_bias), batch, seq)

    memkv = _mm(mem2d, w_mem_kv.astype(BF16), BF16, batch * MEM_LEN, 1024)
    y_xa = _xa(p, memkv, batch, seq)

    merged = _merge(y_ssd, y_swa, y_xa, w_ssd_o.astype(BF16), w_swa_o.astype(BF16), w_xa_o.astype(BF16), p)
    h1, hp = _out_ln(merged, w_out.astype(BF16), h2d, ln1_g[None, :], ln1_b[None, :], alpha)

    rwt = router_w.T
    r_hi = rwt.astype(BF16)
    r_lo = (rwt - r_hi.astype(F32)).astype(BF16)
    ti = jnp.arange(TM_RT)
    su = (ti[:, None] < ti[None, :]).astype(BF16)
    idx_t, w_t, rank_t, cnt = _router(h1, r_hi, r_lo, router_bias[:, None].astype(F32), su)

    counts = cnt[:, 0].astype(I32)
    padded = (counts + MOE_BLK - 1) // MOE_BLK * MOE_BLK
    pend = jnp.cumsum(padded)
    pstart = pend - padded
    n_blocks = t * TOP_K // MOE_BLK + N_EXPERTS
    blk_lo = jnp.arange(n_blocks, dtype=I32) * MOE_BLK
    block_e = jnp.minimum(jnp.sum((pend[None, :] <= blk_lo[:, None]).astype(I32), axis=1), N_EXPERTS - 1)
    of_e = block_e[:, None] == jnp.arange(N_EXPERTS, dtype=I32)[None, :]
    valid_end = jnp.sum(jnp.where(of_e, (pstart + counts)[None, :], 0), axis=1)
    block_nv = jnp.clip(valid_end - blk_lo, 0, MOE_BLK).astype(I32)
    eids = jnp.arange(N_EXPERTS, dtype=I32)
    later_active = (eids[None, :] > eids[:, None]) & (padded > 0)[None, :]
    next_active = jnp.min(jnp.where(later_active, eids[None, :], N_EXPERTS), axis=1)
    next_active = jnp.where(next_active >= N_EXPERTS, -1, next_active)
    seg_start = jnp.sum(jnp.where(of_e, pstart[None, :], 0), axis=1)
    block_first = ((blk_lo == seg_start) & (block_nv > 0)).astype(I32)
    block_next = jnp.sum(jnp.where(of_e, next_active[None, :], 0), axis=1).astype(I32)

    dest_flat = _dest(idx_t, rank_t, pstart.astype(F32)[:, None]).reshape(-1)
    x_sorted = _dispatch(dest_flat, hp, n_blocks * MOE_BLK)
    ys = _experts(block_e, block_nv, block_first, block_next, x_sorted, w1, w3, w2)
    return _final(dest_flat, ys, w_t.T, h1, ws1.astype(BF16), ws3.astype(BF16), ws2.astype(BF16), ln2_g[None, :],
                  ln2_b[None, :], alpha)


def kernel(x, mem, w_in, conv_w, conv_b, dt_bias, a_log, d_skip, ssd_norm_g, swa_sinks, rel_bias, w_mem_kv, w_ssd_o, w_swa_o, w_xa_o, w_out, ln1_g, ln1_b, router_w, router_bias, w1, w3, w2, ws1, ws3, ws2, ln2_g, ln2_b):
    batch, seq, d = x.shape
    depth = w_in.shape[0]
    alpha = (2.0 * depth) ** 0.25
    h = x.reshape(batch * seq, d)
    mem2d = mem.reshape(batch * MEM_LEN, d)
    for i in range(depth):
        h = _layer(h, mem2d, batch, seq, w_in[i], conv_w[i], conv_b[i], dt_bias[i], a_log[i], d_skip[i],
                   ssd_norm_g[i], swa_sinks[i], rel_bias, w_mem_kv[i], w_ssd_o[i], w_swa_o[i], w_xa_o[i], w_out[i],
                   ln1_g[i], ln1_b[i], router_w[i], router_bias[i], w1[i], w3[i], w2[i], ws1[i], ws3[i], ws2[i],
                   ln2_g[i], ln2_b[i], alpha)
    return h.reshape(batch, seq, d)
```

```python
import functools
import math

import jax
import jax.numpy as jnp
from jax import lax
from jax.experimental import pallas as pl
from jax.experimental.pallas import tpu as pltpu

F32 = jnp.float32
BF16 = jnp.bfloat16
I32 = jnp.int32
U32 = jnp.uint32

D_MODEL = 2048
MEM_LEN = 256
SSD_D_INNER = D_MODEL
SSD_HEAD_DIM = 64
SSD_HEADS = 32
SSD_GROUPS = 4
SSD_STATE = 128
SSD_CONV = 4
SSD_CHUNK = 128
SWA_HEADS = 16
SWA_KV_HEADS = 4
SWA_HEAD_DIM = 64
SWA_WINDOW = 128
SWA_BLOCK = 128
REL_BUCKETS = 32
REL_MAX_DIST = 128
XA_HEADS = 4
XA_HEAD_DIM = 256
N_BRANCH = 3
N_EXPERTS = 64
TOP_K = 8
N_EXPERT_GROUPS = 8
TOPK_GROUPS = 4
EXPERT_DIM = 512
SHARED_DIM = 512
ROUTED_SCALE = 2.5
LN_EPS = 1e-5
RMS_EPS = 1e-5

SWA_Q_DIM = SWA_HEADS * SWA_HEAD_DIM
SWA_KV_DIM = SWA_KV_HEADS * SWA_HEAD_DIM
XA_DIM = XA_HEADS * XA_HEAD_DIM
BC_DIM = 2 * SSD_GROUPS * SSD_STATE

OFF_GATES = 0
OFF_Z = OFF_GATES + N_BRANCH * D_MODEL
OFF_XS = OFF_Z + SSD_D_INNER
OFF_BC = OFF_XS + SSD_D_INNER
OFF_Q = OFF_BC + BC_DIM
OFF_K = OFF_Q + SWA_Q_DIM
OFF_V = OFF_K + 2 * SWA_KV_DIM
OFF_QX = OFF_V + 2 * SWA_KV_DIM
NP_TOTAL = OFF_QX + XA_DIM

LANES = 128
MOE_BLK = 256
NEG_INF = float("-inf")


def _cparams(sem, vmem_mb=48):
    return pltpu.CompilerParams(dimension_semantics=sem, vmem_limit_bytes=vmem_mb * 1024 * 1024)


def _dot(a, b):
    return jnp.dot(a, b, preferred_element_type=F32)


def _dot_nt(a, b):
    return lax.dot_general(a, b, (((1,), (1,)), ((), ())), preferred_element_type=F32)


def _sigmoid(x):
    return 1.0 / (1.0 + jnp.exp(-x))


def _silu(x):
    return x * _sigmoid(x)


TM_IN = 1024
TN_IN = 1024


def _inproj_kernel(x_ref, w_ref, wdt_ref, p_ref, dt_ref, xb_ref):
    @pl.when(pl.program_id(1) == 0)
    def _():
        xb = x_ref[...].astype(BF16)
        xb_ref[...] = xb
        dt_ref[...] = _dot(xb, wdt_ref[...])

    p_ref[...] = _dot(xb_ref[...], w_ref[...]).astype(BF16)


def _inproj(x2d, wp, wdt):
    t = x2d.shape[0]
    return pl.pallas_call(
        _inproj_kernel,
        grid=(t // TM_IN, NP_TOTAL // TN_IN),
        in_specs=[
            pl.BlockSpec((TM_IN, D_MODEL), lambda i, j: (i, 0)),
            pl.BlockSpec((D_MODEL, TN_IN), lambda i, j: (0, j)),
            pl.BlockSpec((D_MODEL, LANES), lambda i, j: (0, 0)),
        ],
        out_specs=[
            pl.BlockSpec((TM_IN, TN_IN), lambda i, j: (i, j)),
            pl.BlockSpec((TM_IN, LANES), lambda i, j: (i, 0)),
        ],
        out_shape=[jax.ShapeDtypeStruct((t, NP_TOTAL), BF16), jax.ShapeDtypeStruct((t, LANES), F32)],
        scratch_shapes=[pltpu.VMEM((TM_IN, D_MODEL), BF16)],
        compiler_params=_cparams(("parallel", "arbitrary")),
        name="inproj",
    )(x2d, wp, wdt)


def _mm_kernel(a_ref, b_ref, o_ref):
    o_ref[...] = _dot(a_ref[...].astype(BF16), b_ref[...]).astype(o_ref.dtype)


def _mm(a, b, out_dtype, tm, tn):
    m, k = a.shape
    n = b.shape[1]
    return pl.pallas_call(
        _mm_kernel,
        grid=(m // tm, n // tn),
        in_specs=[pl.BlockSpec((tm, k), lambda i, j: (i, 0)), pl.BlockSpec((k, tn), lambda i, j: (0, j))],
        out_specs=pl.BlockSpec((tm, tn), lambda i, j: (i, j)),
        out_shape=jax.ShapeDtypeStruct((m, n), out_dtype),
        compiler_params=_cparams(("parallel", "parallel")),
        name="mm",
    )(a, b)


Q = SSD_CHUNK


def _conv_shift_matrix():
    t = jnp.arange(Q)[:, None]
    j = jnp.arange(2 * Q)[None, :]
    blocks = []
    for k in range(SSD_CONV - 1):
        src = t - (SSD_CONV - 1 - k)
        blocks.append(j == jnp.where(src >= 0, src, 2 * Q + src))
    return jnp.concatenate(blocks, axis=0).astype(BF16)


def _ssd_kernel(xs_ref, bc_ref, z_ref, dtr_ref, shift_ref, cwx_ref, cbx_ref, cwb_ref, cbb_ref, dtb_ref, alog_ref,
                dsk_ref, ng_ref, o_ref, hx_ref, hb_ref, st_ref, cumt_ref, dtt_ref, wt_ref, y_ref):
    c = pl.program_id(1)

    @pl.when(c == 0)
    def _():
        hx_ref[...] = jnp.zeros_like(hx_ref)
        hb_ref[...] = jnp.zeros_like(hb_ref)
        st_ref[...] = jnp.zeros_like(st_ref)

    def conv_silu(in_ref, prev_ref, w_ref, b_ref):
        cur = in_ref[...]
        ext = jnp.concatenate([cur, prev_ref[...]], axis=0)
        shifted = _dot(shift_ref[...], ext)
        acc = b_ref[...] + w_ref[SSD_CONV - 1:SSD_CONV, :] * cur.astype(F32)
        for k in range(SSD_CONV - 1):
            acc = acc + w_ref[k:k + 1, :] * shifted[k * Q:(k + 1) * Q, :]
        prev_ref[...] = cur
        return _silu(acc)

    xs_c = conv_silu(xs_ref, hx_ref, cwx_ref, cbx_ref)
    bc_c = conv_silu(bc_ref, hb_ref, cwb_ref, cbb_ref)

    lane = lax.broadcasted_iota(I32, (Q, LANES), 1)
    row = lax.broadcasted_iota(I32, (Q, LANES), 0)
    tril = row >= lane
    lo_mask = lane < SSD_HEAD_DIM

    dtv = dtr_ref[...] + dtb_ref[...]
    dt = jnp.maximum(dtv, 0.0) + jnp.log1p(jnp.exp(-jnp.abs(dtv)))
    a = jnp.where(lane[0:1, :] < SSD_HEADS, -jnp.exp(alog_ref[...]), 0.0)
    la = dt * a
    tri = jnp.where(tril, 1.0, 0.0).astype(BF16)
    p1 = la.astype(BF16)
    r1 = la - p1.astype(F32)
    p2 = r1.astype(BF16)
    p3 = (r1 - p2.astype(F32)).astype(BF16)
    cum = _dot(tri, p1) + _dot(tri, p2) + _dot(tri, p3)
    ecum = jnp.exp(cum)
    cum_t = cum.T
    dt_t = dt.T
    last = cum_t[:, Q - 1:Q]
    cumt_ref[...] = cum_t
    dtt_ref[...] = dt_t
    wt_ref[...] = jnp.exp(last - cum_t) * dt_t
    chunk_decay = jnp.exp(last)

    def head_parts(h, cb, cm, bm_t):
        col = jnp.broadcast_to(cum[:, h:h + 1], (Q, Q))
        rw = cumt_ref[h:h + 1, :]
        dec = jnp.exp(jnp.where(tril, col - rw, NEG_INF))
        m = cb * (dec * dtt_ref[h:h + 1, :])
        ce = cm * jnp.broadcast_to(ecum[:, h:h + 1], (Q, Q))
        sl = bm_t * wt_ref[h:h + 1, :]
        return m.astype(BF16), ce.astype(BF16), sl.astype(BF16)

    for g in range(SSD_GROUPS):
        bm = bc_c[:, g * SSD_STATE:(g + 1) * SSD_STATE]
        cm = bc_c[:, BC_DIM // 2 + g * SSD_STATE:BC_DIM // 2 + (g + 1) * SSD_STATE]
        cb = _dot_nt(cm.astype(BF16), bm.astype(BF16))
        bm_t = bm.T
        for qd in range(SSD_HEADS // SSD_GROUPS // 2):
            hp = g * (SSD_HEADS // SSD_GROUPS // 2) + qd
            ha, hb = 2 * hp, 2 * hp + 1
            sl_ = slice(hp * LANES, (hp + 1) * LANES)
            xs_pair = xs_c[:, sl_]
            x_lo = jnp.where(lo_mask, xs_pair, 0.0).astype(BF16)
            x_hi = jnp.where(lo_mask, 0.0, xs_pair).astype(BF16)
            h_prev = st_ref[:, sl_]
            h_lo = jnp.where(lo_mask, h_prev, 0.0).astype(BF16)
            h_hi = jnp.where(lo_mask, 0.0, h_prev).astype(BF16)
            m_a, ce_a, sl_a = head_parts(ha, cb, cm, bm_t)
            m_b, ce_b, sl_b = head_parts(hb, cb, cm, bm_t)
            lhs = jnp.concatenate([m_a, ce_a, m_b, ce_b], axis=1)
            rhs = jnp.concatenate([x_lo, h_lo, x_hi, h_hi], axis=0)
            y_ref[:, sl_] = _dot(lhs, rhs)
            new = _dot(jnp.concatenate([sl_a, sl_b], axis=1), jnp.concatenate([x_lo, x_hi], axis=0))
            cd = jnp.where(lo_mask[0:1, :], chunk_decay[ha:ha + 1, :], chunk_decay[hb:hb + 1, :])
            st_ref[:, sl_] = h_prev * cd + new

    y = y_ref[...] + xs_c * dsk_ref[...]
    y = y * _silu(z_ref[...].astype(F32))
    gw = SSD_D_INNER // SSD_GROUPS
    for g in range(SSD_GROUPS):
        yg = y[:, g * gw:(g + 1) * gw]
        ms = jnp.mean(yg * yg, axis=-1, keepdims=True)
        o_ref[:, g * gw:(g + 1) * gw] = ((yg * lax.rsqrt(ms + RMS_EPS)) * ng_ref[:, g * gw:(g + 1) * gw]).astype(BF16)


def _ssd(p, dt_raw, cwx, cbx, cwb, cbb, dtb, alog, dsk, ng, batch, seq):
    nc = seq // Q
    t = batch * seq
    rowblk = lambda b, c: b * nc + c
    const = lambda b, c: (0, 0)
    return pl.pallas_call(
        _ssd_kernel,
        grid=(batch, nc),
        in_specs=[
            pl.BlockSpec((Q, SSD_D_INNER), lambda b, c: (rowblk(b, c), OFF_XS // SSD_D_INNER)),
            pl.BlockSpec((Q, BC_DIM), lambda b, c: (rowblk(b, c), OFF_BC // BC_DIM)),
            pl.BlockSpec((Q, SSD_D_INNER), lambda b, c: (rowblk(b, c), OFF_Z // SSD_D_INNER)),
            pl.BlockSpec((Q, LANES), lambda b, c: (rowblk(b, c), 0)),
            pl.BlockSpec(((SSD_CONV - 1) * Q, 2 * Q), const),
            pl.BlockSpec((SSD_CONV, SSD_D_INNER), const),
            pl.BlockSpec((1, SSD_D_INNER), const),
            pl.BlockSpec((SSD_CONV, BC_DIM), const),
            pl.BlockSpec((1, BC_DIM), const),
            pl.BlockSpec((1, LANES), const),
            pl.BlockSpec((1, LANES), const),
            pl.BlockSpec((1, SSD_D_INNER), const),
            pl.BlockSpec((1, SSD_D_INNER), const),
        ],
        out_specs=pl.BlockSpec((Q, SSD_D_INNER), lambda b, c: (rowblk(b, c), 0)),
        out_shape=jax.ShapeDtypeStruct((t, SSD_D_INNER), BF16),
        scratch_shapes=[
            pltpu.VMEM((Q, SSD_D_INNER), BF16),
            pltpu.VMEM((Q, BC_DIM), BF16),
            pltpu.VMEM((SSD_STATE, SSD_D_INNER), F32),
            pltpu.VMEM((LANES, Q), F32),
            pltpu.VMEM((LANES, Q), F32),
            pltpu.VMEM((LANES, Q), F32),
            pltpu.VMEM((Q, SSD_D_INNER), F32),
        ],
        compiler_params=_cparams(("parallel", "arbitrary")),
        name="ssd",
    )(p, p, p, dt_raw, _conv_shift_matrix(), cwx, cbx, cwb, cbb, dtb, alog, dsk, ng)


N_PAIRS = SWA_HEADS // 2


def _swa_kernel(sink_ref, q_ref, kp_ref, kc_ref, vp_ref, vc_ref, bias_ref, o_ref):
    n = pl.program_id(1)
    blk = SWA_BLOCK
    kband = jnp.concatenate([kp_ref[...], kc_ref[...]], axis=0)
    vband = jnp.concatenate([vp_ref[...], vc_ref[...]], axis=0)
    lane = lax.broadcasted_iota(I32, (1, LANES), 1)
    lo_f = jnp.where(lane < SWA_HEAD_DIM, 1.0, 0.0)
    lo_m = lo_f.astype(BF16)
    hi_m = (1.0 - lo_f).astype(BF16)
    colj = lax.broadcasted_iota(I32, (2 * blk, 2 * blk), 1)
    rowi = lax.broadcasted_iota(I32, (2 * blk, 1), 0)
    valid = (colj >= blk) | (n > 0)
    scale = SWA_HEAD_DIM ** -0.5
    for p in range(N_PAIRS):
        kv = p // (SWA_HEADS // SWA_KV_HEADS // 2)
        qp = q_ref[:, p * LANES:(p + 1) * LANES]
        qs = jnp.concatenate([qp * lo_m, qp * hi_m], axis=0)
        kd = kband[:, kv * LANES:(kv + 1) * LANES]
        s = _dot_nt(qs, kd) * scale + bias_ref[p]
        s = jnp.where(valid, s, NEG_INF)
        sink = jnp.where(rowi < blk, sink_ref[2 * p], sink_ref[2 * p + 1])
        m = jnp.maximum(jnp.max(s, axis=-1, keepdims=True), sink)
        e = jnp.exp(s - m)
        den = jnp.sum(e, axis=-1, keepdims=True) + jnp.exp(sink - m)
        pr = (e * (1.0 / den)).astype(BF16)
        vd = vband[:, kv * LANES:(kv + 1) * LANES]
        lhs = jnp.concatenate([pr[:blk], pr[blk:]], axis=1)
        rhs = jnp.concatenate([vd * lo_m, vd * hi_m], axis=0)
        o_ref[:, p * LANES:(p + 1) * LANES] = _dot(lhs, rhs).astype(BF16)


def _swa(p, sinks, bias, batch, seq):
    nb = seq // SWA_BLOCK
    t = batch * seq
    kw = 2 * SWA_KV_DIM
    cur = lambda b, n, s: b * nb + n
    prev = lambda b, n, s: b * nb + jnp.maximum(n - 1, 0)
    return pl.pallas_call(
        _swa_kernel,
        grid_spec=pltpu.PrefetchScalarGridSpec(
            num_scalar_prefetch=1,
            grid=(batch, nb),
            in_specs=[
                pl.BlockSpec((SWA_BLOCK, SWA_Q_DIM), lambda b, n, s: (cur(b, n, s), OFF_Q // SWA_Q_DIM)),
                pl.BlockSpec((SWA_BLOCK, kw), lambda b, n, s: (prev(b, n, s), OFF_K // kw)),
                pl.BlockSpec((SWA_BLOCK, kw), lambda b, n, s: (cur(b, n, s), OFF_K // kw)),
                pl.BlockSpec((SWA_BLOCK, kw), lambda b, n, s: (prev(b, n, s), OFF_V // kw)),
                pl.BlockSpec((SWA_BLOCK, kw), lambda b, n, s: (cur(b, n, s), OFF_V // kw)),
                pl.BlockSpec((N_PAIRS, 2 * SWA_BLOCK, 2 * SWA_BLOCK), lambda b, n, s: (0, 0, 0)),
            ],
            out_specs=pl.BlockSpec((SWA_BLOCK, SWA_Q_DIM), lambda b, n, s: (cur(b, n, s), 0)),
        ),
        out_shape=jax.ShapeDtypeStruct((t, SWA_Q_DIM), BF16),
        compiler_params=_cparams(("parallel", "arbitrary")),
        name="swa",
    )(sinks, p, p, p, p, p, bias)


def _t5_causal_bucket(dist):
    max_exact = REL_BUCKETS // 2
    large = max_exact + (jnp.log(jnp.maximum(dist, 1).astype(F32) / max_exact)
                         / math.log(REL_MAX_DIST / max_exact) * (REL_BUCKETS - max_exact)).astype(I32)
    large = jnp.minimum(large, REL_BUCKETS - 1)
    return jnp.where(dist < max_exact, dist, large)


def _swa_bias_table(rel_bias):
    qi = jnp.arange(SWA_BLOCK)[:, None]
    kj = jnp.arange(2 * SWA_BLOCK)[None, :]
    dist = qi + SWA_BLOCK - kj
    in_window = (dist >= 0) & (dist < SWA_WINDOW)
    bucket = _t5_causal_bucket(jnp.maximum(dist, 0))
    onehot = (bucket[:, :, None] == jnp.arange(REL_BUCKETS)[None, None, :]).astype(F32)
    bias = jnp.einsum('ijb,bh->hij', onehot, rel_bias.astype(F32), precision=lax.Precision.HIGHEST)
    bias = jnp.where(in_window[None, :, :], bias, NEG_INF)
    return bias.reshape(N_PAIRS, 2 * SWA_BLOCK, 2 * SWA_BLOCK)


TL_XA = 512


def _xa_kernel(q_ref, mk_ref, mv_ref, o_ref):
    s = _dot_nt(q_ref[...], mk_ref[...]) * (XA_HEAD_DIM ** -0.5)
    m = jnp.max(s, axis=-1, keepdims=True)
    e = jnp.exp(s - m)
    pr = e * (1.0 / jnp.sum(e, axis=-1, keepdims=True))
    o_ref[...] = _dot(pr.astype(BF16), mv_ref[...]).astype(BF16)


def _xa(p, memkv, batch, seq):
    nl = seq // TL_XA
    t = batch * seq
    return pl.pallas_call(
        _xa_kernel,
        grid=(batch, nl, XA_HEADS),
        in_specs=[
            pl.BlockSpec((TL_XA, XA_HEAD_DIM), lambda b, l, h: (b * nl + l, OFF_QX // XA_HEAD_DIM + h)),
            pl.BlockSpec((MEM_LEN, XA_HEAD_DIM), lambda b, l, h: (b, h)),
            pl.BlockSpec((MEM_LEN, XA_HEAD_DIM), lambda b, l, h: (b, XA_HEADS + h)),
        ],
        out_specs=pl.BlockSpec((TL_XA, XA_HEAD_DIM), lambda b, l, h: (b * nl + l, h)),
        out_shape=jax.ShapeDtypeStruct((t, XA_DIM), BF16),
        compiler_params=_cparams(("parallel", "parallel", "parallel")),
        name="xattn",
    )(p, memkv, memkv)


TM_MG = 1024
TN_MG = 512


def _merge_kernel(ys_ref, yw_ref, yx_ref, ws_ref, ww_ref, wx_ref, g0_ref, g1_ref, g2_ref, o_ref):
    a = _dot(ys_ref[...], ws_ref[...])
    b = _dot(yw_ref[...], ww_ref[...])
    c = _dot(yx_ref[...], wx_ref[...])
    g0 = _sigmoid(g0_ref[...].astype(F32))
    g1 = _sigmoid(g1_ref[...].astype(F32))
    g2 = _sigmoid(g2_ref[...].astype(F32))
    o_ref[...] = (g0 * a + g1 * b + g2 * c).astype(BF16)


def _merge(y_ssd, y_swa, y_xa, w_ssd_o, w_swa_o, w_xa_o, p):
    t = y_ssd.shape[0]
    gpb = D_MODEL // TN_MG
    gate = lambda k: pl.BlockSpec((TM_MG, TN_MG), lambda i, j: (i, OFF_GATES // TN_MG + k * gpb + j))
    return pl.pallas_call(
        _merge_kernel,
        grid=(t // TM_MG, D_MODEL // TN_MG),
        in_specs=[
            pl.BlockSpec((TM_MG, SSD_D_INNER), lambda i, j: (i, 0)),
            pl.BlockSpec((TM_MG, SWA_Q_DIM), lambda i, j: (i, 0)),
            pl.BlockSpec((TM_MG, XA_DIM), lambda i, j: (i, 0)),
            pl.BlockSpec((SSD_D_INNER, TN_MG), lambda i, j: (0, j)),
            pl.BlockSpec((SWA_Q_DIM, TN_MG), lambda i, j: (0, j)),
            pl.BlockSpec((XA_DIM, TN_MG), lambda i, j: (0, j)),
            gate(0), gate(1), gate(2),
        ],
        out_specs=pl.BlockSpec((TM_MG, TN_MG), lambda i, j: (i, j)),
        out_shape=jax.ShapeDtypeStruct((t, D_MODEL), BF16),
        compiler_params=_cparams(("parallel", "parallel")),
        name="merge",
    )(y_ssd, y_swa, y_xa, w_ssd_o, w_swa_o, w_xa_o, p, p, p)


TM_LN = 512


def _layer_norm(v, g, b):
    mu = jnp.mean(v, axis=-1, keepdims=True)
    d = v - mu
    var = jnp.mean(d * d, axis=-1, keepdims=True)
    return (d * lax.rsqrt(var + LN_EPS)) * g + b


def _pack_halves(v):
    n = v.shape[1] // 2
    lo = lax.bitcast_convert_type(v[:, :n].astype(BF16).astype(F32), U32) >> 16
    hi = lax.bitcast_convert_type(v[:, n:].astype(BF16).astype(F32), U32) & jnp.uint32(0xFFFF0000)
    return hi | lo


def _unpack_halves(w):
    lo = lax.bitcast_convert_type(w << 16, F32)
    hi = lax.bitcast_convert_type(w & jnp.uint32(0xFFFF0000), F32)
    return lo, hi


ROW_WORDS = D_MODEL // 2
ROW_TILE = ROW_WORDS // LANES


def _store_row_tiled(ref, words):
    m = words.shape[0]
    for c in range(ROW_TILE):
        ref[pl.ds(c, m, stride=ROW_TILE), :] = words[:, c * LANES:(c + 1) * LANES]


def _load_row_tiled(ref, m):
    return jnp.concatenate([ref[pl.ds(c, m, stride=ROW_TILE), :] for c in range(ROW_TILE)], axis=1)


def _out_ln_kernel(alpha, m_ref, w_ref, x_ref, g_ref, b_ref, h_ref, hp_ref):
    mix = _dot(m_ref[...], w_ref[...])
    hn = _layer_norm(alpha * x_ref[...] + mix, g_ref[...], b_ref[...])
    h_ref[...] = hn
    _store_row_tiled(hp_ref, _pack_halves(hn))


def _out_ln(merged, w_out, x2d, g, b, alpha):
    t = x2d.shape[0]
    row = lambda i: (i, 0)
    const = lambda i: (0, 0)
    return pl.pallas_call(
        functools.partial(_out_ln_kernel, alpha),
        grid=(t // TM_LN,),
        in_specs=[
            pl.BlockSpec((TM_LN, D_MODEL), row),
            pl.BlockSpec((D_MODEL, D_MODEL), const),
            pl.BlockSpec((TM_LN, D_MODEL), row),
            pl.BlockSpec((1, D_MODEL), const),
            pl.BlockSpec((1, D_MODEL), const),
        ],
        out_specs=[pl.BlockSpec((TM_LN, D_MODEL), row), pl.BlockSpec((TM_LN * ROW_TILE, LANES), row)],
        out_shape=[jax.ShapeDtypeStruct((t, D_MODEL), F32), jax.ShapeDtypeStruct((t * ROW_TILE, LANES), U32)],
        compiler_params=_cparams(("parallel",)),
        name="out_ln1",
    )(merged, w_out, x2d, g, b)


TM_RT = 512
EPG = N_EXPERTS // N_EXPERT_GROUPS


def _router_kernel(h_ref, whi_ref, wlo_ref, bias_ref, su_ref, idx_ref, w_ref, rank_ref, cnt_ref, run_ref):
    @pl.when(pl.program_id(0) == 0)
    def _():
        run_ref[...] = jnp.zeros_like(run_ref)

    h = h_ref[...]
    h_hi = h.astype(BF16)
    h_lo = (h - h_hi.astype(F32)).astype(BF16)
    whi = whi_ref[...]
    logits = _dot_nt(whi, h_hi) + _dot_nt(whi, h_lo) + _dot_nt(wlo_ref[...], h_hi)
    scores = _sigmoid(logits)
    sel = scores + bias_ref[...]

    iota_m = lax.broadcasted_iota(I32, (EPG, TM_RT), 0).astype(F32)
    iota_g = lax.broadcasted_iota(I32, (N_EXPERT_GROUPS, TM_RT), 0).astype(F32)
    gs = []
    for g in range(N_EXPERT_GROUPS):
        v = sel[g * EPG:(g + 1) * EPG, :]
        m1 = jnp.max(v, axis=0, keepdims=True)
        i1 = jnp.min(jnp.where(v == m1, iota_m, float(EPG)), axis=0, keepdims=True)
        m2 = jnp.max(jnp.where(iota_m == i1, NEG_INF, v), axis=0, keepdims=True)
        gs.append(m1 + m2)
    gsc = jnp.concatenate(gs, axis=0)
    chosen = jnp.zeros((N_EXPERT_GROUPS, TM_RT), F32)
    for _ in range(TOPK_GROUPS):
        m = jnp.max(gsc, axis=0, keepdims=True)
        ig = jnp.min(jnp.where(gsc == m, iota_g, float(N_EXPERT_GROUPS)), axis=0, keepdims=True)
        hit = iota_g == ig
        chosen = jnp.where(hit, 1.0, chosen)
        gsc = jnp.where(hit, NEG_INF, gsc)
    gmask = jnp.concatenate(
        [jnp.broadcast_to(chosen[g:g + 1, :], (EPG, TM_RT)) for g in range(N_EXPERT_GROUPS)], axis=0)
    masked = jnp.where(gmask > 0.0, sel, NEG_INF)

    iota_e = lax.broadcasted_iota(I32, (N_EXPERTS, TM_RT), 0).astype(F32)
    hits, idxs, ws = [], [], []
    for _ in range(TOP_K):
        m = jnp.max(masked, axis=0, keepdims=True)
        ie = jnp.min(jnp.where(masked == m, iota_e, float(N_EXPERTS)), axis=0, keepdims=True)
        hit = iota_e == ie
        hits.append(hit)
        idxs.append(ie)
        ws.append(jnp.sum(jnp.where(hit, scores, 0.0), axis=0, keepdims=True))
        masked = jnp.where(hit, NEG_INF, masked)
    wsum = ws[0]
    for k in range(1, TOP_K):
        wsum = wsum + ws[k]
    inv = 1.0 / wsum
    idx_ref[...] = jnp.concatenate(idxs, axis=0).astype(I32)
    w_ref[...] = jnp.concatenate([w * inv * ROUTED_SCALE for w in ws], axis=0)

    assign = jnp.zeros((N_EXPERTS, TM_RT), F32)
    for hit in hits:
        assign = jnp.where(hit, 1.0, assign)
    pos = _dot(assign.astype(BF16), su_ref[...]) + run_ref[...]
    rank_ref[...] = jnp.concatenate(
        [jnp.sum(jnp.where(hit, pos, 0.0), axis=0, keepdims=True) for hit in hits], axis=0).astype(I32)
    run = run_ref[...] + jnp.sum(assign, axis=1, keepdims=True)
    run_ref[...] = run
    cnt_ref[...] = run


def _router(h1, r_hi, r_lo, rbias, su):
    t = h1.shape[0]
    tile = lambda i: (0, i)
    const = lambda i: (0, 0)
    return pl.pallas_call(
        _router_kernel,
        grid=(t // TM_RT,),
        in_specs=[
            pl.BlockSpec((TM_RT, D_MODEL), lambda i: (i, 0)),
            pl.BlockSpec((N_EXPERTS, D_MODEL), const),
            pl.BlockSpec((N_EXPERTS, D_MODEL), const),
            pl.BlockSpec((N_EXPERTS, 1), const),
            pl.BlockSpec((TM_RT, TM_RT), const),
        ],
        out_specs=[
            pl.BlockSpec((TOP_K, TM_RT), tile),
            pl.BlockSpec((TOP_K, TM_RT), tile),
            pl.BlockSpec((TOP_K, TM_RT), tile),
            pl.BlockSpec((N_EXPERTS, TM_RT), const),
        ],
        out_shape=[
            jax.ShapeDtypeStruct((TOP_K, t), I32),
            jax.ShapeDtypeStruct((TOP_K, t), F32),
            jax.ShapeDtypeStruct((TOP_K, t), I32),
            jax.ShapeDtypeStruct((N_EXPERTS, TM_RT), F32),
        ],
        scratch_shapes=[pltpu.VMEM((N_EXPERTS, TM_RT), F32)],
        compiler_params=_cparams(("arbitrary",)),
        name="router",
    )(h1, r_hi, r_lo, rbias, su)


TM_DS = 2048


def _dest_kernel(idx_ref, rank_ref, ps_ref, d_ref):
    iota_e = lax.broadcasted_iota(I32, (N_EXPERTS, TM_DS), 0)
    ps = ps_ref[...]
    rows = []
    for k in range(TOP_K):
        hit = iota_e == idx_ref[k:k + 1, :]
        rows.append(jnp.sum(jnp.where(hit, ps, 0.0), axis=0, keepdims=True))
    d_ref[...] = jnp.concatenate(rows, axis=0).astype(I32) + rank_ref[...]


def _dest(idx_t, rank_t, pstart):
    t = idx_t.shape[1]
    tile = lambda i: (0, i)
    return pl.pallas_call(
        _dest_kernel,
        grid=(t // TM_DS,),
        in_specs=[pl.BlockSpec((TOP_K, TM_DS), tile), pl.BlockSpec((TOP_K, TM_DS), tile),
                  pl.BlockSpec((N_EXPERTS, 1), lambda i: (0, 0))],
        out_specs=pl.BlockSpec((TOP_K, TM_DS), tile),
        out_shape=jax.ShapeDtypeStruct((TOP_K, t), I32),
        compiler_params=_cparams(("parallel",)),
        name="dest",
    )(idx_t, rank_t, pstart)


TM_DP = 512


def _dispatch_kernel(dest_ref, hp_ref, xs_ref, sem):
    t_total = pl.num_programs(0) * TM_DP
    base = pl.program_id(0) * TM_DP

    def issue(t, carry):
        src = hp_ref.at[pl.ds(pl.multiple_of(t * ROW_TILE, ROW_TILE), ROW_TILE), :]
        for k in range(TOP_K):
            slot = dest_ref[k * t_total + base + t]
            pltpu.make_async_copy(src, xs_ref.at[slot], sem).start(priority=k % 2)
        return carry

    lax.fori_loop(0, TM_DP, issue, 0)
    for k in range(TOP_K):
        pltpu.make_async_copy(hp_ref.reshape(TM_DP, ROW_TILE, LANES), xs_ref.at[pl.ds(0, TM_DP)], sem).wait()


def _dispatch(dest_flat, hp, n_slots):
    t = hp.shape[0] // ROW_TILE
    return pl.pallas_call(
        _dispatch_kernel,
        grid_spec=pltpu.PrefetchScalarGridSpec(
            num_scalar_prefetch=1,
            grid=(t // TM_DP,),
            in_specs=[pl.BlockSpec((TM_DP * ROW_TILE, LANES), lambda i, d: (i, 0))],
            out_specs=pl.BlockSpec(memory_space=pl.ANY),
            scratch_shapes=[pltpu.SemaphoreType.DMA],
        ),
        out_shape=jax.ShapeDtypeStruct((n_slots, ROW_TILE, LANES), U32),
        compiler_params=_cparams(("arbitrary",)),
        name="dispatch",
    )(dest_flat, hp)


def _expert_kernel(be_ref, nv_ref, first_ref, nxt_ref, x_ref, w1_ref, w3_ref, w2_ref, o_ref,
                   wf1, wf3, wf2, w1b, w3b, w2b, slot_ref, sems):
    b = pl.program_id(0)
    nv = nv_ref[b]

    def weight_copies(e, s):
        return (pltpu.make_async_copy(w1_ref.at[e], wf1.at[s], sems.at[s, 0]),
                pltpu.make_async_copy(w3_ref.at[e], wf3.at[s], sems.at[s, 1]),
                pltpu.make_async_copy(w2_ref.at[e], wf2.at[s], sems.at[s, 2]))

    @pl.when(b == 0)
    def _():
        slot_ref[0] = 0
        for cp in weight_copies(be_ref[0], 0):
            cp.start()

    @pl.when(first_ref[b] == 1)
    def _():
        s = slot_ref[0]
        for cp in weight_copies(be_ref[b], s):
            cp.wait()

        @pl.when(nxt_ref[b] >= 0)
        def _():
            for cp in weight_copies(nxt_ref[b], 1 - s):
                cp.start(priority=1)

        w1b[...] = wf1[s].astype(BF16)
        w3b[...] = wf3[s].astype(BF16)
        w2b[...] = wf2[s].astype(BF16)
        slot_ref[0] = 1 - s

    @pl.when(nv > 0)
    def _():
        live = lax.broadcasted_iota(I32, (MOE_BLK, 1), 0) < nv
        lo, hi = _unpack_halves(jnp.where(live, _load_row_tiled(x_ref, MOE_BLK), jnp.uint32(0)))
        x = jnp.concatenate([lo.astype(BF16), hi.astype(BF16)], axis=1)
        hdn = _silu(_dot(x, w1b[...])) * _dot(x, w3b[...])
        _store_row_tiled(o_ref, _pack_halves(_dot(hdn.astype(BF16), w2b[...])))

    @pl.when(nv <= 0)
    def _():
        o_ref[...] = jnp.zeros_like(o_ref)


def _experts(block_e, block_nv, block_first, block_next, x_sorted, w1, w3, w2):
    n_rows = x_sorted.shape[0]
    n_blocks = n_rows // (MOE_BLK * ROW_TILE)
    row = lambda b, *_: (b, 0)
    return pl.pallas_call(
        _expert_kernel,
        grid_spec=pltpu.PrefetchScalarGridSpec(
            num_scalar_prefetch=4,
            grid=(n_blocks,),
            in_specs=[
                pl.BlockSpec((MOE_BLK * ROW_TILE, LANES), row),
                pl.BlockSpec(memory_space=pl.ANY),
                pl.BlockSpec(memory_space=pl.ANY),
                pl.BlockSpec(memory_space=pl.ANY),
            ],
            out_specs=pl.BlockSpec((MOE_BLK * ROW_TILE, LANES), row),
            scratch_shapes=[
                pltpu.VMEM((2, D_MODEL, EXPERT_DIM), F32),
                pltpu.VMEM((2, D_MODEL, EXPERT_DIM), F32),
                pltpu.VMEM((2, EXPERT_DIM, D_MODEL), F32),
                pltpu.VMEM((D_MODEL, EXPERT_DIM), BF16),
                pltpu.VMEM((D_MODEL, EXPERT_DIM), BF16),
                pltpu.VMEM((EXPERT_DIM, D_MODEL), BF16),
                pltpu.SMEM((1,), I32),
                pltpu.SemaphoreType.DMA((2, 3)),
            ],
        ),
        out_shape=jax.ShapeDtypeStruct((n_rows, LANES), U32),
        compiler_params=_cparams(("arbitrary",), vmem_mb=56),
        name="experts",
    )(block_e, block_nv, block_first, block_next, x_sorted, w1, w3, w2)


TM_FN = 256


def _final_kernel(alpha, dest_ref, ys_ref, w_ref, h_ref, ws1_ref, ws3_ref, ws2_ref, g_ref, b_ref, o_ref, buf, sem):
    t_total = pl.num_programs(0) * TM_FN
    base = pl.program_id(0) * TM_FN

    def issue(t, carry):
        row0 = pl.multiple_of(t * ROW_TILE, ROW_TILE)
        for k in range(TOP_K):
            slot = dest_ref[k * t_total + base + t]
            pltpu.make_async_copy(ys_ref.at[slot], buf.at[k, pl.ds(row0, ROW_TILE), :], sem).start(priority=k % 2)
        return carry

    lax.fori_loop(0, TM_FN, issue, 0)

    h = h_ref[...]
    xb = h.astype(BF16)
    hdn = _silu(_dot(xb, ws1_ref[...])) * _dot(xb, ws3_ref[...])
    shared = _dot(hdn.astype(BF16), ws2_ref[...])

    for k in range(TOP_K):
        pltpu.make_async_copy(ys_ref.at[pl.ds(0, TM_FN)], buf.at[k].reshape(TM_FN, ROW_TILE, LANES), sem).wait()
    half = D_MODEL // 2
    acc_lo = shared[:, :half]
    acc_hi = shared[:, half:]
    for k in range(TOP_K):
        lo, hi = _unpack_halves(_load_row_tiled(buf.at[k], TM_FN))
        wk = w_ref[:, k:k + 1]
        acc_lo = acc_lo + wk * lo
        acc_hi = acc_hi + wk * hi
    ffn = jnp.concatenate([acc_lo, acc_hi], axis=1)
    o_ref[...] = _layer_norm(alpha * h + ffn, g_ref[...], b_ref[...])


def _final(dest_flat, ys, w_tok, h1, ws1, ws3, ws2, g, b, alpha):
    t = h1.shape[0]
    row = lambda i, d: (i, 0)
    const = lambda i, d: (0, 0)
    return pl.pallas_call(
        functools.partial(_final_kernel, alpha),
        grid_spec=pltpu.PrefetchScalarGridSpec(
            num_scalar_prefetch=1,
            grid=(t // TM_FN,),
            in_specs=[
                pl.BlockSpec(memory_space=pl.ANY),
                pl.BlockSpec((TM_FN, TOP_K), row),
                pl.BlockSpec((TM_FN, D_MODEL), row),
                pl.BlockSpec((D_MODEL, SHARED_DIM), const),
                pl.BlockSpec((D_MODEL, SHARED_DIM), const),
                pl.BlockSpec((SHARED_DIM, D_MODEL), const),
                pl.BlockSpec((1, D_MODEL), const),
                pl.BlockSpec((1, D_MODEL), const),
            ],
            out_specs=pl.BlockSpec((TM_FN, D_MODEL), row),
            scratch_shapes=[pltpu.VMEM((TOP_K, TM_FN * ROW_TILE, LANES), U32), pltpu.SemaphoreType.DMA],
        ),
        out_shape=jax.ShapeDtypeStruct((t, D_MODEL), F32),
        compiler_params=_cparams(("arbitrary",)),
        name="combine_ln2",
    )(dest_flat, ys, w_tok, h1, ws1, ws3, ws2, g, b)


def _dup_heads(w, heads, dim):
    d = w.shape[0]
    w = w.reshape(d, heads, 1, dim)
    return jnp.broadcast_to(w, (d, heads, 2, dim)).reshape(d, heads * 2 * dim)


def _prep_w_in(w_in):
    sizes = (SSD_D_INNER, SSD_D_INNER + BC_DIM, SSD_HEADS, SWA_Q_DIM, SWA_KV_DIM, SWA_KV_DIM, XA_DIM,
             N_BRANCH * D_MODEL)
    pts = []
    acc = 0
    for s in sizes[:-1]:
        acc += s
        pts.append(acc)
    z, xbc, dt, q_s, k_s, v_s, q_x, gates = jnp.split(w_in, pts, axis=1)
    wp = jnp.concatenate(
        [gates, z, xbc, q_s, _dup_heads(k_s, SWA_KV_HEADS, SWA_HEAD_DIM),
         _dup_heads(v_s, SWA_KV_HEADS, SWA_HEAD_DIM), q_x], axis=1).astype(BF16)
    wdt = jnp.pad(dt, ((0, 0), (0, LANES - SSD_HEADS))).astype(BF16)
    return wp, wdt


def _pad_lanes(v):
    return jnp.pad(v, (0, LANES - v.shape[0]))[None, :].astype(F32)


def _layer(h2d, mem2d, batch, seq, w_in, conv_w, conv_b, dt_bias, a_log, d_skip, ssd_norm_g, swa_sinks, rel_bias,
           w_mem_kv, w_ssd_o, w_swa_o, w_xa_o, w_out, ln1_g, ln1_b, router_w, router_bias, w1, w3, w2, ws1, ws3,
           ws2, ln2_g, ln2_b, alpha):
    t = batch * seq
    wp, wdt = _prep_w_in(w_in)
    p, dt_raw = _inproj(h2d, wp, wdt)

    y_ssd = _ssd(
        p, dt_raw,
        conv_w[:, :SSD_D_INNER], conv_b[None, :SSD_D_INNER], conv_w[:, SSD_D_INNER:], conv_b[None, SSD_D_INNER:],
        _pad_lanes(dt_bias), _pad_lanes(a_log),
        jnp.repeat(d_skip, SSD_HEAD_DIM)[None, :], ssd_norm_g[None, :], batch, seq)

    y_swa = _swa(p, swa_sinks.astype(F32), _swa_bias_table(rel_bias), batch, seq)

    memkv = _mm(mem2d, w_mem_kv.astype(BF16), BF16, batch * MEM_LEN, 1024)
    y_xa = _xa(p, memkv, batch, seq)

    merged = _merge(y_ssd, y_swa, y_xa, w_ssd_o.astype(BF16), w_swa_o.astype(BF16), w_xa_o.astype(BF16), p)
    h1, hp = _out_ln(merged, w_out.astype(BF16), h2d, ln1_g[None, :], ln1_b[None, :], alpha)

    rwt = router_w.T
    r_hi = rwt.astype(BF16)
    r_lo = (rwt - r_hi.astype(F32)).astype(BF16)
    ti = jnp.arange(TM_RT)
    su = (ti[:, None] < ti[None, :]).astype(BF16)
    idx_t, w_t, rank_t, cnt = _router(h1, r_hi, r_lo, router_bias[:, None].astype(F32), su)

    counts = cnt[:, 0].astype(I32)
    padded = (counts + MOE_BLK - 1) // MOE_BLK * MOE_BLK
    pend = jnp.cumsum(padded)
    pstart = pend - padded
    n_blocks = t * TOP_K // MOE_BLK + N_EXPERTS
    blk_lo = jnp.arange(n_blocks, dtype=I32) * MOE_BLK
    block_e = jnp.minimum(jnp.sum((pend[None, :] <= blk_lo[:, None]).astype(I32), axis=1), N_EXPERTS - 1)
    of_e = block_e[:, None] == jnp.arange(N_EXPERTS, dtype=I32)[None, :]
    valid_end = jnp.sum(jnp.where(of_e, (pstart + counts)[None, :], 0), axis=1)
    block_nv = jnp.clip(valid_end - blk_lo, 0, MOE_BLK).astype(I32)
    eids = jnp.arange(N_EXPERTS, dtype=I32)
    later_active = (eids[None, :] > eids[:, None]) & (padded > 0)[None, :]
    next_active = jnp.min(jnp.where(later_active, eids[None, :], N_EXPERTS), axis=1)
    next_active = jnp.where(next_active >= N_EXPERTS, -1, next_active)
    seg_start = jnp.sum(jnp.where(of_e, pstart[None, :], 0), axis=1)
    block_first = ((blk_lo == seg_start) & (block_nv > 0)).astype(I32)
    block_next = jnp.sum(jnp.where(of_e, next_active[None, :], 0), axis=1).astype(I32)

    dest_flat = _dest(idx_t, rank_t, pstart.astype(F32)[:, None]).reshape(-1)
    n_slots = n_blocks * MOE_BLK
    x_sorted = _dispatch(dest_flat, hp, n_slots)
    ys = _experts(block_e, block_nv, block_first, block_next, x_sorted.reshape(n_slots * ROW_TILE, LANES),
                  w1, w3, w2)
    return _final(dest_flat, ys.reshape(n_slots, ROW_TILE, LANES), w_t.T, h1, ws1.astype(BF16), ws3.astype(BF16), ws2.astype(BF16), ln2_g[None, :],
                  ln2_b[None, :], alpha)


def kernel(x, mem, w_in, conv_w, conv_b, dt_bias, a_log, d_skip, ssd_norm_g, swa_sinks, rel_bias, w_mem_kv, w_ssd_o, w_swa_o, w_xa_o, w_out, ln1_g, ln1_b, router_w, router_bias, w1, w3, w2, ws1, ws3, ws2, ln2_g, ln2_b):
    batch, seq, d = x.shape
    depth = w_in.shape[0]
    alpha = (2.0 * depth) ** 0.25
    h = x.reshape(batch * seq, d)
    mem2d = mem.reshape(batch * MEM_LEN, d)
    for i in range(depth):
        h = _layer(h, mem2d, batch, seq, w_in[i], conv_w[i], conv_b[i], dt_bias[i], a_log[i], d_skip[i],
                   ssd_norm_g[i], swa_sinks[i], rel_bias, w_mem_kv[i], w_ssd_o[i], w_swa_o[i], w_xa_o[i], w_out[i],
                   ln1_g[i], ln1_b[i], router_w[i], router_bias[i], w1[i], w3[i], w2[i], ws1[i], ws3[i], ws2[i],
                   ln2_g[i], ln2_b[i], alpha)
    return h.reshape(batch, seq, d)
```

```python
import functools
import math

import jax
import jax.numpy as jnp
from jax import lax
from jax.experimental import pallas as pl
from jax.experimental.pallas import tpu as pltpu

F32 = jnp.float32
BF16 = jnp.bfloat16
I32 = jnp.int32
U32 = jnp.uint32

D_MODEL = 2048
MEM_LEN = 256
SSD_D_INNER = D_MODEL
SSD_HEAD_DIM = 64
SSD_HEADS = 32
SSD_GROUPS = 4
SSD_STATE = 128
SSD_CONV = 4
SSD_CHUNK = 128
SWA_HEADS = 16
SWA_KV_HEADS = 4
SWA_HEAD_DIM = 64
SWA_WINDOW = 128
SWA_BLOCK = 128
REL_BUCKETS = 32
REL_MAX_DIST = 128
XA_HEADS = 4
XA_HEAD_DIM = 256
N_BRANCH = 3
N_EXPERTS = 64
TOP_K = 8
N_EXPERT_GROUPS = 8
TOPK_GROUPS = 4
EXPERT_DIM = 512
SHARED_DIM = 512
ROUTED_SCALE = 2.5
LN_EPS = 1e-5
RMS_EPS = 1e-5

SWA_Q_DIM = SWA_HEADS * SWA_HEAD_DIM
SWA_KV_DIM = SWA_KV_HEADS * SWA_HEAD_DIM
XA_DIM = XA_HEADS * XA_HEAD_DIM
BC_DIM = 2 * SSD_GROUPS * SSD_STATE

OFF_GATES = 0
OFF_Z = OFF_GATES + N_BRANCH * D_MODEL
OFF_XS = OFF_Z + SSD_D_INNER
OFF_BC = OFF_XS + SSD_D_INNER
OFF_Q = OFF_BC + BC_DIM
OFF_K = OFF_Q + SWA_Q_DIM
OFF_V = OFF_K + 2 * SWA_KV_DIM
OFF_QX = OFF_V + 2 * SWA_KV_DIM
NP_TOTAL = OFF_QX + XA_DIM

LANES = 128
MOE_BLK = 256
NEG_INF = float("-inf")


def _cparams(sem, vmem_mb=48):
    return pltpu.CompilerParams(dimension_semantics=sem, vmem_limit_bytes=vmem_mb * 1024 * 1024)


def _dot(a, b):
    return jnp.dot(a, b, preferred_element_type=F32)


def _dot_nt(a, b):
    return lax.dot_general(a, b, (((1,), (1,)), ((), ())), preferred_element_type=F32)


def _sigmoid(x):
    return 1.0 / (1.0 + jnp.exp(-x))


def _silu(x):
    return x * _sigmoid(x)


TM_IN = 1024
TN_IN = 1024


def _inproj_kernel(x_ref, w_ref, wdt_ref, p_ref, dt_ref, xb_ref):
    @pl.when(pl.program_id(1) == 0)
    def _():
        xb = x_ref[...].astype(BF16)
        xb_ref[...] = xb
        dt_ref[...] = _dot(xb, wdt_ref[...])

    p_ref[...] = _dot(xb_ref[...], w_ref[...]).astype(BF16)


def _inproj(x2d, wp, wdt):
    t = x2d.shape[0]
    return pl.pallas_call(
        _inproj_kernel,
        grid=(t // TM_IN, NP_TOTAL // TN_IN),
        in_specs=[
            pl.BlockSpec((TM_IN, D_MODEL), lambda i, j: (i, 0)),
            pl.BlockSpec((D_MODEL, TN_IN), lambda i, j: (0, j)),
            pl.BlockSpec((D_MODEL, LANES), lambda i, j: (0, 0)),
        ],
        out_specs=[
            pl.BlockSpec((TM_IN, TN_IN), lambda i, j: (i, j)),
            pl.BlockSpec((TM_IN, LANES), lambda i, j: (i, 0)),
        ],
        out_shape=[jax.ShapeDtypeStruct((t, NP_TOTAL), BF16), jax.ShapeDtypeStruct((t, LANES), F32)],
        scratch_shapes=[pltpu.VMEM((TM_IN, D_MODEL), BF16)],
        compiler_params=_cparams(("parallel", "arbitrary")),
        name="inproj",
    )(x2d, wp, wdt)


def _mm_kernel(a_ref, b_ref, o_ref):
    o_ref[...] = _dot(a_ref[...].astype(BF16), b_ref[...]).astype(o_ref.dtype)


def _mm(a, b, out_dtype, tm, tn):
    m, k = a.shape
    n = b.shape[1]
    return pl.pallas_call(
        _mm_kernel,
        grid=(m // tm, n // tn),
        in_specs=[pl.BlockSpec((tm, k), lambda i, j: (i, 0)), pl.BlockSpec((k, tn), lambda i, j: (0, j))],
        out_specs=pl.BlockSpec((tm, tn), lambda i, j: (i, j)),
        out_shape=jax.ShapeDtypeStruct((m, n), out_dtype),
        compiler_params=_cparams(("parallel", "parallel")),
        name="mm",
    )(a, b)


Q = SSD_CHUNK


def _conv_shift_matrix():
    t = jnp.arange(Q)[:, None]
    j = jnp.arange(2 * Q)[None, :]
    blocks = []
    for k in range(SSD_CONV - 1):
        src = t - (SSD_CONV - 1 - k)
        blocks.append(j == jnp.where(src >= 0, src, 2 * Q + src))
    return jnp.concatenate(blocks, axis=0).astype(BF16)


def _ssd_kernel(xs_ref, bc_ref, z_ref, dtr_ref, shift_ref, cwx_ref, cbx_ref, cwb_ref, cbb_ref, dtb_ref, alog_ref,
                dsk_ref, ng_ref, o_ref, hx_ref, hb_ref, st_ref, cumt_ref, dtt_ref, wt_ref, y_ref):
    c = pl.program_id(1)

    @pl.when(c == 0)
    def _():
        hx_ref[...] = jnp.zeros_like(hx_ref)
        hb_ref[...] = jnp.zeros_like(hb_ref)
        st_ref[...] = jnp.zeros_like(st_ref)

    def conv_silu(in_ref, prev_ref, w_ref, b_ref):
        cur = in_ref[...]
        ext = jnp.concatenate([cur, prev_ref[...]], axis=0)
        shifted = _dot(shift_ref[...], ext)
        acc = b_ref[...] + w_ref[SSD_CONV - 1:SSD_CONV, :] * cur.astype(F32)
        for k in range(SSD_CONV - 1):
            acc = acc + w_ref[k:k + 1, :] * shifted[k * Q:(k + 1) * Q, :]
        prev_ref[...] = cur
        return _silu(acc)

    xs_c = conv_silu(xs_ref, hx_ref, cwx_ref, cbx_ref)
    bc_c = conv_silu(bc_ref, hb_ref, cwb_ref, cbb_ref)

    lane = lax.broadcasted_iota(I32, (Q, LANES), 1)
    row = lax.broadcasted_iota(I32, (Q, LANES), 0)
    tril = row >= lane
    lo_mask = lane < SSD_HEAD_DIM

    dtv = dtr_ref[...] + dtb_ref[...]
    dt = jnp.maximum(dtv, 0.0) + jnp.log1p(jnp.exp(-jnp.abs(dtv)))
    a = jnp.where(lane[0:1, :] < SSD_HEADS, -jnp.exp(alog_ref[...]), 0.0)
    la = dt * a
    tri = jnp.where(tril, 1.0, 0.0).astype(BF16)
    p1 = la.astype(BF16)
    r1 = la - p1.astype(F32)
    p2 = r1.astype(BF16)
    p3 = (r1 - p2.astype(F32)).astype(BF16)
    cum = _dot(tri, p1) + _dot(tri, p2) + _dot(tri, p3)
    ecum = jnp.exp(cum)
    cum_t = cum.T
    dt_t = dt.T
    last = cum_t[:, Q - 1:Q]
    cumt_ref[...] = cum_t
    dtt_ref[...] = dt_t
    wt_ref[...] = jnp.exp(last - cum_t) * dt_t
    chunk_decay = jnp.exp(last)

    def head_parts(h, cb, cm, bm_t):
        col = jnp.broadcast_to(cum[:, h:h + 1], (Q, Q))
        rw = cumt_ref[h:h + 1, :]
        dec = jnp.exp(jnp.where(tril, col - rw, NEG_INF))
        m = cb * (dec * dtt_ref[h:h + 1, :])
        ce = cm * jnp.broadcast_to(ecum[:, h:h + 1], (Q, Q))
        sl = bm_t * wt_ref[h:h + 1, :]
        return m.astype(BF16), ce.astype(BF16), sl.astype(BF16)

    for g in range(SSD_GROUPS):
        bm = bc_c[:, g * SSD_STATE:(g + 1) * SSD_STATE]
        cm = bc_c[:, BC_DIM // 2 + g * SSD_STATE:BC_DIM // 2 + (g + 1) * SSD_STATE]
        cb = _dot_nt(cm.astype(BF16), bm.astype(BF16))
        bm_t = bm.T
        for qd in range(SSD_HEADS // SSD_GROUPS // 2):
            hp = g * (SSD_HEADS // SSD_GROUPS // 2) + qd
            ha, hb = 2 * hp, 2 * hp + 1
            sl_ = slice(hp * LANES, (hp + 1) * LANES)
            xs_pair = xs_c[:, sl_]
            x_lo = jnp.where(lo_mask, xs_pair, 0.0).astype(BF16)
            x_hi = jnp.where(lo_mask, 0.0, xs_pair).astype(BF16)
            h_prev = st_ref[:, sl_]
            h_lo = jnp.where(lo_mask, h_prev, 0.0).astype(BF16)
            h_hi = jnp.where(lo_mask, 0.0, h_prev).astype(BF16)
            m_a, ce_a, sl_a = head_parts(ha, cb, cm, bm_t)
            m_b, ce_b, sl_b = head_parts(hb, cb, cm, bm_t)
            lhs = jnp.concatenate([m_a, ce_a, m_b, ce_b], axis=1)
            rhs = jnp.concatenate([x_lo, h_lo, x_hi, h_hi], axis=0)
            y_ref[:, sl_] = _dot(lhs, rhs)
            new = _dot(jnp.concatenate([sl_a, sl_b], axis=1), jnp.concatenate([x_lo, x_hi], axis=0))
            cd = jnp.where(lo_mask[0:1, :], chunk_decay[ha:ha + 1, :], chunk_decay[hb:hb + 1, :])
            st_ref[:, sl_] = h_prev * cd + new

    y = y_ref[...] + xs_c * dsk_ref[...]
    y = y * _silu(z_ref[...].astype(F32))
    gw = SSD_D_INNER // SSD_GROUPS
    for g in range(SSD_GROUPS):
        yg = y[:, g * gw:(g + 1) * gw]
        ms = jnp.mean(yg * yg, axis=-1, keepdims=True)
        o_ref[:, g * gw:(g + 1) * gw] = ((yg * lax.rsqrt(ms + RMS_EPS)) * ng_ref[:, g * gw:(g + 1) * gw]).astype(BF16)


def _ssd(p, dt_raw, cwx, cbx, cwb, cbb, dtb, alog, dsk, ng, batch, seq):
    nc = seq // Q
    t = batch * seq
    rowblk = lambda b, c: b * nc + c
    const = lambda b, c: (0, 0)
    return pl.pallas_call(
        _ssd_kernel,
        grid=(batch, nc),
        in_specs=[
            pl.BlockSpec((Q, SSD_D_INNER), lambda b, c: (rowblk(b, c), OFF_XS // SSD_D_INNER)),
            pl.BlockSpec((Q, BC_DIM), lambda b, c: (rowblk(b, c), OFF_BC // BC_DIM)),
            pl.BlockSpec((Q, SSD_D_INNER), lambda b, c: (rowblk(b, c), OFF_Z // SSD_D_INNER)),
            pl.BlockSpec((Q, LANES), lambda b, c: (rowblk(b, c), 0)),
            pl.BlockSpec(((SSD_CONV - 1) * Q, 2 * Q), const),
            pl.BlockSpec((SSD_CONV, SSD_D_INNER), const),
            pl.BlockSpec((1, SSD_D_INNER), const),
            pl.BlockSpec((SSD_CONV, BC_DIM), const),
            pl.BlockSpec((1, BC_DIM), const),
            pl.BlockSpec((1, LANES), const),
            pl.BlockSpec((1, LANES), const),
            pl.BlockSpec((1, SSD_D_INNER), const),
            pl.BlockSpec((1, SSD_D_INNER), const),
        ],
        out_specs=pl.BlockSpec((Q, SSD_D_INNER), lambda b, c: (rowblk(b, c), 0)),
        out_shape=jax.ShapeDtypeStruct((t, SSD_D_INNER), BF16),
        scratch_shapes=[
            pltpu.VMEM((Q, SSD_D_INNER), BF16),
            pltpu.VMEM((Q, BC_DIM), BF16),
            pltpu.VMEM((SSD_STATE, SSD_D_INNER), F32),
            pltpu.VMEM((LANES, Q), F32),
            pltpu.VMEM((LANES, Q), F32),
            pltpu.VMEM((LANES, Q), F32),
            pltpu.VMEM((Q, SSD_D_INNER), F32),
        ],
        compiler_params=_cparams(("parallel", "arbitrary")),
        name="ssd",
    )(p, p, p, dt_raw, _conv_shift_matrix(), cwx, cbx, cwb, cbb, dtb, alog, dsk, ng)


N_PAIRS = SWA_HEADS // 2


def _swa_kernel(sink_ref, q_ref, kp_ref, kc_ref, vp_ref, vc_ref, bias_ref, o_ref):
    n = pl.program_id(1)
    blk = SWA_BLOCK
    kband = jnp.concatenate([kp_ref[...], kc_ref[...]], axis=0)
    vband = jnp.concatenate([vp_ref[...], vc_ref[...]], axis=0)
    lane = lax.broadcasted_iota(I32, (1, LANES), 1)
    lo_f = jnp.where(lane < SWA_HEAD_DIM, 1.0, 0.0)
    lo_m = lo_f.astype(BF16)
    hi_m = (1.0 - lo_f).astype(BF16)
    colj = lax.broadcasted_iota(I32, (2 * blk, 2 * blk), 1)
    rowi = lax.broadcasted_iota(I32, (2 * blk, 1), 0)
    valid = (colj >= blk) | (n > 0)
    scale = SWA_HEAD_DIM ** -0.5
    for p in range(N_PAIRS):
        kv = p // (SWA_HEADS // SWA_KV_HEADS // 2)
        qp = q_ref[:, p * LANES:(p + 1) * LANES]
        qs = jnp.concatenate([qp * lo_m, qp * hi_m], axis=0)
        kd = kband[:, kv * LANES:(kv + 1) * LANES]
        s = _dot_nt(qs, kd) * scale + bias_ref[p]
        s = jnp.where(valid, s, NEG_INF)
        sink = jnp.where(rowi < blk, sink_ref[2 * p], sink_ref[2 * p + 1])
        m = jnp.maximum(jnp.max(s, axis=-1, keepdims=True), sink)
        e = jnp.exp(s - m)
        den = jnp.sum(e, axis=-1, keepdims=True) + jnp.exp(sink - m)
        pr = (e * (1.0 / den)).astype(BF16)
        vd = vband[:, kv * LANES:(kv + 1) * LANES]
        lhs = jnp.concatenate([pr[:blk], pr[blk:]], axis=1)
        rhs = jnp.concatenate([vd * lo_m, vd * hi_m], axis=0)
        o_ref[:, p * LANES:(p + 1) * LANES] = _dot(lhs, rhs).astype(BF16)


def _swa(p, sinks, bias, batch, seq):
    nb = seq // SWA_BLOCK
    t = batch * seq
    kw = 2 * SWA_KV_DIM
    cur = lambda b, n, s: b * nb + n
    prev = lambda b, n, s: b * nb + jnp.maximum(n - 1, 0)
    return pl.pallas_call(
        _swa_kernel,
        grid_spec=pltpu.PrefetchScalarGridSpec(
            num_scalar_prefetch=1,
            grid=(batch, nb),
            in_specs=[
                pl.BlockSpec((SWA_BLOCK, SWA_Q_DIM), lambda b, n, s: (cur(b, n, s), OFF_Q // SWA_Q_DIM)),
                pl.BlockSpec((SWA_BLOCK, kw), lambda b, n, s: (prev(b, n, s), OFF_K // kw)),
                pl.BlockSpec((SWA_BLOCK, kw), lambda b, n, s: (cur(b, n, s), OFF_K // kw)),
                pl.BlockSpec((SWA_BLOCK, kw), lambda b, n, s: (prev(b, n, s), OFF_V // kw)),
                pl.BlockSpec((SWA_BLOCK, kw), lambda b, n, s: (cur(b, n, s), OFF_V // kw)),
                pl.BlockSpec((N_PAIRS, 2 * SWA_BLOCK, 2 * SWA_BLOCK), lambda b, n, s: (0, 0, 0)),
            ],
            out_specs=pl.BlockSpec((SWA_BLOCK, SWA_Q_DIM), lambda b, n, s: (cur(b, n, s), 0)),
        ),
        out_shape=jax.ShapeDtypeStruct((t, SWA_Q_DIM), BF16),
        compiler_params=_cparams(("parallel", "arbitrary")),
        name="swa",
    )(sinks, p, p, p, p, p, bias)


def _t5_causal_bucket(dist):
    max_exact = REL_BUCKETS // 2
    large = max_exact + (jnp.log(jnp.maximum(dist, 1).astype(F32) / max_exact)
                         / math.log(REL_MAX_DIST / max_exact) * (REL_BUCKETS - max_exact)).astype(I32)
    large = jnp.minimum(large, REL_BUCKETS - 1)
    return jnp.where(dist < max_exact, dist, large)


def _swa_bias_table(rel_bias):
    qi = jnp.arange(SWA_BLOCK)[:, None]
    kj = jnp.arange(2 * SWA_BLOCK)[None, :]
    dist = qi + SWA_BLOCK - kj
    in_window = (dist >= 0) & (dist < SWA_WINDOW)
    bucket = _t5_causal_bucket(jnp.maximum(dist, 0))
    onehot = (bucket[:, :, None] == jnp.arange(REL_BUCKETS)[None, None, :]).astype(F32)
    bias = jnp.einsum('ijb,bh->hij', onehot, rel_bias.astype(F32), precision=lax.Precision.HIGHEST)
    bias = jnp.where(in_window[None, :, :], bias, NEG_INF)
    return bias.reshape(N_PAIRS, 2 * SWA_BLOCK, 2 * SWA_BLOCK)


TL_XA = 512


def _xa_kernel(q_ref, mk_ref, mv_ref, o_ref):
    s = _dot_nt(q_ref[...], mk_ref[...]) * (XA_HEAD_DIM ** -0.5)
    m = jnp.max(s, axis=-1, keepdims=True)
    e = jnp.exp(s - m)
    pr = e * (1.0 / jnp.sum(e, axis=-1, keepdims=True))
    o_ref[...] = _dot(pr.astype(BF16), mv_ref[...]).astype(BF16)


def _xa(p, memkv, batch, seq):
    nl = seq // TL_XA
    t = batch * seq
    return pl.pallas_call(
        _xa_kernel,
        grid=(batch, nl, XA_HEADS),
        in_specs=[
            pl.BlockSpec((TL_XA, XA_HEAD_DIM), lambda b, l, h: (b * nl + l, OFF_QX // XA_HEAD_DIM + h)),
            pl.BlockSpec((MEM_LEN, XA_HEAD_DIM), lambda b, l, h: (b, h)),
            pl.BlockSpec((MEM_LEN, XA_HEAD_DIM), lambda b, l, h: (b, XA_HEADS + h)),
        ],
        out_specs=pl.BlockSpec((TL_XA, XA_HEAD_DIM), lambda b, l, h: (b * nl + l, h)),
        out_shape=jax.ShapeDtypeStruct((t, XA_DIM), BF16),
        compiler_params=_cparams(("parallel", "parallel", "parallel")),
        name="xattn",
    )(p, memkv, memkv)


TM_MG = 1024
TN_MG = 512


def _merge_kernel(ys_ref, yw_ref, yx_ref, ws_ref, ww_ref, wx_ref, g0_ref, g1_ref, g2_ref, o_ref):
    a = _dot(ys_ref[...], ws_ref[...])
    b = _dot(yw_ref[...], ww_ref[...])
    c = _dot(yx_ref[...], wx_ref[...])
    g0 = _sigmoid(g0_ref[...].astype(F32))
    g1 = _sigmoid(g1_ref[...].astype(F32))
    g2 = _sigmoid(g2_ref[...].astype(F32))
    o_ref[...] = (g0 * a + g1 * b + g2 * c).astype(BF16)


def _merge(y_ssd, y_swa, y_xa, w_ssd_o, w_swa_o, w_xa_o, p):
    t = y_ssd.shape[0]
    gpb = D_MODEL // TN_MG
    gate = lambda k: pl.BlockSpec((TM_MG, TN_MG), lambda i, j: (i, OFF_GATES // TN_MG + k * gpb + j))
    return pl.pallas_call(
        _merge_kernel,
        grid=(t // TM_MG, D_MODEL // TN_MG),
        in_specs=[
            pl.BlockSpec((TM_MG, SSD_D_INNER), lambda i, j: (i, 0)),
            pl.BlockSpec((TM_MG, SWA_Q_DIM), lambda i, j: (i, 0)),
            pl.BlockSpec((TM_MG, XA_DIM), lambda i, j: (i, 0)),
            pl.BlockSpec((SSD_D_INNER, TN_MG), lambda i, j: (0, j)),
            pl.BlockSpec((SWA_Q_DIM, TN_MG), lambda i, j: (0, j)),
            pl.BlockSpec((XA_DIM, TN_MG), lambda i, j: (0, j)),
            gate(0), gate(1), gate(2),
        ],
        out_specs=pl.BlockSpec((TM_MG, TN_MG), lambda i, j: (i, j)),
        out_shape=jax.ShapeDtypeStruct((t, D_MODEL), BF16),
        compiler_params=_cparams(("parallel", "parallel")),
        name="merge",
    )(y_ssd, y_swa, y_xa, w_ssd_o, w_swa_o, w_xa_o, p, p, p)


TM_LN = 512


def _layer_norm(v, g, b):
    mu = jnp.mean(v, axis=-1, keepdims=True)
    d = v - mu
    var = jnp.mean(d * d, axis=-1, keepdims=True)
    return (d * lax.rsqrt(var + LN_EPS)) * g + b


def _pack_halves(v):
    n = v.shape[1] // 2
    lo = lax.bitcast_convert_type(v[:, :n].astype(BF16).astype(F32), U32) >> 16
    hi = lax.bitcast_convert_type(v[:, n:].astype(BF16).astype(F32), U32) & jnp.uint32(0xFFFF0000)
    return hi | lo


def _unpack_halves(w):
    lo = lax.bitcast_convert_type(w << 16, F32)
    hi = lax.bitcast_convert_type(w & jnp.uint32(0xFFFF0000), F32)
    return lo, hi


ROW_WORDS = D_MODEL // 2
ROW_TILE = ROW_WORDS // LANES


def _store_row_tiled(ref, words):
    m = words.shape[0]
    for c in range(ROW_TILE):
        ref[pl.ds(c, m, stride=ROW_TILE), :] = words[:, c * LANES:(c + 1) * LANES]


def _load_row_tiled(ref, m):
    return jnp.concatenate([ref[pl.ds(c, m, stride=ROW_TILE), :] for c in range(ROW_TILE)], axis=1)


def _out_ln_kernel(alpha, m_ref, w_ref, x_ref, g_ref, b_ref, h_ref, hp_ref):
    mix = _dot(m_ref[...], w_ref[...])
    hn = _layer_norm(alpha * x_ref[...] + mix, g_ref[...], b_ref[...])
    h_ref[...] = hn
    _store_row_tiled(hp_ref, _pack_halves(hn))


def _out_ln(merged, w_out, x2d, g, b, alpha):
    t = x2d.shape[0]
    row = lambda i: (i, 0)
    const = lambda i: (0, 0)
    return pl.pallas_call(
        functools.partial(_out_ln_kernel, alpha),
        grid=(t // TM_LN,),
        in_specs=[
            pl.BlockSpec((TM_LN, D_MODEL), row),
            pl.BlockSpec((D_MODEL, D_MODEL), const),
            pl.BlockSpec((TM_LN, D_MODEL), row),
            pl.BlockSpec((1, D_MODEL), const),
            pl.BlockSpec((1, D_MODEL), const),
        ],
        out_specs=[pl.BlockSpec((TM_LN, D_MODEL), row), pl.BlockSpec((TM_LN * ROW_TILE, LANES), row)],
        out_shape=[jax.ShapeDtypeStruct((t, D_MODEL), F32), jax.ShapeDtypeStruct((t * ROW_TILE, LANES), U32)],
        compiler_params=_cparams(("parallel",)),
        name="out_ln1",
    )(merged, w_out, x2d, g, b)


TM_RT = 512
EPG = N_EXPERTS // N_EXPERT_GROUPS


def _router_kernel(h_ref, whi_ref, wlo_ref, bias_ref, su_ref, idx_ref, w_ref, rank_ref, cnt_ref, run_ref):
    @pl.when(pl.program_id(0) == 0)
    def _():
        run_ref[...] = jnp.zeros_like(run_ref)

    h = h_ref[...]
    h_hi = h.astype(BF16)
    h_lo = (h - h_hi.astype(F32)).astype(BF16)
    whi = whi_ref[...]
    logits = _dot_nt(whi, h_hi) + _dot_nt(whi, h_lo) + _dot_nt(wlo_ref[...], h_hi)
    scores = _sigmoid(logits)
    sel = scores + bias_ref[...]

    iota_m = lax.broadcasted_iota(I32, (EPG, TM_RT), 0).astype(F32)
    iota_g = lax.broadcasted_iota(I32, (N_EXPERT_GROUPS, TM_RT), 0).astype(F32)
    gs = []
    for g in range(N_EXPERT_GROUPS):
        v = sel[g * EPG:(g + 1) * EPG, :]
        m1 = jnp.max(v, axis=0, keepdims=True)
        i1 = jnp.min(jnp.where(v == m1, iota_m, float(EPG)), axis=0, keepdims=True)
        m2 = jnp.max(jnp.where(iota_m == i1, NEG_INF, v), axis=0, keepdims=True)
        gs.append(m1 + m2)
    gsc = jnp.concatenate(gs, axis=0)
    chosen = jnp.zeros((N_EXPERT_GROUPS, TM_RT), F32)
    for _ in range(TOPK_GROUPS):
        m = jnp.max(gsc, axis=0, keepdims=True)
        ig = jnp.min(jnp.where(gsc == m, iota_g, float(N_EXPERT_GROUPS)), axis=0, keepdims=True)
        hit = iota_g == ig
        chosen = jnp.where(hit, 1.0, chosen)
        gsc = jnp.where(hit, NEG_INF, gsc)
    gmask = jnp.concatenate(
        [jnp.broadcast_to(chosen[g:g + 1, :], (EPG, TM_RT)) for g in range(N_EXPERT_GROUPS)], axis=0)
    masked = jnp.where(gmask > 0.0, sel, NEG_INF)

    iota_e = lax.broadcasted_iota(I32, (N_EXPERTS, TM_RT), 0).astype(F32)
    hits, idxs, ws = [], [], []
    for _ in range(TOP_K):
        m = jnp.max(masked, axis=0, keepdims=True)
        ie = jnp.min(jnp.where(masked == m, iota_e, float(N_EXPERTS)), axis=0, keepdims=True)
        hit = iota_e == ie
        hits.append(hit)
        idxs.append(ie)
        ws.append(jnp.sum(jnp.where(hit, scores, 0.0), axis=0, keepdims=True))
        masked = jnp.where(hit, NEG_INF, masked)
    wsum = ws[0]
    for k in range(1, TOP_K):
        wsum = wsum + ws[k]
    inv = 1.0 / wsum
    idx_ref[...] = jnp.concatenate(idxs, axis=0).astype(I32)
    w_ref[...] = jnp.concatenate([w * inv * ROUTED_SCALE for w in ws], axis=0)

    assign = jnp.zeros((N_EXPERTS, TM_RT), F32)
    for hit in hits:
        assign = jnp.where(hit, 1.0, assign)
    pos = _dot(assign.astype(BF16), su_ref[...]) + run_ref[...]
    rank_ref[...] = jnp.concatenate(
        [jnp.sum(jnp.where(hit, pos, 0.0), axis=0, keepdims=True) for hit in hits], axis=0).astype(I32)
    run = run_ref[...] + jnp.sum(assign, axis=1, keepdims=True)
    run_ref[...] = run
    cnt_ref[...] = run


def _router(h1, r_hi, r_lo, rbias, su):
    t = h1.shape[0]
    tile = lambda i: (0, i)
    const = lambda i: (0, 0)
    return pl.pallas_call(
        _router_kernel,
        grid=(t // TM_RT,),
        in_specs=[
            pl.BlockSpec((TM_RT, D_MODEL), lambda i: (i, 0)),
            pl.BlockSpec((N_EXPERTS, D_MODEL), const),
            pl.BlockSpec((N_EXPERTS, D_MODEL), const),
            pl.BlockSpec((N_EXPERTS, 1), const),
            pl.BlockSpec((TM_RT, TM_RT), const),
        ],
        out_specs=[
            pl.BlockSpec((TOP_K, TM_RT), tile),
            pl.BlockSpec((TOP_K, TM_RT), tile),
            pl.BlockSpec((TOP_K, TM_RT), tile),
            pl.BlockSpec((N_EXPERTS, TM_RT), const),
        ],
        out_shape=[
            jax.ShapeDtypeStruct((TOP_K, t), I32),
            jax.ShapeDtypeStruct((TOP_K, t), F32),
            jax.ShapeDtypeStruct((TOP_K, t), I32),
            jax.ShapeDtypeStruct((N_EXPERTS, TM_RT), F32),
        ],
        scratch_shapes=[pltpu.VMEM((N_EXPERTS, TM_RT), F32)],
        compiler_params=_cparams(("arbitrary",)),
        name="router",
    )(h1, r_hi, r_lo, rbias, su)


TM_DS = 2048


def _dest_kernel(idx_ref, rank_ref, ps_ref, d_ref):
    iota_e = lax.broadcasted_iota(I32, (N_EXPERTS, TM_DS), 0)
    ps = ps_ref[...]
    rows = []
    for k in range(TOP_K):
        hit = iota_e == idx_ref[k:k + 1, :]
        rows.append(jnp.sum(jnp.where(hit, ps, 0.0), axis=0, keepdims=True))
    d_ref[...] = jnp.concatenate(rows, axis=0).astype(I32) + rank_ref[...]


def _dest(idx_t, rank_t, pstart):
    t = idx_t.shape[1]
    tile = lambda i: (0, i)
    return pl.pallas_call(
        _dest_kernel,
        grid=(t // TM_DS,),
        in_specs=[pl.BlockSpec((TOP_K, TM_DS), tile), pl.BlockSpec((TOP_K, TM_DS), tile),
                  pl.BlockSpec((N_EXPERTS, 1), lambda i: (0, 0))],
        out_specs=pl.BlockSpec((TOP_K, TM_DS), tile),
        out_shape=jax.ShapeDtypeStruct((TOP_K, t), I32),
        compiler_params=_cparams(("parallel",)),
        name="dest",
    )(idx_t, rank_t, pstart)


TM_DP = 512


def _dispatch_kernel(dest_ref, hp_ref, h_ref, ws1_ref, ws3_ref, ws2_ref, xs_ref, sh_ref, sem):
    t_total = pl.num_programs(0) * TM_DP
    base = pl.program_id(0) * TM_DP

    def issue(t, carry):
        src = hp_ref.at[pl.ds(pl.multiple_of(t * ROW_TILE, ROW_TILE), ROW_TILE), :]
        for k in range(TOP_K):
            slot = dest_ref[k * t_total + base + t]
            pltpu.make_async_copy(src, xs_ref.at[slot], sem).start(priority=k % 2)
        return carry

    lax.fori_loop(0, TM_DP, issue, 0)

    xb = h_ref[...].astype(BF16)
    hdn = _silu(_dot(xb, ws1_ref[...])) * _dot(xb, ws3_ref[...])
    sh_ref[...] = _dot(hdn.astype(BF16), ws2_ref[...])

    for k in range(TOP_K):
        pltpu.make_async_copy(hp_ref.reshape(TM_DP, ROW_TILE, LANES), xs_ref.at[pl.ds(0, TM_DP)], sem).wait()


def _dispatch(dest_flat, hp, h1, ws1, ws3, ws2, n_slots):
    t = h1.shape[0]
    row = lambda i, d: (i, 0)
    const = lambda i, d: (0, 0)
    return pl.pallas_call(
        _dispatch_kernel,
        grid_spec=pltpu.PrefetchScalarGridSpec(
            num_scalar_prefetch=1,
            grid=(t // TM_DP,),
            in_specs=[
                pl.BlockSpec((TM_DP * ROW_TILE, LANES), row),
                pl.BlockSpec((TM_DP, D_MODEL), row),
                pl.BlockSpec((D_MODEL, SHARED_DIM), const),
                pl.BlockSpec((D_MODEL, SHARED_DIM), const),
                pl.BlockSpec((SHARED_DIM, D_MODEL), const),
            ],
            out_specs=[pl.BlockSpec(memory_space=pl.ANY), pl.BlockSpec((TM_DP, D_MODEL), row)],
            scratch_shapes=[pltpu.SemaphoreType.DMA],
        ),
        out_shape=[jax.ShapeDtypeStruct((n_slots, ROW_TILE, LANES), U32), jax.ShapeDtypeStruct((t, D_MODEL), F32)],
        compiler_params=_cparams(("arbitrary",)),
        name="dispatch",
    )(dest_flat, hp, h1, ws1, ws3, ws2)


N_WSLOTS = 3


def _expert_kernel(be_ref, nv_ref, first_ref, nxt_ref, nxt2_ref, x_ref, w1_ref, w3_ref, w2_ref, o_ref,
                   wf1, wf3, wf2, w1b, w3b, w2b, slot_ref, sems):
    b = pl.program_id(0)
    nv = nv_ref[b]

    def weight_copies(e, s):
        return (pltpu.make_async_copy(w1_ref.at[e], wf1.at[s], sems.at[s, 0]),
                pltpu.make_async_copy(w3_ref.at[e], wf3.at[s], sems.at[s, 1]),
                pltpu.make_async_copy(w2_ref.at[e], wf2.at[s], sems.at[s, 2]))

    @pl.when(b == 0)
    def _():
        slot_ref[0] = 0
        for cp in weight_copies(be_ref[0], 0):
            cp.start()

        @pl.when(nxt_ref[0] >= 0)
        def _():
            for cp in weight_copies(nxt_ref[0], 1):
                cp.start(priority=1)

    @pl.when(first_ref[b] == 1)
    def _():
        s = slot_ref[0]
        for cp in weight_copies(be_ref[b], s):
            cp.wait()

        @pl.when(nxt2_ref[b] >= 0)
        def _():
            s2 = jnp.where(s == 0, N_WSLOTS - 1, s - 1)
            for cp in weight_copies(nxt2_ref[b], s2):
                cp.start(priority=1)

        w1b[...] = wf1[s].astype(BF16)
        w3b[...] = wf3[s].astype(BF16)
        w2b[...] = wf2[s].astype(BF16)
        slot_ref[0] = jnp.where(s == N_WSLOTS - 1, 0, s + 1)

    @pl.when(nv > 0)
    def _():
        live = lax.broadcasted_iota(I32, (MOE_BLK, 1), 0) < nv
        lo, hi = _unpack_halves(jnp.where(live, _load_row_tiled(x_ref, MOE_BLK), jnp.uint32(0)))
        x = jnp.concatenate([lo.astype(BF16), hi.astype(BF16)], axis=1)
        hdn = _silu(_dot(x, w1b[...])) * _dot(x, w3b[...])
        _store_row_tiled(o_ref, _pack_halves(_dot(hdn.astype(BF16), w2b[...])))

    @pl.when(nv <= 0)
    def _():
        o_ref[...] = jnp.zeros_like(o_ref)


def _experts(block_e, block_nv, block_first, block_next, block_next2, x_sorted, w1, w3, w2):
    n_rows = x_sorted.shape[0]
    n_blocks = n_rows // (MOE_BLK * ROW_TILE)
    row = lambda b, *_: (b, 0)
    return pl.pallas_call(
        _expert_kernel,
        grid_spec=pltpu.PrefetchScalarGridSpec(
            num_scalar_prefetch=5,
            grid=(n_blocks,),
            in_specs=[
                pl.BlockSpec((MOE_BLK * ROW_TILE, LANES), row),
                pl.BlockSpec(memory_space=pl.ANY),
                pl.BlockSpec(memory_space=pl.ANY),
                pl.BlockSpec(memory_space=pl.ANY),
            ],
            out_specs=pl.BlockSpec((MOE_BLK * ROW_TILE, LANES), row),
            scratch_shapes=[
                pltpu.VMEM((N_WSLOTS, D_MODEL, EXPERT_DIM), F32),
                pltpu.VMEM((N_WSLOTS, D_MODEL, EXPERT_DIM), F32),
                pltpu.VMEM((N_WSLOTS, EXPERT_DIM, D_MODEL), F32),
                pltpu.VMEM((D_MODEL, EXPERT_DIM), BF16),
                pltpu.VMEM((D_MODEL, EXPERT_DIM), BF16),
                pltpu.VMEM((EXPERT_DIM, D_MODEL), BF16),
                pltpu.SMEM((1,), I32),
                pltpu.SemaphoreType.DMA((N_WSLOTS, 3)),
            ],
        ),
        out_shape=jax.ShapeDtypeStruct((n_rows, LANES), U32),
        compiler_params=_cparams(("arbitrary",), vmem_mb=58),
        name="experts",
    )(block_e, block_nv, block_first, block_next, block_next2, x_sorted, w1, w3, w2)


TM_FN = 256


def _final_kernel(alpha, dest_ref, ys_ref, w_ref, h_ref, sh_ref, g_ref, b_ref, o_ref, buf, sems):
    i = pl.program_id(0)
    n_tiles = pl.num_programs(0)
    t_total = n_tiles * TM_FN

    def issue_tile(tile, slot):
        base = tile * TM_FN

        def issue(t, carry):
            row0 = pl.multiple_of(t * ROW_TILE, ROW_TILE)
            for k in range(TOP_K):
                src = ys_ref.at[dest_ref[k * t_total + base + t]]
                pltpu.make_async_copy(src, buf.at[slot, k, pl.ds(row0, ROW_TILE), :], sems.at[slot]).start(
                    priority=k % 2)
            return carry

        lax.fori_loop(0, TM_FN, issue, 0)

    @pl.when(i == 0)
    def _():
        issue_tile(0, 0)

    @pl.when(i + 1 < n_tiles)
    def _():
        issue_tile(i + 1, (i + 1) % 2)

    slot = i % 2
    for k in range(TOP_K):
        pltpu.make_async_copy(ys_ref.at[pl.ds(0, TM_FN)], buf.at[slot, k].reshape(TM_FN, ROW_TILE, LANES),
                              sems.at[slot]).wait()
    half = D_MODEL // 2
    acc_lo = sh_ref[:, :half]
    acc_hi = sh_ref[:, half:]
    for k in range(TOP_K):
        lo, hi = _unpack_halves(_load_row_tiled(buf.at[slot, k], TM_FN))
        wk = w_ref[:, k:k + 1]
        acc_lo = acc_lo + wk * lo
        acc_hi = acc_hi + wk * hi
    ffn = jnp.concatenate([acc_lo, acc_hi], axis=1)
    o_ref[...] = _layer_norm(alpha * h_ref[...] + ffn, g_ref[...], b_ref[...])


def _final(dest_flat, ys, w_tok, h1, shared, g, b, alpha):
    t = h1.shape[0]
    row = lambda i, d: (i, 0)
    const = lambda i, d: (0, 0)
    return pl.pallas_call(
        functools.partial(_final_kernel, alpha),
        grid_spec=pltpu.PrefetchScalarGridSpec(
            num_scalar_prefetch=1,
            grid=(t // TM_FN,),
            in_specs=[
                pl.BlockSpec(memory_space=pl.ANY),
                pl.BlockSpec((TM_FN, TOP_K), row),
                pl.BlockSpec((TM_FN, D_MODEL), row),
                pl.BlockSpec((TM_FN, D_MODEL), row),
                pl.BlockSpec((1, D_MODEL), const),
                pl.BlockSpec((1, D_MODEL), const),
            ],
            out_specs=pl.BlockSpec((TM_FN, D_MODEL), row),
            scratch_shapes=[pltpu.VMEM((2, TOP_K, TM_FN * ROW_TILE, LANES), U32), pltpu.SemaphoreType.DMA((2,))],
        ),
        out_shape=jax.ShapeDtypeStruct((t, D_MODEL), F32),
        compiler_params=_cparams(("arbitrary",)),
        name="combine_ln2",
    )(dest_flat, ys, w_tok, h1, shared, g, b)


def _dup_heads(w, heads, dim):
    d = w.shape[0]
    w = w.reshape(d, heads, 1, dim)
    return jnp.broadcast_to(w, (d, heads, 2, dim)).reshape(d, heads * 2 * dim)


def _prep_w_in(w_in):
    sizes = (SSD_D_INNER, SSD_D_INNER + BC_DIM, SSD_HEADS, SWA_Q_DIM, SWA_KV_DIM, SWA_KV_DIM, XA_DIM,
             N_BRANCH * D_MODEL)
    pts = []
    acc = 0
    for s in sizes[:-1]:
        acc += s
        pts.append(acc)
    z, xbc, dt, q_s, k_s, v_s, q_x, gates = jnp.split(w_in, pts, axis=1)
    wp = jnp.concatenate(
        [gates, z, xbc, q_s, _dup_heads(k_s, SWA_KV_HEADS, SWA_HEAD_DIM),
         _dup_heads(v_s, SWA_KV_HEADS, SWA_HEAD_DIM), q_x], axis=1).astype(BF16)
    wdt = jnp.pad(dt, ((0, 0), (0, LANES - SSD_HEADS))).astype(BF16)
    return wp, wdt


def _pad_lanes(v):
    return jnp.pad(v, (0, LANES - v.shape[0]))[None, :].astype(F32)


def _layer(h2d, mem2d, batch, seq, w_in, conv_w, conv_b, dt_bias, a_log, d_skip, ssd_norm_g, swa_sinks, rel_bias,
           w_mem_kv, w_ssd_o, w_swa_o, w_xa_o, w_out, ln1_g, ln1_b, router_w, router_bias, w1, w3, w2, ws1, ws3,
           ws2, ln2_g, ln2_b, alpha):
    t = batch * seq
    wp, wdt = _prep_w_in(w_in)
    p, dt_raw = _inproj(h2d, wp, wdt)

    y_ssd = _ssd(
        p, dt_raw,
        conv_w[:, :SSD_D_INNER], conv_b[None, :SSD_D_INNER], conv_w[:, SSD_D_INNER:], conv_b[None, SSD_D_INNER:],
        _pad_lanes(dt_bias), _pad_lanes(a_log),
        jnp.repeat(d_skip, SSD_HEAD_DIM)[None, :], ssd_norm_g[None, :], batch, seq)

    y_swa = _swa(p, swa_sinks.astype(F32), _swa_bias_table(rel_bias), batch, seq)

    memkv = _mm(mem2d, w_mem_kv.astype(BF16), BF16, batch * MEM_LEN, 1024)
    y_xa = _xa(p, memkv, batch, seq)

    merged = _merge(y_ssd, y_swa, y_xa, w_ssd_o.astype(BF16), w_swa_o.astype(BF16), w_xa_o.astype(BF16), p)
    h1, hp = _out_ln(merged, w_out.astype(BF16), h2d, ln1_g[None, :], ln1_b[None, :], alpha)

    rwt = router_w.T
    r_hi = rwt.astype(BF16)
    r_lo = (rwt - r_hi.astype(F32)).astype(BF16)
    ti = jnp.arange(TM_RT)
    su = (ti[:, None] < ti[None, :]).astype(BF16)
    idx_t, w_t, rank_t, cnt = _router(h1, r_hi, r_lo, router_bias[:, None].astype(F32), su)

    counts = cnt[:, 0].astype(I32)
    padded = (counts + MOE_BLK - 1) // MOE_BLK * MOE_BLK
    pend = jnp.cumsum(padded)
    pstart = pend - padded
    n_blocks = t * TOP_K // MOE_BLK + N_EXPERTS
    blk_lo = jnp.arange(n_blocks, dtype=I32) * MOE_BLK
    block_e = jnp.minimum(jnp.sum((pend[None, :] <= blk_lo[:, None]).astype(I32), axis=1), N_EXPERTS - 1)
    of_e = block_e[:, None] == jnp.arange(N_EXPERTS, dtype=I32)[None, :]
    valid_end = jnp.sum(jnp.where(of_e, (pstart + counts)[None, :], 0), axis=1)
    block_nv = jnp.clip(valid_end - blk_lo, 0, MOE_BLK).astype(I32)
    eids = jnp.arange(N_EXPERTS, dtype=I32)
    later_active = (eids[None, :] > eids[:, None]) & (padded > 0)[None, :]
    next_active = jnp.min(jnp.where(later_active, eids[None, :], N_EXPERTS), axis=1)
    next_active = jnp.where(next_active >= N_EXPERTS, -1, next_active)
    seg_start = jnp.sum(jnp.where(of_e, pstart[None, :], 0), axis=1)
    block_first = ((blk_lo == seg_start) & (block_nv > 0)).astype(I32)
    block_next = jnp.sum(jnp.where(of_e, next_active[None, :], 0), axis=1).astype(I32)
    nn_of = next_active[:, None] == eids[None, :]
    next2_active = jnp.where(next_active >= 0, jnp.sum(jnp.where(nn_of, next_active[None, :], 0), axis=1), -1)
    block_next2 = jnp.sum(jnp.where(of_e, next2_active[None, :], 0), axis=1).astype(I32)

    dest_flat = _dest(idx_t, rank_t, pstart.astype(F32)[:, None]).reshape(-1)
    n_slots = n_blocks * MOE_BLK
    x_sorted, shared = _dispatch(dest_flat, hp, h1, ws1.astype(BF16), ws3.astype(BF16), ws2.astype(BF16),
                                 n_slots)
    ys = _experts(block_e, block_nv, block_first, block_next, block_next2,
                  x_sorted.reshape(n_slots * ROW_TILE, LANES), w1, w3, w2)
    return _final(dest_flat, ys.reshape(n_slots, ROW_TILE, LANES), w_t.T, h1, shared, ln2_g[None, :],
                  ln2_b[None, :], alpha)


def kernel(x, mem, w_in, conv_w, conv_b, dt_bias, a_log, d_skip, ssd_norm_g, swa_sinks, rel_bias, w_mem_kv, w_ssd_o, w_swa_o, w_xa_o, w_out, ln1_g, ln1_b, router_w, router_bias, w1, w3, w2, ws1, ws3, ws2, ln2_g, ln2_b):
    batch, seq, d = x.shape
    depth = w_in.shape[0]
    alpha = (2.0 * depth) ** 0.25
    h = x.reshape(batch * seq, d)
    mem2d = mem.reshape(batch * MEM_LEN, d)
    for i in range(depth):
        h = _layer(h, mem2d, batch, seq, w_in[i], conv_w[i], conv_b[i], dt_bias[i], a_log[i], d_skip[i],
                   ssd_norm_g[i], swa_sinks[i], rel_bias, w_mem_kv[i], w_ssd_o[i], w_swa_o[i], w_xa_o[i], w_out[i],
                   ln1_g[i], ln1_b[i], router_w[i], router_bias[i], w1[i], w3[i], w2[i], ws1[i], ws3[i], ws2[i],
                   ln2_g[i], ln2_b[i], alpha)
    return h.reshape(batch, seq, d)
```

```python
import functools
import math

import jax
import jax.numpy as jnp
from jax import lax
from jax.experimental import pallas as pl
from jax.experimental.pallas import tpu as pltpu

F32 = jnp.float32
BF16 = jnp.bfloat16
I32 = jnp.int32
U32 = jnp.uint32

D_MODEL = 2048
MEM_LEN = 256
SSD_D_INNER = D_MODEL
SSD_HEAD_DIM = 64
SSD_HEADS = 32
SSD_GROUPS = 4
SSD_STATE = 128
SSD_CONV = 4
SSD_CHUNK = 128
SWA_HEADS = 16
SWA_KV_HEADS = 4
SWA_HEAD_DIM = 64
SWA_WINDOW = 128
SWA_BLOCK = 128
REL_BUCKETS = 32
REL_MAX_DIST = 128
XA_HEADS = 4
XA_HEAD_DIM = 256
N_BRANCH = 3
N_EXPERTS = 64
TOP_K = 8
N_EXPERT_GROUPS = 8
TOPK_GROUPS = 4
EXPERT_DIM = 512
SHARED_DIM = 512
ROUTED_SCALE = 2.5
LN_EPS = 1e-5
RMS_EPS = 1e-5

SWA_Q_DIM = SWA_HEADS * SWA_HEAD_DIM
SWA_KV_DIM = SWA_KV_HEADS * SWA_HEAD_DIM
XA_DIM = XA_HEADS * XA_HEAD_DIM
BC_DIM = 2 * SSD_GROUPS * SSD_STATE

W_A = SSD_D_INNER + SSD_D_INNER + BC_DIM
OFFA_Z = 0
OFFA_XS = OFFA_Z + SSD_D_INNER
OFFA_BC = OFFA_XS + SSD_D_INNER
W_B = SWA_Q_DIM + 2 * SWA_KV_DIM + XA_DIM + N_BRANCH * D_MODEL
OFFB_Q = 0
OFFB_K = OFFB_Q + SWA_Q_DIM
OFFB_V = OFFB_K + SWA_KV_DIM
OFFB_QX = OFFB_V + SWA_KV_DIM
OFFB_GATES = OFFB_QX + XA_DIM

LANES = 128
MOE_BLK = 256
NEG_INF = float("-inf")


def _cparams(sem, vmem_mb=48):
    return pltpu.CompilerParams(dimension_semantics=sem, vmem_limit_bytes=vmem_mb * 1024 * 1024)


def _dot(a, b):
    return jnp.dot(a, b, preferred_element_type=F32)


def _dot_nt(a, b):
    return lax.dot_general(a, b, (((1,), (1,)), ((), ())), preferred_element_type=F32)


def _sigmoid(x):
    return 1.0 / (1.0 + jnp.exp(-x))


def _silu(x):
    return x * _sigmoid(x)


TM_IN = 1024
TN_IN = 1024


def _inproj_kernel(x_ref, w_ref, wdt_ref, p_ref, dt_ref, xb_ref):
    @pl.when(pl.program_id(1) == 0)
    def _():
        xb = x_ref[...].astype(BF16)
        xb_ref[...] = xb
        dt_ref[...] = _dot(xb, wdt_ref[...])

    p_ref[...] = _dot(xb_ref[...], w_ref[...].astype(BF16)).astype(BF16)


def _inproj(x2d, w_in, wdt):
    t = x2d.shape[0]
    return pl.pallas_call(
        _inproj_kernel,
        grid=(t // TM_IN, W_A // TN_IN),
        in_specs=[
            pl.BlockSpec((TM_IN, D_MODEL), lambda i, j: (i, 0)),
            pl.BlockSpec((D_MODEL, TN_IN), lambda i, j: (0, j)),
            pl.BlockSpec((D_MODEL, LANES), lambda i, j: (0, 0)),
        ],
        out_specs=[
            pl.BlockSpec((TM_IN, TN_IN), lambda i, j: (i, j)),
            pl.BlockSpec((TM_IN, LANES), lambda i, j: (i, 0)),
            pl.BlockSpec((TM_IN, D_MODEL), lambda i, j: (i, 0)),
        ],
        out_shape=[jax.ShapeDtypeStruct((t, W_A), BF16), jax.ShapeDtypeStruct((t, LANES), F32),
                   jax.ShapeDtypeStruct((t, D_MODEL), BF16)],
        compiler_params=_cparams(("parallel", "arbitrary"), vmem_mb=56),
        name="inproj",
    )(x2d, w_in, wdt)


def _mm_kernel(a_ref, b_ref, o_ref):
    o_ref[...] = _dot(a_ref[...].astype(BF16), b_ref[...]).astype(o_ref.dtype)


def _mm(a, b, out_dtype, tm, tn, name, vmem_mb=48):
    m, k = a.shape
    n = b.shape[1]
    return pl.pallas_call(
        _mm_kernel,
        grid=(m // tm, n // tn),
        in_specs=[pl.BlockSpec((tm, k), lambda i, j: (i, 0)), pl.BlockSpec((k, tn), lambda i, j: (0, j))],
        out_specs=pl.BlockSpec((tm, tn), lambda i, j: (i, j)),
        out_shape=jax.ShapeDtypeStruct((m, n), out_dtype),
        compiler_params=_cparams(("parallel", "parallel"), vmem_mb=vmem_mb),
        name=name,
    )(a, b)


Q = SSD_CHUNK


def _conv_shift_matrix():
    t = jnp.arange(Q)[:, None]
    j = jnp.arange(2 * Q)[None, :]
    blocks = []
    for k in range(SSD_CONV - 1):
        src = t - (SSD_CONV - 1 - k)
        blocks.append(j == jnp.where(src >= 0, src, 2 * Q + src))
    return jnp.concatenate(blocks, axis=0).astype(BF16)


def _ssd_kernel(xs_ref, bc_ref, z_ref, dtr_ref, shift_ref, cwx_ref, cbx_ref, cwb_ref, cbb_ref, dtb_ref, alog_ref,
                dsk_ref, ng_ref, o_ref, hx_ref, hb_ref, st_ref, cumt_ref, dtt_ref, wt_ref, y_ref):
    c = pl.program_id(1)

    @pl.when(c == 0)
    def _():
        hx_ref[...] = jnp.zeros_like(hx_ref)
        hb_ref[...] = jnp.zeros_like(hb_ref)
        st_ref[...] = jnp.zeros_like(st_ref)

    def conv_silu(in_ref, prev_ref, w_ref, b_ref):
        cur = in_ref[...]
        ext = jnp.concatenate([cur, prev_ref[...]], axis=0)
        shifted = _dot(shift_ref[...], ext)
        acc = b_ref[...] + w_ref[SSD_CONV - 1:SSD_CONV, :] * cur.astype(F32)
        for k in range(SSD_CONV - 1):
            acc = acc + w_ref[k:k + 1, :] * shifted[k * Q:(k + 1) * Q, :]
        prev_ref[...] = cur
        return _silu(acc)

    xs_c = conv_silu(xs_ref, hx_ref, cwx_ref, cbx_ref)
    bc_c = conv_silu(bc_ref, hb_ref, cwb_ref, cbb_ref)

    lane = lax.broadcasted_iota(I32, (Q, LANES), 1)
    row = lax.broadcasted_iota(I32, (Q, LANES), 0)
    tril = row >= lane
    lo_mask = lane < SSD_HEAD_DIM

    dtv = dtr_ref[...] + dtb_ref[...]
    dt = jnp.maximum(dtv, 0.0) + jnp.log1p(jnp.exp(-jnp.abs(dtv)))
    a = jnp.where(lane[0:1, :] < SSD_HEADS, -jnp.exp(alog_ref[...]), 0.0)
    la = dt * a
    tri = jnp.where(tril, 1.0, 0.0).astype(BF16)
    p1 = la.astype(BF16)
    r1 = la - p1.astype(F32)
    p2 = r1.astype(BF16)
    p3 = (r1 - p2.astype(F32)).astype(BF16)
    cum = _dot(tri, p1) + _dot(tri, p2) + _dot(tri, p3)
    ecum = jnp.exp(cum)
    cum_t = cum.T
    dt_t = dt.T
    last = cum_t[:, Q - 1:Q]
    cumt_ref[...] = cum_t
    dtt_ref[...] = dt_t
    wt_ref[...] = jnp.exp(last - cum_t) * dt_t
    chunk_decay = jnp.exp(last)

    def head_parts(h, cb, cm, bm_t):
        col = jnp.broadcast_to(cum[:, h:h + 1], (Q, Q))
        rw = cumt_ref[h:h + 1, :]
        dec = jnp.exp(jnp.where(tril, col - rw, NEG_INF))
        m = cb * (dec * dtt_ref[h:h + 1, :])
        ce = cm * jnp.broadcast_to(ecum[:, h:h + 1], (Q, Q))
        sl = bm_t * wt_ref[h:h + 1, :]
        return m.astype(BF16), ce.astype(BF16), sl.astype(BF16)

    for g in range(SSD_GROUPS):
        bm = bc_c[:, g * SSD_STATE:(g + 1) * SSD_STATE]
        cm = bc_c[:, BC_DIM // 2 + g * SSD_STATE:BC_DIM // 2 + (g + 1) * SSD_STATE]
        cb = _dot_nt(cm.astype(BF16), bm.astype(BF16))
        bm_t = bm.T
        for qd in range(SSD_HEADS // SSD_GROUPS // 2):
            hp = g * (SSD_HEADS // SSD_GROUPS // 2) + qd
            ha, hb = 2 * hp, 2 * hp + 1
            sl_ = slice(hp * LANES, (hp + 1) * LANES)
            xs_pair = xs_c[:, sl_]
            x_lo = jnp.where(lo_mask, xs_pair, 0.0).astype(BF16)
            x_hi = jnp.where(lo_mask, 0.0, xs_pair).astype(BF16)
            h_prev = st_ref[:, sl_]
            h_lo = jnp.where(lo_mask, h_prev, 0.0).astype(BF16)
            h_hi = jnp.where(lo_mask, 0.0, h_prev).astype(BF16)
            m_a, ce_a, sl_a = head_parts(ha, cb, cm, bm_t)
            m_b, ce_b, sl_b = head_parts(hb, cb, cm, bm_t)
            lhs = jnp.concatenate([m_a, ce_a, m_b, ce_b], axis=1)
            rhs = jnp.concatenate([x_lo, h_lo, x_hi, h_hi], axis=0)
            y_ref[:, sl_] = _dot(lhs, rhs)
            new = _dot(jnp.concatenate([sl_a, sl_b], axis=1), jnp.concatenate([x_lo, x_hi], axis=0))
            cd = jnp.where(lo_mask[0:1, :], chunk_decay[ha:ha + 1, :], chunk_decay[hb:hb + 1, :])
            st_ref[:, sl_] = h_prev * cd + new

    y = y_ref[...] + xs_c * dsk_ref[...]
    y = y * _silu(z_ref[...].astype(F32))
    gw = SSD_D_INNER // SSD_GROUPS
    for g in range(SSD_GROUPS):
        yg = y[:, g * gw:(g + 1) * gw]
        ms = jnp.mean(yg * yg, axis=-1, keepdims=True)
        o_ref[:, g * gw:(g + 1) * gw] = ((yg * lax.rsqrt(ms + RMS_EPS)) * ng_ref[:, g * gw:(g + 1) * gw]).astype(BF16)


def _ssd(p, dt_raw, cwx, cbx, cwb, cbb, dtb, alog, dsk, ng, batch, seq):
    nc = seq // Q
    t = batch * seq
    rowblk = lambda b, c: b * nc + c
    const = lambda b, c: (0, 0)
    return pl.pallas_call(
        _ssd_kernel,
        grid=(batch, nc),
        in_specs=[
            pl.BlockSpec((Q, SSD_D_INNER), lambda b, c: (rowblk(b, c), OFFA_XS // SSD_D_INNER)),
            pl.BlockSpec((Q, BC_DIM), lambda b, c: (rowblk(b, c), OFFA_BC // BC_DIM)),
            pl.BlockSpec((Q, SSD_D_INNER), lambda b, c: (rowblk(b, c), OFFA_Z // SSD_D_INNER)),
            pl.BlockSpec((Q, LANES), lambda b, c: (rowblk(b, c), 0)),
            pl.BlockSpec(((SSD_CONV - 1) * Q, 2 * Q), const),
            pl.BlockSpec((SSD_CONV, SSD_D_INNER), const),
            pl.BlockSpec((1, SSD_D_INNER), const),
            pl.BlockSpec((SSD_CONV, BC_DIM), const),
            pl.BlockSpec((1, BC_DIM), const),
            pl.BlockSpec((1, LANES), const),
            pl.BlockSpec((1, LANES), const),
            pl.BlockSpec((1, SSD_D_INNER), const),
            pl.BlockSpec((1, SSD_D_INNER), const),
        ],
        out_specs=pl.BlockSpec((Q, SSD_D_INNER), lambda b, c: (rowblk(b, c), 0)),
        out_shape=jax.ShapeDtypeStruct((t, SSD_D_INNER), BF16),
        scratch_shapes=[
            pltpu.VMEM((Q, SSD_D_INNER), BF16),
            pltpu.VMEM((Q, BC_DIM), BF16),
            pltpu.VMEM((SSD_STATE, SSD_D_INNER), F32),
            pltpu.VMEM((LANES, Q), F32),
            pltpu.VMEM((LANES, Q), F32),
            pltpu.VMEM((LANES, Q), F32),
            pltpu.VMEM((Q, SSD_D_INNER), F32),
        ],
        compiler_params=_cparams(("parallel", "arbitrary")),
        name="ssd",
    )(p, p, p, dt_raw, _conv_shift_matrix(), cwx, cbx, cwb, cbb, dtb, alog, dsk, ng)


N_PAIRS = SWA_HEADS // 2


def _swa_kernel(sink_ref, q_ref, kp_ref, kc_ref, vp_ref, vc_ref, bias_ref, o_ref):
    n = pl.program_id(1)
    blk = SWA_BLOCK
    kband = jnp.concatenate([kp_ref[...], kc_ref[...]], axis=0)
    vband = jnp.concatenate([vp_ref[...], vc_ref[...]], axis=0)

    lane = lax.broadcasted_iota(I32, (1, LANES), 1)
    lo_f = jnp.where(lane < SWA_HEAD_DIM, 1.0, 0.0)
    lo_m = lo_f.astype(BF16)
    hi_m = (1.0 - lo_f).astype(BF16)

    def head_twice(band, kv):
        tile = band[:, (kv // 2) * LANES:(kv // 2 + 1) * LANES]
        swapped = jnp.concatenate([tile[:, SWA_HEAD_DIM:], tile[:, :SWA_HEAD_DIM]], axis=1)
        return tile * lo_m + swapped * hi_m if kv % 2 == 0 else swapped * lo_m + tile * hi_m

    k_twice = [head_twice(kband, kv) for kv in range(SWA_KV_HEADS)]
    v_twice = [head_twice(vband, kv) for kv in range(SWA_KV_HEADS)]
    colj = lax.broadcasted_iota(I32, (2 * blk, 2 * blk), 1)
    rowi = lax.broadcasted_iota(I32, (2 * blk, 1), 0)
    valid = (colj >= blk) | (n > 0)
    scale = SWA_HEAD_DIM ** -0.5
    for p in range(N_PAIRS):
        kv = p // (SWA_HEADS // SWA_KV_HEADS // 2)
        qp = q_ref[:, p * LANES:(p + 1) * LANES]
        qs = jnp.concatenate([qp * lo_m, qp * hi_m], axis=0)
        s = _dot_nt(qs, k_twice[kv]) * scale + bias_ref[p]
        s = jnp.where(valid, s, NEG_INF)
        sink = jnp.where(rowi < blk, sink_ref[2 * p], sink_ref[2 * p + 1])
        m = jnp.maximum(jnp.max(s, axis=-1, keepdims=True), sink)
        e = jnp.exp(s - m)
        den = jnp.sum(e, axis=-1, keepdims=True) + jnp.exp(sink - m)
        pr = (e * (1.0 / den)).astype(BF16)
        vd = v_twice[kv]
        lhs = jnp.concatenate([pr[:blk], pr[blk:]], axis=1)
        rhs = jnp.concatenate([vd * lo_m, vd * hi_m], axis=0)
        o_ref[:, p * LANES:(p + 1) * LANES] = _dot(lhs, rhs).astype(BF16)


def _swa(p, sinks, bias, batch, seq):
    nb = seq // SWA_BLOCK
    t = batch * seq
    kw = SWA_KV_DIM
    cur = lambda b, n, s: b * nb + n
    prev = lambda b, n, s: b * nb + jnp.maximum(n - 1, 0)
    return pl.pallas_call(
        _swa_kernel,
        grid_spec=pltpu.PrefetchScalarGridSpec(
            num_scalar_prefetch=1,
            grid=(batch, nb),
            in_specs=[
                pl.BlockSpec((SWA_BLOCK, SWA_Q_DIM), lambda b, n, s: (cur(b, n, s), OFFB_Q // SWA_Q_DIM)),
                pl.BlockSpec((SWA_BLOCK, kw), lambda b, n, s: (prev(b, n, s), OFFB_K // kw)),
                pl.BlockSpec((SWA_BLOCK, kw), lambda b, n, s: (cur(b, n, s), OFFB_K // kw)),
                pl.BlockSpec((SWA_BLOCK, kw), lambda b, n, s: (prev(b, n, s), OFFB_V // kw)),
                pl.BlockSpec((SWA_BLOCK, kw), lambda b, n, s: (cur(b, n, s), OFFB_V // kw)),
                pl.BlockSpec((N_PAIRS, 2 * SWA_BLOCK, 2 * SWA_BLOCK), lambda b, n, s: (0, 0, 0)),
            ],
            out_specs=pl.BlockSpec((SWA_BLOCK, SWA_Q_DIM), lambda b, n, s: (cur(b, n, s), 0)),
        ),
        out_shape=jax.ShapeDtypeStruct((t, SWA_Q_DIM), BF16),
        compiler_params=_cparams(("parallel", "arbitrary")),
        name="swa",
    )(sinks, p, p, p, p, p, bias)


def _t5_causal_bucket(dist):
    max_exact = REL_BUCKETS // 2
    large = max_exact + (jnp.log(jnp.maximum(dist, 1).astype(F32) / max_exact)
                         / math.log(REL_MAX_DIST / max_exact) * (REL_BUCKETS - max_exact)).astype(I32)
    large = jnp.minimum(large, REL_BUCKETS - 1)
    return jnp.where(dist < max_exact, dist, large)


def _swa_bias_table(rel_bias):
    qi = jnp.arange(SWA_BLOCK)[:, None]
    kj = jnp.arange(2 * SWA_BLOCK)[None, :]
    dist = qi + SWA_BLOCK - kj
    in_window = (dist >= 0) & (dist < SWA_WINDOW)
    bucket = _t5_causal_bucket(jnp.maximum(dist, 0))
    onehot = (bucket[:, :, None] == jnp.arange(REL_BUCKETS)[None, None, :]).astype(F32)
    bias = jnp.einsum('ijb,bh->hij', onehot, rel_bias.astype(F32), precision=lax.Precision.HIGHEST)
    bias = jnp.where(in_window[None, :, :], bias, NEG_INF)
    return bias.reshape(N_PAIRS, 2 * SWA_BLOCK, 2 * SWA_BLOCK)


TL_XA = 512


def _xa_kernel(q_ref, mk_ref, mv_ref, o_ref):
    s = _dot_nt(q_ref[...], mk_ref[...]) * (XA_HEAD_DIM ** -0.5)
    m = jnp.max(s, axis=-1, keepdims=True)
    e = jnp.exp(s - m)
    pr = e * (1.0 / jnp.sum(e, axis=-1, keepdims=True))
    o_ref[...] = _dot(pr.astype(BF16), mv_ref[...]).astype(BF16)


def _xa(p, memkv, batch, seq):
    nl = seq // TL_XA
    t = batch * seq
    return pl.pallas_call(
        _xa_kernel,
        grid=(batch, nl, XA_HEADS),
        in_specs=[
            pl.BlockSpec((TL_XA, XA_HEAD_DIM), lambda b, l, h: (b * nl + l, OFFB_QX // XA_HEAD_DIM + h)),
            pl.BlockSpec((MEM_LEN, XA_HEAD_DIM), lambda b, l, h: (b, h)),
            pl.BlockSpec((MEM_LEN, XA_HEAD_DIM), lambda b, l, h: (b, XA_HEADS + h)),
        ],
        out_specs=pl.BlockSpec((TL_XA, XA_HEAD_DIM), lambda b, l, h: (b * nl + l, h)),
        out_shape=jax.ShapeDtypeStruct((t, XA_DIM), BF16),
        compiler_params=_cparams(("parallel", "parallel", "parallel")),
        name="xattn",
    )(p, memkv, memkv)


TM_MG = 1024
TN_MG = 512


def _merge_kernel(ys_ref, yw_ref, yx_ref, ws_ref, ww_ref, wx_ref, g0_ref, g1_ref, g2_ref, o_ref):
    a = _dot(ys_ref[...], ws_ref[...])
    b = _dot(yw_ref[...], ww_ref[...])
    c = _dot(yx_ref[...], wx_ref[...])
    g0 = _sigmoid(g0_ref[...].astype(F32))
    g1 = _sigmoid(g1_ref[...].astype(F32))
    g2 = _sigmoid(g2_ref[...].astype(F32))
    o_ref[...] = (g0 * a + g1 * b + g2 * c).astype(BF16)


def _merge(y_ssd, y_swa, y_xa, w_ssd_o, w_swa_o, w_xa_o, p):
    t = y_ssd.shape[0]
    gpb = D_MODEL // TN_MG
    gate = lambda k: pl.BlockSpec((TM_MG, TN_MG), lambda i, j: (i, OFFB_GATES // TN_MG + k * gpb + j))
    return pl.pallas_call(
        _merge_kernel,
        grid=(t // TM_MG, D_MODEL // TN_MG),
        in_specs=[
            pl.BlockSpec((TM_MG, SSD_D_INNER), lambda i, j: (i, 0)),
            pl.BlockSpec((TM_MG, SWA_Q_DIM), lambda i, j: (i, 0)),
            pl.BlockSpec((TM_MG, XA_DIM), lambda i, j: (i, 0)),
            pl.BlockSpec((SSD_D_INNER, TN_MG), lambda i, j: (0, j)),
            pl.BlockSpec((SWA_Q_DIM, TN_MG), lambda i, j: (0, j)),
            pl.BlockSpec((XA_DIM, TN_MG), lambda i, j: (0, j)),
            gate(0), gate(1), gate(2),
        ],
        out_specs=pl.BlockSpec((TM_MG, TN_MG), lambda i, j: (i, j)),
        out_shape=jax.ShapeDtypeStruct((t, D_MODEL), BF16),
        compiler_params=_cparams(("parallel", "parallel")),
        name="merge",
    )(y_ssd, y_swa, y_xa, w_ssd_o, w_swa_o, w_xa_o, p, p, p)


TM_LN = 512


def _layer_norm(v, g, b):
    mu = jnp.mean(v, axis=-1, keepdims=True)
    d = v - mu
    var = jnp.mean(d * d, axis=-1, keepdims=True)
    return (d * lax.rsqrt(var + LN_EPS)) * g + b


def _pack_halves(v):
    n = v.shape[1] // 2
    lo = lax.bitcast_convert_type(v[:, :n].astype(BF16).astype(F32), U32) >> 16
    hi = lax.bitcast_convert_type(v[:, n:].astype(BF16).astype(F32), U32) & jnp.uint32(0xFFFF0000)
    return hi | lo


def _unpack_halves(w):
    lo = lax.bitcast_convert_type(w << 16, F32)
    hi = lax.bitcast_convert_type(w & jnp.uint32(0xFFFF0000), F32)
    return lo, hi


ROW_WORDS = D_MODEL // 2
ROW_TILE = ROW_WORDS // LANES


def _store_row_tiled(ref, words):
    m = words.shape[0]
    for c in range(ROW_TILE):
        ref[pl.ds(c, m, stride=ROW_TILE), :] = words[:, c * LANES:(c + 1) * LANES]


def _load_row_tiled(ref, m):
    return jnp.concatenate([ref[pl.ds(c, m, stride=ROW_TILE), :] for c in range(ROW_TILE)], axis=1)


def _out_ln_kernel(alpha, m_ref, w_ref, x_ref, g_ref, b_ref, h_ref, hp_ref):
    mix = _dot(m_ref[...], w_ref[...])
    hn = _layer_norm(alpha * x_ref[...] + mix, g_ref[...], b_ref[...])
    h_ref[...] = hn
    _store_row_tiled(hp_ref, _pack_halves(hn))


def _out_ln(merged, w_out, x2d, g, b, alpha):
    t = x2d.shape[0]
    row = lambda i: (i, 0)
    const = lambda i: (0, 0)
    return pl.pallas_call(
        functools.partial(_out_ln_kernel, alpha),
        grid=(t // TM_LN,),
        in_specs=[
            pl.BlockSpec((TM_LN, D_MODEL), row),
            pl.BlockSpec((D_MODEL, D_MODEL), const),
            pl.BlockSpec((TM_LN, D_MODEL), row),
            pl.BlockSpec((1, D_MODEL), const),
            pl.BlockSpec((1, D_MODEL), const),
        ],
        out_specs=[pl.BlockSpec((TM_LN, D_MODEL), row), pl.BlockSpec((TM_LN * ROW_TILE, LANES), row)],
        out_shape=[jax.ShapeDtypeStruct((t, D_MODEL), F32), jax.ShapeDtypeStruct((t * ROW_TILE, LANES), U32)],
        compiler_params=_cparams(("parallel",)),
        name="out_ln1",
    )(merged, w_out, x2d, g, b)


TM_RT = 512
EPG = N_EXPERTS // N_EXPERT_GROUPS


def _router_kernel(h_ref, whi_ref, wlo_ref, bias_ref, su_ref, idx_ref, w_ref, rank_ref, cnt_ref, run_ref):
    @pl.when(pl.program_id(0) == 0)
    def _():
        run_ref[...] = jnp.zeros_like(run_ref)

    h = h_ref[...]
    h_hi = h.astype(BF16)
    h_lo = (h - h_hi.astype(F32)).astype(BF16)
    whi = whi_ref[...]
    logits = _dot_nt(whi, h_hi) + _dot_nt(whi, h_lo) + _dot_nt(wlo_ref[...], h_hi)
    scores = _sigmoid(logits)
    sel = scores + bias_ref[...]

    iota_m = lax.broadcasted_iota(I32, (EPG, TM_RT), 0).astype(F32)
    iota_g = lax.broadcasted_iota(I32, (N_EXPERT_GROUPS, TM_RT), 0).astype(F32)
    gs = []
    for g in range(N_EXPERT_GROUPS):
        v = sel[g * EPG:(g + 1) * EPG, :]
        m1 = jnp.max(v, axis=0, keepdims=True)
        i1 = jnp.min(jnp.where(v == m1, iota_m, float(EPG)), axis=0, keepdims=True)
        m2 = jnp.max(jnp.where(iota_m == i1, NEG_INF, v), axis=0, keepdims=True)
        gs.append(m1 + m2)
    gsc = jnp.concatenate(gs, axis=0)
    chosen = jnp.zeros((N_EXPERT_GROUPS, TM_RT), F32)
    for _ in range(TOPK_GROUPS):
        m = jnp.max(gsc, axis=0, keepdims=True)
        ig = jnp.min(jnp.where(gsc == m, iota_g, float(N_EXPERT_GROUPS)), axis=0, keepdims=True)
        hit = iota_g == ig
        chosen = jnp.where(hit, 1.0, chosen)
        gsc = jnp.where(hit, NEG_INF, gsc)
    gmask = jnp.concatenate(
        [jnp.broadcast_to(chosen[g:g + 1, :], (EPG, TM_RT)) for g in range(N_EXPERT_GROUPS)], axis=0)
    masked = jnp.where(gmask > 0.0, sel, NEG_INF)

    iota_e = lax.broadcasted_iota(I32, (N_EXPERTS, TM_RT), 0).astype(F32)
    hits, idxs, ws = [], [], []
    for _ in range(TOP_K):
        m = jnp.max(masked, axis=0, keepdims=True)
        ie = jnp.min(jnp.where(masked == m, iota_e, float(N_EXPERTS)), axis=0, keepdims=True)
        hit = iota_e == ie
        hits.append(hit)
        idxs.append(ie)
        ws.append(jnp.sum(jnp.where(hit, scores, 0.0), axis=0, keepdims=True))
        masked = jnp.where(hit, NEG_INF, masked)
    wsum = ws[0]
    for k in range(1, TOP_K):
        wsum = wsum + ws[k]
    inv = 1.0 / wsum
    idx_ref[...] = jnp.concatenate(idxs, axis=0).astype(I32)
    w_ref[...] = jnp.concatenate([w * inv * ROUTED_SCALE for w in ws], axis=0)

    assign = jnp.zeros((N_EXPERTS, TM_RT), F32)
    for hit in hits:
        assign = jnp.where(hit, 1.0, assign)
    pos = _dot(assign.astype(BF16), su_ref[...]) + run_ref[...]
    rank_ref[...] = jnp.concatenate(
        [jnp.sum(jnp.where(hit, pos, 0.0), axis=0, keepdims=True) for hit in hits], axis=0).astype(I32)
    run = run_ref[...] + jnp.sum(assign, axis=1, keepdims=True)
    run_ref[...] = run
    cnt_ref[...] = run


def _router(h1, r_hi, r_lo, rbias, su):
    t = h1.shape[0]
    tile = lambda i: (0, i)
    const = lambda i: (0, 0)
    return pl.pallas_call(
        _router_kernel,
        grid=(t // TM_RT,),
        in_specs=[
            pl.BlockSpec((TM_RT, D_MODEL), lambda i: (i, 0)),
            pl.BlockSpec((N_EXPERTS, D_MODEL), const),
            pl.BlockSpec((N_EXPERTS, D_MODEL), const),
            pl.BlockSpec((N_EXPERTS, 1), const),
            pl.BlockSpec((TM_RT, TM_RT), const),
        ],
        out_specs=[
            pl.BlockSpec((TOP_K, TM_RT), tile),
            pl.BlockSpec((TOP_K, TM_RT), tile),
            pl.BlockSpec((TOP_K, TM_RT), tile),
            pl.BlockSpec((N_EXPERTS, TM_RT), const),
        ],
        out_shape=[
            jax.ShapeDtypeStruct((TOP_K, t), I32),
            jax.ShapeDtypeStruct((TOP_K, t), F32),
            jax.ShapeDtypeStruct((TOP_K, t), I32),
            jax.ShapeDtypeStruct((N_EXPERTS, TM_RT), F32),
        ],
        scratch_shapes=[pltpu.VMEM((N_EXPERTS, TM_RT), F32)],
        compiler_params=_cparams(("arbitrary",)),
        name="router",
    )(h1, r_hi, r_lo, rbias, su)


TM_DS = 2048


def _dest_kernel(idx_ref, rank_ref, ps_ref, d_ref):
    iota_e = lax.broadcasted_iota(I32, (N_EXPERTS, TM_DS), 0)
    ps = ps_ref[...]
    rows = []
    for k in range(TOP_K):
        hit = iota_e == idx_ref[k:k + 1, :]
        rows.append(jnp.sum(jnp.where(hit, ps, 0.0), axis=0, keepdims=True))
    d_ref[...] = jnp.concatenate(rows, axis=0).astype(I32) + rank_ref[...]


def _dest(idx_t, rank_t, pstart):
    t = idx_t.shape[1]
    tile = lambda i: (0, i)
    return pl.pallas_call(
        _dest_kernel,
        grid=(t // TM_DS,),
        in_specs=[pl.BlockSpec((TOP_K, TM_DS), tile), pl.BlockSpec((TOP_K, TM_DS), tile),
                  pl.BlockSpec((N_EXPERTS, 1), lambda i: (0, 0))],
        out_specs=pl.BlockSpec((TOP_K, TM_DS), tile),
        out_shape=jax.ShapeDtypeStruct((TOP_K, t), I32),
        compiler_params=_cparams(("parallel",)),
        name="dest",
    )(idx_t, rank_t, pstart)


TM_DP = 512


def _dispatch_kernel(dest_ref, hp_ref, h_ref, ws1_ref, ws3_ref, ws2_ref, xs_ref, sh_ref, sem):
    t_total = pl.num_programs(0) * TM_DP
    base = pl.program_id(0) * TM_DP

    def issue(t, carry):
        src = hp_ref.at[pl.ds(pl.multiple_of(t * ROW_TILE, ROW_TILE), ROW_TILE), :]
        for k in range(TOP_K):
            slot = dest_ref[k * t_total + base + t]
            pltpu.make_async_copy(src, xs_ref.at[slot], sem).start(priority=k % 2)
        return carry

    lax.fori_loop(0, TM_DP, issue, 0)

    xb = h_ref[...].astype(BF16)
    hdn = _silu(_dot(xb, ws1_ref[...])) * _dot(xb, ws3_ref[...])
    sh_ref[...] = _dot(hdn.astype(BF16), ws2_ref[...])

    for k in range(TOP_K):
        pltpu.make_async_copy(hp_ref.reshape(TM_DP, ROW_TILE, LANES), xs_ref.at[pl.ds(0, TM_DP)], sem).wait()


def _dispatch(dest_flat, hp, h1, ws1, ws3, ws2, n_slots):
    t = h1.shape[0]
    row = lambda i, d: (i, 0)
    const = lambda i, d: (0, 0)
    return pl.pallas_call(
        _dispatch_kernel,
        grid_spec=pltpu.PrefetchScalarGridSpec(
            num_scalar_prefetch=1,
            grid=(t // TM_DP,),
            in_specs=[
                pl.BlockSpec((TM_DP * ROW_TILE, LANES), row),
                pl.BlockSpec((TM_DP, D_MODEL), row),
                pl.BlockSpec((D_MODEL, SHARED_DIM), const),
                pl.BlockSpec((D_MODEL, SHARED_DIM), const),
                pl.BlockSpec((SHARED_DIM, D_MODEL), const),
            ],
            out_specs=[pl.BlockSpec(memory_space=pl.ANY), pl.BlockSpec((TM_DP, D_MODEL), row)],
            scratch_shapes=[pltpu.SemaphoreType.DMA],
        ),
        out_shape=[jax.ShapeDtypeStruct((n_slots, ROW_TILE, LANES), U32), jax.ShapeDtypeStruct((t, D_MODEL), F32)],
        compiler_params=_cparams(("arbitrary",)),
        name="dispatch",
    )(dest_flat, hp, h1, ws1, ws3, ws2)


N_WSLOTS = 3


def _expert_kernel(be_ref, nv_ref, first_ref, nxt_ref, nxt2_ref, x_ref, w1_ref, w3_ref, w2_ref, o_ref,
                   wf1, wf3, wf2, w1b, w3b, w2b, slot_ref, sems):
    b = pl.program_id(0)
    nv = nv_ref[b]

    def weight_copies(e, s):
        return (pltpu.make_async_copy(w1_ref.at[e], wf1.at[s], sems.at[s, 0]),
                pltpu.make_async_copy(w3_ref.at[e], wf3.at[s], sems.at[s, 1]),
                pltpu.make_async_copy(w2_ref.at[e], wf2.at[s], sems.at[s, 2]))

    @pl.when(b == 0)
    def _():
        slot_ref[0] = 0
        for cp in weight_copies(be_ref[0], 0):
            cp.start()

        @pl.when(nxt_ref[0] >= 0)
        def _():
            for cp in weight_copies(nxt_ref[0], 1):
                cp.start(priority=1)

    @pl.when(first_ref[b] == 1)
    def _():
        s = slot_ref[0]
        for cp in weight_copies(be_ref[b], s):
            cp.wait()

        @pl.when(nxt2_ref[b] >= 0)
        def _():
            s2 = jnp.where(s == 0, N_WSLOTS - 1, s - 1)
            for cp in weight_copies(nxt2_ref[b], s2):
                cp.start(priority=1)

        w1b[...] = wf1[s].astype(BF16)
        w3b[...] = wf3[s].astype(BF16)
        w2b[...] = wf2[s].astype(BF16)
        slot_ref[0] = jnp.where(s == N_WSLOTS - 1, 0, s + 1)

    @pl.when(nv > 0)
    def _():
        live = lax.broadcasted_iota(I32, (MOE_BLK, 1), 0) < nv
        lo, hi = _unpack_halves(jnp.where(live, _load_row_tiled(x_ref, MOE_BLK), jnp.uint32(0)))
        x = jnp.concatenate([lo.astype(BF16), hi.astype(BF16)], axis=1)
        hdn = _silu(_dot(x, w1b[...])) * _dot(x, w3b[...])
        _store_row_tiled(o_ref, _pack_halves(_dot(hdn.astype(BF16), w2b[...])))

    @pl.when(nv <= 0)
    def _():
        o_ref[...] = jnp.zeros_like(o_ref)


def _experts(block_e, block_nv, block_first, block_next, block_next2, x_sorted, w1, w3, w2):
    n_rows = x_sorted.shape[0]
    n_blocks = n_rows // (MOE_BLK * ROW_TILE)
    row = lambda b, *_: (b, 0)
    return pl.pallas_call(
        _expert_kernel,
        grid_spec=pltpu.PrefetchScalarGridSpec(
            num_scalar_prefetch=5,
            grid=(n_blocks,),
            in_specs=[
                pl.BlockSpec((MOE_BLK * ROW_TILE, LANES), row),
                pl.BlockSpec(memory_space=pl.ANY),
                pl.BlockSpec(memory_space=pl.ANY),
                pl.BlockSpec(memory_space=pl.ANY),
            ],
            out_specs=pl.BlockSpec((MOE_BLK * ROW_TILE, LANES), row),
            scratch_shapes=[
                pltpu.VMEM((N_WSLOTS, D_MODEL, EXPERT_DIM), F32),
                pltpu.VMEM((N_WSLOTS, D_MODEL, EXPERT_DIM), F32),
                pltpu.VMEM((N_WSLOTS, EXPERT_DIM, D_MODEL), F32),
                pltpu.VMEM((D_MODEL, EXPERT_DIM), BF16),
                pltpu.VMEM((D_MODEL, EXPERT_DIM), BF16),
                pltpu.VMEM((EXPERT_DIM, D_MODEL), BF16),
                pltpu.SMEM((1,), I32),
                pltpu.SemaphoreType.DMA((N_WSLOTS, 3)),
            ],
        ),
        out_shape=jax.ShapeDtypeStruct((n_rows, LANES), U32),
        compiler_params=_cparams(("arbitrary",), vmem_mb=58),
        name="experts",
    )(block_e, block_nv, block_first, block_next, block_next2, x_sorted, w1, w3, w2)


TM_FN = 256


CH_FN = 8


def _final_kernel(alpha, dest_ref, ys_ref, w_ref, h_ref, sh_ref, g_ref, b_ref, o_ref, buf, sems):
    i = pl.program_id(0)
    n_tiles = pl.num_programs(0)
    t_total = n_tiles * TM_FN
    slot = i % 2
    nslot = 1 - slot
    half = D_MODEL // 2

    def issue_token(tile, sl, c, tt):
        tok = c * CH_FN + tt
        row0 = pl.multiple_of(tok * ROW_TILE, ROW_TILE)
        for k in range(TOP_K):
            src = ys_ref.at[dest_ref[k * t_total + tile * TM_FN + tok]]
            pltpu.make_async_copy(src, buf.at[sl, k, pl.ds(row0, ROW_TILE), :], sems.at[sl]).start(priority=k % 2)

    def wait_tile(sl):
        for k in range(TOP_K):
            pltpu.make_async_copy(ys_ref.at[pl.ds(0, TM_FN)], buf.at[sl, k].reshape(TM_FN, ROW_TILE, LANES),
                                  sems.at[sl]).wait()

    @pl.when(i == 0)
    def _():
        def first(c, carry):
            for tt in range(CH_FN):
                issue_token(0, 0, c, tt)
            return carry
        lax.fori_loop(0, TM_FN // CH_FN, first, 0)

    wait_tile(slot)

    nxt_tile = jnp.minimum(i + 1, n_tiles - 1)

    def body(c, carry):
        tok = pl.ds(pl.multiple_of(c * CH_FN, CH_FN), CH_FN)
        acc_lo = sh_ref[tok, :half]
        acc_hi = sh_ref[tok, half:]
        pre = alpha * h_ref[tok, :]
        wrow = w_ref[tok, :]
        for k in range(TOP_K):
            words = jnp.concatenate(
                [buf[slot, k, pl.ds(c * (CH_FN * ROW_TILE) + r, CH_FN, stride=ROW_TILE), :] for r in range(ROW_TILE)],
                axis=1)
            issue_token(nxt_tile, nslot, c, k)
            lo, hi = _unpack_halves(words)
            wk = wrow[:, k:k + 1]
            acc_lo = acc_lo + wk * lo
            acc_hi = acc_hi + wk * hi
        o_ref[tok, :] = pre + jnp.concatenate([acc_lo, acc_hi], axis=1)
        return carry

    lax.fori_loop(0, TM_FN // CH_FN, body, 0)
    o_ref[...] = _layer_norm(o_ref[...], g_ref[...], b_ref[...])

    @pl.when(i == n_tiles - 1)
    def _():
        wait_tile(nslot)


def _final(dest_flat, ys, w_tok, h1, shared, g, b, alpha):
    t = h1.shape[0]
    row = lambda i, d: (i, 0)
    const = lambda i, d: (0, 0)
    return pl.pallas_call(
        functools.partial(_final_kernel, alpha),
        grid_spec=pltpu.PrefetchScalarGridSpec(
            num_scalar_prefetch=1,
            grid=(t // TM_FN,),
            in_specs=[
                pl.BlockSpec(memory_space=pl.ANY),
                pl.BlockSpec((TM_FN, TOP_K), row),
                pl.BlockSpec((TM_FN, D_MODEL), row),
                pl.BlockSpec((TM_FN, D_MODEL), row),
                pl.BlockSpec((1, D_MODEL), const),
                pl.BlockSpec((1, D_MODEL), const),
            ],
            out_specs=pl.BlockSpec((TM_FN, D_MODEL), row),
            scratch_shapes=[pltpu.VMEM((2, TOP_K, TM_FN * ROW_TILE, LANES), U32), pltpu.SemaphoreType.DMA((2,))],
        ),
        out_shape=jax.ShapeDtypeStruct((t, D_MODEL), F32),
        compiler_params=_cparams(("arbitrary",)),
        name="combine_ln2",
    )(dest_flat, ys, w_tok, h1, shared, g, b)


TM_INB = 512
TN_INB = W_B // 2


def _pad_lanes(v):
    return jnp.pad(v, (0, LANES - v.shape[0]))[None, :].astype(F32)


def _layer(h2d, mem2d, batch, seq, w_in, conv_w, conv_b, dt_bias, a_log, d_skip, ssd_norm_g, swa_sinks, rel_bias,
           w_mem_kv, w_ssd_o, w_swa_o, w_xa_o, w_out, ln1_g, ln1_b, router_w, router_bias, w1, w3, w2, ws1, ws3,
           ws2, ln2_g, ln2_b, alpha):
    t = batch * seq
    wdt = jnp.pad(w_in[:, W_A:W_A + SSD_HEADS], ((0, 0), (0, LANES - SSD_HEADS))).astype(BF16)
    pa, dt_raw, xb = _inproj(h2d, w_in, wdt)
    pb = _mm(xb, w_in[:, W_A + SSD_HEADS:].astype(BF16), BF16, TM_INB, TN_INB, "inproj_b", vmem_mb=56)

    y_ssd = _ssd(
        pa, dt_raw,
        conv_w[:, :SSD_D_INNER], conv_b[None, :SSD_D_INNER], conv_w[:, SSD_D_INNER:], conv_b[None, SSD_D_INNER:],
        _pad_lanes(dt_bias), _pad_lanes(a_log),
        jnp.repeat(d_skip, SSD_HEAD_DIM)[None, :], ssd_norm_g[None, :], batch, seq)

    y_swa = _swa(pb, swa_sinks.astype(F32), _swa_bias_table(rel_bias), batch, seq)

    memkv = _mm(mem2d, w_mem_kv.astype(BF16), BF16, batch * MEM_LEN, 1024, "mem_kv")
    y_xa = _xa(pb, memkv, batch, seq)

    merged = _merge(y_ssd, y_swa, y_xa, w_ssd_o.astype(BF16), w_swa_o.astype(BF16), w_xa_o.astype(BF16), pb)
    h1, hp = _out_ln(merged, w_out.astype(BF16), h2d, ln1_g[None, :], ln1_b[None, :], alpha)

    rwt = router_w.T
    r_hi = rwt.astype(BF16)
    r_lo = (rwt - r_hi.astype(F32)).astype(BF16)
    ti = jnp.arange(TM_RT)
    su = (ti[:, None] < ti[None, :]).astype(BF16)
    idx_t, w_t, rank_t, cnt = _router(h1, r_hi, r_lo, router_bias[:, None].astype(F32), su)

    counts = cnt[:, 0].astype(I32)
    padded = (counts + MOE_BLK - 1) // MOE_BLK * MOE_BLK
    pend = jnp.cumsum(padded)
    pstart = pend - padded
    n_blocks = t * TOP_K // MOE_BLK + N_EXPERTS
    blk_lo = jnp.arange(n_blocks, dtype=I32) * MOE_BLK
    block_e = jnp.minimum(jnp.sum((pend[None, :] <= blk_lo[:, None]).astype(I32), axis=1), N_EXPERTS - 1)
    of_e = block_e[:, None] == jnp.arange(N_EXPERTS, dtype=I32)[None, :]
    valid_end = jnp.sum(jnp.where(of_e, (pstart + counts)[None, :], 0), axis=1)
    block_nv = jnp.clip(valid_end - blk_lo, 0, MOE_BLK).astype(I32)
    eids = jnp.arange(N_EXPERTS, dtype=I32)
    later_active = (eids[None, :] > eids[:, None]) & (padded > 0)[None, :]
    next_active = jnp.min(jnp.where(later_active, eids[None, :], N_EXPERTS), axis=1)
    next_active = jnp.where(next_active >= N_EXPERTS, -1, next_active)
    seg_start = jnp.sum(jnp.where(of_e, pstart[None, :], 0), axis=1)
    block_first = ((blk_lo == seg_start) & (block_nv > 0)).astype(I32)
    block_next = jnp.sum(jnp.where(of_e, next_active[None, :], 0), axis=1).astype(I32)
    nn_of = next_active[:, None] == eids[None, :]
    next2_active = jnp.where(next_active >= 0, jnp.sum(jnp.where(nn_of, next_active[None, :], 0), axis=1), -1)
    block_next2 = jnp.sum(jnp.where(of_e, next2_active[None, :], 0), axis=1).astype(I32)

    dest_flat = _dest(idx_t, rank_t, pstart.astype(F32)[:, None]).reshape(-1)
    n_slots = n_blocks * MOE_BLK
    x_sorted, shared = _dispatch(dest_flat, hp, h1, ws1.astype(BF16), ws3.astype(BF16), ws2.astype(BF16),
                                 n_slots)
    ys = _experts(block_e, block_nv, block_first, block_next, block_next2,
                  x_sorted.reshape(n_slots * ROW_TILE, LANES), w1, w3, w2)
    return _final(dest_flat, ys.reshape(n_slots, ROW_TILE, LANES), w_t.T, h1, shared, ln2_g[None, :],
                  ln2_b[None, :], alpha)


def kernel(x, mem, w_in, conv_w, conv_b, dt_bias, a_log, d_skip, ssd_norm_g, swa_sinks, rel_bias, w_mem_kv, w_ssd_o, w_swa_o, w_xa_o, w_out, ln1_g, ln1_b, router_w, router_bias, w1, w3, w2, ws1, ws3, ws2, ln2_g, ln2_b):
    batch, seq, d = x.shape
    depth = w_in.shape[0]
    alpha = (2.0 * depth) ** 0.25
    h = x.reshape(batch * seq, d)
    mem2d = mem.reshape(batch * MEM_LEN, d)
    for i in range(depth):
        h = _layer(h, mem2d, batch, seq, w_in[i], conv_w[i], conv_b[i], dt_bias[i], a_log[i], d_skip[i],
                   ssd_norm_g[i], swa_sinks[i], rel_bias, w_mem_kv[i], w_ssd_o[i], w_swa_o[i], w_xa_o[i], w_out[i],
                   ln1_g[i], ln1_b[i], router_w[i], router_bias[i], w1[i], w3[i], w2[i], ws1[i], ws3[i], ws2[i],
                   ln2_g[i], ln2_b[i], alpha)
    return h.reshape(batch, seq, d)
```

```python
import functools
import math

import jax
import jax.numpy as jnp
from jax import lax
from jax.experimental import pallas as pl
from jax.experimental.pallas import tpu as pltpu

F32 = jnp.float32
BF16 = jnp.bfloat16
I32 = jnp.int32
U32 = jnp.uint32

D_MODEL = 2048
MEM_LEN = 256
SSD_D_INNER = D_MODEL
SSD_HEAD_DIM = 64
SSD_HEADS = 32
SSD_GROUPS = 4
SSD_STATE = 128
SSD_CONV = 4
SSD_CHUNK = 128
SWA_HEADS = 16
SWA_KV_HEADS = 4
SWA_HEAD_DIM = 64
SWA_WINDOW = 128
SWA_BLOCK = 128
REL_BUCKETS = 32
REL_MAX_DIST = 128
XA_HEADS = 4
XA_HEAD_DIM = 256
N_BRANCH = 3
N_EXPERTS = 64
TOP_K = 8
N_EXPERT_GROUPS = 8
TOPK_GROUPS = 4
EXPERT_DIM = 512
SHARED_DIM = 512
ROUTED_SCALE = 2.5
LN_EPS = 1e-5
RMS_EPS = 1e-5

SWA_Q_DIM = SWA_HEADS * SWA_HEAD_DIM
SWA_KV_DIM = SWA_KV_HEADS * SWA_HEAD_DIM
XA_DIM = XA_HEADS * XA_HEAD_DIM
BC_DIM = 2 * SSD_GROUPS * SSD_STATE

W_A = SSD_D_INNER + SSD_D_INNER + BC_DIM
OFFA_Z = 0
OFFA_XS = OFFA_Z + SSD_D_INNER
OFFA_BC = OFFA_XS + SSD_D_INNER
W_B = SWA_Q_DIM + 2 * SWA_KV_DIM + XA_DIM + N_BRANCH * D_MODEL
OFFB_Q = 0
OFFB_K = OFFB_Q + SWA_Q_DIM
OFFB_V = OFFB_K + SWA_KV_DIM
OFFB_QX = OFFB_V + SWA_KV_DIM
OFFB_GATES = OFFB_QX + XA_DIM

LANES = 128
MOE_BLK = 256
NEG_INF = float("-inf")


def _cparams(sem, vmem_mb=48):
    return pltpu.CompilerParams(dimension_semantics=sem, vmem_limit_bytes=vmem_mb * 1024 * 1024)


def _dot(a, b):
    return jnp.dot(a, b, preferred_element_type=F32)


def _dot_nt(a, b):
    return lax.dot_general(a, b, (((1,), (1,)), ((), ())), preferred_element_type=F32)


def _sigmoid(x):
    return 1.0 / (1.0 + jnp.exp(-x))


def _silu(x):
    return x * _sigmoid(x)


TM_IN = 1024
TN_IN = 1024


def _inproj_kernel(x_ref, w_ref, wdt_ref, p_ref, dt_ref, xb_ref):
    @pl.when(pl.program_id(1) == 0)
    def _():
        xb = x_ref[...].astype(BF16)
        xb_ref[...] = xb
        dt_ref[...] = _dot(xb, wdt_ref[...])

    p_ref[...] = _dot(xb_ref[...], w_ref[...].astype(BF16)).astype(BF16)


def _inproj(x2d, w_in, wdt):
    t = x2d.shape[0]
    return pl.pallas_call(
        _inproj_kernel,
        grid=(t // TM_IN, W_A // TN_IN),
        in_specs=[
            pl.BlockSpec((TM_IN, D_MODEL), lambda i, j: (i, 0)),
            pl.BlockSpec((D_MODEL, TN_IN), lambda i, j: (0, j)),
            pl.BlockSpec((D_MODEL, LANES), lambda i, j: (0, 0)),
        ],
        out_specs=[
            pl.BlockSpec((TM_IN, TN_IN), lambda i, j: (i, j)),
            pl.BlockSpec((TM_IN, LANES), lambda i, j: (i, 0)),
            pl.BlockSpec((TM_IN, D_MODEL), lambda i, j: (i, 0)),
        ],
        out_shape=[jax.ShapeDtypeStruct((t, W_A), BF16), jax.ShapeDtypeStruct((t, LANES), F32),
                   jax.ShapeDtypeStruct((t, D_MODEL), BF16)],
        compiler_params=_cparams(("parallel", "arbitrary"), vmem_mb=56),
        name="inproj",
    )(x2d, w_in, wdt)


TR_WC = 512
TN_WC = 512


def _wcast_kernel(shift, a_ref, b_ref, o_ref):
    if shift == 0:
        o_ref[...] = a_ref[...].astype(BF16)
    else:
        x = jnp.concatenate([a_ref[...], b_ref[...]], axis=1)
        o_ref[...] = pltpu.roll(x, TN_WC + LANES - shift, axis=1)[:, :TN_WC].astype(BF16)


def _wcast(w, start, width, name):
    rows = w.shape[0]
    base = start // LANES * LANES
    per = TN_WC // LANES
    return pl.pallas_call(
        functools.partial(_wcast_kernel, start - base),
        grid=(rows // TR_WC, width // TN_WC),
        in_specs=[
            pl.BlockSpec((TR_WC, TN_WC), lambda r, j: (r, base // TN_WC + j)),
            pl.BlockSpec((TR_WC, LANES), lambda r, j: (r, base // LANES + (j + 1) * per)),
        ],
        out_specs=pl.BlockSpec((TR_WC, TN_WC), lambda r, j: (r, j)),
        out_shape=jax.ShapeDtypeStruct((rows, width), BF16),
        compiler_params=_cparams(("parallel", "parallel")),
        name=name,
    )(w, w)


def _mm_kernel(a_ref, b_ref, o_ref):
    o_ref[...] = _dot(a_ref[...].astype(BF16), b_ref[...]).astype(o_ref.dtype)


def _mm(a, b, out_dtype, tm, tn, name, vmem_mb=48):
    m, k = a.shape
    n = b.shape[1]
    return pl.pallas_call(
        _mm_kernel,
        grid=(m // tm, n // tn),
        in_specs=[pl.BlockSpec((tm, k), lambda i, j: (i, 0)), pl.BlockSpec((k, tn), lambda i, j: (0, j))],
        out_specs=pl.BlockSpec((tm, tn), lambda i, j: (i, j)),
        out_shape=jax.ShapeDtypeStruct((m, n), out_dtype),
        compiler_params=_cparams(("parallel", "parallel"), vmem_mb=vmem_mb),
        name=name,
    )(a, b)


Q = SSD_CHUNK


def _conv_shift_matrix():
    t = jnp.arange(Q)[:, None]
    j = jnp.arange(2 * Q)[None, :]
    blocks = []
    for k in range(SSD_CONV - 1):
        src = t - (SSD_CONV - 1 - k)
        blocks.append(j == jnp.where(src >= 0, src, 2 * Q + src))
    return jnp.concatenate(blocks, axis=0).astype(BF16)


def _ssd_kernel(xs_ref, bc_ref, z_ref, dtr_ref, shift_ref, cwx_ref, cbx_ref, cwb_ref, cbb_ref, dtb_ref, alog_ref,
                dsk_ref, ng_ref, o_ref, hx_ref, hb_ref, st_ref, cumt_ref, dtt_ref, wt_ref, y_ref):
    c = pl.program_id(1)

    @pl.when(c == 0)
    def _():
        hx_ref[...] = jnp.zeros_like(hx_ref)
        hb_ref[...] = jnp.zeros_like(hb_ref)
        st_ref[...] = jnp.zeros_like(st_ref)

    def conv_silu(in_ref, prev_ref, w_ref, b_ref):
        cur = in_ref[...]
        ext = jnp.concatenate([cur, prev_ref[...]], axis=0)
        shifted = _dot(shift_ref[...], ext)
        acc = b_ref[...] + w_ref[SSD_CONV - 1:SSD_CONV, :] * cur.astype(F32)
        for k in range(SSD_CONV - 1):
            acc = acc + w_ref[k:k + 1, :] * shifted[k * Q:(k + 1) * Q, :]
        prev_ref[...] = cur
        return _silu(acc)

    xs_c = conv_silu(xs_ref, hx_ref, cwx_ref, cbx_ref)
    bc_c = conv_silu(bc_ref, hb_ref, cwb_ref, cbb_ref)

    lane = lax.broadcasted_iota(I32, (Q, LANES), 1)
    row = lax.broadcasted_iota(I32, (Q, LANES), 0)
    tril = row >= lane
    lo_mask = lane < SSD_HEAD_DIM

    dtv = dtr_ref[...] + dtb_ref[...]
    dt = jnp.maximum(dtv, 0.0) + jnp.log1p(jnp.exp(-jnp.abs(dtv)))
    a = jnp.where(lane[0:1, :] < SSD_HEADS, -jnp.exp(alog_ref[...]), 0.0)
    la = dt * a
    tri = jnp.where(tril, 1.0, 0.0).astype(BF16)
    p1 = la.astype(BF16)
    r1 = la - p1.astype(F32)
    p2 = r1.astype(BF16)
    p3 = (r1 - p2.astype(F32)).astype(BF16)
    cum = _dot(tri, p1) + _dot(tri, p2) + _dot(tri, p3)
    ecum = jnp.exp(cum)
    cum_t = cum.T
    dt_t = dt.T
    last = cum_t[:, Q - 1:Q]
    cumt_ref[...] = cum_t
    dtt_ref[...] = dt_t
    wt_ref[...] = jnp.exp(last - cum_t) * dt_t
    chunk_decay = jnp.exp(last)

    def head_parts(h, cb, cm, bm_t):
        col = jnp.broadcast_to(cum[:, h:h + 1], (Q, Q))
        rw = cumt_ref[h:h + 1, :]
        dec = jnp.exp(jnp.where(tril, col - rw, NEG_INF))
        m = cb * (dec * dtt_ref[h:h + 1, :])
        ce = cm * jnp.broadcast_to(ecum[:, h:h + 1], (Q, Q))
        sl = bm_t * wt_ref[h:h + 1, :]
        return m.astype(BF16), ce.astype(BF16), sl.astype(BF16)

    for g in range(SSD_GROUPS):
        bm = bc_c[:, g * SSD_STATE:(g + 1) * SSD_STATE]
        cm = bc_c[:, BC_DIM // 2 + g * SSD_STATE:BC_DIM // 2 + (g + 1) * SSD_STATE]
        cb = _dot_nt(cm.astype(BF16), bm.astype(BF16))
        bm_t = bm.T
        for qd in range(SSD_HEADS // SSD_GROUPS // 2):
            hp = g * (SSD_HEADS // SSD_GROUPS // 2) + qd
            ha, hb = 2 * hp, 2 * hp + 1
            sl_ = slice(hp * LANES, (hp + 1) * LANES)
            xs_pair = xs_c[:, sl_]
            x_lo = jnp.where(lo_mask, xs_pair, 0.0).astype(BF16)
            x_hi = jnp.where(lo_mask, 0.0, xs_pair).astype(BF16)
            h_prev = st_ref[:, sl_]
            h_lo = jnp.where(lo_mask, h_prev, 0.0).astype(BF16)
            h_hi = jnp.where(lo_mask, 0.0, h_prev).astype(BF16)
            m_a, ce_a, sl_a = head_parts(ha, cb, cm, bm_t)
            m_b, ce_b, sl_b = head_parts(hb, cb, cm, bm_t)
            lhs = jnp.concatenate([m_a, ce_a, m_b, ce_b], axis=1)
            rhs = jnp.concatenate([x_lo, h_lo, x_hi, h_hi], axis=0)
            y_ref[:, sl_] = _dot(lhs, rhs)
            new = _dot(jnp.concatenate([sl_a, sl_b], axis=1), jnp.concatenate([x_lo, x_hi], axis=0))
            cd = jnp.where(lo_mask[0:1, :], chunk_decay[ha:ha + 1, :], chunk_decay[hb:hb + 1, :])
            st_ref[:, sl_] = h_prev * cd + new

    y = y_ref[...] + xs_c * dsk_ref[...]
    y = y * _silu(z_ref[...].astype(F32))
    gw = SSD_D_INNER // SSD_GROUPS
    for g in range(SSD_GROUPS):
        yg = y[:, g * gw:(g + 1) * gw]
        ms = jnp.mean(yg * yg, axis=-1, keepdims=True)
        o_ref[:, g * gw:(g + 1) * gw] = ((yg * lax.rsqrt(ms + RMS_EPS)) * ng_ref[:, g * gw:(g + 1) * gw]).astype(BF16)


def _ssd(p, dt_raw, cwx, cbx, cwb, cbb, dtb, alog, dsk, ng, batch, seq):
    nc = seq // Q
    t = batch * seq
    rowblk = lambda b, c: b * nc + c
    const = lambda b, c: (0, 0)
    return pl.pallas_call(
        _ssd_kernel,
        grid=(batch, nc),
        in_specs=[
            pl.BlockSpec((Q, SSD_D_INNER), lambda b, c: (rowblk(b, c), OFFA_XS // SSD_D_INNER)),
            pl.BlockSpec((Q, BC_DIM), lambda b, c: (rowblk(b, c), OFFA_BC // BC_DIM)),
            pl.BlockSpec((Q, SSD_D_INNER), lambda b, c: (rowblk(b, c), OFFA_Z // SSD_D_INNER)),
            pl.BlockSpec((Q, LANES), lambda b, c: (rowblk(b, c), 0)),
            pl.BlockSpec(((SSD_CONV - 1) * Q, 2 * Q), const),
            pl.BlockSpec((SSD_CONV, SSD_D_INNER), const),
            pl.BlockSpec((1, SSD_D_INNER), const),
            pl.BlockSpec((SSD_CONV, BC_DIM), const),
            pl.BlockSpec((1, BC_DIM), const),
            pl.BlockSpec((1, LANES), const),
            pl.BlockSpec((1, LANES), const),
            pl.BlockSpec((1, SSD_D_INNER), const),
            pl.BlockSpec((1, SSD_D_INNER), const),
        ],
        out_specs=pl.BlockSpec((Q, SSD_D_INNER), lambda b, c: (rowblk(b, c), 0)),
        out_shape=jax.ShapeDtypeStruct((t, SSD_D_INNER), BF16),
        scratch_shapes=[
            pltpu.VMEM((Q, SSD_D_INNER), BF16),
            pltpu.VMEM((Q, BC_DIM), BF16),
            pltpu.VMEM((SSD_STATE, SSD_D_INNER), F32),
            pltpu.VMEM((LANES, Q), F32),
            pltpu.VMEM((LANES, Q), F32),
            pltpu.VMEM((LANES, Q), F32),
            pltpu.VMEM((Q, SSD_D_INNER), F32),
        ],
        compiler_params=_cparams(("parallel", "arbitrary")),
        name="ssd",
    )(p, p, p, dt_raw, _conv_shift_matrix(), cwx, cbx, cwb, cbb, dtb, alog, dsk, ng)


N_PAIRS = SWA_HEADS // 2


def _swa_kernel(sink_ref, q_ref, kp_ref, kc_ref, vp_ref, vc_ref, bias_ref, o_ref):
    n = pl.program_id(1)
    blk = SWA_BLOCK
    kband = jnp.concatenate([kp_ref[...], kc_ref[...]], axis=0)
    vband = jnp.concatenate([vp_ref[...], vc_ref[...]], axis=0)

    lane = lax.broadcasted_iota(I32, (1, LANES), 1)
    lo_f = jnp.where(lane < SWA_HEAD_DIM, 1.0, 0.0)
    lo_m = lo_f.astype(BF16)
    hi_m = (1.0 - lo_f).astype(BF16)

    def head_twice(band, kv):
        tile = band[:, (kv // 2) * LANES:(kv // 2 + 1) * LANES]
        swapped = jnp.concatenate([tile[:, SWA_HEAD_DIM:], tile[:, :SWA_HEAD_DIM]], axis=1)
        return tile * lo_m + swapped * hi_m if kv % 2 == 0 else swapped * lo_m + tile * hi_m

    k_twice = [head_twice(kband, kv) for kv in range(SWA_KV_HEADS)]
    v_twice = [head_twice(vband, kv) for kv in range(SWA_KV_HEADS)]
    colj = lax.broadcasted_iota(I32, (2 * blk, 2 * blk), 1)
    rowi = lax.broadcasted_iota(I32, (2 * blk, 1), 0)
    valid = (colj >= blk) | (n > 0)
    scale = SWA_HEAD_DIM ** -0.5
    for p in range(N_PAIRS):
        kv = p // (SWA_HEADS // SWA_KV_HEADS // 2)
        qp = q_ref[:, p * LANES:(p + 1) * LANES]
        qs = jnp.concatenate([qp * lo_m, qp * hi_m], axis=0)
        s = _dot_nt(qs, k_twice[kv]) * scale + bias_ref[p]
        s = jnp.where(valid, s, NEG_INF)
        sink = jnp.where(rowi < blk, sink_ref[2 * p], sink_ref[2 * p + 1])
        m = jnp.maximum(jnp.max(s, axis=-1, keepdims=True), sink)
        e = jnp.exp(s - m)
        den = jnp.sum(e, axis=-1, keepdims=True) + jnp.exp(sink - m)
        pr = (e * (1.0 / den)).astype(BF16)
        vd = v_twice[kv]
        lhs = jnp.concatenate([pr[:blk], pr[blk:]], axis=1)
        rhs = jnp.concatenate([vd * lo_m, vd * hi_m], axis=0)
        o_ref[:, p * LANES:(p + 1) * LANES] = _dot(lhs, rhs).astype(BF16)


def _swa(p, sinks, bias, batch, seq):
    nb = seq // SWA_BLOCK
    t = batch * seq
    kw = SWA_KV_DIM
    cur = lambda b, n, s: b * nb + n
    prev = lambda b, n, s: b * nb + jnp.maximum(n - 1, 0)
    return pl.pallas_call(
        _swa_kernel,
        grid_spec=pltpu.PrefetchScalarGridSpec(
            num_scalar_prefetch=1,
            grid=(batch, nb),
            in_specs=[
                pl.BlockSpec((SWA_BLOCK, SWA_Q_DIM), lambda b, n, s: (cur(b, n, s), OFFB_Q // SWA_Q_DIM)),
                pl.BlockSpec((SWA_BLOCK, kw), lambda b, n, s: (prev(b, n, s), OFFB_K // kw)),
                pl.BlockSpec((SWA_BLOCK, kw), lambda b, n, s: (cur(b, n, s), OFFB_K // kw)),
                pl.BlockSpec((SWA_BLOCK, kw), lambda b, n, s: (prev(b, n, s), OFFB_V // kw)),
                pl.BlockSpec((SWA_BLOCK, kw), lambda b, n, s: (cur(b, n, s), OFFB_V // kw)),
                pl.BlockSpec((N_PAIRS, 2 * SWA_BLOCK, 2 * SWA_BLOCK), lambda b, n, s: (0, 0, 0)),
            ],
            out_specs=pl.BlockSpec((SWA_BLOCK, SWA_Q_DIM), lambda b, n, s: (cur(b, n, s), 0)),
        ),
        out_shape=jax.ShapeDtypeStruct((t, SWA_Q_DIM), BF16),
        compiler_params=_cparams(("parallel", "arbitrary")),
        name="swa",
    )(sinks, p, p, p, p, p, bias)


def _t5_causal_bucket(dist):
    max_exact = REL_BUCKETS // 2
    large = max_exact + (jnp.log(jnp.maximum(dist, 1).astype(F32) / max_exact)
                         / math.log(REL_MAX_DIST / max_exact) * (REL_BUCKETS - max_exact)).astype(I32)
    large = jnp.minimum(large, REL_BUCKETS - 1)
    return jnp.where(dist < max_exact, dist, large)


def _swa_bias_table(rel_bias):
    qi = jnp.arange(SWA_BLOCK)[:, None]
    kj = jnp.arange(2 * SWA_BLOCK)[None, :]
    dist = qi + SWA_BLOCK - kj
    in_window = (dist >= 0) & (dist < SWA_WINDOW)
    bucket = _t5_causal_bucket(jnp.maximum(dist, 0))
    onehot = (bucket[:, :, None] == jnp.arange(REL_BUCKETS)[None, None, :]).astype(F32)
    bias = jnp.einsum('ijb,bh->hij', onehot, rel_bias.astype(F32), precision=lax.Precision.HIGHEST)
    bias = jnp.where(in_window[None, :, :], bias, NEG_INF)
    return bias.reshape(N_PAIRS, 2 * SWA_BLOCK, 2 * SWA_BLOCK)


TL_XA = 512


def _xa_kernel(q_ref, mk_ref, mv_ref, o_ref):
    s = _dot_nt(q_ref[...], mk_ref[...]) * (XA_HEAD_DIM ** -0.5)
    m = jnp.max(s, axis=-1, keepdims=True)
    e = jnp.exp(s - m)
    pr = e * (1.0 / jnp.sum(e, axis=-1, keepdims=True))
    o_ref[...] = _dot(pr.astype(BF16), mv_ref[...]).astype(BF16)


def _xa(p, memkv, batch, seq):
    nl = seq // TL_XA
    t = batch * seq
    return pl.pallas_call(
        _xa_kernel,
        grid=(batch, nl, XA_HEADS),
        in_specs=[
            pl.BlockSpec((TL_XA, XA_HEAD_DIM), lambda b, l, h: (b * nl + l, OFFB_QX // XA_HEAD_DIM + h)),
            pl.BlockSpec((MEM_LEN, XA_HEAD_DIM), lambda b, l, h: (b, h)),
            pl.BlockSpec((MEM_LEN, XA_HEAD_DIM), lambda b, l, h: (b, XA_HEADS + h)),
        ],
        out_specs=pl.BlockSpec((TL_XA, XA_HEAD_DIM), lambda b, l, h: (b * nl + l, h)),
        out_shape=jax.ShapeDtypeStruct((t, XA_DIM), BF16),
        compiler_params=_cparams(("parallel", "parallel", "parallel")),
        name="xattn",
    )(p, memkv, memkv)


TM_MG = 1024
TN_MG = 512


N_SUB_MG = 2


def _merge_kernel(ys_ref, yw_ref, yx_ref, ws_ref, ww_ref, wx_ref, g0_ref, g1_ref, g2_ref, o_ref):
    sub = TM_MG // N_SUB_MG
    for s in range(N_SUB_MG):
        rows = slice(s * sub, (s + 1) * sub)
        a = _dot(ys_ref[rows, :], ws_ref[...])
        b = _dot(yw_ref[rows, :], ww_ref[...])
        c = _dot(yx_ref[rows, :], wx_ref[...])
        g0 = _sigmoid(g0_ref[rows, :].astype(F32))
        g1 = _sigmoid(g1_ref[rows, :].astype(F32))
        g2 = _sigmoid(g2_ref[rows, :].astype(F32))
        o_ref[rows, :] = (g0 * a + g1 * b + g2 * c).astype(BF16)


def _merge(y_ssd, y_swa, y_xa, w_ssd_o, w_swa_o, w_xa_o, p):
    t = y_ssd.shape[0]
    gpb = D_MODEL // TN_MG
    gate = lambda k: pl.BlockSpec((TM_MG, TN_MG), lambda i, j: (i, OFFB_GATES // TN_MG + k * gpb + j))
    return pl.pallas_call(
        _merge_kernel,
        grid=(t // TM_MG, D_MODEL // TN_MG),
        in_specs=[
            pl.BlockSpec((TM_MG, SSD_D_INNER), lambda i, j: (i, 0)),
            pl.BlockSpec((TM_MG, SWA_Q_DIM), lambda i, j: (i, 0)),
            pl.BlockSpec((TM_MG, XA_DIM), lambda i, j: (i, 0)),
            pl.BlockSpec((SSD_D_INNER, TN_MG), lambda i, j: (0, j)),
            pl.BlockSpec((SWA_Q_DIM, TN_MG), lambda i, j: (0, j)),
            pl.BlockSpec((XA_DIM, TN_MG), lambda i, j: (0, j)),
            gate(0), gate(1), gate(2),
        ],
        out_specs=pl.BlockSpec((TM_MG, TN_MG), lambda i, j: (i, j)),
        out_shape=jax.ShapeDtypeStruct((t, D_MODEL), BF16),
        compiler_params=_cparams(("parallel", "parallel")),
        name="merge",
    )(y_ssd, y_swa, y_xa, w_ssd_o, w_swa_o, w_xa_o, p, p, p)


TM_LN = 512


def _layer_norm(v, g, b):
    mu = jnp.mean(v, axis=-1, keepdims=True)
    d = v - mu
    var = jnp.mean(d * d, axis=-1, keepdims=True)
    return (d * lax.rsqrt(var + LN_EPS)) * g + b


def _pack_halves(v):
    n = v.shape[1] // 2
    lo = lax.bitcast_convert_type(v[:, :n].astype(BF16).astype(F32), U32) >> 16
    hi = lax.bitcast_convert_type(v[:, n:].astype(BF16).astype(F32), U32) & jnp.uint32(0xFFFF0000)
    return hi | lo


def _unpack_halves(w):
    lo = lax.bitcast_convert_type(w << 16, F32)
    hi = lax.bitcast_convert_type(w & jnp.uint32(0xFFFF0000), F32)
    return lo, hi


ROW_WORDS = D_MODEL // 2
ROW_TILE = ROW_WORDS // LANES


def _store_row_tiled(ref, words):
    m = words.shape[0]
    for c in range(ROW_TILE):
        ref[pl.ds(c, m, stride=ROW_TILE), :] = words[:, c * LANES:(c + 1) * LANES]


def _load_row_tiled(ref, m):
    return jnp.concatenate([ref[pl.ds(c, m, stride=ROW_TILE), :] for c in range(ROW_TILE)], axis=1)


N_SUB_LN = 4


def _out_ln_kernel(alpha, m_ref, w_ref, x_ref, g_ref, b_ref, h_ref, hp_ref):
    sub = TM_LN // N_SUB_LN
    for c in range(N_SUB_LN):
        rows = slice(c * sub, (c + 1) * sub)
        mix = _dot(m_ref[rows, :], w_ref[...])
        hn = _layer_norm(alpha * x_ref[rows, :] + mix, g_ref[...], b_ref[...])
        h_ref[rows, :] = hn
        _store_row_tiled(hp_ref.at[pl.ds(c * sub * ROW_TILE, sub * ROW_TILE), :], _pack_halves(hn))


def _out_ln(merged, w_out, x2d, g, b, alpha):
    t = x2d.shape[0]
    row = lambda i: (i, 0)
    const = lambda i: (0, 0)
    return pl.pallas_call(
        functools.partial(_out_ln_kernel, alpha),
        grid=(t // TM_LN,),
        in_specs=[
            pl.BlockSpec((TM_LN, D_MODEL), row),
            pl.BlockSpec((D_MODEL, D_MODEL), const),
            pl.BlockSpec((TM_LN, D_MODEL), row),
            pl.BlockSpec((1, D_MODEL), const),
            pl.BlockSpec((1, D_MODEL), const),
        ],
        out_specs=[pl.BlockSpec((TM_LN, D_MODEL), row), pl.BlockSpec((TM_LN * ROW_TILE, LANES), row)],
        out_shape=[jax.ShapeDtypeStruct((t, D_MODEL), F32), jax.ShapeDtypeStruct((t * ROW_TILE, LANES), U32)],
        compiler_params=_cparams(("parallel",)),
        name="out_ln1",
    )(merged, w_out, x2d, g, b)


TM_RT = 512
EPG = N_EXPERTS // N_EXPERT_GROUPS


def _router_kernel(h_ref, whi_ref, wlo_ref, bias_ref, su_ref, idx_ref, w_ref, rank_ref, cnt_ref, run_ref):
    @pl.when(pl.program_id(0) == 0)
    def _():
        run_ref[...] = jnp.zeros_like(run_ref)

    h = h_ref[...]
    h_hi = h.astype(BF16)
    h_lo = (h - h_hi.astype(F32)).astype(BF16)
    whi = whi_ref[...]
    logits = _dot_nt(whi, h_hi) + _dot_nt(whi, h_lo) + _dot_nt(wlo_ref[...], h_hi)
    scores = _sigmoid(logits)
    sel = scores + bias_ref[...]

    iota_m = lax.broadcasted_iota(I32, (EPG, TM_RT), 0).astype(F32)
    iota_g = lax.broadcasted_iota(I32, (N_EXPERT_GROUPS, TM_RT), 0).astype(F32)
    gs = []
    for g in range(N_EXPERT_GROUPS):
        v = sel[g * EPG:(g + 1) * EPG, :]
        m1 = jnp.max(v, axis=0, keepdims=True)
        i1 = jnp.min(jnp.where(v == m1, iota_m, float(EPG)), axis=0, keepdims=True)
        m2 = jnp.max(jnp.where(iota_m == i1, NEG_INF, v), axis=0, keepdims=True)
        gs.append(m1 + m2)
    gsc = jnp.concatenate(gs, axis=0)
    chosen = jnp.zeros((N_EXPERT_GROUPS, TM_RT), F32)
    for _ in range(TOPK_GROUPS):
        m = jnp.max(gsc, axis=0, keepdims=True)
        ig = jnp.min(jnp.where(gsc == m, iota_g, float(N_EXPERT_GROUPS)), axis=0, keepdims=True)
        hit = iota_g == ig
        chosen = jnp.where(hit, 1.0, chosen)
        gsc = jnp.where(hit, NEG_INF, gsc)
    gmask = jnp.concatenate(
        [jnp.broadcast_to(chosen[g:g + 1, :], (EPG, TM_RT)) for g in range(N_EXPERT_GROUPS)], axis=0)
    masked = jnp.where(gmask > 0.0, sel, NEG_INF)

    iota_e = lax.broadcasted_iota(I32, (N_EXPERTS, TM_RT), 0).astype(F32)
    hits, idxs, ws = [], [], []
    for _ in range(TOP_K):
        m = jnp.max(masked, axis=0, keepdims=True)
        ie = jnp.min(jnp.where(masked == m, iota_e, float(N_EXPERTS)), axis=0, keepdims=True)
        hit = iota_e == ie
        hits.append(hit)
        idxs.append(ie)
        ws.append(jnp.sum(jnp.where(hit, scores, 0.0), axis=0, keepdims=True))
        masked = jnp.where(hit, NEG_INF, masked)
    wsum = ws[0]
    for k in range(1, TOP_K):
        wsum = wsum + ws[k]
    inv = 1.0 / wsum
    idx_ref[...] = jnp.concatenate(idxs, axis=0).astype(I32)
    w_ref[...] = jnp.concatenate([w * inv * ROUTED_SCALE for w in ws], axis=0)

    assign = jnp.zeros((N_EXPERTS, TM_RT), F32)
    for hit in hits:
        assign = jnp.where(hit, 1.0, assign)
    pos = _dot(assign.astype(BF16), su_ref[...]) + run_ref[...]
    rank_ref[...] = jnp.concatenate(
        [jnp.sum(jnp.where(hit, pos, 0.0), axis=0, keepdims=True) for hit in hits], axis=0).astype(I32)
    run = run_ref[...] + jnp.sum(assign, axis=1, keepdims=True)
    run_ref[...] = run
    cnt_ref[...] = run


def _router(h1, r_hi, r_lo, rbias, su):
    t = h1.shape[0]
    tile = lambda i: (0, i)
    const = lambda i: (0, 0)
    return pl.pallas_call(
        _router_kernel,
        grid=(t // TM_RT,),
        in_specs=[
            pl.BlockSpec((TM_RT, D_MODEL), lambda i: (i, 0)),
            pl.BlockSpec((N_EXPERTS, D_MODEL), const),
            pl.BlockSpec((N_EXPERTS, D_MODEL), const),
            pl.BlockSpec((N_EXPERTS, 1), const),
            pl.BlockSpec((TM_RT, TM_RT), const),
        ],
        out_specs=[
            pl.BlockSpec((TOP_K, TM_RT), tile),
            pl.BlockSpec((TOP_K, TM_RT), tile),
            pl.BlockSpec((TOP_K, TM_RT), tile),
            pl.BlockSpec((N_EXPERTS, TM_RT), const),
        ],
        out_shape=[
            jax.ShapeDtypeStruct((TOP_K, t), I32),
            jax.ShapeDtypeStruct((TOP_K, t), F32),
            jax.ShapeDtypeStruct((TOP_K, t), I32),
            jax.ShapeDtypeStruct((N_EXPERTS, TM_RT), F32),
        ],
        scratch_shapes=[pltpu.VMEM((N_EXPERTS, TM_RT), F32)],
        compiler_params=_cparams(("arbitrary",)),
        name="router",
    )(h1, r_hi, r_lo, rbias, su)


TM_DS = 2048


def _dest_kernel(idx_ref, rank_ref, ps_ref, d_ref):
    iota_e = lax.broadcasted_iota(I32, (N_EXPERTS, TM_DS), 0)
    ps = ps_ref[...]
    rows = []
    for k in range(TOP_K):
        hit = iota_e == idx_ref[k:k + 1, :]
        rows.append(jnp.sum(jnp.where(hit, ps, 0.0), axis=0, keepdims=True))
    d_ref[...] = jnp.concatenate(rows, axis=0).astype(I32) + rank_ref[...]


def _dest(idx_t, rank_t, pstart):
    t = idx_t.shape[1]
    tile = lambda i: (0, i)
    return pl.pallas_call(
        _dest_kernel,
        grid=(t // TM_DS,),
        in_specs=[pl.BlockSpec((TOP_K, TM_DS), tile), pl.BlockSpec((TOP_K, TM_DS), tile),
                  pl.BlockSpec((N_EXPERTS, 1), lambda i: (0, 0))],
        out_specs=pl.BlockSpec((TOP_K, TM_DS), tile),
        out_shape=jax.ShapeDtypeStruct((TOP_K, t), I32),
        compiler_params=_cparams(("parallel",)),
        name="dest",
    )(idx_t, rank_t, pstart)


TM_DP = 512


def _dispatch_kernel(dest_ref, hp_ref, h_ref, ws1_ref, ws3_ref, ws2_ref, xs_ref, sh_ref, sem):
    t_total = pl.num_programs(0) * TM_DP
    base = pl.program_id(0) * TM_DP

    def issue(t, carry):
        src = hp_ref.at[pl.ds(pl.multiple_of(t * ROW_TILE, ROW_TILE), ROW_TILE), :]
        for k in range(TOP_K):
            slot = dest_ref[k * t_total + base + t]
            pltpu.make_async_copy(src, xs_ref.at[slot], sem).start(priority=k % 2)
        return carry

    lax.fori_loop(0, TM_DP, issue, 0)

    xb = h_ref[...].astype(BF16)
    hdn = _silu(_dot(xb, ws1_ref[...])) * _dot(xb, ws3_ref[...])
    sh_ref[...] = _dot(hdn.astype(BF16), ws2_ref[...])

    for k in range(TOP_K):
        pltpu.make_async_copy(hp_ref.reshape(TM_DP, ROW_TILE, LANES), xs_ref.at[pl.ds(0, TM_DP)], sem).wait()


def _dispatch(dest_flat, hp, h1, ws1, ws3, ws2, n_slots):
    t = h1.shape[0]
    row = lambda i, d: (i, 0)
    const = lambda i, d: (0, 0)
    return pl.pallas_call(
        _dispatch_kernel,
        grid_spec=pltpu.PrefetchScalarGridSpec(
            num_scalar_prefetch=1,
            grid=(t // TM_DP,),
            in_specs=[
                pl.BlockSpec((TM_DP * ROW_TILE, LANES), row),
                pl.BlockSpec((TM_DP, D_MODEL), row),
                pl.BlockSpec((D_MODEL, SHARED_DIM), const),
                pl.BlockSpec((D_MODEL, SHARED_DIM), const),
                pl.BlockSpec((SHARED_DIM, D_MODEL), const),
            ],
            out_specs=[pl.BlockSpec(memory_space=pl.ANY), pl.BlockSpec((TM_DP, D_MODEL), row)],
            scratch_shapes=[pltpu.SemaphoreType.DMA],
        ),
        out_shape=[jax.ShapeDtypeStruct((n_slots, ROW_TILE, LANES), U32), jax.ShapeDtypeStruct((t, D_MODEL), F32)],
        compiler_params=_cparams(("arbitrary",)),
        name="dispatch",
    )(dest_flat, hp, h1, ws1, ws3, ws2)


N_WSLOTS = 3


def _expert_kernel(be_ref, nv_ref, first_ref, nxt_ref, nxt2_ref, x_ref, w1_ref, w3_ref, w2_ref, o_ref,
                   wf1, wf3, wf2, w1b, w3b, w2b, slot_ref, sems):
    b = pl.program_id(0)
    nv = nv_ref[b]

    def weight_copies(e, s):
        return (pltpu.make_async_copy(w1_ref.at[e], wf1.at[s], sems.at[s, 0]),
                pltpu.make_async_copy(w3_ref.at[e], wf3.at[s], sems.at[s, 1]),
                pltpu.make_async_copy(w2_ref.at[e], wf2.at[s], sems.at[s, 2]))

    @pl.when(b == 0)
    def _():
        slot_ref[0] = 0
        for cp in weight_copies(be_ref[0], 0):
            cp.start()

        @pl.when(nxt_ref[0] >= 0)
        def _():
            for cp in weight_copies(nxt_ref[0], 1):
                cp.start(priority=1)

    @pl.when(first_ref[b] == 1)
    def _():
        s = slot_ref[0]
        for cp in weight_copies(be_ref[b], s):
            cp.wait()

        @pl.when(nxt2_ref[b] >= 0)
        def _():
            s2 = jnp.where(s == 0, N_WSLOTS - 1, s - 1)
            for cp in weight_copies(nxt2_ref[b], s2):
                cp.start(priority=1)

        w1b[...] = wf1[s].astype(BF16)
        w3b[...] = wf3[s].astype(BF16)
        w2b[...] = wf2[s].astype(BF16)
        slot_ref[0] = jnp.where(s == N_WSLOTS - 1, 0, s + 1)

    @pl.when(nv > 0)
    def _():
        live = lax.broadcasted_iota(I32, (MOE_BLK, 1), 0) < nv
        lo, hi = _unpack_halves(jnp.where(live, _load_row_tiled(x_ref, MOE_BLK), jnp.uint32(0)))
        x = jnp.concatenate([lo.astype(BF16), hi.astype(BF16)], axis=1)
        hdn = _silu(_dot(x, w1b[...])) * _dot(x, w3b[...])
        _store_row_tiled(o_ref, _pack_halves(_dot(hdn.astype(BF16), w2b[...])))

    @pl.when(nv <= 0)
    def _():
        o_ref[...] = jnp.zeros_like(o_ref)


def _experts(block_e, block_nv, block_first, block_next, block_next2, x_sorted, w1, w3, w2):
    n_rows = x_sorted.shape[0]
    n_blocks = n_rows // (MOE_BLK * ROW_TILE)
    row = lambda b, *_: (b, 0)
    return pl.pallas_call(
        _expert_kernel,
        grid_spec=pltpu.PrefetchScalarGridSpec(
            num_scalar_prefetch=5,
            grid=(n_blocks,),
            in_specs=[
                pl.BlockSpec((MOE_BLK * ROW_TILE, LANES), row),
                pl.BlockSpec(memory_space=pl.ANY),
                pl.BlockSpec(memory_space=pl.ANY),
                pl.BlockSpec(memory_space=pl.ANY),
            ],
            out_specs=pl.BlockSpec((MOE_BLK * ROW_TILE, LANES), row),
            scratch_shapes=[
                pltpu.VMEM((N_WSLOTS, D_MODEL, EXPERT_DIM), F32),
                pltpu.VMEM((N_WSLOTS, D_MODEL, EXPERT_DIM), F32),
                pltpu.VMEM((N_WSLOTS, EXPERT_DIM, D_MODEL), F32),
                pltpu.VMEM((D_MODEL, EXPERT_DIM), BF16),
                pltpu.VMEM((D_MODEL, EXPERT_DIM), BF16),
                pltpu.VMEM((EXPERT_DIM, D_MODEL), BF16),
                pltpu.SMEM((1,), I32),
                pltpu.SemaphoreType.DMA((N_WSLOTS, 3)),
            ],
        ),
        out_shape=jax.ShapeDtypeStruct((n_rows, LANES), U32),
        compiler_params=_cparams(("arbitrary",), vmem_mb=58),
        name="experts",
    )(block_e, block_nv, block_first, block_next, block_next2, x_sorted, w1, w3, w2)


TM_FN = 256


CH_FN = 8


def _final_kernel(alpha, dest_ref, ys_ref, w_ref, h_ref, sh_ref, g_ref, b_ref, o_ref, buf, sems):
    i = pl.program_id(0)
    n_tiles = pl.num_programs(0)
    t_total = n_tiles * TM_FN
    slot = i % 2
    nslot = 1 - slot
    half = D_MODEL // 2

    def issue_token(tile, sl, c, tt):
        tok = c * CH_FN + tt
        row0 = pl.multiple_of(tok * ROW_TILE, ROW_TILE)
        for k in range(TOP_K):
            src = ys_ref.at[dest_ref[k * t_total + tile * TM_FN + tok]]
            pltpu.make_async_copy(src, buf.at[sl, k, pl.ds(row0, ROW_TILE), :], sems.at[sl]).start(priority=k % 2)

    def wait_tile(sl):
        for k in range(TOP_K):
            pltpu.make_async_copy(ys_ref.at[pl.ds(0, TM_FN)], buf.at[sl, k].reshape(TM_FN, ROW_TILE, LANES),
                                  sems.at[sl]).wait()

    @pl.when(i == 0)
    def _():
        def first(c, carry):
            for tt in range(CH_FN):
                issue_token(0, 0, c, tt)
            return carry
        lax.fori_loop(0, TM_FN // CH_FN, first, 0)

    wait_tile(slot)

    nxt_tile = jnp.minimum(i + 1, n_tiles - 1)

    def body(c, carry):
        tok = pl.ds(pl.multiple_of(c * CH_FN, CH_FN), CH_FN)
        acc_lo = sh_ref[tok, :half]
        acc_hi = sh_ref[tok, half:]
        pre = alpha * h_ref[tok, :]
        wrow = w_ref[tok, :]
        for k in range(TOP_K):
            words = jnp.concatenate(
                [buf[slot, k, pl.ds(c * (CH_FN * ROW_TILE) + r, CH_FN, stride=ROW_TILE), :] for r in range(ROW_TILE)],
                axis=1)
            issue_token(nxt_tile, nslot, c, k)
            lo, hi = _unpack_halves(words)
            wk = wrow[:, k:k + 1]
            acc_lo = acc_lo + wk * lo
            acc_hi = acc_hi + wk * hi
        o_ref[tok, :] = pre + jnp.concatenate([acc_lo, acc_hi], axis=1)
        return carry

    lax.fori_loop(0, TM_FN // CH_FN, body, 0)
    o_ref[...] = _layer_norm(o_ref[...], g_ref[...], b_ref[...])

    @pl.when(i == n_tiles - 1)
    def _():
        wait_tile(nslot)


def _final(dest_flat, ys, w_tok, h1, shared, g, b, alpha):
    t = h1.shape[0]
    row = lambda i, d: (i, 0)
    const = lambda i, d: (0, 0)
    return pl.pallas_call(
        functools.partial(_final_kernel, alpha),
        grid_spec=pltpu.PrefetchScalarGridSpec(
            num_scalar_prefetch=1,
            grid=(t // TM_FN,),
            in_specs=[
                pl.BlockSpec(memory_space=pl.ANY),
                pl.BlockSpec((TM_FN, TOP_K), row),
                pl.BlockSpec((TM_FN, D_MODEL), row),
                pl.BlockSpec((TM_FN, D_MODEL), row),
                pl.BlockSpec((1, D_MODEL), const),
                pl.BlockSpec((1, D_MODEL), const),
            ],
            out_specs=pl.BlockSpec((TM_FN, D_MODEL), row),
            scratch_shapes=[pltpu.VMEM((2, TOP_K, TM_FN * ROW_TILE, LANES), U32), pltpu.SemaphoreType.DMA((2,))],
        ),
        out_shape=jax.ShapeDtypeStruct((t, D_MODEL), F32),
        compiler_params=_cparams(("arbitrary",)),
        name="combine_ln2",
    )(dest_flat, ys, w_tok, h1, shared, g, b)


TM_INB = 512
TN_INB = W_B // 2


def _pad_lanes(v):
    return jnp.pad(v, (0, LANES - v.shape[0]))[None, :].astype(F32)


def _layer(h2d, mem2d, batch, seq, w_in, conv_w, conv_b, dt_bias, a_log, d_skip, ssd_norm_g, swa_sinks, rel_bias,
           w_mem_kv, w_ssd_o, w_swa_o, w_xa_o, w_out, ln1_g, ln1_b, router_w, router_bias, w1, w3, w2, ws1, ws3,
           ws2, ln2_g, ln2_b, alpha):
    t = batch * seq
    wdt = jnp.pad(w_in[:, W_A:W_A + SSD_HEADS], ((0, 0), (0, LANES - SSD_HEADS))).astype(BF16)
    pa, dt_raw, xb = _inproj(h2d, _wcast(w_in, 0, W_A, "wcast_a"), wdt)
    pb = _mm(xb, _wcast(w_in, W_A + SSD_HEADS, W_B, "wcast_b"), BF16, TM_INB, TN_INB, "inproj_b", vmem_mb=56)

    y_ssd = _ssd(
        pa, dt_raw,
        conv_w[:, :SSD_D_INNER], conv_b[None, :SSD_D_INNER], conv_w[:, SSD_D_INNER:], conv_b[None, SSD_D_INNER:],
        _pad_lanes(dt_bias), _pad_lanes(a_log),
        jnp.repeat(d_skip, SSD_HEAD_DIM)[None, :], ssd_norm_g[None, :], batch, seq)

    y_swa = _swa(pb, swa_sinks.astype(F32), _swa_bias_table(rel_bias), batch, seq)

    memkv = _mm(mem2d, w_mem_kv.astype(BF16), BF16, batch * MEM_LEN, 1024, "mem_kv")
    y_xa = _xa(pb, memkv, batch, seq)

    merged = _merge(y_ssd, y_swa, y_xa, w_ssd_o.astype(BF16), w_swa_o.astype(BF16), w_xa_o.astype(BF16), pb)
    h1, hp = _out_ln(merged, w_out.astype(BF16), h2d, ln1_g[None, :], ln1_b[None, :], alpha)

    rwt = router_w.T
    r_hi = rwt.astype(BF16)
    r_lo = (rwt - r_hi.astype(F32)).astype(BF16)
    ti = jnp.arange(TM_RT)
    su = (ti[:, None] < ti[None, :]).astype(BF16)
    idx_t, w_t, rank_t, cnt = _router(h1, r_hi, r_lo, router_bias[:, None].astype(F32), su)

    counts = cnt[:, 0].astype(I32)
    padded = (counts + MOE_BLK - 1) // MOE_BLK * MOE_BLK
    pend = jnp.cumsum(padded)
    pstart = pend - padded
    n_blocks = t * TOP_K // MOE_BLK + N_EXPERTS
    blk_lo = jnp.arange(n_blocks, dtype=I32) * MOE_BLK
    block_e = jnp.minimum(jnp.sum((pend[None, :] <= blk_lo[:, None]).astype(I32), axis=1), N_EXPERTS - 1)
    of_e = block_e[:, None] == jnp.arange(N_EXPERTS, dtype=I32)[None, :]
    valid_end = jnp.sum(jnp.where(of_e, (pstart + counts)[None, :], 0), axis=1)
    block_nv = jnp.clip(valid_end - blk_lo, 0, MOE_BLK).astype(I32)
    eids = jnp.arange(N_EXPERTS, dtype=I32)
    later_active = (eids[None, :] > eids[:, None]) & (padded > 0)[None, :]
    next_active = jnp.min(jnp.where(later_active, eids[None, :], N_EXPERTS), axis=1)
    next_active = jnp.where(next_active >= N_EXPERTS, -1, next_active)
    seg_start = jnp.sum(jnp.where(of_e, pstart[None, :], 0), axis=1)
    block_first = ((blk_lo == seg_start) & (block_nv > 0)).astype(I32)
    block_next = jnp.sum(jnp.where(of_e, next_active[None, :], 0), axis=1).astype(I32)
    nn_of = next_active[:, None] == eids[None, :]
    next2_active = jnp.where(next_active >= 0, jnp.sum(jnp.where(nn_of, next_active[None, :], 0), axis=1), -1)
    block_next2 = jnp.sum(jnp.where(of_e, next2_active[None, :], 0), axis=1).astype(I32)

    dest_flat = _dest(idx_t, rank_t, pstart.astype(F32)[:, None]).reshape(-1)
    n_slots = n_blocks * MOE_BLK
    x_sorted, shared = _dispatch(dest_flat, hp, h1, ws1.astype(BF16), ws3.astype(BF16), ws2.astype(BF16),
                                 n_slots)
    ys = _experts(block_e, block_nv, block_first, block_next, block_next2,
                  x_sorted.reshape(n_slots * ROW_TILE, LANES), w1, w3, w2)
    return _final(dest_flat, ys.reshape(n_slots, ROW_TILE, LANES), w_t.T, h1, shared, ln2_g[None, :],
                  ln2_b[None, :], alpha)


def kernel(x, mem, w_in, conv_w, conv_b, dt_bias, a_log, d_skip, ssd_norm_g, swa_sinks, rel_bias, w_mem_kv, w_ssd_o, w_swa_o, w_xa_o, w_out, ln1_g, ln1_b, router_w, router_bias, w1, w3, w2, ws1, ws3, ws2, ln2_g, ln2_b):
    batch, seq, d = x.shape
    depth = w_in.shape[0]
    alpha = (2.0 * depth) ** 0.25
    h = x.reshape(batch * seq, d)
    mem2d = mem.reshape(batch * MEM_LEN, d)
    for i in range(depth):
        h = _layer(h, mem2d, batch, seq, w_in[i], conv_w[i], conv_b[i], dt_bias[i], a_log[i], d_skip[i],
                   ssd_norm_g[i], swa_sinks[i], rel_bias, w_mem_kv[i], w_ssd_o[i], w_swa_o[i], w_xa_o[i], w_out[i],
                   ln1_g[i], ln1_b[i], router_w[i], router_bias[i], w1[i], w3[i], w2[i], ws1[i], ws3[i], ws2[i],
                   ln2_g[i], ln2_b[i], alpha)
    return h.reshape(batch, seq, d)
```

```python
import functools
import math

import jax
import jax.numpy as jnp
from jax import lax
from jax.experimental import pallas as pl
from jax.experimental.pallas import tpu as pltpu

F32 = jnp.float32
BF16 = jnp.bfloat16
I32 = jnp.int32
U32 = jnp.uint32

D_MODEL = 2048
MEM_LEN = 256
SSD_D_INNER = D_MODEL
SSD_HEAD_DIM = 64
SSD_HEADS = 32
SSD_GROUPS = 4
SSD_STATE = 128
SSD_CONV = 4
SSD_CHUNK = 128
SWA_HEADS = 16
SWA_KV_HEADS = 4
SWA_HEAD_DIM = 64
SWA_WINDOW = 128
SWA_BLOCK = 128
REL_BUCKETS = 32
REL_MAX_DIST = 128
XA_HEADS = 4
XA_HEAD_DIM = 256
N_BRANCH = 3
N_EXPERTS = 64
TOP_K = 8
N_EXPERT_GROUPS = 8
TOPK_GROUPS = 4
EXPERT_DIM = 512
SHARED_DIM = 512
ROUTED_SCALE = 2.5
LN_EPS = 1e-5
RMS_EPS = 1e-5

SWA_Q_DIM = SWA_HEADS * SWA_HEAD_DIM
SWA_KV_DIM = SWA_KV_HEADS * SWA_HEAD_DIM
XA_DIM = XA_HEADS * XA_HEAD_DIM
BC_DIM = 2 * SSD_GROUPS * SSD_STATE

W_A = SSD_D_INNER + SSD_D_INNER + BC_DIM
OFFA_Z = 0
OFFA_XS = OFFA_Z + SSD_D_INNER
OFFA_BC = OFFA_XS + SSD_D_INNER
W_B = SWA_Q_DIM + 2 * SWA_KV_DIM + XA_DIM + N_BRANCH * D_MODEL
OFFB_Q = 0
OFFB_K = OFFB_Q + SWA_Q_DIM
OFFB_V = OFFB_K + SWA_KV_DIM
OFFB_QX = OFFB_V + SWA_KV_DIM
OFFB_GATES = OFFB_QX + XA_DIM

LANES = 128
MOE_BLK = 256
NEG_INF = float("-inf")


def _cparams(sem, vmem_mb=48):
    return pltpu.CompilerParams(dimension_semantics=sem, vmem_limit_bytes=vmem_mb * 1024 * 1024)


def _dot(a, b):
    return jnp.dot(a, b, preferred_element_type=F32)


def _dot_nt(a, b):
    return lax.dot_general(a, b, (((1,), (1,)), ((), ())), preferred_element_type=F32)


def _sigmoid(x):
    return 1.0 / (1.0 + jnp.exp(-x))


def _silu(x):
    return x * _sigmoid(x)


TM_IN = 1024
TN_IN = 1024


def _inproj_kernel(x_ref, w_ref, wdt_ref, p_ref, dt_ref, xb_ref):
    @pl.when(pl.program_id(1) == 0)
    def _():
        xb = x_ref[...].astype(BF16)
        xb_ref[...] = xb
        dt_ref[...] = _dot(xb, wdt_ref[...])

    p_ref[...] = _dot(xb_ref[...], w_ref[...].astype(BF16)).astype(BF16)


def _inproj(x2d, w_in, wdt):
    t = x2d.shape[0]
    return pl.pallas_call(
        _inproj_kernel,
        grid=(t // TM_IN, W_A // TN_IN),
        in_specs=[
            pl.BlockSpec((TM_IN, D_MODEL), lambda i, j: (i, 0)),
            pl.BlockSpec((D_MODEL, TN_IN), lambda i, j: (0, j)),
            pl.BlockSpec((D_MODEL, LANES), lambda i, j: (0, 0)),
        ],
        out_specs=[
            pl.BlockSpec((TM_IN, TN_IN), lambda i, j: (i, j)),
            pl.BlockSpec((TM_IN, LANES), lambda i, j: (i, 0)),
            pl.BlockSpec((TM_IN, D_MODEL), lambda i, j: (i, 0)),
        ],
        out_shape=[jax.ShapeDtypeStruct((t, W_A), BF16), jax.ShapeDtypeStruct((t, LANES), F32),
                   jax.ShapeDtypeStruct((t, D_MODEL), BF16)],
        compiler_params=_cparams(("parallel", "arbitrary"), vmem_mb=56),
        name="inproj",
    )(x2d, w_in, wdt)


TN_WC = 512
ROW_ALIGN = 32


def _wcast_kernel(a_ref, o_ref):
    o_ref[...] = a_ref[0].T.astype(BF16)


def _wcast(w_t, layer, start, width, name):
    k = w_t.shape[2]
    return pl.pallas_call(
        _wcast_kernel,
        grid=(width // TN_WC,),
        in_specs=[pl.BlockSpec((pl.Element(1), pl.Element(TN_WC), pl.Element(k)),
                               lambda j: (layer, pl.multiple_of(start + j * TN_WC, ROW_ALIGN), 0))],
        out_specs=pl.BlockSpec((k, TN_WC), lambda j: (0, j)),
        out_shape=jax.ShapeDtypeStruct((k, width), BF16),
        compiler_params=_cparams(("parallel",)),
        name=name,
    )(w_t)


def _mm_kernel(a_ref, b_ref, o_ref):
    o_ref[...] = _dot(a_ref[...].astype(BF16), b_ref[...].astype(BF16)).astype(o_ref.dtype)


def _mm(a, b, out_dtype, tm, tn, name, vmem_mb=48):
    m, k = a.shape
    n = b.shape[1]
    return pl.pallas_call(
        _mm_kernel,
        grid=(m // tm, n // tn),
        in_specs=[pl.BlockSpec((tm, k), lambda i, j: (i, 0)), pl.BlockSpec((k, tn), lambda i, j: (0, j))],
        out_specs=pl.BlockSpec((tm, tn), lambda i, j: (i, j)),
        out_shape=jax.ShapeDtypeStruct((m, n), out_dtype),
        compiler_params=_cparams(("parallel", "parallel"), vmem_mb=vmem_mb),
        name=name,
    )(a, b)


Q = SSD_CHUNK


def _conv_shift_matrix():
    t = jnp.arange(Q)[:, None]
    j = jnp.arange(2 * Q)[None, :]
    blocks = []
    for k in range(SSD_CONV - 1):
        src = t - (SSD_CONV - 1 - k)
        blocks.append(j == jnp.where(src >= 0, src, 2 * Q + src))
    return jnp.concatenate(blocks, axis=0).astype(BF16)


def _ssd_kernel(xs_ref, bc_ref, z_ref, dtr_ref, shift_ref, cwx_ref, cbx_ref, cwb_ref, cbb_ref, dtb_ref, alog_ref,
                dsk_ref, ng_ref, o_ref, hx_ref, hb_ref, st_ref, cumt_ref, dtt_ref, wt_ref, y_ref):
    c = pl.program_id(1)

    @pl.when(c == 0)
    def _():
        hx_ref[...] = jnp.zeros_like(hx_ref)
        hb_ref[...] = jnp.zeros_like(hb_ref)
        st_ref[...] = jnp.zeros_like(st_ref)

    def conv_silu(in_ref, prev_ref, w_ref, b_ref):
        cur = in_ref[...]
        ext = jnp.concatenate([cur, prev_ref[...]], axis=0)
        shifted = _dot(shift_ref[...], ext)
        acc = b_ref[...] + w_ref[SSD_CONV - 1:SSD_CONV, :] * cur.astype(F32)
        for k in range(SSD_CONV - 1):
            acc = acc + w_ref[k:k + 1, :] * shifted[k * Q:(k + 1) * Q, :]
        prev_ref[...] = cur
        return _silu(acc)

    xs_c = conv_silu(xs_ref, hx_ref, cwx_ref, cbx_ref)
    bc_c = conv_silu(bc_ref, hb_ref, cwb_ref, cbb_ref)

    lane = lax.broadcasted_iota(I32, (Q, LANES), 1)
    row = lax.broadcasted_iota(I32, (Q, LANES), 0)
    tril = row >= lane
    lo_mask = lane < SSD_HEAD_DIM

    dtv = dtr_ref[...] + dtb_ref[...]
    dt = jnp.maximum(dtv, 0.0) + jnp.log1p(jnp.exp(-jnp.abs(dtv)))
    a = jnp.where(lane[0:1, :] < SSD_HEADS, -jnp.exp(alog_ref[...]), 0.0)
    la = dt * a
    tri = jnp.where(tril, 1.0, 0.0).astype(BF16)
    p1 = la.astype(BF16)
    r1 = la - p1.astype(F32)
    p2 = r1.astype(BF16)
    p3 = (r1 - p2.astype(F32)).astype(BF16)
    cum = _dot(tri, p1) + _dot(tri, p2) + _dot(tri, p3)
    ecum = jnp.exp(cum)
    cum_t = cum.T
    dt_t = dt.T
    last = cum_t[:, Q - 1:Q]
    cumt_ref[...] = cum_t
    dtt_ref[...] = dt_t
    wt_ref[...] = jnp.exp(last - cum_t) * dt_t
    chunk_decay = jnp.exp(last)

    def head_parts(h, cb, cm, bm_t):
        col = jnp.broadcast_to(cum[:, h:h + 1], (Q, Q))
        rw = cumt_ref[h:h + 1, :]
        dec = jnp.exp(jnp.where(tril, col - rw, NEG_INF))
        m = cb * (dec * dtt_ref[h:h + 1, :])
        ce = cm * jnp.broadcast_to(ecum[:, h:h + 1], (Q, Q))
        sl = bm_t * wt_ref[h:h + 1, :]
        return m.astype(BF16), ce.astype(BF16), sl.astype(BF16)

    for g in range(SSD_GROUPS):
        bm = bc_c[:, g * SSD_STATE:(g + 1) * SSD_STATE]
        cm = bc_c[:, BC_DIM // 2 + g * SSD_STATE:BC_DIM // 2 + (g + 1) * SSD_STATE]
        cb = _dot_nt(cm.astype(BF16), bm.astype(BF16))
        bm_t = bm.T
        for qd in range(SSD_HEADS // SSD_GROUPS // 2):
            hp = g * (SSD_HEADS // SSD_GROUPS // 2) + qd
            ha, hb = 2 * hp, 2 * hp + 1
            sl_ = slice(hp * LANES, (hp + 1) * LANES)
            xs_pair = xs_c[:, sl_]
            x_lo = jnp.where(lo_mask, xs_pair, 0.0).astype(BF16)
            x_hi = jnp.where(lo_mask, 0.0, xs_pair).astype(BF16)
            h_prev = st_ref[:, sl_]
            h_lo = jnp.where(lo_mask, h_prev, 0.0).astype(BF16)
            h_hi = jnp.where(lo_mask, 0.0, h_prev).astype(BF16)
            m_a, ce_a, sl_a = head_parts(ha, cb, cm, bm_t)
            m_b, ce_b, sl_b = head_parts(hb, cb, cm, bm_t)
            lhs = jnp.concatenate([m_a, ce_a, m_b, ce_b], axis=1)
            rhs = jnp.concatenate([x_lo, h_lo, x_hi, h_hi], axis=0)
            y_ref[:, sl_] = _dot(lhs, rhs)
            new = _dot(jnp.concatenate([sl_a, sl_b], axis=1), jnp.concatenate([x_lo, x_hi], axis=0))
            cd = jnp.where(lo_mask[0:1, :], chunk_decay[ha:ha + 1, :], chunk_decay[hb:hb + 1, :])
            st_ref[:, sl_] = h_prev * cd + new

    y = y_ref[...] + xs_c * dsk_ref[...]
    y = y * _silu(z_ref[...].astype(F32))
    gw = SSD_D_INNER // SSD_GROUPS
    for g in range(SSD_GROUPS):
        yg = y[:, g * gw:(g + 1) * gw]
        ms = jnp.mean(yg * yg, axis=-1, keepdims=True)
        o_ref[:, g * gw:(g + 1) * gw] = ((yg * lax.rsqrt(ms + RMS_EPS)) * ng_ref[:, g * gw:(g + 1) * gw]).astype(BF16)


def _ssd(p, dt_raw, cwx, cbx, cwb, cbb, dtb, alog, dsk, ng, batch, seq):
    nc = seq // Q
    t = batch * seq
    rowblk = lambda b, c: b * nc + c
    const = lambda b, c: (0, 0)
    return pl.pallas_call(
        _ssd_kernel,
        grid=(batch, nc),
        in_specs=[
            pl.BlockSpec((Q, SSD_D_INNER), lambda b, c: (rowblk(b, c), OFFA_XS // SSD_D_INNER)),
            pl.BlockSpec((Q, BC_DIM), lambda b, c: (rowblk(b, c), OFFA_BC // BC_DIM)),
            pl.BlockSpec((Q, SSD_D_INNER), lambda b, c: (rowblk(b, c), OFFA_Z // SSD_D_INNER)),
            pl.BlockSpec((Q, LANES), lambda b, c: (rowblk(b, c), 0)),
            pl.BlockSpec(((SSD_CONV - 1) * Q, 2 * Q), const),
            pl.BlockSpec((SSD_CONV, SSD_D_INNER), const),
            pl.BlockSpec((1, SSD_D_INNER), const),
            pl.BlockSpec((SSD_CONV, BC_DIM), const),
            pl.BlockSpec((1, BC_DIM), const),
            pl.BlockSpec((1, LANES), const),
            pl.BlockSpec((1, LANES), const),
            pl.BlockSpec((1, SSD_D_INNER), const),
            pl.BlockSpec((1, SSD_D_INNER), const),
        ],
        out_specs=pl.BlockSpec((Q, SSD_D_INNER), lambda b, c: (rowblk(b, c), 0)),
        out_shape=jax.ShapeDtypeStruct((t, SSD_D_INNER), BF16),
        scratch_shapes=[
            pltpu.VMEM((Q, SSD_D_INNER), BF16),
            pltpu.VMEM((Q, BC_DIM), BF16),
            pltpu.VMEM((SSD_STATE, SSD_D_INNER), F32),
            pltpu.VMEM((LANES, Q), F32),
            pltpu.VMEM((LANES, Q), F32),
            pltpu.VMEM((LANES, Q), F32),
            pltpu.VMEM((Q, SSD_D_INNER), F32),
        ],
        compiler_params=_cparams(("parallel", "arbitrary")),
        name="ssd",
    )(p, p, p, dt_raw, _conv_shift_matrix(), cwx, cbx, cwb, cbb, dtb, alog, dsk, ng)


N_PAIRS = SWA_HEADS // 2


def _swa_kernel(sink_ref, q_ref, kp_ref, kc_ref, vp_ref, vc_ref, bias_ref, o_ref):
    n = pl.program_id(1)
    blk = SWA_BLOCK
    kband = jnp.concatenate([kp_ref[...], kc_ref[...]], axis=0)
    vband = jnp.concatenate([vp_ref[...], vc_ref[...]], axis=0)

    lane = lax.broadcasted_iota(I32, (1, LANES), 1)
    lo_f = jnp.where(lane < SWA_HEAD_DIM, 1.0, 0.0)
    lo_m = lo_f.astype(BF16)
    hi_m = (1.0 - lo_f).astype(BF16)

    def head_twice(band, kv):
        tile = band[:, (kv // 2) * LANES:(kv // 2 + 1) * LANES]
        swapped = jnp.concatenate([tile[:, SWA_HEAD_DIM:], tile[:, :SWA_HEAD_DIM]], axis=1)
        return tile * lo_m + swapped * hi_m if kv % 2 == 0 else swapped * lo_m + tile * hi_m

    k_twice = [head_twice(kband, kv) for kv in range(SWA_KV_HEADS)]
    v_twice = [head_twice(vband, kv) for kv in range(SWA_KV_HEADS)]
    colj = lax.broadcasted_iota(I32, (2 * blk, 2 * blk), 1)
    rowi = lax.broadcasted_iota(I32, (2 * blk, 1), 0)
    valid = (colj >= blk) | (n > 0)
    scale = SWA_HEAD_DIM ** -0.5
    for p in range(N_PAIRS):
        kv = p // (SWA_HEADS // SWA_KV_HEADS // 2)
        qp = q_ref[:, p * LANES:(p + 1) * LANES]
        qs = jnp.concatenate([qp * lo_m, qp * hi_m], axis=0)
        s = _dot_nt(qs, k_twice[kv]) * scale + bias_ref[p]
        s = jnp.where(valid, s, NEG_INF)
        sink = jnp.where(rowi < blk, sink_ref[2 * p], sink_ref[2 * p + 1])
        m = jnp.maximum(jnp.max(s, axis=-1, keepdims=True), sink)
        e = jnp.exp(s - m)
        den = jnp.sum(e, axis=-1, keepdims=True) + jnp.exp(sink - m)
        pr = (e * (1.0 / den)).astype(BF16)
        vd = v_twice[kv]
        lhs = jnp.concatenate([pr[:blk], pr[blk:]], axis=1)
        rhs = jnp.concatenate([vd * lo_m, vd * hi_m], axis=0)
        o_ref[:, p * LANES:(p + 1) * LANES] = _dot(lhs, rhs).astype(BF16)


def _swa(p, sinks, bias, batch, seq):
    nb = seq // SWA_BLOCK
    t = batch * seq
    kw = SWA_KV_DIM
    cur = lambda b, n, s: b * nb + n
    prev = lambda b, n, s: b * nb + jnp.maximum(n - 1, 0)
    return pl.pallas_call(
        _swa_kernel,
        grid_spec=pltpu.PrefetchScalarGridSpec(
            num_scalar_prefetch=1,
            grid=(batch, nb),
            in_specs=[
                pl.BlockSpec((SWA_BLOCK, SWA_Q_DIM), lambda b, n, s: (cur(b, n, s), OFFB_Q // SWA_Q_DIM)),
                pl.BlockSpec((SWA_BLOCK, kw), lambda b, n, s: (prev(b, n, s), OFFB_K // kw)),
                pl.BlockSpec((SWA_BLOCK, kw), lambda b, n, s: (cur(b, n, s), OFFB_K // kw)),
                pl.BlockSpec((SWA_BLOCK, kw), lambda b, n, s: (prev(b, n, s), OFFB_V // kw)),
                pl.BlockSpec((SWA_BLOCK, kw), lambda b, n, s: (cur(b, n, s), OFFB_V // kw)),
                pl.BlockSpec((N_PAIRS, 2 * SWA_BLOCK, 2 * SWA_BLOCK), lambda b, n, s: (0, 0, 0)),
            ],
            out_specs=pl.BlockSpec((SWA_BLOCK, SWA_Q_DIM), lambda b, n, s: (cur(b, n, s), 0)),
        ),
        out_shape=jax.ShapeDtypeStruct((t, SWA_Q_DIM), BF16),
        compiler_params=_cparams(("parallel", "arbitrary")),
        name="swa",
    )(sinks, p, p, p, p, p, bias)


def _t5_causal_bucket(dist):
    max_exact = REL_BUCKETS // 2
    large = max_exact + (jnp.log(jnp.maximum(dist, 1).astype(F32) / max_exact)
                         / math.log(REL_MAX_DIST / max_exact) * (REL_BUCKETS - max_exact)).astype(I32)
    large = jnp.minimum(large, REL_BUCKETS - 1)
    return jnp.where(dist < max_exact, dist, large)


def _swa_bias_table(rel_bias):
    qi = jnp.arange(SWA_BLOCK)[:, None]
    kj = jnp.arange(2 * SWA_BLOCK)[None, :]
    dist = qi + SWA_BLOCK - kj
    in_window = (dist >= 0) & (dist < SWA_WINDOW)
    bucket = _t5_causal_bucket(jnp.maximum(dist, 0))
    onehot = (bucket[:, :, None] == jnp.arange(REL_BUCKETS)[None, None, :]).astype(F32)
    bias = jnp.einsum('ijb,bh->hij', onehot, rel_bias.astype(F32), precision=lax.Precision.HIGHEST)
    bias = jnp.where(in_window[None, :, :], bias, NEG_INF)
    return bias.reshape(N_PAIRS, 2 * SWA_BLOCK, 2 * SWA_BLOCK)


TL_XA = 512


XA_HALF = XA_DIM // 2


def _xa_kernel(qa_ref, qb_ref, mk_ref, mv_ref, o_ref):
    for h in range(XA_HEADS):
        q_ref = qa_ref if h < XA_HEADS // 2 else qb_ref
        qc = (h % (XA_HEADS // 2)) * XA_HEAD_DIM
        cols = slice(h * XA_HEAD_DIM, (h + 1) * XA_HEAD_DIM)
        s = _dot_nt(q_ref[:, qc:qc + XA_HEAD_DIM], mk_ref[:, cols]) * (XA_HEAD_DIM ** -0.5)
        m = jnp.max(s, axis=-1, keepdims=True)
        e = jnp.exp(s - m)
        pr = e * (1.0 / jnp.sum(e, axis=-1, keepdims=True))
        o_ref[:, cols] = _dot(pr.astype(BF16), mv_ref[:, cols]).astype(BF16)


def _xa(p, memkv, batch, seq):
    nl = seq // TL_XA
    t = batch * seq
    return pl.pallas_call(
        _xa_kernel,
        grid=(batch, nl),
        in_specs=[
            pl.BlockSpec((TL_XA, XA_HALF), lambda b, l: (b * nl + l, OFFB_QX // XA_HALF)),
            pl.BlockSpec((TL_XA, XA_HALF), lambda b, l: (b * nl + l, OFFB_QX // XA_HALF + 1)),
            pl.BlockSpec((MEM_LEN, XA_DIM), lambda b, l: (b, 0)),
            pl.BlockSpec((MEM_LEN, XA_DIM), lambda b, l: (b, 1)),
        ],
        out_specs=pl.BlockSpec((TL_XA, XA_DIM), lambda b, l: (b * nl + l, 0)),
        out_shape=jax.ShapeDtypeStruct((t, XA_DIM), BF16),
        compiler_params=_cparams(("parallel", "parallel")),
        name="xattn",
    )(p, p, memkv, memkv)


TM_MG = 1024
TN_MG = 512


N_SUB_MG = 2


def _merge_kernel(ys_ref, yw_ref, yx_ref, ws_ref, ww_ref, wx_ref, g0_ref, g1_ref, g2_ref, o_ref):
    sub = TM_MG // N_SUB_MG
    ws = ws_ref[...].astype(BF16)
    ww = ww_ref[...].astype(BF16)
    wx = wx_ref[...].astype(BF16)
    for s in range(N_SUB_MG):
        rows = slice(s * sub, (s + 1) * sub)
        a = _dot(ys_ref[rows, :], ws)
        b = _dot(yw_ref[rows, :], ww)
        c = _dot(yx_ref[rows, :], wx)
        g0 = _sigmoid(g0_ref[rows, :].astype(F32))
        g1 = _sigmoid(g1_ref[rows, :].astype(F32))
        g2 = _sigmoid(g2_ref[rows, :].astype(F32))
        o_ref[rows, :] = (g0 * a + g1 * b + g2 * c).astype(BF16)


def _merge(y_ssd, y_swa, y_xa, w_ssd_o, w_swa_o, w_xa_o, p):
    t = y_ssd.shape[0]
    gpb = D_MODEL // TN_MG
    gate = lambda k: pl.BlockSpec((TM_MG, TN_MG), lambda i, j: (i, OFFB_GATES // TN_MG + k * gpb + j))
    return pl.pallas_call(
        _merge_kernel,
        grid=(t // TM_MG, D_MODEL // TN_MG),
        in_specs=[
            pl.BlockSpec((TM_MG, SSD_D_INNER), lambda i, j: (i, 0)),
            pl.BlockSpec((TM_MG, SWA_Q_DIM), lambda i, j: (i, 0)),
            pl.BlockSpec((TM_MG, XA_DIM), lambda i, j: (i, 0)),
            pl.BlockSpec((SSD_D_INNER, TN_MG), lambda i, j: (0, j)),
            pl.BlockSpec((SWA_Q_DIM, TN_MG), lambda i, j: (0, j)),
            pl.BlockSpec((XA_DIM, TN_MG), lambda i, j: (0, j)),
            gate(0), gate(1), gate(2),
        ],
        out_specs=pl.BlockSpec((TM_MG, TN_MG), lambda i, j: (i, j)),
        out_shape=jax.ShapeDtypeStruct((t, D_MODEL), BF16),
        compiler_params=_cparams(("parallel", "parallel")),
        name="merge",
    )(y_ssd, y_swa, y_xa, w_ssd_o, w_swa_o, w_xa_o, p, p, p)


TM_LN = 512


def _layer_norm(v, g, b):
    mu = jnp.mean(v, axis=-1, keepdims=True)
    d = v - mu
    var = jnp.mean(d * d, axis=-1, keepdims=True)
    return (d * lax.rsqrt(var + LN_EPS)) * g + b


def _pack_halves(v):
    n = v.shape[1] // 2
    lo = lax.bitcast_convert_type(v[:, :n].astype(BF16).astype(F32), U32) >> 16
    hi = lax.bitcast_convert_type(v[:, n:].astype(BF16).astype(F32), U32) & jnp.uint32(0xFFFF0000)
    return hi | lo


def _unpack_halves(w):
    lo = lax.bitcast_convert_type(w << 16, F32)
    hi = lax.bitcast_convert_type(w & jnp.uint32(0xFFFF0000), F32)
    return lo, hi


ROW_WORDS = D_MODEL // 2
ROW_TILE = ROW_WORDS // LANES


def _store_row_tiled(ref, words):
    m = words.shape[0]
    for c in range(ROW_TILE):
        ref[pl.ds(c, m, stride=ROW_TILE), :] = words[:, c * LANES:(c + 1) * LANES]


def _load_row_tiled(ref, m):
    return jnp.concatenate([ref[pl.ds(c, m, stride=ROW_TILE), :] for c in range(ROW_TILE)], axis=1)


N_SUB_LN = 4


def _out_ln_kernel(alpha, m_ref, w_ref, x_ref, g_ref, b_ref, h_ref, hp_ref):
    sub = TM_LN // N_SUB_LN
    for c in range(N_SUB_LN):
        rows = slice(c * sub, (c + 1) * sub)
        mix = _dot(m_ref[rows, :], w_ref[...])
        hn = _layer_norm(alpha * x_ref[rows, :] + mix, g_ref[...], b_ref[...])
        h_ref[rows, :] = hn
        _store_row_tiled(hp_ref.at[pl.ds(c * sub * ROW_TILE, sub * ROW_TILE), :], _pack_halves(hn))


def _out_ln(merged, w_out, x2d, g, b, alpha):
    t = x2d.shape[0]
    row = lambda i: (i, 0)
    const = lambda i: (0, 0)
    return pl.pallas_call(
        functools.partial(_out_ln_kernel, alpha),
        grid=(t // TM_LN,),
        in_specs=[
            pl.BlockSpec((TM_LN, D_MODEL), row),
            pl.BlockSpec((D_MODEL, D_MODEL), const),
            pl.BlockSpec((TM_LN, D_MODEL), row),
            pl.BlockSpec((1, D_MODEL), const),
            pl.BlockSpec((1, D_MODEL), const),
        ],
        out_specs=[pl.BlockSpec((TM_LN, D_MODEL), row), pl.BlockSpec((TM_LN * ROW_TILE, LANES), row)],
        out_shape=[jax.ShapeDtypeStruct((t, D_MODEL), F32), jax.ShapeDtypeStruct((t * ROW_TILE, LANES), U32)],
        compiler_params=_cparams(("parallel",)),
        name="out_ln1",
    )(merged, w_out, x2d, g, b)


TM_RT = 512
EPG = N_EXPERTS // N_EXPERT_GROUPS


def _router_kernel(h_ref, whi_ref, wlo_ref, bias_ref, su_ref, idx_ref, w_ref, rank_ref, cnt_ref, run_ref):
    @pl.when(pl.program_id(0) == 0)
    def _():
        run_ref[...] = jnp.zeros_like(run_ref)

    h = h_ref[...]
    h_hi = h.astype(BF16)
    h_lo = (h - h_hi.astype(F32)).astype(BF16)
    whi = whi_ref[...]
    logits = _dot_nt(whi, h_hi) + _dot_nt(whi, h_lo) + _dot_nt(wlo_ref[...], h_hi)
    scores = _sigmoid(logits)
    sel = scores + bias_ref[...]

    iota_m = lax.broadcasted_iota(I32, (EPG, TM_RT), 0).astype(F32)
    iota_g = lax.broadcasted_iota(I32, (N_EXPERT_GROUPS, TM_RT), 0).astype(F32)
    gs = []
    for g in range(N_EXPERT_GROUPS):
        v = sel[g * EPG:(g + 1) * EPG, :]
        m1 = jnp.max(v, axis=0, keepdims=True)
        i1 = jnp.min(jnp.where(v == m1, iota_m, float(EPG)), axis=0, keepdims=True)
        m2 = jnp.max(jnp.where(iota_m == i1, NEG_INF, v), axis=0, keepdims=True)
        gs.append(m1 + m2)
    gsc = jnp.concatenate(gs, axis=0)
    chosen = jnp.zeros((N_EXPERT_GROUPS, TM_RT), F32)
    for _ in range(TOPK_GROUPS):
        m = jnp.max(gsc, axis=0, keepdims=True)
        ig = jnp.min(jnp.where(gsc == m, iota_g, float(N_EXPERT_GROUPS)), axis=0, keepdims=True)
        hit = iota_g == ig
        chosen = jnp.where(hit, 1.0, chosen)
        gsc = jnp.where(hit, NEG_INF, gsc)
    gmask = jnp.concatenate(
        [jnp.broadcast_to(chosen[g:g + 1, :], (EPG, TM_RT)) for g in range(N_EXPERT_GROUPS)], axis=0)
    masked = jnp.where(gmask > 0.0, sel, NEG_INF)

    iota_e = lax.broadcasted_iota(I32, (N_EXPERTS, TM_RT), 0).astype(F32)
    hits, idxs, ws = [], [], []
    for _ in range(TOP_K):
        m = jnp.max(masked, axis=0, keepdims=True)
        ie = jnp.min(jnp.where(masked == m, iota_e, float(N_EXPERTS)), axis=0, keepdims=True)
        hit = iota_e == ie
        hits.append(hit)
        idxs.append(ie)
        ws.append(jnp.sum(jnp.where(hit, scores, 0.0), axis=0, keepdims=True))
        masked = jnp.where(hit, NEG_INF, masked)
    wsum = ws[0]
    for k in range(1, TOP_K):
        wsum = wsum + ws[k]
    inv = 1.0 / wsum
    idx_ref[...] = jnp.concatenate(idxs, axis=0).astype(I32)
    w_ref[...] = jnp.concatenate([w * inv * ROUTED_SCALE for w in ws], axis=0)

    assign = jnp.zeros((N_EXPERTS, TM_RT), F32)
    for hit in hits:
        assign = jnp.where(hit, 1.0, assign)
    pos = _dot(assign.astype(BF16), su_ref[...]) + run_ref[...]
    rank_ref[...] = jnp.concatenate(
        [jnp.sum(jnp.where(hit, pos, 0.0), axis=0, keepdims=True) for hit in hits], axis=0).astype(I32)
    run = run_ref[...] + jnp.sum(assign, axis=1, keepdims=True)
    run_ref[...] = run
    cnt_ref[...] = run


def _router(h1, r_hi, r_lo, rbias, su):
    t = h1.shape[0]
    tile = lambda i: (0, i)
    const = lambda i: (0, 0)
    return pl.pallas_call(
        _router_kernel,
        grid=(t // TM_RT,),
        in_specs=[
            pl.BlockSpec((TM_RT, D_MODEL), lambda i: (i, 0)),
            pl.BlockSpec((N_EXPERTS, D_MODEL), const),
            pl.BlockSpec((N_EXPERTS, D_MODEL), const),
            pl.BlockSpec((N_EXPERTS, 1), const),
            pl.BlockSpec((TM_RT, TM_RT), const),
        ],
        out_specs=[
            pl.BlockSpec((TOP_K, TM_RT), tile),
            pl.BlockSpec((TOP_K, TM_RT), tile),
            pl.BlockSpec((TOP_K, TM_RT), tile),
            pl.BlockSpec((N_EXPERTS, TM_RT), const),
        ],
        out_shape=[
            jax.ShapeDtypeStruct((TOP_K, t), I32),
            jax.ShapeDtypeStruct((TOP_K, t), F32),
            jax.ShapeDtypeStruct((TOP_K, t), I32),
            jax.ShapeDtypeStruct((N_EXPERTS, TM_RT), F32),
        ],
        scratch_shapes=[pltpu.VMEM((N_EXPERTS, TM_RT), F32)],
        compiler_params=_cparams(("arbitrary",)),
        name="router",
    )(h1, r_hi, r_lo, rbias, su)


TM_DS = 2048


def _dest_kernel(idx_ref, rank_ref, ps_ref, d_ref):
    iota_e = lax.broadcasted_iota(I32, (N_EXPERTS, TM_DS), 0)
    ps = ps_ref[...]
    rows = []
    for k in range(TOP_K):
        hit = iota_e == idx_ref[k:k + 1, :]
        rows.append(jnp.sum(jnp.where(hit, ps, 0.0), axis=0, keepdims=True))
    d_ref[...] = jnp.concatenate(rows, axis=0).astype(I32) + rank_ref[...]


def _dest(idx_t, rank_t, pstart):
    t = idx_t.shape[1]
    tile = lambda i: (0, i)
    return pl.pallas_call(
        _dest_kernel,
        grid=(t // TM_DS,),
        in_specs=[pl.BlockSpec((TOP_K, TM_DS), tile), pl.BlockSpec((TOP_K, TM_DS), tile),
                  pl.BlockSpec((N_EXPERTS, 1), lambda i: (0, 0))],
        out_specs=pl.BlockSpec((TOP_K, TM_DS), tile),
        out_shape=jax.ShapeDtypeStruct((TOP_K, t), I32),
        compiler_params=_cparams(("parallel",)),
        name="dest",
    )(idx_t, rank_t, pstart)


TM_DP = 512


def _dispatch_kernel(dest_ref, hp_ref, h_ref, ws1_ref, ws3_ref, ws2_ref, xs_ref, sh_ref, sem):
    t_total = pl.num_programs(0) * TM_DP
    base = pl.program_id(0) * TM_DP

    def issue(t, carry):
        src = hp_ref.at[pl.ds(pl.multiple_of(t * ROW_TILE, ROW_TILE), ROW_TILE), :]
        for k in range(TOP_K):
            slot = dest_ref[k * t_total + base + t]
            pltpu.make_async_copy(src, xs_ref.at[slot], sem).start(priority=k % 2)
        return carry

    lax.fori_loop(0, TM_DP, issue, 0)

    xb = h_ref[...].astype(BF16)
    hdn = _silu(_dot(xb, ws1_ref[...])) * _dot(xb, ws3_ref[...])
    sh_ref[...] = _dot(hdn.astype(BF16), ws2_ref[...])

    for k in range(TOP_K):
        pltpu.make_async_copy(hp_ref.reshape(TM_DP, ROW_TILE, LANES), xs_ref.at[pl.ds(0, TM_DP)], sem).wait()


def _dispatch(dest_flat, hp, h1, ws1, ws3, ws2, n_slots):
    t = h1.shape[0]
    row = lambda i, d: (i, 0)
    const = lambda i, d: (0, 0)
    return pl.pallas_call(
        _dispatch_kernel,
        grid_spec=pltpu.PrefetchScalarGridSpec(
            num_scalar_prefetch=1,
            grid=(t // TM_DP,),
            in_specs=[
                pl.BlockSpec((TM_DP * ROW_TILE, LANES), row),
                pl.BlockSpec((TM_DP, D_MODEL), row),
                pl.BlockSpec((D_MODEL, SHARED_DIM), const),
                pl.BlockSpec((D_MODEL, SHARED_DIM), const),
                pl.BlockSpec((SHARED_DIM, D_MODEL), const),
            ],
            out_specs=[pl.BlockSpec(memory_space=pl.ANY), pl.BlockSpec((TM_DP, D_MODEL), row)],
            scratch_shapes=[pltpu.SemaphoreType.DMA],
        ),
        out_shape=[jax.ShapeDtypeStruct((n_slots, ROW_TILE, LANES), U32), jax.ShapeDtypeStruct((t, D_MODEL), F32)],
        compiler_params=_cparams(("arbitrary",)),
        name="dispatch",
    )(dest_flat, hp, h1, ws1, ws3, ws2)


N_WSLOTS = 3


def _expert_kernel(be_ref, nv_ref, first_ref, nxt_ref, nxt2_ref, x_ref, w1_ref, w3_ref, w2_ref, o_ref,
                   wf1, wf3, wf2, w1b, w3b, w2b, slot_ref, sems):
    b = pl.program_id(0)
    nv = nv_ref[b]

    def weight_copies(e, s):
        return (pltpu.make_async_copy(w1_ref.at[e], wf1.at[s], sems.at[s, 0]),
                pltpu.make_async_copy(w3_ref.at[e], wf3.at[s], sems.at[s, 1]),
                pltpu.make_async_copy(w2_ref.at[e], wf2.at[s], sems.at[s, 2]))

    @pl.when(b == 0)
    def _():
        slot_ref[0] = 0
        for cp in weight_copies(be_ref[0], 0):
            cp.start()

        @pl.when(nxt_ref[0] >= 0)
        def _():
            for cp in weight_copies(nxt_ref[0], 1):
                cp.start(priority=1)

    @pl.when(first_ref[b] == 1)
    def _():
        s = slot_ref[0]
        for cp in weight_copies(be_ref[b], s):
            cp.wait()

        @pl.when(nxt2_ref[b] >= 0)
        def _():
            s2 = jnp.where(s == 0, N_WSLOTS - 1, s - 1)
            for cp in weight_copies(nxt2_ref[b], s2):
                cp.start(priority=1)

        w1b[...] = wf1[s].astype(BF16)
        w3b[...] = wf3[s].astype(BF16)
        w2b[...] = wf2[s].astype(BF16)
        slot_ref[0] = jnp.where(s == N_WSLOTS - 1, 0, s + 1)

    @pl.when(nv > 0)
    def _():
        live = lax.broadcasted_iota(I32, (MOE_BLK, 1), 0) < nv
        lo, hi = _unpack_halves(jnp.where(live, _load_row_tiled(x_ref, MOE_BLK), jnp.uint32(0)))
        x = jnp.concatenate([lo.astype(BF16), hi.astype(BF16)], axis=1)
        hdn = _silu(_dot(x, w1b[...])) * _dot(x, w3b[...])
        _store_row_tiled(o_ref, _pack_halves(_dot(hdn.astype(BF16), w2b[...])))

    @pl.when(nv <= 0)
    def _():
        o_ref[...] = jnp.zeros_like(o_ref)


def _experts(block_e, block_nv, block_first, block_next, block_next2, x_sorted, w1, w3, w2):
    n_rows = x_sorted.shape[0]
    n_blocks = n_rows // (MOE_BLK * ROW_TILE)
    row = lambda b, *_: (b, 0)
    return pl.pallas_call(
        _expert_kernel,
        grid_spec=pltpu.PrefetchScalarGridSpec(
            num_scalar_prefetch=5,
            grid=(n_blocks,),
            in_specs=[
                pl.BlockSpec((MOE_BLK * ROW_TILE, LANES), row),
                pl.BlockSpec(memory_space=pl.ANY),
                pl.BlockSpec(memory_space=pl.ANY),
                pl.BlockSpec(memory_space=pl.ANY),
            ],
            out_specs=pl.BlockSpec((MOE_BLK * ROW_TILE, LANES), row),
            scratch_shapes=[
                pltpu.VMEM((N_WSLOTS, D_MODEL, EXPERT_DIM), F32),
                pltpu.VMEM((N_WSLOTS, D_MODEL, EXPERT_DIM), F32),
                pltpu.VMEM((N_WSLOTS, EXPERT_DIM, D_MODEL), F32),
                pltpu.VMEM((D_MODEL, EXPERT_DIM), BF16),
                pltpu.VMEM((D_MODEL, EXPERT_DIM), BF16),
                pltpu.VMEM((EXPERT_DIM, D_MODEL), BF16),
                pltpu.SMEM((1,), I32),
                pltpu.SemaphoreType.DMA((N_WSLOTS, 3)),
            ],
        ),
        out_shape=jax.ShapeDtypeStruct((n_rows, LANES), U32),
        compiler_params=_cparams(("arbitrary",), vmem_mb=58),
        name="experts",
    )(block_e, block_nv, block_first, block_next, block_next2, x_sorted, w1, w3, w2)


TM_FN = 256


CH_FN = 8


def _final_kernel(alpha, dest_ref, ys_ref, w_ref, h_ref, sh_ref, g_ref, b_ref, o_ref, buf, sems):
    i = pl.program_id(0)
    n_tiles = pl.num_programs(0)
    t_total = n_tiles * TM_FN
    slot = i % 2
    nslot = 1 - slot
    half = D_MODEL // 2

    def issue_token(tile, sl, c, tt):
        tok = c * CH_FN + tt
        row0 = pl.multiple_of(tok * ROW_TILE, ROW_TILE)
        for k in range(TOP_K):
            src = ys_ref.at[dest_ref[k * t_total + tile * TM_FN + tok]]
            pltpu.make_async_copy(src, buf.at[sl, k, pl.ds(row0, ROW_TILE), :], sems.at[sl]).start(priority=k % 2)

    def wait_tile(sl):
        for k in range(TOP_K):
            pltpu.make_async_copy(ys_ref.at[pl.ds(0, TM_FN)], buf.at[sl, k].reshape(TM_FN, ROW_TILE, LANES),
                                  sems.at[sl]).wait()

    @pl.when(i == 0)
    def _():
        def first(c, carry):
            for tt in range(CH_FN):
                issue_token(0, 0, c, tt)
            return carry
        lax.fori_loop(0, TM_FN // CH_FN, first, 0)

    wait_tile(slot)

    nxt_tile = jnp.minimum(i + 1, n_tiles - 1)

    def body(c, carry):
        tok = pl.ds(pl.multiple_of(c * CH_FN, CH_FN), CH_FN)
        acc_lo = sh_ref[tok, :half]
        acc_hi = sh_ref[tok, half:]
        pre = alpha * h_ref[tok, :]
        wrow = w_ref[tok, :]
        for k in range(TOP_K):
            words = jnp.concatenate(
                [buf[slot, k, pl.ds(c * (CH_FN * ROW_TILE) + r, CH_FN, stride=ROW_TILE), :] for r in range(ROW_TILE)],
                axis=1)
            issue_token(nxt_tile, nslot, c, k)
            lo, hi = _unpack_halves(words)
            wk = wrow[:, k:k + 1]
            acc_lo = acc_lo + wk * lo
            acc_hi = acc_hi + wk * hi
        o_ref[tok, :] = pre + jnp.concatenate([acc_lo, acc_hi], axis=1)
        return carry

    lax.fori_loop(0, TM_FN // CH_FN, body, 0)
    o_ref[...] = _layer_norm(o_ref[...], g_ref[...], b_ref[...])

    @pl.when(i == n_tiles - 1)
    def _():
        wait_tile(nslot)


def _final(dest_flat, ys, w_tok, h1, shared, g, b, alpha):
    t = h1.shape[0]
    row = lambda i, d: (i, 0)
    const = lambda i, d: (0, 0)
    return pl.pallas_call(
        functools.partial(_final_kernel, alpha),
        grid_spec=pltpu.PrefetchScalarGridSpec(
            num_scalar_prefetch=1,
            grid=(t // TM_FN,),
            in_specs=[
                pl.BlockSpec(memory_space=pl.ANY),
                pl.BlockSpec((TM_FN, TOP_K), row),
                pl.BlockSpec((TM_FN, D_MODEL), row),
                pl.BlockSpec((TM_FN, D_MODEL), row),
                pl.BlockSpec((1, D_MODEL), const),
                pl.BlockSpec((1, D_MODEL), const),
            ],
            out_specs=pl.BlockSpec((TM_FN, D_MODEL), row),
            scratch_shapes=[pltpu.VMEM((2, TOP_K, TM_FN * ROW_TILE, LANES), U32), pltpu.SemaphoreType.DMA((2,))],
        ),
        out_shape=jax.ShapeDtypeStruct((t, D_MODEL), F32),
        compiler_params=_cparams(("arbitrary",)),
        name="combine_ln2",
    )(dest_flat, ys, w_tok, h1, shared, g, b)


TM_INB = 512
TN_INB = W_B // 2


def _pad_lanes(v):
    return jnp.pad(v, (0, LANES - v.shape[0]))[None, :].astype(F32)


def _layer(h2d, mem2d, batch, seq, w_in_all, layer, conv_w, conv_b, dt_bias, a_log, d_skip, ssd_norm_g, swa_sinks,
           rel_bias, w_mem_kv, w_ssd_o, w_swa_o, w_xa_o, w_out, ln1_g, ln1_b, router_w, router_bias, w1, w3, w2, ws1,
           ws3, ws2, ln2_g, ln2_b, alpha):
    t = batch * seq
    w_t = jnp.swapaxes(w_in_all, 1, 2)
    wdt = jnp.pad(w_t[layer, W_A:W_A + SSD_HEADS, :].T, ((0, 0), (0, LANES - SSD_HEADS))).astype(BF16)
    pa, dt_raw, xb = _inproj(h2d, _wcast(w_t, layer, 0, W_A, "wcast_a"), wdt)
    pb = _mm(xb, _wcast(w_t, layer, W_A + SSD_HEADS, W_B, "wcast_b"), BF16, TM_INB, TN_INB, "inproj_b", vmem_mb=56)

    y_ssd = _ssd(
        pa, dt_raw,
        conv_w[:, :SSD_D_INNER], conv_b[None, :SSD_D_INNER], conv_w[:, SSD_D_INNER:], conv_b[None, SSD_D_INNER:],
        _pad_lanes(dt_bias), _pad_lanes(a_log),
        jnp.repeat(d_skip, SSD_HEAD_DIM)[None, :], ssd_norm_g[None, :], batch, seq)

    y_swa = _swa(pb, swa_sinks.astype(F32), _swa_bias_table(rel_bias), batch, seq)

    memkv = _mm(mem2d, w_mem_kv, BF16, batch * MEM_LEN, 1024, "mem_kv")
    y_xa = _xa(pb, memkv, batch, seq)

    merged = _merge(y_ssd, y_swa, y_xa, w_ssd_o, w_swa_o, w_xa_o, pb)
    h1, hp = _out_ln(merged, w_out.astype(BF16), h2d, ln1_g[None, :], ln1_b[None, :], alpha)

    rwt = router_w.T
    r_hi = rwt.astype(BF16)
    r_lo = (rwt - r_hi.astype(F32)).astype(BF16)
    ti = jnp.arange(TM_RT)
    su = (ti[:, None] < ti[None, :]).astype(BF16)
    idx_t, w_t, rank_t, cnt = _router(h1, r_hi, r_lo, router_bias[:, None].astype(F32), su)

    counts = cnt[:, 0].astype(I32)
    padded = (counts + MOE_BLK - 1) // MOE_BLK * MOE_BLK
    pend = jnp.cumsum(padded)
    pstart = pend - padded
    n_blocks = t * TOP_K // MOE_BLK + N_EXPERTS
    blk_lo = jnp.arange(n_blocks, dtype=I32) * MOE_BLK
    block_e = jnp.minimum(jnp.sum((pend[None, :] <= blk_lo[:, None]).astype(I32), axis=1), N_EXPERTS - 1)
    of_e = block_e[:, None] == jnp.arange(N_EXPERTS, dtype=I32)[None, :]
    valid_end = jnp.sum(jnp.where(of_e, (pstart + counts)[None, :], 0), axis=1)
    block_nv = jnp.clip(valid_end - blk_lo, 0, MOE_BLK).astype(I32)
    eids = jnp.arange(N_EXPERTS, dtype=I32)
    later_active = (eids[None, :] > eids[:, None]) & (padded > 0)[None, :]
    next_active = jnp.min(jnp.where(later_active, eids[None, :], N_EXPERTS), axis=1)
    next_active = jnp.where(next_active >= N_EXPERTS, -1, next_active)
    seg_start = jnp.sum(jnp.where(of_e, pstart[None, :], 0), axis=1)
    block_first = ((blk_lo == seg_start) & (block_nv > 0)).astype(I32)
    block_next = jnp.sum(jnp.where(of_e, next_active[None, :], 0), axis=1).astype(I32)
    nn_of = next_active[:, None] == eids[None, :]
    next2_active = jnp.where(next_active >= 0, jnp.sum(jnp.where(nn_of, next_active[None, :], 0), axis=1), -1)
    block_next2 = jnp.sum(jnp.where(of_e, next2_active[None, :], 0), axis=1).astype(I32)

    dest_flat = _dest(idx_t, rank_t, pstart.astype(F32)[:, None]).reshape(-1)
    n_slots = n_blocks * MOE_BLK
    x_sorted, shared = _dispatch(dest_flat, hp, h1, ws1.astype(BF16), ws3.astype(BF16), ws2.astype(BF16),
                                 n_slots)
    ys = _experts(block_e, block_nv, block_first, block_next, block_next2,
                  x_sorted.reshape(n_slots * ROW_TILE, LANES), w1, w3, w2)
    return _final(dest_flat, ys.reshape(n_slots, ROW_TILE, LANES), w_t.T, h1, shared, ln2_g[None, :],
                  ln2_b[None, :], alpha)


def kernel(x, mem, w_in, conv_w, conv_b, dt_bias, a_log, d_skip, ssd_norm_g, swa_sinks, rel_bias, w_mem_kv, w_ssd_o, w_swa_o, w_xa_o, w_out, ln1_g, ln1_b, router_w, router_bias, w1, w3, w2, ws1, ws3, ws2, ln2_g, ln2_b):
    batch, seq, d = x.shape
    depth = w_in.shape[0]
    alpha = (2.0 * depth) ** 0.25
    h = x.reshape(batch * seq, d)
    mem2d = mem.reshape(batch * MEM_LEN, d)
    for i in range(depth):
        h = _layer(h, mem2d, batch, seq, w_in, i, conv_w[i], conv_b[i], dt_bias[i], a_log[i], d_skip[i],
                   ssd_norm_g[i], swa_sinks[i], rel_bias, w_mem_kv[i], w_ssd_o[i], w_swa_o[i], w_xa_o[i], w_out[i],
                   ln1_g[i], ln1_b[i], router_w[i], router_bias[i], w1[i], w3[i], w2[i], ws1[i], ws3[i], ws2[i],
                   ln2_g[i], ln2_b[i], alpha)
    return h.reshape(batch, seq, d)
```

```python
import functools
import math

import jax
import jax.numpy as jnp
from jax import lax
from jax.experimental import pallas as pl
from jax.experimental.pallas import tpu as pltpu

F32 = jnp.float32
BF16 = jnp.bfloat16
I32 = jnp.int32
U32 = jnp.uint32

D_MODEL = 2048
MEM_LEN = 256
SSD_D_INNER = D_MODEL
SSD_HEAD_DIM = 64
SSD_HEADS = 32
SSD_GROUPS = 4
SSD_STATE = 128
SSD_CONV = 4
SSD_CHUNK = 128
SWA_HEADS = 16
SWA_KV_HEADS = 4
SWA_HEAD_DIM = 64
SWA_WINDOW = 128
SWA_BLOCK = 128
REL_BUCKETS = 32
REL_MAX_DIST = 128
XA_HEADS = 4
XA_HEAD_DIM = 256
N_BRANCH = 3
N_EXPERTS = 64
TOP_K = 8
N_EXPERT_GROUPS = 8
TOPK_GROUPS = 4
EXPERT_DIM = 512
SHARED_DIM = 512
ROUTED_SCALE = 2.5
LN_EPS = 1e-5
RMS_EPS = 1e-5

SWA_Q_DIM = SWA_HEADS * SWA_HEAD_DIM
SWA_KV_DIM = SWA_KV_HEADS * SWA_HEAD_DIM
XA_DIM = XA_HEADS * XA_HEAD_DIM
BC_DIM = 2 * SSD_GROUPS * SSD_STATE

W_A = SSD_D_INNER + SSD_D_INNER + BC_DIM
OFFA_Z = 0
OFFA_XS = OFFA_Z + SSD_D_INNER
OFFA_BC = OFFA_XS + SSD_D_INNER
W_B = SWA_Q_DIM + 2 * SWA_KV_DIM + XA_DIM + N_BRANCH * D_MODEL
OFFB_Q = 0
OFFB_K = OFFB_Q + SWA_Q_DIM
OFFB_V = OFFB_K + SWA_KV_DIM
OFFB_QX = OFFB_V + SWA_KV_DIM
OFFB_GATES = OFFB_QX + XA_DIM

LANES = 128
MOE_BLK = 256
NEG_INF = float("-inf")


def _cparams(sem, vmem_mb=48):
    return pltpu.CompilerParams(dimension_semantics=sem, vmem_limit_bytes=vmem_mb * 1024 * 1024)


def _dot(a, b):
    return jnp.dot(a, b, preferred_element_type=F32)


def _dot_nt(a, b):
    return lax.dot_general(a, b, (((1,), (1,)), ((), ())), preferred_element_type=F32)


def _sigmoid(x):
    return 1.0 / (1.0 + jnp.exp(-x))


def _silu(x):
    return x * _sigmoid(x)


TM_IN = 512
TN_IN = W_A // 2


def _inproj_kernel(x_ref, w_ref, wdt_ref, p_ref, dt_ref, xb_ref):
    @pl.when(pl.program_id(1) == 0)
    def _():
        xb = x_ref[...].astype(BF16)
        xb_ref[...] = xb
        dt_ref[...] = _dot(xb, wdt_ref[...])

    p_ref[...] = _dot(xb_ref[...], w_ref[...].astype(BF16)).astype(BF16)


def _inproj(x2d, w_in, wdt):
    t = x2d.shape[0]
    return pl.pallas_call(
        _inproj_kernel,
        grid=(t // TM_IN, W_A // TN_IN),
        in_specs=[
            pl.BlockSpec((TM_IN, D_MODEL), lambda i, j: (i, 0)),
            pl.BlockSpec((D_MODEL, TN_IN), lambda i, j: (0, j)),
            pl.BlockSpec((D_MODEL, LANES), lambda i, j: (0, 0)),
        ],
        out_specs=[
            pl.BlockSpec((TM_IN, TN_IN), lambda i, j: (i, j)),
            pl.BlockSpec((TM_IN, LANES), lambda i, j: (i, 0)),
            pl.BlockSpec((TM_IN, D_MODEL), lambda i, j: (i, 0)),
        ],
        out_shape=[jax.ShapeDtypeStruct((t, W_A), BF16), jax.ShapeDtypeStruct((t, LANES), F32),
                   jax.ShapeDtypeStruct((t, D_MODEL), BF16)],
        compiler_params=_cparams(("parallel", "arbitrary"), vmem_mb=56),
        name="inproj",
    )(x2d, w_in, wdt)


TN_WC = 512
ROW_ALIGN = 32


def _wcast_kernel(a_ref, o_ref):
    o_ref[...] = a_ref[0].T.astype(BF16)


def _wcast(w_t, layer, start, width, name):
    k = w_t.shape[2]
    return pl.pallas_call(
        _wcast_kernel,
        grid=(width // TN_WC,),
        in_specs=[pl.BlockSpec((pl.Element(1), pl.Element(TN_WC), pl.Element(k)),
                               lambda j: (layer, pl.multiple_of(start + j * TN_WC, ROW_ALIGN), 0))],
        out_specs=pl.BlockSpec((k, TN_WC), lambda j: (0, j)),
        out_shape=jax.ShapeDtypeStruct((k, width), BF16),
        compiler_params=_cparams(("parallel",)),
        name=name,
    )(w_t)


def _mm_kernel(a_ref, b_ref, o_ref):
    o_ref[...] = _dot(a_ref[...].astype(BF16), b_ref[...].astype(BF16)).astype(o_ref.dtype)


def _mm(a, b, out_dtype, tm, tn, name, vmem_mb=48):
    m, k = a.shape
    n = b.shape[1]
    return pl.pallas_call(
        _mm_kernel,
        grid=(m // tm, n // tn),
        in_specs=[pl.BlockSpec((tm, k), lambda i, j: (i, 0)), pl.BlockSpec((k, tn), lambda i, j: (0, j))],
        out_specs=pl.BlockSpec((tm, tn), lambda i, j: (i, j)),
        out_shape=jax.ShapeDtypeStruct((m, n), out_dtype),
        compiler_params=_cparams(("parallel", "parallel"), vmem_mb=vmem_mb),
        name=name,
    )(a, b)


Q = SSD_CHUNK


def _conv_shift_matrix():
    t = jnp.arange(Q)[:, None]
    j = jnp.arange(2 * Q)[None, :]
    blocks = []
    for k in range(SSD_CONV - 1):
        src = t - (SSD_CONV - 1 - k)
        blocks.append(j == jnp.where(src >= 0, src, 2 * Q + src))
    return jnp.concatenate(blocks, axis=0).astype(BF16)


def _ssd_kernel(xs_ref, bc_ref, z_ref, dtr_ref, shift_ref, cwx_ref, cbx_ref, cwb_ref, cbb_ref, dtb_ref, alog_ref,
                dsk_ref, ng_ref, o_ref, hx_ref, hb_ref, st_ref, cumt_ref, dtt_ref, wt_ref, y_ref):
    c = pl.program_id(1)

    @pl.when(c == 0)
    def _():
        hx_ref[...] = jnp.zeros_like(hx_ref)
        hb_ref[...] = jnp.zeros_like(hb_ref)
        st_ref[...] = jnp.zeros_like(st_ref)

    def conv_silu(in_ref, prev_ref, w_ref, b_ref):
        cur = in_ref[...]
        ext = jnp.concatenate([cur, prev_ref[...]], axis=0)
        shifted = _dot(shift_ref[...], ext)
        acc = b_ref[...] + w_ref[SSD_CONV - 1:SSD_CONV, :] * cur.astype(F32)
        for k in range(SSD_CONV - 1):
            acc = acc + w_ref[k:k + 1, :] * shifted[k * Q:(k + 1) * Q, :]
        prev_ref[...] = cur
        return _silu(acc)

    xs_c = conv_silu(xs_ref, hx_ref, cwx_ref, cbx_ref)
    bc_c = conv_silu(bc_ref, hb_ref, cwb_ref, cbb_ref)

    lane = lax.broadcasted_iota(I32, (Q, LANES), 1)
    row = lax.broadcasted_iota(I32, (Q, LANES), 0)
    tril = row >= lane
    lo_mask = lane < SSD_HEAD_DIM

    dtv = dtr_ref[...] + dtb_ref[...]
    dt = jnp.maximum(dtv, 0.0) + jnp.log1p(jnp.exp(-jnp.abs(dtv)))
    a = jnp.where(lane[0:1, :] < SSD_HEADS, -jnp.exp(alog_ref[...]), 0.0)
    la = dt * a
    tri = jnp.where(tril, 1.0, 0.0).astype(BF16)
    p1 = la.astype(BF16)
    r1 = la - p1.astype(F32)
    p2 = r1.astype(BF16)
    p3 = (r1 - p2.astype(F32)).astype(BF16)
    cum = _dot(tri, p1) + _dot(tri, p2) + _dot(tri, p3)
    ecum = jnp.exp(cum)
    cum_t = cum.T
    dt_t = dt.T
    last = cum_t[:, Q - 1:Q]
    cumt_ref[...] = cum_t
    dtt_ref[...] = dt_t
    wt_ref[...] = jnp.exp(last - cum_t) * dt_t
    chunk_decay = jnp.exp(last)

    def head_parts(h, cb, cm, bm_t):
        col = jnp.broadcast_to(cum[:, h:h + 1], (Q, Q))
        rw = cumt_ref[h:h + 1, :]
        dec = jnp.exp(jnp.where(tril, col - rw, NEG_INF))
        m = cb * (dec * dtt_ref[h:h + 1, :])
        ce = cm * jnp.broadcast_to(ecum[:, h:h + 1], (Q, Q))
        sl = bm_t * wt_ref[h:h + 1, :]
        return m.astype(BF16), ce.astype(BF16), sl.astype(BF16)

    for g in range(SSD_GROUPS):
        bm = bc_c[:, g * SSD_STATE:(g + 1) * SSD_STATE]
        cm = bc_c[:, BC_DIM // 2 + g * SSD_STATE:BC_DIM // 2 + (g + 1) * SSD_STATE]
        cb = _dot_nt(cm.astype(BF16), bm.astype(BF16))
        bm_t = bm.T
        for qd in range(SSD_HEADS // SSD_GROUPS // 2):
            hp = g * (SSD_HEADS // SSD_GROUPS // 2) + qd
            ha, hb = 2 * hp, 2 * hp + 1
            sl_ = slice(hp * LANES, (hp + 1) * LANES)
            xs_pair = xs_c[:, sl_]
            x_lo = jnp.where(lo_mask, xs_pair, 0.0).astype(BF16)
            x_hi = jnp.where(lo_mask, 0.0, xs_pair).astype(BF16)
            h_prev = st_ref[:, sl_]
            h_lo = jnp.where(lo_mask, h_prev, 0.0).astype(BF16)
            h_hi = jnp.where(lo_mask, 0.0, h_prev).astype(BF16)
            m_a, ce_a, sl_a = head_parts(ha, cb, cm, bm_t)
            m_b, ce_b, sl_b = head_parts(hb, cb, cm, bm_t)
            lhs = jnp.concatenate([m_a, ce_a, m_b, ce_b], axis=1)
            rhs = jnp.concatenate([x_lo, h_lo, x_hi, h_hi], axis=0)
            y_ref[:, sl_] = _dot(lhs, rhs)
            new = _dot(jnp.concatenate([sl_a, sl_b], axis=1), jnp.concatenate([x_lo, x_hi], axis=0))
            cd = jnp.where(lo_mask[0:1, :], chunk_decay[ha:ha + 1, :], chunk_decay[hb:hb + 1, :])
            st_ref[:, sl_] = h_prev * cd + new

    y = y_ref[...] + xs_c * dsk_ref[...]
    y = y * _silu(z_ref[...].astype(F32))
    gw = SSD_D_INNER // SSD_GROUPS
    for g in range(SSD_GROUPS):
        yg = y[:, g * gw:(g + 1) * gw]
        ms = jnp.mean(yg * yg, axis=-1, keepdims=True)
        o_ref[:, g * gw:(g + 1) * gw] = ((yg * lax.rsqrt(ms + RMS_EPS)) * ng_ref[:, g * gw:(g + 1) * gw]).astype(BF16)


def _ssd(p, dt_raw, cwx, cbx, cwb, cbb, dtb, alog, dsk, ng, batch, seq):
    nc = seq // Q
    t = batch * seq
    rowblk = lambda b, c: b * nc + c
    const = lambda b, c: (0, 0)
    return pl.pallas_call(
        _ssd_kernel,
        grid=(batch, nc),
        in_specs=[
            pl.BlockSpec((Q, SSD_D_INNER), lambda b, c: (rowblk(b, c), OFFA_XS // SSD_D_INNER)),
            pl.BlockSpec((Q, BC_DIM), lambda b, c: (rowblk(b, c), OFFA_BC // BC_DIM)),
            pl.BlockSpec((Q, SSD_D_INNER), lambda b, c: (rowblk(b, c), OFFA_Z // SSD_D_INNER)),
            pl.BlockSpec((Q, LANES), lambda b, c: (rowblk(b, c), 0)),
            pl.BlockSpec(((SSD_CONV - 1) * Q, 2 * Q), const),
            pl.BlockSpec((SSD_CONV, SSD_D_INNER), const),
            pl.BlockSpec((1, SSD_D_INNER), const),
            pl.BlockSpec((SSD_CONV, BC_DIM), const),
            pl.BlockSpec((1, BC_DIM), const),
            pl.BlockSpec((1, LANES), const),
            pl.BlockSpec((1, LANES), const),
            pl.BlockSpec((1, SSD_D_INNER), const),
            pl.BlockSpec((1, SSD_D_INNER), const),
        ],
        out_specs=pl.BlockSpec((Q, SSD_D_INNER), lambda b, c: (rowblk(b, c), 0)),
        out_shape=jax.ShapeDtypeStruct((t, SSD_D_INNER), BF16),
        scratch_shapes=[
            pltpu.VMEM((Q, SSD_D_INNER), BF16),
            pltpu.VMEM((Q, BC_DIM), BF16),
            pltpu.VMEM((SSD_STATE, SSD_D_INNER), F32),
            pltpu.VMEM((LANES, Q), F32),
            pltpu.VMEM((LANES, Q), F32),
            pltpu.VMEM((LANES, Q), F32),
            pltpu.VMEM((Q, SSD_D_INNER), F32),
        ],
        compiler_params=_cparams(("parallel", "arbitrary")),
        name="ssd",
    )(p, p, p, dt_raw, _conv_shift_matrix(), cwx, cbx, cwb, cbb, dtb, alog, dsk, ng)


N_PAIRS = SWA_HEADS // 2


def _swa_kernel(sink_ref, q_ref, kp_ref, kc_ref, vp_ref, vc_ref, bias_ref, o_ref):
    n = pl.program_id(1)
    blk = SWA_BLOCK
    kband = jnp.concatenate([kp_ref[...], kc_ref[...]], axis=0)
    vband = jnp.concatenate([vp_ref[...], vc_ref[...]], axis=0)

    lane = lax.broadcasted_iota(I32, (1, LANES), 1)
    lo_f = jnp.where(lane < SWA_HEAD_DIM, 1.0, 0.0)
    lo_m = lo_f.astype(BF16)
    hi_m = (1.0 - lo_f).astype(BF16)

    def head_twice(band, kv):
        tile = band[:, (kv // 2) * LANES:(kv // 2 + 1) * LANES]
        swapped = jnp.concatenate([tile[:, SWA_HEAD_DIM:], tile[:, :SWA_HEAD_DIM]], axis=1)
        return tile * lo_m + swapped * hi_m if kv % 2 == 0 else swapped * lo_m + tile * hi_m

    k_twice = [head_twice(kband, kv) for kv in range(SWA_KV_HEADS)]
    v_twice = [head_twice(vband, kv) for kv in range(SWA_KV_HEADS)]
    colj = lax.broadcasted_iota(I32, (2 * blk, 2 * blk), 1)
    rowi = lax.broadcasted_iota(I32, (2 * blk, 1), 0)
    valid = (colj >= blk) | (n > 0)
    scale = SWA_HEAD_DIM ** -0.5
    for p in range(N_PAIRS):
        kv = p // (SWA_HEADS // SWA_KV_HEADS // 2)
        qp = q_ref[:, p * LANES:(p + 1) * LANES]
        qs = jnp.concatenate([qp * lo_m, qp * hi_m], axis=0)
        s = _dot_nt(qs, k_twice[kv]) * scale + bias_ref[p]
        s = jnp.where(valid, s, NEG_INF)
        sink = jnp.where(rowi < blk, sink_ref[2 * p], sink_ref[2 * p + 1])
        m = jnp.maximum(jnp.max(s, axis=-1, keepdims=True), sink)
        e = jnp.exp(s - m)
        den = jnp.sum(e, axis=-1, keepdims=True) + jnp.exp(sink - m)
        pr = (e * (1.0 / den)).astype(BF16)
        vd = v_twice[kv]
        lhs = jnp.concatenate([pr[:blk], pr[blk:]], axis=1)
        rhs = jnp.concatenate([vd * lo_m, vd * hi_m], axis=0)
        o_ref[:, p * LANES:(p + 1) * LANES] = _dot(lhs, rhs).astype(BF16)


def _swa(p, sinks, bias, batch, seq):
    nb = seq // SWA_BLOCK
    t = batch * seq
    kw = SWA_KV_DIM
    cur = lambda b, n, s: b * nb + n
    prev = lambda b, n, s: b * nb + jnp.maximum(n - 1, 0)
    return pl.pallas_call(
        _swa_kernel,
        grid_spec=pltpu.PrefetchScalarGridSpec(
            num_scalar_prefetch=1,
            grid=(batch, nb),
            in_specs=[
                pl.BlockSpec((SWA_BLOCK, SWA_Q_DIM), lambda b, n, s: (cur(b, n, s), OFFB_Q // SWA_Q_DIM)),
                pl.BlockSpec((SWA_BLOCK, kw), lambda b, n, s: (prev(b, n, s), OFFB_K // kw)),
                pl.BlockSpec((SWA_BLOCK, kw), lambda b, n, s: (cur(b, n, s), OFFB_K // kw)),
                pl.BlockSpec((SWA_BLOCK, kw), lambda b, n, s: (prev(b, n, s), OFFB_V // kw)),
                pl.BlockSpec((SWA_BLOCK, kw), lambda b, n, s: (cur(b, n, s), OFFB_V // kw)),
                pl.BlockSpec((N_PAIRS, 2 * SWA_BLOCK, 2 * SWA_BLOCK), lambda b, n, s: (0, 0, 0)),
            ],
            out_specs=pl.BlockSpec((SWA_BLOCK, SWA_Q_DIM), lambda b, n, s: (cur(b, n, s), 0)),
        ),
        out_shape=jax.ShapeDtypeStruct((t, SWA_Q_DIM), BF16),
        compiler_params=_cparams(("parallel", "arbitrary")),
        name="swa",
    )(sinks, p, p, p, p, p, bias)


def _t5_causal_bucket(dist):
    max_exact = REL_BUCKETS // 2
    large = max_exact + (jnp.log(jnp.maximum(dist, 1).astype(F32) / max_exact)
                         / math.log(REL_MAX_DIST / max_exact) * (REL_BUCKETS - max_exact)).astype(I32)
    large = jnp.minimum(large, REL_BUCKETS - 1)
    return jnp.where(dist < max_exact, dist, large)


def _swa_bias_table(rel_bias):
    qi = jnp.arange(SWA_BLOCK)[:, None]
    kj = jnp.arange(2 * SWA_BLOCK)[None, :]
    dist = qi + SWA_BLOCK - kj
    in_window = (dist >= 0) & (dist < SWA_WINDOW)
    bucket = _t5_causal_bucket(jnp.maximum(dist, 0))
    onehot = (bucket[:, :, None] == jnp.arange(REL_BUCKETS)[None, None, :]).astype(F32)
    bias = jnp.einsum('ijb,bh->hij', onehot, rel_bias.astype(F32), precision=lax.Precision.HIGHEST)
    bias = jnp.where(in_window[None, :, :], bias, NEG_INF)
    return bias.reshape(N_PAIRS, 2 * SWA_BLOCK, 2 * SWA_BLOCK)


TL_XA = 512


XA_HALF = XA_DIM // 2


def _xa_kernel(qa_ref, qb_ref, mk_ref, mv_ref, o_ref):
    for h in range(XA_HEADS):
        q_ref = qa_ref if h < XA_HEADS // 2 else qb_ref
        qc = (h % (XA_HEADS // 2)) * XA_HEAD_DIM
        cols = slice(h * XA_HEAD_DIM, (h + 1) * XA_HEAD_DIM)
        s = _dot_nt(q_ref[:, qc:qc + XA_HEAD_DIM], mk_ref[:, cols]) * (XA_HEAD_DIM ** -0.5)
        m = jnp.max(s, axis=-1, keepdims=True)
        e = jnp.exp(s - m)
        pr = e * (1.0 / jnp.sum(e, axis=-1, keepdims=True))
        o_ref[:, cols] = _dot(pr.astype(BF16), mv_ref[:, cols]).astype(BF16)


def _xa(p, memkv, batch, seq):
    nl = seq // TL_XA
    t = batch * seq
    return pl.pallas_call(
        _xa_kernel,
        grid=(batch, nl),
        in_specs=[
            pl.BlockSpec((TL_XA, XA_HALF), lambda b, l: (b * nl + l, OFFB_QX // XA_HALF)),
            pl.BlockSpec((TL_XA, XA_HALF), lambda b, l: (b * nl + l, OFFB_QX // XA_HALF + 1)),
            pl.BlockSpec((MEM_LEN, XA_DIM), lambda b, l: (b, 0)),
            pl.BlockSpec((MEM_LEN, XA_DIM), lambda b, l: (b, 1)),
        ],
        out_specs=pl.BlockSpec((TL_XA, XA_DIM), lambda b, l: (b * nl + l, 0)),
        out_shape=jax.ShapeDtypeStruct((t, XA_DIM), BF16),
        compiler_params=_cparams(("parallel", "parallel")),
        name="xattn",
    )(p, p, memkv, memkv)


TM_MG = 1024
TN_MG = 512


N_SUB_MG = 2


def _merge_kernel(ys_ref, yw_ref, yx_ref, ws_ref, ww_ref, wx_ref, g0_ref, g1_ref, g2_ref, o_ref):
    sub = TM_MG // N_SUB_MG
    ws = ws_ref[...].astype(BF16)
    ww = ww_ref[...].astype(BF16)
    wx = wx_ref[...].astype(BF16)
    for s in range(N_SUB_MG):
        rows = slice(s * sub, (s + 1) * sub)
        a = _dot(ys_ref[rows, :], ws)
        b = _dot(yw_ref[rows, :], ww)
        c = _dot(yx_ref[rows, :], wx)
        g0 = _sigmoid(g0_ref[rows, :].astype(F32))
        g1 = _sigmoid(g1_ref[rows, :].astype(F32))
        g2 = _sigmoid(g2_ref[rows, :].astype(F32))
        o_ref[rows, :] = (g0 * a + g1 * b + g2 * c).astype(BF16)


def _merge(y_ssd, y_swa, y_xa, w_ssd_o, w_swa_o, w_xa_o, p):
    t = y_ssd.shape[0]
    gpb = D_MODEL // TN_MG
    gate = lambda k: pl.BlockSpec((TM_MG, TN_MG), lambda i, j: (i, OFFB_GATES // TN_MG + k * gpb + j))
    return pl.pallas_call(
        _merge_kernel,
        grid=(t // TM_MG, D_MODEL // TN_MG),
        in_specs=[
            pl.BlockSpec((TM_MG, SSD_D_INNER), lambda i, j: (i, 0)),
            pl.BlockSpec((TM_MG, SWA_Q_DIM), lambda i, j: (i, 0)),
            pl.BlockSpec((TM_MG, XA_DIM), lambda i, j: (i, 0)),
            pl.BlockSpec((SSD_D_INNER, TN_MG), lambda i, j: (0, j)),
            pl.BlockSpec((SWA_Q_DIM, TN_MG), lambda i, j: (0, j)),
            pl.BlockSpec((XA_DIM, TN_MG), lambda i, j: (0, j)),
            gate(0), gate(1), gate(2),
        ],
        out_specs=pl.BlockSpec((TM_MG, TN_MG), lambda i, j: (i, j)),
        out_shape=jax.ShapeDtypeStruct((t, D_MODEL), BF16),
        compiler_params=_cparams(("parallel", "parallel")),
        name="merge",
    )(y_ssd, y_swa, y_xa, w_ssd_o, w_swa_o, w_xa_o, p, p, p)


TM_LN = 512


def _layer_norm(v, g, b):
    mu = jnp.mean(v, axis=-1, keepdims=True)
    d = v - mu
    var = jnp.mean(d * d, axis=-1, keepdims=True)
    return (d * lax.rsqrt(var + LN_EPS)) * g + b


def _pack_halves(v):
    n = v.shape[1] // 2
    lo = lax.bitcast_convert_type(v[:, :n].astype(BF16).astype(F32), U32) >> 16
    hi = lax.bitcast_convert_type(v[:, n:].astype(BF16).astype(F32), U32) & jnp.uint32(0xFFFF0000)
    return hi | lo


def _unpack_halves(w):
    lo = lax.bitcast_convert_type(w << 16, F32)
    hi = lax.bitcast_convert_type(w & jnp.uint32(0xFFFF0000), F32)
    return lo, hi


ROW_WORDS = D_MODEL // 2
ROW_TILE = ROW_WORDS // LANES


def _store_row_tiled(ref, words):
    m = words.shape[0]
    for c in range(ROW_TILE):
        ref[pl.ds(c, m, stride=ROW_TILE), :] = words[:, c * LANES:(c + 1) * LANES]


def _load_row_tiled(ref, m):
    return jnp.concatenate([ref[pl.ds(c, m, stride=ROW_TILE), :] for c in range(ROW_TILE)], axis=1)


N_SUB_LN = 4


def _out_ln_kernel(alpha, m_ref, w_ref, x_ref, g_ref, b_ref, h_ref, hp_ref):
    sub = TM_LN // N_SUB_LN
    for c in range(N_SUB_LN):
        rows = slice(c * sub, (c + 1) * sub)
        mix = _dot(m_ref[rows, :], w_ref[...])
        hn = _layer_norm(alpha * x_ref[rows, :] + mix, g_ref[...], b_ref[...])
        h_ref[rows, :] = hn
        _store_row_tiled(hp_ref.at[pl.ds(c * sub * ROW_TILE, sub * ROW_TILE), :], _pack_halves(hn))


def _out_ln(merged, w_out, x2d, g, b, alpha):
    t = x2d.shape[0]
    row = lambda i: (i, 0)
    const = lambda i: (0, 0)
    return pl.pallas_call(
        functools.partial(_out_ln_kernel, alpha),
        grid=(t // TM_LN,),
        in_specs=[
            pl.BlockSpec((TM_LN, D_MODEL), row),
            pl.BlockSpec((D_MODEL, D_MODEL), const),
            pl.BlockSpec((TM_LN, D_MODEL), row),
            pl.BlockSpec((1, D_MODEL), const),
            pl.BlockSpec((1, D_MODEL), const),
        ],
        out_specs=[pl.BlockSpec((TM_LN, D_MODEL), row), pl.BlockSpec((TM_LN * ROW_TILE, LANES), row)],
        out_shape=[jax.ShapeDtypeStruct((t, D_MODEL), F32), jax.ShapeDtypeStruct((t * ROW_TILE, LANES), U32)],
        compiler_params=_cparams(("parallel",)),
        name="out_ln1",
    )(merged, w_out, x2d, g, b)


TM_RT = 512
EPG = N_EXPERTS // N_EXPERT_GROUPS


def _router_kernel(h_ref, whi_ref, wlo_ref, bias_ref, su_ref, idx_ref, w_ref, rank_ref, cnt_ref, run_ref):
    @pl.when(pl.program_id(0) == 0)
    def _():
        run_ref[...] = jnp.zeros_like(run_ref)

    h = h_ref[...]
    h_hi = h.astype(BF16)
    h_lo = (h - h_hi.astype(F32)).astype(BF16)
    whi = whi_ref[...]
    logits = _dot_nt(whi, h_hi) + _dot_nt(whi, h_lo) + _dot_nt(wlo_ref[...], h_hi)
    scores = _sigmoid(logits)
    sel = scores + bias_ref[...]

    iota_m = lax.broadcasted_iota(I32, (EPG, TM_RT), 0).astype(F32)
    iota_g = lax.broadcasted_iota(I32, (N_EXPERT_GROUPS, TM_RT), 0).astype(F32)
    gs = []
    for g in range(N_EXPERT_GROUPS):
        v = sel[g * EPG:(g + 1) * EPG, :]
        m1 = jnp.max(v, axis=0, keepdims=True)
        i1 = jnp.min(jnp.where(v == m1, iota_m, float(EPG)), axis=0, keepdims=True)
        m2 = jnp.max(jnp.where(iota_m == i1, NEG_INF, v), axis=0, keepdims=True)
        gs.append(m1 + m2)
    gsc = jnp.concatenate(gs, axis=0)
    chosen = jnp.zeros((N_EXPERT_GROUPS, TM_RT), F32)
    for _ in range(TOPK_GROUPS):
        m = jnp.max(gsc, axis=0, keepdims=True)
        ig = jnp.min(jnp.where(gsc == m, iota_g, float(N_EXPERT_GROUPS)), axis=0, keepdims=True)
        hit = iota_g == ig
        chosen = jnp.where(hit, 1.0, chosen)
        gsc = jnp.where(hit, NEG_INF, gsc)
    gmask = jnp.concatenate(
        [jnp.broadcast_to(chosen[g:g + 1, :], (EPG, TM_RT)) for g in range(N_EXPERT_GROUPS)], axis=0)
    masked = jnp.where(gmask > 0.0, sel, NEG_INF)

    iota_e = lax.broadcasted_iota(I32, (N_EXPERTS, TM_RT), 0).astype(F32)
    hits, idxs, ws = [], [], []
    for _ in range(TOP_K):
        m = jnp.max(masked, axis=0, keepdims=True)
        ie = jnp.min(jnp.where(masked == m, iota_e, float(N_EXPERTS)), axis=0, keepdims=True)
        hit = iota_e == ie
        hits.append(hit)
        idxs.append(ie)
        ws.append(jnp.sum(jnp.where(hit, scores, 0.0), axis=0, keepdims=True))
        masked = jnp.where(hit, NEG_INF, masked)
    wsum = ws[0]
    for k in range(1, TOP_K):
        wsum = wsum + ws[k]
    inv = 1.0 / wsum
    idx_ref[...] = jnp.concatenate(idxs, axis=0).astype(I32)
    w_ref[...] = jnp.concatenate([w * inv * ROUTED_SCALE for w in ws], axis=0)

    assign = jnp.zeros((N_EXPERTS, TM_RT), F32)
    for hit in hits:
        assign = jnp.where(hit, 1.0, assign)
    pos = _dot(assign.astype(BF16), su_ref[...]) + run_ref[...]
    rank_ref[...] = jnp.concatenate(
        [jnp.sum(jnp.where(hit, pos, 0.0), axis=0, keepdims=True) for hit in hits], axis=0).astype(I32)
    run = run_ref[...] + jnp.sum(assign, axis=1, keepdims=True)
    run_ref[...] = run
    cnt_ref[...] = run


def _router(h1, r_hi, r_lo, rbias, su):
    t = h1.shape[0]
    tile = lambda i: (0, i)
    const = lambda i: (0, 0)
    return pl.pallas_call(
        _router_kernel,
        grid=(t // TM_RT,),
        in_specs=[
            pl.BlockSpec((TM_RT, D_MODEL), lambda i: (i, 0)),
            pl.BlockSpec((N_EXPERTS, D_MODEL), const),
            pl.BlockSpec((N_EXPERTS, D_MODEL), const),
            pl.BlockSpec((N_EXPERTS, 1), const),
            pl.BlockSpec((TM_RT, TM_RT), const),
        ],
        out_specs=[
            pl.BlockSpec((TOP_K, TM_RT), tile),
            pl.BlockSpec((TOP_K, TM_RT), tile),
            pl.BlockSpec((TOP_K, TM_RT), tile),
            pl.BlockSpec((N_EXPERTS, TM_RT), const),
        ],
        out_shape=[
            jax.ShapeDtypeStruct((TOP_K, t), I32),
            jax.ShapeDtypeStruct((TOP_K, t), F32),
            jax.ShapeDtypeStruct((TOP_K, t), I32),
            jax.ShapeDtypeStruct((N_EXPERTS, TM_RT), F32),
        ],
        scratch_shapes=[pltpu.VMEM((N_EXPERTS, TM_RT), F32)],
        compiler_params=_cparams(("arbitrary",)),
        name="router",
    )(h1, r_hi, r_lo, rbias, su)


TM_DS = 2048


def _dest_kernel(idx_ref, rank_ref, ps_ref, d_ref):
    iota_e = lax.broadcasted_iota(I32, (N_EXPERTS, TM_DS), 0)
    ps = ps_ref[...]
    rows = []
    for k in range(TOP_K):
        hit = iota_e == idx_ref[k:k + 1, :]
        rows.append(jnp.sum(jnp.where(hit, ps, 0.0), axis=0, keepdims=True))
    d_ref[...] = jnp.concatenate(rows, axis=0).astype(I32) + rank_ref[...]


def _dest(idx_t, rank_t, pstart):
    t = idx_t.shape[1]
    tile = lambda i: (0, i)
    return pl.pallas_call(
        _dest_kernel,
        grid=(t // TM_DS,),
        in_specs=[pl.BlockSpec((TOP_K, TM_DS), tile), pl.BlockSpec((TOP_K, TM_DS), tile),
                  pl.BlockSpec((N_EXPERTS, 1), lambda i: (0, 0))],
        out_specs=pl.BlockSpec((TOP_K, TM_DS), tile),
        out_shape=jax.ShapeDtypeStruct((TOP_K, t), I32),
        compiler_params=_cparams(("parallel",)),
        name="dest",
    )(idx_t, rank_t, pstart)


TM_DP = 512


def _dispatch_kernel(dest_ref, hp_ref, h_ref, ws1_ref, ws3_ref, ws2_ref, xs_ref, sh_ref, sem):
    t_total = pl.num_programs(0) * TM_DP
    base = pl.program_id(0) * TM_DP

    def issue(t, carry):
        src = hp_ref.at[pl.ds(pl.multiple_of(t * ROW_TILE, ROW_TILE), ROW_TILE), :]
        for k in range(TOP_K):
            slot = dest_ref[k * t_total + base + t]
            pltpu.make_async_copy(src, xs_ref.at[slot], sem).start(priority=k % 2)
        return carry

    lax.fori_loop(0, TM_DP, issue, 0)

    xb = h_ref[...].astype(BF16)
    hdn = _silu(_dot(xb, ws1_ref[...])) * _dot(xb, ws3_ref[...])
    sh_ref[...] = _dot(hdn.astype(BF16), ws2_ref[...])

    for k in range(TOP_K):
        pltpu.make_async_copy(hp_ref.reshape(TM_DP, ROW_TILE, LANES), xs_ref.at[pl.ds(0, TM_DP)], sem).wait()


def _dispatch(dest_flat, hp, h1, ws1, ws3, ws2, n_slots):
    t = h1.shape[0]
    row = lambda i, d: (i, 0)
    const = lambda i, d: (0, 0)
    return pl.pallas_call(
        _dispatch_kernel,
        grid_spec=pltpu.PrefetchScalarGridSpec(
            num_scalar_prefetch=1,
            grid=(t // TM_DP,),
            in_specs=[
                pl.BlockSpec((TM_DP * ROW_TILE, LANES), row),
                pl.BlockSpec((TM_DP, D_MODEL), row),
                pl.BlockSpec((D_MODEL, SHARED_DIM), const),
                pl.BlockSpec((D_MODEL, SHARED_DIM), const),
                pl.BlockSpec((SHARED_DIM, D_MODEL), const),
            ],
            out_specs=[pl.BlockSpec(memory_space=pl.ANY), pl.BlockSpec((TM_DP, D_MODEL), row)],
            scratch_shapes=[pltpu.SemaphoreType.DMA],
        ),
        out_shape=[jax.ShapeDtypeStruct((n_slots, ROW_TILE, LANES), U32), jax.ShapeDtypeStruct((t, D_MODEL), F32)],
        compiler_params=_cparams(("arbitrary",)),
        name="dispatch",
    )(dest_flat, hp, h1, ws1, ws3, ws2)


N_WSLOTS = 2
BLK_PER_STEP = 2


def _expert_kernel(be_ref, nv_ref, first_ref, nxt_ref, x_ref, w1_ref, w3_ref, w2_ref, o_ref,
                   wf1, wf3, wf2, w1b, w3b, w2b, slot_ref, sems):
    step = pl.program_id(0)

    def weight_copies(e, s):
        return (pltpu.make_async_copy(w1_ref.at[e], wf1.at[s], sems.at[s, 0]),
                pltpu.make_async_copy(w3_ref.at[e], wf3.at[s], sems.at[s, 1]),
                pltpu.make_async_copy(w2_ref.at[e], wf2.at[s], sems.at[s, 2]))

    @pl.when(step == 0)
    def _():
        slot_ref[0] = 0
        for cp in weight_copies(be_ref[0], 0):
            cp.start()

    def one_block(b, xq, oq):
        nv = nv_ref[b]

        @pl.when(first_ref[b] == 1)
        def _():
            s = slot_ref[0]
            for cp in weight_copies(be_ref[b], s):
                cp.wait()

            @pl.when(nxt_ref[b] >= 0)
            def _():
                for cp in weight_copies(nxt_ref[b], 1 - s):
                    cp.start(priority=1)

            w1b[...] = wf1[s].astype(BF16)
            w3b[...] = wf3[s].astype(BF16)
            w2b[...] = wf2[s].astype(BF16)
            slot_ref[0] = 1 - s

        @pl.when(nv > 0)
        def _():
            live = lax.broadcasted_iota(I32, (MOE_BLK, 1), 0) < nv
            lo, hi = _unpack_halves(jnp.where(live, _load_row_tiled(xq, MOE_BLK), jnp.uint32(0)))
            x = jnp.concatenate([lo.astype(BF16), hi.astype(BF16)], axis=1)
            hdn = _silu(_dot(x, w1b[...])) * _dot(x, w3b[...])
            _store_row_tiled(oq, _pack_halves(_dot(hdn.astype(BF16), w2b[...])))

        @pl.when(nv <= 0)
        def _():
            oq[...] = jnp.zeros_like(oq)

    blk_rows = MOE_BLK * ROW_TILE
    for q in range(BLK_PER_STEP):
        rows = pl.ds(q * blk_rows, blk_rows)
        one_block(step * BLK_PER_STEP + q, x_ref.at[rows, :], o_ref.at[rows, :])


def _experts(block_e, block_nv, block_first, block_next, x_sorted, w1, w3, w2):
    n_rows = x_sorted.shape[0]
    step_rows = BLK_PER_STEP * MOE_BLK * ROW_TILE
    row = lambda b, *_: (b, 0)
    return pl.pallas_call(
        _expert_kernel,
        grid_spec=pltpu.PrefetchScalarGridSpec(
            num_scalar_prefetch=4,
            grid=(n_rows // step_rows,),
            in_specs=[
                pl.BlockSpec((step_rows, LANES), row),
                pl.BlockSpec(memory_space=pl.ANY),
                pl.BlockSpec(memory_space=pl.ANY),
                pl.BlockSpec(memory_space=pl.ANY),
            ],
            out_specs=pl.BlockSpec((step_rows, LANES), row),
            scratch_shapes=[
                pltpu.VMEM((N_WSLOTS, D_MODEL, EXPERT_DIM), F32),
                pltpu.VMEM((N_WSLOTS, D_MODEL, EXPERT_DIM), F32),
                pltpu.VMEM((N_WSLOTS, EXPERT_DIM, D_MODEL), F32),
                pltpu.VMEM((D_MODEL, EXPERT_DIM), BF16),
                pltpu.VMEM((D_MODEL, EXPERT_DIM), BF16),
                pltpu.VMEM((EXPERT_DIM, D_MODEL), BF16),
                pltpu.SMEM((1,), I32),
                pltpu.SemaphoreType.DMA((N_WSLOTS, 3)),
            ],
        ),
        out_shape=jax.ShapeDtypeStruct((n_rows, LANES), U32),
        compiler_params=_cparams(("arbitrary",), vmem_mb=58),
        name="experts",
    )(block_e, block_nv, block_first, block_next, x_sorted, w1, w3, w2)


TM_FN = 256


CH_FN = 8


def _final_kernel(alpha, dest_ref, ys_ref, w_ref, h_ref, sh_ref, g_ref, b_ref, o_ref, buf, sems):
    i = pl.program_id(0)
    n_tiles = pl.num_programs(0)
    t_total = n_tiles * TM_FN
    slot = i % 2
    nslot = 1 - slot
    half = D_MODEL // 2

    def issue_token(tile, sl, c, tt):
        tok = c * CH_FN + tt
        row0 = pl.multiple_of(tok * ROW_TILE, ROW_TILE)
        for k in range(TOP_K):
            src = ys_ref.at[dest_ref[k * t_total + tile * TM_FN + tok]]
            pltpu.make_async_copy(src, buf.at[sl, k, pl.ds(row0, ROW_TILE), :], sems.at[sl]).start(priority=k % 2)

    def wait_tile(sl):
        for k in range(TOP_K):
            pltpu.make_async_copy(ys_ref.at[pl.ds(0, TM_FN)], buf.at[sl, k].reshape(TM_FN, ROW_TILE, LANES),
                                  sems.at[sl]).wait()

    @pl.when(i == 0)
    def _():
        def first(c, carry):
            for tt in range(CH_FN):
                issue_token(0, 0, c, tt)
            return carry
        lax.fori_loop(0, TM_FN // CH_FN, first, 0)

    wait_tile(slot)

    nxt_tile = jnp.minimum(i + 1, n_tiles - 1)

    def body(c, carry):
        tok = pl.ds(pl.multiple_of(c * CH_FN, CH_FN), CH_FN)
        acc_lo = sh_ref[tok, :half]
        acc_hi = sh_ref[tok, half:]
        pre = alpha * h_ref[tok, :]
        wrow = w_ref[tok, :]
        for k in range(TOP_K):
            words = jnp.concatenate(
                [buf[slot, k, pl.ds(c * (CH_FN * ROW_TILE) + r, CH_FN, stride=ROW_TILE), :] for r in range(ROW_TILE)],
                axis=1)
            issue_token(nxt_tile, nslot, c, k)
            lo, hi = _unpack_halves(words)
            wk = wrow[:, k:k + 1]
            acc_lo = acc_lo + wk * lo
            acc_hi = acc_hi + wk * hi
        o_ref[tok, :] = pre + jnp.concatenate([acc_lo, acc_hi], axis=1)
        return carry

    lax.fori_loop(0, TM_FN // CH_FN, body, 0)
    o_ref[...] = _layer_norm(o_ref[...], g_ref[...], b_ref[...])

    @pl.when(i == n_tiles - 1)
    def _():
        wait_tile(nslot)


def _final(dest_flat, ys, w_tok, h1, shared, g, b, alpha):
    t = h1.shape[0]
    row = lambda i, d: (i, 0)
    const = lambda i, d: (0, 0)
    return pl.pallas_call(
        functools.partial(_final_kernel, alpha),
        grid_spec=pltpu.PrefetchScalarGridSpec(
            num_scalar_prefetch=1,
            grid=(t // TM_FN,),
            in_specs=[
                pl.BlockSpec(memory_space=pl.ANY),
                pl.BlockSpec((TM_FN, TOP_K), row),
                pl.BlockSpec((TM_FN, D_MODEL), row),
                pl.BlockSpec((TM_FN, D_MODEL), row),
                pl.BlockSpec((1, D_MODEL), const),
                pl.BlockSpec((1, D_MODEL), const),
            ],
            out_specs=pl.BlockSpec((TM_FN, D_MODEL), row),
            scratch_shapes=[pltpu.VMEM((2, TOP_K, TM_FN * ROW_TILE, LANES), U32), pltpu.SemaphoreType.DMA((2,))],
        ),
        out_shape=jax.ShapeDtypeStruct((t, D_MODEL), F32),
        compiler_params=_cparams(("arbitrary",)),
        name="combine_ln2",
    )(dest_flat, ys, w_tok, h1, shared, g, b)


TM_INB = 512
TN_INB = W_B // 2


def _pad_lanes(v):
    return jnp.pad(v, (0, LANES - v.shape[0]))[None, :].astype(F32)


def _layer(h2d, mem2d, batch, seq, w_in_all, layer, conv_w, conv_b, dt_bias, a_log, d_skip, ssd_norm_g, swa_sinks,
           rel_bias, w_mem_kv, w_ssd_o, w_swa_o, w_xa_o, w_out, ln1_g, ln1_b, router_w, router_bias, w1, w3, w2, ws1,
           ws3, ws2, ln2_g, ln2_b, alpha):
    t = batch * seq
    w_t = jnp.swapaxes(w_in_all, 1, 2)
    wdt = jnp.pad(w_t[layer, W_A:W_A + SSD_HEADS, :].T, ((0, 0), (0, LANES - SSD_HEADS))).astype(BF16)
    pa, dt_raw, xb = _inproj(h2d, _wcast(w_t, layer, 0, W_A, "wcast_a"), wdt)
    pb = _mm(xb, _wcast(w_t, layer, W_A + SSD_HEADS, W_B, "wcast_b"), BF16, TM_INB, TN_INB, "inproj_b", vmem_mb=56)

    y_ssd = _ssd(
        pa, dt_raw,
        conv_w[:, :SSD_D_INNER], conv_b[None, :SSD_D_INNER], conv_w[:, SSD_D_INNER:], conv_b[None, SSD_D_INNER:],
        _pad_lanes(dt_bias), _pad_lanes(a_log),
        jnp.repeat(d_skip, SSD_HEAD_DIM)[None, :], ssd_norm_g[None, :], batch, seq)

    y_swa = _swa(pb, swa_sinks.astype(F32), _swa_bias_table(rel_bias), batch, seq)

    memkv = _mm(mem2d, w_mem_kv, BF16, batch * MEM_LEN, 1024, "mem_kv")
    y_xa = _xa(pb, memkv, batch, seq)

    merged = _merge(y_ssd, y_swa, y_xa, w_ssd_o, w_swa_o, w_xa_o, pb)
    h1, hp = _out_ln(merged, w_out.astype(BF16), h2d, ln1_g[None, :], ln1_b[None, :], alpha)

    rwt = router_w.T
    r_hi = rwt.astype(BF16)
    r_lo = (rwt - r_hi.astype(F32)).astype(BF16)
    ti = jnp.arange(TM_RT)
    su = (ti[:, None] < ti[None, :]).astype(BF16)
    idx_t, w_t, rank_t, cnt = _router(h1, r_hi, r_lo, router_bias[:, None].astype(F32), su)

    counts = cnt[:, 0].astype(I32)
    padded = (counts + MOE_BLK - 1) // MOE_BLK * MOE_BLK
    pend = jnp.cumsum(padded)
    pstart = pend - padded
    n_blocks = t * TOP_K // MOE_BLK + N_EXPERTS
    blk_lo = jnp.arange(n_blocks, dtype=I32) * MOE_BLK
    block_e = jnp.minimum(jnp.sum((pend[None, :] <= blk_lo[:, None]).astype(I32), axis=1), N_EXPERTS - 1)
    of_e = block_e[:, None] == jnp.arange(N_EXPERTS, dtype=I32)[None, :]
    valid_end = jnp.sum(jnp.where(of_e, (pstart + counts)[None, :], 0), axis=1)
    block_nv = jnp.clip(valid_end - blk_lo, 0, MOE_BLK).astype(I32)
    eids = jnp.arange(N_EXPERTS, dtype=I32)
    later_active = (eids[None, :] > eids[:, None]) & (padded > 0)[None, :]
    next_active = jnp.min(jnp.where(later_active, eids[None, :], N_EXPERTS), axis=1)
    next_active = jnp.where(next_active >= N_EXPERTS, -1, next_active)
    seg_start = jnp.sum(jnp.where(of_e, pstart[None, :], 0), axis=1)
    block_first = ((blk_lo == seg_start) & (block_nv > 0)).astype(I32)
    block_next = jnp.sum(jnp.where(of_e, next_active[None, :], 0), axis=1).astype(I32)

    dest_flat = _dest(idx_t, rank_t, pstart.astype(F32)[:, None]).reshape(-1)
    n_slots = n_blocks * MOE_BLK
    x_sorted, shared = _dispatch(dest_flat, hp, h1, ws1.astype(BF16), ws3.astype(BF16), ws2.astype(BF16),
                                 n_slots)
    ys = _experts(block_e, block_nv, block_first, block_next,
                  x_sorted.reshape(n_slots * ROW_TILE, LANES), w1, w3, w2)
    return _final(dest_flat, ys.reshape(n_slots, ROW_TILE, LANES), w_t.T, h1, shared, ln2_g[None, :],
                  ln2_b[None, :], alpha)


def kernel(x, mem, w_in, conv_w, conv_b, dt_bias, a_log, d_skip, ssd_norm_g, swa_sinks, rel_bias, w_mem_kv, w_ssd_o, w_swa_o, w_xa_o, w_out, ln1_g, ln1_b, router_w, router_bias, w1, w3, w2, ws1, ws3, ws2, ln2_g, ln2_b):
    batch, seq, d = x.shape
    depth = w_in.shape[0]
    alpha = (2.0 * depth) ** 0.25
    h = x.reshape(batch * seq, d)
    mem2d = mem.reshape(batch * MEM_LEN, d)
    for i in range(depth):
        h = _layer(h, mem2d, batch, seq, w_in, i, conv_w[i], conv_b[i], dt_bias[i], a_log[i], d_skip[i],
                   ssd_norm_g[i], swa_sinks[i], rel_bias, w_mem_kv[i], w_ssd_o[i], w_swa_o[i], w_xa_o[i], w_out[i],
                   ln1_g[i], ln1_b[i], router_w[i], router_bias[i], w1[i], w3[i], w2[i], ws1[i], ws3[i], ws2[i],
                   ln2_g[i], ln2_b[i], alpha)
    return h.reshape(batch, seq, d)
```

```python
import functools
import math

import jax
import jax.numpy as jnp
from jax import lax
from jax.experimental import pallas as pl
from jax.experimental.pallas import tpu as pltpu

F32 = jnp.float32
BF16 = jnp.bfloat16
I32 = jnp.int32
U32 = jnp.uint32

D_MODEL = 2048
MEM_LEN = 256
SSD_D_INNER = D_MODEL
SSD_HEAD_DIM = 64
SSD_HEADS = 32
SSD_GROUPS = 4
SSD_STATE = 128
SSD_CONV = 4
SSD_CHUNK = 128
SWA_HEADS = 16
SWA_KV_HEADS = 4
SWA_HEAD_DIM = 64
SWA_WINDOW = 128
SWA_BLOCK = 128
REL_BUCKETS = 32
REL_MAX_DIST = 128
XA_HEADS = 4
XA_HEAD_DIM = 256
N_BRANCH = 3
N_EXPERTS = 64
TOP_K = 8
N_EXPERT_GROUPS = 8
TOPK_GROUPS = 4
EXPERT_DIM = 512
SHARED_DIM = 512
ROUTED_SCALE = 2.5
LN_EPS = 1e-5
RMS_EPS = 1e-5

SWA_Q_DIM = SWA_HEADS * SWA_HEAD_DIM
SWA_KV_DIM = SWA_KV_HEADS * SWA_HEAD_DIM
XA_DIM = XA_HEADS * XA_HEAD_DIM
BC_DIM = 2 * SSD_GROUPS * SSD_STATE

W_A = SSD_D_INNER + SSD_D_INNER + BC_DIM
OFFA_Z = 0
OFFA_XS = OFFA_Z + SSD_D_INNER
OFFA_BC = OFFA_XS + SSD_D_INNER
W_B = SWA_Q_DIM + 2 * SWA_KV_DIM + XA_DIM + N_BRANCH * D_MODEL
OFFB_Q = 0
OFFB_K = OFFB_Q + SWA_Q_DIM
OFFB_V = OFFB_K + SWA_KV_DIM
OFFB_QX = OFFB_V + SWA_KV_DIM
OFFB_GATES = OFFB_QX + XA_DIM

LANES = 128
MOE_BLK = 256
NEG_INF = float("-inf")


def _cparams(sem, vmem_mb=48):
    return pltpu.CompilerParams(dimension_semantics=sem, vmem_limit_bytes=vmem_mb * 1024 * 1024)


def _dot(a, b):
    return jnp.dot(a, b, preferred_element_type=F32)


def _dot_nt(a, b):
    return lax.dot_general(a, b, (((1,), (1,)), ((), ())), preferred_element_type=F32)


def _sigmoid(x):
    return 1.0 / (1.0 + jnp.exp(-x))


def _silu(x):
    return x * _sigmoid(x)


TM_IN = 512
TN_IN = W_A // 2


def _inproj_kernel(x_ref, w_ref, wdt_ref, p_ref, dt_ref, xb_ref):
    @pl.when(pl.program_id(1) == 0)
    def _():
        xb = x_ref[...].astype(BF16)
        xb_ref[...] = xb
        dt_ref[...] = _dot(xb, wdt_ref[...])

    p_ref[...] = _dot(xb_ref[...], w_ref[...].astype(BF16)).astype(BF16)


def _inproj(x2d, w_in, wdt):
    t = x2d.shape[0]
    return pl.pallas_call(
        _inproj_kernel,
        grid=(t // TM_IN, W_A // TN_IN),
        in_specs=[
            pl.BlockSpec((TM_IN, D_MODEL), lambda i, j: (i, 0)),
            pl.BlockSpec((D_MODEL, TN_IN), lambda i, j: (0, j)),
            pl.BlockSpec((D_MODEL, LANES), lambda i, j: (0, 0)),
        ],
        out_specs=[
            pl.BlockSpec((TM_IN, TN_IN), lambda i, j: (i, j)),
            pl.BlockSpec((TM_IN, LANES), lambda i, j: (i, 0)),
            pl.BlockSpec((TM_IN, D_MODEL), lambda i, j: (i, 0)),
        ],
        out_shape=[jax.ShapeDtypeStruct((t, W_A), BF16), jax.ShapeDtypeStruct((t, LANES), F32),
                   jax.ShapeDtypeStruct((t, D_MODEL), BF16)],
        compiler_params=_cparams(("parallel", "arbitrary"), vmem_mb=56),
        name="inproj",
    )(x2d, w_in, wdt)


TN_WC = 512
ROW_ALIGN = 32


def _wcast_kernel(a_ref, o_ref):
    o_ref[...] = a_ref[0].T.astype(BF16)


def _wcast(w_t, layer, start, width, name):
    k = w_t.shape[2]
    return pl.pallas_call(
        _wcast_kernel,
        grid=(width // TN_WC,),
        in_specs=[pl.BlockSpec((pl.Element(1), pl.Element(TN_WC), pl.Element(k)),
                               lambda j: (layer, pl.multiple_of(start + j * TN_WC, ROW_ALIGN), 0))],
        out_specs=pl.BlockSpec((k, TN_WC), lambda j: (0, j)),
        out_shape=jax.ShapeDtypeStruct((k, width), BF16),
        compiler_params=_cparams(("parallel",)),
        name=name,
    )(w_t)


def _mm_kernel(a_ref, b_ref, o_ref):
    o_ref[...] = _dot(a_ref[...].astype(BF16), b_ref[...].astype(BF16)).astype(o_ref.dtype)


def _mm(a, b, out_dtype, tm, tn, name, vmem_mb=48):
    m, k = a.shape
    n = b.shape[1]
    return pl.pallas_call(
        _mm_kernel,
        grid=(m // tm, n // tn),
        in_specs=[pl.BlockSpec((tm, k), lambda i, j: (i, 0)), pl.BlockSpec((k, tn), lambda i, j: (0, j))],
        out_specs=pl.BlockSpec((tm, tn), lambda i, j: (i, j)),
        out_shape=jax.ShapeDtypeStruct((m, n), out_dtype),
        compiler_params=_cparams(("parallel", "parallel"), vmem_mb=vmem_mb),
        name=name,
    )(a, b)


Q = SSD_CHUNK


def _conv_shift_matrix():
    t = jnp.arange(Q)[:, None]
    j = jnp.arange(2 * Q)[None, :]
    blocks = []
    for k in range(SSD_CONV - 1):
        src = t - (SSD_CONV - 1 - k)
        blocks.append(j == jnp.where(src >= 0, src, 2 * Q + src))
    return jnp.concatenate(blocks, axis=0).astype(BF16)


def _ssd_kernel(xs_ref, bc_ref, z_ref, dtr_ref, shift_ref, cwx_ref, cbx_ref, cwb_ref, cbb_ref, dtb_ref, alog_ref,
                dsk_ref, ng_ref, o_ref, hx_ref, hb_ref, st_ref, cumt_ref, dtt_ref, wt_ref, y_ref):
    c = pl.program_id(1)

    @pl.when(c == 0)
    def _():
        hx_ref[...] = jnp.zeros_like(hx_ref)
        hb_ref[...] = jnp.zeros_like(hb_ref)
        st_ref[...] = jnp.zeros_like(st_ref)

    def conv_silu(in_ref, prev_ref, w_ref, b_ref):
        cur = in_ref[...]
        ext = jnp.concatenate([cur, prev_ref[...]], axis=0)
        shifted = _dot(shift_ref[...], ext)
        acc = b_ref[...] + w_ref[SSD_CONV - 1:SSD_CONV, :] * cur.astype(F32)
        for k in range(SSD_CONV - 1):
            acc = acc + w_ref[k:k + 1, :] * shifted[k * Q:(k + 1) * Q, :]
        prev_ref[...] = cur
        return _silu(acc)

    xs_c = conv_silu(xs_ref, hx_ref, cwx_ref, cbx_ref)
    bc_c = conv_silu(bc_ref, hb_ref, cwb_ref, cbb_ref)

    lane = lax.broadcasted_iota(I32, (Q, LANES), 1)
    row = lax.broadcasted_iota(I32, (Q, LANES), 0)
    tril = row >= lane
    lo_mask = lane < SSD_HEAD_DIM

    dtv = dtr_ref[...] + dtb_ref[...]
    dt = jnp.maximum(dtv, 0.0) + jnp.log1p(jnp.exp(-jnp.abs(dtv)))
    a = jnp.where(lane[0:1, :] < SSD_HEADS, -jnp.exp(alog_ref[...]), 0.0)
    la = dt * a
    tri = jnp.where(tril, 1.0, 0.0).astype(BF16)
    p1 = la.astype(BF16)
    r1 = la - p1.astype(F32)
    p2 = r1.astype(BF16)
    p3 = (r1 - p2.astype(F32)).astype(BF16)
    cum = _dot(tri, p1) + _dot(tri, p2) + _dot(tri, p3)
    ecum = jnp.exp(cum)
    cum_t = cum.T
    dt_t = dt.T
    last = cum_t[:, Q - 1:Q]
    cumt_ref[...] = cum_t
    dtt_ref[...] = dt_t
    wt_ref[...] = jnp.exp(last - cum_t) * dt_t
    chunk_decay = jnp.exp(last)

    def head_parts(h, cb, cm, bm_t):
        col = jnp.broadcast_to(cum[:, h:h + 1], (Q, Q))
        rw = cumt_ref[h:h + 1, :]
        dec = jnp.exp(jnp.where(tril, col - rw, NEG_INF))
        m = cb * (dec * dtt_ref[h:h + 1, :])
        ce = cm * jnp.broadcast_to(ecum[:, h:h + 1], (Q, Q))
        sl = bm_t * wt_ref[h:h + 1, :]
        return m.astype(BF16), ce.astype(BF16), sl.astype(BF16)

    for g in range(SSD_GROUPS):
        bm = bc_c[:, g * SSD_STATE:(g + 1) * SSD_STATE]
        cm = bc_c[:, BC_DIM // 2 + g * SSD_STATE:BC_DIM // 2 + (g + 1) * SSD_STATE]
        cb = _dot_nt(cm.astype(BF16), bm.astype(BF16))
        bm_t = bm.T
        for qd in range(SSD_HEADS // SSD_GROUPS // 2):
            hp = g * (SSD_HEADS // SSD_GROUPS // 2) + qd
            ha, hb = 2 * hp, 2 * hp + 1
            sl_ = slice(hp * LANES, (hp + 1) * LANES)
            xs_pair = xs_c[:, sl_]
            x_lo = jnp.where(lo_mask, xs_pair, 0.0).astype(BF16)
            x_hi = jnp.where(lo_mask, 0.0, xs_pair).astype(BF16)
            h_prev = st_ref[:, sl_]
            h_lo = jnp.where(lo_mask, h_prev, 0.0).astype(BF16)
            h_hi = jnp.where(lo_mask, 0.0, h_prev).astype(BF16)
            m_a, ce_a, sl_a = head_parts(ha, cb, cm, bm_t)
            m_b, ce_b, sl_b = head_parts(hb, cb, cm, bm_t)
            lhs = jnp.concatenate([m_a, ce_a, m_b, ce_b], axis=1)
            rhs = jnp.concatenate([x_lo, h_lo, x_hi, h_hi], axis=0)
            y_ref[:, sl_] = _dot(lhs, rhs)
            new = _dot(jnp.concatenate([sl_a, sl_b], axis=1), jnp.concatenate([x_lo, x_hi], axis=0))
            cd = jnp.where(lo_mask[0:1, :], chunk_decay[ha:ha + 1, :], chunk_decay[hb:hb + 1, :])
            st_ref[:, sl_] = h_prev * cd + new

    y = y_ref[...] + xs_c * dsk_ref[...]
    y = y * _silu(z_ref[...].astype(F32))
    gw = SSD_D_INNER // SSD_GROUPS
    for g in range(SSD_GROUPS):
        yg = y[:, g * gw:(g + 1) * gw]
        ms = jnp.mean(yg * yg, axis=-1, keepdims=True)
        o_ref[:, g * gw:(g + 1) * gw] = ((yg * lax.rsqrt(ms + RMS_EPS)) * ng_ref[:, g * gw:(g + 1) * gw]).astype(BF16)


def _ssd(p, dt_raw, cwx, cbx, cwb, cbb, dtb, alog, dsk, ng, batch, seq):
    nc = seq // Q
    t = batch * seq
    rowblk = lambda b, c: b * nc + c
    const = lambda b, c: (0, 0)
    return pl.pallas_call(
        _ssd_kernel,
        grid=(batch, nc),
        in_specs=[
            pl.BlockSpec((Q, SSD_D_INNER), lambda b, c: (rowblk(b, c), OFFA_XS // SSD_D_INNER)),
            pl.BlockSpec((Q, BC_DIM), lambda b, c: (rowblk(b, c), OFFA_BC // BC_DIM)),
            pl.BlockSpec((Q, SSD_D_INNER), lambda b, c: (rowblk(b, c), OFFA_Z // SSD_D_INNER)),
            pl.BlockSpec((Q, LANES), lambda b, c: (rowblk(b, c), 0)),
            pl.BlockSpec(((SSD_CONV - 1) * Q, 2 * Q), const),
            pl.BlockSpec((SSD_CONV, SSD_D_INNER), const),
            pl.BlockSpec((1, SSD_D_INNER), const),
            pl.BlockSpec((SSD_CONV, BC_DIM), const),
            pl.BlockSpec((1, BC_DIM), const),
            pl.BlockSpec((1, LANES), const),
            pl.BlockSpec((1, LANES), const),
            pl.BlockSpec((1, SSD_D_INNER), const),
            pl.BlockSpec((1, SSD_D_INNER), const),
        ],
        out_specs=pl.BlockSpec((Q, SSD_D_INNER), lambda b, c: (rowblk(b, c), 0)),
        out_shape=jax.ShapeDtypeStruct((t, SSD_D_INNER), BF16),
        scratch_shapes=[
            pltpu.VMEM((Q, SSD_D_INNER), BF16),
            pltpu.VMEM((Q, BC_DIM), BF16),
            pltpu.VMEM((SSD_STATE, SSD_D_INNER), F32),
            pltpu.VMEM((LANES, Q), F32),
            pltpu.VMEM((LANES, Q), F32),
            pltpu.VMEM((LANES, Q), F32),
            pltpu.VMEM((Q, SSD_D_INNER), F32),
        ],
        compiler_params=_cparams(("parallel", "arbitrary")),
        name="ssd",
    )(p, p, p, dt_raw, _conv_shift_matrix(), cwx, cbx, cwb, cbb, dtb, alog, dsk, ng)


N_PAIRS = SWA_HEADS // 2


def _swa_kernel(sink_ref, q_ref, kp_ref, kc_ref, vp_ref, vc_ref, bias_ref, o_ref):
    n = pl.program_id(1)
    blk = SWA_BLOCK
    kband = jnp.concatenate([kp_ref[...], kc_ref[...]], axis=0)
    vband = jnp.concatenate([vp_ref[...], vc_ref[...]], axis=0)

    lane = lax.broadcasted_iota(I32, (1, LANES), 1)
    lo_f = jnp.where(lane < SWA_HEAD_DIM, 1.0, 0.0)
    lo_m = lo_f.astype(BF16)
    hi_m = (1.0 - lo_f).astype(BF16)

    def head_twice(band, kv):
        tile = band[:, (kv // 2) * LANES:(kv // 2 + 1) * LANES]
        swapped = jnp.concatenate([tile[:, SWA_HEAD_DIM:], tile[:, :SWA_HEAD_DIM]], axis=1)
        return tile * lo_m + swapped * hi_m if kv % 2 == 0 else swapped * lo_m + tile * hi_m

    k_twice = [head_twice(kband, kv) for kv in range(SWA_KV_HEADS)]
    v_twice = [head_twice(vband, kv) for kv in range(SWA_KV_HEADS)]
    colj = lax.broadcasted_iota(I32, (2 * blk, 2 * blk), 1)
    rowi = lax.broadcasted_iota(I32, (2 * blk, 1), 0)
    valid = (colj >= blk) | (n > 0)
    scale = SWA_HEAD_DIM ** -0.5
    for p in range(N_PAIRS):
        kv = p // (SWA_HEADS // SWA_KV_HEADS // 2)
        qp = q_ref[:, p * LANES:(p + 1) * LANES]
        qs = jnp.concatenate([qp * lo_m, qp * hi_m], axis=0)
        s = _dot_nt(qs, k_twice[kv]) * scale + bias_ref[p]
        s = jnp.where(valid, s, NEG_INF)
        sink = jnp.where(rowi < blk, sink_ref[2 * p], sink_ref[2 * p + 1])
        m = jnp.maximum(jnp.max(s, axis=-1, keepdims=True), sink)
        e = jnp.exp(s - m)
        den = jnp.sum(e, axis=-1, keepdims=True) + jnp.exp(sink - m)
        pr = (e * (1.0 / den)).astype(BF16)
        vd = v_twice[kv]
        lhs = jnp.concatenate([pr[:blk], pr[blk:]], axis=1)
        rhs = jnp.concatenate([vd * lo_m, vd * hi_m], axis=0)
        o_ref[:, p * LANES:(p + 1) * LANES] = _dot(lhs, rhs).astype(BF16)


def _swa(p, sinks, bias, batch, seq):
    nb = seq // SWA_BLOCK
    t = batch * seq
    kw = SWA_KV_DIM
    cur = lambda b, n, s: b * nb + n
    prev = lambda b, n, s: b * nb + jnp.maximum(n - 1, 0)
    return pl.pallas_call(
        _swa_kernel,
        grid_spec=pltpu.PrefetchScalarGridSpec(
            num_scalar_prefetch=1,
            grid=(batch, nb),
            in_specs=[
                pl.BlockSpec((SWA_BLOCK, SWA_Q_DIM), lambda b, n, s: (cur(b, n, s), OFFB_Q // SWA_Q_DIM)),
                pl.BlockSpec((SWA_BLOCK, kw), lambda b, n, s: (prev(b, n, s), OFFB_K // kw)),
                pl.BlockSpec((SWA_BLOCK, kw), lambda b, n, s: (cur(b, n, s), OFFB_K // kw)),
                pl.BlockSpec((SWA_BLOCK, kw), lambda b, n, s: (prev(b, n, s), OFFB_V // kw)),
                pl.BlockSpec((SWA_BLOCK, kw), lambda b, n, s: (cur(b, n, s), OFFB_V // kw)),
                pl.BlockSpec((N_PAIRS, 2 * SWA_BLOCK, 2 * SWA_BLOCK), lambda b, n, s: (0, 0, 0)),
            ],
            out_specs=pl.BlockSpec((SWA_BLOCK, SWA_Q_DIM), lambda b, n, s: (cur(b, n, s), 0)),
        ),
        out_shape=jax.ShapeDtypeStruct((t, SWA_Q_DIM), BF16),
        compiler_params=_cparams(("parallel", "arbitrary")),
        name="swa",
    )(sinks, p, p, p, p, p, bias)


def _t5_causal_bucket(dist):
    max_exact = REL_BUCKETS // 2
    large = max_exact + (jnp.log(jnp.maximum(dist, 1).astype(F32) / max_exact)
                         / math.log(REL_MAX_DIST / max_exact) * (REL_BUCKETS - max_exact)).astype(I32)
    large = jnp.minimum(large, REL_BUCKETS - 1)
    return jnp.where(dist < max_exact, dist, large)


def _swa_bias_table(rel_bias):
    qi = jnp.arange(SWA_BLOCK)[:, None]
    kj = jnp.arange(2 * SWA_BLOCK)[None, :]
    dist = qi + SWA_BLOCK - kj
    in_window = (dist >= 0) & (dist < SWA_WINDOW)
    bucket = _t5_causal_bucket(jnp.maximum(dist, 0))
    onehot = (bucket[:, :, None] == jnp.arange(REL_BUCKETS)[None, None, :]).astype(F32)
    bias = jnp.einsum('ijb,bh->hij', onehot, rel_bias.astype(F32), precision=lax.Precision.HIGHEST)
    bias = jnp.where(in_window[None, :, :], bias, NEG_INF)
    return bias.reshape(N_PAIRS, 2 * SWA_BLOCK, 2 * SWA_BLOCK)


TL_XA = 512


XA_HALF = XA_DIM // 2


def _xa_kernel(qa_ref, qb_ref, mk_ref, mv_ref, o_ref):
    for h in range(XA_HEADS):
        q_ref = qa_ref if h < XA_HEADS // 2 else qb_ref
        qc = (h % (XA_HEADS // 2)) * XA_HEAD_DIM
        cols = slice(h * XA_HEAD_DIM, (h + 1) * XA_HEAD_DIM)
        s = _dot_nt(q_ref[:, qc:qc + XA_HEAD_DIM], mk_ref[:, cols]) * (XA_HEAD_DIM ** -0.5)
        m = jnp.max(s, axis=-1, keepdims=True)
        e = jnp.exp(s - m)
        pr = e * (1.0 / jnp.sum(e, axis=-1, keepdims=True))
        o_ref[:, cols] = _dot(pr.astype(BF16), mv_ref[:, cols]).astype(BF16)


def _xa(p, memkv, batch, seq):
    nl = seq // TL_XA
    t = batch * seq
    return pl.pallas_call(
        _xa_kernel,
        grid=(batch, nl),
        in_specs=[
            pl.BlockSpec((TL_XA, XA_HALF), lambda b, l: (b * nl + l, OFFB_QX // XA_HALF)),
            pl.BlockSpec((TL_XA, XA_HALF), lambda b, l: (b * nl + l, OFFB_QX // XA_HALF + 1)),
            pl.BlockSpec((MEM_LEN, XA_DIM), lambda b, l: (b, 0)),
            pl.BlockSpec((MEM_LEN, XA_DIM), lambda b, l: (b, 1)),
        ],
        out_specs=pl.BlockSpec((TL_XA, XA_DIM), lambda b, l: (b * nl + l, 0)),
        out_shape=jax.ShapeDtypeStruct((t, XA_DIM), BF16),
        compiler_params=_cparams(("parallel", "parallel")),
        name="xattn",
    )(p, p, memkv, memkv)


TM_MG = 1024
TN_MG = 512


N_SUB_MG = 2


def _merge_kernel(ys_ref, yw_ref, yx_ref, ws_ref, ww_ref, wx_ref, g0_ref, g1_ref, g2_ref, o_ref):
    sub = TM_MG // N_SUB_MG
    ws = ws_ref[...].astype(BF16)
    ww = ww_ref[...].astype(BF16)
    wx = wx_ref[...].astype(BF16)
    for s in range(N_SUB_MG):
        rows = slice(s * sub, (s + 1) * sub)
        a = _dot(ys_ref[rows, :], ws)
        b = _dot(yw_ref[rows, :], ww)
        c = _dot(yx_ref[rows, :], wx)
        g0 = _sigmoid(g0_ref[rows, :].astype(F32))
        g1 = _sigmoid(g1_ref[rows, :].astype(F32))
        g2 = _sigmoid(g2_ref[rows, :].astype(F32))
        o_ref[rows, :] = (g0 * a + g1 * b + g2 * c).astype(BF16)


def _merge(y_ssd, y_swa, y_xa, w_ssd_o, w_swa_o, w_xa_o, p):
    t = y_ssd.shape[0]
    gpb = D_MODEL // TN_MG
    gate = lambda k: pl.BlockSpec((TM_MG, TN_MG), lambda i, j: (i, OFFB_GATES // TN_MG + k * gpb + j))
    return pl.pallas_call(
        _merge_kernel,
        grid=(t // TM_MG, D_MODEL // TN_MG),
        in_specs=[
            pl.BlockSpec((TM_MG, SSD_D_INNER), lambda i, j: (i, 0)),
            pl.BlockSpec((TM_MG, SWA_Q_DIM), lambda i, j: (i, 0)),
            pl.BlockSpec((TM_MG, XA_DIM), lambda i, j: (i, 0)),
            pl.BlockSpec((SSD_D_INNER, TN_MG), lambda i, j: (0, j)),
            pl.BlockSpec((SWA_Q_DIM, TN_MG), lambda i, j: (0, j)),
            pl.BlockSpec((XA_DIM, TN_MG), lambda i, j: (0, j)),
            gate(0), gate(1), gate(2),
        ],
        out_specs=pl.BlockSpec((TM_MG, TN_MG), lambda i, j: (i, j)),
        out_shape=jax.ShapeDtypeStruct((t, D_MODEL), BF16),
        compiler_params=_cparams(("parallel", "parallel")),
        name="merge",
    )(y_ssd, y_swa, y_xa, w_ssd_o, w_swa_o, w_xa_o, p, p, p)


TM_LN = 512


def _layer_norm(v, g, b):
    mu = jnp.mean(v, axis=-1, keepdims=True)
    d = v - mu
    var = jnp.mean(d * d, axis=-1, keepdims=True)
    return (d * lax.rsqrt(var + LN_EPS)) * g + b


def _pack_halves(v):
    n = v.shape[1] // 2
    lo = lax.bitcast_convert_type(v[:, :n].astype(BF16).astype(F32), U32) >> 16
    hi = lax.bitcast_convert_type(v[:, n:].astype(BF16).astype(F32), U32) & jnp.uint32(0xFFFF0000)
    return hi | lo


def _unpack_halves(w):
    lo = lax.bitcast_convert_type(w << 16, F32)
    hi = lax.bitcast_convert_type(w & jnp.uint32(0xFFFF0000), F32)
    return lo, hi


ROW_WORDS = D_MODEL // 2
ROW_TILE = ROW_WORDS // LANES


def _store_row_tiled(ref, words):
    m = words.shape[0]
    for c in range(ROW_TILE):
        ref[pl.ds(c, m, stride=ROW_TILE), :] = words[:, c * LANES:(c + 1) * LANES]


def _load_row_tiled(ref, m):
    return jnp.concatenate([ref[pl.ds(c, m, stride=ROW_TILE), :] for c in range(ROW_TILE)], axis=1)


N_SUB_LN = 4


def _out_ln_kernel(alpha, m_ref, w_ref, x_ref, g_ref, b_ref, h_ref, hp_ref):
    sub = TM_LN // N_SUB_LN
    for c in range(N_SUB_LN):
        rows = slice(c * sub, (c + 1) * sub)
        mix = _dot(m_ref[rows, :], w_ref[...])
        hn = _layer_norm(alpha * x_ref[rows, :] + mix, g_ref[...], b_ref[...])
        h_ref[rows, :] = hn
        _store_row_tiled(hp_ref.at[pl.ds(c * sub * ROW_TILE, sub * ROW_TILE), :], _pack_halves(hn))


def _out_ln(merged, w_out, x2d, g, b, alpha):
    t = x2d.shape[0]
    row = lambda i: (i, 0)
    const = lambda i: (0, 0)
    return pl.pallas_call(
        functools.partial(_out_ln_kernel, alpha),
        grid=(t // TM_LN,),
        in_specs=[
            pl.BlockSpec((TM_LN, D_MODEL), row),
            pl.BlockSpec((D_MODEL, D_MODEL), const),
            pl.BlockSpec((TM_LN, D_MODEL), row),
            pl.BlockSpec((1, D_MODEL), const),
            pl.BlockSpec((1, D_MODEL), const),
        ],
        out_specs=[pl.BlockSpec((TM_LN, D_MODEL), row), pl.BlockSpec((TM_LN * ROW_TILE, LANES), row)],
        out_shape=[jax.ShapeDtypeStruct((t, D_MODEL), F32), jax.ShapeDtypeStruct((t * ROW_TILE, LANES), U32)],
        compiler_params=_cparams(("parallel",)),
        name="out_ln1",
    )(merged, w_out, x2d, g, b)


TM_RT = 512
EPG = N_EXPERTS // N_EXPERT_GROUPS


def _router_kernel(h_ref, whi_ref, wlo_ref, bias_ref, su_ref, idx_ref, w_ref, rank_ref, cnt_ref, run_ref):
    @pl.when(pl.program_id(0) == 0)
    def _():
        run_ref[...] = jnp.zeros_like(run_ref)

    h = h_ref[...]
    h_hi = h.astype(BF16)
    h_lo = (h - h_hi.astype(F32)).astype(BF16)
    whi = whi_ref[...]
    logits = _dot_nt(whi, h_hi) + _dot_nt(whi, h_lo) + _dot_nt(wlo_ref[...], h_hi)
    scores = _sigmoid(logits)
    sel = scores + bias_ref[...]

    iota_m = lax.broadcasted_iota(I32, (EPG, TM_RT), 0).astype(F32)
    iota_g = lax.broadcasted_iota(I32, (N_EXPERT_GROUPS, TM_RT), 0).astype(F32)
    gs = []
    for g in range(N_EXPERT_GROUPS):
        v = sel[g * EPG:(g + 1) * EPG, :]
        m1 = jnp.max(v, axis=0, keepdims=True)
        i1 = jnp.min(jnp.where(v == m1, iota_m, float(EPG)), axis=0, keepdims=True)
        m2 = jnp.max(jnp.where(iota_m == i1, NEG_INF, v), axis=0, keepdims=True)
        gs.append(m1 + m2)
    gsc = jnp.concatenate(gs, axis=0)
    chosen = jnp.zeros((N_EXPERT_GROUPS, TM_RT), F32)
    for _ in range(TOPK_GROUPS):
        m = jnp.max(gsc, axis=0, keepdims=True)
        ig = jnp.min(jnp.where(gsc == m, iota_g, float(N_EXPERT_GROUPS)), axis=0, keepdims=True)
        hit = iota_g == ig
        chosen = jnp.where(hit, 1.0, chosen)
        gsc = jnp.where(hit, NEG_INF, gsc)
    gmask = jnp.concatenate(
        [jnp.broadcast_to(chosen[g:g + 1, :], (EPG, TM_RT)) for g in range(N_EXPERT_GROUPS)], axis=0)
    masked = jnp.where(gmask > 0.0, sel, NEG_INF)

    iota_e = lax.broadcasted_iota(I32, (N_EXPERTS, TM_RT), 0).astype(F32)
    hits, idxs, ws = [], [], []
    for _ in range(TOP_K):
        m = jnp.max(masked, axis=0, keepdims=True)
        ie = jnp.min(jnp.where(masked == m, iota_e, float(N_EXPERTS)), axis=0, keepdims=True)
        hit = iota_e == ie
        hits.append(hit)
        idxs.append(ie)
        ws.append(jnp.sum(jnp.where(hit, scores, 0.0), axis=0, keepdims=True))
        masked = jnp.where(hit, NEG_INF, masked)
    wsum = ws[0]
    for k in range(1, TOP_K):
        wsum = wsum + ws[k]
    inv = 1.0 / wsum
    idx_ref[...] = jnp.concatenate(idxs, axis=0).astype(I32)
    w_ref[...] = jnp.concatenate([w * inv * ROUTED_SCALE for w in ws], axis=0)

    assign = jnp.zeros((N_EXPERTS, TM_RT), F32)
    for hit in hits:
        assign = jnp.where(hit, 1.0, assign)
    pos = _dot(assign.astype(BF16), su_ref[...]) + run_ref[...]
    rank_ref[...] = jnp.concatenate(
        [jnp.sum(jnp.where(hit, pos, 0.0), axis=0, keepdims=True) for hit in hits], axis=0).astype(I32)
    run = run_ref[...] + jnp.sum(assign, axis=1, keepdims=True)
    run_ref[...] = run
    cnt_ref[...] = run


def _router(h1, r_hi, r_lo, rbias, su):
    t = h1.shape[0]
    tile = lambda i: (0, i)
    const = lambda i: (0, 0)
    return pl.pallas_call(
        _router_kernel,
        grid=(t // TM_RT,),
        in_specs=[
            pl.BlockSpec((TM_RT, D_MODEL), lambda i: (i, 0)),
            pl.BlockSpec((N_EXPERTS, D_MODEL), const),
            pl.BlockSpec((N_EXPERTS, D_MODEL), const),
            pl.BlockSpec((N_EXPERTS, 1), const),
            pl.BlockSpec((TM_RT, TM_RT), const),
        ],
        out_specs=[
            pl.BlockSpec((TOP_K, TM_RT), tile),
            pl.BlockSpec((TOP_K, TM_RT), tile),
            pl.BlockSpec((TOP_K, TM_RT), tile),
            pl.BlockSpec((N_EXPERTS, TM_RT), const),
        ],
        out_shape=[
            jax.ShapeDtypeStruct((TOP_K, t), I32),
            jax.ShapeDtypeStruct((TOP_K, t), F32),
            jax.ShapeDtypeStruct((TOP_K, t), I32),
            jax.ShapeDtypeStruct((N_EXPERTS, TM_RT), F32),
        ],
        scratch_shapes=[pltpu.VMEM((N_EXPERTS, TM_RT), F32)],
        compiler_params=_cparams(("arbitrary",)),
        name="router",
    )(h1, r_hi, r_lo, rbias, su)


TM_DS = 2048


def _dest_kernel(idx_ref, rank_ref, ps_ref, d_ref):
    iota_e = lax.broadcasted_iota(I32, (N_EXPERTS, TM_DS), 0)
    ps = ps_ref[...]
    rows = []
    for k in range(TOP_K):
        hit = iota_e == idx_ref[k:k + 1, :]
        rows.append(jnp.sum(jnp.where(hit, ps, 0.0), axis=0, keepdims=True))
    d_ref[...] = jnp.concatenate(rows, axis=0).astype(I32) + rank_ref[...]


def _dest(idx_t, rank_t, pstart):
    t = idx_t.shape[1]
    tile = lambda i: (0, i)
    return pl.pallas_call(
        _dest_kernel,
        grid=(t // TM_DS,),
        in_specs=[pl.BlockSpec((TOP_K, TM_DS), tile), pl.BlockSpec((TOP_K, TM_DS), tile),
                  pl.BlockSpec((N_EXPERTS, 1), lambda i: (0, 0))],
        out_specs=pl.BlockSpec((TOP_K, TM_DS), tile),
        out_shape=jax.ShapeDtypeStruct((TOP_K, t), I32),
        compiler_params=_cparams(("parallel",)),
        name="dest",
    )(idx_t, rank_t, pstart)


TM_DP = 512


def _dispatch_kernel(dest_ref, hp_ref, h_ref, ws1_ref, ws3_ref, ws2_ref, xs_ref, sh_ref, sem):
    t_total = pl.num_programs(0) * TM_DP
    base = pl.program_id(0) * TM_DP

    def issue(t, carry):
        src = hp_ref.at[pl.ds(pl.multiple_of(t * ROW_TILE, ROW_TILE), ROW_TILE), :]
        for k in range(TOP_K):
            slot = dest_ref[k * t_total + base + t]
            pltpu.make_async_copy(src, xs_ref.at[slot], sem).start(priority=k % 2)
        return carry

    n_piece = 8
    per = TM_DP // n_piece
    batches = iter(range(n_piece))

    def issue_batch():
        lo = next(batches) * per
        lax.fori_loop(lo, lo + per, issue, 0)

    xb = h_ref[...].astype(BF16)
    hh = SHARED_DIM // 2
    issue_batch()
    a0 = _dot(xb, ws1_ref[:, :hh])
    issue_batch()
    a1 = _dot(xb, ws1_ref[:, hh:])
    issue_batch()
    g0 = _dot(xb, ws3_ref[:, :hh])
    issue_batch()
    g1 = _dot(xb, ws3_ref[:, hh:])
    hdn = jnp.concatenate([_silu(a0) * g0, _silu(a1) * g1], axis=1).astype(BF16)
    qn = D_MODEL // 4
    for c in range(4):
        issue_batch()
        sh_ref[:, c * qn:(c + 1) * qn] = _dot(hdn, ws2_ref[:, c * qn:(c + 1) * qn])

    for k in range(TOP_K):
        pltpu.make_async_copy(hp_ref.reshape(TM_DP, ROW_TILE, LANES), xs_ref.at[pl.ds(0, TM_DP)], sem).wait()


def _dispatch(dest_flat, hp, h1, ws1, ws3, ws2, n_slots):
    t = h1.shape[0]
    row = lambda i, d: (i, 0)
    const = lambda i, d: (0, 0)
    return pl.pallas_call(
        _dispatch_kernel,
        grid_spec=pltpu.PrefetchScalarGridSpec(
            num_scalar_prefetch=1,
            grid=(t // TM_DP,),
            in_specs=[
                pl.BlockSpec((TM_DP * ROW_TILE, LANES), row),
                pl.BlockSpec((TM_DP, D_MODEL), row),
                pl.BlockSpec((D_MODEL, SHARED_DIM), const),
                pl.BlockSpec((D_MODEL, SHARED_DIM), const),
                pl.BlockSpec((SHARED_DIM, D_MODEL), const),
            ],
            out_specs=[pl.BlockSpec(memory_space=pl.ANY), pl.BlockSpec((TM_DP, D_MODEL), row)],
            scratch_shapes=[pltpu.SemaphoreType.DMA],
        ),
        out_shape=[jax.ShapeDtypeStruct((n_slots, ROW_TILE, LANES), U32), jax.ShapeDtypeStruct((t, D_MODEL), F32)],
        compiler_params=_cparams(("arbitrary",)),
        name="dispatch",
    )(dest_flat, hp, h1, ws1, ws3, ws2)


N_WSLOTS = 2
BLK_PER_STEP = 2


def _expert_kernel(be_ref, nv_ref, first_ref, nxt_ref, x_ref, w1_ref, w3_ref, w2_ref, o_ref,
                   wf1, wf3, wf2, w1b, w3b, w2b, slot_ref, sems):
    step = pl.program_id(0)

    def weight_copies(e, s):
        return (pltpu.make_async_copy(w1_ref.at[e], wf1.at[s], sems.at[s, 0]),
                pltpu.make_async_copy(w3_ref.at[e], wf3.at[s], sems.at[s, 1]),
                pltpu.make_async_copy(w2_ref.at[e], wf2.at[s], sems.at[s, 2]))

    @pl.when(step == 0)
    def _():
        slot_ref[0] = 0
        for cp in weight_copies(be_ref[0], 0):
            cp.start()

    def one_block(b, xq, oq):
        nv = nv_ref[b]

        @pl.when(first_ref[b] == 1)
        def _():
            s = slot_ref[0]
            for cp in weight_copies(be_ref[b], s):
                cp.wait()

            @pl.when(nxt_ref[b] >= 0)
            def _():
                for cp in weight_copies(nxt_ref[b], 1 - s):
                    cp.start(priority=1)

            w1b[...] = wf1[s].astype(BF16)
            w3b[...] = wf3[s].astype(BF16)
            w2b[...] = wf2[s].astype(BF16)
            slot_ref[0] = 1 - s

        @pl.when(nv > 0)
        def _():
            live = lax.broadcasted_iota(I32, (MOE_BLK, 1), 0) < nv
            lo, hi = _unpack_halves(jnp.where(live, _load_row_tiled(xq, MOE_BLK), jnp.uint32(0)))
            x = jnp.concatenate([lo.astype(BF16), hi.astype(BF16)], axis=1)
            hdn = _silu(_dot(x, w1b[...])) * _dot(x, w3b[...])
            _store_row_tiled(oq, _pack_halves(_dot(hdn.astype(BF16), w2b[...])))

        @pl.when(nv <= 0)
        def _():
            oq[...] = jnp.zeros_like(oq)

    blk_rows = MOE_BLK * ROW_TILE
    for q in range(BLK_PER_STEP):
        rows = pl.ds(q * blk_rows, blk_rows)
        one_block(step * BLK_PER_STEP + q, x_ref.at[rows, :], o_ref.at[rows, :])


def _experts(block_e, block_nv, block_first, block_next, x_sorted, w1, w3, w2):
    n_rows = x_sorted.shape[0]
    step_rows = BLK_PER_STEP * MOE_BLK * ROW_TILE
    row = lambda b, *_: (b, 0)
    return pl.pallas_call(
        _expert_kernel,
        grid_spec=pltpu.PrefetchScalarGridSpec(
            num_scalar_prefetch=4,
            grid=(n_rows // step_rows,),
            in_specs=[
                pl.BlockSpec((step_rows, LANES), row),
                pl.BlockSpec(memory_space=pl.ANY),
                pl.BlockSpec(memory_space=pl.ANY),
                pl.BlockSpec(memory_space=pl.ANY),
            ],
            out_specs=pl.BlockSpec((step_rows, LANES), row),
            scratch_shapes=[
                pltpu.VMEM((N_WSLOTS, D_MODEL, EXPERT_DIM), F32),
                pltpu.VMEM((N_WSLOTS, D_MODEL, EXPERT_DIM), F32),
                pltpu.VMEM((N_WSLOTS, EXPERT_DIM, D_MODEL), F32),
                pltpu.VMEM((D_MODEL, EXPERT_DIM), BF16),
                pltpu.VMEM((D_MODEL, EXPERT_DIM), BF16),
                pltpu.VMEM((EXPERT_DIM, D_MODEL), BF16),
                pltpu.SMEM((1,), I32),
                pltpu.SemaphoreType.DMA((N_WSLOTS, 3)),
            ],
        ),
        out_shape=jax.ShapeDtypeStruct((n_rows, LANES), U32),
        compiler_params=_cparams(("arbitrary",), vmem_mb=58),
        name="experts",
    )(block_e, block_nv, block_first, block_next, x_sorted, w1, w3, w2)


TM_FN = 256


CH_FN = 8


def _final_kernel(alpha, dest_ref, ys_ref, w_ref, h_ref, sh_ref, g_ref, b_ref, o_ref, buf, sems):
    i = pl.program_id(0)
    n_tiles = pl.num_programs(0)
    t_total = n_tiles * TM_FN
    slot = i % 2
    nslot = 1 - slot
    half = D_MODEL // 2

    def issue_token(tile, sl, c, tt):
        tok = c * CH_FN + tt
        row0 = pl.multiple_of(tok * ROW_TILE, ROW_TILE)
        for k in range(TOP_K):
            src = ys_ref.at[dest_ref[k * t_total + tile * TM_FN + tok]]
            pltpu.make_async_copy(src, buf.at[sl, k, pl.ds(row0, ROW_TILE), :], sems.at[sl]).start(priority=k % 2)

    def wait_tile(sl):
        for k in range(TOP_K):
            pltpu.make_async_copy(ys_ref.at[pl.ds(0, TM_FN)], buf.at[sl, k].reshape(TM_FN, ROW_TILE, LANES),
                                  sems.at[sl]).wait()

    @pl.when(i == 0)
    def _():
        def first(c, carry):
            for tt in range(CH_FN):
                issue_token(0, 0, c, tt)
            return carry
        lax.fori_loop(0, TM_FN // CH_FN, first, 0)

    nxt_tile = jnp.minimum(i + 1, n_tiles - 1)

    def combine_from(cur, nxt):
        wait_tile(cur)

        def body(c, carry):
            tok = pl.ds(pl.multiple_of(c * CH_FN, CH_FN), CH_FN)
            acc_lo = sh_ref[tok, :half]
            acc_hi = sh_ref[tok, half:]
            pre = alpha * h_ref[tok, :]
            wrow = w_ref[tok, :]
            for k in range(TOP_K):
                words = jnp.concatenate(
                    [buf[cur, k, pl.ds(c * (CH_FN * ROW_TILE) + r, CH_FN, stride=ROW_TILE), :]
                     for r in range(ROW_TILE)], axis=1)
                issue_token(nxt_tile, nxt, c, k)
                lo, hi = _unpack_halves(words)
                wk = wrow[:, k:k + 1]
                acc_lo = acc_lo + wk * lo
                acc_hi = acc_hi + wk * hi
            o_ref[tok, :] = pre + jnp.concatenate([acc_lo, acc_hi], axis=1)
            return carry

        lax.fori_loop(0, TM_FN // CH_FN, body, 0)

        @pl.when(i == n_tiles - 1)
        def _():
            wait_tile(nxt)

    for parity in range(2):
        @pl.when(slot == parity)
        def _():
            combine_from(parity, 1 - parity)

    o_ref[...] = _layer_norm(o_ref[...], g_ref[...], b_ref[...])


def _final(dest_flat, ys, w_tok, h1, shared, g, b, alpha):
    t = h1.shape[0]
    row = lambda i, d: (i, 0)
    const = lambda i, d: (0, 0)
    return pl.pallas_call(
        functools.partial(_final_kernel, alpha),
        grid_spec=pltpu.PrefetchScalarGridSpec(
            num_scalar_prefetch=1,
            grid=(t // TM_FN,),
            in_specs=[
                pl.BlockSpec(memory_space=pl.ANY),
                pl.BlockSpec((TM_FN, TOP_K), row),
                pl.BlockSpec((TM_FN, D_MODEL), row),
                pl.BlockSpec((TM_FN, D_MODEL), row),
                pl.BlockSpec((1, D_MODEL), const),
                pl.BlockSpec((1, D_MODEL), const),
            ],
            out_specs=pl.BlockSpec((TM_FN, D_MODEL), row),
            scratch_shapes=[pltpu.VMEM((2, TOP_K, TM_FN * ROW_TILE, LANES), U32), pltpu.SemaphoreType.DMA((2,))],
        ),
        out_shape=jax.ShapeDtypeStruct((t, D_MODEL), F32),
        compiler_params=_cparams(("arbitrary",)),
        name="combine_ln2",
    )(dest_flat, ys, w_tok, h1, shared, g, b)


TM_INB = 512
TN_INB = W_B // 2


def _pad_lanes(v):
    return jnp.pad(v, (0, LANES - v.shape[0]))[None, :].astype(F32)


def _layer(h2d, mem2d, batch, seq, w_in_all, layer, conv_w, conv_b, dt_bias, a_log, d_skip, ssd_norm_g, swa_sinks,
           rel_bias, w_mem_kv, w_ssd_o, w_swa_o, w_xa_o, w_out, ln1_g, ln1_b, router_w, router_bias, w1, w3, w2, ws1,
           ws3, ws2, ln2_g, ln2_b, alpha):
    t = batch * seq
    w_t = jnp.swapaxes(w_in_all, 1, 2)
    wdt = jnp.pad(w_t[layer, W_A:W_A + SSD_HEADS, :].T, ((0, 0), (0, LANES - SSD_HEADS))).astype(BF16)
    pa, dt_raw, xb = _inproj(h2d, _wcast(w_t, layer, 0, W_A, "wcast_a"), wdt)
    pb = _mm(xb, _wcast(w_t, layer, W_A + SSD_HEADS, W_B, "wcast_b"), BF16, TM_INB, TN_INB, "inproj_b", vmem_mb=56)

    y_ssd = _ssd(
        pa, dt_raw,
        conv_w[:, :SSD_D_INNER], conv_b[None, :SSD_D_INNER], conv_w[:, SSD_D_INNER:], conv_b[None, SSD_D_INNER:],
        _pad_lanes(dt_bias), _pad_lanes(a_log),
        jnp.repeat(d_skip, SSD_HEAD_DIM)[None, :], ssd_norm_g[None, :], batch, seq)

    y_swa = _swa(pb, swa_sinks.astype(F32), _swa_bias_table(rel_bias), batch, seq)

    memkv = _mm(mem2d, w_mem_kv, BF16, batch * MEM_LEN, 1024, "mem_kv")
    y_xa = _xa(pb, memkv, batch, seq)

    merged = _merge(y_ssd, y_swa, y_xa, w_ssd_o, w_swa_o, w_xa_o, pb)
    h1, hp = _out_ln(merged, w_out.astype(BF16), h2d, ln1_g[None, :], ln1_b[None, :], alpha)

    rwt = router_w.T
    r_hi = rwt.astype(BF16)
    r_lo = (rwt - r_hi.astype(F32)).astype(BF16)
    ti = jnp.arange(TM_RT)
    su = (ti[:, None] < ti[None, :]).astype(BF16)
    idx_t, w_t, rank_t, cnt = _router(h1, r_hi, r_lo, router_bias[:, None].astype(F32), su)

    counts = cnt[:, 0].astype(I32)
    padded = (counts + MOE_BLK - 1) // MOE_BLK * MOE_BLK
    pend = jnp.cumsum(padded)
    pstart = pend - padded
    n_blocks = t * TOP_K // MOE_BLK + N_EXPERTS
    blk_lo = jnp.arange(n_blocks, dtype=I32) * MOE_BLK
    block_e = jnp.minimum(jnp.sum((pend[None, :] <= blk_lo[:, None]).astype(I32), axis=1), N_EXPERTS - 1)
    of_e = block_e[:, None] == jnp.arange(N_EXPERTS, dtype=I32)[None, :]
    valid_end = jnp.sum(jnp.where(of_e, (pstart + counts)[None, :], 0), axis=1)
    block_nv = jnp.clip(valid_end - blk_lo, 0, MOE_BLK).astype(I32)
    eids = jnp.arange(N_EXPERTS, dtype=I32)
    later_active = (eids[None, :] > eids[:, None]) & (padded > 0)[None, :]
    next_active = jnp.min(jnp.where(later_active, eids[None, :], N_EXPERTS), axis=1)
    next_active = jnp.where(next_active >= N_EXPERTS, -1, next_active)
    seg_start = jnp.sum(jnp.where(of_e, pstart[None, :], 0), axis=1)
    block_first = ((blk_lo == seg_start) & (block_nv > 0)).astype(I32)
    block_next = jnp.sum(jnp.where(of_e, next_active[None, :], 0), axis=1).astype(I32)

    dest_flat = _dest(idx_t, rank_t, pstart.astype(F32)[:, None]).reshape(-1)
    n_slots = n_blocks * MOE_BLK
    x_sorted, shared = _dispatch(dest_flat, hp, h1, ws1.astype(BF16), ws3.astype(BF16), ws2.astype(BF16),
                                 n_slots)
    ys = _experts(block_e, block_nv, block_first, block_next,
                  x_sorted.reshape(n_slots * ROW_TILE, LANES), w1, w3, w2)
    return _final(dest_flat, ys.reshape(n_slots, ROW_TILE, LANES), w_t.T, h1, shared, ln2_g[None, :],
                  ln2_b[None, :], alpha)


def kernel(x, mem, w_in, conv_w, conv_b, dt_bias, a_log, d_skip, ssd_norm_g, swa_sinks, rel_bias, w_mem_kv, w_ssd_o, w_swa_o, w_xa_o, w_out, ln1_g, ln1_b, router_w, router_bias, w1, w3, w2, ws1, ws3, ws2, ln2_g, ln2_b):
    batch, seq, d = x.shape
    depth = w_in.shape[0]
    alpha = (2.0 * depth) ** 0.25
    h = x.reshape(batch * seq, d)
    mem2d = mem.reshape(batch * MEM_LEN, d)
    for i in range(depth):
        h = _layer(h, mem2d, batch, seq, w_in, i, conv_w[i], conv_b[i], dt_bias[i], a_log[i], d_skip[i],
                   ssd_norm_g[i], swa_sinks[i], rel_bias, w_mem_kv[i], w_ssd_o[i], w_swa_o[i], w_xa_o[i], w_out[i],
                   ln1_g[i], ln1_b[i], router_w[i], router_bias[i], w1[i], w3[i], w2[i], ws1[i], ws3[i], ws2[i],
                   ln2_g[i], ln2_b[i], alpha)
    return h.reshape(batch, seq, d)
```

```python
import functools
import math

import jax
import jax.numpy as jnp
from jax import lax
from jax.experimental import pallas as pl
from jax.experimental.pallas import tpu as pltpu

F32 = jnp.float32
BF16 = jnp.bfloat16
I32 = jnp.int32
U32 = jnp.uint32

D_MODEL = 2048
MEM_LEN = 256
SSD_D_INNER = D_MODEL
SSD_HEAD_DIM = 64
SSD_HEADS = 32
SSD_GROUPS = 4
SSD_STATE = 128
SSD_CONV = 4
SSD_CHUNK = 128
SWA_HEADS = 16
SWA_KV_HEADS = 4
SWA_HEAD_DIM = 64
SWA_WINDOW = 128
SWA_BLOCK = 128
REL_BUCKETS = 32
REL_MAX_DIST = 128
XA_HEADS = 4
XA_HEAD_DIM = 256
N_BRANCH = 3
N_EXPERTS = 64
TOP_K = 8
N_EXPERT_GROUPS = 8
TOPK_GROUPS = 4
EXPERT_DIM = 512
SHARED_DIM = 512
ROUTED_SCALE = 2.5
LN_EPS = 1e-5
RMS_EPS = 1e-5

SWA_Q_DIM = SWA_HEADS * SWA_HEAD_DIM
SWA_KV_DIM = SWA_KV_HEADS * SWA_HEAD_DIM
XA_DIM = XA_HEADS * XA_HEAD_DIM
BC_DIM = 2 * SSD_GROUPS * SSD_STATE

W_A = SSD_D_INNER + SSD_D_INNER + BC_DIM
OFFA_Z = 0
OFFA_XS = OFFA_Z + SSD_D_INNER
OFFA_BC = OFFA_XS + SSD_D_INNER
W_B = SWA_Q_DIM + 2 * SWA_KV_DIM + XA_DIM + N_BRANCH * D_MODEL
OFFB_Q = 0
OFFB_K = OFFB_Q + SWA_Q_DIM
OFFB_V = OFFB_K + SWA_KV_DIM
OFFB_QX = OFFB_V + SWA_KV_DIM
OFFB_GATES = OFFB_QX + XA_DIM

LANES = 128
MOE_BLK = 256
NEG_INF = float("-inf")


def _cparams(sem, vmem_mb=48):
    return pltpu.CompilerParams(dimension_semantics=sem, vmem_limit_bytes=vmem_mb * 1024 * 1024)


def _dot(a, b):
    return jnp.dot(a, b, preferred_element_type=F32)


def _dot_nt(a, b):
    return lax.dot_general(a, b, (((1,), (1,)), ((), ())), preferred_element_type=F32)


def _sigmoid(x):
    return 1.0 / (1.0 + jnp.exp(-x))


def _silu(x):
    return x * _sigmoid(x)


TM_IN = 512
TN_IN = W_A // 2


def _inproj_kernel(x_ref, w_ref, wdt_ref, p_ref, dt_ref, xb_ref):
    @pl.when(pl.program_id(1) == 0)
    def _():
        xb = x_ref[...].astype(BF16)
        xb_ref[...] = xb
        dt_ref[...] = _dot(xb, wdt_ref[...])

    p_ref[...] = _dot(xb_ref[...], w_ref[...].astype(BF16)).astype(BF16)


def _inproj(x2d, w_in, wdt):
    t = x2d.shape[0]
    return pl.pallas_call(
        _inproj_kernel,
        grid=(t // TM_IN, W_A // TN_IN),
        in_specs=[
            pl.BlockSpec((TM_IN, D_MODEL), lambda i, j: (i, 0)),
            pl.BlockSpec((D_MODEL, TN_IN), lambda i, j: (0, j)),
            pl.BlockSpec((D_MODEL, LANES), lambda i, j: (0, 0)),
        ],
        out_specs=[
            pl.BlockSpec((TM_IN, TN_IN), lambda i, j: (i, j)),
            pl.BlockSpec((TM_IN, LANES), lambda i, j: (i, 0)),
            pl.BlockSpec((TM_IN, D_MODEL), lambda i, j: (i, 0)),
        ],
        out_shape=[jax.ShapeDtypeStruct((t, W_A), BF16), jax.ShapeDtypeStruct((t, LANES), F32),
                   jax.ShapeDtypeStruct((t, D_MODEL), BF16)],
        compiler_params=_cparams(("parallel", "arbitrary"), vmem_mb=56),
        name="inproj",
    )(x2d, w_in, wdt)


TN_WC = 512
ROW_ALIGN = 32


def _wcast_kernel(a_ref, o_ref):
    o_ref[...] = a_ref[0].T.astype(BF16)


def _wcast(w_t, layer, start, width, name):
    k = w_t.shape[2]
    return pl.pallas_call(
        _wcast_kernel,
        grid=(width // TN_WC,),
        in_specs=[pl.BlockSpec((pl.Element(1), pl.Element(TN_WC), pl.Element(k)),
                               lambda j: (layer, pl.multiple_of(start + j * TN_WC, ROW_ALIGN), 0))],
        out_specs=pl.BlockSpec((k, TN_WC), lambda j: (0, j)),
        out_shape=jax.ShapeDtypeStruct((k, width), BF16),
        compiler_params=_cparams(("parallel",)),
        name=name,
    )(w_t)


def _mm_kernel(a_ref, b_ref, o_ref):
    o_ref[...] = _dot(a_ref[...].astype(BF16), b_ref[...].astype(BF16)).astype(o_ref.dtype)


def _mm(a, b, out_dtype, tm, tn, name, vmem_mb=48):
    m, k = a.shape
    n = b.shape[1]
    return pl.pallas_call(
        _mm_kernel,
        grid=(m // tm, n // tn),
        in_specs=[pl.BlockSpec((tm, k), lambda i, j: (i, 0)), pl.BlockSpec((k, tn), lambda i, j: (0, j))],
        out_specs=pl.BlockSpec((tm, tn), lambda i, j: (i, j)),
        out_shape=jax.ShapeDtypeStruct((m, n), out_dtype),
        compiler_params=_cparams(("parallel", "parallel"), vmem_mb=vmem_mb),
        name=name,
    )(a, b)


Q = SSD_CHUNK


def _conv_shift_matrix():
    t = jnp.arange(Q)[:, None]
    j = jnp.arange(2 * Q)[None, :]
    blocks = []
    for k in range(SSD_CONV - 1):
        src = t - (SSD_CONV - 1 - k)
        blocks.append(j == jnp.where(src >= 0, src, 2 * Q + src))
    return jnp.concatenate(blocks, axis=0).astype(BF16)


def _ssd_kernel(xs_ref, bc_ref, z_ref, dtr_ref, shift_ref, cwx_ref, cbx_ref, cwb_ref, cbb_ref, dtb_ref, alog_ref,
                dsk_ref, ng_ref, o_ref, hx_ref, hb_ref, st_ref, cumt_ref, dtt_ref, wt_ref, y_ref):
    c = pl.program_id(1)

    @pl.when(c == 0)
    def _():
        hx_ref[...] = jnp.zeros_like(hx_ref)
        hb_ref[...] = jnp.zeros_like(hb_ref)
        st_ref[...] = jnp.zeros_like(st_ref)

    def conv_silu(in_ref, prev_ref, w_ref, b_ref):
        cur = in_ref[...]
        ext = jnp.concatenate([cur, prev_ref[...]], axis=0)
        shifted = _dot(shift_ref[...], ext)
        acc = b_ref[...] + w_ref[SSD_CONV - 1:SSD_CONV, :] * cur.astype(F32)
        for k in range(SSD_CONV - 1):
            acc = acc + w_ref[k:k + 1, :] * shifted[k * Q:(k + 1) * Q, :]
        prev_ref[...] = cur
        return _silu(acc)

    xs_c = conv_silu(xs_ref, hx_ref, cwx_ref, cbx_ref)
    bc_c = conv_silu(bc_ref, hb_ref, cwb_ref, cbb_ref)

    lane = lax.broadcasted_iota(I32, (Q, LANES), 1)
    row = lax.broadcasted_iota(I32, (Q, LANES), 0)
    tril = row >= lane
    lo_mask = lane < SSD_HEAD_DIM

    dtv = dtr_ref[...] + dtb_ref[...]
    dt = jnp.maximum(dtv, 0.0) + jnp.log1p(jnp.exp(-jnp.abs(dtv)))
    a = jnp.where(lane[0:1, :] < SSD_HEADS, -jnp.exp(alog_ref[...]), 0.0)
    la = dt * a
    tri = jnp.where(tril, 1.0, 0.0).astype(BF16)
    p1 = la.astype(BF16)
    r1 = la - p1.astype(F32)
    p2 = r1.astype(BF16)
    p3 = (r1 - p2.astype(F32)).astype(BF16)
    cum = _dot(tri, p1) + _dot(tri, p2) + _dot(tri, p3)
    ecum = jnp.exp(cum)
    cum_t = cum.T
    dt_t = dt.T
    last = cum_t[:, Q - 1:Q]
    cumt_ref[...] = cum_t
    dtt_ref[...] = dt_t
    wt_ref[...] = jnp.exp(last - cum_t) * dt_t
    chunk_decay = jnp.exp(last)

    def head_parts(h, cb, cm, bm_t):
        col = jnp.broadcast_to(cum[:, h:h + 1], (Q, Q))
        rw = cumt_ref[h:h + 1, :]
        dec = jnp.exp(jnp.where(tril, col - rw, NEG_INF))
        m = cb * (dec * dtt_ref[h:h + 1, :])
        ce = cm * jnp.broadcast_to(ecum[:, h:h + 1], (Q, Q))
        sl = bm_t * wt_ref[h:h + 1, :]
        return m.astype(BF16), ce.astype(BF16), sl.astype(BF16)

    for g in range(SSD_GROUPS):
        bm = bc_c[:, g * SSD_STATE:(g + 1) * SSD_STATE]
        cm = bc_c[:, BC_DIM // 2 + g * SSD_STATE:BC_DIM // 2 + (g + 1) * SSD_STATE]
        cb = _dot_nt(cm.astype(BF16), bm.astype(BF16))
        bm_t = bm.T
        for qd in range(SSD_HEADS // SSD_GROUPS // 2):
            hp = g * (SSD_HEADS // SSD_GROUPS // 2) + qd
            ha, hb = 2 * hp, 2 * hp + 1
            sl_ = slice(hp * LANES, (hp + 1) * LANES)
            xs_pair = xs_c[:, sl_]
            x_lo = jnp.where(lo_mask, xs_pair, 0.0).astype(BF16)
            x_hi = jnp.where(lo_mask, 0.0, xs_pair).astype(BF16)
            h_prev = st_ref[:, sl_]
            h_lo = jnp.where(lo_mask, h_prev, 0.0).astype(BF16)
            h_hi = jnp.where(lo_mask, 0.0, h_prev).astype(BF16)
            m_a, ce_a, sl_a = head_parts(ha, cb, cm, bm_t)
            m_b, ce_b, sl_b = head_parts(hb, cb, cm, bm_t)
            lhs = jnp.concatenate([m_a, ce_a, m_b, ce_b], axis=1)
            rhs = jnp.concatenate([x_lo, h_lo, x_hi, h_hi], axis=0)
            y_ref[:, sl_] = _dot(lhs, rhs)
            new = _dot(jnp.concatenate([sl_a, sl_b], axis=1), jnp.concatenate([x_lo, x_hi], axis=0))
            cd = jnp.where(lo_mask[0:1, :], chunk_decay[ha:ha + 1, :], chunk_decay[hb:hb + 1, :])
            st_ref[:, sl_] = h_prev * cd + new

    y = y_ref[...] + xs_c * dsk_ref[...]
    y = y * _silu(z_ref[...].astype(F32))
    gw = SSD_D_INNER // SSD_GROUPS
    for g in range(SSD_GROUPS):
        yg = y[:, g * gw:(g + 1) * gw]
        ms = jnp.mean(yg * yg, axis=-1, keepdims=True)
        o_ref[:, g * gw:(g + 1) * gw] = ((yg * lax.rsqrt(ms + RMS_EPS)) * ng_ref[:, g * gw:(g + 1) * gw]).astype(BF16)


def _ssd(p, dt_raw, cwx, cbx, cwb, cbb, dtb, alog, dsk, ng, batch, seq):
    nc = seq // Q
    t = batch * seq
    rowblk = lambda b, c: b * nc + c
    const = lambda b, c: (0, 0)
    return pl.pallas_call(
        _ssd_kernel,
        grid=(batch, nc),
        in_specs=[
            pl.BlockSpec((Q, SSD_D_INNER), lambda b, c: (rowblk(b, c), OFFA_XS // SSD_D_INNER)),
            pl.BlockSpec((Q, BC_DIM), lambda b, c: (rowblk(b, c), OFFA_BC // BC_DIM)),
            pl.BlockSpec((Q, SSD_D_INNER), lambda b, c: (rowblk(b, c), OFFA_Z // SSD_D_INNER)),
            pl.BlockSpec((Q, LANES), lambda b, c: (rowblk(b, c), 0)),
            pl.BlockSpec(((SSD_CONV - 1) * Q, 2 * Q), const),
            pl.BlockSpec((SSD_CONV, SSD_D_INNER), const),
            pl.BlockSpec((1, SSD_D_INNER), const),
            pl.BlockSpec((SSD_CONV, BC_DIM), const),
            pl.BlockSpec((1, BC_DIM), const),
            pl.BlockSpec((1, LANES), const),
            pl.BlockSpec((1, LANES), const),
            pl.BlockSpec((1, SSD_D_INNER), const),
            pl.BlockSpec((1, SSD_D_INNER), const),
        ],
        out_specs=pl.BlockSpec((Q, SSD_D_INNER), lambda b, c: (rowblk(b, c), 0)),
        out_shape=jax.ShapeDtypeStruct((t, SSD_D_INNER), BF16),
        scratch_shapes=[
            pltpu.VMEM((Q, SSD_D_INNER), BF16),
            pltpu.VMEM((Q, BC_DIM), BF16),
            pltpu.VMEM((SSD_STATE, SSD_D_INNER), F32),
            pltpu.VMEM((LANES, Q), F32),
            pltpu.VMEM((LANES, Q), F32),
            pltpu.VMEM((LANES, Q), F32),
            pltpu.VMEM((Q, SSD_D_INNER), F32),
        ],
        compiler_params=_cparams(("parallel", "arbitrary")),
        name="ssd",
    )(p, p, p, dt_raw, _conv_shift_matrix(), cwx, cbx, cwb, cbb, dtb, alog, dsk, ng)


N_PAIRS = SWA_HEADS // 2


def _swa_kernel(sink_ref, q_ref, kp_ref, kc_ref, vp_ref, vc_ref, bias_ref, o_ref):
    n = pl.program_id(1)
    blk = SWA_BLOCK
    kband = jnp.concatenate([kp_ref[...], kc_ref[...]], axis=0)
    vband = jnp.concatenate([vp_ref[...], vc_ref[...]], axis=0)

    lane = lax.broadcasted_iota(I32, (1, LANES), 1)
    lo_f = jnp.where(lane < SWA_HEAD_DIM, 1.0, 0.0)
    lo_m = lo_f.astype(BF16)
    hi_m = (1.0 - lo_f).astype(BF16)

    def head_twice(band, kv):
        tile = band[:, (kv // 2) * LANES:(kv // 2 + 1) * LANES]
        swapped = jnp.concatenate([tile[:, SWA_HEAD_DIM:], tile[:, :SWA_HEAD_DIM]], axis=1)
        return tile * lo_m + swapped * hi_m if kv % 2 == 0 else swapped * lo_m + tile * hi_m

    k_twice = [head_twice(kband, kv) for kv in range(SWA_KV_HEADS)]
    v_twice = [head_twice(vband, kv) for kv in range(SWA_KV_HEADS)]
    rowi = lax.broadcasted_iota(I32, (2 * blk, 1), 0)
    scale = SWA_HEAD_DIM ** -0.5
    assert math.frexp(scale)[0] == 0.5
    lo_q = (lo_f * scale).astype(BF16)
    hi_q = ((1.0 - lo_f) * scale).astype(BF16)
    for p in range(N_PAIRS):
        kv = p // (SWA_HEADS // SWA_KV_HEADS // 2)
        qp = q_ref[:, p * LANES:(p + 1) * LANES]
        qs = jnp.concatenate([qp * lo_q, qp * hi_q], axis=0)
        s = _dot_nt(qs, k_twice[kv]) + bias_ref[0, p]
        sink = jnp.where(rowi < blk, sink_ref[2 * p], sink_ref[2 * p + 1])
        m = jnp.maximum(jnp.max(s, axis=-1, keepdims=True), sink)
        e = jnp.exp(s - m)
        den = jnp.sum(e, axis=-1, keepdims=True) + jnp.exp(sink - m)
        pr = (e * (1.0 / den)).astype(BF16)
        vd = v_twice[kv]
        lhs = jnp.concatenate([pr[:blk], pr[blk:]], axis=1)
        rhs = jnp.concatenate([vd * lo_m, vd * hi_m], axis=0)
        o_ref[:, p * LANES:(p + 1) * LANES] = _dot(lhs, rhs).astype(BF16)


def _swa(p, sinks, bias, batch, seq):
    nb = seq // SWA_BLOCK
    t = batch * seq
    kw = SWA_KV_DIM
    cur = lambda b, n, s: b * nb + n
    prev = lambda b, n, s: b * nb + jnp.maximum(n - 1, 0)
    return pl.pallas_call(
        _swa_kernel,
        grid_spec=pltpu.PrefetchScalarGridSpec(
            num_scalar_prefetch=1,
            grid=(batch, nb),
            in_specs=[
                pl.BlockSpec((SWA_BLOCK, SWA_Q_DIM), lambda b, n, s: (cur(b, n, s), OFFB_Q // SWA_Q_DIM)),
                pl.BlockSpec((SWA_BLOCK, kw), lambda b, n, s: (prev(b, n, s), OFFB_K // kw)),
                pl.BlockSpec((SWA_BLOCK, kw), lambda b, n, s: (cur(b, n, s), OFFB_K // kw)),
                pl.BlockSpec((SWA_BLOCK, kw), lambda b, n, s: (prev(b, n, s), OFFB_V // kw)),
                pl.BlockSpec((SWA_BLOCK, kw), lambda b, n, s: (cur(b, n, s), OFFB_V // kw)),
                pl.BlockSpec((1, N_PAIRS, 2 * SWA_BLOCK, 2 * SWA_BLOCK),
                             lambda b, n, s: (jnp.minimum(n, 1), 0, 0, 0)),
            ],
            out_specs=pl.BlockSpec((SWA_BLOCK, SWA_Q_DIM), lambda b, n, s: (cur(b, n, s), 0)),
        ),
        out_shape=jax.ShapeDtypeStruct((t, SWA_Q_DIM), BF16),
        compiler_params=_cparams(("parallel", "arbitrary")),
        name="swa",
    )(sinks, p, p, p, p, p, bias)


def _t5_causal_bucket(dist):
    max_exact = REL_BUCKETS // 2
    large = max_exact + (jnp.log(jnp.maximum(dist, 1).astype(F32) / max_exact)
                         / math.log(REL_MAX_DIST / max_exact) * (REL_BUCKETS - max_exact)).astype(I32)
    large = jnp.minimum(large, REL_BUCKETS - 1)
    return jnp.where(dist < max_exact, dist, large)


def _swa_bias_table(rel_bias):
    qi = jnp.arange(SWA_BLOCK)[:, None]
    kj = jnp.arange(2 * SWA_BLOCK)[None, :]
    dist = qi + SWA_BLOCK - kj
    in_window = (dist >= 0) & (dist < SWA_WINDOW)
    bucket = _t5_causal_bucket(jnp.maximum(dist, 0))
    onehot = (bucket[:, :, None] == jnp.arange(REL_BUCKETS)[None, None, :]).astype(F32)
    bias = jnp.einsum('ijb,bh->hij', onehot, rel_bias.astype(F32), precision=lax.Precision.HIGHEST)
    rest = jnp.where(in_window[None, :, :], bias, NEG_INF)
    first = jnp.where((kj >= SWA_BLOCK)[None, :, :], rest, NEG_INF)
    return jnp.stack([first, rest]).reshape(2, N_PAIRS, 2 * SWA_BLOCK, 2 * SWA_BLOCK)


TL_XA = 512


XA_HALF = XA_DIM // 2


def _xa_kernel(qa_ref, qb_ref, mk_ref, mv_ref, o_ref):
    for h in range(XA_HEADS):
        q_ref = qa_ref if h < XA_HEADS // 2 else qb_ref
        qc = (h % (XA_HEADS // 2)) * XA_HEAD_DIM
        cols = slice(h * XA_HEAD_DIM, (h + 1) * XA_HEAD_DIM)
        s = _dot_nt(q_ref[:, qc:qc + XA_HEAD_DIM], mk_ref[:, cols]) * (XA_HEAD_DIM ** -0.5)
        m = jnp.max(s, axis=-1, keepdims=True)
        e = jnp.exp(s - m)
        pr = e * (1.0 / jnp.sum(e, axis=-1, keepdims=True))
        o_ref[:, cols] = _dot(pr.astype(BF16), mv_ref[:, cols]).astype(BF16)


def _xa(p, memkv, batch, seq):
    nl = seq // TL_XA
    t = batch * seq
    return pl.pallas_call(
        _xa_kernel,
        grid=(batch, nl),
        in_specs=[
            pl.BlockSpec((TL_XA, XA_HALF), lambda b, l: (b * nl + l, OFFB_QX // XA_HALF)),
            pl.BlockSpec((TL_XA, XA_HALF), lambda b, l: (b * nl + l, OFFB_QX // XA_HALF + 1)),
            pl.BlockSpec((MEM_LEN, XA_DIM), lambda b, l: (b, 0)),
            pl.BlockSpec((MEM_LEN, XA_DIM), lambda b, l: (b, 1)),
        ],
        out_specs=pl.BlockSpec((TL_XA, XA_DIM), lambda b, l: (b * nl + l, 0)),
        out_shape=jax.ShapeDtypeStruct((t, XA_DIM), BF16),
        compiler_params=_cparams(("parallel", "parallel")),
        name="xattn",
    )(p, p, memkv, memkv)


TM_MG = 1024
TN_MG = 512


N_SUB_MG = 2


def _merge_kernel(ys_ref, yw_ref, yx_ref, ws_ref, ww_ref, wx_ref, g0_ref, g1_ref, g2_ref, o_ref):
    sub = TM_MG // N_SUB_MG
    ws = ws_ref[...].astype(BF16)
    ww = ww_ref[...].astype(BF16)
    wx = wx_ref[...].astype(BF16)
    for s in range(N_SUB_MG):
        rows = slice(s * sub, (s + 1) * sub)
        a = _dot(ys_ref[rows, :], ws)
        b = _dot(yw_ref[rows, :], ww)
        c = _dot(yx_ref[rows, :], wx)
        g0 = _sigmoid(g0_ref[rows, :].astype(F32))
        g1 = _sigmoid(g1_ref[rows, :].astype(F32))
        g2 = _sigmoid(g2_ref[rows, :].astype(F32))
        o_ref[rows, :] = (g0 * a + g1 * b + g2 * c).astype(BF16)


def _merge(y_ssd, y_swa, y_xa, w_ssd_o, w_swa_o, w_xa_o, p):
    t = y_ssd.shape[0]
    gpb = D_MODEL // TN_MG
    gate = lambda k: pl.BlockSpec((TM_MG, TN_MG), lambda i, j: (i, OFFB_GATES // TN_MG + k * gpb + j))
    return pl.pallas_call(
        _merge_kernel,
        grid=(t // TM_MG, D_MODEL // TN_MG),
        in_specs=[
            pl.BlockSpec((TM_MG, SSD_D_INNER), lambda i, j: (i, 0)),
            pl.BlockSpec((TM_MG, SWA_Q_DIM), lambda i, j: (i, 0)),
            pl.BlockSpec((TM_MG, XA_DIM), lambda i, j: (i, 0)),
            pl.BlockSpec((SSD_D_INNER, TN_MG), lambda i, j: (0, j)),
            pl.BlockSpec((SWA_Q_DIM, TN_MG), lambda i, j: (0, j)),
            pl.BlockSpec((XA_DIM, TN_MG), lambda i, j: (0, j)),
            gate(0), gate(1), gate(2),
        ],
        out_specs=pl.BlockSpec((TM_MG, TN_MG), lambda i, j: (i, j)),
        out_shape=jax.ShapeDtypeStruct((t, D_MODEL), BF16),
        compiler_params=_cparams(("parallel", "parallel")),
        name="merge",
    )(y_ssd, y_swa, y_xa, w_ssd_o, w_swa_o, w_xa_o, p, p, p)


TM_LN = 512


def _layer_norm(v, g, b):
    mu = jnp.mean(v, axis=-1, keepdims=True)
    d = v - mu
    var = jnp.mean(d * d, axis=-1, keepdims=True)
    return (d * lax.rsqrt(var + LN_EPS)) * g + b


def _pack_halves(v):
    n = v.shape[1] // 2
    lo = lax.bitcast_convert_type(v[:, :n].astype(BF16).astype(F32), U32) >> 16
    hi = lax.bitcast_convert_type(v[:, n:].astype(BF16).astype(F32), U32) & jnp.uint32(0xFFFF0000)
    return hi | lo


def _unpack_halves(w):
    lo = lax.bitcast_convert_type(w << 16, F32)
    hi = lax.bitcast_convert_type(w & jnp.uint32(0xFFFF0000), F32)
    return lo, hi


ROW_WORDS = D_MODEL // 2
ROW_TILE = ROW_WORDS // LANES


def _store_row_tiled(ref, words):
    m = words.shape[0]
    for c in range(ROW_TILE):
        ref[pl.ds(c, m, stride=ROW_TILE), :] = words[:, c * LANES:(c + 1) * LANES]


def _load_row_tiled(ref, m):
    return jnp.concatenate([ref[pl.ds(c, m, stride=ROW_TILE), :] for c in range(ROW_TILE)], axis=1)


N_SUB_LN = 4


def _out_ln_kernel(alpha, m_ref, w_ref, x_ref, g_ref, b_ref, h_ref, hp_ref):
    sub = TM_LN // N_SUB_LN
    for c in range(N_SUB_LN):
        rows = slice(c * sub, (c + 1) * sub)
        mix = _dot(m_ref[rows, :], w_ref[...])
        hn = _layer_norm(alpha * x_ref[rows, :] + mix, g_ref[...], b_ref[...])
        h_ref[rows, :] = hn
        _store_row_tiled(hp_ref.at[pl.ds(c * sub * ROW_TILE, sub * ROW_TILE), :], _pack_halves(hn))


def _out_ln(merged, w_out, x2d, g, b, alpha):
    t = x2d.shape[0]
    row = lambda i: (i, 0)
    const = lambda i: (0, 0)
    return pl.pallas_call(
        functools.partial(_out_ln_kernel, alpha),
        grid=(t // TM_LN,),
        in_specs=[
            pl.BlockSpec((TM_LN, D_MODEL), row),
            pl.BlockSpec((D_MODEL, D_MODEL), const),
            pl.BlockSpec((TM_LN, D_MODEL), row),
            pl.BlockSpec((1, D_MODEL), const),
            pl.BlockSpec((1, D_MODEL), const),
        ],
        out_specs=[pl.BlockSpec((TM_LN, D_MODEL), row), pl.BlockSpec((TM_LN * ROW_TILE, LANES), row)],
        out_shape=[jax.ShapeDtypeStruct((t, D_MODEL), F32), jax.ShapeDtypeStruct((t * ROW_TILE, LANES), U32)],
        compiler_params=_cparams(("parallel",)),
        name="out_ln1",
    )(merged, w_out, x2d, g, b)


TM_RT = 512
EPG = N_EXPERTS // N_EXPERT_GROUPS


def _router_kernel(h_ref, whi_ref, wlo_ref, bias_ref, su_ref, idx_ref, w_ref, rank_ref, cnt_ref, run_ref):
    @pl.when(pl.program_id(0) == 0)
    def _():
        run_ref[...] = jnp.zeros_like(run_ref)

    h = h_ref[...]
    h_hi = h.astype(BF16)
    h_lo = (h - h_hi.astype(F32)).astype(BF16)
    whi = whi_ref[...]
    logits = _dot_nt(whi, h_hi) + _dot_nt(whi, h_lo) + _dot_nt(wlo_ref[...], h_hi)
    scores = _sigmoid(logits)
    sel = scores + bias_ref[...]

    iota_m = lax.broadcasted_iota(I32, (EPG, TM_RT), 0).astype(F32)
    iota_g = lax.broadcasted_iota(I32, (N_EXPERT_GROUPS, TM_RT), 0).astype(F32)
    gs = []
    for g in range(N_EXPERT_GROUPS):
        v = sel[g * EPG:(g + 1) * EPG, :]
        m1 = jnp.max(v, axis=0, keepdims=True)
        i1 = jnp.min(jnp.where(v == m1, iota_m, float(EPG)), axis=0, keepdims=True)
        m2 = jnp.max(jnp.where(iota_m == i1, NEG_INF, v), axis=0, keepdims=True)
        gs.append(m1 + m2)
    gsc = jnp.concatenate(gs, axis=0)
    chosen = jnp.zeros((N_EXPERT_GROUPS, TM_RT), F32)
    for _ in range(TOPK_GROUPS):
        m = jnp.max(gsc, axis=0, keepdims=True)
        ig = jnp.min(jnp.where(gsc == m, iota_g, float(N_EXPERT_GROUPS)), axis=0, keepdims=True)
        hit = iota_g == ig
        chosen = jnp.where(hit, 1.0, chosen)
        gsc = jnp.where(hit, NEG_INF, gsc)
    gmask = jnp.concatenate(
        [jnp.broadcast_to(chosen[g:g + 1, :], (EPG, TM_RT)) for g in range(N_EXPERT_GROUPS)], axis=0)
    masked = jnp.where(gmask > 0.0, sel, NEG_INF)

    iota_e = lax.broadcasted_iota(I32, (N_EXPERTS, TM_RT), 0).astype(F32)
    hits, idxs, ws = [], [], []
    for _ in range(TOP_K):
        m = jnp.max(masked, axis=0, keepdims=True)
        ie = jnp.min(jnp.where(masked == m, iota_e, float(N_EXPERTS)), axis=0, keepdims=True)
        hit = iota_e == ie
        hits.append(hit)
        idxs.append(ie)
        ws.append(jnp.sum(jnp.where(hit, scores, 0.0), axis=0, keepdims=True))
        masked = jnp.where(hit, NEG_INF, masked)
    wsum = ws[0]
    for k in range(1, TOP_K):
        wsum = wsum + ws[k]
    inv = 1.0 / wsum
    idx_ref[...] = jnp.concatenate(idxs, axis=0).astype(I32)
    w_ref[...] = jnp.concatenate([w * inv * ROUTED_SCALE for w in ws], axis=0)

    assign = jnp.zeros((N_EXPERTS, TM_RT), F32)
    for hit in hits:
        assign = jnp.where(hit, 1.0, assign)
    pos = _dot(assign.astype(BF16), su_ref[...]) + run_ref[...]
    rank_ref[...] = jnp.concatenate(
        [jnp.sum(jnp.where(hit, pos, 0.0), axis=0, keepdims=True) for hit in hits], axis=0).astype(I32)
    run = run_ref[...] + jnp.sum(assign, axis=1, keepdims=True)
    run_ref[...] = run
    cnt_ref[...] = run


def _router(h1, r_hi, r_lo, rbias, su):
    t = h1.shape[0]
    tile = lambda i: (0, i)
    const = lambda i: (0, 0)
    return pl.pallas_call(
        _router_kernel,
        grid=(t // TM_RT,),
        in_specs=[
            pl.BlockSpec((TM_RT, D_MODEL), lambda i: (i, 0)),
            pl.BlockSpec((N_EXPERTS, D_MODEL), const),
            pl.BlockSpec((N_EXPERTS, D_MODEL), const),
            pl.BlockSpec((N_EXPERTS, 1), const),
            pl.BlockSpec((TM_RT, TM_RT), const),
        ],
        out_specs=[
            pl.BlockSpec((TOP_K, TM_RT), tile),
            pl.BlockSpec((TOP_K, TM_RT), tile),
            pl.BlockSpec((TOP_K, TM_RT), tile),
            pl.BlockSpec((N_EXPERTS, TM_RT), const),
        ],
        out_shape=[
            jax.ShapeDtypeStruct((TOP_K, t), I32),
            jax.ShapeDtypeStruct((TOP_K, t), F32),
            jax.ShapeDtypeStruct((TOP_K, t), I32),
            jax.ShapeDtypeStruct((N_EXPERTS, TM_RT), F32),
        ],
        scratch_shapes=[pltpu.VMEM((N_EXPERTS, TM_RT), F32)],
        compiler_params=_cparams(("arbitrary",)),
        name="router",
    )(h1, r_hi, r_lo, rbias, su)


TM_DS = 2048


def _dest_kernel(idx_ref, rank_ref, ps_ref, d_ref):
    iota_e = lax.broadcasted_iota(I32, (N_EXPERTS, TM_DS), 0)
    ps = ps_ref[...]
    rows = []
    for k in range(TOP_K):
        hit = iota_e == idx_ref[k:k + 1, :]
        rows.append(jnp.sum(jnp.where(hit, ps, 0.0), axis=0, keepdims=True))
    d_ref[...] = jnp.concatenate(rows, axis=0).astype(I32) + rank_ref[...]


def _dest(idx_t, rank_t, pstart):
    t = idx_t.shape[1]
    tile = lambda i: (0, i)
    return pl.pallas_call(
        _dest_kernel,
        grid=(t // TM_DS,),
        in_specs=[pl.BlockSpec((TOP_K, TM_DS), tile), pl.BlockSpec((TOP_K, TM_DS), tile),
                  pl.BlockSpec((N_EXPERTS, 1), lambda i: (0, 0))],
        out_specs=pl.BlockSpec((TOP_K, TM_DS), tile),
        out_shape=jax.ShapeDtypeStruct((TOP_K, t), I32),
        compiler_params=_cparams(("parallel",)),
        name="dest",
    )(idx_t, rank_t, pstart)


TM_DP = 512


def _dispatch_kernel(dest_ref, hp_ref, h_ref, ws1_ref, ws3_ref, ws2_ref, xs_ref, sh_ref, sem):
    t_total = pl.num_programs(0) * TM_DP
    base = pl.program_id(0) * TM_DP

    def issue(t, carry):
        src = hp_ref.at[pl.ds(pl.multiple_of(t * ROW_TILE, ROW_TILE), ROW_TILE), :]
        for k in range(TOP_K):
            slot = dest_ref[k * t_total + base + t]
            pltpu.make_async_copy(src, xs_ref.at[slot], sem).start(priority=k % 2)
        return carry

    lax.fori_loop(0, TM_DP, issue, 0)

    xb = h_ref[...].astype(BF16)
    hdn = _silu(_dot(xb, ws1_ref[...])) * _dot(xb, ws3_ref[...])
    sh_ref[...] = _dot(hdn.astype(BF16), ws2_ref[...])

    for k in range(TOP_K):
        pltpu.make_async_copy(hp_ref.reshape(TM_DP, ROW_TILE, LANES), xs_ref.at[pl.ds(0, TM_DP)], sem).wait()


def _dispatch(dest_flat, hp, h1, ws1, ws3, ws2, n_slots):
    t = h1.shape[0]
    row = lambda i, d: (i, 0)
    const = lambda i, d: (0, 0)
    return pl.pallas_call(
        _dispatch_kernel,
        grid_spec=pltpu.PrefetchScalarGridSpec(
            num_scalar_prefetch=1,
            grid=(t // TM_DP,),
            in_specs=[
                pl.BlockSpec((TM_DP * ROW_TILE, LANES), row),
                pl.BlockSpec((TM_DP, D_MODEL), row),
                pl.BlockSpec((D_MODEL, SHARED_DIM), const),
                pl.BlockSpec((D_MODEL, SHARED_DIM), const),
                pl.BlockSpec((SHARED_DIM, D_MODEL), const),
            ],
            out_specs=[pl.BlockSpec(memory_space=pl.ANY), pl.BlockSpec((TM_DP, D_MODEL), row)],
            scratch_shapes=[pltpu.SemaphoreType.DMA],
        ),
        out_shape=[jax.ShapeDtypeStruct((n_slots, ROW_TILE, LANES), U32), jax.ShapeDtypeStruct((t, D_MODEL), F32)],
        compiler_params=_cparams(("arbitrary",)),
        name="dispatch",
    )(dest_flat, hp, h1, ws1, ws3, ws2)


N_WSLOTS = 2
BLK_PER_STEP = 2


def _expert_kernel(be_ref, nv_ref, first_ref, nxt_ref, x_ref, w1_ref, w3_ref, w2_ref, o_ref,
                   wf1, wf3, wf2, w1b, w3b, w2b, slot_ref, sems):
    step = pl.program_id(0)

    def weight_copies(e, s):
        return (pltpu.make_async_copy(w1_ref.at[e], wf1.at[s], sems.at[s, 0]),
                pltpu.make_async_copy(w3_ref.at[e], wf3.at[s], sems.at[s, 1]),
                pltpu.make_async_copy(w2_ref.at[e], wf2.at[s], sems.at[s, 2]))

    @pl.when(step == 0)
    def _():
        slot_ref[0] = 0
        for cp in weight_copies(be_ref[0], 0):
            cp.start()

    def switch_weights(b):
        @pl.when(first_ref[b] == 1)
        def _():
            s = slot_ref[0]
            for cp in weight_copies(be_ref[b], s):
                cp.wait()

            @pl.when(nxt_ref[b] >= 0)
            def _():
                for cp in weight_copies(nxt_ref[b], 1 - s):
                    cp.start(priority=1)

            w1b[...] = wf1[s].astype(BF16)
            w3b[...] = wf3[s].astype(BF16)
            w2b[...] = wf2[s].astype(BF16)
            slot_ref[0] = 1 - s

    def ffn(xq, oq, m, nv):
        live = lax.broadcasted_iota(I32, (m, 1), 0) < nv
        lo, hi = _unpack_halves(jnp.where(live, _load_row_tiled(xq, m), jnp.uint32(0)))
        x = jnp.concatenate([lo.astype(BF16), hi.astype(BF16)], axis=1)
        hdn = _silu(_dot(x, w1b[...])) * _dot(x, w3b[...])
        _store_row_tiled(oq, _pack_halves(_dot(hdn.astype(BF16), w2b[...])))

    def one_block(b, xq, oq):
        nv = nv_ref[b]

        @pl.when(nv > 0)
        def _():
            ffn(xq, oq, MOE_BLK, nv)

        @pl.when(nv <= 0)
        def _():
            oq[...] = jnp.zeros_like(oq)

    assert BLK_PER_STEP == 2
    blk_rows = MOE_BLK * ROW_TILE
    b0 = step * BLK_PER_STEP
    b1 = b0 + 1
    lo_rows = pl.ds(0, blk_rows)
    hi_rows = pl.ds(blk_rows, blk_rows)
    same = (first_ref[b1] == 0) & (nv_ref[b1] > 0)
    switch_weights(b0)

    @pl.when(same)
    def _():
        ffn(x_ref, o_ref, 2 * MOE_BLK, MOE_BLK + nv_ref[b1])

    @pl.when(jnp.logical_not(same))
    def _():
        one_block(b0, x_ref.at[lo_rows, :], o_ref.at[lo_rows, :])
        switch_weights(b1)
        one_block(b1, x_ref.at[hi_rows, :], o_ref.at[hi_rows, :])


def _experts(block_e, block_nv, block_first, block_next, x_sorted, w1, w3, w2):
    n_rows = x_sorted.shape[0]
    step_rows = BLK_PER_STEP * MOE_BLK * ROW_TILE
    row = lambda b, *_: (b, 0)
    return pl.pallas_call(
        _expert_kernel,
        grid_spec=pltpu.PrefetchScalarGridSpec(
            num_scalar_prefetch=4,
            grid=(n_rows // step_rows,),
            in_specs=[
                pl.BlockSpec((step_rows, LANES), row),
                pl.BlockSpec(memory_space=pl.ANY),
                pl.BlockSpec(memory_space=pl.ANY),
                pl.BlockSpec(memory_space=pl.ANY),
            ],
            out_specs=pl.BlockSpec((step_rows, LANES), row),
            scratch_shapes=[
                pltpu.VMEM((N_WSLOTS, D_MODEL, EXPERT_DIM), F32),
                pltpu.VMEM((N_WSLOTS, D_MODEL, EXPERT_DIM), F32),
                pltpu.VMEM((N_WSLOTS, EXPERT_DIM, D_MODEL), F32),
                pltpu.VMEM((D_MODEL, EXPERT_DIM), BF16),
                pltpu.VMEM((D_MODEL, EXPERT_DIM), BF16),
                pltpu.VMEM((EXPERT_DIM, D_MODEL), BF16),
                pltpu.SMEM((1,), I32),
                pltpu.SemaphoreType.DMA((N_WSLOTS, 3)),
            ],
        ),
        out_shape=jax.ShapeDtypeStruct((n_rows, LANES), U32),
        compiler_params=_cparams(("arbitrary",), vmem_mb=58),
        name="experts",
    )(block_e, block_nv, block_first, block_next, x_sorted, w1, w3, w2)


TM_FN = 256


CH_FN = 8


def _final_kernel(alpha, dest_ref, ys_ref, w_ref, h_ref, sh_ref, g_ref, b_ref, o_ref, buf, sems):
    i = pl.program_id(0)
    n_tiles = pl.num_programs(0)
    t_total = n_tiles * TM_FN
    slot = i % 2
    nslot = 1 - slot
    half = D_MODEL // 2

    def issue_token(tile, sl, c, tt):
        tok = c * CH_FN + tt
        row0 = pl.multiple_of(tok * ROW_TILE, ROW_TILE)
        for k in range(TOP_K):
            src = ys_ref.at[dest_ref[k * t_total + tile * TM_FN + tok]]
            pltpu.make_async_copy(src, buf.at[sl, k, pl.ds(row0, ROW_TILE), :], sems.at[sl]).start(priority=k % 2)

    def wait_tile(sl):
        for k in range(TOP_K):
            pltpu.make_async_copy(ys_ref.at[pl.ds(0, TM_FN)], buf.at[sl, k].reshape(TM_FN, ROW_TILE, LANES),
                                  sems.at[sl]).wait()

    @pl.when(i == 0)
    def _():
        def first(c, carry):
            for tt in range(CH_FN):
                issue_token(0, 0, c, tt)
            return carry
        lax.fori_loop(0, TM_FN // CH_FN, first, 0)

    nxt_tile = jnp.minimum(i + 1, n_tiles - 1)

    def combine_from(cur, nxt):
        wait_tile(cur)

        def body(c, carry):
            tok = pl.ds(pl.multiple_of(c * CH_FN, CH_FN), CH_FN)
            acc_lo = sh_ref[tok, :half]
            acc_hi = sh_ref[tok, half:]
            pre = alpha * h_ref[tok, :]
            wrow = w_ref[tok, :]
            for k in range(TOP_K):
                words = jnp.concatenate(
                    [buf[cur, k, pl.ds(c * (CH_FN * ROW_TILE) + r, CH_FN, stride=ROW_TILE), :]
                     for r in range(ROW_TILE)], axis=1)
                issue_token(nxt_tile, nxt, c, k)
                lo, hi = _unpack_halves(words)
                wk = wrow[:, k:k + 1]
                acc_lo = acc_lo + wk * lo
                acc_hi = acc_hi + wk * hi
            o_ref[tok, :] = pre + jnp.concatenate([acc_lo, acc_hi], axis=1)
            return carry

        lax.fori_loop(0, TM_FN // CH_FN, body, 0)

        @pl.when(i == n_tiles - 1)
        def _():
            wait_tile(nxt)

    for parity in range(2):
        @pl.when(slot == parity)
        def _():
            combine_from(parity, 1 - parity)

    o_ref[...] = _layer_norm(o_ref[...], g_ref[...], b_ref[...])


def _final(dest_flat, ys, w_tok, h1, shared, g, b, alpha):
    t = h1.shape[0]
    row = lambda i, d: (i, 0)
    const = lambda i, d: (0, 0)
    return pl.pallas_call(
        functools.partial(_final_kernel, alpha),
        grid_spec=pltpu.PrefetchScalarGridSpec(
            num_scalar_prefetch=1,
            grid=(t // TM_FN,),
            in_specs=[
                pl.BlockSpec(memory_space=pl.ANY),
                pl.BlockSpec((TM_FN, TOP_K), row),
                pl.BlockSpec((TM_FN, D_MODEL), row),
                pl.BlockSpec((TM_FN, D_MODEL), row),
                pl.BlockSpec((1, D_MODEL), const),
                pl.BlockSpec((1, D_MODEL), const),
            ],
            out_specs=pl.BlockSpec((TM_FN, D_MODEL), row),
            scratch_shapes=[pltpu.VMEM((2, TOP_K, TM_FN * ROW_TILE, LANES), U32), pltpu.SemaphoreType.DMA((2,))],
        ),
        out_shape=jax.ShapeDtypeStruct((t, D_MODEL), F32),
        compiler_params=_cparams(("arbitrary",)),
        name="combine_ln2",
    )(dest_flat, ys, w_tok, h1, shared, g, b)


TM_INB = 512
TN_INB = W_B // 2


def _pad_lanes(v):
    return jnp.pad(v, (0, LANES - v.shape[0]))[None, :].astype(F32)


def _layer(h2d, mem2d, batch, seq, w_in_all, layer, conv_w, conv_b, dt_bias, a_log, d_skip, ssd_norm_g, swa_sinks,
           rel_bias, w_mem_kv, w_ssd_o, w_swa_o, w_xa_o, w_out, ln1_g, ln1_b, router_w, router_bias, w1, w3, w2, ws1,
           ws3, ws2, ln2_g, ln2_b, alpha):
    t = batch * seq
    w_t = jnp.swapaxes(w_in_all, 1, 2)
    wdt = jnp.pad(w_t[layer, W_A:W_A + SSD_HEADS, :].T, ((0, 0), (0, LANES - SSD_HEADS))).astype(BF16)
    pa, dt_raw, xb = _inproj(h2d, _wcast(w_t, layer, 0, W_A, "wcast_a"), wdt)
    pb = _mm(xb, _wcast(w_t, layer, W_A + SSD_HEADS, W_B, "wcast_b"), BF16, TM_INB, TN_INB, "inproj_b", vmem_mb=56)

    y_ssd = _ssd(
        pa, dt_raw,
        conv_w[:, :SSD_D_INNER], conv_b[None, :SSD_D_INNER], conv_w[:, SSD_D_INNER:], conv_b[None, SSD_D_INNER:],
        _pad_lanes(dt_bias), _pad_lanes(a_log),
        jnp.repeat(d_skip, SSD_HEAD_DIM)[None, :], ssd_norm_g[None, :], batch, seq)

    y_swa = _swa(pb, swa_sinks.astype(F32), _swa_bias_table(rel_bias), batch, seq)

    memkv = _mm(mem2d, w_mem_kv, BF16, batch * MEM_LEN, 1024, "mem_kv")
    y_xa = _xa(pb, memkv, batch, seq)

    merged = _merge(y_ssd, y_swa, y_xa, w_ssd_o, w_swa_o, w_xa_o, pb)
    h1, hp = _out_ln(merged, w_out.astype(BF16), h2d, ln1_g[None, :], ln1_b[None, :], alpha)

    rwt = router_w.T
    r_hi = rwt.astype(BF16)
    r_lo = (rwt - r_hi.astype(F32)).astype(BF16)
    ti = jnp.arange(TM_RT)
    su = (ti[:, None] < ti[None, :]).astype(BF16)
    idx_t, w_t, rank_t, cnt = _router(h1, r_hi, r_lo, router_bias[:, None].astype(F32), su)

    counts = cnt[:, 0].astype(I32)
    padded = (counts + MOE_BLK - 1) // MOE_BLK * MOE_BLK
    pend = jnp.cumsum(padded)
    pstart = pend - padded
    n_blocks = t * TOP_K // MOE_BLK + N_EXPERTS
    blk_lo = jnp.arange(n_blocks, dtype=I32) * MOE_BLK
    block_e = jnp.minimum(jnp.sum((pend[None, :] <= blk_lo[:, None]).astype(I32), axis=1), N_EXPERTS - 1)
    of_e = block_e[:, None] == jnp.arange(N_EXPERTS, dtype=I32)[None, :]
    valid_end = jnp.sum(jnp.where(of_e, (pstart + counts)[None, :], 0), axis=1)
    block_nv = jnp.clip(valid_end - blk_lo, 0, MOE_BLK).astype(I32)
    eids = jnp.arange(N_EXPERTS, dtype=I32)
    later_active = (eids[None, :] > eids[:, None]) & (padded > 0)[None, :]
    next_active = jnp.min(jnp.where(later_active, eids[None, :], N_EXPERTS), axis=1)
    next_active = jnp.where(next_active >= N_EXPERTS, -1, next_active)
    seg_start = jnp.sum(jnp.where(of_e, pstart[None, :], 0), axis=1)
    block_first = ((blk_lo == seg_start) & (block_nv > 0)).astype(I32)
    block_next = jnp.sum(jnp.where(of_e, next_active[None, :], 0), axis=1).astype(I32)

    dest_flat = _dest(idx_t, rank_t, pstart.astype(F32)[:, None]).reshape(-1)
    n_slots = n_blocks * MOE_BLK
    x_sorted, shared = _dispatch(dest_flat, hp, h1, ws1.astype(BF16), ws3.astype(BF16), ws2.astype(BF16),
                                 n_slots)
    ys = _experts(block_e, block_nv, block_first, block_next,
                  x_sorted.reshape(n_slots * ROW_TILE, LANES), w1, w3, w2)
    return _final(dest_flat, ys.reshape(n_slots, ROW_TILE, LANES), w_t.T, h1, shared, ln2_g[None, :],
                  ln2_b[None, :], alpha)


def kernel(x, mem, w_in, conv_w, conv_b, dt_bias, a_log, d_skip, ssd_norm_g, swa_sinks, rel_bias, w_mem_kv, w_ssd_o, w_swa_o, w_xa_o, w_out, ln1_g, ln1_b, router_w, router_bias, w1, w3, w2, ws1, ws3, ws2, ln2_g, ln2_b):
    batch, seq, d = x.shape
    depth = w_in.shape[0]
    alpha = (2.0 * depth) ** 0.25
    h = x.reshape(batch * seq, d)
    mem2d = mem.reshape(batch * MEM_LEN, d)
    for i in range(depth):
        h = _layer(h, mem2d, batch, seq, w_in, i, conv_w[i], conv_b[i], dt_bias[i], a_log[i], d_skip[i],
                   ssd_norm_g[i], swa_sinks[i], rel_bias, w_mem_kv[i], w_ssd_o[i], w_swa_o[i], w_xa_o[i], w_out[i],
                   ln1_g[i], ln1_b[i], router_w[i], router_bias[i], w1[i], w3[i], w2[i], ws1[i], ws3[i], ws2[i],
                   ln2_g[i], ln2_b[i], alpha)
    return h.reshape(batch, seq, d)
```

```python
import functools
import math

import jax
import jax.numpy as jnp
from jax import lax
from jax.experimental import pallas as pl
from jax.experimental.pallas import tpu as pltpu

F32 = jnp.float32
BF16 = jnp.bfloat16
I32 = jnp.int32
U32 = jnp.uint32

D_MODEL = 2048
MEM_LEN = 256
SSD_D_INNER = D_MODEL
SSD_HEAD_DIM = 64
SSD_HEADS = 32
SSD_GROUPS = 4
SSD_STATE = 128
SSD_CONV = 4
SSD_CHUNK = 128
SWA_HEADS = 16
SWA_KV_HEADS = 4
SWA_HEAD_DIM = 64
SWA_WINDOW = 128
SWA_BLOCK = 128
REL_BUCKETS = 32
REL_MAX_DIST = 128
XA_HEADS = 4
XA_HEAD_DIM = 256
N_BRANCH = 3
N_EXPERTS = 64
TOP_K = 8
N_EXPERT_GROUPS = 8
TOPK_GROUPS = 4
EXPERT_DIM = 512
SHARED_DIM = 512
ROUTED_SCALE = 2.5
LN_EPS = 1e-5
RMS_EPS = 1e-5

SWA_Q_DIM = SWA_HEADS * SWA_HEAD_DIM
SWA_KV_DIM = SWA_KV_HEADS * SWA_HEAD_DIM
XA_DIM = XA_HEADS * XA_HEAD_DIM
BC_DIM = 2 * SSD_GROUPS * SSD_STATE

W_A = SSD_D_INNER + SSD_D_INNER + BC_DIM
OFFA_Z = 0
OFFA_XS = OFFA_Z + SSD_D_INNER
OFFA_BC = OFFA_XS + SSD_D_INNER
W_B = SWA_Q_DIM + 2 * SWA_KV_DIM + XA_DIM + N_BRANCH * D_MODEL
OFFB_Q = 0
OFFB_K = OFFB_Q + SWA_Q_DIM
OFFB_V = OFFB_K + SWA_KV_DIM
OFFB_QX = OFFB_V + SWA_KV_DIM
OFFB_GATES = OFFB_QX + XA_DIM

LANES = 128
MOE_BLK = 256
NEG_INF = float("-inf")


def _cparams(sem, vmem_mb=48):
    return pltpu.CompilerParams(dimension_semantics=sem, vmem_limit_bytes=vmem_mb * 1024 * 1024)


def _dot(a, b):
    return jnp.dot(a, b, preferred_element_type=F32)


def _dot_nt(a, b):
    return lax.dot_general(a, b, (((1,), (1,)), ((), ())), preferred_element_type=F32)


def _sigmoid(x):
    return 1.0 / (1.0 + jnp.exp(-x))


def _silu(x):
    return x * _sigmoid(x)


TM_IN = 512
TN_IN = W_A // 2


def _inproj_kernel(x_ref, w_ref, wdt_ref, p_ref, dt_ref, xb_ref):
    @pl.when(pl.program_id(1) == 0)
    def _():
        xb = x_ref[...].astype(BF16)
        xb_ref[...] = xb
        dt_ref[...] = _dot(xb, wdt_ref[...])

    p_ref[...] = _dot(xb_ref[...], w_ref[...].astype(BF16)).astype(BF16)


def _inproj(x2d, w_in, wdt):
    t = x2d.shape[0]
    return pl.pallas_call(
        _inproj_kernel,
        grid=(t // TM_IN, W_A // TN_IN),
        in_specs=[
            pl.BlockSpec((TM_IN, D_MODEL), lambda i, j: (i, 0)),
            pl.BlockSpec((D_MODEL, TN_IN), lambda i, j: (0, j)),
            pl.BlockSpec((D_MODEL, LANES), lambda i, j: (0, 0)),
        ],
        out_specs=[
            pl.BlockSpec((TM_IN, TN_IN), lambda i, j: (i, j)),
            pl.BlockSpec((TM_IN, LANES), lambda i, j: (i, 0)),
            pl.BlockSpec((TM_IN, D_MODEL), lambda i, j: (i, 0)),
        ],
        out_shape=[jax.ShapeDtypeStruct((t, W_A), BF16), jax.ShapeDtypeStruct((t, LANES), F32),
                   jax.ShapeDtypeStruct((t, D_MODEL), BF16)],
        compiler_params=_cparams(("parallel", "arbitrary"), vmem_mb=56),
        name="inproj",
    )(x2d, w_in, wdt)


TN_WC = 512
ROW_ALIGN = 32


def _wcast_kernel(a_ref, o_ref):
    o_ref[...] = a_ref[0].T.astype(BF16)


def _wcast(w_t, layer, start, width, name):
    k = w_t.shape[2]
    return pl.pallas_call(
        _wcast_kernel,
        grid=(width // TN_WC,),
        in_specs=[pl.BlockSpec((pl.Element(1), pl.Element(TN_WC), pl.Element(k)),
                               lambda j: (layer, pl.multiple_of(start + j * TN_WC, ROW_ALIGN), 0))],
        out_specs=pl.BlockSpec((k, TN_WC), lambda j: (0, j)),
        out_shape=jax.ShapeDtypeStruct((k, width), BF16),
        compiler_params=_cparams(("parallel",)),
        name=name,
    )(w_t)


def _mm_kernel(a_ref, b_ref, o_ref):
    o_ref[...] = _dot(a_ref[...].astype(BF16), b_ref[...].astype(BF16)).astype(o_ref.dtype)


def _mm(a, b, out_dtype, tm, tn, name, vmem_mb=48):
    m, k = a.shape
    n = b.shape[1]
    return pl.pallas_call(
        _mm_kernel,
        grid=(m // tm, n // tn),
        in_specs=[pl.BlockSpec((tm, k), lambda i, j: (i, 0)), pl.BlockSpec((k, tn), lambda i, j: (0, j))],
        out_specs=pl.BlockSpec((tm, tn), lambda i, j: (i, j)),
        out_shape=jax.ShapeDtypeStruct((m, n), out_dtype),
        compiler_params=_cparams(("parallel", "parallel"), vmem_mb=vmem_mb),
        name=name,
    )(a, b)


Q = SSD_CHUNK


def _conv_shift_matrix():
    t = jnp.arange(Q)[:, None]
    j = jnp.arange(2 * Q)[None, :]
    blocks = []
    for k in range(SSD_CONV - 1):
        src = t - (SSD_CONV - 1 - k)
        blocks.append(j == jnp.where(src >= 0, src, 2 * Q + src))
    return jnp.concatenate(blocks, axis=0).astype(BF16)


def _ssd_kernel(xs_ref, bc_ref, z_ref, dtr_ref, shift_ref, cwx_ref, cbx_ref, cwb_ref, cbb_ref, dtb_ref, alog_ref,
                dsk_ref, ng_ref, o_ref, hx_ref, hb_ref, st_ref, cumt_ref, dtt_ref, wt_ref, y_ref):
    c = pl.program_id(1)

    @pl.when(c == 0)
    def _():
        hx_ref[...] = jnp.zeros_like(hx_ref)
        hb_ref[...] = jnp.zeros_like(hb_ref)
        st_ref[...] = jnp.zeros_like(st_ref)

    def conv_silu(in_ref, prev_ref, w_ref, b_ref):
        cur = in_ref[...]
        ext = jnp.concatenate([cur, prev_ref[...]], axis=0)
        shifted = _dot(shift_ref[...], ext)
        acc = b_ref[...] + w_ref[SSD_CONV - 1:SSD_CONV, :] * cur.astype(F32)
        for k in range(SSD_CONV - 1):
            acc = acc + w_ref[k:k + 1, :] * shifted[k * Q:(k + 1) * Q, :]
        prev_ref[...] = cur
        return _silu(acc)

    xs_c = conv_silu(xs_ref, hx_ref, cwx_ref, cbx_ref)
    bc_c = conv_silu(bc_ref, hb_ref, cwb_ref, cbb_ref)

    lane = lax.broadcasted_iota(I32, (Q, LANES), 1)
    row = lax.broadcasted_iota(I32, (Q, LANES), 0)
    tril = row >= lane
    lo_mask = lane < SSD_HEAD_DIM

    dtv = dtr_ref[...] + dtb_ref[...]
    dt = jnp.maximum(dtv, 0.0) + jnp.log1p(jnp.exp(-jnp.abs(dtv)))
    a = jnp.where(lane[0:1, :] < SSD_HEADS, -jnp.exp(alog_ref[...]), 0.0)
    la = dt * a
    tri = jnp.where(tril, 1.0, 0.0).astype(BF16)
    p1 = la.astype(BF16)
    r1 = la - p1.astype(F32)
    p2 = r1.astype(BF16)
    p3 = (r1 - p2.astype(F32)).astype(BF16)
    cum = _dot(tri, p1) + _dot(tri, p2) + _dot(tri, p3)
    ecum = jnp.exp(cum)
    cum_t = cum.T
    dt_t = dt.T
    last = cum_t[:, Q - 1:Q]
    cumt_ref[...] = cum_t
    dtt_ref[...] = dt_t
    wt_ref[...] = jnp.exp(last - cum_t) * dt_t
    chunk_decay = jnp.exp(last)

    def head_parts(h, cb, cm, bm_t):
        col = jnp.broadcast_to(cum[:, h:h + 1], (Q, Q))
        rw = cumt_ref[h:h + 1, :]
        dec = jnp.exp(jnp.where(tril, col - rw, NEG_INF))
        m = cb * (dec * dtt_ref[h:h + 1, :])
        ce = cm * jnp.broadcast_to(ecum[:, h:h + 1], (Q, Q))
        sl = bm_t * wt_ref[h:h + 1, :]
        return m.astype(BF16), ce.astype(BF16), sl.astype(BF16)

    for g in range(SSD_GROUPS):
        bm = bc_c[:, g * SSD_STATE:(g + 1) * SSD_STATE]
        cm = bc_c[:, BC_DIM // 2 + g * SSD_STATE:BC_DIM // 2 + (g + 1) * SSD_STATE]
        cb = _dot_nt(cm.astype(BF16), bm.astype(BF16))
        bm_t = bm.T
        for qd in range(SSD_HEADS // SSD_GROUPS // 2):
            hp = g * (SSD_HEADS // SSD_GROUPS // 2) + qd
            ha, hb = 2 * hp, 2 * hp + 1
            sl_ = slice(hp * LANES, (hp + 1) * LANES)
            xs_pair = xs_c[:, sl_]
            x_lo = jnp.where(lo_mask, xs_pair, 0.0).astype(BF16)
            x_hi = jnp.where(lo_mask, 0.0, xs_pair).astype(BF16)
            h_prev = st_ref[:, sl_]
            h_lo = jnp.where(lo_mask, h_prev, 0.0).astype(BF16)
            h_hi = jnp.where(lo_mask, 0.0, h_prev).astype(BF16)
            m_a, ce_a, sl_a = head_parts(ha, cb, cm, bm_t)
            m_b, ce_b, sl_b = head_parts(hb, cb, cm, bm_t)
            lhs = jnp.concatenate([m_a, ce_a, m_b, ce_b], axis=1)
            rhs = jnp.concatenate([x_lo, h_lo, x_hi, h_hi], axis=0)
            y_ref[:, sl_] = _dot(lhs, rhs)
            new = _dot(jnp.concatenate([sl_a, sl_b], axis=1), jnp.concatenate([x_lo, x_hi], axis=0))
            cd = jnp.where(lo_mask[0:1, :], chunk_decay[ha:ha + 1, :], chunk_decay[hb:hb + 1, :])
            st_ref[:, sl_] = h_prev * cd + new

    y = y_ref[...] + xs_c * dsk_ref[...]
    y = y * _silu(z_ref[...].astype(F32))
    gw = SSD_D_INNER // SSD_GROUPS
    for g in range(SSD_GROUPS):
        yg = y[:, g * gw:(g + 1) * gw]
        ms = jnp.mean(yg * yg, axis=-1, keepdims=True)
        o_ref[:, g * gw:(g + 1) * gw] = ((yg * lax.rsqrt(ms + RMS_EPS)) * ng_ref[:, g * gw:(g + 1) * gw]).astype(BF16)


def _ssd(p, dt_raw, cwx, cbx, cwb, cbb, dtb, alog, dsk, ng, batch, seq):
    nc = seq // Q
    t = batch * seq
    rowblk = lambda b, c: b * nc + c
    const = lambda b, c: (0, 0)
    return pl.pallas_call(
        _ssd_kernel,
        grid=(batch, nc),
        in_specs=[
            pl.BlockSpec((Q, SSD_D_INNER), lambda b, c: (rowblk(b, c), OFFA_XS // SSD_D_INNER)),
            pl.BlockSpec((Q, BC_DIM), lambda b, c: (rowblk(b, c), OFFA_BC // BC_DIM)),
            pl.BlockSpec((Q, SSD_D_INNER), lambda b, c: (rowblk(b, c), OFFA_Z // SSD_D_INNER)),
            pl.BlockSpec((Q, LANES), lambda b, c: (rowblk(b, c), 0)),
            pl.BlockSpec(((SSD_CONV - 1) * Q, 2 * Q), const),
            pl.BlockSpec((SSD_CONV, SSD_D_INNER), const),
            pl.BlockSpec((1, SSD_D_INNER), const),
            pl.BlockSpec((SSD_CONV, BC_DIM), const),
            pl.BlockSpec((1, BC_DIM), const),
            pl.BlockSpec((1, LANES), const),
            pl.BlockSpec((1, LANES), const),
            pl.BlockSpec((1, SSD_D_INNER), const),
            pl.BlockSpec((1, SSD_D_INNER), const),
        ],
        out_specs=pl.BlockSpec((Q, SSD_D_INNER), lambda b, c: (rowblk(b, c), 0)),
        out_shape=jax.ShapeDtypeStruct((t, SSD_D_INNER), BF16),
        scratch_shapes=[
            pltpu.VMEM((Q, SSD_D_INNER), BF16),
            pltpu.VMEM((Q, BC_DIM), BF16),
            pltpu.VMEM((SSD_STATE, SSD_D_INNER), F32),
            pltpu.VMEM((LANES, Q), F32),
            pltpu.VMEM((LANES, Q), F32),
            pltpu.VMEM((LANES, Q), F32),
            pltpu.VMEM((Q, SSD_D_INNER), F32),
        ],
        compiler_params=_cparams(("parallel", "arbitrary")),
        name="ssd",
    )(p, p, p, dt_raw, _conv_shift_matrix(), cwx, cbx, cwb, cbb, dtb, alog, dsk, ng)


N_PAIRS = SWA_HEADS // 2


def _swa_kernel(sink_ref, q_ref, kp_ref, kc_ref, vp_ref, vc_ref, bias_ref, o_ref):
    n = pl.program_id(1)
    blk = SWA_BLOCK
    kband = jnp.concatenate([kp_ref[...], kc_ref[...]], axis=0)
    vband = jnp.concatenate([vp_ref[...], vc_ref[...]], axis=0)

    lane = lax.broadcasted_iota(I32, (1, LANES), 1)
    lo_f = jnp.where(lane < SWA_HEAD_DIM, 1.0, 0.0)
    lo_m = lo_f.astype(BF16)
    hi_m = (1.0 - lo_f).astype(BF16)

    def head_twice(band, kv):
        tile = band[:, (kv // 2) * LANES:(kv // 2 + 1) * LANES]
        swapped = jnp.concatenate([tile[:, SWA_HEAD_DIM:], tile[:, :SWA_HEAD_DIM]], axis=1)
        return tile * lo_m + swapped * hi_m if kv % 2 == 0 else swapped * lo_m + tile * hi_m

    k_twice = [head_twice(kband, kv) for kv in range(SWA_KV_HEADS)]
    v_twice = [head_twice(vband, kv) for kv in range(SWA_KV_HEADS)]
    rowi = lax.broadcasted_iota(I32, (2 * blk, 1), 0)
    scale = SWA_HEAD_DIM ** -0.5
    assert math.frexp(scale)[0] == 0.5
    lo_q = (lo_f * scale).astype(BF16)
    hi_q = ((1.0 - lo_f) * scale).astype(BF16)
    for p in range(N_PAIRS):
        kv = p // (SWA_HEADS // SWA_KV_HEADS // 2)
        qp = q_ref[:, p * LANES:(p + 1) * LANES]
        qs = jnp.concatenate([qp * lo_q, qp * hi_q], axis=0)
        s = _dot_nt(qs, k_twice[kv]) + bias_ref[0, p]
        sink = jnp.where(rowi < blk, sink_ref[2 * p], sink_ref[2 * p + 1])
        m = jnp.maximum(jnp.max(s, axis=-1, keepdims=True), sink)
        e = jnp.exp(s - m)
        den = jnp.sum(e, axis=-1, keepdims=True) + jnp.exp(sink - m)
        pr = (e * (1.0 / den)).astype(BF16)
        vd = v_twice[kv]
        lhs = jnp.concatenate([pr[:blk], pr[blk:]], axis=1)
        rhs = jnp.concatenate([vd * lo_m, vd * hi_m], axis=0)
        o_ref[:, p * LANES:(p + 1) * LANES] = _dot(lhs, rhs).astype(BF16)


def _swa(p, sinks, bias, batch, seq):
    nb = seq // SWA_BLOCK
    t = batch * seq
    kw = SWA_KV_DIM
    cur = lambda b, n, s: b * nb + n
    prev = lambda b, n, s: b * nb + jnp.maximum(n - 1, 0)
    return pl.pallas_call(
        _swa_kernel,
        grid_spec=pltpu.PrefetchScalarGridSpec(
            num_scalar_prefetch=1,
            grid=(batch, nb),
            in_specs=[
                pl.BlockSpec((SWA_BLOCK, SWA_Q_DIM), lambda b, n, s: (cur(b, n, s), OFFB_Q // SWA_Q_DIM)),
                pl.BlockSpec((SWA_BLOCK, kw), lambda b, n, s: (prev(b, n, s), OFFB_K // kw)),
                pl.BlockSpec((SWA_BLOCK, kw), lambda b, n, s: (cur(b, n, s), OFFB_K // kw)),
                pl.BlockSpec((SWA_BLOCK, kw), lambda b, n, s: (prev(b, n, s), OFFB_V // kw)),
                pl.BlockSpec((SWA_BLOCK, kw), lambda b, n, s: (cur(b, n, s), OFFB_V // kw)),
                pl.BlockSpec((1, N_PAIRS, 2 * SWA_BLOCK, 2 * SWA_BLOCK),
                             lambda b, n, s: (jnp.minimum(n, 1), 0, 0, 0)),
            ],
            out_specs=pl.BlockSpec((SWA_BLOCK, SWA_Q_DIM), lambda b, n, s: (cur(b, n, s), 0)),
        ),
        out_shape=jax.ShapeDtypeStruct((t, SWA_Q_DIM), BF16),
        compiler_params=_cparams(("parallel", "arbitrary")),
        name="swa",
    )(sinks, p, p, p, p, p, bias)


def _t5_causal_bucket(dist):
    max_exact = REL_BUCKETS // 2
    large = max_exact + (jnp.log(jnp.maximum(dist, 1).astype(F32) / max_exact)
                         / math.log(REL_MAX_DIST / max_exact) * (REL_BUCKETS - max_exact)).astype(I32)
    large = jnp.minimum(large, REL_BUCKETS - 1)
    return jnp.where(dist < max_exact, dist, large)


def _swa_bias_table(rel_bias):
    qi = jnp.arange(SWA_BLOCK)[:, None]
    kj = jnp.arange(2 * SWA_BLOCK)[None, :]
    dist = qi + SWA_BLOCK - kj
    in_window = (dist >= 0) & (dist < SWA_WINDOW)
    bucket = _t5_causal_bucket(jnp.maximum(dist, 0))
    onehot = (bucket[:, :, None] == jnp.arange(REL_BUCKETS)[None, None, :]).astype(F32)
    bias = jnp.einsum('ijb,bh->hij', onehot, rel_bias.astype(F32), precision=lax.Precision.HIGHEST)
    rest = jnp.where(in_window[None, :, :], bias, NEG_INF)
    first = jnp.where((kj >= SWA_BLOCK)[None, :, :], rest, NEG_INF)
    return jnp.stack([first, rest]).reshape(2, N_PAIRS, 2 * SWA_BLOCK, 2 * SWA_BLOCK)


TL_XA = 512


XA_HALF = XA_DIM // 2


def _xa_kernel(qa_ref, qb_ref, mk_ref, mv_ref, o_ref):
    for h in range(XA_HEADS):
        q_ref = qa_ref if h < XA_HEADS // 2 else qb_ref
        qc = (h % (XA_HEADS // 2)) * XA_HEAD_DIM
        cols = slice(h * XA_HEAD_DIM, (h + 1) * XA_HEAD_DIM)
        s = _dot_nt(q_ref[:, qc:qc + XA_HEAD_DIM], mk_ref[:, cols]) * (XA_HEAD_DIM ** -0.5)
        m = jnp.max(s, axis=-1, keepdims=True)
        e = jnp.exp(s - m)
        pr = e * (1.0 / jnp.sum(e, axis=-1, keepdims=True))
        o_ref[:, cols] = _dot(pr.astype(BF16), mv_ref[:, cols]).astype(BF16)


def _xa(p, memkv, batch, seq):
    nl = seq // TL_XA
    t = batch * seq
    return pl.pallas_call(
        _xa_kernel,
        grid=(batch, nl),
        in_specs=[
            pl.BlockSpec((TL_XA, XA_HALF), lambda b, l: (b * nl + l, OFFB_QX // XA_HALF)),
            pl.BlockSpec((TL_XA, XA_HALF), lambda b, l: (b * nl + l, OFFB_QX // XA_HALF + 1)),
            pl.BlockSpec((MEM_LEN, XA_DIM), lambda b, l: (b, 0)),
            pl.BlockSpec((MEM_LEN, XA_DIM), lambda b, l: (b, 1)),
        ],
        out_specs=pl.BlockSpec((TL_XA, XA_DIM), lambda b, l: (b * nl + l, 0)),
        out_shape=jax.ShapeDtypeStruct((t, XA_DIM), BF16),
        compiler_params=_cparams(("parallel", "parallel")),
        name="xattn",
    )(p, p, memkv, memkv)


TM_MG = 1024
TN_MG = 512


N_SUB_MG = 2


def _merge_kernel(ys_ref, yw_ref, yx_ref, ws_ref, ww_ref, wx_ref, g0_ref, g1_ref, g2_ref, o_ref):
    sub = TM_MG // N_SUB_MG
    ws = ws_ref[...].astype(BF16)
    ww = ww_ref[...].astype(BF16)
    wx = wx_ref[...].astype(BF16)
    for s in range(N_SUB_MG):
        rows = slice(s * sub, (s + 1) * sub)
        a = _dot(ys_ref[rows, :], ws)
        b = _dot(yw_ref[rows, :], ww)
        c = _dot(yx_ref[rows, :], wx)
        g0 = _sigmoid(g0_ref[rows, :].astype(F32))
        g1 = _sigmoid(g1_ref[rows, :].astype(F32))
        g2 = _sigmoid(g2_ref[rows, :].astype(F32))
        o_ref[rows, :] = (g0 * a + g1 * b + g2 * c).astype(BF16)


def _merge(y_ssd, y_swa, y_xa, w_ssd_o, w_swa_o, w_xa_o, p):
    t = y_ssd.shape[0]
    gpb = D_MODEL // TN_MG
    gate = lambda k: pl.BlockSpec((TM_MG, TN_MG), lambda i, j: (i, OFFB_GATES // TN_MG + k * gpb + j))
    return pl.pallas_call(
        _merge_kernel,
        grid=(t // TM_MG, D_MODEL // TN_MG),
        in_specs=[
            pl.BlockSpec((TM_MG, SSD_D_INNER), lambda i, j: (i, 0)),
            pl.BlockSpec((TM_MG, SWA_Q_DIM), lambda i, j: (i, 0)),
            pl.BlockSpec((TM_MG, XA_DIM), lambda i, j: (i, 0)),
            pl.BlockSpec((SSD_D_INNER, TN_MG), lambda i, j: (0, j)),
            pl.BlockSpec((SWA_Q_DIM, TN_MG), lambda i, j: (0, j)),
            pl.BlockSpec((XA_DIM, TN_MG), lambda i, j: (0, j)),
            gate(0), gate(1), gate(2),
        ],
        out_specs=pl.BlockSpec((TM_MG, TN_MG), lambda i, j: (i, j)),
        out_shape=jax.ShapeDtypeStruct((t, D_MODEL), BF16),
        compiler_params=_cparams(("parallel", "parallel")),
        name="merge",
    )(y_ssd, y_swa, y_xa, w_ssd_o, w_swa_o, w_xa_o, p, p, p)


TM_LN = 512


def _layer_norm(v, g, b):
    mu = jnp.mean(v, axis=-1, keepdims=True)
    d = v - mu
    var = jnp.mean(d * d, axis=-1, keepdims=True)
    return (d * lax.rsqrt(var + LN_EPS)) * g + b


def _pack_halves(v):
    n = v.shape[1] // 2
    lo = lax.bitcast_convert_type(v[:, :n].astype(BF16).astype(F32), U32) >> 16
    hi = lax.bitcast_convert_type(v[:, n:].astype(BF16).astype(F32), U32) & jnp.uint32(0xFFFF0000)
    return hi | lo


def _unpack_halves(w):
    lo = lax.bitcast_convert_type(w << 16, F32)
    hi = lax.bitcast_convert_type(w & jnp.uint32(0xFFFF0000), F32)
    return lo, hi


ROW_WORDS = D_MODEL // 2
ROW_TILE = ROW_WORDS // LANES


def _store_row_tiled(ref, words):
    m = words.shape[0]
    for c in range(ROW_TILE):
        ref[pl.ds(c, m, stride=ROW_TILE), :] = words[:, c * LANES:(c + 1) * LANES]


def _load_row_tiled(ref, m):
    return jnp.concatenate([ref[pl.ds(c, m, stride=ROW_TILE), :] for c in range(ROW_TILE)], axis=1)


N_SUB_LN = 4


def _out_ln_kernel(alpha, m_ref, w_ref, x_ref, g_ref, b_ref, h_ref, hp_ref):
    sub = TM_LN // N_SUB_LN
    for c in range(N_SUB_LN):
        rows = slice(c * sub, (c + 1) * sub)
        mix = _dot(m_ref[rows, :], w_ref[...])
        hn = _layer_norm(alpha * x_ref[rows, :] + mix, g_ref[...], b_ref[...])
        h_ref[rows, :] = hn
        _store_row_tiled(hp_ref.at[pl.ds(c * sub * ROW_TILE, sub * ROW_TILE), :], _pack_halves(hn))


def _out_ln(merged, w_out, x2d, g, b, alpha):
    t = x2d.shape[0]
    row = lambda i: (i, 0)
    const = lambda i: (0, 0)
    return pl.pallas_call(
        functools.partial(_out_ln_kernel, alpha),
        grid=(t // TM_LN,),
        in_specs=[
            pl.BlockSpec((TM_LN, D_MODEL), row),
            pl.BlockSpec((D_MODEL, D_MODEL), const),
            pl.BlockSpec((TM_LN, D_MODEL), row),
            pl.BlockSpec((1, D_MODEL), const),
            pl.BlockSpec((1, D_MODEL), const),
        ],
        out_specs=[pl.BlockSpec((TM_LN, D_MODEL), row), pl.BlockSpec((TM_LN * ROW_TILE, LANES), row)],
        out_shape=[jax.ShapeDtypeStruct((t, D_MODEL), F32), jax.ShapeDtypeStruct((t * ROW_TILE, LANES), U32)],
        compiler_params=_cparams(("parallel",)),
        name="out_ln1",
    )(merged, w_out, x2d, g, b)


TM_RT = 512
EPG = N_EXPERTS // N_EXPERT_GROUPS


def _router_kernel(h_ref, whi_ref, wlo_ref, bias_ref, su_ref, idx_ref, w_ref, rank_ref, cnt_ref, run_ref):
    @pl.when(pl.program_id(0) == 0)
    def _():
        run_ref[...] = jnp.zeros_like(run_ref)

    h = h_ref[...]
    h_hi = h.astype(BF16)
    h_lo = (h - h_hi.astype(F32)).astype(BF16)
    whi = whi_ref[...]
    logits = _dot_nt(whi, h_hi) + _dot_nt(whi, h_lo) + _dot_nt(wlo_ref[...], h_hi)
    scores = _sigmoid(logits)
    sel = scores + bias_ref[...]

    iota_m = lax.broadcasted_iota(I32, (EPG, TM_RT), 0).astype(F32)
    iota_g = lax.broadcasted_iota(I32, (N_EXPERT_GROUPS, TM_RT), 0).astype(F32)
    gs = []
    for g in range(N_EXPERT_GROUPS):
        v = sel[g * EPG:(g + 1) * EPG, :]
        m1 = jnp.max(v, axis=0, keepdims=True)
        i1 = jnp.min(jnp.where(v == m1, iota_m, float(EPG)), axis=0, keepdims=True)
        m2 = jnp.max(jnp.where(iota_m == i1, NEG_INF, v), axis=0, keepdims=True)
        gs.append(m1 + m2)
    gsc = jnp.concatenate(gs, axis=0)
    chosen = jnp.zeros((N_EXPERT_GROUPS, TM_RT), F32)
    for _ in range(TOPK_GROUPS):
        m = jnp.max(gsc, axis=0, keepdims=True)
        ig = jnp.min(jnp.where(gsc == m, iota_g, float(N_EXPERT_GROUPS)), axis=0, keepdims=True)
        hit = iota_g == ig
        chosen = jnp.where(hit, 1.0, chosen)
        gsc = jnp.where(hit, NEG_INF, gsc)
    gmask = jnp.concatenate(
        [jnp.broadcast_to(chosen[g:g + 1, :], (EPG, TM_RT)) for g in range(N_EXPERT_GROUPS)], axis=0)
    masked = jnp.where(gmask > 0.0, sel, NEG_INF)

    iota_e = lax.broadcasted_iota(I32, (N_EXPERTS, TM_RT), 0).astype(F32)
    hits, idxs, ws = [], [], []
    for _ in range(TOP_K):
        m = jnp.max(masked, axis=0, keepdims=True)
        ie = jnp.min(jnp.where(masked == m, iota_e, float(N_EXPERTS)), axis=0, keepdims=True)
        hit = iota_e == ie
        hits.append(hit)
        idxs.append(ie)
        ws.append(jnp.sum(jnp.where(hit, scores, 0.0), axis=0, keepdims=True))
        masked = jnp.where(hit, NEG_INF, masked)
    wsum = ws[0]
    for k in range(1, TOP_K):
        wsum = wsum + ws[k]
    inv = 1.0 / wsum
    idx_ref[...] = jnp.concatenate(idxs, axis=0).astype(I32)
    w_ref[...] = jnp.concatenate([w * inv * ROUTED_SCALE for w in ws], axis=0)

    assign = jnp.zeros((N_EXPERTS, TM_RT), F32)
    for hit in hits:
        assign = jnp.where(hit, 1.0, assign)
    pos = _dot(assign.astype(BF16), su_ref[...]) + run_ref[...]
    rank_ref[...] = jnp.concatenate(
        [jnp.sum(jnp.where(hit, pos, 0.0), axis=0, keepdims=True) for hit in hits], axis=0).astype(I32)
    run = run_ref[...] + jnp.sum(assign, axis=1, keepdims=True)
    run_ref[...] = run
    cnt_ref[...] = run


def _router(h1, r_hi, r_lo, rbias, su):
    t = h1.shape[0]
    tile = lambda i: (0, i)
    const = lambda i: (0, 0)
    return pl.pallas_call(
        _router_kernel,
        grid=(t // TM_RT,),
        in_specs=[
            pl.BlockSpec((TM_RT, D_MODEL), lambda i: (i, 0)),
            pl.BlockSpec((N_EXPERTS, D_MODEL), const),
            pl.BlockSpec((N_EXPERTS, D_MODEL), const),
            pl.BlockSpec((N_EXPERTS, 1), const),
            pl.BlockSpec((TM_RT, TM_RT), const),
        ],
        out_specs=[
            pl.BlockSpec((TOP_K, TM_RT), tile),
            pl.BlockSpec((TOP_K, TM_RT), tile),
            pl.BlockSpec((TOP_K, TM_RT), tile),
            pl.BlockSpec((N_EXPERTS, TM_RT), const),
        ],
        out_shape=[
            jax.ShapeDtypeStruct((TOP_K, t), I32),
            jax.ShapeDtypeStruct((TOP_K, t), F32),
            jax.ShapeDtypeStruct((TOP_K, t), I32),
            jax.ShapeDtypeStruct((N_EXPERTS, TM_RT), F32),
        ],
        scratch_shapes=[pltpu.VMEM((N_EXPERTS, TM_RT), F32)],
        compiler_params=_cparams(("arbitrary",)),
        name="router",
    )(h1, r_hi, r_lo, rbias, su)


TM_DS = 2048


def _dest_kernel(idx_ref, rank_ref, ps_ref, d_ref):
    iota_e = lax.broadcasted_iota(I32, (N_EXPERTS, TM_DS), 0)
    ps = ps_ref[...]
    rows = []
    for k in range(TOP_K):
        hit = iota_e == idx_ref[k:k + 1, :]
        rows.append(jnp.sum(jnp.where(hit, ps, 0.0), axis=0, keepdims=True))
    d_ref[...] = jnp.concatenate(rows, axis=0).astype(I32) + rank_ref[...]


def _dest(idx_t, rank_t, pstart):
    t = idx_t.shape[1]
    tile = lambda i: (0, i)
    return pl.pallas_call(
        _dest_kernel,
        grid=(t // TM_DS,),
        in_specs=[pl.BlockSpec((TOP_K, TM_DS), tile), pl.BlockSpec((TOP_K, TM_DS), tile),
                  pl.BlockSpec((N_EXPERTS, 1), lambda i: (0, 0))],
        out_specs=pl.BlockSpec((TOP_K, TM_DS), tile),
        out_shape=jax.ShapeDtypeStruct((TOP_K, t), I32),
        compiler_params=_cparams(("parallel",)),
        name="dest",
    )(idx_t, rank_t, pstart)


TM_DP = 512


PAD_BITS = tuple(1 << i for i in reversed(range(MOE_BLK.bit_length() - 1)))


def _dispatch_kernel(dest_ref, pad_ref, hp_ref, h_ref, ws1_ref, ws3_ref, ws2_ref, xs_ref, sh_ref, zbuf, sem, zsem):
    t_total = pl.num_programs(0) * TM_DP
    base = pl.program_id(0) * TM_DP

    def pad_copies(act):
        def per_expert(e, carry):
            start = pad_ref[e]
            length = pad_ref[N_EXPERTS + e]
            for bit in PAD_BITS:
                @pl.when((length & bit) != 0)
                def _():
                    off = length - (length & (2 * bit - 1))
                    act(pltpu.make_async_copy(zbuf.at[pl.ds(0, bit)], xs_ref.at[pl.ds(start + off, bit)], zsem))
            return carry

        lax.fori_loop(0, N_EXPERTS, per_expert, 0)

        def per_block(i, carry):
            row = pad_ref[2 * N_EXPERTS] + i * MOE_BLK
            act(pltpu.make_async_copy(zbuf, xs_ref.at[pl.ds(row, MOE_BLK)], zsem))
            return carry

        lax.fori_loop(0, pad_ref[2 * N_EXPERTS + 1], per_block, 0)

    @pl.when(pl.program_id(0) == 0)
    def _():
        zbuf[...] = jnp.zeros_like(zbuf)
        pad_copies(lambda cp: cp.start())

    def issue(t, carry):
        src = hp_ref.at[pl.ds(pl.multiple_of(t * ROW_TILE, ROW_TILE), ROW_TILE), :]
        for k in range(TOP_K):
            slot = dest_ref[k * t_total + base + t]
            pltpu.make_async_copy(src, xs_ref.at[slot], sem).start(priority=k % 2)
        return carry

    lax.fori_loop(0, TM_DP, issue, 0)

    xb = h_ref[...].astype(BF16)
    hdn = _silu(_dot(xb, ws1_ref[...])) * _dot(xb, ws3_ref[...])
    sh_ref[...] = _dot(hdn.astype(BF16), ws2_ref[...])

    for k in range(TOP_K):
        pltpu.make_async_copy(hp_ref.reshape(TM_DP, ROW_TILE, LANES), xs_ref.at[pl.ds(0, TM_DP)], sem).wait()

    @pl.when(pl.program_id(0) == 0)
    def _():
        pad_copies(lambda cp: cp.wait())


def _dispatch(dest_flat, pad_info, hp, h1, ws1, ws3, ws2, n_slots):
    t = h1.shape[0]
    row = lambda i, *_: (i, 0)
    const = lambda i, *_: (0, 0)
    return pl.pallas_call(
        _dispatch_kernel,
        grid_spec=pltpu.PrefetchScalarGridSpec(
            num_scalar_prefetch=2,
            grid=(t // TM_DP,),
            in_specs=[
                pl.BlockSpec((TM_DP * ROW_TILE, LANES), row),
                pl.BlockSpec((TM_DP, D_MODEL), row),
                pl.BlockSpec((D_MODEL, SHARED_DIM), const),
                pl.BlockSpec((D_MODEL, SHARED_DIM), const),
                pl.BlockSpec((SHARED_DIM, D_MODEL), const),
            ],
            out_specs=[pl.BlockSpec(memory_space=pl.ANY), pl.BlockSpec((TM_DP, D_MODEL), row)],
            scratch_shapes=[pltpu.VMEM((MOE_BLK, ROW_TILE, LANES), U32), pltpu.SemaphoreType.DMA,
                            pltpu.SemaphoreType.DMA],
        ),
        out_shape=[jax.ShapeDtypeStruct((n_slots, ROW_TILE, LANES), U32), jax.ShapeDtypeStruct((t, D_MODEL), F32)],
        compiler_params=_cparams(("arbitrary",)),
        name="dispatch",
    )(dest_flat, pad_info, hp, h1, ws1, ws3, ws2)


N_WSLOTS = 2
BLK_PER_STEP = 2


def _expert_kernel(be_ref, nv_ref, first_ref, nxt_ref, x_ref, w1_ref, w3_ref, w2_ref, o_ref,
                   wf1, wf3, wf2, w1b, w3b, w2b, slot_ref, sems):
    step = pl.program_id(0)

    def weight_copies(e, s):
        return (pltpu.make_async_copy(w1_ref.at[e], wf1.at[s], sems.at[s, 0]),
                pltpu.make_async_copy(w3_ref.at[e], wf3.at[s], sems.at[s, 1]),
                pltpu.make_async_copy(w2_ref.at[e], wf2.at[s], sems.at[s, 2]))

    @pl.when(step == 0)
    def _():
        slot_ref[0] = 0
        for cp in weight_copies(be_ref[0], 0):
            cp.start()

    def switch_weights(b):
        @pl.when(first_ref[b] == 1)
        def _():
            s = slot_ref[0]
            for cp in weight_copies(be_ref[b], s):
                cp.wait()

            @pl.when(nxt_ref[b] >= 0)
            def _():
                for cp in weight_copies(nxt_ref[b], 1 - s):
                    cp.start(priority=1)

            w1b[...] = wf1[s].astype(BF16)
            w3b[...] = wf3[s].astype(BF16)
            w2b[...] = wf2[s].astype(BF16)
            slot_ref[0] = 1 - s

    def ffn(xq, oq, m):
        lo, hi = _unpack_halves(_load_row_tiled(xq, m))
        x = jnp.concatenate([lo.astype(BF16), hi.astype(BF16)], axis=1)
        hdn = _silu(_dot(x, w1b[...])) * _dot(x, w3b[...])
        _store_row_tiled(oq, _pack_halves(_dot(hdn.astype(BF16), w2b[...])))

    def one_block(b, xq, oq):
        nv = nv_ref[b]

        @pl.when(nv > 0)
        def _():
            ffn(xq, oq, MOE_BLK)

        @pl.when(nv <= 0)
        def _():
            oq[...] = jnp.zeros_like(oq)

    assert BLK_PER_STEP == 2
    blk_rows = MOE_BLK * ROW_TILE
    b0 = step * BLK_PER_STEP
    b1 = b0 + 1
    lo_rows = pl.ds(0, blk_rows)
    hi_rows = pl.ds(blk_rows, blk_rows)
    same = (first_ref[b1] == 0) & (nv_ref[b1] > 0)
    switch_weights(b0)

    @pl.when(same)
    def _():
        ffn(x_ref, o_ref, 2 * MOE_BLK)

    @pl.when(jnp.logical_not(same))
    def _():
        one_block(b0, x_ref.at[lo_rows, :], o_ref.at[lo_rows, :])
        switch_weights(b1)
        one_block(b1, x_ref.at[hi_rows, :], o_ref.at[hi_rows, :])


def _experts(block_e, block_nv, block_first, block_next, x_sorted, w1, w3, w2):
    n_rows = x_sorted.shape[0]
    step_rows = BLK_PER_STEP * MOE_BLK * ROW_TILE
    row = lambda b, *_: (b, 0)
    return pl.pallas_call(
        _expert_kernel,
        grid_spec=pltpu.PrefetchScalarGridSpec(
            num_scalar_prefetch=4,
            grid=(n_rows // step_rows,),
            in_specs=[
                pl.BlockSpec((step_rows, LANES), row),
                pl.BlockSpec(memory_space=pl.ANY),
                pl.BlockSpec(memory_space=pl.ANY),
                pl.BlockSpec(memory_space=pl.ANY),
            ],
            out_specs=pl.BlockSpec((step_rows, LANES), row),
            scratch_shapes=[
                pltpu.VMEM((N_WSLOTS, D_MODEL, EXPERT_DIM), F32),
                pltpu.VMEM((N_WSLOTS, D_MODEL, EXPERT_DIM), F32),
                pltpu.VMEM((N_WSLOTS, EXPERT_DIM, D_MODEL), F32),
                pltpu.VMEM((D_MODEL, EXPERT_DIM), BF16),
                pltpu.VMEM((D_MODEL, EXPERT_DIM), BF16),
                pltpu.VMEM((EXPERT_DIM, D_MODEL), BF16),
                pltpu.SMEM((1,), I32),
                pltpu.SemaphoreType.DMA((N_WSLOTS, 3)),
            ],
        ),
        out_shape=jax.ShapeDtypeStruct((n_rows, LANES), U32),
        compiler_params=_cparams(("arbitrary",), vmem_mb=58),
        name="experts",
    )(block_e, block_nv, block_first, block_next, x_sorted, w1, w3, w2)


TM_FN = 256


CH_FN = 8


def _final_kernel(alpha, dest_ref, ys_ref, w_ref, h_ref, sh_ref, g_ref, b_ref, o_ref, buf, sems):
    i = pl.program_id(0)
    n_tiles = pl.num_programs(0)
    t_total = n_tiles * TM_FN
    slot = i % 2
    nslot = 1 - slot
    half = D_MODEL // 2

    def issue_token(tile, sl, c, tt):
        tok = c * CH_FN + tt
        row0 = pl.multiple_of(tok * ROW_TILE, ROW_TILE)
        for k in range(TOP_K):
            src = ys_ref.at[dest_ref[k * t_total + tile * TM_FN + tok]]
            pltpu.make_async_copy(src, buf.at[sl, k, pl.ds(row0, ROW_TILE), :], sems.at[sl]).start(priority=k % 2)

    def wait_tile(sl):
        for k in range(TOP_K):
            pltpu.make_async_copy(ys_ref.at[pl.ds(0, TM_FN)], buf.at[sl, k].reshape(TM_FN, ROW_TILE, LANES),
                                  sems.at[sl]).wait()

    @pl.when(i == 0)
    def _():
        def first(c, carry):
            for tt in range(CH_FN):
                issue_token(0, 0, c, tt)
            return carry
        lax.fori_loop(0, TM_FN // CH_FN, first, 0)

    nxt_tile = jnp.minimum(i + 1, n_tiles - 1)

    def combine_from(cur, nxt):
        wait_tile(cur)

        def body(c, carry):
            tok = pl.ds(pl.multiple_of(c * CH_FN, CH_FN), CH_FN)
            acc_lo = sh_ref[tok, :half]
            acc_hi = sh_ref[tok, half:]
            pre = alpha * h_ref[tok, :]
            wrow = w_ref[tok, :]
            for k in range(TOP_K):
                words = jnp.concatenate(
                    [buf[cur, k, pl.ds(c * (CH_FN * ROW_TILE) + r, CH_FN, stride=ROW_TILE), :]
                     for r in range(ROW_TILE)], axis=1)
                issue_token(nxt_tile, nxt, c, k)
                lo, hi = _unpack_halves(words)
                wk = wrow[:, k:k + 1]
                acc_lo = acc_lo + wk * lo
                acc_hi = acc_hi + wk * hi
            o_ref[tok, :] = pre + jnp.concatenate([acc_lo, acc_hi], axis=1)
            return carry

        lax.fori_loop(0, TM_FN // CH_FN, body, 0)

        @pl.when(i == n_tiles - 1)
        def _():
            wait_tile(nxt)

    for parity in range(2):
        @pl.when(slot == parity)
        def _():
            combine_from(parity, 1 - parity)

    o_ref[...] = _layer_norm(o_ref[...], g_ref[...], b_ref[...])


def _final(dest_flat, ys, w_tok, h1, shared, g, b, alpha):
    t = h1.shape[0]
    row = lambda i, d: (i, 0)
    const = lambda i, d: (0, 0)
    return pl.pallas_call(
        functools.partial(_final_kernel, alpha),
        grid_spec=pltpu.PrefetchScalarGridSpec(
            num_scalar_prefetch=1,
            grid=(t // TM_FN,),
            in_specs=[
                pl.BlockSpec(memory_space=pl.ANY),
                pl.BlockSpec((TM_FN, TOP_K), row),
                pl.BlockSpec((TM_FN, D_MODEL), row),
                pl.BlockSpec((TM_FN, D_MODEL), row),
                pl.BlockSpec((1, D_MODEL), const),
                pl.BlockSpec((1, D_MODEL), const),
            ],
            out_specs=pl.BlockSpec((TM_FN, D_MODEL), row),
            scratch_shapes=[pltpu.VMEM((2, TOP_K, TM_FN * ROW_TILE, LANES), U32), pltpu.SemaphoreType.DMA((2,))],
        ),
        out_shape=jax.ShapeDtypeStruct((t, D_MODEL), F32),
        compiler_params=_cparams(("arbitrary",)),
        name="combine_ln2",
    )(dest_flat, ys, w_tok, h1, shared, g, b)


TM_INB = 512
TN_INB = W_B // 2


def _pad_lanes(v):
    return jnp.pad(v, (0, LANES - v.shape[0]))[None, :].astype(F32)


def _layer(h2d, mem2d, batch, seq, w_in_all, layer, conv_w, conv_b, dt_bias, a_log, d_skip, ssd_norm_g, swa_sinks,
           rel_bias, w_mem_kv, w_ssd_o, w_swa_o, w_xa_o, w_out, ln1_g, ln1_b, router_w, router_bias, w1, w3, w2, ws1,
           ws3, ws2, ln2_g, ln2_b, alpha):
    t = batch * seq
    w_t = jnp.swapaxes(w_in_all, 1, 2)
    wdt = jnp.pad(w_t[layer, W_A:W_A + SSD_HEADS, :].T, ((0, 0), (0, LANES - SSD_HEADS))).astype(BF16)
    pa, dt_raw, xb = _inproj(h2d, _wcast(w_t, layer, 0, W_A, "wcast_a"), wdt)
    pb = _mm(xb, _wcast(w_t, layer, W_A + SSD_HEADS, W_B, "wcast_b"), BF16, TM_INB, TN_INB, "inproj_b", vmem_mb=56)

    y_ssd = _ssd(
        pa, dt_raw,
        conv_w[:, :SSD_D_INNER], conv_b[None, :SSD_D_INNER], conv_w[:, SSD_D_INNER:], conv_b[None, SSD_D_INNER:],
        _pad_lanes(dt_bias), _pad_lanes(a_log),
        jnp.repeat(d_skip, SSD_HEAD_DIM)[None, :], ssd_norm_g[None, :], batch, seq)

    y_swa = _swa(pb, swa_sinks.astype(F32), _swa_bias_table(rel_bias), batch, seq)

    memkv = _mm(mem2d, w_mem_kv, BF16, batch * MEM_LEN, 1024, "mem_kv")
    y_xa = _xa(pb, memkv, batch, seq)

    merged = _merge(y_ssd, y_swa, y_xa, w_ssd_o, w_swa_o, w_xa_o, pb)
    h1, hp = _out_ln(merged, w_out.astype(BF16), h2d, ln1_g[None, :], ln1_b[None, :], alpha)

    rwt = router_w.T
    r_hi = rwt.astype(BF16)
    r_lo = (rwt - r_hi.astype(F32)).astype(BF16)
    ti = jnp.arange(TM_RT)
    su = (ti[:, None] < ti[None, :]).astype(BF16)
    idx_t, w_t, rank_t, cnt = _router(h1, r_hi, r_lo, router_bias[:, None].astype(F32), su)

    counts = cnt[:, 0].astype(I32)
    padded = (counts + MOE_BLK - 1) // MOE_BLK * MOE_BLK
    pend = jnp.cumsum(padded)
    pstart = pend - padded
    n_blocks = t * TOP_K // MOE_BLK + N_EXPERTS
    blk_lo = jnp.arange(n_blocks, dtype=I32) * MOE_BLK
    block_e = jnp.minimum(jnp.sum((pend[None, :] <= blk_lo[:, None]).astype(I32), axis=1), N_EXPERTS - 1)
    of_e = block_e[:, None] == jnp.arange(N_EXPERTS, dtype=I32)[None, :]
    valid_end = jnp.sum(jnp.where(of_e, (pstart + counts)[None, :], 0), axis=1)
    block_nv = jnp.clip(valid_end - blk_lo, 0, MOE_BLK).astype(I32)
    eids = jnp.arange(N_EXPERTS, dtype=I32)
    later_active = (eids[None, :] > eids[:, None]) & (padded > 0)[None, :]
    next_active = jnp.min(jnp.where(later_active, eids[None, :], N_EXPERTS), axis=1)
    next_active = jnp.where(next_active >= N_EXPERTS, -1, next_active)
    seg_start = jnp.sum(jnp.where(of_e, pstart[None, :], 0), axis=1)
    block_first = ((blk_lo == seg_start) & (block_nv > 0)).astype(I32)
    block_next = jnp.sum(jnp.where(of_e, next_active[None, :], 0), axis=1).astype(I32)

    dest_flat = _dest(idx_t, rank_t, pstart.astype(F32)[:, None]).reshape(-1)
    n_slots = n_blocks * MOE_BLK
    pad_info = jnp.concatenate([pstart + counts, padded - counts, pend[-1:],
                                (n_slots - pend[-1:]) // MOE_BLK]).astype(I32)
    x_sorted, shared = _dispatch(dest_flat, pad_info, hp, h1, ws1.astype(BF16), ws3.astype(BF16),
                                 ws2.astype(BF16), n_slots)
    ys = _experts(block_e, block_nv, block_first, block_next,
                  x_sorted.reshape(n_slots * ROW_TILE, LANES), w1, w3, w2)
    return _final(dest_flat, ys.reshape(n_slots, ROW_TILE, LANES), w_t.T, h1, shared, ln2_g[None, :],
                  ln2_b[None, :], alpha)


def kernel(x, mem, w_in, conv_w, conv_b, dt_bias, a_log, d_skip, ssd_norm_g, swa_sinks, rel_bias, w_mem_kv, w_ssd_o, w_swa_o, w_xa_o, w_out, ln1_g, ln1_b, router_w, router_bias, w1, w3, w2, ws1, ws3, ws2, ln2_g, ln2_b):
    batch, seq, d = x.shape
    depth = w_in.shape[0]
    alpha = (2.0 * depth) ** 0.25
    h = x.reshape(batch * seq, d)
    mem2d = mem.reshape(batch * MEM_LEN, d)
    for i in range(depth):
        h = _layer(h, mem2d, batch, seq, w_in, i, conv_w[i], conv_b[i], dt_bias[i], a_log[i], d_skip[i],
                   ssd_norm_g[i], swa_sinks[i], rel_bias, w_mem_kv[i], w_ssd_o[i], w_swa_o[i], w_xa_o[i], w_out[i],
                   ln1_g[i], ln1_b[i], router_w[i], router_bias[i], w1[i], w3[i], w2[i], ws1[i], ws3[i], ws2[i],
                   ln2_g[i], ln2_b[i], alpha)
    return h.reshape(batch, seq, d)
```

```python
import functools
import math

import jax
import jax.numpy as jnp
from jax import lax
from jax.experimental import pallas as pl
from jax.experimental.pallas import tpu as pltpu

F32 = jnp.float32
BF16 = jnp.bfloat16
I32 = jnp.int32
U32 = jnp.uint32

D_MODEL = 2048
MEM_LEN = 256
SSD_D_INNER = D_MODEL
SSD_HEAD_DIM = 64
SSD_HEADS = 32
SSD_GROUPS = 4
SSD_STATE = 128
SSD_CONV = 4
SSD_CHUNK = 128
SWA_HEADS = 16
SWA_KV_HEADS = 4
SWA_HEAD_DIM = 64
SWA_WINDOW = 128
SWA_BLOCK = 128
REL_BUCKETS = 32
REL_MAX_DIST = 128
XA_HEADS = 4
XA_HEAD_DIM = 256
N_BRANCH = 3
N_EXPERTS = 64
TOP_K = 8
N_EXPERT_GROUPS = 8
TOPK_GROUPS = 4
EXPERT_DIM = 512
SHARED_DIM = 512
ROUTED_SCALE = 2.5
LN_EPS = 1e-5
RMS_EPS = 1e-5

SWA_Q_DIM = SWA_HEADS * SWA_HEAD_DIM
SWA_KV_DIM = SWA_KV_HEADS * SWA_HEAD_DIM
XA_DIM = XA_HEADS * XA_HEAD_DIM
BC_DIM = 2 * SSD_GROUPS * SSD_STATE

W_A = SSD_D_INNER + SSD_D_INNER + BC_DIM
OFFA_Z = 0
OFFA_XS = OFFA_Z + SSD_D_INNER
OFFA_BC = OFFA_XS + SSD_D_INNER
W_B = SWA_Q_DIM + 2 * SWA_KV_DIM + XA_DIM + N_BRANCH * D_MODEL
OFFB_Q = 0
OFFB_K = OFFB_Q + SWA_Q_DIM
OFFB_V = OFFB_K + SWA_KV_DIM
OFFB_QX = OFFB_V + SWA_KV_DIM
OFFB_GATES = OFFB_QX + XA_DIM

LANES = 128
MOE_BLK = 256
NEG_INF = float("-inf")


def _cparams(sem, vmem_mb=48):
    return pltpu.CompilerParams(dimension_semantics=sem, vmem_limit_bytes=vmem_mb * 1024 * 1024)


def _dot(a, b):
    return jnp.dot(a, b, preferred_element_type=F32)


def _dot_nt(a, b):
    return lax.dot_general(a, b, (((1,), (1,)), ((), ())), preferred_element_type=F32)


def _sigmoid(x):
    return 1.0 / (1.0 + jnp.exp(-x))


def _silu(x):
    return x * _sigmoid(x)


TM_IN = 512
TN_IN = W_A // 2


def _inproj_kernel(x_ref, w_ref, wdt_ref, p_ref, dt_ref, xb_ref):
    @pl.when(pl.program_id(1) == 0)
    def _():
        xb = x_ref[...].astype(BF16)
        xb_ref[...] = xb
        dt_ref[...] = _dot(xb, wdt_ref[...])

    p_ref[...] = _dot(xb_ref[...], w_ref[...].astype(BF16)).astype(BF16)


def _inproj(x2d, w_in, wdt):
    t = x2d.shape[0]
    return pl.pallas_call(
        _inproj_kernel,
        grid=(t // TM_IN, W_A // TN_IN),
        in_specs=[
            pl.BlockSpec((TM_IN, D_MODEL), lambda i, j: (i, 0)),
            pl.BlockSpec((D_MODEL, TN_IN), lambda i, j: (0, j)),
            pl.BlockSpec((D_MODEL, LANES), lambda i, j: (0, 0)),
        ],
        out_specs=[
            pl.BlockSpec((TM_IN, TN_IN), lambda i, j: (i, j)),
            pl.BlockSpec((TM_IN, LANES), lambda i, j: (i, 0)),
            pl.BlockSpec((TM_IN, D_MODEL), lambda i, j: (i, 0)),
        ],
        out_shape=[jax.ShapeDtypeStruct((t, W_A), BF16), jax.ShapeDtypeStruct((t, LANES), F32),
                   jax.ShapeDtypeStruct((t, D_MODEL), BF16)],
        compiler_params=_cparams(("parallel", "arbitrary"), vmem_mb=56),
        name="inproj",
    )(x2d, w_in, wdt)


TN_WC = 512
ROW_ALIGN = 32


def _wcast_kernel(a_ref, o_ref):
    o_ref[...] = a_ref[0].T.astype(BF16)


def _wcast(w_t, layer, start, width, name):
    k = w_t.shape[2]
    return pl.pallas_call(
        _wcast_kernel,
        grid=(width // TN_WC,),
        in_specs=[pl.BlockSpec((pl.Element(1), pl.Element(TN_WC), pl.Element(k)),
                               lambda j: (layer, pl.multiple_of(start + j * TN_WC, ROW_ALIGN), 0))],
        out_specs=pl.BlockSpec((k, TN_WC), lambda j: (0, j)),
        out_shape=jax.ShapeDtypeStruct((k, width), BF16),
        compiler_params=_cparams(("parallel",)),
        name=name,
    )(w_t)


def _mm_kernel(a_ref, b_ref, o_ref):
    o_ref[...] = _dot(a_ref[...].astype(BF16), b_ref[...].astype(BF16)).astype(o_ref.dtype)


def _mm(a, b, out_dtype, tm, tn, name, vmem_mb=48):
    m, k = a.shape
    n = b.shape[1]
    return pl.pallas_call(
        _mm_kernel,
        grid=(m // tm, n // tn),
        in_specs=[pl.BlockSpec((tm, k), lambda i, j: (i, 0)), pl.BlockSpec((k, tn), lambda i, j: (0, j))],
        out_specs=pl.BlockSpec((tm, tn), lambda i, j: (i, j)),
        out_shape=jax.ShapeDtypeStruct((m, n), out_dtype),
        compiler_params=_cparams(("parallel", "parallel"), vmem_mb=vmem_mb),
        name=name,
    )(a, b)


Q = SSD_CHUNK


def _conv_shift_matrix():
    t = jnp.arange(Q)[:, None]
    j = jnp.arange(2 * Q)[None, :]
    blocks = []
    for k in range(SSD_CONV - 1):
        src = t - (SSD_CONV - 1 - k)
        blocks.append(j == jnp.where(src >= 0, src, 2 * Q + src))
    return jnp.concatenate(blocks, axis=0).astype(BF16)


def _ssd_kernel(xs_ref, bc_ref, z_ref, dtr_ref, shift_ref, cwx_ref, cbx_ref, cwb_ref, cbb_ref, dtb_ref, alog_ref,
                dsk_ref, ng_ref, o_ref, hx_ref, hb_ref, st_ref, cumt_ref, dtt_ref, wt_ref, y_ref):
    c = pl.program_id(1)

    @pl.when(c == 0)
    def _():
        hx_ref[...] = jnp.zeros_like(hx_ref)
        hb_ref[...] = jnp.zeros_like(hb_ref)
        st_ref[...] = jnp.zeros_like(st_ref)

    def conv_silu(in_ref, prev_ref, w_ref, b_ref):
        cur = in_ref[...]
        ext = jnp.concatenate([cur, prev_ref[...]], axis=0)
        shifted = _dot(shift_ref[...], ext)
        acc = b_ref[...] + w_ref[SSD_CONV - 1:SSD_CONV, :] * cur.astype(F32)
        for k in range(SSD_CONV - 1):
            acc = acc + w_ref[k:k + 1, :] * shifted[k * Q:(k + 1) * Q, :]
        prev_ref[...] = cur
        return _silu(acc)

    xs_c = conv_silu(xs_ref, hx_ref, cwx_ref, cbx_ref)
    bc_c = conv_silu(bc_ref, hb_ref, cwb_ref, cbb_ref)

    lane = lax.broadcasted_iota(I32, (Q, LANES), 1)
    row = lax.broadcasted_iota(I32, (Q, LANES), 0)
    tril = row >= lane
    lo_mask = lane < SSD_HEAD_DIM

    dtv = dtr_ref[...] + dtb_ref[...]
    dt = jnp.maximum(dtv, 0.0) + jnp.log1p(jnp.exp(-jnp.abs(dtv)))
    a = jnp.where(lane[0:1, :] < SSD_HEADS, -jnp.exp(alog_ref[...]), 0.0)
    la = dt * a
    tri = jnp.where(tril, 1.0, 0.0).astype(BF16)
    p1 = la.astype(BF16)
    r1 = la - p1.astype(F32)
    p2 = r1.astype(BF16)
    p3 = (r1 - p2.astype(F32)).astype(BF16)
    cum = _dot(tri, p1) + _dot(tri, p2) + _dot(tri, p3)
    ecum = jnp.exp(cum)
    cum_t = cum.T
    dt_t = dt.T
    last = cum_t[:, Q - 1:Q]
    cumt_ref[...] = cum_t
    dtt_ref[...] = dt_t
    wt_ref[...] = jnp.exp(last - cum_t) * dt_t
    chunk_decay = jnp.exp(last)

    def head_parts(h, cb, cm, bm_t):
        col = jnp.broadcast_to(cum[:, h:h + 1], (Q, Q))
        rw = cumt_ref[h:h + 1, :]
        dec = jnp.exp(jnp.where(tril, col - rw, NEG_INF))
        m = cb * (dec * dtt_ref[h:h + 1, :])
        ce = cm * jnp.broadcast_to(ecum[:, h:h + 1], (Q, Q))
        sl = bm_t * wt_ref[h:h + 1, :]
        return m.astype(BF16), ce.astype(BF16), sl.astype(BF16)

    for g in range(SSD_GROUPS):
        bm = bc_c[:, g * SSD_STATE:(g + 1) * SSD_STATE]
        cm = bc_c[:, BC_DIM // 2 + g * SSD_STATE:BC_DIM // 2 + (g + 1) * SSD_STATE]
        cb = _dot_nt(cm.astype(BF16), bm.astype(BF16))
        bm_t = bm.T
        for qd in range(SSD_HEADS // SSD_GROUPS // 2):
            hp = g * (SSD_HEADS // SSD_GROUPS // 2) + qd
            ha, hb = 2 * hp, 2 * hp + 1
            sl_ = slice(hp * LANES, (hp + 1) * LANES)
            xs_pair = xs_c[:, sl_]
            x_lo = jnp.where(lo_mask, xs_pair, 0.0).astype(BF16)
            x_hi = jnp.where(lo_mask, 0.0, xs_pair).astype(BF16)
            h_prev = st_ref[:, sl_]
            h_lo = jnp.where(lo_mask, h_prev, 0.0).astype(BF16)
            h_hi = jnp.where(lo_mask, 0.0, h_prev).astype(BF16)
            m_a, ce_a, sl_a = head_parts(ha, cb, cm, bm_t)
            m_b, ce_b, sl_b = head_parts(hb, cb, cm, bm_t)
            lhs = jnp.concatenate([m_a, ce_a, m_b, ce_b], axis=1)
            rhs = jnp.concatenate([x_lo, h_lo, x_hi, h_hi], axis=0)
            y_ref[:, sl_] = _dot(lhs, rhs)
            new = _dot(jnp.concatenate([sl_a, sl_b], axis=1), jnp.concatenate([x_lo, x_hi], axis=0))
            cd = jnp.where(lo_mask[0:1, :], chunk_decay[ha:ha + 1, :], chunk_decay[hb:hb + 1, :])
            st_ref[:, sl_] = h_prev * cd + new

    y = y_ref[...] + xs_c * dsk_ref[...]
    y = y * _silu(z_ref[...].astype(F32))
    gw = SSD_D_INNER // SSD_GROUPS
    for g in range(SSD_GROUPS):
        yg = y[:, g * gw:(g + 1) * gw]
        ms = jnp.mean(yg * yg, axis=-1, keepdims=True)
        o_ref[:, g * gw:(g + 1) * gw] = ((yg * lax.rsqrt(ms + RMS_EPS)) * ng_ref[:, g * gw:(g + 1) * gw]).astype(BF16)


def _ssd(p, dt_raw, cwx, cbx, cwb, cbb, dtb, alog, dsk, ng, batch, seq):
    nc = seq // Q
    t = batch * seq
    rowblk = lambda b, c: b * nc + c
    const = lambda b, c: (0, 0)
    return pl.pallas_call(
        _ssd_kernel,
        grid=(batch, nc),
        in_specs=[
            pl.BlockSpec((Q, SSD_D_INNER), lambda b, c: (rowblk(b, c), OFFA_XS // SSD_D_INNER)),
            pl.BlockSpec((Q, BC_DIM), lambda b, c: (rowblk(b, c), OFFA_BC // BC_DIM)),
            pl.BlockSpec((Q, SSD_D_INNER), lambda b, c: (rowblk(b, c), OFFA_Z // SSD_D_INNER)),
            pl.BlockSpec((Q, LANES), lambda b, c: (rowblk(b, c), 0)),
            pl.BlockSpec(((SSD_CONV - 1) * Q, 2 * Q), const),
            pl.BlockSpec((SSD_CONV, SSD_D_INNER), const),
            pl.BlockSpec((1, SSD_D_INNER), const),
            pl.BlockSpec((SSD_CONV, BC_DIM), const),
            pl.BlockSpec((1, BC_DIM), const),
            pl.BlockSpec((1, LANES), const),
            pl.BlockSpec((1, LANES), const),
            pl.BlockSpec((1, SSD_D_INNER), const),
            pl.BlockSpec((1, SSD_D_INNER), const),
        ],
        out_specs=pl.BlockSpec((Q, SSD_D_INNER), lambda b, c: (rowblk(b, c), 0)),
        out_shape=jax.ShapeDtypeStruct((t, SSD_D_INNER), BF16),
        scratch_shapes=[
            pltpu.VMEM((Q, SSD_D_INNER), BF16),
            pltpu.VMEM((Q, BC_DIM), BF16),
            pltpu.VMEM((SSD_STATE, SSD_D_INNER), F32),
            pltpu.VMEM((LANES, Q), F32),
            pltpu.VMEM((LANES, Q), F32),
            pltpu.VMEM((LANES, Q), F32),
            pltpu.VMEM((Q, SSD_D_INNER), F32),
        ],
        compiler_params=_cparams(("parallel", "arbitrary")),
        name="ssd",
    )(p, p, p, dt_raw, _conv_shift_matrix(), cwx, cbx, cwb, cbb, dtb, alog, dsk, ng)


N_PAIRS = SWA_HEADS // 2


def _swa_kernel(sink_ref, q_ref, kp_ref, kc_ref, vp_ref, vc_ref, bias_ref, o_ref):
    n = pl.program_id(1)
    blk = SWA_BLOCK
    kband = jnp.concatenate([kp_ref[...], kc_ref[...]], axis=0)
    vband = jnp.concatenate([vp_ref[...], vc_ref[...]], axis=0)

    lane = lax.broadcasted_iota(I32, (1, LANES), 1)
    lo_f = jnp.where(lane < SWA_HEAD_DIM, 1.0, 0.0)
    lo_m = lo_f.astype(BF16)
    hi_m = (1.0 - lo_f).astype(BF16)

    def head_twice(band, kv):
        tile = band[:, (kv // 2) * LANES:(kv // 2 + 1) * LANES]
        swapped = jnp.concatenate([tile[:, SWA_HEAD_DIM:], tile[:, :SWA_HEAD_DIM]], axis=1)
        return tile * lo_m + swapped * hi_m if kv % 2 == 0 else swapped * lo_m + tile * hi_m

    k_twice = [head_twice(kband, kv) for kv in range(SWA_KV_HEADS)]
    v_twice = [head_twice(vband, kv) for kv in range(SWA_KV_HEADS)]
    rowi = lax.broadcasted_iota(I32, (2 * blk, 1), 0)
    scale = SWA_HEAD_DIM ** -0.5
    assert math.frexp(scale)[0] == 0.5
    lo_q = (lo_f * scale).astype(BF16)
    hi_q = ((1.0 - lo_f) * scale).astype(BF16)
    for p in range(N_PAIRS):
        kv = p // (SWA_HEADS // SWA_KV_HEADS // 2)
        qp = q_ref[:, p * LANES:(p + 1) * LANES]
        qs = jnp.concatenate([qp * lo_q, qp * hi_q], axis=0)
        s = _dot_nt(qs, k_twice[kv]) + bias_ref[0, p]
        sink = jnp.where(rowi < blk, sink_ref[2 * p], sink_ref[2 * p + 1])
        m = jnp.maximum(jnp.max(s, axis=-1, keepdims=True), sink)
        e = jnp.exp(s - m)
        den = jnp.sum(e, axis=-1, keepdims=True) + jnp.exp(sink - m)
        pr = (e * (1.0 / den)).astype(BF16)
        vd = v_twice[kv]
        lhs = jnp.concatenate([pr[:blk], pr[blk:]], axis=1)
        rhs = jnp.concatenate([vd * lo_m, vd * hi_m], axis=0)
        o_ref[:, p * LANES:(p + 1) * LANES] = _dot(lhs, rhs).astype(BF16)


def _swa(p, sinks, bias, batch, seq):
    nb = seq // SWA_BLOCK
    t = batch * seq
    kw = SWA_KV_DIM
    cur = lambda b, n, s: b * nb + n
    prev = lambda b, n, s: b * nb + jnp.maximum(n - 1, 0)
    return pl.pallas_call(
        _swa_kernel,
        grid_spec=pltpu.PrefetchScalarGridSpec(
            num_scalar_prefetch=1,
            grid=(batch, nb),
            in_specs=[
                pl.BlockSpec((SWA_BLOCK, SWA_Q_DIM), lambda b, n, s: (cur(b, n, s), OFFB_Q // SWA_Q_DIM)),
                pl.BlockSpec((SWA_BLOCK, kw), lambda b, n, s: (prev(b, n, s), OFFB_K // kw)),
                pl.BlockSpec((SWA_BLOCK, kw), lambda b, n, s: (cur(b, n, s), OFFB_K // kw)),
                pl.BlockSpec((SWA_BLOCK, kw), lambda b, n, s: (prev(b, n, s), OFFB_V // kw)),
                pl.BlockSpec((SWA_BLOCK, kw), lambda b, n, s: (cur(b, n, s), OFFB_V // kw)),
                pl.BlockSpec((1, N_PAIRS, 2 * SWA_BLOCK, 2 * SWA_BLOCK),
                             lambda b, n, s: (jnp.minimum(n, 1), 0, 0, 0)),
            ],
            out_specs=pl.BlockSpec((SWA_BLOCK, SWA_Q_DIM), lambda b, n, s: (cur(b, n, s), 0)),
        ),
        out_shape=jax.ShapeDtypeStruct((t, SWA_Q_DIM), BF16),
        compiler_params=_cparams(("parallel", "arbitrary")),
        name="swa",
    )(sinks, p, p, p, p, p, bias)


def _t5_causal_bucket(dist):
    max_exact = REL_BUCKETS // 2
    large = max_exact + (jnp.log(jnp.maximum(dist, 1).astype(F32) / max_exact)
                         / math.log(REL_MAX_DIST / max_exact) * (REL_BUCKETS - max_exact)).astype(I32)
    large = jnp.minimum(large, REL_BUCKETS - 1)
    return jnp.where(dist < max_exact, dist, large)


def _swa_bias_table(rel_bias):
    qi = jnp.arange(SWA_BLOCK)[:, None]
    kj = jnp.arange(2 * SWA_BLOCK)[None, :]
    dist = qi + SWA_BLOCK - kj
    in_window = (dist >= 0) & (dist < SWA_WINDOW)
    bucket = _t5_causal_bucket(jnp.maximum(dist, 0))
    onehot = (bucket[:, :, None] == jnp.arange(REL_BUCKETS)[None, None, :]).astype(F32)
    bias = jnp.einsum('ijb,bh->hij', onehot, rel_bias.astype(F32), precision=lax.Precision.HIGHEST)
    rest = jnp.where(in_window[None, :, :], bias, NEG_INF)
    first = jnp.where((kj >= SWA_BLOCK)[None, :, :], rest, NEG_INF)
    return jnp.stack([first, rest]).reshape(2, N_PAIRS, 2 * SWA_BLOCK, 2 * SWA_BLOCK)


TL_XA = 512


XA_HALF = XA_DIM // 2


def _xa_kernel(qa_ref, qb_ref, mk_ref, mv_ref, o_ref):
    for h in range(XA_HEADS):
        q_ref = qa_ref if h < XA_HEADS // 2 else qb_ref
        qc = (h % (XA_HEADS // 2)) * XA_HEAD_DIM
        cols = slice(h * XA_HEAD_DIM, (h + 1) * XA_HEAD_DIM)
        s = _dot_nt(q_ref[:, qc:qc + XA_HEAD_DIM], mk_ref[:, cols]) * (XA_HEAD_DIM ** -0.5)
        m = jnp.max(s, axis=-1, keepdims=True)
        e = jnp.exp(s - m)
        pr = e * (1.0 / jnp.sum(e, axis=-1, keepdims=True))
        o_ref[:, cols] = _dot(pr.astype(BF16), mv_ref[:, cols]).astype(BF16)


def _xa(p, memkv, batch, seq):
    nl = seq // TL_XA
    t = batch * seq
    return pl.pallas_call(
        _xa_kernel,
        grid=(batch, nl),
        in_specs=[
            pl.BlockSpec((TL_XA, XA_HALF), lambda b, l: (b * nl + l, OFFB_QX // XA_HALF)),
            pl.BlockSpec((TL_XA, XA_HALF), lambda b, l: (b * nl + l, OFFB_QX // XA_HALF + 1)),
            pl.BlockSpec((MEM_LEN, XA_DIM), lambda b, l: (b, 0)),
            pl.BlockSpec((MEM_LEN, XA_DIM), lambda b, l: (b, 1)),
        ],
        out_specs=pl.BlockSpec((TL_XA, XA_DIM), lambda b, l: (b * nl + l, 0)),
        out_shape=jax.ShapeDtypeStruct((t, XA_DIM), BF16),
        compiler_params=_cparams(("parallel", "parallel")),
        name="xattn",
    )(p, p, memkv, memkv)


TM_MG = 1024
TN_MG = 512


N_SUB_MG = 2


def _merge_kernel(ys_ref, yw_ref, yx_ref, ws_ref, ww_ref, wx_ref, g0_ref, g1_ref, g2_ref, o_ref):
    sub = TM_MG // N_SUB_MG
    ws = ws_ref[...].astype(BF16)
    ww = ww_ref[...].astype(BF16)
    wx = wx_ref[...].astype(BF16)
    for s in range(N_SUB_MG):
        rows = slice(s * sub, (s + 1) * sub)
        a = _dot(ys_ref[rows, :], ws)
        b = _dot(yw_ref[rows, :], ww)
        c = _dot(yx_ref[rows, :], wx)
        g0 = _sigmoid(g0_ref[rows, :].astype(F32))
        g1 = _sigmoid(g1_ref[rows, :].astype(F32))
        g2 = _sigmoid(g2_ref[rows, :].astype(F32))
        o_ref[rows, :] = (g0 * a + g1 * b + g2 * c).astype(BF16)


def _merge(y_ssd, y_swa, y_xa, w_ssd_o, w_swa_o, w_xa_o, p):
    t = y_ssd.shape[0]
    gpb = D_MODEL // TN_MG
    gate = lambda k: pl.BlockSpec((TM_MG, TN_MG), lambda i, j: (i, OFFB_GATES // TN_MG + k * gpb + j))
    return pl.pallas_call(
        _merge_kernel,
        grid=(t // TM_MG, D_MODEL // TN_MG),
        in_specs=[
            pl.BlockSpec((TM_MG, SSD_D_INNER), lambda i, j: (i, 0)),
            pl.BlockSpec((TM_MG, SWA_Q_DIM), lambda i, j: (i, 0)),
            pl.BlockSpec((TM_MG, XA_DIM), lambda i, j: (i, 0)),
            pl.BlockSpec((SSD_D_INNER, TN_MG), lambda i, j: (0, j)),
            pl.BlockSpec((SWA_Q_DIM, TN_MG), lambda i, j: (0, j)),
            pl.BlockSpec((XA_DIM, TN_MG), lambda i, j: (0, j)),
            gate(0), gate(1), gate(2),
        ],
        out_specs=pl.BlockSpec((TM_MG, TN_MG), lambda i, j: (i, j)),
        out_shape=jax.ShapeDtypeStruct((t, D_MODEL), BF16),
        compiler_params=_cparams(("parallel", "parallel")),
        name="merge",
    )(y_ssd, y_swa, y_xa, w_ssd_o, w_swa_o, w_xa_o, p, p, p)


TM_LN = 512


def _layer_norm(v, g, b):
    mu = jnp.mean(v, axis=-1, keepdims=True)
    d = v - mu
    var = jnp.mean(d * d, axis=-1, keepdims=True)
    return (d * lax.rsqrt(var + LN_EPS)) * g + b


def _pack_halves(v):
    n = v.shape[1] // 2
    lo = lax.bitcast_convert_type(v[:, :n].astype(BF16).astype(F32), U32) >> 16
    hi = lax.bitcast_convert_type(v[:, n:].astype(BF16).astype(F32), U32) & jnp.uint32(0xFFFF0000)
    return hi | lo


def _unpack_halves(w):
    lo = lax.bitcast_convert_type(w << 16, F32)
    hi = lax.bitcast_convert_type(w & jnp.uint32(0xFFFF0000), F32)
    return lo, hi


ROW_WORDS = D_MODEL // 2
ROW_TILE = ROW_WORDS // LANES


def _store_row_tiled(ref, words):
    m = words.shape[0]
    for c in range(ROW_TILE):
        ref[pl.ds(c, m, stride=ROW_TILE), :] = words[:, c * LANES:(c + 1) * LANES]


def _load_row_tiled(ref, m):
    return jnp.concatenate([ref[pl.ds(c, m, stride=ROW_TILE), :] for c in range(ROW_TILE)], axis=1)


N_SUB_LN = 4


def _out_ln_kernel(alpha, m_ref, w_ref, x_ref, g_ref, b_ref, h_ref, hp_ref):
    sub = TM_LN // N_SUB_LN
    for c in range(N_SUB_LN):
        rows = slice(c * sub, (c + 1) * sub)
        mix = _dot(m_ref[rows, :], w_ref[...])
        hn = _layer_norm(alpha * x_ref[rows, :] + mix, g_ref[...], b_ref[...])
        h_ref[rows, :] = hn
        _store_row_tiled(hp_ref.at[pl.ds(c * sub * ROW_TILE, sub * ROW_TILE), :], _pack_halves(hn))


def _out_ln(merged, w_out, x2d, g, b, alpha):
    t = x2d.shape[0]
    row = lambda i: (i, 0)
    const = lambda i: (0, 0)
    return pl.pallas_call(
        functools.partial(_out_ln_kernel, alpha),
        grid=(t // TM_LN,),
        in_specs=[
            pl.BlockSpec((TM_LN, D_MODEL), row),
            pl.BlockSpec((D_MODEL, D_MODEL), const),
            pl.BlockSpec((TM_LN, D_MODEL), row),
            pl.BlockSpec((1, D_MODEL), const),
            pl.BlockSpec((1, D_MODEL), const),
        ],
        out_specs=[pl.BlockSpec((TM_LN, D_MODEL), row), pl.BlockSpec((TM_LN * ROW_TILE, LANES), row)],
        out_shape=[jax.ShapeDtypeStruct((t, D_MODEL), F32), jax.ShapeDtypeStruct((t * ROW_TILE, LANES), U32)],
        compiler_params=_cparams(("parallel",)),
        name="out_ln1",
    )(merged, w_out, x2d, g, b)


TM_RT = 512
EPG = N_EXPERTS // N_EXPERT_GROUPS


def _router_kernel(h_ref, whi_ref, wlo_ref, bias_ref, su_ref, idx_ref, w_ref, rank_ref, cnt_ref, run_ref):
    @pl.when(pl.program_id(0) == 0)
    def _():
        run_ref[...] = jnp.zeros_like(run_ref)

    h = h_ref[...]
    h_hi = h.astype(BF16)
    h_lo = (h - h_hi.astype(F32)).astype(BF16)
    whi = whi_ref[...]
    logits = _dot_nt(whi, h_hi) + _dot_nt(whi, h_lo) + _dot_nt(wlo_ref[...], h_hi)
    scores = _sigmoid(logits)
    sel = scores + bias_ref[...]

    iota_m = lax.broadcasted_iota(I32, (EPG, TM_RT), 0).astype(F32)
    iota_g = lax.broadcasted_iota(I32, (N_EXPERT_GROUPS, TM_RT), 0).astype(F32)
    gs = []
    for g in range(N_EXPERT_GROUPS):
        v = sel[g * EPG:(g + 1) * EPG, :]
        m1 = jnp.max(v, axis=0, keepdims=True)
        i1 = jnp.min(jnp.where(v == m1, iota_m, float(EPG)), axis=0, keepdims=True)
        m2 = jnp.max(jnp.where(iota_m == i1, NEG_INF, v), axis=0, keepdims=True)
        gs.append(m1 + m2)
    gsc = jnp.concatenate(gs, axis=0)
    chosen = jnp.zeros((N_EXPERT_GROUPS, TM_RT), F32)
    for _ in range(TOPK_GROUPS):
        m = jnp.max(gsc, axis=0, keepdims=True)
        ig = jnp.min(jnp.where(gsc == m, iota_g, float(N_EXPERT_GROUPS)), axis=0, keepdims=True)
        hit = iota_g == ig
        chosen = jnp.where(hit, 1.0, chosen)
        gsc = jnp.where(hit, NEG_INF, gsc)
    gmask = jnp.concatenate(
        [jnp.broadcast_to(chosen[g:g + 1, :], (EPG, TM_RT)) for g in range(N_EXPERT_GROUPS)], axis=0)
    masked = jnp.where(gmask > 0.0, sel, NEG_INF)

    iota_e = lax.broadcasted_iota(I32, (N_EXPERTS, TM_RT), 0).astype(F32)
    hits, idxs, ws = [], [], []
    for _ in range(TOP_K):
        m = jnp.max(masked, axis=0, keepdims=True)
        ie = jnp.min(jnp.where(masked == m, iota_e, float(N_EXPERTS)), axis=0, keepdims=True)
        hit = iota_e == ie
        hits.append(hit)
        idxs.append(ie)
        ws.append(jnp.sum(jnp.where(hit, scores, 0.0), axis=0, keepdims=True))
        masked = jnp.where(hit, NEG_INF, masked)
    wsum = ws[0]
    for k in range(1, TOP_K):
        wsum = wsum + ws[k]
    inv = 1.0 / wsum
    idx_ref[...] = jnp.concatenate(idxs, axis=0).astype(I32)
    w_ref[...] = jnp.concatenate([w * inv * ROUTED_SCALE for w in ws], axis=0)

    assign = jnp.zeros((N_EXPERTS, TM_RT), F32)
    for hit in hits:
        assign = jnp.where(hit, 1.0, assign)
    pos = _dot(assign.astype(BF16), su_ref[...]) + run_ref[...]
    rank_ref[...] = jnp.concatenate(
        [jnp.sum(jnp.where(hit, pos, 0.0), axis=0, keepdims=True) for hit in hits], axis=0).astype(I32)
    run = run_ref[...] + jnp.sum(assign, axis=1, keepdims=True)
    run_ref[...] = run
    cnt_ref[...] = run


def _router(h1, r_hi, r_lo, rbias, su):
    t = h1.shape[0]
    tile = lambda i: (0, i)
    const = lambda i: (0, 0)
    return pl.pallas_call(
        _router_kernel,
        grid=(t // TM_RT,),
        in_specs=[
            pl.BlockSpec((TM_RT, D_MODEL), lambda i: (i, 0)),
            pl.BlockSpec((N_EXPERTS, D_MODEL), const),
            pl.BlockSpec((N_EXPERTS, D_MODEL), const),
            pl.BlockSpec((N_EXPERTS, 1), const),
            pl.BlockSpec((TM_RT, TM_RT), const),
        ],
        out_specs=[
            pl.BlockSpec((TOP_K, TM_RT), tile),
            pl.BlockSpec((TOP_K, TM_RT), tile),
            pl.BlockSpec((TOP_K, TM_RT), tile),
            pl.BlockSpec((N_EXPERTS, TM_RT), const),
        ],
        out_shape=[
            jax.ShapeDtypeStruct((TOP_K, t), I32),
            jax.ShapeDtypeStruct((TOP_K, t), F32),
            jax.ShapeDtypeStruct((TOP_K, t), I32),
            jax.ShapeDtypeStruct((N_EXPERTS, TM_RT), F32),
        ],
        scratch_shapes=[pltpu.VMEM((N_EXPERTS, TM_RT), F32)],
        compiler_params=_cparams(("arbitrary",)),
        name="router",
    )(h1, r_hi, r_lo, rbias, su)


TM_DS = 2048


def _dest_kernel(idx_ref, rank_ref, ps_ref, d_ref):
    iota_e = lax.broadcasted_iota(I32, (N_EXPERTS, TM_DS), 0)
    ps = ps_ref[...]
    rows = []
    for k in range(TOP_K):
        hit = iota_e == idx_ref[k:k + 1, :]
        rows.append(jnp.sum(jnp.where(hit, ps, 0.0), axis=0, keepdims=True))
    d_ref[...] = jnp.concatenate(rows, axis=0).astype(I32) + rank_ref[...]


def _dest(idx_t, rank_t, pstart):
    t = idx_t.shape[1]
    tile = lambda i: (0, i)
    return pl.pallas_call(
        _dest_kernel,
        grid=(t // TM_DS,),
        in_specs=[pl.BlockSpec((TOP_K, TM_DS), tile), pl.BlockSpec((TOP_K, TM_DS), tile),
                  pl.BlockSpec((N_EXPERTS, 1), lambda i: (0, 0))],
        out_specs=pl.BlockSpec((TOP_K, TM_DS), tile),
        out_shape=jax.ShapeDtypeStruct((TOP_K, t), I32),
        compiler_params=_cparams(("parallel",)),
        name="dest",
    )(idx_t, rank_t, pstart)


TM_DP = 512


PAD_BITS = tuple(1 << i for i in reversed(range(MOE_BLK.bit_length() - 1)))


def _dispatch_kernel(dest_ref, pad_ref, hp_ref, h_ref, ws1_ref, ws3_ref, ws2_ref, xs_ref, sh_ref, zbuf, sem, zsem):
    t_total = pl.num_programs(0) * TM_DP
    base = pl.program_id(0) * TM_DP

    def pad_copies(act):
        def per_expert(e, carry):
            start = pad_ref[e]
            length = pad_ref[N_EXPERTS + e]
            for bit in PAD_BITS:
                @pl.when((length & bit) != 0)
                def _():
                    off = length - (length & (2 * bit - 1))
                    act(pltpu.make_async_copy(zbuf.at[pl.ds(0, bit)], xs_ref.at[pl.ds(start + off, bit)], zsem))
            return carry

        lax.fori_loop(0, N_EXPERTS, per_expert, 0)

        def per_block(i, carry):
            row = pad_ref[2 * N_EXPERTS] + i * MOE_BLK
            act(pltpu.make_async_copy(zbuf, xs_ref.at[pl.ds(row, MOE_BLK)], zsem))
            return carry

        lax.fori_loop(0, pad_ref[2 * N_EXPERTS + 1], per_block, 0)

    @pl.when(pl.program_id(0) == 0)
    def _():
        zbuf[...] = jnp.zeros_like(zbuf)
        pad_copies(lambda cp: cp.start())

    def issue(t, carry):
        src = hp_ref.at[pl.ds(pl.multiple_of(t * ROW_TILE, ROW_TILE), ROW_TILE), :]
        for k in range(TOP_K):
            slot = dest_ref[k * t_total + base + t]
            pltpu.make_async_copy(src, xs_ref.at[slot], sem).start(priority=k % 2)
        return carry

    lax.fori_loop(0, TM_DP, issue, 0)

    xb = h_ref[...].astype(BF16)
    hdn = _silu(_dot(xb, ws1_ref[...])) * _dot(xb, ws3_ref[...])
    sh_ref[...] = _dot(hdn.astype(BF16), ws2_ref[...])

    for k in range(TOP_K):
        pltpu.make_async_copy(hp_ref.reshape(TM_DP, ROW_TILE, LANES), xs_ref.at[pl.ds(0, TM_DP)], sem).wait()

    @pl.when(pl.program_id(0) == pl.num_programs(0) - 1)
    def _():
        pad_copies(lambda cp: cp.wait())


def _dispatch(dest_flat, pad_info, hp, h1, ws1, ws3, ws2, n_slots):
    t = h1.shape[0]
    row = lambda i, *_: (i, 0)
    const = lambda i, *_: (0, 0)
    return pl.pallas_call(
        _dispatch_kernel,
        grid_spec=pltpu.PrefetchScalarGridSpec(
            num_scalar_prefetch=2,
            grid=(t // TM_DP,),
            in_specs=[
                pl.BlockSpec((TM_DP * ROW_TILE, LANES), row),
                pl.BlockSpec((TM_DP, D_MODEL), row),
                pl.BlockSpec((D_MODEL, SHARED_DIM), const),
                pl.BlockSpec((D_MODEL, SHARED_DIM), const),
                pl.BlockSpec((SHARED_DIM, D_MODEL), const),
            ],
            out_specs=[pl.BlockSpec(memory_space=pl.ANY), pl.BlockSpec((TM_DP, D_MODEL), row)],
            scratch_shapes=[pltpu.VMEM((MOE_BLK, ROW_TILE, LANES), U32), pltpu.SemaphoreType.DMA,
                            pltpu.SemaphoreType.DMA],
        ),
        out_shape=[jax.ShapeDtypeStruct((n_slots, ROW_TILE, LANES), U32), jax.ShapeDtypeStruct((t, D_MODEL), F32)],
        compiler_params=_cparams(("arbitrary",)),
        name="dispatch",
    )(dest_flat, pad_info, hp, h1, ws1, ws3, ws2)


N_WSLOTS = 2
BLK_PER_STEP = 2


def _expert_kernel(be_ref, nv_ref, first_ref, nxt_ref, x_ref, w1_ref, w3_ref, w2_ref, o_ref,
                   wf1, wf3, wf2, w1b, w3b, w2b, slot_ref, sems):
    step = pl.program_id(0)

    def weight_copies(e, s):
        return (pltpu.make_async_copy(w1_ref.at[e], wf1.at[s], sems.at[s, 0]),
                pltpu.make_async_copy(w3_ref.at[e], wf3.at[s], sems.at[s, 1]),
                pltpu.make_async_copy(w2_ref.at[e], wf2.at[s], sems.at[s, 2]))

    @pl.when(step == 0)
    def _():
        slot_ref[0] = 0
        for cp in weight_copies(be_ref[0], 0):
            cp.start()

    def switch_weights(b):
        @pl.when(first_ref[b] == 1)
        def _():
            s = slot_ref[0]
            for cp in weight_copies(be_ref[b], s):
                cp.wait()

            @pl.when(nxt_ref[b] >= 0)
            def _():
                for cp in weight_copies(nxt_ref[b], 1 - s):
                    cp.start(priority=1)

            w1b[...] = wf1[s].astype(BF16)
            w3b[...] = wf3[s].astype(BF16)
            w2b[...] = wf2[s].astype(BF16)
            slot_ref[0] = 1 - s

    def ffn(xq, oq, m):
        lo, hi = _unpack_halves(_load_row_tiled(xq, m))
        x = jnp.concatenate([lo.astype(BF16), hi.astype(BF16)], axis=1)
        hdn = _silu(_dot(x, w1b[...])) * _dot(x, w3b[...])
        _store_row_tiled(oq, _pack_halves(_dot(hdn.astype(BF16), w2b[...])))

    def one_block(b, xq, oq):
        nv = nv_ref[b]

        @pl.when(nv > 0)
        def _():
            ffn(xq, oq, MOE_BLK)

        @pl.when(nv <= 0)
        def _():
            oq[...] = jnp.zeros_like(oq)

    assert BLK_PER_STEP == 2
    blk_rows = MOE_BLK * ROW_TILE
    b0 = step * BLK_PER_STEP
    b1 = b0 + 1
    lo_rows = pl.ds(0, blk_rows)
    hi_rows = pl.ds(blk_rows, blk_rows)
    same = (first_ref[b1] == 0) & (nv_ref[b1] > 0)
    switch_weights(b0)

    @pl.when(same)
    def _():
        ffn(x_ref, o_ref, 2 * MOE_BLK)

    @pl.when(jnp.logical_not(same))
    def _():
        one_block(b0, x_ref.at[lo_rows, :], o_ref.at[lo_rows, :])
        switch_weights(b1)
        one_block(b1, x_ref.at[hi_rows, :], o_ref.at[hi_rows, :])


def _experts(block_e, block_nv, block_first, block_next, x_sorted, w1, w3, w2):
    n_rows = x_sorted.shape[0]
    step_rows = BLK_PER_STEP * MOE_BLK * ROW_TILE
    row = lambda b, *_: (b, 0)
    return pl.pallas_call(
        _expert_kernel,
        grid_spec=pltpu.PrefetchScalarGridSpec(
            num_scalar_prefetch=4,
            grid=(n_rows // step_rows,),
            in_specs=[
                pl.BlockSpec((step_rows, LANES), row),
                pl.BlockSpec(memory_space=pl.ANY),
                pl.BlockSpec(memory_space=pl.ANY),
                pl.BlockSpec(memory_space=pl.ANY),
            ],
            out_specs=pl.BlockSpec((step_rows, LANES), row),
            scratch_shapes=[
                pltpu.VMEM((N_WSLOTS, D_MODEL, EXPERT_DIM), F32),
                pltpu.VMEM((N_WSLOTS, D_MODEL, EXPERT_DIM), F32),
                pltpu.VMEM((N_WSLOTS, EXPERT_DIM, D_MODEL), F32),
                pltpu.VMEM((D_MODEL, EXPERT_DIM), BF16),
                pltpu.VMEM((D_MODEL, EXPERT_DIM), BF16),
                pltpu.VMEM((EXPERT_DIM, D_MODEL), BF16),
                pltpu.SMEM((1,), I32),
                pltpu.SemaphoreType.DMA((N_WSLOTS, 3)),
            ],
        ),
        out_shape=jax.ShapeDtypeStruct((n_rows, LANES), U32),
        compiler_params=_cparams(("arbitrary",), vmem_mb=58),
        name="experts",
    )(block_e, block_nv, block_first, block_next, x_sorted, w1, w3, w2)


TM_FN = 256


CH_FN = 8


def _final_kernel(alpha, dest_ref, ys_ref, w_ref, h_ref, sh_ref, g_ref, b_ref, o_ref, buf, sems):
    i = pl.program_id(0)
    n_tiles = pl.num_programs(0)
    t_total = n_tiles * TM_FN
    slot = i % 2
    nslot = 1 - slot
    half = D_MODEL // 2

    def issue_token(tile, sl, c, tt):
        tok = c * CH_FN + tt
        row0 = pl.multiple_of(tok * ROW_TILE, ROW_TILE)
        for k in range(TOP_K):
            src = ys_ref.at[dest_ref[k * t_total + tile * TM_FN + tok]]
            pltpu.make_async_copy(src, buf.at[sl, k, pl.ds(row0, ROW_TILE), :], sems.at[sl]).start(priority=k % 2)

    def wait_tile(sl):
        for k in range(TOP_K):
            pltpu.make_async_copy(ys_ref.at[pl.ds(0, TM_FN)], buf.at[sl, k].reshape(TM_FN, ROW_TILE, LANES),
                                  sems.at[sl]).wait()

    @pl.when(i == 0)
    def _():
        def first(c, carry):
            for tt in range(CH_FN):
                issue_token(0, 0, c, tt)
            return carry
        lax.fori_loop(0, TM_FN // CH_FN, first, 0)

    nxt_tile = jnp.minimum(i + 1, n_tiles - 1)

    def combine_from(cur, nxt):
        wait_tile(cur)

        def body(c, carry):
            tok = pl.ds(pl.multiple_of(c * CH_FN, CH_FN), CH_FN)
            acc_lo = sh_ref[tok, :half]
            acc_hi = sh_ref[tok, half:]
            pre = alpha * h_ref[tok, :]
            wrow = w_ref[tok, :]
            for k in range(TOP_K):
                words = jnp.concatenate(
                    [buf[cur, k, pl.ds(c * (CH_FN * ROW_TILE) + r, CH_FN, stride=ROW_TILE), :]
                     for r in range(ROW_TILE)], axis=1)
                issue_token(nxt_tile, nxt, c, k)
                lo, hi = _unpack_halves(words)
                wk = wrow[:, k:k + 1]
                acc_lo = acc_lo + wk * lo
                acc_hi = acc_hi + wk * hi
            o_ref[tok, :] = pre + jnp.concatenate([acc_lo, acc_hi], axis=1)
            return carry

        lax.fori_loop(0, TM_FN // CH_FN, body, 0)

        @pl.when(i == n_tiles - 1)
        def _():
            wait_tile(nxt)

    for parity in range(2):
        @pl.when(slot == parity)
        def _():
            combine_from(parity, 1 - parity)

    o_ref[...] = _layer_norm(o_ref[...], g_ref[...], b_ref[...])


def _final(dest_flat, ys, w_tok, h1, shared, g, b, alpha):
    t = h1.shape[0]
    row = lambda i, d: (i, 0)
    const = lambda i, d: (0, 0)
    return pl.pallas_call(
        functools.partial(_final_kernel, alpha),
        grid_spec=pltpu.PrefetchScalarGridSpec(
            num_scalar_prefetch=1,
            grid=(t // TM_FN,),
            in_specs=[
                pl.BlockSpec(memory_space=pl.ANY),
                pl.BlockSpec((TM_FN, TOP_K), row),
                pl.BlockSpec((TM_FN, D_MODEL), row),
                pl.BlockSpec((TM_FN, D_MODEL), row),
                pl.BlockSpec((1, D_MODEL), const),
                pl.BlockSpec((1, D_MODEL), const),
            ],
            out_specs=pl.BlockSpec((TM_FN, D_MODEL), row),
            scratch_shapes=[pltpu.VMEM((2, TOP_K, TM_FN * ROW_TILE, LANES), U32), pltpu.SemaphoreType.DMA((2,))],
        ),
        out_shape=jax.ShapeDtypeStruct((t, D_MODEL), F32),
        compiler_params=_cparams(("arbitrary",)),
        name="combine_ln2",
    )(dest_flat, ys, w_tok, h1, shared, g, b)


TM_INB = 512
TN_INB = W_B // 2


def _pad_lanes(v):
    return jnp.pad(v, (0, LANES - v.shape[0]))[None, :].astype(F32)


def _layer(h2d, mem2d, batch, seq, w_in_all, layer, conv_w, conv_b, dt_bias, a_log, d_skip, ssd_norm_g, swa_sinks,
           rel_bias, w_mem_kv, w_ssd_o, w_swa_o, w_xa_o, w_out, ln1_g, ln1_b, router_w, router_bias, w1, w3, w2, ws1,
           ws3, ws2, ln2_g, ln2_b, alpha):
    t = batch * seq
    w_t = jnp.swapaxes(w_in_all, 1, 2)
    wdt = jnp.pad(w_t[layer, W_A:W_A + SSD_HEADS, :].T, ((0, 0), (0, LANES - SSD_HEADS))).astype(BF16)
    pa, dt_raw, xb = _inproj(h2d, _wcast(w_t, layer, 0, W_A, "wcast_a"), wdt)
    pb = _mm(xb, _wcast(w_t, layer, W_A + SSD_HEADS, W_B, "wcast_b"), BF16, TM_INB, TN_INB, "inproj_b", vmem_mb=56)

    y_ssd = _ssd(
        pa, dt_raw,
        conv_w[:, :SSD_D_INNER], conv_b[None, :SSD_D_INNER], conv_w[:, SSD_D_INNER:], conv_b[None, SSD_D_INNER:],
        _pad_lanes(dt_bias), _pad_lanes(a_log),
        jnp.repeat(d_skip, SSD_HEAD_DIM)[None, :], ssd_norm_g[None, :], batch, seq)

    y_swa = _swa(pb, swa_sinks.astype(F32), _swa_bias_table(rel_bias), batch, seq)

    memkv = _mm(mem2d, w_mem_kv, BF16, batch * MEM_LEN, 1024, "mem_kv")
    y_xa = _xa(pb, memkv, batch, seq)

    merged = _merge(y_ssd, y_swa, y_xa, w_ssd_o, w_swa_o, w_xa_o, pb)
    h1, hp = _out_ln(merged, w_out.astype(BF16), h2d, ln1_g[None, :], ln1_b[None, :], alpha)

    rwt = router_w.T
    r_hi = rwt.astype(BF16)
    r_lo = (rwt - r_hi.astype(F32)).astype(BF16)
    ti = jnp.arange(TM_RT)
    su = (ti[:, None] < ti[None, :]).astype(BF16)
    idx_t, w_t, rank_t, cnt = _router(h1, r_hi, r_lo, router_bias[:, None].astype(F32), su)

    counts = cnt[:, 0].astype(I32)
    padded = (counts + MOE_BLK - 1) // MOE_BLK * MOE_BLK
    pend = jnp.cumsum(padded)
    pstart = pend - padded
    n_blocks = t * TOP_K // MOE_BLK + N_EXPERTS
    blk_lo = jnp.arange(n_blocks, dtype=I32) * MOE_BLK
    block_e = jnp.minimum(jnp.sum((pend[None, :] <= blk_lo[:, None]).astype(I32), axis=1), N_EXPERTS - 1)
    of_e = block_e[:, None] == jnp.arange(N_EXPERTS, dtype=I32)[None, :]
    valid_end = jnp.sum(jnp.where(of_e, (pstart + counts)[None, :], 0), axis=1)
    block_nv = jnp.clip(valid_end - blk_lo, 0, MOE_BLK).astype(I32)
    eids = jnp.arange(N_EXPERTS, dtype=I32)
    later_active = (eids[None, :] > eids[:, None]) & (padded > 0)[None, :]
    next_active = jnp.min(jnp.where(later_active, eids[None, :], N_EXPERTS), axis=1)
    next_active = jnp.where(next_active >= N_EXPERTS, -1, next_active)
    seg_start = jnp.sum(jnp.where(of_e, pstart[None, :], 0), axis=1)
    block_first = ((blk_lo == seg_start) & (block_nv > 0)).astype(I32)
    block_next = jnp.sum(jnp.where(of_e, next_active[None, :], 0), axis=1).astype(I32)

    dest_flat = _dest(idx_t, rank_t, pstart.astype(F32)[:, None]).reshape(-1)
    n_slots = n_blocks * MOE_BLK
    pad_info = jnp.concatenate([pstart + counts, padded - counts, pend[-1:],
                                (n_slots - pend[-1:]) // MOE_BLK]).astype(I32)
    x_sorted, shared = _dispatch(dest_flat, pad_info, hp, h1, ws1.astype(BF16), ws3.astype(BF16),
                                 ws2.astype(BF16), n_slots)
    ys = _experts(block_e, block_nv, block_first, block_next,
                  x_sorted.reshape(n_slots * ROW_TILE, LANES), w1, w3, w2)
    return _final(dest_flat, ys.reshape(n_slots, ROW_TILE, LANES), w_t.T, h1, shared, ln2_g[None, :],
                  ln2_b[None, :], alpha)


def kernel(x, mem, w_in, conv_w, conv_b, dt_bias, a_log, d_skip, ssd_norm_g, swa_sinks, rel_bias, w_mem_kv, w_ssd_o, w_swa_o, w_xa_o, w_out, ln1_g, ln1_b, router_w, router_bias, w1, w3, w2, ws1, ws3, ws2, ln2_g, ln2_b):
    batch, seq, d = x.shape
    depth = w_in.shape[0]
    alpha = (2.0 * depth) ** 0.25
    h = x.reshape(batch * seq, d)
    mem2d = mem.reshape(batch * MEM_LEN, d)
    for i in range(depth):
        h = _layer(h, mem2d, batch, seq, w_in, i, conv_w[i], conv_b[i], dt_bias[i], a_log[i], d_skip[i],
                   ssd_norm_g[i], swa_sinks[i], rel_bias, w_mem_kv[i], w_ssd_o[i], w_swa_o[i], w_xa_o[i], w_out[i],
                   ln1_g[i], ln1_b[i], router_w[i], router_bias[i], w1[i], w3[i], w2[i], ws1[i], ws3[i], ws2[i],
                   ln2_g[i], ln2_b[i], alpha)
    return h.reshape(batch, seq, d)
```

```python
import functools
import math

import jax
import jax.numpy as jnp
from jax import lax
from jax.experimental import pallas as pl
from jax.experimental.pallas import tpu as pltpu

F32 = jnp.float32
BF16 = jnp.bfloat16
I32 = jnp.int32
U32 = jnp.uint32

D_MODEL = 2048
MEM_LEN = 256
SSD_D_INNER = D_MODEL
SSD_HEAD_DIM = 64
SSD_HEADS = 32
SSD_GROUPS = 4
SSD_STATE = 128
SSD_CONV = 4
SSD_CHUNK = 128
SWA_HEADS = 16
SWA_KV_HEADS = 4
SWA_HEAD_DIM = 64
SWA_WINDOW = 128
SWA_BLOCK = 128
REL_BUCKETS = 32
REL_MAX_DIST = 128
XA_HEADS = 4
XA_HEAD_DIM = 256
N_BRANCH = 3
N_EXPERTS = 64
TOP_K = 8
N_EXPERT_GROUPS = 8
TOPK_GROUPS = 4
EXPERT_DIM = 512
SHARED_DIM = 512
ROUTED_SCALE = 2.5
LN_EPS = 1e-5
RMS_EPS = 1e-5

SWA_Q_DIM = SWA_HEADS * SWA_HEAD_DIM
SWA_KV_DIM = SWA_KV_HEADS * SWA_HEAD_DIM
XA_DIM = XA_HEADS * XA_HEAD_DIM
BC_DIM = 2 * SSD_GROUPS * SSD_STATE

W_A = SSD_D_INNER + SSD_D_INNER + BC_DIM
OFFA_Z = 0
OFFA_XS = OFFA_Z + SSD_D_INNER
OFFA_BC = OFFA_XS + SSD_D_INNER
W_B = SWA_Q_DIM + 2 * SWA_KV_DIM + XA_DIM + N_BRANCH * D_MODEL
OFFB_Q = 0
OFFB_K = OFFB_Q + SWA_Q_DIM
OFFB_V = OFFB_K + SWA_KV_DIM
OFFB_QX = OFFB_V + SWA_KV_DIM
OFFB_GATES = OFFB_QX + XA_DIM

LANES = 128
MOE_BLK = 256
NEG_INF = float("-inf")


VMEM_MB = 48
VMEM_MB_LARGE = 56
VMEM_MB_EXPERTS = 58


def _cparams(sem, vmem_mb=VMEM_MB):
    return pltpu.CompilerParams(dimension_semantics=sem, vmem_limit_bytes=vmem_mb * 1024 * 1024)


def _dot(a, b):
    return jnp.dot(a, b, preferred_element_type=F32)


def _dot_nt(a, b):
    return lax.dot_general(a, b, (((1,), (1,)), ((), ())), preferred_element_type=F32)


def _sigmoid(x):
    return 1.0 / (1.0 + jnp.exp(-x))


def _silu(x):
    return x * _sigmoid(x)


TM_IN = 512
TN_IN = W_A // 2


def _inproj_kernel(x_ref, w_ref, wdt_ref, p_ref, dt_ref, xb_ref):
    @pl.when(pl.program_id(1) == 0)
    def _():
        xb = x_ref[...].astype(BF16)
        xb_ref[...] = xb
        dt_ref[...] = _dot(xb, wdt_ref[...])

    p_ref[...] = _dot(xb_ref[...], w_ref[...]).astype(BF16)


def _inproj(x2d, w_in, wdt):
    t = x2d.shape[0]
    return pl.pallas_call(
        _inproj_kernel,
        grid=(t // TM_IN, W_A // TN_IN),
        in_specs=[
            pl.BlockSpec((TM_IN, D_MODEL), lambda i, j: (i, 0)),
            pl.BlockSpec((D_MODEL, TN_IN), lambda i, j: (0, j)),
            pl.BlockSpec((D_MODEL, LANES), lambda i, j: (0, 0)),
        ],
        out_specs=[
            pl.BlockSpec((TM_IN, TN_IN), lambda i, j: (i, j)),
            pl.BlockSpec((TM_IN, LANES), lambda i, j: (i, 0)),
            pl.BlockSpec((TM_IN, D_MODEL), lambda i, j: (i, 0)),
        ],
        out_shape=[jax.ShapeDtypeStruct((t, W_A), BF16), jax.ShapeDtypeStruct((t, LANES), F32),
                   jax.ShapeDtypeStruct((t, D_MODEL), BF16)],
        compiler_params=_cparams(("parallel", "arbitrary"), vmem_mb=VMEM_MB_LARGE),
        name="inproj",
    )(x2d, w_in, wdt)


TN_WC = 512
ROW_ALIGN = 32


def _wcast_kernel(a_ref, o_ref):
    o_ref[...] = a_ref[0].T.astype(BF16)


def _wcast(w_t, layer, start, width, name):
    k = w_t.shape[2]
    return pl.pallas_call(
        _wcast_kernel,
        grid=(width // TN_WC,),
        in_specs=[pl.BlockSpec((pl.Element(1), pl.Element(TN_WC), pl.Element(k)),
                               lambda j: (layer, pl.multiple_of(start + j * TN_WC, ROW_ALIGN), 0))],
        out_specs=pl.BlockSpec((k, TN_WC), lambda j: (0, j)),
        out_shape=jax.ShapeDtypeStruct((k, width), BF16),
        compiler_params=_cparams(("parallel",)),
        name=name,
    )(w_t)


def _mm_kernel(a_ref, b_ref, o_ref):
    o_ref[...] = _dot(a_ref[...].astype(BF16), b_ref[...].astype(BF16)).astype(o_ref.dtype)


def _mm(a, b, out_dtype, tm, tn, name, vmem_mb=VMEM_MB):
    m, k = a.shape
    n = b.shape[1]
    return pl.pallas_call(
        _mm_kernel,
        grid=(m // tm, n // tn),
        in_specs=[pl.BlockSpec((tm, k), lambda i, j: (i, 0)), pl.BlockSpec((k, tn), lambda i, j: (0, j))],
        out_specs=pl.BlockSpec((tm, tn), lambda i, j: (i, j)),
        out_shape=jax.ShapeDtypeStruct((m, n), out_dtype),
        compiler_params=_cparams(("parallel", "parallel"), vmem_mb=vmem_mb),
        name=name,
    )(a, b)


Q = SSD_CHUNK


def _conv_shift_matrix():
    t = jnp.arange(Q)[:, None]
    j = jnp.arange(2 * Q)[None, :]
    blocks = []
    for k in range(SSD_CONV - 1):
        src = t - (SSD_CONV - 1 - k)
        blocks.append(j == jnp.where(src >= 0, src, 2 * Q + src))
    return jnp.concatenate(blocks, axis=0).astype(BF16)


def _ssd_kernel(xs_ref, bc_ref, z_ref, dtr_ref, shift_ref, cwx_ref, cbx_ref, cwb_ref, cbb_ref, dtb_ref, alog_ref,
                dsk_ref, ng_ref, o_ref, hx_ref, hb_ref, st_ref, cumt_ref, dtt_ref, wt_ref, y_ref):
    c = pl.program_id(1)

    @pl.when(c == 0)
    def _():
        hx_ref[...] = jnp.zeros_like(hx_ref)
        hb_ref[...] = jnp.zeros_like(hb_ref)
        st_ref[...] = jnp.zeros_like(st_ref)

    def conv_silu(in_ref, prev_ref, w_ref, b_ref):
        cur = in_ref[...]
        ext = jnp.concatenate([cur, prev_ref[...]], axis=0)
        shifted = _dot(shift_ref[...], ext)
        acc = b_ref[...] + w_ref[SSD_CONV - 1:SSD_CONV, :] * cur.astype(F32)
        for k in range(SSD_CONV - 1):
            acc = acc + w_ref[k:k + 1, :] * shifted[k * Q:(k + 1) * Q, :]
        prev_ref[...] = cur
        return _silu(acc)

    xs_c = conv_silu(xs_ref, hx_ref, cwx_ref, cbx_ref)
    bc_c = conv_silu(bc_ref, hb_ref, cwb_ref, cbb_ref)

    lane = lax.broadcasted_iota(I32, (Q, LANES), 1)
    row = lax.broadcasted_iota(I32, (Q, LANES), 0)
    tril = row >= lane
    lo_mask = lane < SSD_HEAD_DIM

    dtv = dtr_ref[...] + dtb_ref[...]
    dt = jnp.maximum(dtv, 0.0) + jnp.log1p(jnp.exp(-jnp.abs(dtv)))
    a = jnp.where(lane[0:1, :] < SSD_HEADS, -jnp.exp(alog_ref[...]), 0.0)
    la = dt * a
    tri = jnp.where(tril, 1.0, 0.0).astype(BF16)
    p1 = la.astype(BF16)
    r1 = la - p1.astype(F32)
    p2 = r1.astype(BF16)
    p3 = (r1 - p2.astype(F32)).astype(BF16)
    cum = _dot(tri, p1) + _dot(tri, p2) + _dot(tri, p3)
    ecum = jnp.exp(cum)
    cum_t = cum.T
    dt_t = dt.T
    last = cum_t[:, Q - 1:Q]
    cumt_ref[...] = cum_t
    dtt_ref[...] = dt_t
    wt_ref[...] = jnp.exp(last - cum_t) * dt_t
    chunk_decay = jnp.exp(last)

    def head_parts(h, cb, cm, bm_t):
        col = jnp.broadcast_to(cum[:, h:h + 1], (Q, Q))
        rw = cumt_ref[h:h + 1, :]
        dec = jnp.exp(jnp.where(tril, col - rw, NEG_INF))
        m = cb * (dec * dtt_ref[h:h + 1, :])
        ce = cm * jnp.broadcast_to(ecum[:, h:h + 1], (Q, Q))
        sl = bm_t * wt_ref[h:h + 1, :]
        return m.astype(BF16), ce.astype(BF16), sl.astype(BF16)

    for g in range(SSD_GROUPS):
        bm = bc_c[:, g * SSD_STATE:(g + 1) * SSD_STATE]
        cm = bc_c[:, BC_DIM // 2 + g * SSD_STATE:BC_DIM // 2 + (g + 1) * SSD_STATE]
        cb = _dot_nt(cm.astype(BF16), bm.astype(BF16))
        bm_t = bm.T
        for qd in range(SSD_HEADS // SSD_GROUPS // 2):
            hp = g * (SSD_HEADS // SSD_GROUPS // 2) + qd
            ha, hb = 2 * hp, 2 * hp + 1
            sl_ = slice(hp * LANES, (hp + 1) * LANES)
            xs_pair = xs_c[:, sl_]
            x_lo = jnp.where(lo_mask, xs_pair, 0.0).astype(BF16)
            x_hi = jnp.where(lo_mask, 0.0, xs_pair).astype(BF16)
            h_prev = st_ref[:, sl_]
            h_lo = jnp.where(lo_mask, h_prev, 0.0).astype(BF16)
            h_hi = jnp.where(lo_mask, 0.0, h_prev).astype(BF16)
            m_a, ce_a, sl_a = head_parts(ha, cb, cm, bm_t)
            m_b, ce_b, sl_b = head_parts(hb, cb, cm, bm_t)
            lhs = jnp.concatenate([m_a, ce_a, m_b, ce_b], axis=1)
            rhs = jnp.concatenate([x_lo, h_lo, x_hi, h_hi], axis=0)
            y_ref[:, sl_] = _dot(lhs, rhs)
            new = _dot(jnp.concatenate([sl_a, sl_b], axis=1), jnp.concatenate([x_lo, x_hi], axis=0))
            cd = jnp.where(lo_mask[0:1, :], chunk_decay[ha:ha + 1, :], chunk_decay[hb:hb + 1, :])
            st_ref[:, sl_] = h_prev * cd + new

    y = y_ref[...] + xs_c * dsk_ref[...]
    y = y * _silu(z_ref[...].astype(F32))
    gw = SSD_D_INNER // SSD_GROUPS
    for g in range(SSD_GROUPS):
        yg = y[:, g * gw:(g + 1) * gw]
        ms = jnp.mean(yg * yg, axis=-1, keepdims=True)
        o_ref[:, g * gw:(g + 1) * gw] = ((yg * lax.rsqrt(ms + RMS_EPS)) * ng_ref[:, g * gw:(g + 1) * gw]).astype(BF16)


def _ssd(p, dt_raw, cwx, cbx, cwb, cbb, dtb, alog, dsk, ng, batch, seq):
    nc = seq // Q
    t = batch * seq
    rowblk = lambda b, c: b * nc + c
    const = lambda b, c: (0, 0)
    return pl.pallas_call(
        _ssd_kernel,
        grid=(batch, nc),
        in_specs=[
            pl.BlockSpec((Q, SSD_D_INNER), lambda b, c: (rowblk(b, c), OFFA_XS // SSD_D_INNER)),
            pl.BlockSpec((Q, BC_DIM), lambda b, c: (rowblk(b, c), OFFA_BC // BC_DIM)),
            pl.BlockSpec((Q, SSD_D_INNER), lambda b, c: (rowblk(b, c), OFFA_Z // SSD_D_INNER)),
            pl.BlockSpec((Q, LANES), lambda b, c: (rowblk(b, c), 0)),
            pl.BlockSpec(((SSD_CONV - 1) * Q, 2 * Q), const),
            pl.BlockSpec((SSD_CONV, SSD_D_INNER), const),
            pl.BlockSpec((1, SSD_D_INNER), const),
            pl.BlockSpec((SSD_CONV, BC_DIM), const),
            pl.BlockSpec((1, BC_DIM), const),
            pl.BlockSpec((1, LANES), const),
            pl.BlockSpec((1, LANES), const),
            pl.BlockSpec((1, SSD_D_INNER), const),
            pl.BlockSpec((1, SSD_D_INNER), const),
        ],
        out_specs=pl.BlockSpec((Q, SSD_D_INNER), lambda b, c: (rowblk(b, c), 0)),
        out_shape=jax.ShapeDtypeStruct((t, SSD_D_INNER), BF16),
        scratch_shapes=[
            pltpu.VMEM((Q, SSD_D_INNER), BF16),
            pltpu.VMEM((Q, BC_DIM), BF16),
            pltpu.VMEM((SSD_STATE, SSD_D_INNER), F32),
            pltpu.VMEM((LANES, Q), F32),
            pltpu.VMEM((LANES, Q), F32),
            pltpu.VMEM((LANES, Q), F32),
            pltpu.VMEM((Q, SSD_D_INNER), F32),
        ],
        compiler_params=_cparams(("parallel", "arbitrary")),
        name="ssd",
    )(p, p, p, dt_raw, _conv_shift_matrix(), cwx, cbx, cwb, cbb, dtb, alog, dsk, ng)


N_PAIRS = SWA_HEADS // 2


def _swa_kernel(sink_ref, q_ref, kp_ref, kc_ref, vp_ref, vc_ref, bias_ref, o_ref):
    blk = SWA_BLOCK
    kband = jnp.concatenate([kp_ref[...], kc_ref[...]], axis=0)
    vband = jnp.concatenate([vp_ref[...], vc_ref[...]], axis=0)

    lane = lax.broadcasted_iota(I32, (1, LANES), 1)
    lo_f = jnp.where(lane < SWA_HEAD_DIM, 1.0, 0.0)
    lo_m = lo_f.astype(BF16)
    hi_m = (1.0 - lo_f).astype(BF16)

    def head_twice(band, kv):
        tile = band[:, (kv // 2) * LANES:(kv // 2 + 1) * LANES]
        swapped = jnp.concatenate([tile[:, SWA_HEAD_DIM:], tile[:, :SWA_HEAD_DIM]], axis=1)
        return tile * lo_m + swapped * hi_m if kv % 2 == 0 else swapped * lo_m + tile * hi_m

    k_twice = [head_twice(kband, kv) for kv in range(SWA_KV_HEADS)]
    v_twice = [head_twice(vband, kv) for kv in range(SWA_KV_HEADS)]
    rowi = lax.broadcasted_iota(I32, (2 * blk, 1), 0)
    scale = SWA_HEAD_DIM ** -0.5
    assert math.frexp(scale)[0] == 0.5
    lo_q = (lo_f * scale).astype(BF16)
    hi_q = ((1.0 - lo_f) * scale).astype(BF16)
    for p in range(N_PAIRS):
        kv = p // (SWA_HEADS // SWA_KV_HEADS // 2)
        qp = q_ref[:, p * LANES:(p + 1) * LANES]
        qs = jnp.concatenate([qp * lo_q, qp * hi_q], axis=0)
        s = _dot_nt(qs, k_twice[kv]) + bias_ref[0, p]
        sink = jnp.where(rowi < blk, sink_ref[2 * p], sink_ref[2 * p + 1])
        m = jnp.maximum(jnp.max(s, axis=-1, keepdims=True), sink)
        e = jnp.exp(s - m)
        den = jnp.sum(e, axis=-1, keepdims=True) + jnp.exp(sink - m)
        pr = (e * (1.0 / den)).astype(BF16)
        vd = v_twice[kv]
        lhs = jnp.concatenate([pr[:blk], pr[blk:]], axis=1)
        rhs = jnp.concatenate([vd * lo_m, vd * hi_m], axis=0)
        o_ref[:, p * LANES:(p + 1) * LANES] = _dot(lhs, rhs).astype(BF16)


def _swa(p, sinks, bias, batch, seq):
    nb = seq // SWA_BLOCK
    t = batch * seq
    kw = SWA_KV_DIM
    cur = lambda b, n, s: b * nb + n
    prev = lambda b, n, s: b * nb + jnp.maximum(n - 1, 0)
    return pl.pallas_call(
        _swa_kernel,
        grid_spec=pltpu.PrefetchScalarGridSpec(
            num_scalar_prefetch=1,
            grid=(batch, nb),
            in_specs=[
                pl.BlockSpec((SWA_BLOCK, SWA_Q_DIM), lambda b, n, s: (cur(b, n, s), OFFB_Q // SWA_Q_DIM)),
                pl.BlockSpec((SWA_BLOCK, kw), lambda b, n, s: (prev(b, n, s), OFFB_K // kw)),
                pl.BlockSpec((SWA_BLOCK, kw), lambda b, n, s: (cur(b, n, s), OFFB_K // kw)),
                pl.BlockSpec((SWA_BLOCK, kw), lambda b, n, s: (prev(b, n, s), OFFB_V // kw)),
                pl.BlockSpec((SWA_BLOCK, kw), lambda b, n, s: (cur(b, n, s), OFFB_V // kw)),
                pl.BlockSpec((1, N_PAIRS, 2 * SWA_BLOCK, 2 * SWA_BLOCK),
                             lambda b, n, s: (jnp.minimum(n, 1), 0, 0, 0)),
            ],
            out_specs=pl.BlockSpec((SWA_BLOCK, SWA_Q_DIM), lambda b, n, s: (cur(b, n, s), 0)),
        ),
        out_shape=jax.ShapeDtypeStruct((t, SWA_Q_DIM), BF16),
        compiler_params=_cparams(("parallel", "arbitrary")),
        name="swa",
    )(sinks, p, p, p, p, p, bias)


def _t5_causal_bucket(dist):
    max_exact = REL_BUCKETS // 2
    large = max_exact + (jnp.log(jnp.maximum(dist, 1).astype(F32) / max_exact)
                         / math.log(REL_MAX_DIST / max_exact) * (REL_BUCKETS - max_exact)).astype(I32)
    large = jnp.minimum(large, REL_BUCKETS - 1)
    return jnp.where(dist < max_exact, dist, large)


def _swa_bias_table(rel_bias):
    qi = jnp.arange(SWA_BLOCK)[:, None]
    kj = jnp.arange(2 * SWA_BLOCK)[None, :]
    dist = qi + SWA_BLOCK - kj
    in_window = (dist >= 0) & (dist < SWA_WINDOW)
    bucket = _t5_causal_bucket(jnp.maximum(dist, 0))
    onehot = (bucket[:, :, None] == jnp.arange(REL_BUCKETS)[None, None, :]).astype(F32)
    bias = jnp.einsum('ijb,bh->hij', onehot, rel_bias.astype(F32), precision=lax.Precision.HIGHEST)
    rest = jnp.where(in_window[None, :, :], bias, NEG_INF)
    first = jnp.where((kj >= SWA_BLOCK)[None, :, :], rest, NEG_INF)
    return jnp.stack([first, rest]).reshape(2, N_PAIRS, 2 * SWA_BLOCK, 2 * SWA_BLOCK)


TL_XA = 512


XA_HALF = XA_DIM // 2


def _xa_kernel(qa_ref, qb_ref, mk_ref, mv_ref, o_ref):
    for h in range(XA_HEADS):
        q_ref = qa_ref if h < XA_HEADS // 2 else qb_ref
        qc = (h % (XA_HEADS // 2)) * XA_HEAD_DIM
        cols = slice(h * XA_HEAD_DIM, (h + 1) * XA_HEAD_DIM)
        s = _dot_nt(q_ref[:, qc:qc + XA_HEAD_DIM], mk_ref[:, cols]) * (XA_HEAD_DIM ** -0.5)
        m = jnp.max(s, axis=-1, keepdims=True)
        e = jnp.exp(s - m)
        pr = e * (1.0 / jnp.sum(e, axis=-1, keepdims=True))
        o_ref[:, cols] = _dot(pr.astype(BF16), mv_ref[:, cols]).astype(BF16)


def _xa(p, memkv, batch, seq):
    nl = seq // TL_XA
    t = batch * seq
    return pl.pallas_call(
        _xa_kernel,
        grid=(batch, nl),
        in_specs=[
            pl.BlockSpec((TL_XA, XA_HALF), lambda b, l: (b * nl + l, OFFB_QX // XA_HALF)),
            pl.BlockSpec((TL_XA, XA_HALF), lambda b, l: (b * nl + l, OFFB_QX // XA_HALF + 1)),
            pl.BlockSpec((MEM_LEN, XA_DIM), lambda b, l: (b, 0)),
            pl.BlockSpec((MEM_LEN, XA_DIM), lambda b, l: (b, 1)),
        ],
        out_specs=pl.BlockSpec((TL_XA, XA_DIM), lambda b, l: (b * nl + l, 0)),
        out_shape=jax.ShapeDtypeStruct((t, XA_DIM), BF16),
        compiler_params=_cparams(("parallel", "parallel")),
        name="xattn",
    )(p, p, memkv, memkv)


TM_MG = 1024
TN_MG = 512


N_SUB_MG = 2


def _merge_kernel(ys_ref, yw_ref, yx_ref, ws_ref, ww_ref, wx_ref, g0_ref, g1_ref, g2_ref, o_ref):
    sub = TM_MG // N_SUB_MG
    ws = ws_ref[...].astype(BF16)
    ww = ww_ref[...].astype(BF16)
    wx = wx_ref[...].astype(BF16)
    for s in range(N_SUB_MG):
        rows = slice(s * sub, (s + 1) * sub)
        a = _dot(ys_ref[rows, :], ws)
        b = _dot(yw_ref[rows, :], ww)
        c = _dot(yx_ref[rows, :], wx)
        g0 = _sigmoid(g0_ref[rows, :].astype(F32))
        g1 = _sigmoid(g1_ref[rows, :].astype(F32))
        g2 = _sigmoid(g2_ref[rows, :].astype(F32))
        o_ref[rows, :] = (g0 * a + g1 * b + g2 * c).astype(BF16)


def _merge(y_ssd, y_swa, y_xa, w_ssd_o, w_swa_o, w_xa_o, p):
    t = y_ssd.shape[0]
    gpb = D_MODEL // TN_MG
    gate = lambda k: pl.BlockSpec((TM_MG, TN_MG), lambda i, j: (i, OFFB_GATES // TN_MG + k * gpb + j))
    return pl.pallas_call(
        _merge_kernel,
        grid=(t // TM_MG, D_MODEL // TN_MG),
        in_specs=[
            pl.BlockSpec((TM_MG, SSD_D_INNER), lambda i, j: (i, 0)),
            pl.BlockSpec((TM_MG, SWA_Q_DIM), lambda i, j: (i, 0)),
            pl.BlockSpec((TM_MG, XA_DIM), lambda i, j: (i, 0)),
            pl.BlockSpec((SSD_D_INNER, TN_MG), lambda i, j: (0, j)),
            pl.BlockSpec((SWA_Q_DIM, TN_MG), lambda i, j: (0, j)),
            pl.BlockSpec((XA_DIM, TN_MG), lambda i, j: (0, j)),
            gate(0), gate(1), gate(2),
        ],
        out_specs=pl.BlockSpec((TM_MG, TN_MG), lambda i, j: (i, j)),
        out_shape=jax.ShapeDtypeStruct((t, D_MODEL), BF16),
        compiler_params=_cparams(("parallel", "parallel")),
        name="merge",
    )(y_ssd, y_swa, y_xa, w_ssd_o, w_swa_o, w_xa_o, p, p, p)


TM_LN = 512


def _layer_norm(v, g, b):
    mu = jnp.mean(v, axis=-1, keepdims=True)
    d = v - mu
    var = jnp.mean(d * d, axis=-1, keepdims=True)
    return (d * lax.rsqrt(var + LN_EPS)) * g + b


def _pack_halves(v):
    n = v.shape[1] // 2
    lo = lax.bitcast_convert_type(v[:, :n].astype(BF16).astype(F32), U32) >> 16
    hi = lax.bitcast_convert_type(v[:, n:].astype(BF16).astype(F32), U32) & jnp.uint32(0xFFFF0000)
    return hi | lo


def _unpack_halves(w):
    lo = lax.bitcast_convert_type(w << 16, F32)
    hi = lax.bitcast_convert_type(w & jnp.uint32(0xFFFF0000), F32)
    return lo, hi


ROW_WORDS = D_MODEL // 2
ROW_TILE = ROW_WORDS // LANES


def _store_row_tiled(ref, words):
    m = words.shape[0]
    for c in range(ROW_TILE):
        ref[pl.ds(c, m, stride=ROW_TILE), :] = words[:, c * LANES:(c + 1) * LANES]


def _load_row_tiled(ref, m):
    return jnp.concatenate([ref[pl.ds(c, m, stride=ROW_TILE), :] for c in range(ROW_TILE)], axis=1)


N_SUB_LN = 4


def _out_ln_kernel(alpha, m_ref, w_ref, x_ref, g_ref, b_ref, h_ref, hp_ref):
    sub = TM_LN // N_SUB_LN
    for c in range(N_SUB_LN):
        rows = slice(c * sub, (c + 1) * sub)
        mix = _dot(m_ref[rows, :], w_ref[...])
        hn = _layer_norm(alpha * x_ref[rows, :] + mix, g_ref[...], b_ref[...])
        h_ref[rows, :] = hn
        _store_row_tiled(hp_ref.at[pl.ds(c * sub * ROW_TILE, sub * ROW_TILE), :], _pack_halves(hn))


def _out_ln(merged, w_out, x2d, g, b, alpha):
    t = x2d.shape[0]
    row = lambda i: (i, 0)
    const = lambda i: (0, 0)
    return pl.pallas_call(
        functools.partial(_out_ln_kernel, alpha),
        grid=(t // TM_LN,),
        in_specs=[
            pl.BlockSpec((TM_LN, D_MODEL), row),
            pl.BlockSpec((D_MODEL, D_MODEL), const),
            pl.BlockSpec((TM_LN, D_MODEL), row),
            pl.BlockSpec((1, D_MODEL), const),
            pl.BlockSpec((1, D_MODEL), const),
        ],
        out_specs=[pl.BlockSpec((TM_LN, D_MODEL), row), pl.BlockSpec((TM_LN * ROW_TILE, LANES), row)],
        out_shape=[jax.ShapeDtypeStruct((t, D_MODEL), F32), jax.ShapeDtypeStruct((t * ROW_TILE, LANES), U32)],
        compiler_params=_cparams(("parallel",)),
        name="out_ln1",
    )(merged, w_out, x2d, g, b)


TM_RT = 512
EPG = N_EXPERTS // N_EXPERT_GROUPS


def _router_kernel(h_ref, whi_ref, wlo_ref, bias_ref, su_ref, idx_ref, w_ref, rank_ref, cnt_ref, run_ref):
    @pl.when(pl.program_id(0) == 0)
    def _():
        run_ref[...] = jnp.zeros_like(run_ref)

    h = h_ref[...]
    h_hi = h.astype(BF16)
    h_lo = (h - h_hi.astype(F32)).astype(BF16)
    whi = whi_ref[...]
    logits = _dot_nt(whi, h_hi) + _dot_nt(whi, h_lo) + _dot_nt(wlo_ref[...], h_hi)
    scores = _sigmoid(logits)
    sel = scores + bias_ref[...]

    iota_m = lax.broadcasted_iota(I32, (EPG, TM_RT), 0).astype(F32)
    iota_g = lax.broadcasted_iota(I32, (N_EXPERT_GROUPS, TM_RT), 0).astype(F32)
    gs = []
    for g in range(N_EXPERT_GROUPS):
        v = sel[g * EPG:(g + 1) * EPG, :]
        m1 = jnp.max(v, axis=0, keepdims=True)
        i1 = jnp.min(jnp.where(v == m1, iota_m, float(EPG)), axis=0, keepdims=True)
        m2 = jnp.max(jnp.where(iota_m == i1, NEG_INF, v), axis=0, keepdims=True)
        gs.append(m1 + m2)
    gsc = jnp.concatenate(gs, axis=0)
    chosen = jnp.zeros((N_EXPERT_GROUPS, TM_RT), F32)
    for _ in range(TOPK_GROUPS):
        m = jnp.max(gsc, axis=0, keepdims=True)
        ig = jnp.min(jnp.where(gsc == m, iota_g, float(N_EXPERT_GROUPS)), axis=0, keepdims=True)
        hit = iota_g == ig
        chosen = jnp.where(hit, 1.0, chosen)
        gsc = jnp.where(hit, NEG_INF, gsc)
    gmask = jnp.concatenate(
        [jnp.broadcast_to(chosen[g:g + 1, :], (EPG, TM_RT)) for g in range(N_EXPERT_GROUPS)], axis=0)
    masked = jnp.where(gmask > 0.0, sel, NEG_INF)

    iota_e = lax.broadcasted_iota(I32, (N_EXPERTS, TM_RT), 0).astype(F32)
    hits, idxs, ws = [], [], []
    for _ in range(TOP_K):
        m = jnp.max(masked, axis=0, keepdims=True)
        ie = jnp.min(jnp.where(masked == m, iota_e, float(N_EXPERTS)), axis=0, keepdims=True)
        hit = iota_e == ie
        hits.append(hit)
        idxs.append(ie)
        ws.append(jnp.sum(jnp.where(hit, scores, 0.0), axis=0, keepdims=True))
        masked = jnp.where(hit, NEG_INF, masked)
    wsum = ws[0]
    for k in range(1, TOP_K):
        wsum = wsum + ws[k]
    inv = 1.0 / wsum
    idx_ref[...] = jnp.concatenate(idxs, axis=0).astype(I32)
    w_ref[...] = jnp.concatenate([w * inv * ROUTED_SCALE for w in ws], axis=0)

    assign = jnp.zeros((N_EXPERTS, TM_RT), F32)
    for hit in hits:
        assign = jnp.where(hit, 1.0, assign)
    pos = _dot(assign.astype(BF16), su_ref[...]) + run_ref[...]
    rank_ref[...] = jnp.concatenate(
        [jnp.sum(jnp.where(hit, pos, 0.0), axis=0, keepdims=True) for hit in hits], axis=0).astype(I32)
    run = run_ref[...] + jnp.sum(assign, axis=1, keepdims=True)
    run_ref[...] = run
    cnt_ref[...] = run


def _router(h1, r_hi, r_lo, rbias, su):
    t = h1.shape[0]
    tile = lambda i: (0, i)
    const = lambda i: (0, 0)
    return pl.pallas_call(
        _router_kernel,
        grid=(t // TM_RT,),
        in_specs=[
            pl.BlockSpec((TM_RT, D_MODEL), lambda i: (i, 0)),
            pl.BlockSpec((N_EXPERTS, D_MODEL), const),
            pl.BlockSpec((N_EXPERTS, D_MODEL), const),
            pl.BlockSpec((N_EXPERTS, 1), const),
            pl.BlockSpec((TM_RT, TM_RT), const),
        ],
        out_specs=[
            pl.BlockSpec((TOP_K, TM_RT), tile),
            pl.BlockSpec((TOP_K, TM_RT), tile),
            pl.BlockSpec((TOP_K, TM_RT), tile),
            pl.BlockSpec((N_EXPERTS, TM_RT), const),
        ],
        out_shape=[
            jax.ShapeDtypeStruct((TOP_K, t), I32),
            jax.ShapeDtypeStruct((TOP_K, t), F32),
            jax.ShapeDtypeStruct((TOP_K, t), I32),
            jax.ShapeDtypeStruct((N_EXPERTS, TM_RT), F32),
        ],
        scratch_shapes=[pltpu.VMEM((N_EXPERTS, TM_RT), F32)],
        compiler_params=_cparams(("arbitrary",)),
        name="router",
    )(h1, r_hi, r_lo, rbias, su)


TM_DS = 2048


def _dest_kernel(idx_ref, rank_ref, ps_ref, d_ref):
    iota_e = lax.broadcasted_iota(I32, (N_EXPERTS, TM_DS), 0)
    ps = ps_ref[...]
    rows = []
    for k in range(TOP_K):
        hit = iota_e == idx_ref[k:k + 1, :]
        rows.append(jnp.sum(jnp.where(hit, ps, 0.0), axis=0, keepdims=True))
    d_ref[...] = jnp.concatenate(rows, axis=0).astype(I32) + rank_ref[...]


def _dest(idx_t, rank_t, pstart):
    t = idx_t.shape[1]
    tile = lambda i: (0, i)
    return pl.pallas_call(
        _dest_kernel,
        grid=(t // TM_DS,),
        in_specs=[pl.BlockSpec((TOP_K, TM_DS), tile), pl.BlockSpec((TOP_K, TM_DS), tile),
                  pl.BlockSpec((N_EXPERTS, 1), lambda i: (0, 0))],
        out_specs=pl.BlockSpec((TOP_K, TM_DS), tile),
        out_shape=jax.ShapeDtypeStruct((TOP_K, t), I32),
        compiler_params=_cparams(("parallel",)),
        name="dest",
    )(idx_t, rank_t, pstart)


TM_DP = 512


PAD_BITS = tuple(1 << i for i in reversed(range(MOE_BLK.bit_length() - 1)))


def _dispatch_kernel(dest_ref, pad_ref, hp_ref, h_ref, ws1_ref, ws3_ref, ws2_ref, xs_ref, sh_ref, zbuf, sem, zsem):
    t_total = pl.num_programs(0) * TM_DP
    base = pl.program_id(0) * TM_DP

    def pad_copies(act):
        def per_expert(e, carry):
            start = pad_ref[e]
            length = pad_ref[N_EXPERTS + e]
            for bit in PAD_BITS:
                @pl.when((length & bit) != 0)
                def _():
                    off = length - (length & (2 * bit - 1))
                    act(pltpu.make_async_copy(zbuf.at[pl.ds(0, bit)], xs_ref.at[pl.ds(start + off, bit)], zsem))
            return carry

        lax.fori_loop(0, N_EXPERTS, per_expert, 0)

        def per_block(i, carry):
            row = pad_ref[2 * N_EXPERTS] + i * MOE_BLK
            act(pltpu.make_async_copy(zbuf, xs_ref.at[pl.ds(row, MOE_BLK)], zsem))
            return carry

        lax.fori_loop(0, pad_ref[2 * N_EXPERTS + 1], per_block, 0)

    @pl.when(pl.program_id(0) == 0)
    def _():
        zbuf[...] = jnp.zeros_like(zbuf)
        pad_copies(lambda cp: cp.start())

    def issue(t, carry):
        src = hp_ref.at[pl.ds(pl.multiple_of(t * ROW_TILE, ROW_TILE), ROW_TILE), :]
        for k in range(TOP_K):
            slot = dest_ref[k * t_total + base + t]
            pltpu.make_async_copy(src, xs_ref.at[slot], sem).start(priority=k % 2)
        return carry

    lax.fori_loop(0, TM_DP, issue, 0)

    xb = h_ref[...].astype(BF16)
    hdn = _silu(_dot(xb, ws1_ref[...])) * _dot(xb, ws3_ref[...])
    sh_ref[...] = _dot(hdn.astype(BF16), ws2_ref[...])

    for k in range(TOP_K):
        pltpu.make_async_copy(hp_ref.reshape(TM_DP, ROW_TILE, LANES), xs_ref.at[pl.ds(0, TM_DP)], sem).wait()

    @pl.when(pl.program_id(0) == 0)
    def _():
        pad_copies(lambda cp: cp.wait())


def _dispatch(dest_flat, pad_info, hp, h1, ws1, ws3, ws2, n_slots):
    t = h1.shape[0]
    row = lambda i, *_: (i, 0)
    const = lambda i, *_: (0, 0)
    return pl.pallas_call(
        _dispatch_kernel,
        grid_spec=pltpu.PrefetchScalarGridSpec(
            num_scalar_prefetch=2,
            grid=(t // TM_DP,),
            in_specs=[
                pl.BlockSpec((TM_DP * ROW_TILE, LANES), row),
                pl.BlockSpec((TM_DP, D_MODEL), row),
                pl.BlockSpec((D_MODEL, SHARED_DIM), const),
                pl.BlockSpec((D_MODEL, SHARED_DIM), const),
                pl.BlockSpec((SHARED_DIM, D_MODEL), const),
            ],
            out_specs=[pl.BlockSpec(memory_space=pl.ANY), pl.BlockSpec((TM_DP, D_MODEL), row)],
            scratch_shapes=[pltpu.VMEM((MOE_BLK, ROW_TILE, LANES), U32), pltpu.SemaphoreType.DMA,
                            pltpu.SemaphoreType.DMA],
        ),
        out_shape=[jax.ShapeDtypeStruct((n_slots, ROW_TILE, LANES), U32), jax.ShapeDtypeStruct((t, D_MODEL), F32)],
        compiler_params=_cparams(("arbitrary",)),
        name="dispatch",
    )(dest_flat, pad_info, hp, h1, ws1, ws3, ws2)


N_WSLOTS = 2
BLK_PER_STEP = 2


def _expert_kernel(be_ref, nv_ref, first_ref, nxt_ref, x_ref, w1_ref, w3_ref, w2_ref, o_ref,
                   wf1, wf3, wf2, w1b, w3b, w2b, slot_ref, sems):
    step = pl.program_id(0)

    def weight_copies(e, s):
        return (pltpu.make_async_copy(w1_ref.at[e], wf1.at[s], sems.at[s, 0]),
                pltpu.make_async_copy(w3_ref.at[e], wf3.at[s], sems.at[s, 1]),
                pltpu.make_async_copy(w2_ref.at[e], wf2.at[s], sems.at[s, 2]))

    @pl.when(step == 0)
    def _():
        slot_ref[0] = 0
        for cp in weight_copies(be_ref[0], 0):
            cp.start()

    def switch_weights(b):
        @pl.when(first_ref[b] == 1)
        def _():
            s = slot_ref[0]
            for cp in weight_copies(be_ref[b], s):
                cp.wait()

            @pl.when(nxt_ref[b] >= 0)
            def _():
                for cp in weight_copies(nxt_ref[b], 1 - s):
                    cp.start(priority=1)

            w1b[...] = wf1[s].astype(BF16)
            w3b[...] = wf3[s].astype(BF16)
            w2b[...] = wf2[s].astype(BF16)
            slot_ref[0] = 1 - s

    def ffn(xq, oq, m):
        lo, hi = _unpack_halves(_load_row_tiled(xq, m))
        x = jnp.concatenate([lo.astype(BF16), hi.astype(BF16)], axis=1)
        hdn = _silu(_dot(x, w1b[...])) * _dot(x, w3b[...])
        _store_row_tiled(oq, _pack_halves(_dot(hdn.astype(BF16), w2b[...])))

    def one_block(b, xq, oq):
        nv = nv_ref[b]

        @pl.when(nv > 0)
        def _():
            ffn(xq, oq, MOE_BLK)

        @pl.when(nv <= 0)
        def _():
            oq[...] = jnp.zeros_like(oq)

    assert BLK_PER_STEP == 2
    blk_rows = MOE_BLK * ROW_TILE
    b0 = step * BLK_PER_STEP
    b1 = b0 + 1
    lo_rows = pl.ds(0, blk_rows)
    hi_rows = pl.ds(blk_rows, blk_rows)
    same = (first_ref[b1] == 0) & (nv_ref[b1] > 0)
    switch_weights(b0)

    @pl.when(same)
    def _():
        ffn(x_ref, o_ref, 2 * MOE_BLK)

    @pl.when(jnp.logical_not(same))
    def _():
        one_block(b0, x_ref.at[lo_rows, :], o_ref.at[lo_rows, :])
        switch_weights(b1)
        one_block(b1, x_ref.at[hi_rows, :], o_ref.at[hi_rows, :])


def _experts(block_e, block_nv, block_first, block_next, x_sorted, w1, w3, w2):
    n_rows = x_sorted.shape[0]
    step_rows = BLK_PER_STEP * MOE_BLK * ROW_TILE
    row = lambda b, *_: (b, 0)
    return pl.pallas_call(
        _expert_kernel,
        grid_spec=pltpu.PrefetchScalarGridSpec(
            num_scalar_prefetch=4,
            grid=(n_rows // step_rows,),
            in_specs=[
                pl.BlockSpec((step_rows, LANES), row),
                pl.BlockSpec(memory_space=pl.ANY),
                pl.BlockSpec(memory_space=pl.ANY),
                pl.BlockSpec(memory_space=pl.ANY),
            ],
            out_specs=pl.BlockSpec((step_rows, LANES), row),
            scratch_shapes=[
                pltpu.VMEM((N_WSLOTS, D_MODEL, EXPERT_DIM), F32),
                pltpu.VMEM((N_WSLOTS, D_MODEL, EXPERT_DIM), F32),
                pltpu.VMEM((N_WSLOTS, EXPERT_DIM, D_MODEL), F32),
                pltpu.VMEM((D_MODEL, EXPERT_DIM), BF16),
                pltpu.VMEM((D_MODEL, EXPERT_DIM), BF16),
                pltpu.VMEM((EXPERT_DIM, D_MODEL), BF16),
                pltpu.SMEM((1,), I32),
                pltpu.SemaphoreType.DMA((N_WSLOTS, 3)),
            ],
        ),
        out_shape=jax.ShapeDtypeStruct((n_rows, LANES), U32),
        compiler_params=_cparams(("arbitrary",), vmem_mb=VMEM_MB_EXPERTS),
        name="experts",
    )(block_e, block_nv, block_first, block_next, x_sorted, w1, w3, w2)


TM_FN = 256


CH_FN = 8


def _final_kernel(alpha, dest_ref, ys_ref, w_ref, h_ref, sh_ref, g_ref, b_ref, o_ref, buf, sems):
    i = pl.program_id(0)
    n_tiles = pl.num_programs(0)
    t_total = n_tiles * TM_FN
    slot = i % 2
    half = D_MODEL // 2

    def issue_token(tile, sl, c, tt):
        tok = c * CH_FN + tt
        row0 = pl.multiple_of(tok * ROW_TILE, ROW_TILE)
        for k in range(TOP_K):
            src = ys_ref.at[dest_ref[k * t_total + tile * TM_FN + tok]]
            pltpu.make_async_copy(src, buf.at[sl, k, pl.ds(row0, ROW_TILE), :], sems.at[sl]).start(priority=k % 2)

    def wait_tile(sl):
        for k in range(TOP_K):
            pltpu.make_async_copy(ys_ref.at[pl.ds(0, TM_FN)], buf.at[sl, k].reshape(TM_FN, ROW_TILE, LANES),
                                  sems.at[sl]).wait()

    @pl.when(i == 0)
    def _():
        def first(c, carry):
            for tt in range(CH_FN):
                issue_token(0, 0, c, tt)
            return carry
        lax.fori_loop(0, TM_FN // CH_FN, first, 0)

    nxt_tile = jnp.minimum(i + 1, n_tiles - 1)

    def combine_from(cur, nxt):
        wait_tile(cur)

        def body(c, carry):
            tok = pl.ds(pl.multiple_of(c * CH_FN, CH_FN), CH_FN)
            acc_lo = sh_ref[tok, :half]
            acc_hi = sh_ref[tok, half:]
            pre = alpha * h_ref[tok, :]
            wrow = w_ref[tok, :]
            for k in range(TOP_K):
                words = jnp.concatenate(
                    [buf[cur, k, pl.ds(c * (CH_FN * ROW_TILE) + r, CH_FN, stride=ROW_TILE), :]
                     for r in range(ROW_TILE)], axis=1)
                issue_token(nxt_tile, nxt, c, k)
                lo, hi = _unpack_halves(words)
                wk = wrow[:, k:k + 1]
                acc_lo = acc_lo + wk * lo
                acc_hi = acc_hi + wk * hi
            o_ref[tok, :] = pre + jnp.concatenate([acc_lo, acc_hi], axis=1)
            return carry

        lax.fori_loop(0, TM_FN // CH_FN, body, 0)

        @pl.when(i == n_tiles - 1)
        def _():
            wait_tile(nxt)

    for parity in range(2):
        @pl.when(slot == parity)
        def _():
            combine_from(parity, 1 - parity)

    o_ref[...] = _layer_norm(o_ref[...], g_ref[...], b_ref[...])


def _final(dest_flat, ys, w_tok, h1, shared, g, b, alpha):
    t = h1.shape[0]
    row = lambda i, d: (i, 0)
    const = lambda i, d: (0, 0)
    return pl.pallas_call(
        functools.partial(_final_kernel, alpha),
        grid_spec=pltpu.PrefetchScalarGridSpec(
            num_scalar_prefetch=1,
            grid=(t // TM_FN,),
            in_specs=[
                pl.BlockSpec(memory_space=pl.ANY),
                pl.BlockSpec((TM_FN, TOP_K), row),
                pl.BlockSpec((TM_FN, D_MODEL), row),
                pl.BlockSpec((TM_FN, D_MODEL), row),
                pl.BlockSpec((1, D_MODEL), const),
                pl.BlockSpec((1, D_MODEL), const),
            ],
            out_specs=pl.BlockSpec((TM_FN, D_MODEL), row),
            scratch_shapes=[pltpu.VMEM((2, TOP_K, TM_FN * ROW_TILE, LANES), U32), pltpu.SemaphoreType.DMA((2,))],
        ),
        out_shape=jax.ShapeDtypeStruct((t, D_MODEL), F32),
        compiler_params=_cparams(("arbitrary",)),
        name="combine_ln2",
    )(dest_flat, ys, w_tok, h1, shared, g, b)


TM_INB = 512
TN_INB = W_B // 2


def _pad_lanes(v):
    return jnp.pad(v, (0, LANES - v.shape[0]))[None, :].astype(F32)


def _layer(h2d, mem2d, batch, seq, w_in_all, layer, conv_w, conv_b, dt_bias, a_log, d_skip, ssd_norm_g, swa_sinks,
           rel_bias, w_mem_kv, w_ssd_o, w_swa_o, w_xa_o, w_out, ln1_g, ln1_b, router_w, router_bias, w1, w3, w2, ws1,
           ws3, ws2, ln2_g, ln2_b, alpha):
    t = batch * seq
    w_t = jnp.swapaxes(w_in_all, 1, 2)
    wdt = jnp.pad(w_t[layer, W_A:W_A + SSD_HEADS, :].T, ((0, 0), (0, LANES - SSD_HEADS))).astype(BF16)
    pa, dt_raw, xb = _inproj(h2d, _wcast(w_t, layer, 0, W_A, "wcast_a"), wdt)
    pb = _mm(xb, _wcast(w_t, layer, W_A + SSD_HEADS, W_B, "wcast_b"), BF16, TM_INB, TN_INB, "inproj_b",
             vmem_mb=VMEM_MB_LARGE)

    y_ssd = _ssd(
        pa, dt_raw,
        conv_w[:, :SSD_D_INNER], conv_b[None, :SSD_D_INNER], conv_w[:, SSD_D_INNER:], conv_b[None, SSD_D_INNER:],
        _pad_lanes(dt_bias), _pad_lanes(a_log),
        jnp.repeat(d_skip, SSD_HEAD_DIM)[None, :], ssd_norm_g[None, :], batch, seq)

    y_swa = _swa(pb, swa_sinks.astype(F32), _swa_bias_table(rel_bias), batch, seq)

    memkv = _mm(mem2d, w_mem_kv, BF16, batch * MEM_LEN, 1024, "mem_kv")
    y_xa = _xa(pb, memkv, batch, seq)

    merged = _merge(y_ssd, y_swa, y_xa, w_ssd_o, w_swa_o, w_xa_o, pb)
    h1, hp = _out_ln(merged, w_out.astype(BF16), h2d, ln1_g[None, :], ln1_b[None, :], alpha)

    rwt = router_w.T
    r_hi = rwt.astype(BF16)
    r_lo = (rwt - r_hi.astype(F32)).astype(BF16)
    ti = jnp.arange(TM_RT)
    su = (ti[:, None] < ti[None, :]).astype(BF16)
    idx_t, w_t, rank_t, cnt = _router(h1, r_hi, r_lo, router_bias[:, None].astype(F32), su)

    counts = cnt[:, 0].astype(I32)
    padded = (counts + MOE_BLK - 1) // MOE_BLK * MOE_BLK
    pend = jnp.cumsum(padded)
    pstart = pend - padded
    n_blocks = t * TOP_K // MOE_BLK + N_EXPERTS
    blk_lo = jnp.arange(n_blocks, dtype=I32) * MOE_BLK
    block_e = jnp.minimum(jnp.sum((pend[None, :] <= blk_lo[:, None]).astype(I32), axis=1), N_EXPERTS - 1)
    of_e = block_e[:, None] == jnp.arange(N_EXPERTS, dtype=I32)[None, :]
    valid_end = jnp.sum(jnp.where(of_e, (pstart + counts)[None, :], 0), axis=1)
    block_nv = jnp.clip(valid_end - blk_lo, 0, MOE_BLK).astype(I32)
    eids = jnp.arange(N_EXPERTS, dtype=I32)
    later_active = (eids[None, :] > eids[:, None]) & (padded > 0)[None, :]
    next_active = jnp.min(jnp.where(later_active, eids[None, :], N_EXPERTS), axis=1)
    next_active = jnp.where(next_active >= N_EXPERTS, -1, next_active)
    seg_start = jnp.sum(jnp.where(of_e, pstart[None, :], 0), axis=1)
    block_first = ((blk_lo == seg_start) & (block_nv > 0)).astype(I32)
    block_next = jnp.sum(jnp.where(of_e, next_active[None, :], 0), axis=1).astype(I32)

    dest_flat = _dest(idx_t, rank_t, pstart.astype(F32)[:, None]).reshape(-1)
    n_slots = n_blocks * MOE_BLK
    pad_info = jnp.concatenate([pstart + counts, padded - counts, pend[-1:],
                                (n_slots - pend[-1:]) // MOE_BLK]).astype(I32)
    x_sorted, shared = _dispatch(dest_flat, pad_info, hp, h1, ws1.astype(BF16), ws3.astype(BF16),
                                 ws2.astype(BF16), n_slots)
    ys = _experts(block_e, block_nv, block_first, block_next,
                  x_sorted.reshape(n_slots * ROW_TILE, LANES), w1, w3, w2)
    return _final(dest_flat, ys.reshape(n_slots, ROW_TILE, LANES), w_t.T, h1, shared, ln2_g[None, :],
                  ln2_b[None, :], alpha)


def kernel(x, mem, w_in, conv_w, conv_b, dt_bias, a_log, d_skip, ssd_norm_g, swa_sinks, rel_bias, w_mem_kv, w_ssd_o, w_swa_o, w_xa_o, w_out, ln1_g, ln1_b, router_w, router_bias, w1, w3, w2, ws1, ws3, ws2, ln2_g, ln2_b):
    batch, seq, d = x.shape
    depth = w_in.shape[0]
    alpha = (2.0 * depth) ** 0.25
    h = x.reshape(batch * seq, d)
    mem2d = mem.reshape(batch * MEM_LEN, d)
    for i in range(depth):
        h = _layer(h, mem2d, batch, seq, w_in, i, conv_w[i], conv_b[i], dt_bias[i], a_log[i], d_skip[i],
                   ssd_norm_g[i], swa_sinks[i], rel_bias, w_mem_kv[i], w_ssd_o[i], w_swa_o[i], w_xa_o[i], w_out[i],
                   ln1_g[i], ln1_b[i], router_w[i], router_bias[i], w1[i], w3[i], w2[i], ws1[i], ws3[i], ws2[i],
                   ln2_g[i], ln2_b[i], alpha)
    return h.reshape(batch, seq, d)
```

```python
import functools
import math

import jax
import jax.numpy as jnp
from jax import lax
from jax.experimental import pallas as pl
from jax.experimental.pallas import tpu as pltpu

F32 = jnp.float32
BF16 = jnp.bfloat16
I32 = jnp.int32
U32 = jnp.uint32

D_MODEL = 2048
MEM_LEN = 256
SSD_D_INNER = D_MODEL
SSD_HEAD_DIM = 64
SSD_HEADS = 32
SSD_GROUPS = 4
SSD_STATE = 128
SSD_CONV = 4
SSD_CHUNK = 128
SWA_HEADS = 16
SWA_KV_HEADS = 4
SWA_HEAD_DIM = 64
SWA_WINDOW = 128
SWA_BLOCK = 128
REL_BUCKETS = 32
REL_MAX_DIST = 128
XA_HEADS = 4
XA_HEAD_DIM = 256
N_BRANCH = 3
N_EXPERTS = 64
TOP_K = 8
N_EXPERT_GROUPS = 8
TOPK_GROUPS = 4
EXPERT_DIM = 512
SHARED_DIM = 512
ROUTED_SCALE = 2.5
LN_EPS = 1e-5
RMS_EPS = 1e-5

SWA_Q_DIM = SWA_HEADS * SWA_HEAD_DIM
SWA_KV_DIM = SWA_KV_HEADS * SWA_HEAD_DIM
XA_DIM = XA_HEADS * XA_HEAD_DIM
BC_DIM = 2 * SSD_GROUPS * SSD_STATE

W_A = SSD_D_INNER + SSD_D_INNER + BC_DIM
OFFA_Z = 0
OFFA_XS = OFFA_Z + SSD_D_INNER
OFFA_BC = OFFA_XS + SSD_D_INNER
W_B = SWA_Q_DIM + 2 * SWA_KV_DIM + XA_DIM + N_BRANCH * D_MODEL
OFFB_Q = 0
OFFB_K = OFFB_Q + SWA_Q_DIM
OFFB_V = OFFB_K + SWA_KV_DIM
OFFB_QX = OFFB_V + SWA_KV_DIM
OFFB_GATES = OFFB_QX + XA_DIM

LANES = 128
MOE_BLK = 256
NEG_INF = float("-inf")


VMEM_MB = 48
VMEM_MB_LARGE = 56
VMEM_MB_EXPERTS = 58


def _cparams(sem, vmem_mb=VMEM_MB):
    return pltpu.CompilerParams(dimension_semantics=sem, vmem_limit_bytes=vmem_mb * 1024 * 1024)


def _dot(a, b):
    return jnp.dot(a, b, preferred_element_type=F32)


def _dot_nt(a, b):
    return lax.dot_general(a, b, (((1,), (1,)), ((), ())), preferred_element_type=F32)


def _sigmoid(x):
    return 1.0 / (1.0 + jnp.exp(-x))


def _silu(x):
    return x * _sigmoid(x)


TM_IN = 512
TN_IN = W_A // 2


def _inproj_kernel(x_ref, w_ref, wdt_ref, p_ref, dt_ref, xb_ref):
    xb = x_ref[...].astype(BF16)

    @pl.when(pl.program_id(0) == 0)
    def _():
        xb_ref[...] = xb
        dt_ref[...] = _dot(xb, wdt_ref[...])

    p_ref[...] = _dot(xb, w_ref[...]).astype(BF16)


def _inproj(x2d, w_in, wdt):
    t = x2d.shape[0]
    n_i = t // TM_IN
    once = lambda j, i: (jnp.where(j == 0, i, n_i - 1), 0)
    return pl.pallas_call(
        _inproj_kernel,
        grid=(W_A // TN_IN, n_i),
        in_specs=[
            pl.BlockSpec((TM_IN, D_MODEL), lambda j, i: (i, 0)),
            pl.BlockSpec((D_MODEL, TN_IN), lambda j, i: (0, j)),
            pl.BlockSpec((D_MODEL, LANES), lambda j, i: (0, 0)),
        ],
        out_specs=[
            pl.BlockSpec((TM_IN, TN_IN), lambda j, i: (i, j)),
            pl.BlockSpec((TM_IN, LANES), once),
            pl.BlockSpec((TM_IN, D_MODEL), once),
        ],
        out_shape=[jax.ShapeDtypeStruct((t, W_A), BF16), jax.ShapeDtypeStruct((t, LANES), F32),
                   jax.ShapeDtypeStruct((t, D_MODEL), BF16)],
        compiler_params=_cparams(("arbitrary", "arbitrary"), vmem_mb=VMEM_MB_LARGE),
        name="inproj",
    )(x2d, w_in, wdt)


TN_WC = 512
ROW_ALIGN = 32


def _wcast_kernel(a_ref, o_ref):
    o_ref[...] = a_ref[0].T.astype(BF16)


def _wcast(w_t, layer, start, width, name):
    k = w_t.shape[2]
    return pl.pallas_call(
        _wcast_kernel,
        grid=(width // TN_WC,),
        in_specs=[pl.BlockSpec((pl.Element(1), pl.Element(TN_WC), pl.Element(k)),
                               lambda j: (layer, pl.multiple_of(start + j * TN_WC, ROW_ALIGN), 0))],
        out_specs=pl.BlockSpec((k, TN_WC), lambda j: (0, j)),
        out_shape=jax.ShapeDtypeStruct((k, width), BF16),
        compiler_params=_cparams(("parallel",)),
        name=name,
    )(w_t)


def _mm_kernel(a_ref, b_ref, o_ref):
    o_ref[...] = _dot(a_ref[...].astype(BF16), b_ref[...].astype(BF16)).astype(o_ref.dtype)


def _mm(a, b, out_dtype, tm, tn, name, vmem_mb=VMEM_MB):
    m, k = a.shape
    n = b.shape[1]
    return pl.pallas_call(
        _mm_kernel,
        grid=(m // tm, n // tn),
        in_specs=[pl.BlockSpec((tm, k), lambda i, j: (i, 0)), pl.BlockSpec((k, tn), lambda i, j: (0, j))],
        out_specs=pl.BlockSpec((tm, tn), lambda i, j: (i, j)),
        out_shape=jax.ShapeDtypeStruct((m, n), out_dtype),
        compiler_params=_cparams(("parallel", "parallel"), vmem_mb=vmem_mb),
        name=name,
    )(a, b)


Q = SSD_CHUNK


def _conv_shift_matrix():
    t = jnp.arange(Q)[:, None]
    j = jnp.arange(2 * Q)[None, :]
    blocks = []
    for k in range(SSD_CONV - 1):
        src = t - (SSD_CONV - 1 - k)
        blocks.append(j == jnp.where(src >= 0, src, 2 * Q + src))
    return jnp.concatenate(blocks, axis=0).astype(BF16)


def _ssd_kernel(xs_ref, bc_ref, z_ref, dtr_ref, shift_ref, cwx_ref, cbx_ref, cwb_ref, cbb_ref, dtb_ref, alog_ref,
                dsk_ref, ng_ref, o_ref, hx_ref, hb_ref, st_ref, cumt_ref, dtt_ref, wt_ref, y_ref):
    c = pl.program_id(1)

    @pl.when(c == 0)
    def _():
        hx_ref[...] = jnp.zeros_like(hx_ref)
        hb_ref[...] = jnp.zeros_like(hb_ref)
        st_ref[...] = jnp.zeros_like(st_ref)

    def conv_silu(in_ref, prev_ref, w_ref, b_ref):
        cur = in_ref[...]
        ext = jnp.concatenate([cur, prev_ref[...]], axis=0)
        shifted = _dot(shift_ref[...], ext)
        acc = b_ref[...] + w_ref[SSD_CONV - 1:SSD_CONV, :] * cur.astype(F32)
        for k in range(SSD_CONV - 1):
            acc = acc + w_ref[k:k + 1, :] * shifted[k * Q:(k + 1) * Q, :]
        prev_ref[...] = cur
        return _silu(acc)

    xs_c = conv_silu(xs_ref, hx_ref, cwx_ref, cbx_ref)
    bc_c = conv_silu(bc_ref, hb_ref, cwb_ref, cbb_ref)

    lane = lax.broadcasted_iota(I32, (Q, LANES), 1)
    row = lax.broadcasted_iota(I32, (Q, LANES), 0)
    tril = row >= lane
    lo_mask = lane < SSD_HEAD_DIM

    dtv = dtr_ref[...] + dtb_ref[...]
    dt = jnp.maximum(dtv, 0.0) + jnp.log1p(jnp.exp(-jnp.abs(dtv)))
    a = jnp.where(lane[0:1, :] < SSD_HEADS, -jnp.exp(alog_ref[...]), 0.0)
    la = dt * a
    tri = jnp.where(tril, 1.0, 0.0).astype(BF16)
    p1 = la.astype(BF16)
    r1 = la - p1.astype(F32)
    p2 = r1.astype(BF16)
    p3 = (r1 - p2.astype(F32)).astype(BF16)
    cum = _dot(tri, p1) + _dot(tri, p2) + _dot(tri, p3)
    ecum = jnp.exp(cum)
    cum_t = cum.T
    dt_t = dt.T
    last = cum_t[:, Q - 1:Q]
    cumt_ref[...] = cum_t
    dtt_ref[...] = dt_t
    wt_ref[...] = jnp.exp(last - cum_t) * dt_t
    chunk_decay = jnp.exp(last)

    def head_parts(h, cb, cm, bm_t):
        col = jnp.broadcast_to(cum[:, h:h + 1], (Q, Q))
        rw = cumt_ref[h:h + 1, :]
        dec = jnp.exp(jnp.where(tril, col - rw, NEG_INF))
        m = cb * (dec * dtt_ref[h:h + 1, :])
        ce = cm * jnp.broadcast_to(ecum[:, h:h + 1], (Q, Q))
        sl = bm_t * wt_ref[h:h + 1, :]
        return m.astype(BF16), ce.astype(BF16), sl.astype(BF16)

    for g in range(SSD_GROUPS):
        bm = bc_c[:, g * SSD_STATE:(g + 1) * SSD_STATE]
        cm = bc_c[:, BC_DIM // 2 + g * SSD_STATE:BC_DIM // 2 + (g + 1) * SSD_STATE]
        cb = _dot_nt(cm.astype(BF16), bm.astype(BF16))
        bm_t = bm.T
        for qd in range(SSD_HEADS // SSD_GROUPS // 2):
            hp = g * (SSD_HEADS // SSD_GROUPS // 2) + qd
            ha, hb = 2 * hp, 2 * hp + 1
            sl_ = slice(hp * LANES, (hp + 1) * LANES)
            xs_pair = xs_c[:, sl_]
            x_lo = jnp.where(lo_mask, xs_pair, 0.0).astype(BF16)
            x_hi = jnp.where(lo_mask, 0.0, xs_pair).astype(BF16)
            h_prev = st_ref[:, sl_]
            h_lo = jnp.where(lo_mask, h_prev, 0.0).astype(BF16)
            h_hi = jnp.where(lo_mask, 0.0, h_prev).astype(BF16)
            m_a, ce_a, sl_a = head_parts(ha, cb, cm, bm_t)
            m_b, ce_b, sl_b = head_parts(hb, cb, cm, bm_t)
            lhs = jnp.concatenate([m_a, ce_a, m_b, ce_b], axis=1)
            rhs = jnp.concatenate([x_lo, h_lo, x_hi, h_hi], axis=0)
            y_ref[:, sl_] = _dot(lhs, rhs)
            new = _dot(jnp.concatenate([sl_a, sl_b], axis=1), jnp.concatenate([x_lo, x_hi], axis=0))
            cd = jnp.where(lo_mask[0:1, :], chunk_decay[ha:ha + 1, :], chunk_decay[hb:hb + 1, :])
            st_ref[:, sl_] = h_prev * cd + new

    y = y_ref[...] + xs_c * dsk_ref[...]
    y = y * _silu(z_ref[...].astype(F32))
    gw = SSD_D_INNER // SSD_GROUPS
    for g in range(SSD_GROUPS):
        yg = y[:, g * gw:(g + 1) * gw]
        ms = jnp.mean(yg * yg, axis=-1, keepdims=True)
        o_ref[:, g * gw:(g + 1) * gw] = ((yg * lax.rsqrt(ms + RMS_EPS)) * ng_ref[:, g * gw:(g + 1) * gw]).astype(BF16)


def _ssd(p, dt_raw, cwx, cbx, cwb, cbb, dtb, alog, dsk, ng, batch, seq):
    nc = seq // Q
    t = batch * seq
    rowblk = lambda b, c: b * nc + c
    const = lambda b, c: (0, 0)
    return pl.pallas_call(
        _ssd_kernel,
        grid=(batch, nc),
        in_specs=[
            pl.BlockSpec((Q, SSD_D_INNER), lambda b, c: (rowblk(b, c), OFFA_XS // SSD_D_INNER)),
            pl.BlockSpec((Q, BC_DIM), lambda b, c: (rowblk(b, c), OFFA_BC // BC_DIM)),
            pl.BlockSpec((Q, SSD_D_INNER), lambda b, c: (rowblk(b, c), OFFA_Z // SSD_D_INNER)),
            pl.BlockSpec((Q, LANES), lambda b, c: (rowblk(b, c), 0)),
            pl.BlockSpec(((SSD_CONV - 1) * Q, 2 * Q), const),
            pl.BlockSpec((SSD_CONV, SSD_D_INNER), const),
            pl.BlockSpec((1, SSD_D_INNER), const),
            pl.BlockSpec((SSD_CONV, BC_DIM), const),
            pl.BlockSpec((1, BC_DIM), const),
            pl.BlockSpec((1, LANES), const),
            pl.BlockSpec((1, LANES), const),
            pl.BlockSpec((1, SSD_D_INNER), const),
            pl.BlockSpec((1, SSD_D_INNER), const),
        ],
        out_specs=pl.BlockSpec((Q, SSD_D_INNER), lambda b, c: (rowblk(b, c), 0)),
        out_shape=jax.ShapeDtypeStruct((t, SSD_D_INNER), BF16),
        scratch_shapes=[
            pltpu.VMEM((Q, SSD_D_INNER), BF16),
            pltpu.VMEM((Q, BC_DIM), BF16),
            pltpu.VMEM((SSD_STATE, SSD_D_INNER), F32),
            pltpu.VMEM((LANES, Q), F32),
            pltpu.VMEM((LANES, Q), F32),
            pltpu.VMEM((LANES, Q), F32),
            pltpu.VMEM((Q, SSD_D_INNER), F32),
        ],
        compiler_params=_cparams(("parallel", "arbitrary")),
        name="ssd",
    )(p, p, p, dt_raw, _conv_shift_matrix(), cwx, cbx, cwb, cbb, dtb, alog, dsk, ng)


N_PAIRS = SWA_HEADS // 2


def _swa_kernel(sink_ref, q_ref, kp_ref, kc_ref, vp_ref, vc_ref, bias_ref, o_ref):
    blk = SWA_BLOCK
    kband = jnp.concatenate([kp_ref[...], kc_ref[...]], axis=0)
    vband = jnp.concatenate([vp_ref[...], vc_ref[...]], axis=0)

    lane = lax.broadcasted_iota(I32, (1, LANES), 1)
    lo_f = jnp.where(lane < SWA_HEAD_DIM, 1.0, 0.0)
    lo_m = lo_f.astype(BF16)
    hi_m = (1.0 - lo_f).astype(BF16)

    def head_twice(band, kv):
        tile = band[:, (kv // 2) * LANES:(kv // 2 + 1) * LANES]
        swapped = jnp.concatenate([tile[:, SWA_HEAD_DIM:], tile[:, :SWA_HEAD_DIM]], axis=1)
        return tile * lo_m + swapped * hi_m if kv % 2 == 0 else swapped * lo_m + tile * hi_m

    k_twice = [head_twice(kband, kv) for kv in range(SWA_KV_HEADS)]
    v_twice = [head_twice(vband, kv) for kv in range(SWA_KV_HEADS)]
    rowi = lax.broadcasted_iota(I32, (2 * blk, 1), 0)
    scale = SWA_HEAD_DIM ** -0.5
    assert math.frexp(scale)[0] == 0.5
    lo_q = (lo_f * scale).astype(BF16)
    hi_q = ((1.0 - lo_f) * scale).astype(BF16)
    for p in range(N_PAIRS):
        kv = p // (SWA_HEADS // SWA_KV_HEADS // 2)
        qp = q_ref[:, p * LANES:(p + 1) * LANES]
        qs = jnp.concatenate([qp * lo_q, qp * hi_q], axis=0)
        s = _dot_nt(qs, k_twice[kv]) + bias_ref[0, p]
        sink = jnp.where(rowi < blk, sink_ref[2 * p], sink_ref[2 * p + 1])
        m = jnp.maximum(jnp.max(s, axis=-1, keepdims=True), sink)
        e = jnp.exp(s - m)
        den = jnp.sum(e, axis=-1, keepdims=True) + jnp.exp(sink - m)
        pr = (e * (1.0 / den)).astype(BF16)
        vd = v_twice[kv]
        lhs = jnp.concatenate([pr[:blk], pr[blk:]], axis=1)
        rhs = jnp.concatenate([vd * lo_m, vd * hi_m], axis=0)
        o_ref[:, p * LANES:(p + 1) * LANES] = _dot(lhs, rhs).astype(BF16)


def _swa(p, sinks, bias, batch, seq):
    nb = seq // SWA_BLOCK
    t = batch * seq
    kw = SWA_KV_DIM
    cur = lambda b, n, s: b * nb + n
    prev = lambda b, n, s: b * nb + jnp.maximum(n - 1, 0)
    return pl.pallas_call(
        _swa_kernel,
        grid_spec=pltpu.PrefetchScalarGridSpec(
            num_scalar_prefetch=1,
            grid=(batch, nb),
            in_specs=[
                pl.BlockSpec((SWA_BLOCK, SWA_Q_DIM), lambda b, n, s: (cur(b, n, s), OFFB_Q // SWA_Q_DIM)),
                pl.BlockSpec((SWA_BLOCK, kw), lambda b, n, s: (prev(b, n, s), OFFB_K // kw)),
                pl.BlockSpec((SWA_BLOCK, kw), lambda b, n, s: (cur(b, n, s), OFFB_K // kw)),
                pl.BlockSpec((SWA_BLOCK, kw), lambda b, n, s: (prev(b, n, s), OFFB_V // kw)),
                pl.BlockSpec((SWA_BLOCK, kw), lambda b, n, s: (cur(b, n, s), OFFB_V // kw)),
                pl.BlockSpec((1, N_PAIRS, 2 * SWA_BLOCK, 2 * SWA_BLOCK),
                             lambda b, n, s: (jnp.minimum(n, 1), 0, 0, 0)),
            ],
            out_specs=pl.BlockSpec((SWA_BLOCK, SWA_Q_DIM), lambda b, n, s: (cur(b, n, s), 0)),
        ),
        out_shape=jax.ShapeDtypeStruct((t, SWA_Q_DIM), BF16),
        compiler_params=_cparams(("parallel", "arbitrary")),
        name="swa",
    )(sinks, p, p, p, p, p, bias)


def _t5_causal_bucket(dist):
    max_exact = REL_BUCKETS // 2
    large = max_exact + (jnp.log(jnp.maximum(dist, 1).astype(F32) / max_exact)
                         / math.log(REL_MAX_DIST / max_exact) * (REL_BUCKETS - max_exact)).astype(I32)
    large = jnp.minimum(large, REL_BUCKETS - 1)
    return jnp.where(dist < max_exact, dist, large)


def _swa_bias_table(rel_bias):
    qi = jnp.arange(SWA_BLOCK)[:, None]
    kj = jnp.arange(2 * SWA_BLOCK)[None, :]
    dist = qi + SWA_BLOCK - kj
    in_window = (dist >= 0) & (dist < SWA_WINDOW)
    bucket = _t5_causal_bucket(jnp.maximum(dist, 0))
    onehot = (bucket[:, :, None] == jnp.arange(REL_BUCKETS)[None, None, :]).astype(F32)
    bias = jnp.einsum('ijb,bh->hij', onehot, rel_bias.astype(F32), precision=lax.Precision.HIGHEST)
    rest = jnp.where(in_window[None, :, :], bias, NEG_INF)
    first = jnp.where((kj >= SWA_BLOCK)[None, :, :], rest, NEG_INF)
    return jnp.stack([first, rest]).reshape(2, N_PAIRS, 2 * SWA_BLOCK, 2 * SWA_BLOCK)


TL_XA = 512


XA_HALF = XA_DIM // 2


def _xa_kernel(qa_ref, qb_ref, mk_ref, mv_ref, o_ref):
    for h in range(XA_HEADS):
        q_ref = qa_ref if h < XA_HEADS // 2 else qb_ref
        qc = (h % (XA_HEADS // 2)) * XA_HEAD_DIM
        cols = slice(h * XA_HEAD_DIM, (h + 1) * XA_HEAD_DIM)
        s = _dot_nt(q_ref[:, qc:qc + XA_HEAD_DIM], mk_ref[:, cols]) * (XA_HEAD_DIM ** -0.5)
        m = jnp.max(s, axis=-1, keepdims=True)
        e = jnp.exp(s - m)
        pr = e * (1.0 / jnp.sum(e, axis=-1, keepdims=True))
        o_ref[:, cols] = _dot(pr.astype(BF16), mv_ref[:, cols]).astype(BF16)


def _xa(p, memkv, batch, seq):
    nl = seq // TL_XA
    t = batch * seq
    return pl.pallas_call(
        _xa_kernel,
        grid=(batch, nl),
        in_specs=[
            pl.BlockSpec((TL_XA, XA_HALF), lambda b, l: (b * nl + l, OFFB_QX // XA_HALF)),
            pl.BlockSpec((TL_XA, XA_HALF), lambda b, l: (b * nl + l, OFFB_QX // XA_HALF + 1)),
            pl.BlockSpec((MEM_LEN, XA_DIM), lambda b, l: (b, 0)),
            pl.BlockSpec((MEM_LEN, XA_DIM), lambda b, l: (b, 1)),
        ],
        out_specs=pl.BlockSpec((TL_XA, XA_DIM), lambda b, l: (b * nl + l, 0)),
        out_shape=jax.ShapeDtypeStruct((t, XA_DIM), BF16),
        compiler_params=_cparams(("parallel", "parallel")),
        name="xattn",
    )(p, p, memkv, memkv)


TM_MG = 1024
TN_MG = 512


N_SUB_MG = 2


def _merge_kernel(ys_ref, yw_ref, yx_ref, ws_ref, ww_ref, wx_ref, g0_ref, g1_ref, g2_ref, o_ref):
    sub = TM_MG // N_SUB_MG
    ws = ws_ref[...].astype(BF16)
    ww = ww_ref[...].astype(BF16)
    wx = wx_ref[...].astype(BF16)
    for s in range(N_SUB_MG):
        rows = slice(s * sub, (s + 1) * sub)
        a = _dot(ys_ref[rows, :], ws)
        b = _dot(yw_ref[rows, :], ww)
        c = _dot(yx_ref[rows, :], wx)
        g0 = _sigmoid(g0_ref[rows, :].astype(F32))
        g1 = _sigmoid(g1_ref[rows, :].astype(F32))
        g2 = _sigmoid(g2_ref[rows, :].astype(F32))
        o_ref[rows, :] = (g0 * a + g1 * b + g2 * c).astype(BF16)


def _merge(y_ssd, y_swa, y_xa, w_ssd_o, w_swa_o, w_xa_o, p):
    t = y_ssd.shape[0]
    gpb = D_MODEL // TN_MG
    gate = lambda k: pl.BlockSpec((TM_MG, TN_MG), lambda i, j: (i, OFFB_GATES // TN_MG + k * gpb + j))
    return pl.pallas_call(
        _merge_kernel,
        grid=(t // TM_MG, D_MODEL // TN_MG),
        in_specs=[
            pl.BlockSpec((TM_MG, SSD_D_INNER), lambda i, j: (i, 0)),
            pl.BlockSpec((TM_MG, SWA_Q_DIM), lambda i, j: (i, 0)),
            pl.BlockSpec((TM_MG, XA_DIM), lambda i, j: (i, 0)),
            pl.BlockSpec((SSD_D_INNER, TN_MG), lambda i, j: (0, j)),
            pl.BlockSpec((SWA_Q_DIM, TN_MG), lambda i, j: (0, j)),
            pl.BlockSpec((XA_DIM, TN_MG), lambda i, j: (0, j)),
            gate(0), gate(1), gate(2),
        ],
        out_specs=pl.BlockSpec((TM_MG, TN_MG), lambda i, j: (i, j)),
        out_shape=jax.ShapeDtypeStruct((t, D_MODEL), BF16),
        compiler_params=_cparams(("parallel", "parallel")),
        name="merge",
    )(y_ssd, y_swa, y_xa, w_ssd_o, w_swa_o, w_xa_o, p, p, p)


TM_LN = 512


def _layer_norm(v, g, b):
    mu = jnp.mean(v, axis=-1, keepdims=True)
    d = v - mu
    var = jnp.mean(d * d, axis=-1, keepdims=True)
    return (d * lax.rsqrt(var + LN_EPS)) * g + b


def _pack_halves(v):
    n = v.shape[1] // 2
    lo = lax.bitcast_convert_type(v[:, :n].astype(BF16).astype(F32), U32) >> 16
    hi = lax.bitcast_convert_type(v[:, n:].astype(BF16).astype(F32), U32) & jnp.uint32(0xFFFF0000)
    return hi | lo


def _unpack_halves(w):
    lo = lax.bitcast_convert_type(w << 16, F32)
    hi = lax.bitcast_convert_type(w & jnp.uint32(0xFFFF0000), F32)
    return lo, hi


ROW_WORDS = D_MODEL // 2
ROW_TILE = ROW_WORDS // LANES


def _store_row_tiled(ref, words):
    m = words.shape[0]
    for c in range(ROW_TILE):
        ref[pl.ds(c, m, stride=ROW_TILE), :] = words[:, c * LANES:(c + 1) * LANES]


def _load_row_tiled(ref, m):
    return jnp.concatenate([ref[pl.ds(c, m, stride=ROW_TILE), :] for c in range(ROW_TILE)], axis=1)


N_SUB_LN = 4


def _out_ln_kernel(alpha, m_ref, w_ref, x_ref, g_ref, b_ref, h_ref, hp_ref):
    sub = TM_LN // N_SUB_LN
    for c in range(N_SUB_LN):
        rows = slice(c * sub, (c + 1) * sub)
        mix = _dot(m_ref[rows, :], w_ref[...])
        hn = _layer_norm(alpha * x_ref[rows, :] + mix, g_ref[...], b_ref[...])
        h_ref[rows, :] = hn
        _store_row_tiled(hp_ref.at[pl.ds(c * sub * ROW_TILE, sub * ROW_TILE), :], _pack_halves(hn))


def _out_ln(merged, w_out, x2d, g, b, alpha):
    t = x2d.shape[0]
    row = lambda i: (i, 0)
    const = lambda i: (0, 0)
    return pl.pallas_call(
        functools.partial(_out_ln_kernel, alpha),
        grid=(t // TM_LN,),
        in_specs=[
            pl.BlockSpec((TM_LN, D_MODEL), row),
            pl.BlockSpec((D_MODEL, D_MODEL), const),
            pl.BlockSpec((TM_LN, D_MODEL), row),
            pl.BlockSpec((1, D_MODEL), const),
            pl.BlockSpec((1, D_MODEL), const),
        ],
        out_specs=[pl.BlockSpec((TM_LN, D_MODEL), row), pl.BlockSpec((TM_LN * ROW_TILE, LANES), row)],
        out_shape=[jax.ShapeDtypeStruct((t, D_MODEL), F32), jax.ShapeDtypeStruct((t * ROW_TILE, LANES), U32)],
        compiler_params=_cparams(("parallel",)),
        name="out_ln1",
    )(merged, w_out, x2d, g, b)


TM_RT = 512
EPG = N_EXPERTS // N_EXPERT_GROUPS


def _router_kernel(h_ref, whi_ref, wlo_ref, bias_ref, su_ref, idx_ref, w_ref, rank_ref, cnt_ref, run_ref):
    @pl.when(pl.program_id(0) == 0)
    def _():
        run_ref[...] = jnp.zeros_like(run_ref)

    h = h_ref[...]
    h_hi = h.astype(BF16)
    h_lo = (h - h_hi.astype(F32)).astype(BF16)
    whi = whi_ref[...]
    logits = _dot_nt(whi, h_hi) + _dot_nt(whi, h_lo) + _dot_nt(wlo_ref[...], h_hi)
    scores = _sigmoid(logits)
    sel = scores + bias_ref[...]

    iota_m = lax.broadcasted_iota(I32, (EPG, TM_RT), 0).astype(F32)
    iota_g = lax.broadcasted_iota(I32, (N_EXPERT_GROUPS, TM_RT), 0).astype(F32)
    gs = []
    for g in range(N_EXPERT_GROUPS):
        v = sel[g * EPG:(g + 1) * EPG, :]
        m1 = jnp.max(v, axis=0, keepdims=True)
        i1 = jnp.min(jnp.where(v == m1, iota_m, float(EPG)), axis=0, keepdims=True)
        m2 = jnp.max(jnp.where(iota_m == i1, NEG_INF, v), axis=0, keepdims=True)
        gs.append(m1 + m2)
    gsc = jnp.concatenate(gs, axis=0)
    chosen = jnp.zeros((N_EXPERT_GROUPS, TM_RT), F32)
    for _ in range(TOPK_GROUPS):
        m = jnp.max(gsc, axis=0, keepdims=True)
        ig = jnp.min(jnp.where(gsc == m, iota_g, float(N_EXPERT_GROUPS)), axis=0, keepdims=True)
        hit = iota_g == ig
        chosen = jnp.where(hit, 1.0, chosen)
        gsc = jnp.where(hit, NEG_INF, gsc)
    gmask = jnp.concatenate(
        [jnp.broadcast_to(chosen[g:g + 1, :], (EPG, TM_RT)) for g in range(N_EXPERT_GROUPS)], axis=0)
    masked = jnp.where(gmask > 0.0, sel, NEG_INF)

    iota_e = lax.broadcasted_iota(I32, (N_EXPERTS, TM_RT), 0).astype(F32)
    hits, idxs, ws = [], [], []
    for _ in range(TOP_K):
        m = jnp.max(masked, axis=0, keepdims=True)
        ie = jnp.min(jnp.where(masked == m, iota_e, float(N_EXPERTS)), axis=0, keepdims=True)
        hit = iota_e == ie
        hits.append(hit)
        idxs.append(ie)
        ws.append(jnp.sum(jnp.where(hit, scores, 0.0), axis=0, keepdims=True))
        masked = jnp.where(hit, NEG_INF, masked)
    wsum = ws[0]
    for k in range(1, TOP_K):
        wsum = wsum + ws[k]
    inv = 1.0 / wsum
    idx_ref[...] = jnp.concatenate(idxs, axis=0).astype(I32)
    w_ref[...] = jnp.concatenate([w * inv * ROUTED_SCALE for w in ws], axis=0)

    assign = jnp.zeros((N_EXPERTS, TM_RT), F32)
    for hit in hits:
        assign = jnp.where(hit, 1.0, assign)
    pos = _dot(assign.astype(BF16), su_ref[...]) + run_ref[...]
    rank_ref[...] = jnp.concatenate(
        [jnp.sum(jnp.where(hit, pos, 0.0), axis=0, keepdims=True) for hit in hits], axis=0).astype(I32)
    run = run_ref[...] + jnp.sum(assign, axis=1, keepdims=True)
    run_ref[...] = run
    cnt_ref[...] = run


def _router(h1, r_hi, r_lo, rbias, su):
    t = h1.shape[0]
    tile = lambda i: (0, i)
    const = lambda i: (0, 0)
    return pl.pallas_call(
        _router_kernel,
        grid=(t // TM_RT,),
        in_specs=[
            pl.BlockSpec((TM_RT, D_MODEL), lambda i: (i, 0)),
            pl.BlockSpec((N_EXPERTS, D_MODEL), const),
            pl.BlockSpec((N_EXPERTS, D_MODEL), const),
            pl.BlockSpec((N_EXPERTS, 1), const),
            pl.BlockSpec((TM_RT, TM_RT), const),
        ],
        out_specs=[
            pl.BlockSpec((TOP_K, TM_RT), tile),
            pl.BlockSpec((TOP_K, TM_RT), tile),
            pl.BlockSpec((TOP_K, TM_RT), tile),
            pl.BlockSpec((N_EXPERTS, TM_RT), const),
        ],
        out_shape=[
            jax.ShapeDtypeStruct((TOP_K, t), I32),
            jax.ShapeDtypeStruct((TOP_K, t), F32),
            jax.ShapeDtypeStruct((TOP_K, t), I32),
            jax.ShapeDtypeStruct((N_EXPERTS, TM_RT), F32),
        ],
        scratch_shapes=[pltpu.VMEM((N_EXPERTS, TM_RT), F32)],
        compiler_params=_cparams(("arbitrary",)),
        name="router",
    )(h1, r_hi, r_lo, rbias, su)


TM_DS = 2048


def _dest_kernel(idx_ref, rank_ref, ps_ref, d_ref):
    iota_e = lax.broadcasted_iota(I32, (N_EXPERTS, TM_DS), 0)
    ps = ps_ref[...]
    rows = []
    for k in range(TOP_K):
        hit = iota_e == idx_ref[k:k + 1, :]
        rows.append(jnp.sum(jnp.where(hit, ps, 0.0), axis=0, keepdims=True))
    d_ref[...] = jnp.concatenate(rows, axis=0).astype(I32) + rank_ref[...]


def _dest(idx_t, rank_t, pstart):
    t = idx_t.shape[1]
    tile = lambda i: (0, i)
    return pl.pallas_call(
        _dest_kernel,
        grid=(t // TM_DS,),
        in_specs=[pl.BlockSpec((TOP_K, TM_DS), tile), pl.BlockSpec((TOP_K, TM_DS), tile),
                  pl.BlockSpec((N_EXPERTS, 1), lambda i: (0, 0))],
        out_specs=pl.BlockSpec((TOP_K, TM_DS), tile),
        out_shape=jax.ShapeDtypeStruct((TOP_K, t), I32),
        compiler_params=_cparams(("parallel",)),
        name="dest",
    )(idx_t, rank_t, pstart)


TM_DP = 512


PAD_BITS = tuple(1 << i for i in reversed(range(MOE_BLK.bit_length() - 1)))


def _dispatch_kernel(dest_ref, pad_ref, hp_ref, h_ref, ws1_ref, ws3_ref, ws2_ref, xs_ref, sh_ref, zbuf, sem, zsem):
    t_total = pl.num_programs(0) * TM_DP
    base = pl.program_id(0) * TM_DP

    def pad_copies(act):
        def per_expert(e, carry):
            start = pad_ref[e]
            length = pad_ref[N_EXPERTS + e]
            for bit in PAD_BITS:
                @pl.when((length & bit) != 0)
                def _():
                    off = length - (length & (2 * bit - 1))
                    act(pltpu.make_async_copy(zbuf.at[pl.ds(0, bit)], xs_ref.at[pl.ds(start + off, bit)], zsem))
            return carry

        lax.fori_loop(0, N_EXPERTS, per_expert, 0)

        def per_block(i, carry):
            row = pad_ref[2 * N_EXPERTS] + i * MOE_BLK
            act(pltpu.make_async_copy(zbuf, xs_ref.at[pl.ds(row, MOE_BLK)], zsem))
            return carry

        lax.fori_loop(0, pad_ref[2 * N_EXPERTS + 1], per_block, 0)

    @pl.when(pl.program_id(0) == 0)
    def _():
        zbuf[...] = jnp.zeros_like(zbuf)
        pad_copies(lambda cp: cp.start())

    def issue(t, carry):
        src = hp_ref.at[pl.ds(pl.multiple_of(t * ROW_TILE, ROW_TILE), ROW_TILE), :]
        for k in range(TOP_K):
            slot = dest_ref[k * t_total + base + t]
            pltpu.make_async_copy(src, xs_ref.at[slot], sem).start(priority=k % 2)
        return carry

    lax.fori_loop(0, TM_DP, issue, 0)

    xb = h_ref[...].astype(BF16)
    hdn = _silu(_dot(xb, ws1_ref[...])) * _dot(xb, ws3_ref[...])
    sh_ref[...] = _dot(hdn.astype(BF16), ws2_ref[...])

    for k in range(TOP_K):
        pltpu.make_async_copy(hp_ref.reshape(TM_DP, ROW_TILE, LANES), xs_ref.at[pl.ds(0, TM_DP)], sem).wait()

    @pl.when(pl.program_id(0) == 0)
    def _():
        pad_copies(lambda cp: cp.wait())


def _dispatch(dest_flat, pad_info, hp, h1, ws1, ws3, ws2, n_slots):
    t = h1.shape[0]
    row = lambda i, *_: (i, 0)
    const = lambda i, *_: (0, 0)
    return pl.pallas_call(
        _dispatch_kernel,
        grid_spec=pltpu.PrefetchScalarGridSpec(
            num_scalar_prefetch=2,
            grid=(t // TM_DP,),
            in_specs=[
                pl.BlockSpec((TM_DP * ROW_TILE, LANES), row),
                pl.BlockSpec((TM_DP, D_MODEL), row),
                pl.BlockSpec((D_MODEL, SHARED_DIM), const),
                pl.BlockSpec((D_MODEL, SHARED_DIM), const),
                pl.BlockSpec((SHARED_DIM, D_MODEL), const),
            ],
            out_specs=[pl.BlockSpec(memory_space=pl.ANY), pl.BlockSpec((TM_DP, D_MODEL), row)],
            scratch_shapes=[pltpu.VMEM((MOE_BLK, ROW_TILE, LANES), U32), pltpu.SemaphoreType.DMA,
                            pltpu.SemaphoreType.DMA],
        ),
        out_shape=[jax.ShapeDtypeStruct((n_slots, ROW_TILE, LANES), U32), jax.ShapeDtypeStruct((t, D_MODEL), F32)],
        compiler_params=_cparams(("arbitrary",)),
        name="dispatch",
    )(dest_flat, pad_info, hp, h1, ws1, ws3, ws2)


N_WSLOTS = 2
BLK_PER_STEP = 2


def _expert_kernel(be_ref, nv_ref, first_ref, nxt_ref, x_ref, w1_ref, w3_ref, w2_ref, o_ref,
                   wf1, wf3, wf2, w1b, w3b, w2b, slot_ref, sems):
    step = pl.program_id(0)

    def weight_copies(e, s):
        return (pltpu.make_async_copy(w1_ref.at[e], wf1.at[s], sems.at[s, 0]),
                pltpu.make_async_copy(w3_ref.at[e], wf3.at[s], sems.at[s, 1]),
                pltpu.make_async_copy(w2_ref.at[e], wf2.at[s], sems.at[s, 2]))

    @pl.when(step == 0)
    def _():
        slot_ref[0] = 0
        for cp in weight_copies(be_ref[0], 0):
            cp.start()

    def switch_weights(b):
        @pl.when(first_ref[b] == 1)
        def _():
            s = slot_ref[0]
            for cp in weight_copies(be_ref[b], s):
                cp.wait()

            @pl.when(nxt_ref[b] >= 0)
            def _():
                for cp in weight_copies(nxt_ref[b], 1 - s):
                    cp.start(priority=1)

            w1b[...] = wf1[s].astype(BF16)
            w3b[...] = wf3[s].astype(BF16)
            w2b[...] = wf2[s].astype(BF16)
            slot_ref[0] = 1 - s

    def ffn(xq, oq, m):
        lo, hi = _unpack_halves(_load_row_tiled(xq, m))
        x = jnp.concatenate([lo.astype(BF16), hi.astype(BF16)], axis=1)
        hdn = _silu(_dot(x, w1b[...])) * _dot(x, w3b[...])
        _store_row_tiled(oq, _pack_halves(_dot(hdn.astype(BF16), w2b[...])))

    def one_block(b, xq, oq):
        nv = nv_ref[b]

        @pl.when(nv > 0)
        def _():
            ffn(xq, oq, MOE_BLK)

        @pl.when(nv <= 0)
        def _():
            oq[...] = jnp.zeros_like(oq)

    assert BLK_PER_STEP == 2
    blk_rows = MOE_BLK * ROW_TILE
    b0 = step * BLK_PER_STEP
    b1 = b0 + 1
    lo_rows = pl.ds(0, blk_rows)
    hi_rows = pl.ds(blk_rows, blk_rows)
    same = (first_ref[b1] == 0) & (nv_ref[b1] > 0)
    switch_weights(b0)

    @pl.when(same)
    def _():
        ffn(x_ref, o_ref, 2 * MOE_BLK)

    @pl.when(jnp.logical_not(same))
    def _():
        one_block(b0, x_ref.at[lo_rows, :], o_ref.at[lo_rows, :])
        switch_weights(b1)
        one_block(b1, x_ref.at[hi_rows, :], o_ref.at[hi_rows, :])


def _experts(block_e, block_nv, block_first, block_next, x_sorted, w1, w3, w2):
    n_rows = x_sorted.shape[0]
    step_rows = BLK_PER_STEP * MOE_BLK * ROW_TILE
    row = lambda b, *_: (b, 0)
    return pl.pallas_call(
        _expert_kernel,
        grid_spec=pltpu.PrefetchScalarGridSpec(
            num_scalar_prefetch=4,
            grid=(n_rows // step_rows,),
            in_specs=[
                pl.BlockSpec((step_rows, LANES), row),
                pl.BlockSpec(memory_space=pl.ANY),
                pl.BlockSpec(memory_space=pl.ANY),
                pl.BlockSpec(memory_space=pl.ANY),
            ],
            out_specs=pl.BlockSpec((step_rows, LANES), row),
            scratch_shapes=[
                pltpu.VMEM((N_WSLOTS, D_MODEL, EXPERT_DIM), F32),
                pltpu.VMEM((N_WSLOTS, D_MODEL, EXPERT_DIM), F32),
                pltpu.VMEM((N_WSLOTS, EXPERT_DIM, D_MODEL), F32),
                pltpu.VMEM((D_MODEL, EXPERT_DIM), BF16),
                pltpu.VMEM((D_MODEL, EXPERT_DIM), BF16),
                pltpu.VMEM((EXPERT_DIM, D_MODEL), BF16),
                pltpu.SMEM((1,), I32),
                pltpu.SemaphoreType.DMA((N_WSLOTS, 3)),
            ],
        ),
        out_shape=jax.ShapeDtypeStruct((n_rows, LANES), U32),
        compiler_params=_cparams(("arbitrary",), vmem_mb=VMEM_MB_EXPERTS),
        name="experts",
    )(block_e, block_nv, block_first, block_next, x_sorted, w1, w3, w2)


TM_FN = 256


CH_FN = 8


def _final_kernel(alpha, dest_ref, ys_ref, w_ref, h_ref, sh_ref, g_ref, b_ref, o_ref, buf, sems):
    i = pl.program_id(0)
    n_tiles = pl.num_programs(0)
    t_total = n_tiles * TM_FN
    slot = i % 2
    half = D_MODEL // 2

    def issue_token(tile, sl, c, tt):
        tok = c * CH_FN + tt
        row0 = pl.multiple_of(tok * ROW_TILE, ROW_TILE)
        for k in range(TOP_K):
            src = ys_ref.at[dest_ref[k * t_total + tile * TM_FN + tok]]
            pltpu.make_async_copy(src, buf.at[sl, k, pl.ds(row0, ROW_TILE), :], sems.at[sl]).start(priority=k % 2)

    def wait_tile(sl):
        for k in range(TOP_K):
            pltpu.make_async_copy(ys_ref.at[pl.ds(0, TM_FN)], buf.at[sl, k].reshape(TM_FN, ROW_TILE, LANES),
                                  sems.at[sl]).wait()

    @pl.when(i == 0)
    def _():
        def first(c, carry):
            for tt in range(CH_FN):
                issue_token(0, 0, c, tt)
            return carry
        lax.fori_loop(0, TM_FN // CH_FN, first, 0)

    nxt_tile = jnp.minimum(i + 1, n_tiles - 1)

    def combine_from(cur, nxt):
        wait_tile(cur)

        def body(c, carry):
            tok = pl.ds(pl.multiple_of(c * CH_FN, CH_FN), CH_FN)
            acc_lo = sh_ref[tok, :half]
            acc_hi = sh_ref[tok, half:]
            pre = alpha * h_ref[tok, :]
            wrow = w_ref[tok, :]
            for k in range(TOP_K):
                words = jnp.concatenate(
                    [buf[cur, k, pl.ds(c * (CH_FN * ROW_TILE) + r, CH_FN, stride=ROW_TILE), :]
                     for r in range(ROW_TILE)], axis=1)
                issue_token(nxt_tile, nxt, c, k)
                lo, hi = _unpack_halves(words)
                wk = wrow[:, k:k + 1]
                acc_lo = acc_lo + wk * lo
                acc_hi = acc_hi + wk * hi
            o_ref[tok, :] = pre + jnp.concatenate([acc_lo, acc_hi], axis=1)
            return carry

        lax.fori_loop(0, TM_FN // CH_FN, body, 0)

        @pl.when(i == n_tiles - 1)
        def _():
            wait_tile(nxt)

    for parity in range(2):
        @pl.when(slot == parity)
        def _():
            combine_from(parity, 1 - parity)

    o_ref[...] = _layer_norm(o_ref[...], g_ref[...], b_ref[...])


def _final(dest_flat, ys, w_tok, h1, shared, g, b, alpha):
    t = h1.shape[0]
    row = lambda i, d: (i, 0)
    const = lambda i, d: (0, 0)
    return pl.pallas_call(
        functools.partial(_final_kernel, alpha),
        grid_spec=pltpu.PrefetchScalarGridSpec(
            num_scalar_prefetch=1,
            grid=(t // TM_FN,),
            in_specs=[
                pl.BlockSpec(memory_space=pl.ANY),
                pl.BlockSpec((TM_FN, TOP_K), row),
                pl.BlockSpec((TM_FN, D_MODEL), row),
                pl.BlockSpec((TM_FN, D_MODEL), row),
                pl.BlockSpec((1, D_MODEL), const),
                pl.BlockSpec((1, D_MODEL), const),
            ],
            out_specs=pl.BlockSpec((TM_FN, D_MODEL), row),
            scratch_shapes=[pltpu.VMEM((2, TOP_K, TM_FN * ROW_TILE, LANES), U32), pltpu.SemaphoreType.DMA((2,))],
        ),
        out_shape=jax.ShapeDtypeStruct((t, D_MODEL), F32),
        compiler_params=_cparams(("arbitrary",)),
        name="combine_ln2",
    )(dest_flat, ys, w_tok, h1, shared, g, b)


TM_INB = 512
TN_INB = W_B // 2


def _pad_lanes(v):
    return jnp.pad(v, (0, LANES - v.shape[0]))[None, :].astype(F32)


def _layer(h2d, mem2d, batch, seq, w_in_all, layer, conv_w, conv_b, dt_bias, a_log, d_skip, ssd_norm_g, swa_sinks,
           rel_bias, w_mem_kv, w_ssd_o, w_swa_o, w_xa_o, w_out, ln1_g, ln1_b, router_w, router_bias, w1, w3, w2, ws1,
           ws3, ws2, ln2_g, ln2_b, alpha):
    t = batch * seq
    w_t = jnp.swapaxes(w_in_all, 1, 2)
    wdt = jnp.pad(w_t[layer, W_A:W_A + SSD_HEADS, :].T, ((0, 0), (0, LANES - SSD_HEADS))).astype(BF16)
    pa, dt_raw, xb = _inproj(h2d, _wcast(w_t, layer, 0, W_A, "wcast_a"), wdt)
    pb = _mm(xb, _wcast(w_t, layer, W_A + SSD_HEADS, W_B, "wcast_b"), BF16, TM_INB, TN_INB, "inproj_b",
             vmem_mb=VMEM_MB_LARGE)

    y_ssd = _ssd(
        pa, dt_raw,
        conv_w[:, :SSD_D_INNER], conv_b[None, :SSD_D_INNER], conv_w[:, SSD_D_INNER:], conv_b[None, SSD_D_INNER:],
        _pad_lanes(dt_bias), _pad_lanes(a_log),
        jnp.repeat(d_skip, SSD_HEAD_DIM)[None, :], ssd_norm_g[None, :], batch, seq)

    y_swa = _swa(pb, swa_sinks.astype(F32), _swa_bias_table(rel_bias), batch, seq)

    memkv = _mm(mem2d, w_mem_kv, BF16, batch * MEM_LEN, 1024, "mem_kv")
    y_xa = _xa(pb, memkv, batch, seq)

    merged = _merge(y_ssd, y_swa, y_xa, w_ssd_o, w_swa_o, w_xa_o, pb)
    h1, hp = _out_ln(merged, w_out.astype(BF16), h2d, ln1_g[None, :], ln1_b[None, :], alpha)

    rwt = router_w.T
    r_hi = rwt.astype(BF16)
    r_lo = (rwt - r_hi.astype(F32)).astype(BF16)
    ti = jnp.arange(TM_RT)
    su = (ti[:, None] < ti[None, :]).astype(BF16)
    idx_t, w_t, rank_t, cnt = _router(h1, r_hi, r_lo, router_bias[:, None].astype(F32), su)

    counts = cnt[:, 0].astype(I32)
    padded = (counts + MOE_BLK - 1) // MOE_BLK * MOE_BLK
    pend = jnp.cumsum(padded)
    pstart = pend - padded
    n_blocks = t * TOP_K // MOE_BLK + N_EXPERTS
    blk_lo = jnp.arange(n_blocks, dtype=I32) * MOE_BLK
    block_e = jnp.minimum(jnp.sum((pend[None, :] <= blk_lo[:, None]).astype(I32), axis=1), N_EXPERTS - 1)
    of_e = block_e[:, None] == jnp.arange(N_EXPERTS, dtype=I32)[None, :]
    valid_end = jnp.sum(jnp.where(of_e, (pstart + counts)[None, :], 0), axis=1)
    block_nv = jnp.clip(valid_end - blk_lo, 0, MOE_BLK).astype(I32)
    eids = jnp.arange(N_EXPERTS, dtype=I32)
    later_active = (eids[None, :] > eids[:, None]) & (padded > 0)[None, :]
    next_active = jnp.min(jnp.where(later_active, eids[None, :], N_EXPERTS), axis=1)
    next_active = jnp.where(next_active >= N_EXPERTS, -1, next_active)
    seg_start = jnp.sum(jnp.where(of_e, pstart[None, :], 0), axis=1)
    block_first = ((blk_lo == seg_start) & (block_nv > 0)).astype(I32)
    block_next = jnp.sum(jnp.where(of_e, next_active[None, :], 0), axis=1).astype(I32)

    dest_flat = _dest(idx_t, rank_t, pstart.astype(F32)[:, None]).reshape(-1)
    n_slots = n_blocks * MOE_BLK
    pad_info = jnp.concatenate([pstart + counts, padded - counts, pend[-1:],
                                (n_slots - pend[-1:]) // MOE_BLK]).astype(I32)
    x_sorted, shared = _dispatch(dest_flat, pad_info, hp, h1, ws1.astype(BF16), ws3.astype(BF16),
                                 ws2.astype(BF16), n_slots)
    ys = _experts(block_e, block_nv, block_first, block_next,
                  x_sorted.reshape(n_slots * ROW_TILE, LANES), w1, w3, w2)
    return _final(dest_flat, ys.reshape(n_slots, ROW_TILE, LANES), w_t.T, h1, shared, ln2_g[None, :],
                  ln2_b[None, :], alpha)


def kernel(x, mem, w_in, conv_w, conv_b, dt_bias, a_log, d_skip, ssd_norm_g, swa_sinks, rel_bias, w_mem_kv, w_ssd_o, w_swa_o, w_xa_o, w_out, ln1_g, ln1_b, router_w, router_bias, w1, w3, w2, ws1, ws3, ws2, ln2_g, ln2_b):
    batch, seq, d = x.shape
    depth = w_in.shape[0]
    alpha = (2.0 * depth) ** 0.25
    h = x.reshape(batch * seq, d)
    mem2d = mem.reshape(batch * MEM_LEN, d)
    for i in range(depth):
        h = _layer(h, mem2d, batch, seq, w_in, i, conv_w[i], conv_b[i], dt_bias[i], a_log[i], d_skip[i],
                   ssd_norm_g[i], swa_sinks[i], rel_bias, w_mem_kv[i], w_ssd_o[i], w_swa_o[i], w_xa_o[i], w_out[i],
                   ln1_g[i], ln1_b[i], router_w[i], router_bias[i], w1[i], w3[i], w2[i], ws1[i], ws3[i], ws2[i],
                   ln2_g[i], ln2_b[i], alpha)
    return h.reshape(batch, seq, d)
```

```python
import functools
import math

import jax
import jax.numpy as jnp
from jax import lax
from jax.experimental import pallas as pl
from jax.experimental.pallas import tpu as pltpu

F32 = jnp.float32
BF16 = jnp.bfloat16
I32 = jnp.int32
U32 = jnp.uint32

D_MODEL = 2048
MEM_LEN = 256
SSD_D_INNER = D_MODEL
SSD_HEAD_DIM = 64
SSD_HEADS = 32
SSD_GROUPS = 4
SSD_STATE = 128
SSD_CONV = 4
SSD_CHUNK = 128
SWA_HEADS = 16
SWA_KV_HEADS = 4
SWA_HEAD_DIM = 64
SWA_WINDOW = 128
SWA_BLOCK = 128
REL_BUCKETS = 32
REL_MAX_DIST = 128
XA_HEADS = 4
XA_HEAD_DIM = 256
N_BRANCH = 3
N_EXPERTS = 64
TOP_K = 8
N_EXPERT_GROUPS = 8
TOPK_GROUPS = 4
EXPERT_DIM = 512
SHARED_DIM = 512
ROUTED_SCALE = 2.5
LN_EPS = 1e-5
RMS_EPS = 1e-5

SWA_Q_DIM = SWA_HEADS * SWA_HEAD_DIM
SWA_KV_DIM = SWA_KV_HEADS * SWA_HEAD_DIM
XA_DIM = XA_HEADS * XA_HEAD_DIM
BC_DIM = 2 * SSD_GROUPS * SSD_STATE

W_A = SSD_D_INNER + SSD_D_INNER + BC_DIM
OFFA_Z = 0
OFFA_XS = OFFA_Z + SSD_D_INNER
OFFA_BC = OFFA_XS + SSD_D_INNER
W_B = SWA_Q_DIM + 2 * SWA_KV_DIM + XA_DIM + N_BRANCH * D_MODEL
OFFB_Q = 0
OFFB_K = OFFB_Q + SWA_Q_DIM
OFFB_V = OFFB_K + SWA_KV_DIM
OFFB_QX = OFFB_V + SWA_KV_DIM
OFFB_GATES = OFFB_QX + XA_DIM

LANES = 128
MOE_BLK = 256
NEG_INF = float("-inf")


VMEM_MB = 48
VMEM_MB_LARGE = 56
VMEM_MB_EXPERTS = 58


def _cparams(sem, vmem_mb=VMEM_MB):
    return pltpu.CompilerParams(dimension_semantics=sem, vmem_limit_bytes=vmem_mb * 1024 * 1024)


def _dot(a, b):
    return jnp.dot(a, b, preferred_element_type=F32)


def _dot_nt(a, b):
    return lax.dot_general(a, b, (((1,), (1,)), ((), ())), preferred_element_type=F32)


def _sigmoid(x):
    return 1.0 / (1.0 + jnp.exp(-x))


def _silu(x):
    return x * _sigmoid(x)


TM_IN = 512
TN_IN = W_A // 2


def _inproj_kernel(x_ref, w_ref, wdt_ref, p_ref, dt_ref, xb_ref):
    @pl.when(pl.program_id(1) == 0)
    def _():
        xb = x_ref[...].astype(BF16)
        xb_ref[...] = xb
        dt_ref[...] = _dot(xb, wdt_ref[...])

    p_ref[...] = _dot(xb_ref[...], w_ref[...]).astype(BF16)


def _inproj(x2d, w_in, wdt):
    t = x2d.shape[0]
    return pl.pallas_call(
        _inproj_kernel,
        grid=(t // TM_IN, W_A // TN_IN),
        in_specs=[
            pl.BlockSpec((TM_IN, D_MODEL), lambda i, j: (i, 0)),
            pl.BlockSpec((D_MODEL, TN_IN), lambda i, j: (0, j)),
            pl.BlockSpec((D_MODEL, LANES), lambda i, j: (0, 0)),
        ],
        out_specs=[
            pl.BlockSpec((TM_IN, TN_IN), lambda i, j: (i, j)),
            pl.BlockSpec((TM_IN, LANES), lambda i, j: (i, 0)),
            pl.BlockSpec((TM_IN, D_MODEL), lambda i, j: (i, 0)),
        ],
        out_shape=[jax.ShapeDtypeStruct((t, W_A), BF16), jax.ShapeDtypeStruct((t, LANES), F32),
                   jax.ShapeDtypeStruct((t, D_MODEL), BF16)],
        compiler_params=_cparams(("parallel", "arbitrary"), vmem_mb=VMEM_MB_LARGE),
        name="inproj",
    )(x2d, w_in, wdt)


TN_WC = 512
ROW_ALIGN = 32


def _wcast_kernel(a_ref, o_ref):
    o_ref[...] = a_ref[0].T.astype(BF16)


def _wcast(w_t, layer, start, width, name):
    k = w_t.shape[2]
    return pl.pallas_call(
        _wcast_kernel,
        grid=(width // TN_WC,),
        in_specs=[pl.BlockSpec((pl.Element(1), pl.Element(TN_WC), pl.Element(k)),
                               lambda j: (layer, pl.multiple_of(start + j * TN_WC, ROW_ALIGN), 0))],
        out_specs=pl.BlockSpec((k, TN_WC), lambda j: (0, j)),
        out_shape=jax.ShapeDtypeStruct((k, width), BF16),
        compiler_params=_cparams(("parallel",)),
        name=name,
    )(w_t)


def _mm_kernel(a_ref, b_ref, o_ref):
    o_ref[...] = _dot(a_ref[...].astype(BF16), b_ref[...].astype(BF16)).astype(o_ref.dtype)


def _mm(a, b, out_dtype, tm, tn, name, vmem_mb=VMEM_MB):
    m, k = a.shape
    n = b.shape[1]
    return pl.pallas_call(
        _mm_kernel,
        grid=(m // tm, n // tn),
        in_specs=[pl.BlockSpec((tm, k), lambda i, j: (i, 0)), pl.BlockSpec((k, tn), lambda i, j: (0, j))],
        out_specs=pl.BlockSpec((tm, tn), lambda i, j: (i, j)),
        out_shape=jax.ShapeDtypeStruct((m, n), out_dtype),
        compiler_params=_cparams(("parallel", "parallel"), vmem_mb=vmem_mb),
        name=name,
    )(a, b)


Q = SSD_CHUNK


def _conv_shift_matrix():
    t = jnp.arange(Q)[:, None]
    j = jnp.arange(2 * Q)[None, :]
    blocks = []
    for k in range(SSD_CONV - 1):
        src = t - (SSD_CONV - 1 - k)
        blocks.append(j == jnp.where(src >= 0, src, 2 * Q + src))
    return jnp.concatenate(blocks, axis=0).astype(BF16)


def _ssd_kernel(xs_ref, bc_ref, z_ref, dtr_ref, shift_ref, cwx_ref, cbx_ref, cwb_ref, cbb_ref, dtb_ref, alog_ref,
                dsk_ref, ng_ref, o_ref, hx_ref, hb_ref, st_ref, cumt_ref, dtt_ref, wt_ref, y_ref):
    c = pl.program_id(1)

    @pl.when(c == 0)
    def _():
        hx_ref[...] = jnp.zeros_like(hx_ref)
        hb_ref[...] = jnp.zeros_like(hb_ref)
        st_ref[...] = jnp.zeros_like(st_ref)

    def conv_silu(in_ref, prev_ref, w_ref, b_ref):
        cur = in_ref[...]
        ext = jnp.concatenate([cur, prev_ref[...]], axis=0)
        shifted = _dot(shift_ref[...], ext)
        acc = b_ref[...] + w_ref[SSD_CONV - 1:SSD_CONV, :] * cur.astype(F32)
        for k in range(SSD_CONV - 1):
            acc = acc + w_ref[k:k + 1, :] * shifted[k * Q:(k + 1) * Q, :]
        prev_ref[...] = cur
        return _silu(acc)

    xs_c = conv_silu(xs_ref, hx_ref, cwx_ref, cbx_ref)
    bc_c = conv_silu(bc_ref, hb_ref, cwb_ref, cbb_ref)

    lane = lax.broadcasted_iota(I32, (Q, LANES), 1)
    row = lax.broadcasted_iota(I32, (Q, LANES), 0)
    tril = row >= lane
    lo_mask = lane < SSD_HEAD_DIM

    dtv = dtr_ref[...] + dtb_ref[...]
    dt = jnp.maximum(dtv, 0.0) + jnp.log1p(jnp.exp(-jnp.abs(dtv)))
    a = jnp.where(lane[0:1, :] < SSD_HEADS, -jnp.exp(alog_ref[...]), 0.0)
    la = dt * a
    tri = jnp.where(tril, 1.0, 0.0).astype(BF16)
    p1 = la.astype(BF16)
    r1 = la - p1.astype(F32)
    p2 = r1.astype(BF16)
    p3 = (r1 - p2.astype(F32)).astype(BF16)
    cum = _dot(tri, p1) + _dot(tri, p2) + _dot(tri, p3)
    ecum = jnp.exp(cum)
    cum_t = cum.T
    dt_t = dt.T
    last = cum_t[:, Q - 1:Q]
    cumt_ref[...] = cum_t
    dtt_ref[...] = dt_t
    wt_ref[...] = jnp.exp(last - cum_t) * dt_t
    chunk_decay = jnp.exp(last)

    def head_parts(h, cb, cm, bm_t):
        col = jnp.broadcast_to(cum[:, h:h + 1], (Q, Q))
        rw = cumt_ref[h:h + 1, :]
        dec = jnp.exp(jnp.where(tril, col - rw, NEG_INF))
        m = cb * (dec * dtt_ref[h:h + 1, :])
        ce = cm * jnp.broadcast_to(ecum[:, h:h + 1], (Q, Q))
        sl = bm_t * wt_ref[h:h + 1, :]
        return m.astype(BF16), ce.astype(BF16), sl.astype(BF16)

    for g in range(SSD_GROUPS):
        bm = bc_c[:, g * SSD_STATE:(g + 1) * SSD_STATE]
        cm = bc_c[:, BC_DIM // 2 + g * SSD_STATE:BC_DIM // 2 + (g + 1) * SSD_STATE]
        cb = _dot_nt(cm.astype(BF16), bm.astype(BF16))
        bm_t = bm.T
        for qd in range(SSD_HEADS // SSD_GROUPS // 2):
            hp = g * (SSD_HEADS // SSD_GROUPS // 2) + qd
            ha, hb = 2 * hp, 2 * hp + 1
            sl_ = slice(hp * LANES, (hp + 1) * LANES)
            xs_pair = xs_c[:, sl_]
            x_lo = jnp.where(lo_mask, xs_pair, 0.0).astype(BF16)
            x_hi = jnp.where(lo_mask, 0.0, xs_pair).astype(BF16)
            h_prev = st_ref[:, sl_]
            h_lo = jnp.where(lo_mask, h_prev, 0.0).astype(BF16)
            h_hi = jnp.where(lo_mask, 0.0, h_prev).astype(BF16)
            m_a, ce_a, sl_a = head_parts(ha, cb, cm, bm_t)
            m_b, ce_b, sl_b = head_parts(hb, cb, cm, bm_t)
            lhs = jnp.concatenate([m_a, ce_a, m_b, ce_b], axis=1)
            rhs = jnp.concatenate([x_lo, h_lo, x_hi, h_hi], axis=0)
            y_ref[:, sl_] = _dot(lhs, rhs)
            new = _dot(jnp.concatenate([sl_a, sl_b], axis=1), jnp.concatenate([x_lo, x_hi], axis=0))
            cd = jnp.where(lo_mask[0:1, :], chunk_decay[ha:ha + 1, :], chunk_decay[hb:hb + 1, :])
            st_ref[:, sl_] = h_prev * cd + new

    y = y_ref[...] + xs_c * dsk_ref[...]
    y = y * _silu(z_ref[...].astype(F32))
    gw = SSD_D_INNER // SSD_GROUPS
    for g in range(SSD_GROUPS):
        yg = y[:, g * gw:(g + 1) * gw]
        ms = jnp.mean(yg * yg, axis=-1, keepdims=True)
        o_ref[:, g * gw:(g + 1) * gw] = ((yg * lax.rsqrt(ms + RMS_EPS)) * ng_ref[:, g * gw:(g + 1) * gw]).astype(BF16)


def _ssd(p, dt_raw, cwx, cbx, cwb, cbb, dtb, alog, dsk, ng, batch, seq):
    nc = seq // Q
    t = batch * seq
    rowblk = lambda b, c: b * nc + c
    const = lambda b, c: (0, 0)
    return pl.pallas_call(
        _ssd_kernel,
        grid=(batch, nc),
        in_specs=[
            pl.BlockSpec((Q, SSD_D_INNER), lambda b, c: (rowblk(b, c), OFFA_XS // SSD_D_INNER)),
            pl.BlockSpec((Q, BC_DIM), lambda b, c: (rowblk(b, c), OFFA_BC // BC_DIM)),
            pl.BlockSpec((Q, SSD_D_INNER), lambda b, c: (rowblk(b, c), OFFA_Z // SSD_D_INNER)),
            pl.BlockSpec((Q, LANES), lambda b, c: (rowblk(b, c), 0)),
            pl.BlockSpec(((SSD_CONV - 1) * Q, 2 * Q), const),
            pl.BlockSpec((SSD_CONV, SSD_D_INNER), const),
            pl.BlockSpec((1, SSD_D_INNER), const),
            pl.BlockSpec((SSD_CONV, BC_DIM), const),
            pl.BlockSpec((1, BC_DIM), const),
            pl.BlockSpec((1, LANES), const),
            pl.BlockSpec((1, LANES), const),
            pl.BlockSpec((1, SSD_D_INNER), const),
            pl.BlockSpec((1, SSD_D_INNER), const),
        ],
        out_specs=pl.BlockSpec((Q, SSD_D_INNER), lambda b, c: (rowblk(b, c), 0)),
        out_shape=jax.ShapeDtypeStruct((t, SSD_D_INNER), BF16),
        scratch_shapes=[
            pltpu.VMEM((Q, SSD_D_INNER), BF16),
            pltpu.VMEM((Q, BC_DIM), BF16),
            pltpu.VMEM((SSD_STATE, SSD_D_INNER), F32),
            pltpu.VMEM((LANES, Q), F32),
            pltpu.VMEM((LANES, Q), F32),
            pltpu.VMEM((LANES, Q), F32),
            pltpu.VMEM((Q, SSD_D_INNER), F32),
        ],
        compiler_params=_cparams(("parallel", "arbitrary")),
        name="ssd",
    )(p, p, p, dt_raw, _conv_shift_matrix(), cwx, cbx, cwb, cbb, dtb, alog, dsk, ng)


N_PAIRS = SWA_HEADS // 2


def _swa_kernel(sink_ref, q_ref, kp_ref, kc_ref, vp_ref, vc_ref, bias_ref, o_ref):
    blk = SWA_BLOCK
    kband = jnp.concatenate([kp_ref[...], kc_ref[...]], axis=0)
    vband = jnp.concatenate([vp_ref[...], vc_ref[...]], axis=0)

    lane = lax.broadcasted_iota(I32, (1, LANES), 1)
    lo_f = jnp.where(lane < SWA_HEAD_DIM, 1.0, 0.0)
    lo_m = lo_f.astype(BF16)
    hi_m = (1.0 - lo_f).astype(BF16)

    def head_twice(band, kv):
        tile = band[:, (kv // 2) * LANES:(kv // 2 + 1) * LANES]
        swapped = jnp.concatenate([tile[:, SWA_HEAD_DIM:], tile[:, :SWA_HEAD_DIM]], axis=1)
        return tile * lo_m + swapped * hi_m if kv % 2 == 0 else swapped * lo_m + tile * hi_m

    k_twice = [head_twice(kband, kv) for kv in range(SWA_KV_HEADS)]
    v_twice = [head_twice(vband, kv) for kv in range(SWA_KV_HEADS)]
    rowi = lax.broadcasted_iota(I32, (2 * blk, 1), 0)
    scale = SWA_HEAD_DIM ** -0.5
    assert math.frexp(scale)[0] == 0.5
    lo_q = (lo_f * scale).astype(BF16)
    hi_q = ((1.0 - lo_f) * scale).astype(BF16)
    for p in range(N_PAIRS):
        kv = p // (SWA_HEADS // SWA_KV_HEADS // 2)
        qp = q_ref[:, p * LANES:(p + 1) * LANES]
        qs = jnp.concatenate([qp * lo_q, qp * hi_q], axis=0)
        s = _dot_nt(qs, k_twice[kv]) + bias_ref[0, p]
        sink = jnp.where(rowi < blk, sink_ref[2 * p], sink_ref[2 * p + 1])
        m = jnp.maximum(jnp.max(s, axis=-1, keepdims=True), sink)
        e = jnp.exp(s - m)
        den = jnp.sum(e, axis=-1, keepdims=True) + jnp.exp(sink - m)
        pr = (e * (1.0 / den)).astype(BF16)
        vd = v_twice[kv]
        lhs = jnp.concatenate([pr[:blk], pr[blk:]], axis=1)
        rhs = jnp.concatenate([vd * lo_m, vd * hi_m], axis=0)
        o_ref[:, p * LANES:(p + 1) * LANES] = _dot(lhs, rhs).astype(BF16)


def _swa(p, sinks, bias, batch, seq):
    nb = seq // SWA_BLOCK
    t = batch * seq
    kw = SWA_KV_DIM
    cur = lambda b, n, s: b * nb + n
    prev = lambda b, n, s: b * nb + jnp.maximum(n - 1, 0)
    return pl.pallas_call(
        _swa_kernel,
        grid_spec=pltpu.PrefetchScalarGridSpec(
            num_scalar_prefetch=1,
            grid=(batch, nb),
            in_specs=[
                pl.BlockSpec((SWA_BLOCK, SWA_Q_DIM), lambda b, n, s: (cur(b, n, s), OFFB_Q // SWA_Q_DIM)),
                pl.BlockSpec((SWA_BLOCK, kw), lambda b, n, s: (prev(b, n, s), OFFB_K // kw)),
                pl.BlockSpec((SWA_BLOCK, kw), lambda b, n, s: (cur(b, n, s), OFFB_K // kw)),
                pl.BlockSpec((SWA_BLOCK, kw), lambda b, n, s: (prev(b, n, s), OFFB_V // kw)),
                pl.BlockSpec((SWA_BLOCK, kw), lambda b, n, s: (cur(b, n, s), OFFB_V // kw)),
                pl.BlockSpec((1, N_PAIRS, 2 * SWA_BLOCK, 2 * SWA_BLOCK),
                             lambda b, n, s: (jnp.minimum(n, 1), 0, 0, 0)),
            ],
            out_specs=pl.BlockSpec((SWA_BLOCK, SWA_Q_DIM), lambda b, n, s: (cur(b, n, s), 0)),
        ),
        out_shape=jax.ShapeDtypeStruct((t, SWA_Q_DIM), BF16),
        compiler_params=_cparams(("parallel", "arbitrary")),
        name="swa",
    )(sinks, p, p, p, p, p, bias)


def _t5_causal_bucket(dist):
    max_exact = REL_BUCKETS // 2
    large = max_exact + (jnp.log(jnp.maximum(dist, 1).astype(F32) / max_exact)
                         / math.log(REL_MAX_DIST / max_exact) * (REL_BUCKETS - max_exact)).astype(I32)
    large = jnp.minimum(large, REL_BUCKETS - 1)
    return jnp.where(dist < max_exact, dist, large)


def _swa_bias_table(rel_bias):
    qi = jnp.arange(SWA_BLOCK)[:, None]
    kj = jnp.arange(2 * SWA_BLOCK)[None, :]
    dist = qi + SWA_BLOCK - kj
    in_window = (dist >= 0) & (dist < SWA_WINDOW)
    bucket = _t5_causal_bucket(jnp.maximum(dist, 0))
    onehot = (bucket[:, :, None] == jnp.arange(REL_BUCKETS)[None, None, :]).astype(F32)
    bias = jnp.einsum('ijb,bh->hij', onehot, rel_bias.astype(F32), precision=lax.Precision.HIGHEST)
    rest = jnp.where(in_window[None, :, :], bias, NEG_INF)
    first = jnp.where((kj >= SWA_BLOCK)[None, :, :], rest, NEG_INF)
    return jnp.stack([first, rest]).reshape(2, N_PAIRS, 2 * SWA_BLOCK, 2 * SWA_BLOCK)


TL_XA = 512


XA_HALF = XA_DIM // 2


def _xa_kernel(qa_ref, qb_ref, mk_ref, mv_ref, o_ref):
    for h in range(XA_HEADS):
        q_ref = qa_ref if h < XA_HEADS // 2 else qb_ref
        qc = (h % (XA_HEADS // 2)) * XA_HEAD_DIM
        cols = slice(h * XA_HEAD_DIM, (h + 1) * XA_HEAD_DIM)
        s = _dot_nt(q_ref[:, qc:qc + XA_HEAD_DIM], mk_ref[:, cols]) * (XA_HEAD_DIM ** -0.5)
        m = jnp.max(s, axis=-1, keepdims=True)
        e = jnp.exp(s - m)
        pr = e * (1.0 / jnp.sum(e, axis=-1, keepdims=True))
        o_ref[:, cols] = _dot(pr.astype(BF16), mv_ref[:, cols]).astype(BF16)


def _xa(p, memkv, batch, seq):
    nl = seq // TL_XA
    t = batch * seq
    return pl.pallas_call(
        _xa_kernel,
        grid=(batch, nl),
        in_specs=[
            pl.BlockSpec((TL_XA, XA_HALF), lambda b, l: (b * nl + l, OFFB_QX // XA_HALF)),
            pl.BlockSpec((TL_XA, XA_HALF), lambda b, l: (b * nl + l, OFFB_QX // XA_HALF + 1)),
            pl.BlockSpec((MEM_LEN, XA_DIM), lambda b, l: (b, 0)),
            pl.BlockSpec((MEM_LEN, XA_DIM), lambda b, l: (b, 1)),
        ],
        out_specs=pl.BlockSpec((TL_XA, XA_DIM), lambda b, l: (b * nl + l, 0)),
        out_shape=jax.ShapeDtypeStruct((t, XA_DIM), BF16),
        compiler_params=_cparams(("parallel", "parallel")),
        name="xattn",
    )(p, p, memkv, memkv)


TM_MG = 1024
TN_MG = 512


N_SUB_MG = 2


def _merge_kernel(ys_ref, yw_ref, yx_ref, ws_ref, ww_ref, wx_ref, g0_ref, g1_ref, g2_ref, o_ref):
    sub = TM_MG // N_SUB_MG
    ws = ws_ref[...].astype(BF16)
    ww = ww_ref[...].astype(BF16)
    wx = wx_ref[...].astype(BF16)
    for s in range(N_SUB_MG):
        rows = slice(s * sub, (s + 1) * sub)
        a = _dot(ys_ref[rows, :], ws)
        b = _dot(yw_ref[rows, :], ww)
        c = _dot(yx_ref[rows, :], wx)
        g0 = _sigmoid(g0_ref[rows, :].astype(F32))
        g1 = _sigmoid(g1_ref[rows, :].astype(F32))
        g2 = _sigmoid(g2_ref[rows, :].astype(F32))
        o_ref[rows, :] = (g0 * a + g1 * b + g2 * c).astype(BF16)


def _merge(y_ssd, y_swa, y_xa, w_ssd_o, w_swa_o, w_xa_o, p):
    t = y_ssd.shape[0]
    gpb = D_MODEL // TN_MG
    gate = lambda k: pl.BlockSpec((TM_MG, TN_MG), lambda i, j: (i, OFFB_GATES // TN_MG + k * gpb + j))
    return pl.pallas_call(
        _merge_kernel,
        grid=(t // TM_MG, D_MODEL // TN_MG),
        in_specs=[
            pl.BlockSpec((TM_MG, SSD_D_INNER), lambda i, j: (i, 0)),
            pl.BlockSpec((TM_MG, SWA_Q_DIM), lambda i, j: (i, 0)),
            pl.BlockSpec((TM_MG, XA_DIM), lambda i, j: (i, 0)),
            pl.BlockSpec((SSD_D_INNER, TN_MG), lambda i, j: (0, j)),
            pl.BlockSpec((SWA_Q_DIM, TN_MG), lambda i, j: (0, j)),
            pl.BlockSpec((XA_DIM, TN_MG), lambda i, j: (0, j)),
            gate(0), gate(1), gate(2),
        ],
        out_specs=pl.BlockSpec((TM_MG, TN_MG), lambda i, j: (i, j)),
        out_shape=jax.ShapeDtypeStruct((t, D_MODEL), BF16),
        compiler_params=_cparams(("parallel", "parallel")),
        name="merge",
    )(y_ssd, y_swa, y_xa, w_ssd_o, w_swa_o, w_xa_o, p, p, p)


TM_LN = 512


def _layer_norm(v, g, b):
    mu = jnp.mean(v, axis=-1, keepdims=True)
    d = v - mu
    var = jnp.mean(d * d, axis=-1, keepdims=True)
    return (d * lax.rsqrt(var + LN_EPS)) * g + b


def _pack_halves(v):
    n = v.shape[1] // 2
    lo = lax.bitcast_convert_type(v[:, :n].astype(BF16).astype(F32), U32) >> 16
    hi = lax.bitcast_convert_type(v[:, n:].astype(BF16).astype(F32), U32) & jnp.uint32(0xFFFF0000)
    return hi | lo


def _unpack_halves(w):
    lo = lax.bitcast_convert_type(w << 16, F32)
    hi = lax.bitcast_convert_type(w & jnp.uint32(0xFFFF0000), F32)
    return lo, hi


ROW_WORDS = D_MODEL // 2
ROW_TILE = ROW_WORDS // LANES


def _store_row_tiled(ref, words):
    m = words.shape[0]
    for c in range(ROW_TILE):
        ref[pl.ds(c, m, stride=ROW_TILE), :] = words[:, c * LANES:(c + 1) * LANES]


def _load_row_tiled(ref, m):
    return jnp.concatenate([ref[pl.ds(c, m, stride=ROW_TILE), :] for c in range(ROW_TILE)], axis=1)


N_SUB_LN = 4


def _out_ln_kernel(alpha, m_ref, w_ref, x_ref, g_ref, b_ref, h_ref, hp_ref):
    sub = TM_LN // N_SUB_LN
    for c in range(N_SUB_LN):
        rows = slice(c * sub, (c + 1) * sub)
        mix = _dot(m_ref[rows, :], w_ref[...])
        hn = _layer_norm(alpha * x_ref[rows, :] + mix, g_ref[...], b_ref[...])
        h_ref[rows, :] = hn
        _store_row_tiled(hp_ref.at[pl.ds(c * sub * ROW_TILE, sub * ROW_TILE), :], _pack_halves(hn))


def _out_ln(merged, w_out, x2d, g, b, alpha):
    t = x2d.shape[0]
    row = lambda i: (i, 0)
    const = lambda i: (0, 0)
    return pl.pallas_call(
        functools.partial(_out_ln_kernel, alpha),
        grid=(t // TM_LN,),
        in_specs=[
            pl.BlockSpec((TM_LN, D_MODEL), row),
            pl.BlockSpec((D_MODEL, D_MODEL), const),
            pl.BlockSpec((TM_LN, D_MODEL), row),
            pl.BlockSpec((1, D_MODEL), const),
            pl.BlockSpec((1, D_MODEL), const),
        ],
        out_specs=[pl.BlockSpec((TM_LN, D_MODEL), row), pl.BlockSpec((TM_LN * ROW_TILE, LANES), row)],
        out_shape=[jax.ShapeDtypeStruct((t, D_MODEL), F32), jax.ShapeDtypeStruct((t * ROW_TILE, LANES), U32)],
        compiler_params=_cparams(("parallel",)),
        name="out_ln1",
    )(merged, w_out, x2d, g, b)


TM_RT = 512
EPG = N_EXPERTS // N_EXPERT_GROUPS


def _router_kernel(h_ref, whi_ref, wlo_ref, bias_ref, su_ref, idx_ref, w_ref, rank_ref, cnt_ref, run_ref):
    @pl.when(pl.program_id(0) == 0)
    def _():
        run_ref[...] = jnp.zeros_like(run_ref)

    h = h_ref[...]
    h_hi = h.astype(BF16)
    h_lo = (h - h_hi.astype(F32)).astype(BF16)
    whi = whi_ref[...]
    logits = _dot_nt(whi, h_hi) + _dot_nt(whi, h_lo) + _dot_nt(wlo_ref[...], h_hi)
    scores = _sigmoid(logits)
    sel = scores + bias_ref[...]

    iota_m = lax.broadcasted_iota(I32, (EPG, TM_RT), 0).astype(F32)
    iota_g = lax.broadcasted_iota(I32, (N_EXPERT_GROUPS, TM_RT), 0).astype(F32)
    gs = []
    for g in range(N_EXPERT_GROUPS):
        v = sel[g * EPG:(g + 1) * EPG, :]
        m1 = jnp.max(v, axis=0, keepdims=True)
        i1 = jnp.min(jnp.where(v == m1, iota_m, float(EPG)), axis=0, keepdims=True)
        m2 = jnp.max(jnp.where(iota_m == i1, NEG_INF, v), axis=0, keepdims=True)
        gs.append(m1 + m2)
    gsc = jnp.concatenate(gs, axis=0)
    chosen = jnp.zeros((N_EXPERT_GROUPS, TM_RT), F32)
    for _ in range(TOPK_GROUPS):
        m = jnp.max(gsc, axis=0, keepdims=True)
        ig = jnp.min(jnp.where(gsc == m, iota_g, float(N_EXPERT_GROUPS)), axis=0, keepdims=True)
        hit = iota_g == ig
        chosen = jnp.where(hit, 1.0, chosen)
        gsc = jnp.where(hit, NEG_INF, gsc)
    gmask = jnp.concatenate(
        [jnp.broadcast_to(chosen[g:g + 1, :], (EPG, TM_RT)) for g in range(N_EXPERT_GROUPS)], axis=0)
    masked = jnp.where(gmask > 0.0, sel, NEG_INF)

    iota_e = lax.broadcasted_iota(I32, (N_EXPERTS, TM_RT), 0).astype(F32)
    hits, idxs, ws = [], [], []
    for _ in range(TOP_K):
        m = jnp.max(masked, axis=0, keepdims=True)
        ie = jnp.min(jnp.where(masked == m, iota_e, float(N_EXPERTS)), axis=0, keepdims=True)
        hit = iota_e == ie
        hits.append(hit)
        idxs.append(ie)
        ws.append(jnp.sum(jnp.where(hit, scores, 0.0), axis=0, keepdims=True))
        masked = jnp.where(hit, NEG_INF, masked)
    wsum = ws[0]
    for k in range(1, TOP_K):
        wsum = wsum + ws[k]
    inv = 1.0 / wsum
    idx_ref[...] = jnp.concatenate(idxs, axis=0).astype(I32)
    w_ref[...] = jnp.concatenate([w * inv * ROUTED_SCALE for w in ws], axis=0)

    assign = jnp.zeros((N_EXPERTS, TM_RT), F32)
    for hit in hits:
        assign = jnp.where(hit, 1.0, assign)
    pos = _dot(assign.astype(BF16), su_ref[...]) + run_ref[...]
    rank_ref[...] = jnp.concatenate(
        [jnp.sum(jnp.where(hit, pos, 0.0), axis=0, keepdims=True) for hit in hits], axis=0).astype(I32)
    run = run_ref[...] + jnp.sum(assign, axis=1, keepdims=True)
    run_ref[...] = run
    cnt_ref[...] = run


def _router(h1, r_hi, r_lo, rbias, su):
    t = h1.shape[0]
    tile = lambda i: (0, i)
    const = lambda i: (0, 0)
    return pl.pallas_call(
        _router_kernel,
        grid=(t // TM_RT,),
        in_specs=[
            pl.BlockSpec((TM_RT, D_MODEL), lambda i: (i, 0)),
            pl.BlockSpec((N_EXPERTS, D_MODEL), const),
            pl.BlockSpec((N_EXPERTS, D_MODEL), const),
            pl.BlockSpec((N_EXPERTS, 1), const),
            pl.BlockSpec((TM_RT, TM_RT), const),
        ],
        out_specs=[
            pl.BlockSpec((TOP_K, TM_RT), tile),
            pl.BlockSpec((TOP_K, TM_RT), tile),
            pl.BlockSpec((TOP_K, TM_RT), tile),
            pl.BlockSpec((N_EXPERTS, TM_RT), const),
        ],
        out_shape=[
            jax.ShapeDtypeStruct((TOP_K, t), I32),
            jax.ShapeDtypeStruct((TOP_K, t), F32),
            jax.ShapeDtypeStruct((TOP_K, t), I32),
            jax.ShapeDtypeStruct((N_EXPERTS, TM_RT), F32),
        ],
        scratch_shapes=[pltpu.VMEM((N_EXPERTS, TM_RT), F32)],
        compiler_params=_cparams(("arbitrary",)),
        name="router",
    )(h1, r_hi, r_lo, rbias, su)


TM_DS = 2048


def _dest_kernel(idx_ref, rank_ref, ps_ref, d_ref):
    iota_e = lax.broadcasted_iota(I32, (N_EXPERTS, TM_DS), 0)
    ps = ps_ref[...]
    rows = []
    for k in range(TOP_K):
        hit = iota_e == idx_ref[k:k + 1, :]
        rows.append(jnp.sum(jnp.where(hit, ps, 0.0), axis=0, keepdims=True))
    d_ref[...] = jnp.concatenate(rows, axis=0).astype(I32) + rank_ref[...]


def _dest(idx_t, rank_t, pstart):
    t = idx_t.shape[1]
    tile = lambda i: (0, i)
    return pl.pallas_call(
        _dest_kernel,
        grid=(t // TM_DS,),
        in_specs=[pl.BlockSpec((TOP_K, TM_DS), tile), pl.BlockSpec((TOP_K, TM_DS), tile),
                  pl.BlockSpec((N_EXPERTS, 1), lambda i: (0, 0))],
        out_specs=pl.BlockSpec((TOP_K, TM_DS), tile),
        out_shape=jax.ShapeDtypeStruct((TOP_K, t), I32),
        compiler_params=_cparams(("parallel",)),
        name="dest",
    )(idx_t, rank_t, pstart)


TM_DP = 512


PAD_BITS = tuple(1 << i for i in reversed(range(MOE_BLK.bit_length() - 1)))


def _dispatch_kernel(dest_ref, pad_ref, hp_ref, h_ref, ws1_ref, ws3_ref, ws2_ref, xs_ref, sh_ref, zbuf, sem, zsem):
    t_total = pl.num_programs(0) * TM_DP
    base = pl.program_id(0) * TM_DP

    def pad_copies(act):
        def per_expert(e, carry):
            start = pad_ref[e]
            length = pad_ref[N_EXPERTS + e]
            for bit in PAD_BITS:
                @pl.when((length & bit) != 0)
                def _():
                    off = length - (length & (2 * bit - 1))
                    act(pltpu.make_async_copy(zbuf.at[pl.ds(0, bit)], xs_ref.at[pl.ds(start + off, bit)], zsem))
            return carry

        lax.fori_loop(0, N_EXPERTS, per_expert, 0)

        def per_block(i, carry):
            row = pad_ref[2 * N_EXPERTS] + i * MOE_BLK
            act(pltpu.make_async_copy(zbuf, xs_ref.at[pl.ds(row, MOE_BLK)], zsem))
            return carry

        lax.fori_loop(0, pad_ref[2 * N_EXPERTS + 1], per_block, 0)

    @pl.when(pl.program_id(0) == 0)
    def _():
        zbuf[...] = jnp.zeros_like(zbuf)
        pad_copies(lambda cp: cp.start())

    def issue(t, carry):
        src = hp_ref.at[pl.ds(pl.multiple_of(t * ROW_TILE, ROW_TILE), ROW_TILE), :]
        for k in range(TOP_K):
            slot = dest_ref[k * t_total + base + t]
            pltpu.make_async_copy(src, xs_ref.at[slot], sem).start(priority=k % 2)
        return carry

    lax.fori_loop(0, TM_DP, issue, 0)

    xb = h_ref[...].astype(BF16)
    hdn = _silu(_dot(xb, ws1_ref[...])) * _dot(xb, ws3_ref[...])
    sh_ref[...] = _dot(hdn.astype(BF16), ws2_ref[...])

    for k in range(TOP_K):
        pltpu.make_async_copy(hp_ref.reshape(TM_DP, ROW_TILE, LANES), xs_ref.at[pl.ds(0, TM_DP)], sem).wait()

    @pl.when(pl.program_id(0) == 0)
    def _():
        pad_copies(lambda cp: cp.wait())


def _dispatch(dest_flat, pad_info, hp, h1, ws1, ws3, ws2, n_slots):
    t = h1.shape[0]
    row = lambda i, *_: (i, 0)
    const = lambda i, *_: (0, 0)
    return pl.pallas_call(
        _dispatch_kernel,
        grid_spec=pltpu.PrefetchScalarGridSpec(
            num_scalar_prefetch=2,
            grid=(t // TM_DP,),
            in_specs=[
                pl.BlockSpec((TM_DP * ROW_TILE, LANES), row),
                pl.BlockSpec((TM_DP, D_MODEL), row),
                pl.BlockSpec((D_MODEL, SHARED_DIM), const),
                pl.BlockSpec((D_MODEL, SHARED_DIM), const),
                pl.BlockSpec((SHARED_DIM, D_MODEL), const),
            ],
            out_specs=[pl.BlockSpec(memory_space=pl.ANY), pl.BlockSpec((TM_DP, D_MODEL), row)],
            scratch_shapes=[pltpu.VMEM((MOE_BLK, ROW_TILE, LANES), U32), pltpu.SemaphoreType.DMA,
                            pltpu.SemaphoreType.DMA],
        ),
        out_shape=[jax.ShapeDtypeStruct((n_slots, ROW_TILE, LANES), U32), jax.ShapeDtypeStruct((t, D_MODEL), F32)],
        compiler_params=_cparams(("arbitrary",)),
        name="dispatch",
    )(dest_flat, pad_info, hp, h1, ws1, ws3, ws2)


N_WSLOTS = 2
BLK_PER_STEP = 4


def _expert_kernel(be_ref, nv_ref, first_ref, nxt_ref, x_ref, w1_ref, w3_ref, w2_ref, o_ref,
                   wf1, wf3, wf2, w1b, w3b, w2b, slot_ref, sems):
    step = pl.program_id(0)

    def weight_copies(e, s):
        return (pltpu.make_async_copy(w1_ref.at[e], wf1.at[s], sems.at[s, 0]),
                pltpu.make_async_copy(w3_ref.at[e], wf3.at[s], sems.at[s, 1]),
                pltpu.make_async_copy(w2_ref.at[e], wf2.at[s], sems.at[s, 2]))

    @pl.when(step == 0)
    def _():
        slot_ref[0] = 0
        for cp in weight_copies(be_ref[0], 0):
            cp.start()

    def switch_weights(b):
        @pl.when(first_ref[b] == 1)
        def _():
            s = slot_ref[0]
            for cp in weight_copies(be_ref[b], s):
                cp.wait()

            @pl.when(nxt_ref[b] >= 0)
            def _():
                for cp in weight_copies(nxt_ref[b], 1 - s):
                    cp.start(priority=1)

            w1b[...] = wf1[s].astype(BF16)
            w3b[...] = wf3[s].astype(BF16)
            w2b[...] = wf2[s].astype(BF16)
            slot_ref[0] = 1 - s

    def ffn(xq, oq, m):
        lo, hi = _unpack_halves(_load_row_tiled(xq, m))
        x = jnp.concatenate([lo.astype(BF16), hi.astype(BF16)], axis=1)
        hdn = _silu(_dot(x, w1b[...])) * _dot(x, w3b[...])
        _store_row_tiled(oq, _pack_halves(_dot(hdn.astype(BF16), w2b[...])))

    def one_block(b, xq, oq):
        nv = nv_ref[b]

        @pl.when(nv > 0)
        def _():
            ffn(xq, oq, MOE_BLK)

        @pl.when(nv <= 0)
        def _():
            oq[...] = jnp.zeros_like(oq)

    assert BLK_PER_STEP % 2 == 0
    blk_rows = MOE_BLK * ROW_TILE

    def block_pair(b0, row0):
        b1 = b0 + 1
        lo_rows = pl.ds(row0, blk_rows)
        hi_rows = pl.ds(row0 + blk_rows, blk_rows)
        both = pl.ds(row0, 2 * blk_rows)
        same = (first_ref[b1] == 0) & (nv_ref[b1] > 0)
        switch_weights(b0)

        @pl.when(same)
        def _():
            ffn(x_ref.at[both, :], o_ref.at[both, :], 2 * MOE_BLK)

        @pl.when(jnp.logical_not(same))
        def _():
            one_block(b0, x_ref.at[lo_rows, :], o_ref.at[lo_rows, :])
            switch_weights(b1)
            one_block(b1, x_ref.at[hi_rows, :], o_ref.at[hi_rows, :])

    for pr in range(BLK_PER_STEP // 2):
        block_pair(step * BLK_PER_STEP + 2 * pr, 2 * pr * blk_rows)


def _experts(block_e, block_nv, block_first, block_next, x_sorted, w1, w3, w2):
    n_rows = x_sorted.shape[0]
    step_rows = BLK_PER_STEP * MOE_BLK * ROW_TILE
    row = lambda b, *_: (b, 0)
    return pl.pallas_call(
        _expert_kernel,
        grid_spec=pltpu.PrefetchScalarGridSpec(
            num_scalar_prefetch=4,
            grid=(n_rows // step_rows,),
            in_specs=[
                pl.BlockSpec((step_rows, LANES), row),
                pl.BlockSpec(memory_space=pl.ANY),
                pl.BlockSpec(memory_space=pl.ANY),
                pl.BlockSpec(memory_space=pl.ANY),
            ],
            out_specs=pl.BlockSpec((step_rows, LANES), row),
            scratch_shapes=[
                pltpu.VMEM((N_WSLOTS, D_MODEL, EXPERT_DIM), F32),
                pltpu.VMEM((N_WSLOTS, D_MODEL, EXPERT_DIM), F32),
                pltpu.VMEM((N_WSLOTS, EXPERT_DIM, D_MODEL), F32),
                pltpu.VMEM((D_MODEL, EXPERT_DIM), BF16),
                pltpu.VMEM((D_MODEL, EXPERT_DIM), BF16),
                pltpu.VMEM((EXPERT_DIM, D_MODEL), BF16),
                pltpu.SMEM((1,), I32),
                pltpu.SemaphoreType.DMA((N_WSLOTS, 3)),
            ],
        ),
        out_shape=jax.ShapeDtypeStruct((n_rows, LANES), U32),
        compiler_params=_cparams(("arbitrary",), vmem_mb=VMEM_MB_EXPERTS),
        name="experts",
    )(block_e, block_nv, block_first, block_next, x_sorted, w1, w3, w2)


TM_FN = 256


CH_FN = 8


def _final_kernel(alpha, dest_ref, ys_ref, w_ref, h_ref, sh_ref, g_ref, b_ref, o_ref, buf, sems):
    i = pl.program_id(0)
    n_tiles = pl.num_programs(0)
    t_total = n_tiles * TM_FN
    slot = i % 2
    half = D_MODEL // 2

    def issue_token(tile, sl, c, tt):
        tok = c * CH_FN + tt
        row0 = pl.multiple_of(tok * ROW_TILE, ROW_TILE)
        for k in range(TOP_K):
            src = ys_ref.at[dest_ref[k * t_total + tile * TM_FN + tok]]
            pltpu.make_async_copy(src, buf.at[sl, k, pl.ds(row0, ROW_TILE), :], sems.at[sl]).start(priority=k % 2)

    def wait_tile(sl):
        for k in range(TOP_K):
            pltpu.make_async_copy(ys_ref.at[pl.ds(0, TM_FN)], buf.at[sl, k].reshape(TM_FN, ROW_TILE, LANES),
                                  sems.at[sl]).wait()

    @pl.when(i == 0)
    def _():
        def first(c, carry):
            for tt in range(CH_FN):
                issue_token(0, 0, c, tt)
            return carry
        lax.fori_loop(0, TM_FN // CH_FN, first, 0)

    nxt_tile = jnp.minimum(i + 1, n_tiles - 1)

    def combine_from(cur, nxt):
        wait_tile(cur)

        def body(c, carry):
            tok = pl.ds(pl.multiple_of(c * CH_FN, CH_FN), CH_FN)
            acc_lo = sh_ref[tok, :half]
            acc_hi = sh_ref[tok, half:]
            pre = alpha * h_ref[tok, :]
            wrow = w_ref[tok, :]
            for k in range(TOP_K):
                words = jnp.concatenate(
                    [buf[cur, k, pl.ds(c * (CH_FN * ROW_TILE) + r, CH_FN, stride=ROW_TILE), :]
                     for r in range(ROW_TILE)], axis=1)
                issue_token(nxt_tile, nxt, c, k)
                lo, hi = _unpack_halves(words)
                wk = wrow[:, k:k + 1]
                acc_lo = acc_lo + wk * lo
                acc_hi = acc_hi + wk * hi
            o_ref[tok, :] = pre + jnp.concatenate([acc_lo, acc_hi], axis=1)
            return carry

        lax.fori_loop(0, TM_FN // CH_FN, body, 0)

        @pl.when(i == n_tiles - 1)
        def _():
            wait_tile(nxt)

    for parity in range(2):
        @pl.when(slot == parity)
        def _():
            combine_from(parity, 1 - parity)

    o_ref[...] = _layer_norm(o_ref[...], g_ref[...], b_ref[...])


def _final(dest_flat, ys, w_tok, h1, shared, g, b, alpha):
    t = h1.shape[0]
    row = lambda i, d: (i, 0)
    const = lambda i, d: (0, 0)
    return pl.pallas_call(
        functools.partial(_final_kernel, alpha),
        grid_spec=pltpu.PrefetchScalarGridSpec(
            num_scalar_prefetch=1,
            grid=(t // TM_FN,),
            in_specs=[
                pl.BlockSpec(memory_space=pl.ANY),
                pl.BlockSpec((TM_FN, TOP_K), row),
                pl.BlockSpec((TM_FN, D_MODEL), row),
                pl.BlockSpec((TM_FN, D_MODEL), row),
                pl.BlockSpec((1, D_MODEL), const),
                pl.BlockSpec((1, D_MODEL), const),
            ],
            out_specs=pl.BlockSpec((TM_FN, D_MODEL), row),
            scratch_shapes=[pltpu.VMEM((2, TOP_K, TM_FN * ROW_TILE, LANES), U32), pltpu.SemaphoreType.DMA((2,))],
        ),
        out_shape=jax.ShapeDtypeStruct((t, D_MODEL), F32),
        compiler_params=_cparams(("arbitrary",)),
        name="combine_ln2",
    )(dest_flat, ys, w_tok, h1, shared, g, b)


TM_INB = 512
TN_INB = W_B // 2


def _pad_lanes(v):
    return jnp.pad(v, (0, LANES - v.shape[0]))[None, :].astype(F32)


def _layer(h2d, mem2d, batch, seq, w_in_all, layer, conv_w, conv_b, dt_bias, a_log, d_skip, ssd_norm_g, swa_sinks,
           rel_bias, w_mem_kv, w_ssd_o, w_swa_o, w_xa_o, w_out, ln1_g, ln1_b, router_w, router_bias, w1, w3, w2, ws1,
           ws3, ws2, ln2_g, ln2_b, alpha):
    t = batch * seq
    w_t = jnp.swapaxes(w_in_all, 1, 2)
    wdt = jnp.pad(w_t[layer, W_A:W_A + SSD_HEADS, :].T, ((0, 0), (0, LANES - SSD_HEADS))).astype(BF16)
    pa, dt_raw, xb = _inproj(h2d, _wcast(w_t, layer, 0, W_A, "wcast_a"), wdt)
    pb = _mm(xb, _wcast(w_t, layer, W_A + SSD_HEADS, W_B, "wcast_b"), BF16, TM_INB, TN_INB, "inproj_b",
             vmem_mb=VMEM_MB_LARGE)

    y_ssd = _ssd(
        pa, dt_raw,
        conv_w[:, :SSD_D_INNER], conv_b[None, :SSD_D_INNER], conv_w[:, SSD_D_INNER:], conv_b[None, SSD_D_INNER:],
        _pad_lanes(dt_bias), _pad_lanes(a_log),
        jnp.repeat(d_skip, SSD_HEAD_DIM)[None, :], ssd_norm_g[None, :], batch, seq)

    y_swa = _swa(pb, swa_sinks.astype(F32), _swa_bias_table(rel_bias), batch, seq)

    memkv = _mm(mem2d, w_mem_kv, BF16, batch * MEM_LEN, 1024, "mem_kv")
    y_xa = _xa(pb, memkv, batch, seq)

    merged = _merge(y_ssd, y_swa, y_xa, w_ssd_o, w_swa_o, w_xa_o, pb)
    h1, hp = _out_ln(merged, w_out.astype(BF16), h2d, ln1_g[None, :], ln1_b[None, :], alpha)

    rwt = router_w.T
    r_hi = rwt.astype(BF16)
    r_lo = (rwt - r_hi.astype(F32)).astype(BF16)
    ti = jnp.arange(TM_RT)
    su = (ti[:, None] < ti[None, :]).astype(BF16)
    idx_t, w_t, rank_t, cnt = _router(h1, r_hi, r_lo, router_bias[:, None].astype(F32), su)

    counts = cnt[:, 0].astype(I32)
    padded = (counts + MOE_BLK - 1) // MOE_BLK * MOE_BLK
    pend = jnp.cumsum(padded)
    pstart = pend - padded
    n_blocks = t * TOP_K // MOE_BLK + N_EXPERTS
    blk_lo = jnp.arange(n_blocks, dtype=I32) * MOE_BLK
    block_e = jnp.minimum(jnp.sum((pend[None, :] <= blk_lo[:, None]).astype(I32), axis=1), N_EXPERTS - 1)
    of_e = block_e[:, None] == jnp.arange(N_EXPERTS, dtype=I32)[None, :]
    valid_end = jnp.sum(jnp.where(of_e, (pstart + counts)[None, :], 0), axis=1)
    block_nv = jnp.clip(valid_end - blk_lo, 0, MOE_BLK).astype(I32)
    eids = jnp.arange(N_EXPERTS, dtype=I32)
    later_active = (eids[None, :] > eids[:, None]) & (padded > 0)[None, :]
    next_active = jnp.min(jnp.where(later_active, eids[None, :], N_EXPERTS), axis=1)
    next_active = jnp.where(next_active >= N_EXPERTS, -1, next_active)
    seg_start = jnp.sum(jnp.where(of_e, pstart[None, :], 0), axis=1)
    block_first = ((blk_lo == seg_start) & (block_nv > 0)).astype(I32)
    block_next = jnp.sum(jnp.where(of_e, next_active[None, :], 0), axis=1).astype(I32)

    dest_flat = _dest(idx_t, rank_t, pstart.astype(F32)[:, None]).reshape(-1)
    n_slots = n_blocks * MOE_BLK
    pad_info = jnp.concatenate([pstart + counts, padded - counts, pend[-1:],
                                (n_slots - pend[-1:]) // MOE_BLK]).astype(I32)
    x_sorted, shared = _dispatch(dest_flat, pad_info, hp, h1, ws1.astype(BF16), ws3.astype(BF16),
                                 ws2.astype(BF16), n_slots)
    ys = _experts(block_e, block_nv, block_first, block_next,
                  x_sorted.reshape(n_slots * ROW_TILE, LANES), w1, w3, w2)
    return _final(dest_flat, ys.reshape(n_slots, ROW_TILE, LANES), w_t.T, h1, shared, ln2_g[None, :],
                  ln2_b[None, :], alpha)


def kernel(x, mem, w_in, conv_w, conv_b, dt_bias, a_log, d_skip, ssd_norm_g, swa_sinks, rel_bias, w_mem_kv, w_ssd_o, w_swa_o, w_xa_o, w_out, ln1_g, ln1_b, router_w, router_bias, w1, w3, w2, ws1, ws3, ws2, ln2_g, ln2_b):
    batch, seq, d = x.shape
    depth = w_in.shape[0]
    alpha = (2.0 * depth) ** 0.25
    h = x.reshape(batch * seq, d)
    mem2d = mem.reshape(batch * MEM_LEN, d)
    for i in range(depth):
        h = _layer(h, mem2d, batch, seq, w_in, i, conv_w[i], conv_b[i], dt_bias[i], a_log[i], d_skip[i],
                   ssd_norm_g[i], swa_sinks[i], rel_bias, w_mem_kv[i], w_ssd_o[i], w_swa_o[i], w_xa_o[i], w_out[i],
                   ln1_g[i], ln1_b[i], router_w[i], router_bias[i], w1[i], w3[i], w2[i], ws1[i], ws3[i], ws2[i],
                   ln2_g[i], ln2_b[i], alpha)
    return h.reshape(batch, seq, d)
```

```python
import functools
import math

import jax
import jax.numpy as jnp
from jax import lax
from jax.experimental import pallas as pl
from jax.experimental.pallas import tpu as pltpu

F32 = jnp.float32
BF16 = jnp.bfloat16
I32 = jnp.int32
U32 = jnp.uint32

D_MODEL = 2048
MEM_LEN = 256
SSD_D_INNER = D_MODEL
SSD_HEAD_DIM = 64
SSD_HEADS = 32
SSD_GROUPS = 4
SSD_STATE = 128
SSD_CONV = 4
SSD_CHUNK = 128
SWA_HEADS = 16
SWA_KV_HEADS = 4
SWA_HEAD_DIM = 64
SWA_WINDOW = 128
SWA_BLOCK = 128
REL_BUCKETS = 32
REL_MAX_DIST = 128
XA_HEADS = 4
XA_HEAD_DIM = 256
N_BRANCH = 3
N_EXPERTS = 64
TOP_K = 8
N_EXPERT_GROUPS = 8
TOPK_GROUPS = 4
EXPERT_DIM = 512
SHARED_DIM = 512
ROUTED_SCALE = 2.5
LN_EPS = 1e-5
RMS_EPS = 1e-5

SWA_Q_DIM = SWA_HEADS * SWA_HEAD_DIM
SWA_KV_DIM = SWA_KV_HEADS * SWA_HEAD_DIM
XA_DIM = XA_HEADS * XA_HEAD_DIM
BC_DIM = 2 * SSD_GROUPS * SSD_STATE

W_A = SSD_D_INNER + SSD_D_INNER + BC_DIM
OFFA_Z = 0
OFFA_XS = OFFA_Z + SSD_D_INNER
OFFA_BC = OFFA_XS + SSD_D_INNER
W_B = SWA_Q_DIM + 2 * SWA_KV_DIM + XA_DIM + N_BRANCH * D_MODEL
OFFB_Q = 0
OFFB_K = OFFB_Q + SWA_Q_DIM
OFFB_V = OFFB_K + SWA_KV_DIM
OFFB_QX = OFFB_V + SWA_KV_DIM
OFFB_GATES = OFFB_QX + XA_DIM

LANES = 128
MOE_BLK = 256
NEG_INF = float("-inf")


VMEM_MB = 48
VMEM_MB_LARGE = 56
VMEM_MB_EXPERTS = 58


def _cparams(sem, vmem_mb=VMEM_MB):
    return pltpu.CompilerParams(dimension_semantics=sem, vmem_limit_bytes=vmem_mb * 1024 * 1024)


def _dot(a, b):
    return jnp.dot(a, b, preferred_element_type=F32)


def _dot_nt(a, b):
    return lax.dot_general(a, b, (((1,), (1,)), ((), ())), preferred_element_type=F32)


def _sigmoid(x):
    return 1.0 / (1.0 + jnp.exp(-x))


def _silu(x):
    return x * _sigmoid(x)


TM_IN = 512
TN_IN = W_A // 2


def _inproj_kernel(x_ref, w_ref, wdt_ref, p_ref, dt_ref, xb_ref):
    @pl.when(pl.program_id(1) == 0)
    def _():
        xb = x_ref[...].astype(BF16)
        xb_ref[...] = xb
        dt_ref[...] = _dot(xb, wdt_ref[...])

    p_ref[...] = _dot(xb_ref[...], w_ref[...]).astype(BF16)


def _inproj(x2d, w_in, wdt):
    t = x2d.shape[0]
    return pl.pallas_call(
        _inproj_kernel,
        grid=(t // TM_IN, W_A // TN_IN),
        in_specs=[
            pl.BlockSpec((TM_IN, D_MODEL), lambda i, j: (i, 0)),
            pl.BlockSpec((D_MODEL, TN_IN), lambda i, j: (0, j)),
            pl.BlockSpec((D_MODEL, LANES), lambda i, j: (0, 0)),
        ],
        out_specs=[
            pl.BlockSpec((TM_IN, TN_IN), lambda i, j: (i, j)),
            pl.BlockSpec((TM_IN, LANES), lambda i, j: (i, 0)),
            pl.BlockSpec((TM_IN, D_MODEL), lambda i, j: (i, 0)),
        ],
        out_shape=[jax.ShapeDtypeStruct((t, W_A), BF16), jax.ShapeDtypeStruct((t, LANES), F32),
                   jax.ShapeDtypeStruct((t, D_MODEL), BF16)],
        compiler_params=_cparams(("parallel", "arbitrary"), vmem_mb=VMEM_MB_LARGE),
        name="inproj",
    )(x2d, w_in, wdt)


TN_WC = 512
ROW_ALIGN = 32


def _wcast_kernel(a_ref, o_ref):
    o_ref[...] = a_ref[0].T.astype(BF16)


def _wcast(w_t, layer, start, width, name):
    k = w_t.shape[2]
    return pl.pallas_call(
        _wcast_kernel,
        grid=(width // TN_WC,),
        in_specs=[pl.BlockSpec((pl.Element(1), pl.Element(TN_WC), pl.Element(k)),
                               lambda j: (layer, pl.multiple_of(start + j * TN_WC, ROW_ALIGN), 0))],
        out_specs=pl.BlockSpec((k, TN_WC), lambda j: (0, j)),
        out_shape=jax.ShapeDtypeStruct((k, width), BF16),
        compiler_params=_cparams(("parallel",)),
        name=name,
    )(w_t)


def _mm_kernel(a_ref, b_ref, o_ref):
    o_ref[...] = _dot(a_ref[...].astype(BF16), b_ref[...].astype(BF16)).astype(o_ref.dtype)


def _mm(a, b, out_dtype, tm, tn, name, vmem_mb=VMEM_MB):
    m, k = a.shape
    n = b.shape[1]
    return pl.pallas_call(
        _mm_kernel,
        grid=(m // tm, n // tn),
        in_specs=[pl.BlockSpec((tm, k), lambda i, j: (i, 0)), pl.BlockSpec((k, tn), lambda i, j: (0, j))],
        out_specs=pl.BlockSpec((tm, tn), lambda i, j: (i, j)),
        out_shape=jax.ShapeDtypeStruct((m, n), out_dtype),
        compiler_params=_cparams(("parallel", "parallel"), vmem_mb=vmem_mb),
        name=name,
    )(a, b)


Q = SSD_CHUNK


def _conv_shift_matrix():
    t = jnp.arange(Q)[:, None]
    j = jnp.arange(2 * Q)[None, :]
    blocks = []
    for k in range(SSD_CONV - 1):
        src = t - (SSD_CONV - 1 - k)
        blocks.append(j == jnp.where(src >= 0, src, 2 * Q + src))
    return jnp.concatenate(blocks, axis=0).astype(BF16)


def _ssd_kernel(xs_ref, bc_ref, z_ref, dtr_ref, shift_ref, cwx_ref, cbx_ref, cwb_ref, cbb_ref, dtb_ref, alog_ref,
                dsk_ref, ng_ref, o_ref, hx_ref, hb_ref, st_ref, cumt_ref, dtt_ref, wt_ref, y_ref):
    c = pl.program_id(1)

    @pl.when(c == 0)
    def _():
        hx_ref[...] = jnp.zeros_like(hx_ref)
        hb_ref[...] = jnp.zeros_like(hb_ref)
        st_ref[...] = jnp.zeros_like(st_ref)

    def conv_silu(in_ref, prev_ref, w_ref, b_ref):
        cur = in_ref[...]
        ext = jnp.concatenate([cur, prev_ref[...]], axis=0)
        shifted = _dot(shift_ref[...], ext)
        acc = b_ref[...] + w_ref[SSD_CONV - 1:SSD_CONV, :] * cur.astype(F32)
        for k in range(SSD_CONV - 1):
            acc = acc + w_ref[k:k + 1, :] * shifted[k * Q:(k + 1) * Q, :]
        prev_ref[...] = cur
        return _silu(acc)

    xs_c = conv_silu(xs_ref, hx_ref, cwx_ref, cbx_ref)
    bc_c = conv_silu(bc_ref, hb_ref, cwb_ref, cbb_ref)

    lane = lax.broadcasted_iota(I32, (Q, LANES), 1)
    row = lax.broadcasted_iota(I32, (Q, LANES), 0)
    tril = row >= lane
    lo_mask = lane < SSD_HEAD_DIM

    dtv = dtr_ref[...] + dtb_ref[...]
    dt = jnp.maximum(dtv, 0.0) + jnp.log1p(jnp.exp(-jnp.abs(dtv)))
    a = jnp.where(lane[0:1, :] < SSD_HEADS, -jnp.exp(alog_ref[...]), 0.0)
    la = dt * a
    tri = jnp.where(tril, 1.0, 0.0).astype(BF16)
    p1 = la.astype(BF16)
    r1 = la - p1.astype(F32)
    p2 = r1.astype(BF16)
    p3 = (r1 - p2.astype(F32)).astype(BF16)
    cum = _dot(tri, p1) + _dot(tri, p2) + _dot(tri, p3)
    ecum = jnp.exp(cum)
    cum_t = cum.T
    dt_t = dt.T
    last = cum_t[:, Q - 1:Q]
    cumt_ref[...] = cum_t
    dtt_ref[...] = dt_t
    wt_ref[...] = jnp.exp(last - cum_t) * dt_t
    chunk_decay = jnp.exp(last)

    def head_parts(h, cb, cm, bm_t):
        col = jnp.broadcast_to(cum[:, h:h + 1], (Q, Q))
        rw = cumt_ref[h:h + 1, :]
        dec = jnp.exp(jnp.where(tril, col - rw, NEG_INF))
        m = cb * (dec * dtt_ref[h:h + 1, :])
        ce = cm * jnp.broadcast_to(ecum[:, h:h + 1], (Q, Q))
        sl = bm_t * wt_ref[h:h + 1, :]
        return m.astype(BF16), ce.astype(BF16), sl.astype(BF16)

    for g in range(SSD_GROUPS):
        bm = bc_c[:, g * SSD_STATE:(g + 1) * SSD_STATE]
        cm = bc_c[:, BC_DIM // 2 + g * SSD_STATE:BC_DIM // 2 + (g + 1) * SSD_STATE]
        cb = _dot_nt(cm.astype(BF16), bm.astype(BF16))
        bm_t = bm.T
        for qd in range(SSD_HEADS // SSD_GROUPS // 2):
            hp = g * (SSD_HEADS // SSD_GROUPS // 2) + qd
            ha, hb = 2 * hp, 2 * hp + 1
            sl_ = slice(hp * LANES, (hp + 1) * LANES)
            xs_pair = xs_c[:, sl_]
            x_lo = jnp.where(lo_mask, xs_pair, 0.0).astype(BF16)
            x_hi = jnp.where(lo_mask, 0.0, xs_pair).astype(BF16)
            h_prev = st_ref[:, sl_]
            h_lo = jnp.where(lo_mask, h_prev, 0.0).astype(BF16)
            h_hi = jnp.where(lo_mask, 0.0, h_prev).astype(BF16)
            m_a, ce_a, sl_a = head_parts(ha, cb, cm, bm_t)
            m_b, ce_b, sl_b = head_parts(hb, cb, cm, bm_t)
            lhs = jnp.concatenate([m_a, ce_a, m_b, ce_b], axis=1)
            rhs = jnp.concatenate([x_lo, h_lo, x_hi, h_hi], axis=0)
            y_ref[:, sl_] = _dot(lhs, rhs)
            new = _dot(jnp.concatenate([sl_a, sl_b], axis=1), jnp.concatenate([x_lo, x_hi], axis=0))
            cd = jnp.where(lo_mask[0:1, :], chunk_decay[ha:ha + 1, :], chunk_decay[hb:hb + 1, :])
            st_ref[:, sl_] = h_prev * cd + new

    y = y_ref[...] + xs_c * dsk_ref[...]
    y = y * _silu(z_ref[...].astype(F32))
    gw = SSD_D_INNER // SSD_GROUPS
    for g in range(SSD_GROUPS):
        yg = y[:, g * gw:(g + 1) * gw]
        ms = jnp.mean(yg * yg, axis=-1, keepdims=True)
        o_ref[:, g * gw:(g + 1) * gw] = ((yg * lax.rsqrt(ms + RMS_EPS)) * ng_ref[:, g * gw:(g + 1) * gw]).astype(BF16)


def _ssd(p, dt_raw, cwx, cbx, cwb, cbb, dtb, alog, dsk, ng, batch, seq):
    nc = seq // Q
    t = batch * seq
    rowblk = lambda b, c: b * nc + c
    const = lambda b, c: (0, 0)
    return pl.pallas_call(
        _ssd_kernel,
        grid=(batch, nc),
        in_specs=[
            pl.BlockSpec((Q, SSD_D_INNER), lambda b, c: (rowblk(b, c), OFFA_XS // SSD_D_INNER)),
            pl.BlockSpec((Q, BC_DIM), lambda b, c: (rowblk(b, c), OFFA_BC // BC_DIM)),
            pl.BlockSpec((Q, SSD_D_INNER), lambda b, c: (rowblk(b, c), OFFA_Z // SSD_D_INNER)),
            pl.BlockSpec((Q, LANES), lambda b, c: (rowblk(b, c), 0)),
            pl.BlockSpec(((SSD_CONV - 1) * Q, 2 * Q), const),
            pl.BlockSpec((SSD_CONV, SSD_D_INNER), const),
            pl.BlockSpec((1, SSD_D_INNER), const),
            pl.BlockSpec((SSD_CONV, BC_DIM), const),
            pl.BlockSpec((1, BC_DIM), const),
            pl.BlockSpec((1, LANES), const),
            pl.BlockSpec((1, LANES), const),
            pl.BlockSpec((1, SSD_D_INNER), const),
            pl.BlockSpec((1, SSD_D_INNER), const),
        ],
        out_specs=pl.BlockSpec((Q, SSD_D_INNER), lambda b, c: (rowblk(b, c), 0)),
        out_shape=jax.ShapeDtypeStruct((t, SSD_D_INNER), BF16),
        scratch_shapes=[
            pltpu.VMEM((Q, SSD_D_INNER), BF16),
            pltpu.VMEM((Q, BC_DIM), BF16),
            pltpu.VMEM((SSD_STATE, SSD_D_INNER), F32),
            pltpu.VMEM((LANES, Q), F32),
            pltpu.VMEM((LANES, Q), F32),
            pltpu.VMEM((LANES, Q), F32),
            pltpu.VMEM((Q, SSD_D_INNER), F32),
        ],
        compiler_params=_cparams(("parallel", "arbitrary")),
        name="ssd",
    )(p, p, p, dt_raw, _conv_shift_matrix(), cwx, cbx, cwb, cbb, dtb, alog, dsk, ng)


N_PAIRS = SWA_HEADS // 2


def _swa_kernel(sink_ref, q_ref, kp_ref, kc_ref, vp_ref, vc_ref, bias_ref, o_ref):
    blk = SWA_BLOCK
    kband = jnp.concatenate([kp_ref[...], kc_ref[...]], axis=0)
    vband = jnp.concatenate([vp_ref[...], vc_ref[...]], axis=0)

    lane = lax.broadcasted_iota(I32, (1, LANES), 1)
    lo_f = jnp.where(lane < SWA_HEAD_DIM, 1.0, 0.0)
    lo_m = lo_f.astype(BF16)
    hi_m = (1.0 - lo_f).astype(BF16)

    def head_twice(band, kv):
        tile = band[:, (kv // 2) * LANES:(kv // 2 + 1) * LANES]
        swapped = jnp.concatenate([tile[:, SWA_HEAD_DIM:], tile[:, :SWA_HEAD_DIM]], axis=1)
        return tile * lo_m + swapped * hi_m if kv % 2 == 0 else swapped * lo_m + tile * hi_m

    k_twice = [head_twice(kband, kv) for kv in range(SWA_KV_HEADS)]
    v_twice = [head_twice(vband, kv) for kv in range(SWA_KV_HEADS)]
    rowi = lax.broadcasted_iota(I32, (2 * blk, 1), 0)
    scale = SWA_HEAD_DIM ** -0.5
    assert math.frexp(scale)[0] == 0.5
    lo_q = (lo_f * scale).astype(BF16)
    hi_q = ((1.0 - lo_f) * scale).astype(BF16)
    for p in range(N_PAIRS):
        kv = p // (SWA_HEADS // SWA_KV_HEADS // 2)
        qp = q_ref[:, p * LANES:(p + 1) * LANES]
        qs = jnp.concatenate([qp * lo_q, qp * hi_q], axis=0)
        s = _dot_nt(qs, k_twice[kv]) + bias_ref[0, p]
        sink = jnp.where(rowi < blk, sink_ref[2 * p], sink_ref[2 * p + 1])
        m = jnp.maximum(jnp.max(s, axis=-1, keepdims=True), sink)
        e = jnp.exp(s - m)
        den = jnp.sum(e, axis=-1, keepdims=True) + jnp.exp(sink - m)
        pr = (e * (1.0 / den)).astype(BF16)
        vd = v_twice[kv]
        lhs = jnp.concatenate([pr[:blk], pr[blk:]], axis=1)
        rhs = jnp.concatenate([vd * lo_m, vd * hi_m], axis=0)
        o_ref[:, p * LANES:(p + 1) * LANES] = _dot(lhs, rhs).astype(BF16)


def _swa(p, sinks, bias, batch, seq):
    nb = seq // SWA_BLOCK
    t = batch * seq
    kw = SWA_KV_DIM
    cur = lambda b, n, s: b * nb + n
    prev = lambda b, n, s: b * nb + jnp.maximum(n - 1, 0)
    return pl.pallas_call(
        _swa_kernel,
        grid_spec=pltpu.PrefetchScalarGridSpec(
            num_scalar_prefetch=1,
            grid=(batch, nb),
            in_specs=[
                pl.BlockSpec((SWA_BLOCK, SWA_Q_DIM), lambda b, n, s: (cur(b, n, s), OFFB_Q // SWA_Q_DIM)),
                pl.BlockSpec((SWA_BLOCK, kw), lambda b, n, s: (prev(b, n, s), OFFB_K // kw)),
                pl.BlockSpec((SWA_BLOCK, kw), lambda b, n, s: (cur(b, n, s), OFFB_K // kw)),
                pl.BlockSpec((SWA_BLOCK, kw), lambda b, n, s: (prev(b, n, s), OFFB_V // kw)),
                pl.BlockSpec((SWA_BLOCK, kw), lambda b, n, s: (cur(b, n, s), OFFB_V // kw)),
                pl.BlockSpec((1, N_PAIRS, 2 * SWA_BLOCK, 2 * SWA_BLOCK),
                             lambda b, n, s: (jnp.minimum(n, 1), 0, 0, 0)),
            ],
            out_specs=pl.BlockSpec((SWA_BLOCK, SWA_Q_DIM), lambda b, n, s: (cur(b, n, s), 0)),
        ),
        out_shape=jax.ShapeDtypeStruct((t, SWA_Q_DIM), BF16),
        compiler_params=_cparams(("parallel", "arbitrary")),
        name="swa",
    )(sinks, p, p, p, p, p, bias)


def _t5_causal_bucket(dist):
    max_exact = REL_BUCKETS // 2
    large = max_exact + (jnp.log(jnp.maximum(dist, 1).astype(F32) / max_exact)
                         / math.log(REL_MAX_DIST / max_exact) * (REL_BUCKETS - max_exact)).astype(I32)
    large = jnp.minimum(large, REL_BUCKETS - 1)
    return jnp.where(dist < max_exact, dist, large)


def _swa_bias_table(rel_bias):
    qi = jnp.arange(SWA_BLOCK)[:, None]
    kj = jnp.arange(2 * SWA_BLOCK)[None, :]
    dist = qi + SWA_BLOCK - kj
    in_window = (dist >= 0) & (dist < SWA_WINDOW)
    bucket = _t5_causal_bucket(jnp.maximum(dist, 0))
    onehot = (bucket[:, :, None] == jnp.arange(REL_BUCKETS)[None, None, :]).astype(F32)
    bias = jnp.einsum('ijb,bh->hij', onehot, rel_bias.astype(F32), precision=lax.Precision.HIGHEST)
    rest = jnp.where(in_window[None, :, :], bias, NEG_INF)
    first = jnp.where((kj >= SWA_BLOCK)[None, :, :], rest, NEG_INF)
    return jnp.stack([first, rest]).reshape(2, N_PAIRS, 2 * SWA_BLOCK, 2 * SWA_BLOCK)


TL_XA = 512


XA_HALF = XA_DIM // 2


def _xa_kernel(qa_ref, qb_ref, mk_ref, mv_ref, o_ref):
    for h in range(XA_HEADS):
        q_ref = qa_ref if h < XA_HEADS // 2 else qb_ref
        qc = (h % (XA_HEADS // 2)) * XA_HEAD_DIM
        cols = slice(h * XA_HEAD_DIM, (h + 1) * XA_HEAD_DIM)
        s = _dot_nt(q_ref[:, qc:qc + XA_HEAD_DIM], mk_ref[:, cols]) * (XA_HEAD_DIM ** -0.5)
        m = jnp.max(s, axis=-1, keepdims=True)
        e = jnp.exp(s - m)
        pr = e * (1.0 / jnp.sum(e, axis=-1, keepdims=True))
        o_ref[:, cols] = _dot(pr.astype(BF16), mv_ref[:, cols]).astype(BF16)


def _xa(p, memkv, batch, seq):
    nl = seq // TL_XA
    t = batch * seq
    return pl.pallas_call(
        _xa_kernel,
        grid=(batch, nl),
        in_specs=[
            pl.BlockSpec((TL_XA, XA_HALF), lambda b, l: (b * nl + l, OFFB_QX // XA_HALF)),
            pl.BlockSpec((TL_XA, XA_HALF), lambda b, l: (b * nl + l, OFFB_QX // XA_HALF + 1)),
            pl.BlockSpec((MEM_LEN, XA_DIM), lambda b, l: (b, 0)),
            pl.BlockSpec((MEM_LEN, XA_DIM), lambda b, l: (b, 1)),
        ],
        out_specs=pl.BlockSpec((TL_XA, XA_DIM), lambda b, l: (b * nl + l, 0)),
        out_shape=jax.ShapeDtypeStruct((t, XA_DIM), BF16),
        compiler_params=_cparams(("parallel", "parallel")),
        name="xattn",
    )(p, p, memkv, memkv)


TM_MG = 1024
TN_MG = 512


N_SUB_MG = 2


def _merge_kernel(ys_ref, yw_ref, yx_ref, ws_ref, ww_ref, wx_ref, g0_ref, g1_ref, g2_ref, o_ref):
    sub = TM_MG // N_SUB_MG
    ws = ws_ref[...].astype(BF16)
    ww = ww_ref[...].astype(BF16)
    wx = wx_ref[...].astype(BF16)
    for s in range(N_SUB_MG):
        rows = slice(s * sub, (s + 1) * sub)
        a = _dot(ys_ref[rows, :], ws)
        b = _dot(yw_ref[rows, :], ww)
        c = _dot(yx_ref[rows, :], wx)
        g0 = _sigmoid(g0_ref[rows, :].astype(F32))
        g1 = _sigmoid(g1_ref[rows, :].astype(F32))
        g2 = _sigmoid(g2_ref[rows, :].astype(F32))
        o_ref[rows, :] = (g0 * a + g1 * b + g2 * c).astype(BF16)


def _merge(y_ssd, y_swa, y_xa, w_ssd_o, w_swa_o, w_xa_o, p):
    t = y_ssd.shape[0]
    gpb = D_MODEL // TN_MG
    gate = lambda k: pl.BlockSpec((TM_MG, TN_MG), lambda i, j: (i, OFFB_GATES // TN_MG + k * gpb + j))
    return pl.pallas_call(
        _merge_kernel,
        grid=(t // TM_MG, D_MODEL // TN_MG),
        in_specs=[
            pl.BlockSpec((TM_MG, SSD_D_INNER), lambda i, j: (i, 0)),
            pl.BlockSpec((TM_MG, SWA_Q_DIM), lambda i, j: (i, 0)),
            pl.BlockSpec((TM_MG, XA_DIM), lambda i, j: (i, 0)),
            pl.BlockSpec((SSD_D_INNER, TN_MG), lambda i, j: (0, j)),
            pl.BlockSpec((SWA_Q_DIM, TN_MG), lambda i, j: (0, j)),
            pl.BlockSpec((XA_DIM, TN_MG), lambda i, j: (0, j)),
            gate(0), gate(1), gate(2),
        ],
        out_specs=pl.BlockSpec((TM_MG, TN_MG), lambda i, j: (i, j)),
        out_shape=jax.ShapeDtypeStruct((t, D_MODEL), BF16),
        compiler_params=_cparams(("parallel", "parallel")),
        name="merge",
    )(y_ssd, y_swa, y_xa, w_ssd_o, w_swa_o, w_xa_o, p, p, p)


TM_LN = 512


def _layer_norm(v, g, b):
    mu = jnp.mean(v, axis=-1, keepdims=True)
    d = v - mu
    var = jnp.mean(d * d, axis=-1, keepdims=True)
    return (d * lax.rsqrt(var + LN_EPS)) * g + b


def _pack_halves(v):
    n = v.shape[1] // 2
    lo = lax.bitcast_convert_type(v[:, :n].astype(BF16).astype(F32), U32) >> 16
    hi = lax.bitcast_convert_type(v[:, n:].astype(BF16).astype(F32), U32) & jnp.uint32(0xFFFF0000)
    return hi | lo


def _unpack_halves(w):
    lo = lax.bitcast_convert_type(w << 16, F32)
    hi = lax.bitcast_convert_type(w & jnp.uint32(0xFFFF0000), F32)
    return lo, hi


ROW_WORDS = D_MODEL // 2
ROW_TILE = ROW_WORDS // LANES


def _store_row_tiled(ref, words):
    m = words.shape[0]
    for c in range(ROW_TILE):
        ref[pl.ds(c, m, stride=ROW_TILE), :] = words[:, c * LANES:(c + 1) * LANES]


def _load_row_tiled(ref, m):
    return jnp.concatenate([ref[pl.ds(c, m, stride=ROW_TILE), :] for c in range(ROW_TILE)], axis=1)


N_SUB_LN = 4


def _out_ln_kernel(alpha, m_ref, w_ref, x_ref, g_ref, b_ref, h_ref, hp_ref):
    sub = TM_LN // N_SUB_LN
    for c in range(N_SUB_LN):
        rows = slice(c * sub, (c + 1) * sub)
        mix = _dot(m_ref[rows, :], w_ref[...])
        hn = _layer_norm(alpha * x_ref[rows, :] + mix, g_ref[...], b_ref[...])
        h_ref[rows, :] = hn
        _store_row_tiled(hp_ref.at[pl.ds(c * sub * ROW_TILE, sub * ROW_TILE), :], _pack_halves(hn))


def _out_ln(merged, w_out, x2d, g, b, alpha):
    t = x2d.shape[0]
    row = lambda i: (i, 0)
    const = lambda i: (0, 0)
    return pl.pallas_call(
        functools.partial(_out_ln_kernel, alpha),
        grid=(t // TM_LN,),
        in_specs=[
            pl.BlockSpec((TM_LN, D_MODEL), row),
            pl.BlockSpec((D_MODEL, D_MODEL), const),
            pl.BlockSpec((TM_LN, D_MODEL), row),
            pl.BlockSpec((1, D_MODEL), const),
            pl.BlockSpec((1, D_MODEL), const),
        ],
        out_specs=[pl.BlockSpec((TM_LN, D_MODEL), row), pl.BlockSpec((TM_LN * ROW_TILE, LANES), row)],
        out_shape=[jax.ShapeDtypeStruct((t, D_MODEL), F32), jax.ShapeDtypeStruct((t * ROW_TILE, LANES), U32)],
        compiler_params=_cparams(("parallel",)),
        name="out_ln1",
    )(merged, w_out, x2d, g, b)


TM_RT = 512
EPG = N_EXPERTS // N_EXPERT_GROUPS


def _router_kernel(h_ref, whi_ref, wlo_ref, bias_ref, su_ref, idx_ref, w_ref, rank_ref, cnt_ref, run_ref):
    @pl.when(pl.program_id(0) == 0)
    def _():
        run_ref[...] = jnp.zeros_like(run_ref)

    h = h_ref[...]
    h_hi = h.astype(BF16)
    h_lo = (h - h_hi.astype(F32)).astype(BF16)
    whi = whi_ref[...]
    logits = _dot_nt(whi, h_hi) + _dot_nt(whi, h_lo) + _dot_nt(wlo_ref[...], h_hi)
    scores = _sigmoid(logits)
    sel = scores + bias_ref[...]

    iota_m = lax.broadcasted_iota(I32, (EPG, TM_RT), 0).astype(F32)
    iota_g = lax.broadcasted_iota(I32, (N_EXPERT_GROUPS, TM_RT), 0).astype(F32)
    gs = []
    for g in range(N_EXPERT_GROUPS):
        v = sel[g * EPG:(g + 1) * EPG, :]
        m1 = jnp.max(v, axis=0, keepdims=True)
        i1 = jnp.min(jnp.where(v == m1, iota_m, float(EPG)), axis=0, keepdims=True)
        m2 = jnp.max(jnp.where(iota_m == i1, NEG_INF, v), axis=0, keepdims=True)
        gs.append(m1 + m2)
    gsc = jnp.concatenate(gs, axis=0)
    chosen = jnp.zeros((N_EXPERT_GROUPS, TM_RT), F32)
    for _ in range(TOPK_GROUPS):
        m = jnp.max(gsc, axis=0, keepdims=True)
        ig = jnp.min(jnp.where(gsc == m, iota_g, float(N_EXPERT_GROUPS)), axis=0, keepdims=True)
        hit = iota_g == ig
        chosen = jnp.where(hit, 1.0, chosen)
        gsc = jnp.where(hit, NEG_INF, gsc)
    gmask = jnp.concatenate(
        [jnp.broadcast_to(chosen[g:g + 1, :], (EPG, TM_RT)) for g in range(N_EXPERT_GROUPS)], axis=0)
    masked = jnp.where(gmask > 0.0, sel, NEG_INF)

    iota_e = lax.broadcasted_iota(I32, (N_EXPERTS, TM_RT), 0).astype(F32)
    hits, idxs, ws = [], [], []
    for _ in range(TOP_K):
        m = jnp.max(masked, axis=0, keepdims=True)
        ie = jnp.min(jnp.where(masked == m, iota_e, float(N_EXPERTS)), axis=0, keepdims=True)
        hit = iota_e == ie
        hits.append(hit)
        idxs.append(ie)
        ws.append(jnp.sum(jnp.where(hit, scores, 0.0), axis=0, keepdims=True))
        masked = jnp.where(hit, NEG_INF, masked)
    wsum = ws[0]
    for k in range(1, TOP_K):
        wsum = wsum + ws[k]
    inv = 1.0 / wsum
    idx_ref[...] = jnp.concatenate(idxs, axis=0).astype(I32)
    w_ref[...] = jnp.concatenate([w * inv * ROUTED_SCALE for w in ws], axis=0)

    assign = jnp.zeros((N_EXPERTS, TM_RT), F32)
    for hit in hits:
        assign = jnp.where(hit, 1.0, assign)
    pos = _dot(assign.astype(BF16), su_ref[...]) + run_ref[...]
    rank_ref[...] = jnp.concatenate(
        [jnp.sum(jnp.where(hit, pos, 0.0), axis=0, keepdims=True) for hit in hits], axis=0).astype(I32)
    run = run_ref[...] + jnp.sum(assign, axis=1, keepdims=True)
    run_ref[...] = run
    cnt_ref[...] = run


def _router(h1, r_hi, r_lo, rbias, su):
    t = h1.shape[0]
    tile = lambda i: (0, i)
    const = lambda i: (0, 0)
    return pl.pallas_call(
        _router_kernel,
        grid=(t // TM_RT,),
        in_specs=[
            pl.BlockSpec((TM_RT, D_MODEL), lambda i: (i, 0)),
            pl.BlockSpec((N_EXPERTS, D_MODEL), const),
            pl.BlockSpec((N_EXPERTS, D_MODEL), const),
            pl.BlockSpec((N_EXPERTS, 1), const),
            pl.BlockSpec((TM_RT, TM_RT), const),
        ],
        out_specs=[
            pl.BlockSpec((TOP_K, TM_RT), tile),
            pl.BlockSpec((TOP_K, TM_RT), tile),
            pl.BlockSpec((TOP_K, TM_RT), tile),
            pl.BlockSpec((N_EXPERTS, TM_RT), const),
        ],
        out_shape=[
            jax.ShapeDtypeStruct((TOP_K, t), I32),
            jax.ShapeDtypeStruct((TOP_K, t), F32),
            jax.ShapeDtypeStruct((TOP_K, t), I32),
            jax.ShapeDtypeStruct((N_EXPERTS, TM_RT), F32),
        ],
        scratch_shapes=[pltpu.VMEM((N_EXPERTS, TM_RT), F32)],
        compiler_params=_cparams(("arbitrary",)),
        name="router",
    )(h1, r_hi, r_lo, rbias, su)


TM_DS = 2048


def _dest_kernel(idx_ref, rank_ref, ps_ref, d_ref):
    iota_e = lax.broadcasted_iota(I32, (N_EXPERTS, TM_DS), 0)
    ps = ps_ref[...]
    rows = []
    for k in range(TOP_K):
        hit = iota_e == idx_ref[k:k + 1, :]
        rows.append(jnp.sum(jnp.where(hit, ps, 0.0), axis=0, keepdims=True))
    d_ref[...] = jnp.concatenate(rows, axis=0).astype(I32) + rank_ref[...]


def _dest(idx_t, rank_t, pstart):
    t = idx_t.shape[1]
    tile = lambda i: (0, i)
    return pl.pallas_call(
        _dest_kernel,
        grid=(t // TM_DS,),
        in_specs=[pl.BlockSpec((TOP_K, TM_DS), tile), pl.BlockSpec((TOP_K, TM_DS), tile),
                  pl.BlockSpec((N_EXPERTS, 1), lambda i: (0, 0))],
        out_specs=pl.BlockSpec((TOP_K, TM_DS), tile),
        out_shape=jax.ShapeDtypeStruct((TOP_K, t), I32),
        compiler_params=_cparams(("parallel",)),
        name="dest",
    )(idx_t, rank_t, pstart)


TM_DP = 512


PAD_BITS = tuple(1 << i for i in reversed(range(MOE_BLK.bit_length() - 1)))


def _dispatch_kernel(dest_ref, pad_ref, hp_ref, h_ref, ws1_ref, ws3_ref, ws2_ref, xs_ref, sh_ref, zbuf, sem, zsem):
    t_total = pl.num_programs(0) * TM_DP
    base = pl.program_id(0) * TM_DP

    def pad_copies(act):
        def per_expert(e, carry):
            start = pad_ref[e]
            length = pad_ref[N_EXPERTS + e]
            for bit in PAD_BITS:
                @pl.when((length & bit) != 0)
                def _():
                    off = length - (length & (2 * bit - 1))
                    act(pltpu.make_async_copy(zbuf.at[pl.ds(0, bit)], xs_ref.at[pl.ds(start + off, bit)], zsem))
            return carry

        lax.fori_loop(0, N_EXPERTS, per_expert, 0)

        def per_block(i, carry):
            row = pad_ref[2 * N_EXPERTS] + i * MOE_BLK
            act(pltpu.make_async_copy(zbuf, xs_ref.at[pl.ds(row, MOE_BLK)], zsem))
            return carry

        lax.fori_loop(0, pad_ref[2 * N_EXPERTS + 1], per_block, 0)

    @pl.when(pl.program_id(0) == 0)
    def _():
        zbuf[...] = jnp.zeros_like(zbuf)
        pad_copies(lambda cp: cp.start())

    def issue(t, carry):
        src = hp_ref.at[pl.ds(pl.multiple_of(t * ROW_TILE, ROW_TILE), ROW_TILE), :]
        for k in range(TOP_K):
            slot = dest_ref[k * t_total + base + t]
            pltpu.make_async_copy(src, xs_ref.at[slot], sem).start(priority=k % 2)
        return carry

    lax.fori_loop(0, TM_DP, issue, 0)

    xb = h_ref[...].astype(BF16)
    hdn = _silu(_dot(xb, ws1_ref[...])) * _dot(xb, ws3_ref[...])
    sh_ref[...] = _dot(hdn.astype(BF16), ws2_ref[...])

    for k in range(TOP_K):
        pltpu.make_async_copy(hp_ref.reshape(TM_DP, ROW_TILE, LANES), xs_ref.at[pl.ds(0, TM_DP)], sem).wait()

    @pl.when(pl.program_id(0) == 0)
    def _():
        pad_copies(lambda cp: cp.wait())


def _dispatch(dest_flat, pad_info, hp, h1, ws1, ws3, ws2, n_slots):
    t = h1.shape[0]
    row = lambda i, *_: (i, 0)
    const = lambda i, *_: (0, 0)
    return pl.pallas_call(
        _dispatch_kernel,
        grid_spec=pltpu.PrefetchScalarGridSpec(
            num_scalar_prefetch=2,
            grid=(t // TM_DP,),
            in_specs=[
                pl.BlockSpec((TM_DP * ROW_TILE, LANES), row),
                pl.BlockSpec((TM_DP, D_MODEL), row),
                pl.BlockSpec((D_MODEL, SHARED_DIM), const),
                pl.BlockSpec((D_MODEL, SHARED_DIM), const),
                pl.BlockSpec((SHARED_DIM, D_MODEL), const),
            ],
            out_specs=[pl.BlockSpec(memory_space=pl.ANY), pl.BlockSpec((TM_DP, D_MODEL), row)],
            scratch_shapes=[pltpu.VMEM((MOE_BLK, ROW_TILE, LANES), U32), pltpu.SemaphoreType.DMA,
                            pltpu.SemaphoreType.DMA],
        ),
        out_shape=[jax.ShapeDtypeStruct((n_slots, ROW_TILE, LANES), U32), jax.ShapeDtypeStruct((t, D_MODEL), F32)],
        compiler_params=_cparams(("arbitrary",)),
        name="dispatch",
    )(dest_flat, pad_info, hp, h1, ws1, ws3, ws2)


N_WSLOTS = 2
BLK_PER_STEP = 2


def _expert_kernel(be_ref, nv_ref, first_ref, nxt_ref, x_ref, w1_ref, w3_ref, w2_ref, o_ref,
                   wf1, wf3, wf2, w1b, w3b, w2b, slot_ref, sems):
    step = pl.program_id(0)

    def weight_copies(e, s):
        return (pltpu.make_async_copy(w1_ref.at[e], wf1.at[s], sems.at[s, 0]),
                pltpu.make_async_copy(w3_ref.at[e], wf3.at[s], sems.at[s, 1]),
                pltpu.make_async_copy(w2_ref.at[e], wf2.at[s], sems.at[s, 2]))

    @pl.when(step == 0)
    def _():
        slot_ref[0] = 0
        for cp in weight_copies(be_ref[0], 0):
            cp.start()

    def switch_weights(b):
        @pl.when(first_ref[b] == 1)
        def _():
            s = slot_ref[0]
            for cp in weight_copies(be_ref[b], s):
                cp.wait()

            @pl.when(nxt_ref[b] >= 0)
            def _():
                for cp in weight_copies(nxt_ref[b], 1 - s):
                    cp.start()

            w1b[...] = wf1[s].astype(BF16)
            w3b[...] = wf3[s].astype(BF16)
            w2b[...] = wf2[s].astype(BF16)
            slot_ref[0] = 1 - s

    def ffn(xq, oq, m):
        lo, hi = _unpack_halves(_load_row_tiled(xq, m))
        x = jnp.concatenate([lo.astype(BF16), hi.astype(BF16)], axis=1)
        hdn = _silu(_dot(x, w1b[...])) * _dot(x, w3b[...])
        _store_row_tiled(oq, _pack_halves(_dot(hdn.astype(BF16), w2b[...])))

    def one_block(b, xq, oq):
        nv = nv_ref[b]

        @pl.when(nv > 0)
        def _():
            ffn(xq, oq, MOE_BLK)

        @pl.when(nv <= 0)
        def _():
            oq[...] = jnp.zeros_like(oq)

    assert BLK_PER_STEP == 2
    blk_rows = MOE_BLK * ROW_TILE
    b0 = step * BLK_PER_STEP
    b1 = b0 + 1
    lo_rows = pl.ds(0, blk_rows)
    hi_rows = pl.ds(blk_rows, blk_rows)
    same = (first_ref[b1] == 0) & (nv_ref[b1] > 0)
    switch_weights(b0)

    @pl.when(same)
    def _():
        ffn(x_ref, o_ref, 2 * MOE_BLK)

    @pl.when(jnp.logical_not(same))
    def _():
        one_block(b0, x_ref.at[lo_rows, :], o_ref.at[lo_rows, :])
        switch_weights(b1)
        one_block(b1, x_ref.at[hi_rows, :], o_ref.at[hi_rows, :])


def _experts(block_e, block_nv, block_first, block_next, x_sorted, w1, w3, w2):
    n_rows = x_sorted.shape[0]
    step_rows = BLK_PER_STEP * MOE_BLK * ROW_TILE
    row = lambda b, *_: (b, 0)
    return pl.pallas_call(
        _expert_kernel,
        grid_spec=pltpu.PrefetchScalarGridSpec(
            num_scalar_prefetch=4,
            grid=(n_rows // step_rows,),
            in_specs=[
                pl.BlockSpec((step_rows, LANES), row),
                pl.BlockSpec(memory_space=pl.ANY),
                pl.BlockSpec(memory_space=pl.ANY),
                pl.BlockSpec(memory_space=pl.ANY),
            ],
            out_specs=pl.BlockSpec((step_rows, LANES), row),
            scratch_shapes=[
                pltpu.VMEM((N_WSLOTS, D_MODEL, EXPERT_DIM), F32),
                pltpu.VMEM((N_WSLOTS, D_MODEL, EXPERT_DIM), F32),
                pltpu.VMEM((N_WSLOTS, EXPERT_DIM, D_MODEL), F32),
                pltpu.VMEM((D_MODEL, EXPERT_DIM), BF16),
                pltpu.VMEM((D_MODEL, EXPERT_DIM), BF16),
                pltpu.VMEM((EXPERT_DIM, D_MODEL), BF16),
                pltpu.SMEM((1,), I32),
                pltpu.SemaphoreType.DMA((N_WSLOTS, 3)),
            ],
        ),
        out_shape=jax.ShapeDtypeStruct((n_rows, LANES), U32),
        compiler_params=_cparams(("arbitrary",), vmem_mb=VMEM_MB_EXPERTS),
        name="experts",
    )(block_e, block_nv, block_first, block_next, x_sorted, w1, w3, w2)


TM_FN = 256


CH_FN = 8


def _final_kernel(alpha, dest_ref, ys_ref, w_ref, h_ref, sh_ref, g_ref, b_ref, o_ref, buf, sems):
    i = pl.program_id(0)
    n_tiles = pl.num_programs(0)
    t_total = n_tiles * TM_FN
    slot = i % 2
    half = D_MODEL // 2

    def issue_token(tile, sl, c, tt):
        tok = c * CH_FN + tt
        row0 = pl.multiple_of(tok * ROW_TILE, ROW_TILE)
        for k in range(TOP_K):
            src = ys_ref.at[dest_ref[k * t_total + tile * TM_FN + tok]]
            pltpu.make_async_copy(src, buf.at[sl, k, pl.ds(row0, ROW_TILE), :], sems.at[sl]).start(priority=k % 2)

    def wait_tile(sl):
        for k in range(TOP_K):
            pltpu.make_async_copy(ys_ref.at[pl.ds(0, TM_FN)], buf.at[sl, k].reshape(TM_FN, ROW_TILE, LANES),
                                  sems.at[sl]).wait()

    @pl.when(i == 0)
    def _():
        def first(c, carry):
            for tt in range(CH_FN):
                issue_token(0, 0, c, tt)
            return carry
        lax.fori_loop(0, TM_FN // CH_FN, first, 0)

    nxt_tile = jnp.minimum(i + 1, n_tiles - 1)

    def combine_from(cur, nxt):
        wait_tile(cur)

        def body(c, carry):
            tok = pl.ds(pl.multiple_of(c * CH_FN, CH_FN), CH_FN)
            acc_lo = sh_ref[tok, :half]
            acc_hi = sh_ref[tok, half:]
            pre = alpha * h_ref[tok, :]
            wrow = w_ref[tok, :]
            for k in range(TOP_K):
                words = jnp.concatenate(
                    [buf[cur, k, pl.ds(c * (CH_FN * ROW_TILE) + r, CH_FN, stride=ROW_TILE), :]
                     for r in range(ROW_TILE)], axis=1)
                issue_token(nxt_tile, nxt, c, k)
                lo, hi = _unpack_halves(words)
                wk = wrow[:, k:k + 1]
                acc_lo = acc_lo + wk * lo
                acc_hi = acc_hi + wk * hi
            o_ref[tok, :] = pre + jnp.concatenate([acc_lo, acc_hi], axis=1)
            return carry

        lax.fori_loop(0, TM_FN // CH_FN, body, 0)

        @pl.when(i == n_tiles - 1)
        def _():
            wait_tile(nxt)

    for parity in range(2):
        @pl.when(slot == parity)
        def _():
            combine_from(parity, 1 - parity)

    o_ref[...] = _layer_norm(o_ref[...], g_ref[...], b_ref[...])


def _final(dest_flat, ys, w_tok, h1, shared, g, b, alpha):
    t = h1.shape[0]
    row = lambda i, d: (i, 0)
    const = lambda i, d: (0, 0)
    return pl.pallas_call(
        functools.partial(_final_kernel, alpha),
        grid_spec=pltpu.PrefetchScalarGridSpec(
            num_scalar_prefetch=1,
            grid=(t // TM_FN,),
            in_specs=[
                pl.BlockSpec(memory_space=pl.ANY),
                pl.BlockSpec((TM_FN, TOP_K), row),
                pl.BlockSpec((TM_FN, D_MODEL), row),
                pl.BlockSpec((TM_FN, D_MODEL), row),
                pl.BlockSpec((1, D_MODEL), const),
                pl.BlockSpec((1, D_MODEL), const),
            ],
            out_specs=pl.BlockSpec((TM_FN, D_MODEL), row),
            scratch_shapes=[pltpu.VMEM((2, TOP_K, TM_FN * ROW_TILE, LANES), U32), pltpu.SemaphoreType.DMA((2,))],
        ),
        out_shape=jax.ShapeDtypeStruct((t, D_MODEL), F32),
        compiler_params=_cparams(("arbitrary",)),
        name="combine_ln2",
    )(dest_flat, ys, w_tok, h1, shared, g, b)


TM_INB = 512
TN_INB = W_B // 2


def _pad_lanes(v):
    return jnp.pad(v, (0, LANES - v.shape[0]))[None, :].astype(F32)


def _layer(h2d, mem2d, batch, seq, w_in_all, layer, conv_w, conv_b, dt_bias, a_log, d_skip, ssd_norm_g, swa_sinks,
           rel_bias, w_mem_kv, w_ssd_o, w_swa_o, w_xa_o, w_out, ln1_g, ln1_b, router_w, router_bias, w1, w3, w2, ws1,
           ws3, ws2, ln2_g, ln2_b, alpha):
    t = batch * seq
    w_t = jnp.swapaxes(w_in_all, 1, 2)
    wdt = jnp.pad(w_t[layer, W_A:W_A + SSD_HEADS, :].T, ((0, 0), (0, LANES - SSD_HEADS))).astype(BF16)
    pa, dt_raw, xb = _inproj(h2d, _wcast(w_t, layer, 0, W_A, "wcast_a"), wdt)
    pb = _mm(xb, _wcast(w_t, layer, W_A + SSD_HEADS, W_B, "wcast_b"), BF16, TM_INB, TN_INB, "inproj_b",
             vmem_mb=VMEM_MB_LARGE)

    y_ssd = _ssd(
        pa, dt_raw,
        conv_w[:, :SSD_D_INNER], conv_b[None, :SSD_D_INNER], conv_w[:, SSD_D_INNER:], conv_b[None, SSD_D_INNER:],
        _pad_lanes(dt_bias), _pad_lanes(a_log),
        jnp.repeat(d_skip, SSD_HEAD_DIM)[None, :], ssd_norm_g[None, :], batch, seq)

    y_swa = _swa(pb, swa_sinks.astype(F32), _swa_bias_table(rel_bias), batch, seq)

    memkv = _mm(mem2d, w_mem_kv, BF16, batch * MEM_LEN, 1024, "mem_kv")
    y_xa = _xa(pb, memkv, batch, seq)

    merged = _merge(y_ssd, y_swa, y_xa, w_ssd_o, w_swa_o, w_xa_o, pb)
    h1, hp = _out_ln(merged, w_out.astype(BF16), h2d, ln1_g[None, :], ln1_b[None, :], alpha)

    rwt = router_w.T
    r_hi = rwt.astype(BF16)
    r_lo = (rwt - r_hi.astype(F32)).astype(BF16)
    ti = jnp.arange(TM_RT)
    su = (ti[:, None] < ti[None, :]).astype(BF16)
    idx_t, w_t, rank_t, cnt = _router(h1, r_hi, r_lo, router_bias[:, None].astype(F32), su)

    counts = cnt[:, 0].astype(I32)
    padded = (counts + MOE_BLK - 1) // MOE_BLK * MOE_BLK
    pend = jnp.cumsum(padded)
    pstart = pend - padded
    n_blocks = t * TOP_K // MOE_BLK + N_EXPERTS
    blk_lo = jnp.arange(n_blocks, dtype=I32) * MOE_BLK
    block_e = jnp.minimum(jnp.sum((pend[None, :] <= blk_lo[:, None]).astype(I32), axis=1), N_EXPERTS - 1)
    of_e = block_e[:, None] == jnp.arange(N_EXPERTS, dtype=I32)[None, :]
    valid_end = jnp.sum(jnp.where(of_e, (pstart + counts)[None, :], 0), axis=1)
    block_nv = jnp.clip(valid_end - blk_lo, 0, MOE_BLK).astype(I32)
    eids = jnp.arange(N_EXPERTS, dtype=I32)
    later_active = (eids[None, :] > eids[:, None]) & (padded > 0)[None, :]
    next_active = jnp.min(jnp.where(later_active, eids[None, :], N_EXPERTS), axis=1)
    next_active = jnp.where(next_active >= N_EXPERTS, -1, next_active)
    seg_start = jnp.sum(jnp.where(of_e, pstart[None, :], 0), axis=1)
    block_first = ((blk_lo == seg_start) & (block_nv > 0)).astype(I32)
    block_next = jnp.sum(jnp.where(of_e, next_active[None, :], 0), axis=1).astype(I32)

    dest_flat = _dest(idx_t, rank_t, pstart.astype(F32)[:, None]).reshape(-1)
    n_slots = n_blocks * MOE_BLK
    pad_info = jnp.concatenate([pstart + counts, padded - counts, pend[-1:],
                                (n_slots - pend[-1:]) // MOE_BLK]).astype(I32)
    x_sorted, shared = _dispatch(dest_flat, pad_info, hp, h1, ws1.astype(BF16), ws3.astype(BF16),
                                 ws2.astype(BF16), n_slots)
    ys = _experts(block_e, block_nv, block_first, block_next,
                  x_sorted.reshape(n_slots * ROW_TILE, LANES), w1, w3, w2)
    return _final(dest_flat, ys.reshape(n_slots, ROW_TILE, LANES), w_t.T, h1, shared, ln2_g[None, :],
                  ln2_b[None, :], alpha)


def kernel(x, mem, w_in, conv_w, conv_b, dt_bias, a_log, d_skip, ssd_norm_g, swa_sinks, rel_bias, w_mem_kv, w_ssd_o, w_swa_o, w_xa_o, w_out, ln1_g, ln1_b, router_w, router_bias, w1, w3, w2, ws1, ws3, ws2, ln2_g, ln2_b):
    batch, seq, d = x.shape
    depth = w_in.shape[0]
    alpha = (2.0 * depth) ** 0.25
    h = x.reshape(batch * seq, d)
    mem2d = mem.reshape(batch * MEM_LEN, d)
    for i in range(depth):
        h = _layer(h, mem2d, batch, seq, w_in, i, conv_w[i], conv_b[i], dt_bias[i], a_log[i], d_skip[i],
                   ssd_norm_g[i], swa_sinks[i], rel_bias, w_mem_kv[i], w_ssd_o[i], w_swa_o[i], w_xa_o[i], w_out[i],
                   ln1_g[i], ln1_b[i], router_w[i], router_bias[i], w1[i], w3[i], w2[i], ws1[i], ws3[i], ws2[i],
                   ln2_g[i], ln2_b[i], alpha)
    return h.reshape(batch, seq, d)
```

```python
import functools
import math

import jax
import jax.numpy as jnp
from jax import lax
from jax.experimental import pallas as pl
from jax.experimental.pallas import tpu as pltpu

F32 = jnp.float32
BF16 = jnp.bfloat16
I32 = jnp.int32
U32 = jnp.uint32

D_MODEL = 2048
MEM_LEN = 256
SSD_D_INNER = D_MODEL
SSD_HEAD_DIM = 64
SSD_HEADS = 32
SSD_GROUPS = 4
SSD_STATE = 128
SSD_CONV = 4
SSD_CHUNK = 128
SWA_HEADS = 16
SWA_KV_HEADS = 4
SWA_HEAD_DIM = 64
SWA_WINDOW = 128
SWA_BLOCK = 128
REL_BUCKETS = 32
REL_MAX_DIST = 128
XA_HEADS = 4
XA_HEAD_DIM = 256
N_BRANCH = 3
N_EXPERTS = 64
TOP_K = 8
N_EXPERT_GROUPS = 8
TOPK_GROUPS = 4
EXPERT_DIM = 512
SHARED_DIM = 512
ROUTED_SCALE = 2.5
LN_EPS = 1e-5
RMS_EPS = 1e-5

SWA_Q_DIM = SWA_HEADS * SWA_HEAD_DIM
SWA_KV_DIM = SWA_KV_HEADS * SWA_HEAD_DIM
XA_DIM = XA_HEADS * XA_HEAD_DIM
BC_DIM = 2 * SSD_GROUPS * SSD_STATE

W_A = SSD_D_INNER + SSD_D_INNER + BC_DIM
OFFA_Z = 0
OFFA_XS = OFFA_Z + SSD_D_INNER
OFFA_BC = OFFA_XS + SSD_D_INNER
W_B = SWA_Q_DIM + 2 * SWA_KV_DIM + XA_DIM + N_BRANCH * D_MODEL
OFFB_Q = 0
OFFB_K = OFFB_Q + SWA_Q_DIM
OFFB_V = OFFB_K + SWA_KV_DIM
OFFB_QX = OFFB_V + SWA_KV_DIM
OFFB_GATES = OFFB_QX + XA_DIM

LANES = 128
MOE_BLK = 256
NEG_INF = float("-inf")


VMEM_MB = 48
VMEM_MB_LARGE = 56
VMEM_MB_EXPERTS = 58


def _cparams(sem, vmem_mb=VMEM_MB):
    return pltpu.CompilerParams(dimension_semantics=sem, vmem_limit_bytes=vmem_mb * 1024 * 1024)


def _dot(a, b):
    return jnp.dot(a, b, preferred_element_type=F32)


def _dot_nt(a, b):
    return lax.dot_general(a, b, (((1,), (1,)), ((), ())), preferred_element_type=F32)


def _sigmoid(x):
    return 1.0 / (1.0 + jnp.exp(-x))


def _silu(x):
    return x * _sigmoid(x)


TM_IN = 512
TN_IN = W_A // 2


def _inproj_kernel(x_ref, w_ref, wdt_ref, p_ref, dt_ref, xb_ref):
    @pl.when(pl.program_id(1) == 0)
    def _():
        xb = x_ref[...].astype(BF16)
        xb_ref[...] = xb
        dt_ref[...] = _dot(xb, wdt_ref[...])

    p_ref[...] = _dot(xb_ref[...], w_ref[...]).astype(BF16)


def _inproj(x2d, w_in, wdt):
    t = x2d.shape[0]
    return pl.pallas_call(
        _inproj_kernel,
        grid=(t // TM_IN, W_A // TN_IN),
        in_specs=[
            pl.BlockSpec((TM_IN, D_MODEL), lambda i, j: (i, 0)),
            pl.BlockSpec((D_MODEL, TN_IN), lambda i, j: (0, j)),
            pl.BlockSpec((D_MODEL, LANES), lambda i, j: (0, 0)),
        ],
        out_specs=[
            pl.BlockSpec((TM_IN, TN_IN), lambda i, j: (i, j)),
            pl.BlockSpec((TM_IN, LANES), lambda i, j: (i, 0)),
            pl.BlockSpec((TM_IN, D_MODEL), lambda i, j: (i, 0)),
        ],
        out_shape=[jax.ShapeDtypeStruct((t, W_A), BF16), jax.ShapeDtypeStruct((t, LANES), F32),
                   jax.ShapeDtypeStruct((t, D_MODEL), BF16)],
        compiler_params=_cparams(("parallel", "arbitrary"), vmem_mb=VMEM_MB_LARGE),
        name="inproj",
    )(x2d, w_in, wdt)


TN_WC = 512
ROW_ALIGN = 32


def _wcast_kernel(a_ref, o_ref):
    o_ref[...] = a_ref[0].T.astype(BF16)


def _wcast(w_t, layer, start, width, name):
    k = w_t.shape[2]
    return pl.pallas_call(
        _wcast_kernel,
        grid=(width // TN_WC,),
        in_specs=[pl.BlockSpec((pl.Element(1), pl.Element(TN_WC), pl.Element(k)),
                               lambda j: (layer, pl.multiple_of(start + j * TN_WC, ROW_ALIGN), 0))],
        out_specs=pl.BlockSpec((k, TN_WC), lambda j: (0, j)),
        out_shape=jax.ShapeDtypeStruct((k, width), BF16),
        compiler_params=_cparams(("parallel",)),
        name=name,
    )(w_t)


def _mm_kernel(a_ref, b_ref, o_ref):
    o_ref[...] = _dot(a_ref[...].astype(BF16), b_ref[...].astype(BF16)).astype(o_ref.dtype)


def _mm(a, b, out_dtype, tm, tn, name, vmem_mb=VMEM_MB):
    m, k = a.shape
    n = b.shape[1]
    return pl.pallas_call(
        _mm_kernel,
        grid=(m // tm, n // tn),
        in_specs=[pl.BlockSpec((tm, k), lambda i, j: (i, 0)), pl.BlockSpec((k, tn), lambda i, j: (0, j))],
        out_specs=pl.BlockSpec((tm, tn), lambda i, j: (i, j)),
        out_shape=jax.ShapeDtypeStruct((m, n), out_dtype),
        compiler_params=_cparams(("parallel", "parallel"), vmem_mb=vmem_mb),
        name=name,
    )(a, b)


Q = SSD_CHUNK


def _conv_shift_matrix():
    t = jnp.arange(Q)[:, None]
    j = jnp.arange(2 * Q)[None, :]
    blocks = []
    for k in range(SSD_CONV - 1):
        src = t - (SSD_CONV - 1 - k)
        blocks.append(j == jnp.where(src >= 0, src, 2 * Q + src))
    return jnp.concatenate(blocks, axis=0).astype(BF16)


def _ssd_kernel(xs_ref, bc_ref, z_ref, dtr_ref, shift_ref, cwx_ref, cbx_ref, cwb_ref, cbb_ref, dtb_ref, alog_ref,
                dsk_ref, ng_ref, o_ref, hx_ref, hb_ref, st_ref, cumt_ref, dtt_ref, wt_ref, y_ref):
    c = pl.program_id(1)

    @pl.when(c == 0)
    def _():
        hx_ref[...] = jnp.zeros_like(hx_ref)
        hb_ref[...] = jnp.zeros_like(hb_ref)
        st_ref[...] = jnp.zeros_like(st_ref)

    def conv_silu(in_ref, prev_ref, w_ref, b_ref):
        cur = in_ref[...]
        ext = jnp.concatenate([cur, prev_ref[...]], axis=0)
        shifted = _dot(shift_ref[...], ext)
        acc = b_ref[...] + w_ref[SSD_CONV - 1:SSD_CONV, :] * cur.astype(F32)
        for k in range(SSD_CONV - 1):
            acc = acc + w_ref[k:k + 1, :] * shifted[k * Q:(k + 1) * Q, :]
        prev_ref[...] = cur
        return _silu(acc)

    xs_c = conv_silu(xs_ref, hx_ref, cwx_ref, cbx_ref)
    bc_c = conv_silu(bc_ref, hb_ref, cwb_ref, cbb_ref)

    lane = lax.broadcasted_iota(I32, (Q, LANES), 1)
    row = lax.broadcasted_iota(I32, (Q, LANES), 0)
    tril = row >= lane
    lo_mask = lane < SSD_HEAD_DIM

    dtv = dtr_ref[...] + dtb_ref[...]
    dt = jnp.maximum(dtv, 0.0) + jnp.log1p(jnp.exp(-jnp.abs(dtv)))
    a = jnp.where(lane[0:1, :] < SSD_HEADS, -jnp.exp(alog_ref[...]), 0.0)
    la = dt * a
    tri = jnp.where(tril, 1.0, 0.0).astype(BF16)
    p1 = la.astype(BF16)
    r1 = la - p1.astype(F32)
    p2 = r1.astype(BF16)
    p3 = (r1 - p2.astype(F32)).astype(BF16)
    cum = _dot(tri, p1) + _dot(tri, p2) + _dot(tri, p3)
    ecum = jnp.exp(cum)
    cum_t = cum.T
    dt_t = dt.T
    last = cum_t[:, Q - 1:Q]
    cumt_ref[...] = cum_t
    dtt_ref[...] = dt_t
    wt_ref[...] = jnp.exp(last - cum_t) * dt_t
    chunk_decay = jnp.exp(last)

    def head_parts(h, cb, cm, bm_t):
        col = jnp.broadcast_to(cum[:, h:h + 1], (Q, Q))
        rw = cumt_ref[h:h + 1, :]
        dec = jnp.exp(jnp.where(tril, col - rw, NEG_INF))
        m = cb * (dec * dtt_ref[h:h + 1, :])
        ce = cm * jnp.broadcast_to(ecum[:, h:h + 1], (Q, Q))
        sl = bm_t * wt_ref[h:h + 1, :]
        return m.astype(BF16), ce.astype(BF16), sl.astype(BF16)

    for g in range(SSD_GROUPS):
        bm = bc_c[:, g * SSD_STATE:(g + 1) * SSD_STATE]
        cm = bc_c[:, BC_DIM // 2 + g * SSD_STATE:BC_DIM // 2 + (g + 1) * SSD_STATE]
        cb = _dot_nt(cm.astype(BF16), bm.astype(BF16))
        bm_t = bm.T
        for qd in range(SSD_HEADS // SSD_GROUPS // 2):
            hp = g * (SSD_HEADS // SSD_GROUPS // 2) + qd
            ha, hb = 2 * hp, 2 * hp + 1
            sl_ = slice(hp * LANES, (hp + 1) * LANES)
            xs_pair = xs_c[:, sl_]
            x_lo = jnp.where(lo_mask, xs_pair, 0.0).astype(BF16)
            x_hi = jnp.where(lo_mask, 0.0, xs_pair).astype(BF16)
            h_prev = st_ref[:, sl_]
            h_lo = jnp.where(lo_mask, h_prev, 0.0).astype(BF16)
            h_hi = jnp.where(lo_mask, 0.0, h_prev).astype(BF16)
            m_a, ce_a, sl_a = head_parts(ha, cb, cm, bm_t)
            m_b, ce_b, sl_b = head_parts(hb, cb, cm, bm_t)
            lhs = jnp.concatenate([m_a, ce_a, m_b, ce_b], axis=1)
            rhs = jnp.concatenate([x_lo, h_lo, x_hi, h_hi], axis=0)
            y_ref[:, sl_] = _dot(lhs, rhs)
            new = _dot(jnp.concatenate([sl_a, sl_b], axis=1), jnp.concatenate([x_lo, x_hi], axis=0))
            cd = jnp.where(lo_mask[0:1, :], chunk_decay[ha:ha + 1, :], chunk_decay[hb:hb + 1, :])
            st_ref[:, sl_] = h_prev * cd + new

    y = y_ref[...] + xs_c * dsk_ref[...]
    y = y * _silu(z_ref[...].astype(F32))
    gw = SSD_D_INNER // SSD_GROUPS
    for g in range(SSD_GROUPS):
        yg = y[:, g * gw:(g + 1) * gw]
        ms = jnp.mean(yg * yg, axis=-1, keepdims=True)
        o_ref[:, g * gw:(g + 1) * gw] = ((yg * lax.rsqrt(ms + RMS_EPS)) * ng_ref[:, g * gw:(g + 1) * gw]).astype(BF16)


def _ssd(p, dt_raw, cwx, cbx, cwb, cbb, dtb, alog, dsk, ng, batch, seq):
    nc = seq // Q
    t = batch * seq
    rowblk = lambda b, c: b * nc + c
    const = lambda b, c: (0, 0)
    return pl.pallas_call(
        _ssd_kernel,
        grid=(batch, nc),
        in_specs=[
            pl.BlockSpec((Q, SSD_D_INNER), lambda b, c: (rowblk(b, c), OFFA_XS // SSD_D_INNER)),
            pl.BlockSpec((Q, BC_DIM), lambda b, c: (rowblk(b, c), OFFA_BC // BC_DIM)),
            pl.BlockSpec((Q, SSD_D_INNER), lambda b, c: (rowblk(b, c), OFFA_Z // SSD_D_INNER)),
            pl.BlockSpec((Q, LANES), lambda b, c: (rowblk(b, c), 0)),
            pl.BlockSpec(((SSD_CONV - 1) * Q, 2 * Q), const),
            pl.BlockSpec((SSD_CONV, SSD_D_INNER), const),
            pl.BlockSpec((1, SSD_D_INNER), const),
            pl.BlockSpec((SSD_CONV, BC_DIM), const),
            pl.BlockSpec((1, BC_DIM), const),
            pl.BlockSpec((1, LANES), const),
            pl.BlockSpec((1, LANES), const),
            pl.BlockSpec((1, SSD_D_INNER), const),
            pl.BlockSpec((1, SSD_D_INNER), const),
        ],
        out_specs=pl.BlockSpec((Q, SSD_D_INNER), lambda b, c: (rowblk(b, c), 0)),
        out_shape=jax.ShapeDtypeStruct((t, SSD_D_INNER), BF16),
        scratch_shapes=[
            pltpu.VMEM((Q, SSD_D_INNER), BF16),
            pltpu.VMEM((Q, BC_DIM), BF16),
            pltpu.VMEM((SSD_STATE, SSD_D_INNER), F32),
            pltpu.VMEM((LANES, Q), F32),
            pltpu.VMEM((LANES, Q), F32),
            pltpu.VMEM((LANES, Q), F32),
            pltpu.VMEM((Q, SSD_D_INNER), F32),
        ],
        compiler_params=_cparams(("parallel", "arbitrary")),
        name="ssd",
    )(p, p, p, dt_raw, _conv_shift_matrix(), cwx, cbx, cwb, cbb, dtb, alog, dsk, ng)


N_PAIRS = SWA_HEADS // 2


def _swa_kernel(sink_ref, q_ref, kp_ref, kc_ref, vp_ref, vc_ref, bias_ref, o_ref):
    blk = SWA_BLOCK
    kband = jnp.concatenate([kp_ref[...], kc_ref[...]], axis=0)
    vband = jnp.concatenate([vp_ref[...], vc_ref[...]], axis=0)

    lane = lax.broadcasted_iota(I32, (1, LANES), 1)
    lo_f = jnp.where(lane < SWA_HEAD_DIM, 1.0, 0.0)
    lo_m = lo_f.astype(BF16)
    hi_m = (1.0 - lo_f).astype(BF16)

    def head_twice(band, kv):
        tile = band[:, (kv // 2) * LANES:(kv // 2 + 1) * LANES]
        swapped = jnp.concatenate([tile[:, SWA_HEAD_DIM:], tile[:, :SWA_HEAD_DIM]], axis=1)
        return tile * lo_m + swapped * hi_m if kv % 2 == 0 else swapped * lo_m + tile * hi_m

    k_twice = [head_twice(kband, kv) for kv in range(SWA_KV_HEADS)]
    v_twice = [head_twice(vband, kv) for kv in range(SWA_KV_HEADS)]
    rowi = lax.broadcasted_iota(I32, (2 * blk, 1), 0)
    scale = SWA_HEAD_DIM ** -0.5
    assert math.frexp(scale)[0] == 0.5
    lo_q = (lo_f * scale).astype(BF16)
    hi_q = ((1.0 - lo_f) * scale).astype(BF16)
    for p in range(N_PAIRS):
        kv = p // (SWA_HEADS // SWA_KV_HEADS // 2)
        qp = q_ref[:, p * LANES:(p + 1) * LANES]
        qs = jnp.concatenate([qp * lo_q, qp * hi_q], axis=0)
        s = _dot_nt(qs, k_twice[kv]) + bias_ref[0, p]
        sink = jnp.where(rowi < blk, sink_ref[2 * p], sink_ref[2 * p + 1])
        m = jnp.maximum(jnp.max(s, axis=-1, keepdims=True), sink)
        e = jnp.exp(s - m)
        den = jnp.sum(e, axis=-1, keepdims=True) + jnp.exp(sink - m)
        pr = (e * (1.0 / den)).astype(BF16)
        vd = v_twice[kv]
        lhs = jnp.concatenate([pr[:blk], pr[blk:]], axis=1)
        rhs = jnp.concatenate([vd * lo_m, vd * hi_m], axis=0)
        o_ref[:, p * LANES:(p + 1) * LANES] = _dot(lhs, rhs).astype(BF16)


def _swa(p, sinks, bias, batch, seq):
    nb = seq // SWA_BLOCK
    t = batch * seq
    kw = SWA_KV_DIM
    cur = lambda b, n, s: b * nb + n
    prev = lambda b, n, s: b * nb + jnp.maximum(n - 1, 0)
    return pl.pallas_call(
        _swa_kernel,
        grid_spec=pltpu.PrefetchScalarGridSpec(
            num_scalar_prefetch=1,
            grid=(batch, nb),
            in_specs=[
                pl.BlockSpec((SWA_BLOCK, SWA_Q_DIM), lambda b, n, s: (cur(b, n, s), OFFB_Q // SWA_Q_DIM)),
                pl.BlockSpec((SWA_BLOCK, kw), lambda b, n, s: (prev(b, n, s), OFFB_K // kw)),
                pl.BlockSpec((SWA_BLOCK, kw), lambda b, n, s: (cur(b, n, s), OFFB_K // kw)),
                pl.BlockSpec((SWA_BLOCK, kw), lambda b, n, s: (prev(b, n, s), OFFB_V // kw)),
                pl.BlockSpec((SWA_BLOCK, kw), lambda b, n, s: (cur(b, n, s), OFFB_V // kw)),
                pl.BlockSpec((1, N_PAIRS, 2 * SWA_BLOCK, 2 * SWA_BLOCK),
                             lambda b, n, s: (jnp.minimum(n, 1), 0, 0, 0)),
            ],
            out_specs=pl.BlockSpec((SWA_BLOCK, SWA_Q_DIM), lambda b, n, s: (cur(b, n, s), 0)),
        ),
        out_shape=jax.ShapeDtypeStruct((t, SWA_Q_DIM), BF16),
        compiler_params=_cparams(("parallel", "arbitrary")),
        name="swa",
    )(sinks, p, p, p, p, p, bias)


def _t5_causal_bucket(dist):
    max_exact = REL_BUCKETS // 2
    large = max_exact + (jnp.log(jnp.maximum(dist, 1).astype(F32) / max_exact)
                         / math.log(REL_MAX_DIST / max_exact) * (REL_BUCKETS - max_exact)).astype(I32)
    large = jnp.minimum(large, REL_BUCKETS - 1)
    return jnp.where(dist < max_exact, dist, large)


def _swa_bias_table(rel_bias):
    qi = jnp.arange(SWA_BLOCK)[:, None]
    kj = jnp.arange(2 * SWA_BLOCK)[None, :]
    dist = qi + SWA_BLOCK - kj
    in_window = (dist >= 0) & (dist < SWA_WINDOW)
    bucket = _t5_causal_bucket(jnp.maximum(dist, 0))
    onehot = (bucket[:, :, None] == jnp.arange(REL_BUCKETS)[None, None, :]).astype(F32)
    bias = jnp.einsum('ijb,bh->hij', onehot, rel_bias.astype(F32), precision=lax.Precision.HIGHEST)
    rest = jnp.where(in_window[None, :, :], bias, NEG_INF)
    first = jnp.where((kj >= SWA_BLOCK)[None, :, :], rest, NEG_INF)
    return jnp.stack([first, rest]).reshape(2, N_PAIRS, 2 * SWA_BLOCK, 2 * SWA_BLOCK)


TL_XA = 512


XA_HALF = XA_DIM // 2


def _xa_kernel(qa_ref, qb_ref, mk_ref, mv_ref, o_ref):
    for h in range(XA_HEADS):
        q_ref = qa_ref if h < XA_HEADS // 2 else qb_ref
        qc = (h % (XA_HEADS // 2)) * XA_HEAD_DIM
        cols = slice(h * XA_HEAD_DIM, (h + 1) * XA_HEAD_DIM)
        s = _dot_nt(q_ref[:, qc:qc + XA_HEAD_DIM], mk_ref[:, cols]) * (XA_HEAD_DIM ** -0.5)
        m = jnp.max(s, axis=-1, keepdims=True)
        e = jnp.exp(s - m)
        pr = e * (1.0 / jnp.sum(e, axis=-1, keepdims=True))
        o_ref[:, cols] = _dot(pr.astype(BF16), mv_ref[:, cols]).astype(BF16)


def _xa(p, memkv, batch, seq):
    nl = seq // TL_XA
    t = batch * seq
    return pl.pallas_call(
        _xa_kernel,
        grid=(batch, nl),
        in_specs=[
            pl.BlockSpec((TL_XA, XA_HALF), lambda b, l: (b * nl + l, OFFB_QX // XA_HALF)),
            pl.BlockSpec((TL_XA, XA_HALF), lambda b, l: (b * nl + l, OFFB_QX // XA_HALF + 1)),
            pl.BlockSpec((MEM_LEN, XA_DIM), lambda b, l: (b, 0)),
            pl.BlockSpec((MEM_LEN, XA_DIM), lambda b, l: (b, 1)),
        ],
        out_specs=pl.BlockSpec((TL_XA, XA_DIM), lambda b, l: (b * nl + l, 0)),
        out_shape=jax.ShapeDtypeStruct((t, XA_DIM), BF16),
        compiler_params=_cparams(("parallel", "parallel")),
        name="xattn",
    )(p, p, memkv, memkv)


TM_MG = 1024
TN_MG = 512


N_SUB_MG = 2


def _merge_kernel(ys_ref, yw_ref, yx_ref, ws_ref, ww_ref, wx_ref, g0_ref, g1_ref, g2_ref, o_ref):
    sub = TM_MG // N_SUB_MG
    ws = ws_ref[...].astype(BF16)
    ww = ww_ref[...].astype(BF16)
    wx = wx_ref[...].astype(BF16)
    for s in range(N_SUB_MG):
        rows = slice(s * sub, (s + 1) * sub)
        a = _dot(ys_ref[rows, :], ws)
        b = _dot(yw_ref[rows, :], ww)
        c = _dot(yx_ref[rows, :], wx)
        g0 = _sigmoid(g0_ref[rows, :].astype(F32))
        g1 = _sigmoid(g1_ref[rows, :].astype(F32))
        g2 = _sigmoid(g2_ref[rows, :].astype(F32))
        o_ref[rows, :] = (g0 * a + g1 * b + g2 * c).astype(BF16)


def _merge(y_ssd, y_swa, y_xa, w_ssd_o, w_swa_o, w_xa_o, p):
    t = y_ssd.shape[0]
    gpb = D_MODEL // TN_MG
    gate = lambda k: pl.BlockSpec((TM_MG, TN_MG), lambda i, j: (i, OFFB_GATES // TN_MG + k * gpb + j))
    return pl.pallas_call(
        _merge_kernel,
        grid=(t // TM_MG, D_MODEL // TN_MG),
        in_specs=[
            pl.BlockSpec((TM_MG, SSD_D_INNER), lambda i, j: (i, 0)),
            pl.BlockSpec((TM_MG, SWA_Q_DIM), lambda i, j: (i, 0)),
            pl.BlockSpec((TM_MG, XA_DIM), lambda i, j: (i, 0)),
            pl.BlockSpec((SSD_D_INNER, TN_MG), lambda i, j: (0, j)),
            pl.BlockSpec((SWA_Q_DIM, TN_MG), lambda i, j: (0, j)),
            pl.BlockSpec((XA_DIM, TN_MG), lambda i, j: (0, j)),
            gate(0), gate(1), gate(2),
        ],
        out_specs=pl.BlockSpec((TM_MG, TN_MG), lambda i, j: (i, j)),
        out_shape=jax.ShapeDtypeStruct((t, D_MODEL), BF16),
        compiler_params=_cparams(("parallel", "parallel")),
        name="merge",
    )(y_ssd, y_swa, y_xa, w_ssd_o, w_swa_o, w_xa_o, p, p, p)


TM_LN = 512


def _layer_norm(v, g, b):
    mu = jnp.mean(v, axis=-1, keepdims=True)
    d = v - mu
    var = jnp.mean(d * d, axis=-1, keepdims=True)
    return (d * lax.rsqrt(var + LN_EPS)) * g + b


def _pack_halves(v):
    n = v.shape[1] // 2
    lo = lax.bitcast_convert_type(v[:, :n].astype(BF16).astype(F32), U32) >> 16
    hi = lax.bitcast_convert_type(v[:, n:].astype(BF16).astype(F32), U32) & jnp.uint32(0xFFFF0000)
    return hi | lo


def _unpack_halves(w):
    lo = lax.bitcast_convert_type(w << 16, F32)
    hi = lax.bitcast_convert_type(w & jnp.uint32(0xFFFF0000), F32)
    return lo, hi


ROW_WORDS = D_MODEL // 2
ROW_TILE = ROW_WORDS // LANES


def _store_row_tiled(ref, words):
    m = words.shape[0]
    for c in range(ROW_TILE):
        ref[pl.ds(c, m, stride=ROW_TILE), :] = words[:, c * LANES:(c + 1) * LANES]


def _load_row_tiled(ref, m):
    return jnp.concatenate([ref[pl.ds(c, m, stride=ROW_TILE), :] for c in range(ROW_TILE)], axis=1)


N_SUB_LN = 4


def _out_ln_kernel(alpha, m_ref, w_ref, x_ref, g_ref, b_ref, h_ref, hp_ref):
    sub = TM_LN // N_SUB_LN
    for c in range(N_SUB_LN):
        rows = slice(c * sub, (c + 1) * sub)
        mix = _dot(m_ref[rows, :], w_ref[...])
        hn = _layer_norm(alpha * x_ref[rows, :] + mix, g_ref[...], b_ref[...])
        h_ref[rows, :] = hn
        _store_row_tiled(hp_ref.at[pl.ds(c * sub * ROW_TILE, sub * ROW_TILE), :], _pack_halves(hn))


def _out_ln(merged, w_out, x2d, g, b, alpha):
    t = x2d.shape[0]
    row = lambda i: (i, 0)
    const = lambda i: (0, 0)
    return pl.pallas_call(
        functools.partial(_out_ln_kernel, alpha),
        grid=(t // TM_LN,),
        in_specs=[
            pl.BlockSpec((TM_LN, D_MODEL), row),
            pl.BlockSpec((D_MODEL, D_MODEL), const),
            pl.BlockSpec((TM_LN, D_MODEL), row),
            pl.BlockSpec((1, D_MODEL), const),
            pl.BlockSpec((1, D_MODEL), const),
        ],
        out_specs=[pl.BlockSpec((TM_LN, D_MODEL), row), pl.BlockSpec((TM_LN * ROW_TILE, LANES), row)],
        out_shape=[jax.ShapeDtypeStruct((t, D_MODEL), F32), jax.ShapeDtypeStruct((t * ROW_TILE, LANES), U32)],
        compiler_params=_cparams(("parallel",)),
        name="out_ln1",
    )(merged, w_out, x2d, g, b)


TM_RT = 512
EPG = N_EXPERTS // N_EXPERT_GROUPS


def _router_kernel(h_ref, whi_ref, wlo_ref, bias_ref, su_ref, idx_ref, w_ref, rank_ref, cnt_ref, run_ref):
    @pl.when(pl.program_id(0) == 0)
    def _():
        run_ref[...] = jnp.zeros_like(run_ref)

    h = h_ref[...]
    h_hi = h.astype(BF16)
    h_lo = (h - h_hi.astype(F32)).astype(BF16)
    whi = whi_ref[...]
    logits = _dot_nt(whi, h_hi) + _dot_nt(whi, h_lo) + _dot_nt(wlo_ref[...], h_hi)
    scores = _sigmoid(logits)
    sel = scores + bias_ref[...]

    iota_m = lax.broadcasted_iota(I32, (EPG, TM_RT), 0).astype(F32)
    iota_g = lax.broadcasted_iota(I32, (N_EXPERT_GROUPS, TM_RT), 0).astype(F32)
    gs = []
    for g in range(N_EXPERT_GROUPS):
        v = sel[g * EPG:(g + 1) * EPG, :]
        m1 = jnp.max(v, axis=0, keepdims=True)
        i1 = jnp.min(jnp.where(v == m1, iota_m, float(EPG)), axis=0, keepdims=True)
        m2 = jnp.max(jnp.where(iota_m == i1, NEG_INF, v), axis=0, keepdims=True)
        gs.append(m1 + m2)
    gsc = jnp.concatenate(gs, axis=0)
    chosen = jnp.zeros((N_EXPERT_GROUPS, TM_RT), F32)
    for _ in range(TOPK_GROUPS):
        m = jnp.max(gsc, axis=0, keepdims=True)
        ig = jnp.min(jnp.where(gsc == m, iota_g, float(N_EXPERT_GROUPS)), axis=0, keepdims=True)
        hit = iota_g == ig
        chosen = jnp.where(hit, 1.0, chosen)
        gsc = jnp.where(hit, NEG_INF, gsc)
    gmask = jnp.concatenate(
        [jnp.broadcast_to(chosen[g:g + 1, :], (EPG, TM_RT)) for g in range(N_EXPERT_GROUPS)], axis=0)
    masked = jnp.where(gmask > 0.0, sel, NEG_INF)

    iota_e = lax.broadcasted_iota(I32, (N_EXPERTS, TM_RT), 0).astype(F32)
    hits, idxs, ws = [], [], []
    for _ in range(TOP_K):
        m = jnp.max(masked, axis=0, keepdims=True)
        ie = jnp.min(jnp.where(masked == m, iota_e, float(N_EXPERTS)), axis=0, keepdims=True)
        hit = iota_e == ie
        hits.append(hit)
        idxs.append(ie)
        ws.append(jnp.sum(jnp.where(hit, scores, 0.0), axis=0, keepdims=True))
        masked = jnp.where(hit, NEG_INF, masked)
    wsum = ws[0]
    for k in range(1, TOP_K):
        wsum = wsum + ws[k]
    inv = 1.0 / wsum
    idx_ref[...] = jnp.concatenate(idxs, axis=0).astype(I32)
    w_ref[...] = jnp.concatenate([w * inv * ROUTED_SCALE for w in ws], axis=0)

    assign = jnp.zeros((N_EXPERTS, TM_RT), F32)
    for hit in hits:
        assign = jnp.where(hit, 1.0, assign)
    pos = _dot(assign.astype(BF16), su_ref[...]) + run_ref[...]
    rank_ref[...] = jnp.concatenate(
        [jnp.sum(jnp.where(hit, pos, 0.0), axis=0, keepdims=True) for hit in hits], axis=0).astype(I32)
    run = run_ref[...] + jnp.sum(assign, axis=1, keepdims=True)
    run_ref[...] = run
    cnt_ref[...] = run


def _router(h1, r_hi, r_lo, rbias, su):
    t = h1.shape[0]
    tile = lambda i: (0, i)
    const = lambda i: (0, 0)
    return pl.pallas_call(
        _router_kernel,
        grid=(t // TM_RT,),
        in_specs=[
            pl.BlockSpec((TM_RT, D_MODEL), lambda i: (i, 0)),
            pl.BlockSpec((N_EXPERTS, D_MODEL), const),
            pl.BlockSpec((N_EXPERTS, D_MODEL), const),
            pl.BlockSpec((N_EXPERTS, 1), const),
            pl.BlockSpec((TM_RT, TM_RT), const),
        ],
        out_specs=[
            pl.BlockSpec((TOP_K, TM_RT), tile),
            pl.BlockSpec((TOP_K, TM_RT), tile),
            pl.BlockSpec((TOP_K, TM_RT), tile),
            pl.BlockSpec((N_EXPERTS, TM_RT), const),
        ],
        out_shape=[
            jax.ShapeDtypeStruct((TOP_K, t), I32),
            jax.ShapeDtypeStruct((TOP_K, t), F32),
            jax.ShapeDtypeStruct((TOP_K, t), I32),
            jax.ShapeDtypeStruct((N_EXPERTS, TM_RT), F32),
        ],
        scratch_shapes=[pltpu.VMEM((N_EXPERTS, TM_RT), F32)],
        compiler_params=_cparams(("arbitrary",)),
        name="router",
    )(h1, r_hi, r_lo, rbias, su)


TM_DS = 2048


def _dest_kernel(idx_ref, rank_ref, ps_ref, d_ref):
    iota_e = lax.broadcasted_iota(I32, (N_EXPERTS, TM_DS), 0)
    ps = ps_ref[...]
    rows = []
    for k in range(TOP_K):
        hit = iota_e == idx_ref[k:k + 1, :]
        rows.append(jnp.sum(jnp.where(hit, ps, 0.0), axis=0, keepdims=True))
    d_ref[...] = jnp.concatenate(rows, axis=0).astype(I32) + rank_ref[...]


def _dest(idx_t, rank_t, pstart):
    t = idx_t.shape[1]
    tile = lambda i: (0, i)
    return pl.pallas_call(
        _dest_kernel,
        grid=(t // TM_DS,),
        in_specs=[pl.BlockSpec((TOP_K, TM_DS), tile), pl.BlockSpec((TOP_K, TM_DS), tile),
                  pl.BlockSpec((N_EXPERTS, 1), lambda i: (0, 0))],
        out_specs=pl.BlockSpec((TOP_K, TM_DS), tile),
        out_shape=jax.ShapeDtypeStruct((TOP_K, t), I32),
        compiler_params=_cparams(("parallel",)),
        name="dest",
    )(idx_t, rank_t, pstart)


TM_DP = 1024


PAD_BITS = tuple(1 << i for i in reversed(range(MOE_BLK.bit_length() - 1)))


def _dispatch_kernel(dest_ref, pad_ref, hp_ref, h_ref, ws1_ref, ws3_ref, ws2_ref, xs_ref, sh_ref, zbuf, sem, zsem):
    t_total = pl.num_programs(0) * TM_DP
    base = pl.program_id(0) * TM_DP

    def pad_copies(act):
        def per_expert(e, carry):
            start = pad_ref[e]
            length = pad_ref[N_EXPERTS + e]
            for bit in PAD_BITS:
                @pl.when((length & bit) != 0)
                def _():
                    off = length - (length & (2 * bit - 1))
                    act(pltpu.make_async_copy(zbuf.at[pl.ds(0, bit)], xs_ref.at[pl.ds(start + off, bit)], zsem))
            return carry

        lax.fori_loop(0, N_EXPERTS, per_expert, 0)

        def per_block(i, carry):
            row = pad_ref[2 * N_EXPERTS] + i * MOE_BLK
            act(pltpu.make_async_copy(zbuf, xs_ref.at[pl.ds(row, MOE_BLK)], zsem))
            return carry

        lax.fori_loop(0, pad_ref[2 * N_EXPERTS + 1], per_block, 0)

    @pl.when(pl.program_id(0) == 0)
    def _():
        zbuf[...] = jnp.zeros_like(zbuf)
        pad_copies(lambda cp: cp.start())

    def issue(t, carry):
        src = hp_ref.at[pl.ds(pl.multiple_of(t * ROW_TILE, ROW_TILE), ROW_TILE), :]
        for k in range(TOP_K):
            slot = dest_ref[k * t_total + base + t]
            pltpu.make_async_copy(src, xs_ref.at[slot], sem).start(priority=k % 2)
        return carry

    lax.fori_loop(0, TM_DP, issue, 0)

    xb = h_ref[...].astype(BF16)
    hdn = _silu(_dot(xb, ws1_ref[...])) * _dot(xb, ws3_ref[...])
    sh_ref[...] = _dot(hdn.astype(BF16), ws2_ref[...])

    for k in range(TOP_K):
        pltpu.make_async_copy(hp_ref.reshape(TM_DP, ROW_TILE, LANES), xs_ref.at[pl.ds(0, TM_DP)], sem).wait()

    @pl.when(pl.program_id(0) == 0)
    def _():
        pad_copies(lambda cp: cp.wait())


def _dispatch(dest_flat, pad_info, hp, h1, ws1, ws3, ws2, n_slots):
    t = h1.shape[0]
    row = lambda i, *_: (i, 0)
    const = lambda i, *_: (0, 0)
    return pl.pallas_call(
        _dispatch_kernel,
        grid_spec=pltpu.PrefetchScalarGridSpec(
            num_scalar_prefetch=2,
            grid=(t // TM_DP,),
            in_specs=[
                pl.BlockSpec((TM_DP * ROW_TILE, LANES), row),
                pl.BlockSpec((TM_DP, D_MODEL), row),
                pl.BlockSpec((D_MODEL, SHARED_DIM), const, pipeline_mode=pl.Buffered(1)),
                pl.BlockSpec((D_MODEL, SHARED_DIM), const, pipeline_mode=pl.Buffered(1)),
                pl.BlockSpec((SHARED_DIM, D_MODEL), const, pipeline_mode=pl.Buffered(1)),
            ],
            out_specs=[pl.BlockSpec(memory_space=pl.ANY), pl.BlockSpec((TM_DP, D_MODEL), row)],
            scratch_shapes=[pltpu.VMEM((MOE_BLK, ROW_TILE, LANES), U32), pltpu.SemaphoreType.DMA,
                            pltpu.SemaphoreType.DMA],
        ),
        out_shape=[jax.ShapeDtypeStruct((n_slots, ROW_TILE, LANES), U32), jax.ShapeDtypeStruct((t, D_MODEL), F32)],
        compiler_params=_cparams(("arbitrary",), vmem_mb=VMEM_MB_LARGE),
        name="dispatch",
    )(dest_flat, pad_info, hp, h1, ws1, ws3, ws2)


N_WSLOTS = 2
BLK_PER_STEP = 2


def _expert_kernel(be_ref, nv_ref, first_ref, nxt_ref, x_ref, w1_ref, w3_ref, w2_ref, o_ref,
                   wf1, wf3, wf2, w1b, w3b, w2b, slot_ref, sems):
    step = pl.program_id(0)

    def weight_copies(e, s):
        return (pltpu.make_async_copy(w1_ref.at[e], wf1.at[s], sems.at[s, 0]),
                pltpu.make_async_copy(w3_ref.at[e], wf3.at[s], sems.at[s, 1]),
                pltpu.make_async_copy(w2_ref.at[e], wf2.at[s], sems.at[s, 2]))

    @pl.when(step == 0)
    def _():
        slot_ref[0] = 0
        for cp in weight_copies(be_ref[0], 0):
            cp.start()

    def switch_weights(b):
        @pl.when(first_ref[b] == 1)
        def _():
            s = slot_ref[0]
            for cp in weight_copies(be_ref[b], s):
                cp.wait()

            @pl.when(nxt_ref[b] >= 0)
            def _():
                for cp in weight_copies(nxt_ref[b], 1 - s):
                    cp.start(priority=1)

            w1b[...] = wf1[s].astype(BF16)
            w3b[...] = wf3[s].astype(BF16)
            w2b[...] = wf2[s].astype(BF16)
            slot_ref[0] = 1 - s

    def ffn(xq, oq, m):
        lo, hi = _unpack_halves(_load_row_tiled(xq, m))
        x = jnp.concatenate([lo.astype(BF16), hi.astype(BF16)], axis=1)
        hdn = _silu(_dot(x, w1b[...])) * _dot(x, w3b[...])
        _store_row_tiled(oq, _pack_halves(_dot(hdn.astype(BF16), w2b[...])))

    def one_block(b, xq, oq):
        nv = nv_ref[b]

        @pl.when(nv > 0)
        def _():
            ffn(xq, oq, MOE_BLK)

        @pl.when(nv <= 0)
        def _():
            oq[...] = jnp.zeros_like(oq)

    assert BLK_PER_STEP == 2
    blk_rows = MOE_BLK * ROW_TILE
    b0 = step * BLK_PER_STEP
    b1 = b0 + 1
    lo_rows = pl.ds(0, blk_rows)
    hi_rows = pl.ds(blk_rows, blk_rows)
    same = (first_ref[b1] == 0) & (nv_ref[b1] > 0)
    switch_weights(b0)

    @pl.when(same)
    def _():
        ffn(x_ref, o_ref, 2 * MOE_BLK)

    @pl.when(jnp.logical_not(same))
    def _():
        one_block(b0, x_ref.at[lo_rows, :], o_ref.at[lo_rows, :])
        switch_weights(b1)
        one_block(b1, x_ref.at[hi_rows, :], o_ref.at[hi_rows, :])


def _experts(block_e, block_nv, block_first, block_next, x_sorted, w1, w3, w2):
    n_rows = x_sorted.shape[0]
    step_rows = BLK_PER_STEP * MOE_BLK * ROW_TILE
    row = lambda b, *_: (b, 0)
    return pl.pallas_call(
        _expert_kernel,
        grid_spec=pltpu.PrefetchScalarGridSpec(
            num_scalar_prefetch=4,
            grid=(n_rows // step_rows,),
            in_specs=[
                pl.BlockSpec((step_rows, LANES), row),
                pl.BlockSpec(memory_space=pl.ANY),
                pl.BlockSpec(memory_space=pl.ANY),
                pl.BlockSpec(memory_space=pl.ANY),
            ],
            out_specs=pl.BlockSpec((step_rows, LANES), row),
            scratch_shapes=[
                pltpu.VMEM((N_WSLOTS, D_MODEL, EXPERT_DIM), F32),
                pltpu.VMEM((N_WSLOTS, D_MODEL, EXPERT_DIM), F32),
                pltpu.VMEM((N_WSLOTS, EXPERT_DIM, D_MODEL), F32),
                pltpu.VMEM((D_MODEL, EXPERT_DIM), BF16),
                pltpu.VMEM((D_MODEL, EXPERT_DIM), BF16),
                pltpu.VMEM((EXPERT_DIM, D_MODEL), BF16),
                pltpu.SMEM((1,), I32),
                pltpu.SemaphoreType.DMA((N_WSLOTS, 3)),
            ],
        ),
        out_shape=jax.ShapeDtypeStruct((n_rows, LANES), U32),
        compiler_params=_cparams(("arbitrary",), vmem_mb=VMEM_MB_EXPERTS),
        name="experts",
    )(block_e, block_nv, block_first, block_next, x_sorted, w1, w3, w2)


TM_FN = 256


CH_FN = 8


def _final_kernel(alpha, dest_ref, ys_ref, w_ref, h_ref, sh_ref, g_ref, b_ref, o_ref, buf, sems):
    i = pl.program_id(0)
    n_tiles = pl.num_programs(0)
    t_total = n_tiles * TM_FN
    slot = i % 2
    half = D_MODEL // 2

    def issue_token(tile, sl, c, tt):
        tok = c * CH_FN + tt
        row0 = pl.multiple_of(tok * ROW_TILE, ROW_TILE)
        for k in range(TOP_K):
            src = ys_ref.at[dest_ref[k * t_total + tile * TM_FN + tok]]
            pltpu.make_async_copy(src, buf.at[sl, k, pl.ds(row0, ROW_TILE), :], sems.at[sl]).start(priority=k % 2)

    def wait_tile(sl):
        for k in range(TOP_K):
            pltpu.make_async_copy(ys_ref.at[pl.ds(0, TM_FN)], buf.at[sl, k].reshape(TM_FN, ROW_TILE, LANES),
                                  sems.at[sl]).wait()

    @pl.when(i == 0)
    def _():
        def first(c, carry):
            for tt in range(CH_FN):
                issue_token(0, 0, c, tt)
            return carry
        lax.fori_loop(0, TM_FN // CH_FN, first, 0)

    nxt_tile = jnp.minimum(i + 1, n_tiles - 1)

    def combine_from(cur, nxt):
        wait_tile(cur)

        def body(c, carry):
            tok = pl.ds(pl.multiple_of(c * CH_FN, CH_FN), CH_FN)
            acc_lo = sh_ref[tok, :half]
            acc_hi = sh_ref[tok, half:]
            pre = alpha * h_ref[tok, :]
            wrow = w_ref[tok, :]
            for k in range(TOP_K):
                words = jnp.concatenate(
                    [buf[cur, k, pl.ds(c * (CH_FN * ROW_TILE) + r, CH_FN, stride=ROW_TILE), :]
                     for r in range(ROW_TILE)], axis=1)
                issue_token(nxt_tile, nxt, c, k)
                lo, hi = _unpack_halves(words)
                wk = wrow[:, k:k + 1]
                acc_lo = acc_lo + wk * lo
                acc_hi = acc_hi + wk * hi
            o_ref[tok, :] = pre + jnp.concatenate([acc_lo, acc_hi], axis=1)
            return carry

        lax.fori_loop(0, TM_FN // CH_FN, body, 0)

        @pl.when(i == n_tiles - 1)
        def _():
            wait_tile(nxt)

    for parity in range(2):
        @pl.when(slot == parity)
        def _():
            combine_from(parity, 1 - parity)

    o_ref[...] = _layer_norm(o_ref[...], g_ref[...], b_ref[...])


def _final(dest_flat, ys, w_tok, h1, shared, g, b, alpha):
    t = h1.shape[0]
    row = lambda i, d: (i, 0)
    const = lambda i, d: (0, 0)
    return pl.pallas_call(
        functools.partial(_final_kernel, alpha),
        grid_spec=pltpu.PrefetchScalarGridSpec(
            num_scalar_prefetch=1,
            grid=(t // TM_FN,),
            in_specs=[
                pl.BlockSpec(memory_space=pl.ANY),
                pl.BlockSpec((TM_FN, TOP_K), row),
                pl.BlockSpec((TM_FN, D_MODEL), row),
                pl.BlockSpec((TM_FN, D_MODEL), row),
                pl.BlockSpec((1, D_MODEL), const),
                pl.BlockSpec((1, D_MODEL), const),
            ],
            out_specs=pl.BlockSpec((TM_FN, D_MODEL), row),
            scratch_shapes=[pltpu.VMEM((2, TOP_K, TM_FN * ROW_TILE, LANES), U32), pltpu.SemaphoreType.DMA((2,))],
        ),
        out_shape=jax.ShapeDtypeStruct((t, D_MODEL), F32),
        compiler_params=_cparams(("arbitrary",)),
        name="combine_ln2",
    )(dest_flat, ys, w_tok, h1, shared, g, b)


TM_INB = 512
TN_INB = W_B // 2


def _pad_lanes(v):
    return jnp.pad(v, (0, LANES - v.shape[0]))[None, :].astype(F32)


def _layer(h2d, mem2d, batch, seq, w_in_all, layer, conv_w, conv_b, dt_bias, a_log, d_skip, ssd_norm_g, swa_sinks,
           rel_bias, w_mem_kv, w_ssd_o, w_swa_o, w_xa_o, w_out, ln1_g, ln1_b, router_w, router_bias, w1, w3, w2, ws1,
           ws3, ws2, ln2_g, ln2_b, alpha):
    t = batch * seq
    w_t = jnp.swapaxes(w_in_all, 1, 2)
    wdt = jnp.pad(w_t[layer, W_A:W_A + SSD_HEADS, :].T, ((0, 0), (0, LANES - SSD_HEADS))).astype(BF16)
    pa, dt_raw, xb = _inproj(h2d, _wcast(w_t, layer, 0, W_A, "wcast_a"), wdt)
    pb = _mm(xb, _wcast(w_t, layer, W_A + SSD_HEADS, W_B, "wcast_b"), BF16, TM_INB, TN_INB, "inproj_b",
             vmem_mb=VMEM_MB_LARGE)

    y_ssd = _ssd(
        pa, dt_raw,
        conv_w[:, :SSD_D_INNER], conv_b[None, :SSD_D_INNER], conv_w[:, SSD_D_INNER:], conv_b[None, SSD_D_INNER:],
        _pad_lanes(dt_bias), _pad_lanes(a_log),
        jnp.repeat(d_skip, SSD_HEAD_DIM)[None, :], ssd_norm_g[None, :], batch, seq)

    y_swa = _swa(pb, swa_sinks.astype(F32), _swa_bias_table(rel_bias), batch, seq)

    memkv = _mm(mem2d, w_mem_kv, BF16, batch * MEM_LEN, 1024, "mem_kv")
    y_xa = _xa(pb, memkv, batch, seq)

    merged = _merge(y_ssd, y_swa, y_xa, w_ssd_o, w_swa_o, w_xa_o, pb)
    h1, hp = _out_ln(merged, w_out.astype(BF16), h2d, ln1_g[None, :], ln1_b[None, :], alpha)

    rwt = router_w.T
    r_hi = rwt.astype(BF16)
    r_lo = (rwt - r_hi.astype(F32)).astype(BF16)
    ti = jnp.arange(TM_RT)
    su = (ti[:, None] < ti[None, :]).astype(BF16)
    idx_t, w_t, rank_t, cnt = _router(h1, r_hi, r_lo, router_bias[:, None].astype(F32), su)

    counts = cnt[:, 0].astype(I32)
    padded = (counts + MOE_BLK - 1) // MOE_BLK * MOE_BLK
    pend = jnp.cumsum(padded)
    pstart = pend - padded
    n_blocks = t * TOP_K // MOE_BLK + N_EXPERTS
    blk_lo = jnp.arange(n_blocks, dtype=I32) * MOE_BLK
    block_e = jnp.minimum(jnp.sum((pend[None, :] <= blk_lo[:, None]).astype(I32), axis=1), N_EXPERTS - 1)
    of_e = block_e[:, None] == jnp.arange(N_EXPERTS, dtype=I32)[None, :]
    valid_end = jnp.sum(jnp.where(of_e, (pstart + counts)[None, :], 0), axis=1)
    block_nv = jnp.clip(valid_end - blk_lo, 0, MOE_BLK).astype(I32)
    eids = jnp.arange(N_EXPERTS, dtype=I32)
    later_active = (eids[None, :] > eids[:, None]) & (padded > 0)[None, :]
    next_active = jnp.min(jnp.where(later_active, eids[None, :], N_EXPERTS), axis=1)
    next_active = jnp.where(next_active >= N_EXPERTS, -1, next_active)
    seg_start = jnp.sum(jnp.where(of_e, pstart[None, :], 0), axis=1)
    block_first = ((blk_lo == seg_start) & (block_nv > 0)).astype(I32)
    block_next = jnp.sum(jnp.where(of_e, next_active[None, :], 0), axis=1).astype(I32)

    dest_flat = _dest(idx_t, rank_t, pstart.astype(F32)[:, None]).reshape(-1)
    n_slots = n_blocks * MOE_BLK
    pad_info = jnp.concatenate([pstart + counts, padded - counts, pend[-1:],
                                (n_slots - pend[-1:]) // MOE_BLK]).astype(I32)
    x_sorted, shared = _dispatch(dest_flat, pad_info, hp, h1, ws1.astype(BF16), ws3.astype(BF16),
                                 ws2.astype(BF16), n_slots)
    ys = _experts(block_e, block_nv, block_first, block_next,
                  x_sorted.reshape(n_slots * ROW_TILE, LANES), w1, w3, w2)
    return _final(dest_flat, ys.reshape(n_slots, ROW_TILE, LANES), w_t.T, h1, shared, ln2_g[None, :],
                  ln2_b[None, :], alpha)


def kernel(x, mem, w_in, conv_w, conv_b, dt_bias, a_log, d_skip, ssd_norm_g, swa_sinks, rel_bias, w_mem_kv, w_ssd_o, w_swa_o, w_xa_o, w_out, ln1_g, ln1_b, router_w, router_bias, w1, w3, w2, ws1, ws3, ws2, ln2_g, ln2_b):
    batch, seq, d = x.shape
    depth = w_in.shape[0]
    alpha = (2.0 * depth) ** 0.25
    h = x.reshape(batch * seq, d)
    mem2d = mem.reshape(batch * MEM_LEN, d)
    for i in range(depth):
        h = _layer(h, mem2d, batch, seq, w_in, i, conv_w[i], conv_b[i], dt_bias[i], a_log[i], d_skip[i],
                   ssd_norm_g[i], swa_sinks[i], rel_bias, w_mem_kv[i], w_ssd_o[i], w_swa_o[i], w_xa_o[i], w_out[i],
                   ln1_g[i], ln1_b[i], router_w[i], router_bias[i], w1[i], w3[i], w2[i], ws1[i], ws3[i], ws2[i],
                   ln2_g[i], ln2_b[i], alpha)
    return h.reshape(batch, seq, d)
```
